```python
import math
import jax, jax.numpy as jnp
from jax import lax
import numpy as np

D_MODEL = 1024
BATCH = 4
SEQ = 4096
DEPTH = 1

CHUNK = 64
HEAD_DIM = 64
A_HEADS = 8
A_WIDTH = A_HEADS * HEAD_DIM
LEFT_CHUNKS = 8
BAND = (LEFT_CHUNKS + 1) * CHUNK
MAX_REL = 128
B_HEADS = 4
B_QK_DIM = HEAD_DIM
B_V_DIM = 2 * HEAD_DIM
B_WIDTH = B_HEADS * B_V_DIM
MIX_WIDTH = A_WIDTH + B_WIDTH
IN_WIDTH = 3 * A_WIDTH + 2 * (B_HEADS * 2 * B_QK_DIM) + B_WIDTH
ROPE_THETA = 500000.0
ROPE_DIM = B_QK_DIM // 4
Q_BLOCK = 128
N_EXPERTS = 32
TOP_K = 4
D_EXPERT = D_MODEL
SWIGLU_LIMIT = 7.0
SWIGLU_ALPHA = 1.702
EPS = 1e-6
NEG_INF = -1e30
N_MOD = 6

kernel_name = "hybrid_chunked_diffattn_moe_block"


def rms_norm(x, g):
    xf = x.astype(jnp.float32)
    y = xf * lax.rsqrt(jnp.mean(xf * xf, axis=-1, keepdims=True) + EPS)
    return (y * g.astype(jnp.float32)).astype(x.dtype)


def modulate(h, shift, scale):
    return h * (1.0 + scale[:, None, :]) + shift[:, None, :]


def lambda_init_fn(layer_idx):
    return 0.8 - 0.6 * math.exp(-0.3 * layer_idx)


def partial_rope(x, cos, sin):
    half = ROPE_DIM // 2
    x1 = x[..., :half].astype(jnp.float32)
    x2 = x[..., half:ROPE_DIM].astype(jnp.float32)
    r1 = x1 * cos - x2 * sin
    r2 = x2 * cos + x1 * sin
    return jnp.concatenate([r1.astype(x.dtype), r2.astype(x.dtype), x[..., ROPE_DIM:]], axis=-1)


def chunked_relbias_attention(q, k, v, rel_table):
    b, s, h, d = q.shape
    nc = s // CHUNK
    qc = q.reshape(b, nc, CHUNK, h, d)
    pad = ((0, 0), (LEFT_CHUNKS * CHUNK, 0), (0, 0), (0, 0))
    kp = jnp.pad(k, pad).reshape(b, nc + LEFT_CHUNKS, CHUNK, h, d)
    vp = jnp.pad(v, pad).reshape(b, nc + LEFT_CHUNKS, CHUNK, h, d)
    kb = jnp.concatenate([kp[:, j:j + nc] for j in range(LEFT_CHUNKS + 1)], axis=2)
    vb = jnp.concatenate([vp[:, j:j + nc] for j in range(LEFT_CHUNKS + 1)], axis=2)
    rel = LEFT_CHUNKS * CHUNK + np.arange(CHUNK)[:, None] - np.arange(BAND)[None, :]
    rel_idx = np.clip(rel, -MAX_REL, MAX_REL) + MAX_REL
    bias = rel_table.astype(jnp.float32)[:, rel_idx]
    key_abs = (jnp.arange(nc)[:, None] - LEFT_CHUNKS) * CHUNK + jnp.arange(BAND)[None, :]
    valid = key_abs >= 0
    scores = jnp.einsum('bcihd,bcjhd->bhcij', qc, kb).astype(jnp.float32) * (d ** -0.5)
    scores = scores + bias[None, :, None, :, :]
    scores = jnp.where(valid[None, None, :, None, :], scores, NEG_INF)
    p = jax.nn.softmax(scores, axis=-1).astype(v.dtype)
    out = jnp.einsum('bhcij,bcjhd->bcihd', p, vb)
    return out.reshape(b, s, h * d)


def differential_attention(q, k, v, lam, g_subln, lambda_init):
    b, s, h = q.shape[:3]
    nb = s // Q_BLOCK
    scale = B_QK_DIM ** -0.5
    key_chunk = jnp.arange(s) // CHUNK
    qb = q.reshape(b, nb, Q_BLOCK, h, 2, B_QK_DIM).transpose(1, 0, 2, 3, 4, 5)

    def block(args):
        q_blk, blk = args
        sc = jnp.einsum('bqhcd,bkhcd->bhcqk', q_blk, k).astype(jnp.float32) * scale
        q_chunk = (blk * Q_BLOCK + jnp.arange(Q_BLOCK)) // CHUNK
        mask = key_chunk[None, :] <= q_chunk[:, None]
        sc = jnp.where(mask[None, None, None], sc, NEG_INF)
        p = jax.nn.softmax(sc, axis=-1)
        w = (p[:, :, 0] - lam * p[:, :, 1]).astype(v.dtype)
        o = jnp.einsum('bhqk,bkhd->bqhd', w, v)
        return rms_norm(o, g_subln) * (1.0 - lambda_init)

    out = lax.map(block, (qb, jnp.arange(nb)))
    return out.transpose(1, 0, 2, 3, 4).reshape(b, s, h * B_V_DIM)


def moe_ffn(h, w_router, b_router, w_gate_up, b_gate_up, w_down, b_down):
    b, s, d = h.shape
    hf = h.reshape(-1, d)
    n_tok = hf.shape[0]
    logits = (hf @ w_router).astype(jnp.float32) + b_router.astype(jnp.float32)
    top_v, top_i = lax.top_k(logits, TOP_K)
    top_w = jax.nn.softmax(top_v, axis=-1)
    flat_e = top_i.reshape(-1)
    flat_w = top_w.reshape(-1)
    order = jnp.argsort(flat_e, stable=True)
    sorted_e = flat_e[order]
    tok = order // TOP_K
    xs = hf[tok]
    group_sizes = jnp.bincount(flat_e, length=N_EXPERTS).astype(jnp.int32)
    gu = lax.ragged_dot(xs, w_gate_up, group_sizes) + b_gate_up[sorted_e]
    gate = jnp.minimum(gu[:, :D_EXPERT], SWIGLU_LIMIT)
    up = jnp.clip(gu[:, D_EXPERT:], -SWIGLU_LIMIT, SWIGLU_LIMIT)
    act = (up + 1.0) * (gate * jax.nn.sigmoid(SWIGLU_ALPHA * gate))
    y = lax.ragged_dot(act, w_down, group_sizes) + b_down[sorted_e]
    y = y * flat_w[order][:, None].astype(y.dtype)
    return jax.ops.segment_sum(y, tok, num_segments=n_tok).reshape(b, s, d)


def setup_inputs(seed: int = 0) -> dict:
    key = jax.random.key(seed)
    ks = jax.random.split(key, 24)
    f32 = jnp.float32
    nrm = lambda k, shp, sc: jax.random.normal(k, shp, f32) * sc
    x = nrm(ks[0], (BATCH, SEQ, D_MODEL), 1.0)
    c = nrm(ks[1], (BATCH, D_MODEL), 1.0)
    offset = jax.random.randint(ks[2], (BATCH, 1), 0, SEQ, dtype=jnp.int32)
    positions = (jnp.arange(SEQ, dtype=jnp.int32)[None, :] + offset).astype(jnp.int32)
    return {
        "x": x,
        "c": c,
        "positions": positions,
        "w_ada": nrm(ks[3], (DEPTH, D_MODEL, N_MOD * D_MODEL), 0.5 * D_MODEL ** -0.5),
        "b_ada": nrm(ks[4], (DEPTH, N_MOD * D_MODEL), 0.02),
        "g_mix_norm": 1.0 + nrm(ks[5], (DEPTH, D_MODEL), 0.02),
        "w_in": nrm(ks[6], (DEPTH, D_MODEL, IN_WIDTH), D_MODEL ** -0.5),
        "rel_bias": nrm(ks[7], (DEPTH, A_HEADS, 2 * MAX_REL + 1), 0.5),
        "lambda_q1": nrm(ks[8], (DEPTH, B_QK_DIM), 0.1),
        "lambda_k1": nrm(ks[9], (DEPTH, B_QK_DIM), 0.1),
        "lambda_q2": nrm(ks[10], (DEPTH, B_QK_DIM), 0.1),
        "lambda_k2": nrm(ks[11], (DEPTH, B_QK_DIM), 0.1),
        "g_subln": 1.0 + nrm(ks[12], (DEPTH, B_V_DIM), 0.02),
        "w_out": nrm(ks[13], (DEPTH, MIX_WIDTH, D_MODEL), MIX_WIDTH ** -0.5),
        "g_ffn_norm": 1.0 + nrm(ks[14], (DEPTH, D_MODEL), 0.02),
        "w_router": nrm(ks[15], (DEPTH, D_MODEL, N_EXPERTS), D_MODEL ** -0.5),
        "b_router": nrm(ks[16], (DEPTH, N_EXPERTS), 0.01),
        "w_gate_up": nrm(ks[17], (DEPTH, N_EXPERTS, D_MODEL, 2 * D_EXPERT), D_MODEL ** -0.5),
        "b_gate_up": nrm(ks[18], (DEPTH, N_EXPERTS, 2 * D_EXPERT), 0.02),
        "w_down": nrm(ks[19], (DEPTH, N_EXPERTS, D_EXPERT, D_MODEL), D_EXPERT ** -0.5),
        "b_down": nrm(ks[20], (DEPTH, N_EXPERTS, D_MODEL), 0.02),
        "g_final": 1.0 + nrm(ks[21], (D_MODEL,), 0.02),
    }


def reference(x, c, positions, w_ada, b_ada, g_mix_norm, w_in, rel_bias, lambda_q1, lambda_k1,
              lambda_q2, lambda_k2, g_subln, w_out, g_ffn_norm, w_router, b_router, w_gate_up,
              b_gate_up, w_down, b_down, g_final):
    b, s, _ = x.shape
    inv_freq = ROPE_THETA ** (-jnp.arange(0, ROPE_DIM, 2, dtype=jnp.float32) / ROPE_DIM)
    ang = positions.astype(jnp.float32)[..., None] * inv_freq
    cos = jnp.cos(ang)[:, :, None, None, :]
    sin = jnp.sin(ang)[:, :, None, None, :]
    c_act = jax.nn.silu(c)
    qk_b = B_HEADS * 2 * B_QK_DIM
    for l in range(DEPTH):
        mod = c_act @ w_ada[l] + b_ada[l]
        sh_m, sc_m, gt_m, sh_f, sc_f, gt_f = jnp.split(mod, N_MOD, axis=-1)
        lambda_init = lambda_init_fn(l)

        h = modulate(rms_norm(x, g_mix_norm[l]), sh_m, sc_m)
        proj = h @ w_in[l]
        q_a, k_a, v_a, q_b, k_b, v_b = jnp.split(
            proj, np.cumsum([A_WIDTH, A_WIDTH, A_WIDTH, qk_b, qk_b]).tolist(), axis=-1)
        shp_a = (b, s, A_HEADS, HEAD_DIM)
        o_a = chunked_relbias_attention(q_a.reshape(shp_a), k_a.reshape(shp_a),
                                        v_a.reshape(shp_a), rel_bias[l])
        shp_b = (b, s, B_HEADS, 2, B_QK_DIM)
        q_b = partial_rope(q_b.reshape(shp_b), cos, sin)
        k_b = partial_rope(k_b.reshape(shp_b), cos, sin)
        lam = (jnp.exp(jnp.sum(lambda_q1[l].astype(jnp.float32) * lambda_k1[l].astype(jnp.float32)))
               - jnp.exp(jnp.sum(lambda_q2[l].astype(jnp.float32) * lambda_k2[l].astype(jnp.float32)))
               + lambda_init)
        o_b = differential_attention(q_b, k_b, v_b.reshape(b, s, B_HEADS, B_V_DIM), lam,
                                     g_subln[l], lambda_init)
        mix = jnp.concatenate([o_a, o_b], axis=-1) @ w_out[l]
        x = x + gt_m[:, None, :] * mix

        h = modulate(rms_norm(x, g_ffn_norm[l]), sh_f, sc_f)
        ffn = moe_ffn(h, w_router[l], b_router[l], w_gate_up[l], b_gate_up[l], w_down[l], b_down[l])
        x = x + gt_f[:, None, :] * ffn
    return rms_norm(x, g_final)
```

```python
import functools
import math

import jax
import jax.numpy as jnp
import numpy as np
from jax import lax
from jax.experimental import pallas as pl
from jax.experimental.pallas import tpu as pltpu

D_MODEL = 1024
CHUNK = 64
HEAD_DIM = 64
A_HEADS = 8
A_WIDTH = A_HEADS * HEAD_DIM
LEFT_CHUNKS = 8
MAX_REL = 128
B_HEADS = 4
B_QK_DIM = HEAD_DIM
B_V_DIM = 2 * HEAD_DIM
B_WIDTH = B_HEADS * B_V_DIM
ROPE_THETA = 500000.0
ROPE_DIM = B_QK_DIM // 4
N_EXPERTS = 32
TOP_K = 4
D_EXPERT = D_MODEL
SWIGLU_LIMIT = 7.0
SWIGLU_ALPHA = 1.702
EPS = 1e-6
NEG_INF = -1e30
N_MOD = 6

LANES = 128
SUBLANES = 8
VMEM_BYTES_V7X = 64 * 1024 * 1024
VMEM_LIMIT = VMEM_BYTES_V7X * 7 // 8

TOK_TILE = 512
A_QBLK = 2 * CHUNK
A_BAND = (LEFT_CHUNKS + 2) * CHUNK
B_TQ = 512
B_TK = 512
ROW_TILE = 512
FF_CHUNK = 512
COMB_TILE = 256

F32 = jnp.float32
BF16 = jnp.bfloat16


def _cparams(n_axes, vmem=None):
    return pltpu.CompilerParams(
        dimension_semantics=("arbitrary",) * n_axes,
        vmem_limit_bytes=vmem,
    )


def _ada_kernel(c_ref, w_ref, b_ref, o_ref):
    c = c_ref[...]
    act = c * jax.nn.sigmoid(c)
    o_ref[...] = jnp.dot(act, w_ref[...], preferred_element_type=F32,
                         precision=lax.Precision.HIGHEST) + b_ref[...]


def _ada(c, w_ada, b_ada):
    b = c.shape[0]
    rows = -(-b // SUBLANES) * SUBLANES
    c_pad = jnp.pad(c, ((0, rows - b), (0, 0)))
    n_out = w_ada.shape[1]
    out = pl.pallas_call(
        _ada_kernel,
        grid=(n_out // D_MODEL,),
        in_specs=[
            pl.BlockSpec((rows, D_MODEL), lambda j: (0, 0)),
            pl.BlockSpec((D_MODEL, D_MODEL), lambda j: (0, j)),
            pl.BlockSpec((1, D_MODEL), lambda j: (0, j)),
        ],
        out_specs=pl.BlockSpec((rows, D_MODEL), lambda j: (0, j)),
        out_shape=jax.ShapeDtypeStruct((rows, n_out), F32),
        compiler_params=_cparams(1),
        name="ada",
    )(c_pad, w_ada, b_ada.reshape(1, n_out))
    mod = out[:b].reshape(b, N_MOD, D_MODEL)
    return jnp.pad(mod, ((0, 0), (0, SUBLANES - N_MOD), (0, 0)))


def _in_proj_kernel(x_ref, mod_ref, g_ref, w_ref, rc_ref, rm_ref, rp_ref,
                    qa_ref, ka_ref, va_ref, qb_ref, kb_ref, vb_ref):
    x = x_ref[...]
    mod = mod_ref[...]
    y = x * lax.rsqrt(jnp.mean(x * x, axis=-1, keepdims=True) + EPS) * g_ref[...]
    h = (y * (1.0 + mod[1:2, :]) + mod[0:1, :]).astype(BF16)
    rc, rm, rp = rc_ref[...], rm_ref[...], rp_ref[...]
    q_scale = HEAD_DIM ** -0.5

    def rope(p):
        cols = []
        for s in range(p.shape[1] // LANES):
            v = p[:, s * LANES:(s + 1) * LANES]
            cols.append(v * rc + pltpu.roll(v, LANES - ROPE_DIM // 2, 1) * rm
                        + pltpu.roll(v, ROPE_DIM // 2, 1) * rp)
        return jnp.concatenate(cols, axis=1)

    outs = (qa_ref, ka_ref, va_ref, qb_ref, kb_ref, vb_ref)
    for j, o_ref in enumerate(outs):
        p = jnp.dot(h, w_ref[:, j * A_WIDTH:(j + 1) * A_WIDTH], preferred_element_type=F32)
        if j in (3, 4):
            p = rope(p)
        if j in (0, 3):
            p = p * q_scale
        o_ref[...] = p.astype(BF16)


def _in_proj(x2, mod, g_mix, w_in_bf, rope_c, rope_m, rope_p, seq):
    n = x2.shape[0]
    tiles_per_seq = seq // TOK_TILE
    row = lambda i: (i, 0)
    fixed = lambda i: (0, 0)
    out_sd = jax.ShapeDtypeStruct((n, A_WIDTH), BF16)
    return pl.pallas_call(
        _in_proj_kernel,
        grid=(n // TOK_TILE,),
        in_specs=[
            pl.BlockSpec((TOK_TILE, D_MODEL), row),
            pl.BlockSpec((None, SUBLANES, D_MODEL), lambda i: (i // tiles_per_seq, 0, 0)),
            pl.BlockSpec((1, D_MODEL), fixed),
            pl.BlockSpec(w_in_bf.shape, fixed),
            pl.BlockSpec((TOK_TILE, LANES), row),
            pl.BlockSpec((TOK_TILE, LANES), row),
            pl.BlockSpec((TOK_TILE, LANES), row),
        ],
        out_specs=[pl.BlockSpec((TOK_TILE, A_WIDTH), row)] * 6,
        out_shape=[out_sd] * 6,
        compiler_params=_cparams(1, VMEM_LIMIT),
        name="in_proj",
    )(x2, mod, g_mix.reshape(1, D_MODEL), w_in_bf, rope_c, rope_m, rope_p)


def _attn_a_kernel(q_ref, kp_ref, kc_ref, vp_ref, vc_ref, bias_ref, o_ref, k_sc, v_sc):
    g = pl.program_id(1)
    blk = q_ref.shape[0]
    k_sc[0:blk, :] = kp_ref[...]
    k_sc[blk:2 * blk, :] = kc_ref[...]
    v_sc[0:blk, :] = vp_ref[...]
    v_sc[blk:2 * blk, :] = vc_ref[...]
    lane = lax.broadcasted_iota(jnp.int32, (A_QBLK, LANES), 1)
    col = lax.broadcasted_iota(jnp.int32, (A_QBLK, A_BAND), 1)
    for m in range(blk // A_QBLK):
        r0 = m * A_QBLK
        q = q_ref[r0:r0 + A_QBLK, :]
        kband = k_sc[r0:r0 + A_BAND, :]
        vband = v_sc[r0:r0 + A_BAND, :]
        valid = jnp.logical_or(g > 0, col + r0 >= blk)
        halves = []
        for hh in range(2):
            in_head = (lane < HEAD_DIM) if hh == 0 else (lane >= HEAD_DIM)
            qh = jnp.where(in_head, q, jnp.zeros_like(q))
            s = lax.dot_general(qh, kband, (((1,), (1,)), ((), ())), preferred_element_type=F32)
            s = s + bias_ref[hh]
            s = jnp.where(valid, s, NEG_INF)
            s_max = jnp.max(s, axis=-1, keepdims=True)
            p = jnp.exp(s - s_max)
            denom = jnp.sum(p, axis=-1, keepdims=True)
            o = jnp.dot(p.astype(BF16), vband, preferred_element_type=F32)
            halves.append(o / denom)
        o_ref[r0:r0 + A_QBLK, :] = jnp.where(lane < HEAD_DIM, halves[0], halves[1]).astype(BF16)


def _attn_a(qa, ka, va, bias2, batch, seq):
    n = qa.shape[0]
    blk = LEFT_CHUNKS * CHUNK
    nblk = seq // blk
    n_pairs = A_WIDTH // LANES
    cur = lambda b, g, p: (b * nblk + g, p)
    prev = lambda b, g, p: (b * nblk + jnp.maximum(g - 1, 0), p)
    return pl.pallas_call(
        _attn_a_kernel,
        grid=(batch, nblk, n_pairs),
        in_specs=[
            pl.BlockSpec((blk, LANES), cur),
            pl.BlockSpec((blk, LANES), prev),
            pl.BlockSpec((blk, LANES), cur),
            pl.BlockSpec((blk, LANES), prev),
            pl.BlockSpec((blk, LANES), cur),
            pl.BlockSpec((2, A_QBLK, A_BAND), lambda b, g, p: (p, 0, 0)),
        ],
        out_specs=pl.BlockSpec((blk, LANES), cur),
        out_shape=jax.ShapeDtypeStruct((n, A_WIDTH), BF16),
        scratch_shapes=[pltpu.VMEM((2 * blk, LANES), BF16), pltpu.VMEM((2 * blk, LANES), BF16)],
        compiler_params=_cparams(3),
        name="attn_a",
    )(qa, ka, ka, va, va, bias2)


def _rel_bias_blocks(rel_table):
    i = np.arange(A_QBLK)[:, None]
    j = np.arange(A_BAND)[None, :]
    rel = LEFT_CHUNKS * CHUNK + i - j
    idx = np.clip(rel, -MAX_REL, MAX_REL) + MAX_REL
    qc = i // CHUNK
    kc = j // CHUNK
    in_band = (kc >= qc) & (kc <= qc + LEFT_CHUNKS)
    bias = rel_table.astype(F32)[:, idx]
    return jnp.where(jnp.asarray(in_band)[None], bias, NEG_INF)


def _attn_b_kernel(lam_ref, q_ref, k_ref, v_ref, g_ref, o_ref, m_sc, l_sc, acc_sc, *, out_scale):
    qi = pl.program_id(2)
    ki = pl.program_id(3)

    @pl.when(ki == 0)
    def _():
        m_sc[...] = jnp.full(m_sc.shape, NEG_INF, F32)
        l_sc[...] = jnp.zeros(l_sc.shape, F32)
        acc_sc[...] = jnp.zeros(acc_sc.shape, F32)

    def step(diagonal):
        q = q_ref[...]
        k = k_ref[...]
        v = v_ref[...]
        lane = lax.broadcasted_iota(jnp.int32, q.shape, 1)
        if diagonal:
            row_c = lax.broadcasted_iota(jnp.int32, (B_TQ, B_TK), 0) // CHUNK
            col_c = lax.broadcasted_iota(jnp.int32, (B_TQ, B_TK), 1) // CHUNK
            keep = col_c <= row_c
        for c in range(2):
            in_comp = (lane < B_QK_DIM) if c == 0 else (lane >= B_QK_DIM)
            qc = jnp.where(in_comp, q, jnp.zeros_like(q))
            s = lax.dot_general(qc, k, (((1,), (1,)), ((), ())), preferred_element_type=F32)
            if diagonal:
                s = jnp.where(keep, s, NEG_INF)
            m_prev = m_sc[c]
            m_new = jnp.maximum(m_prev, jnp.max(s, axis=-1, keepdims=True))
            alpha = jnp.exp(m_prev - m_new)
            p = jnp.exp(s - m_new)
            l_sc[c] = alpha * l_sc[c] + jnp.sum(p, axis=-1, keepdims=True)
            acc_sc[c] = alpha * acc_sc[c] + jnp.dot(p.astype(BF16), v, preferred_element_type=F32)
            m_sc[c] = m_new

    @pl.when(ki < qi)
    def _():
        step(False)

    @pl.when(ki == qi)
    def _():
        step(True)
        lam = lam_ref[0]
        o = acc_sc[0] / l_sc[0] - lam * (acc_sc[1] / l_sc[1])
        y = o * lax.rsqrt(jnp.mean(o * o, axis=-1, keepdims=True) + EPS) * g_ref[...]
        o_ref[...] = (y * out_scale).astype(BF16)


def _attn_b(lam, qb, kb, vb, g_subln, batch, seq, out_scale):
    assert B_TQ == B_TK
    n = qb.shape[0]
    nq = seq // B_TQ
    q_map = lambda b, h, qi, ki: (b * nq + qi, h)
    kv_map = lambda b, h, qi, ki: (b * nq + jnp.minimum(ki, qi), h)
    return pl.pallas_call(
        functools.partial(_attn_b_kernel, out_scale=out_scale),
        grid=(batch, B_HEADS, nq, nq),
        in_specs=[
            pl.BlockSpec(memory_space=pltpu.SMEM),
            pl.BlockSpec((B_TQ, LANES), q_map),
            pl.BlockSpec((B_TK, LANES), kv_map),
            pl.BlockSpec((B_TK, LANES), kv_map),
            pl.BlockSpec((1, B_V_DIM), lambda b, h, qi, ki: (0, 0)),
        ],
        out_specs=pl.BlockSpec((B_TQ, LANES), q_map),
        out_shape=jax.ShapeDtypeStruct((n, B_WIDTH), BF16),
        scratch_shapes=[
            pltpu.VMEM((2, B_TQ, 1), F32),
            pltpu.VMEM((2, B_TQ, 1), F32),
            pltpu.VMEM((2, B_TQ, B_V_DIM), F32),
        ],
        compiler_params=_cparams(4),
        name="attn_b",
    )(lam, qb, kb, vb, g_subln.reshape(1, B_V_DIM))


def _out_route_kernel(oa_ref, ob_ref, x_ref, mod_ref, wo_ref, g_ref, wr_ref, br_ref,
                      x1_ref, hp_ref, ti_ref, tw_ref, rk_ref, cnt_ref, tri_sc, carry_sc):
    i = pl.program_id(0)
    tm = x_ref.shape[0]

    @pl.when(i == 0)
    def _():
        r = lax.broadcasted_iota(jnp.int32, (tm, tm), 0)
        c = lax.broadcasted_iota(jnp.int32, (tm, tm), 1)
        tri_sc[...] = jnp.where(c < r, 1.0, 0.0).astype(BF16)
        carry_sc[...] = jnp.zeros(carry_sc.shape, F32)

    mod = mod_ref[...]
    o = jnp.concatenate([oa_ref[...], ob_ref[...]], axis=1)
    mix = jnp.dot(o, wo_ref[...], preferred_element_type=F32)
    x1 = x_ref[...] + mod[2:3, :] * mix
    x1_ref[...] = x1
    y = x1 * lax.rsqrt(jnp.mean(x1 * x1, axis=-1, keepdims=True) + EPS) * g_ref[...]
    h = y * (1.0 + mod[4:5, :]) + mod[3:4, :]
    hb = h.astype(BF16)

    half = D_MODEL // 2
    hi = lax.bitcast_convert_type(hb[:, :half].astype(F32), jnp.int32)
    lo = lax.bitcast_convert_type(hb[:, half:].astype(F32), jnp.int32)
    hp_ref[...] = hi | lax.shift_right_logical(lo, jnp.full(lo.shape, 16, jnp.int32))

    logits = jnp.dot(hb, wr_ref[...], preferred_element_type=F32) + br_ref[...]
    eid = lax.broadcasted_iota(jnp.int32, logits.shape, 1).astype(F32)
    work = logits
    vals, ids = [], []
    chosen = jnp.zeros(logits.shape, F32)
    for _ in range(TOP_K):
        v = jnp.max(work, axis=-1, keepdims=True)
        e = jnp.min(jnp.where(work == v, eid, float(N_EXPERTS)), axis=-1, keepdims=True)
        hit = eid == e
        vals.append(v)
        ids.append(e)
        chosen = jnp.where(hit, 1.0, chosen)
        work = jnp.where(hit, -jnp.inf, work)
    ex = [jnp.exp(v - vals[0]) for v in vals]
    den = ex[0] + ex[1] + ex[2] + ex[3]

    before = jnp.dot(tri_sc[...], chosen.astype(BF16), preferred_element_type=F32) + carry_sc[...]
    slot = lax.broadcasted_iota(jnp.int32, (tm, TOP_K), 1)
    ti = jnp.zeros((tm, TOP_K), F32)
    tw = jnp.zeros((tm, TOP_K), F32)
    rk = jnp.zeros((tm, TOP_K), F32)
    for kk in range(TOP_K):
        r_k = jnp.sum(jnp.where(eid == ids[kk], before, 0.0), axis=-1, keepdims=True)
        ti = jnp.where(slot == kk, ids[kk], ti)
        tw = jnp.where(slot == kk, ex[kk] / den, tw)
        rk = jnp.where(slot == kk, r_k, rk)
    ti_ref[...] = ti.astype(jnp.int32)
    tw_ref[...] = tw
    rk_ref[...] = rk.astype(jnp.int32)
    carry = carry_sc[...] + jnp.sum(chosen, axis=0, keepdims=True)
    carry_sc[...] = carry
    cnt_ref[...] = carry.astype(jnp.int32)


def _out_route(oa, ob, x2, mod, w_out_bf, g_ffn, w_router_bf, b_router, seq):
    n = x2.shape[0]
    tiles_per_seq = seq // TOK_TILE
    row = lambda i: (i, 0)
    fixed = lambda i: (0, 0)
    return pl.pallas_call(
        _out_route_kernel,
        grid=(n // TOK_TILE,),
        in_specs=[
            pl.BlockSpec((TOK_TILE, A_WIDTH), row),
            pl.BlockSpec((TOK_TILE, B_WIDTH), row),
            pl.BlockSpec((TOK_TILE, D_MODEL), row),
            pl.BlockSpec((None, SUBLANES, D_MODEL), lambda i: (i // tiles_per_seq, 0, 0)),
            pl.BlockSpec((D_MODEL, D_MODEL), fixed),
            pl.BlockSpec((1, D_MODEL), fixed),
            pl.BlockSpec((D_MODEL, N_EXPERTS), fixed),
            pl.BlockSpec((1, N_EXPERTS), fixed),
        ],
        out_specs=[
            pl.BlockSpec((TOK_TILE, D_MODEL), row),
            pl.BlockSpec((TOK_TILE, D_MODEL // 2), row),
            pl.BlockSpec((TOK_TILE, TOP_K), row),
            pl.BlockSpec((TOK_TILE, TOP_K), row),
            pl.BlockSpec((TOK_TILE, TOP_K), row),
            pl.BlockSpec((1, N_EXPERTS), fixed),
        ],
        out_shape=[
            jax.ShapeDtypeStruct((n, D_MODEL), F32),
            jax.ShapeDtypeStruct((n, D_MODEL // 2), jnp.int32),
            jax.ShapeDtypeStruct((n, TOP_K), jnp.int32),
            jax.ShapeDtypeStruct((n, TOP_K), F32),
            jax.ShapeDtypeStruct((n, TOP_K), jnp.int32),
            jax.ShapeDtypeStruct((1, N_EXPERTS), jnp.int32),
        ],
        scratch_shapes=[pltpu.VMEM((TOK_TILE, TOK_TILE), BF16), pltpu.VMEM((1, N_EXPERTS), F32)],
        compiler_params=_cparams(1, VMEM_LIMIT),
        name="out_route",
    )(oa, ob, x2, mod, w_out_bf, g_ffn.reshape(1, D_MODEL), w_router_bf,
      b_router.reshape(1, N_EXPERTS))


def _dispatch_kernel(pos_ref, hp_ref, xs_in_ref, xs_ref, sem):
    del xs_in_ref
    i = pl.program_id(0)
    tm = hp_ref.shape[0]
    base = i * tm * TOP_K

    def issue(t, carry):
        for kk in range(TOP_K):
            r = pos_ref[base + t * TOP_K + kk]
            pltpu.make_async_copy(hp_ref.at[pl.ds(t, 1), :], xs_ref.at[pl.ds(r, 1), :], sem).start()
        return carry

    lax.fori_loop(0, tm, issue, 0)
    for _ in range(TOP_K):
        pltpu.make_async_copy(hp_ref, xs_ref.at[pl.ds(0, tm), :], sem).wait()


def _dispatch(pos, hp, n_rows):
    n, width = hp.shape
    xs_init = jnp.zeros((n_rows, width), jnp.int32)
    return pl.pallas_call(
        _dispatch_kernel,
        grid_spec=pltpu.PrefetchScalarGridSpec(
            num_scalar_prefetch=1,
            grid=(n // TOK_TILE,),
            in_specs=[
                pl.BlockSpec((TOK_TILE, width), lambda i, pos: (i, 0)),
                pl.BlockSpec(memory_space=pl.ANY),
            ],
            out_specs=pl.BlockSpec(memory_space=pl.ANY),
            scratch_shapes=[pltpu.SemaphoreType.DMA(())],
        ),
        out_shape=jax.ShapeDtypeStruct((n_rows, width), jnp.int32),
        input_output_aliases={2: 0},
        compiler_params=_cparams(1),
        name="dispatch",
    )(pos, hp, xs_init)


def _experts_kernel(te_ref, first_ref, live_ref, blk_ref, xs_ref, wgu_ref, bgu_ref, wd_ref,
                    bd_ref, y_ref, wgu_sc, wd_sc):
    del te_ref, blk_ref
    j = pl.program_id(0)

    @pl.when(live_ref[j] == 1)
    def _():
        @pl.when(first_ref[j] == 1)
        def _():
            wgu_sc[...] = wgu_ref[...].astype(BF16)
            wd_sc[...] = wd_ref[...].astype(BF16)

        xw = xs_ref[...]
        hi = lax.bitcast_convert_type(xw & jnp.int32(-65536), F32).astype(BF16)
        lo = lax.bitcast_convert_type(lax.shift_left(xw, jnp.full(xw.shape, 16, jnp.int32)),
                                      F32).astype(BF16)
        x = jnp.concatenate([hi, lo], axis=1)
        acc = jnp.zeros((x.shape[0], D_MODEL), F32)
        for c in range(D_EXPERT // FF_CHUNK):
            lo_c, hi_c = c * FF_CHUNK, (c + 1) * FF_CHUNK
            gate = jnp.dot(x, wgu_sc[:, lo_c:hi_c], preferred_element_type=F32) + bgu_ref[:, lo_c:hi_c]
            up = (jnp.dot(x, wgu_sc[:, D_EXPERT + lo_c:D_EXPERT + hi_c], preferred_element_type=F32)
                  + bgu_ref[:, D_EXPERT + lo_c:D_EXPERT + hi_c])
            gate = jnp.minimum(gate, SWIGLU_LIMIT)
            up = jnp.clip(up, -SWIGLU_LIMIT, SWIGLU_LIMIT)
            act = (up + 1.0) * (gate * jax.nn.sigmoid(SWIGLU_ALPHA * gate))
            acc = acc + jnp.dot(act.astype(BF16), wd_sc[lo_c:hi_c, :], preferred_element_type=F32)
        y_ref[...] = acc + bd_ref[...]

    @pl.when(live_ref[j] == 0)
    def _():
        y_ref[...] = jnp.zeros(y_ref.shape, F32)


def _experts(tile_e, tile_first, tile_live, tile_blk, xs, w_gate_up, b_gate_up, w_down, b_down):
    n_rows, width = xs.shape
    n_tiles = n_rows // ROW_TILE
    by_tile = lambda j, te, fi, li, tb: (tb[j], 0)
    by_expert = lambda j, te, fi, li, tb: (te[j], 0, 0)
    return pl.pallas_call(
        _experts_kernel,
        grid_spec=pltpu.PrefetchScalarGridSpec(
            num_scalar_prefetch=4,
            grid=(n_tiles,),
            in_specs=[
                pl.BlockSpec((ROW_TILE, width), by_tile),
                pl.BlockSpec((None, D_MODEL, 2 * D_EXPERT), by_expert),
                pl.BlockSpec((None, 1, 2 * D_EXPERT), by_expert),
                pl.BlockSpec((None, D_EXPERT, D_MODEL), by_expert),
                pl.BlockSpec((None, 1, D_MODEL), by_expert),
            ],
            out_specs=pl.BlockSpec((ROW_TILE, D_MODEL), lambda j, te, fi, li, tb: (j, 0)),
            scratch_shapes=[
                pltpu.VMEM((D_MODEL, 2 * D_EXPERT), BF16),
                pltpu.VMEM((D_EXPERT, D_MODEL), BF16),
            ],
        ),
        out_shape=jax.ShapeDtypeStruct((n_rows, D_MODEL), F32),
        compiler_params=_cparams(1, VMEM_LIMIT),
        name="experts",
    )(tile_e, tile_first, tile_live, tile_blk, xs, w_gate_up,
      b_gate_up.reshape(N_EXPERTS, 1, 2 * D_EXPERT), w_down, b_down.reshape(N_EXPERTS, 1, D_MODEL))


def _combine_kernel(pos_ref, tw_ref, x1_ref, mod_ref, g_ref, y_ref, o_ref, ybuf, sem):
    i = pl.program_id(0)
    tm = x1_ref.shape[0]
    base = i * tm * TOP_K

    def issue(t, carry):
        for kk in range(TOP_K):
            r = pos_ref[base + t * TOP_K + kk]
            pltpu.make_async_copy(y_ref.at[pl.ds(r, 1), :], ybuf.at[kk, pl.ds(t, 1), :], sem).start()
        return carry

    lax.fori_loop(0, tm, issue, 0)
    for kk in range(TOP_K):
        pltpu.make_async_copy(y_ref.at[pl.ds(0, tm), :], ybuf.at[kk], sem).wait()

    tw = tw_ref[...]
    ffn = tw[:, 0:1] * ybuf[0]
    for kk in range(1, TOP_K):
        ffn = ffn + tw[:, kk:kk + 1] * ybuf[kk]
    x2 = x1_ref[...] + mod_ref[5:6, :] * ffn
    o_ref[...] = x2 * lax.rsqrt(jnp.mean(x2 * x2, axis=-1, keepdims=True) + EPS) * g_ref[...]


def _combine(pos, tw, x1, mod, g_final, y, seq):
    n = x1.shape[0]
    tiles_per_seq = seq // COMB_TILE
    row = lambda i, pos: (i, 0)
    return pl.pallas_call(
        _combine_kernel,
        grid_spec=pltpu.PrefetchScalarGridSpec(
            num_scalar_prefetch=1,
            grid=(n // COMB_TILE,),
            in_specs=[
                pl.BlockSpec((COMB_TILE, TOP_K), row),
                pl.BlockSpec((COMB_TILE, D_MODEL), row),
                pl.BlockSpec((None, SUBLANES, D_MODEL), lambda i, pos: (i // tiles_per_seq, 0, 0)),
                pl.BlockSpec((1, D_MODEL), lambda i, pos: (0, 0)),
                pl.BlockSpec(memory_space=pl.ANY),
            ],
            out_specs=pl.BlockSpec((COMB_TILE, D_MODEL), row),
            scratch_shapes=[
                pltpu.VMEM((TOP_K, COMB_TILE, D_MODEL), F32),
                pltpu.SemaphoreType.DMA(()),
            ],
        ),
        out_shape=jax.ShapeDtypeStruct((n, D_MODEL), F32),
        compiler_params=_cparams(1, VMEM_LIMIT),
        name="combine",
    )(pos, tw, x1, mod, g_final.reshape(1, D_MODEL), y)


def _rope_tables(positions):
    half = ROPE_DIM // 2
    inv_freq = ROPE_THETA ** (-jnp.arange(0, ROPE_DIM, 2, dtype=F32) / ROPE_DIM)
    ang = positions.reshape(-1).astype(F32)[:, None] * inv_freq
    cos, sin = jnp.cos(ang), jnp.sin(ang)
    n = ang.shape[0]
    rest = B_QK_DIM - ROPE_DIM
    c = jnp.concatenate([cos, cos, jnp.ones((n, rest), F32)], axis=1)
    m = jnp.concatenate([-sin, jnp.zeros((n, half + rest), F32)], axis=1)
    p = jnp.concatenate([jnp.zeros((n, half), F32), sin, jnp.zeros((n, rest), F32)], axis=1)
    rep = LANES // B_QK_DIM
    return jnp.tile(c, (1, rep)), jnp.tile(m, (1, rep)), jnp.tile(p, (1, rep))


def _routing_tables(top_i, rank, counts):
    n = top_i.shape[0]
    n_tiles = n * TOP_K // ROW_TILE + N_EXPERTS
    padded = (counts + ROW_TILE - 1) // ROW_TILE * ROW_TILE
    ends = jnp.cumsum(padded)
    starts = ends - padded
    onehot = top_i[..., None] == jnp.arange(N_EXPERTS, dtype=jnp.int32)
    pos = jnp.sum(jnp.where(onehot, starts, 0), axis=-1) + rank
    tile_row = jnp.arange(n_tiles, dtype=jnp.int32) * ROW_TILE
    live = tile_row < ends[-1]
    last_live = jnp.maximum(ends[-1] // ROW_TILE - 1, 0)
    tile_blk = jnp.minimum(jnp.arange(n_tiles, dtype=jnp.int32), last_live)
    tile_e = jnp.sum(tile_blk[:, None] * ROW_TILE >= ends[None, :], axis=1).astype(jnp.int32)
    tile_e = jnp.minimum(tile_e, N_EXPERTS - 1)
    prev_e = jnp.concatenate([jnp.full((1,), -1, jnp.int32), tile_e[:-1]])
    first = jnp.logical_and(live, tile_e != prev_e)
    return (pos.reshape(-1).astype(jnp.int32), tile_e, first.astype(jnp.int32),
            live.astype(jnp.int32), tile_blk, n_tiles * ROW_TILE)


def kernel(x, c, positions, w_ada, b_ada, g_mix_norm, w_in, rel_bias, lambda_q1, lambda_k1,
           lambda_q2, lambda_k2, g_subln, w_out, g_ffn_norm, w_router, b_router, w_gate_up,
           b_gate_up, w_down, b_down, g_final):
    batch, seq, _ = x.shape
    depth = w_ada.shape[0]
    assert depth == 1, "the combine kernel applies the final norm, so it must follow the only layer"
    n = batch * seq
    rope_c, rope_m, rope_p = _rope_tables(positions)
    x2 = x.reshape(n, D_MODEL)
    for l in range(depth):
        lambda_init = 0.8 - 0.6 * math.exp(-0.3 * l)
        mod = _ada(c, w_ada[l], b_ada[l])
        qa, ka, va, qb, kb, vb = _in_proj(x2, mod, g_mix_norm[l], w_in[l].astype(BF16),
                                          rope_c, rope_m, rope_p, seq)
        oa = _attn_a(qa, ka, va, _rel_bias_blocks(rel_bias[l]), batch, seq)
        lam = (jnp.exp(jnp.sum(lambda_q1[l].astype(F32) * lambda_k1[l].astype(F32)))
               - jnp.exp(jnp.sum(lambda_q2[l].astype(F32) * lambda_k2[l].astype(F32)))
               + lambda_init).reshape(1)
        ob = _attn_b(lam, qb, kb, vb, g_subln[l], batch, seq, 1.0 - lambda_init)
        x1, hp, top_i, top_w, rank, counts = _out_route(
            oa, ob, x2, mod, w_out[l].astype(BF16), g_ffn_norm[l], w_router[l].astype(BF16),
            b_router[l], seq)
        pos, tile_e, tile_first, tile_live, tile_blk, n_rows = _routing_tables(
            top_i, rank, counts[0])
        xs = _dispatch(pos, hp, n_rows)
        y = _experts(tile_e, tile_first, tile_live, tile_blk, xs, w_gate_up[l], b_gate_up[l],
                     w_down[l], b_down[l])
        x2 = _combine(pos, top_w, x1, mod, g_final, y, seq)
    return x2.reshape(batch, seq, D_MODEL)
```

```python
import functools
import math

import jax
import jax.numpy as jnp
from jax import lax
from jax.experimental import pallas as pl
from jax.experimental.pallas import tpu as pltpu

D_MODEL = 1024
CHUNK = 64
HEAD_DIM = 64
A_HEADS = 8
A_WIDTH = A_HEADS * HEAD_DIM
LEFT_CHUNKS = 8
MAX_REL = 128
B_HEADS = 4
B_QK_DIM = HEAD_DIM
B_V_DIM = 2 * HEAD_DIM
B_WIDTH = B_HEADS * B_V_DIM
ROPE_THETA = 500000.0
ROPE_DIM = B_QK_DIM // 4
N_EXPERTS = 32
TOP_K = 4
D_EXPERT = D_MODEL
SWIGLU_LIMIT = 7.0
SWIGLU_ALPHA = 1.702
EPS = 1e-6
NEG_INF = -1e30
LOG2_E = math.log2(math.e)
N_MOD = 6

LANES = 128
SUBLANES = 8
VMEM_BYTES_V7X = 64 * 1024 * 1024
VMEM_LIMIT = VMEM_BYTES_V7X * 7 // 8

TOK_TILE = 512
A_QBLK = 2 * CHUNK
A_BAND = (LEFT_CHUNKS + 2) * CHUNK
A_ROLL = A_BAND + A_QBLK
B_TQ = 512
B_TK = 512
B_SUB = 128
ROW_TILE = 512
FF_CHUNK = 512
COMB_TILE = 256

F32 = jnp.float32
BF16 = jnp.bfloat16


def _cparams(n_axes, vmem=None):
    return pltpu.CompilerParams(
        dimension_semantics=("arbitrary",) * n_axes,
        vmem_limit_bytes=vmem,
    )


def _ada_kernel(c_ref, w_ref, b_ref, o_ref):
    c = c_ref[...]
    act = c * jax.nn.sigmoid(c)
    o_ref[...] = jnp.dot(act, w_ref[...], preferred_element_type=F32,
                         precision=lax.Precision.HIGHEST) + b_ref[...]


def _ada(c, w_ada, b_ada):
    b = c.shape[0]
    rows = -(-b // SUBLANES) * SUBLANES
    c_pad = jnp.pad(c, ((0, rows - b), (0, 0)))
    n_out = w_ada.shape[1]
    out = pl.pallas_call(
        _ada_kernel,
        grid=(n_out // D_MODEL,),
        in_specs=[
            pl.BlockSpec((rows, D_MODEL), lambda j: (0, 0)),
            pl.BlockSpec((D_MODEL, D_MODEL), lambda j: (0, j)),
            pl.BlockSpec((1, D_MODEL), lambda j: (0, j)),
        ],
        out_specs=pl.BlockSpec((rows, D_MODEL), lambda j: (0, j)),
        out_shape=jax.ShapeDtypeStruct((rows, n_out), F32),
        compiler_params=_cparams(1),
        name="ada",
    )(c_pad, w_ada, b_ada.reshape(1, n_out))
    mod = out[:b].reshape(b, N_MOD, D_MODEL)
    return jnp.pad(mod, ((0, 0), (0, SUBLANES - N_MOD), (0, 0)))


def _in_proj_kernel(x_ref, mod_ref, g_ref, w_ref, rc_ref, rm_ref, rp_ref,
                    qa_ref, ka_ref, va_ref, qb_ref, kb_ref, vb_ref):
    x = x_ref[...]
    mod = mod_ref[...]
    y = x * lax.rsqrt(jnp.mean(x * x, axis=-1, keepdims=True) + EPS) * g_ref[...]
    h = (y * (1.0 + mod[1:2, :]) + mod[0:1, :]).astype(BF16)
    rc, rm, rp = rc_ref[...], rm_ref[...], rp_ref[...]
    q_scale = HEAD_DIM ** -0.5

    def rope(p):
        cols = []
        for s in range(p.shape[1] // LANES):
            v = p[:, s * LANES:(s + 1) * LANES]
            cols.append(v * rc + pltpu.roll(v, LANES - ROPE_DIM // 2, 1) * rm
                        + pltpu.roll(v, ROPE_DIM // 2, 1) * rp)
        return jnp.concatenate(cols, axis=1)

    outs = (qa_ref, ka_ref, va_ref, qb_ref, kb_ref, vb_ref)
    for j, o_ref in enumerate(outs):
        p = jnp.dot(h, w_ref[:, j * A_WIDTH:(j + 1) * A_WIDTH], preferred_element_type=F32)
        if j in (3, 4):
            p = rope(p)
        if j in (0, 3):
            p = p * (q_scale * LOG2_E)
        pb = p.astype(BF16)
        for s in range(A_WIDTH // LANES):
            o_ref[s] = pb[:, s * LANES:(s + 1) * LANES]


def _in_proj(x2, mod, g_mix, w_in_bf, rope_c, rope_m, rope_p, seq):
    n = x2.shape[0]
    tiles_per_seq = seq // TOK_TILE
    row = lambda i: (i, 0)
    fixed = lambda i: (0, 0)
    n_slabs = A_WIDTH // LANES
    out_sd = jax.ShapeDtypeStruct((n_slabs, n, LANES), BF16)
    return pl.pallas_call(
        _in_proj_kernel,
        grid=(n // TOK_TILE,),
        in_specs=[
            pl.BlockSpec((TOK_TILE, D_MODEL), row),
            pl.BlockSpec((None, SUBLANES, D_MODEL), lambda i: (i // tiles_per_seq, 0, 0)),
            pl.BlockSpec((1, D_MODEL), fixed),
            pl.BlockSpec(w_in_bf.shape, fixed),
            pl.BlockSpec((TOK_TILE, LANES), row),
            pl.BlockSpec((TOK_TILE, LANES), row),
            pl.BlockSpec((TOK_TILE, LANES), row),
        ],
        out_specs=[pl.BlockSpec((n_slabs, TOK_TILE, LANES), lambda i: (0, i, 0))] * 6,
        out_shape=[out_sd] * 6,
        compiler_params=_cparams(1, VMEM_LIMIT),
        name="in_proj",
    )(x2, mod, g_mix.reshape(1, D_MODEL), w_in_bf, rope_c, rope_m, rope_p)


def _attn_a_kernel(q_ref, kp_ref, kc_ref, vp_ref, vc_ref, bias_ref, o_ref, k_sc, v_sc):
    g = pl.program_id(1)
    n_pairs, blk, _ = q_ref.shape
    lane = lax.broadcasted_iota(jnp.int32, (A_QBLK, LANES), 1)
    col = lax.broadcasted_iota(jnp.int32, (A_QBLK, A_BAND), 1)
    q_chunk = lax.broadcasted_iota(jnp.int32, (A_QBLK, A_BAND), 0) // CHUNK
    k_chunk = col // CHUNK
    in_band = jnp.logical_and(k_chunk >= q_chunk, k_chunk <= q_chunk + LEFT_CHUNKS)
    ones = jnp.ones((A_BAND, LANES), BF16)

    def pair(p, carry):
        k_sc[0:blk, :] = kp_ref[p]
        k_sc[blk:2 * blk, :] = kc_ref[p]
        v_sc[0:blk, :] = vp_ref[p]
        v_sc[blk:2 * blk, :] = vc_ref[p]
        biases = []
        for hh in range(2):
            rolled = pltpu.roll(jnp.broadcast_to(bias_ref[2 * p + hh], (A_QBLK, A_ROLL)), 0, 1,
                                stride=1, stride_axis=0)
            biases.append(jnp.where(in_band, rolled[:, :A_BAND], NEG_INF))
        for m in range(blk // A_QBLK):
            r0 = m * A_QBLK
            q = q_ref[p, r0:r0 + A_QBLK, :]
            kband = k_sc[r0:r0 + A_BAND, :]
            v_ext = jnp.concatenate([v_sc[r0:r0 + A_BAND, :], ones], axis=1)
            valid = jnp.logical_or(g > 0, col + r0 >= blk)
            halves = []
            for hh in range(2):
                in_head = (lane < HEAD_DIM) if hh == 0 else (lane >= HEAD_DIM)
                qh = jnp.where(in_head, q, jnp.zeros_like(q))
                s = lax.dot_general(qh, kband, (((1,), (1,)), ((), ())),
                                    preferred_element_type=F32)
                s = jnp.where(valid, s + biases[hh], NEG_INF)
                pr = jnp.exp2(s - jnp.max(s, axis=1, keepdims=True))
                pv = jnp.dot(pr.astype(BF16), v_ext, preferred_element_type=F32)
                halves.append(pv[:, :LANES] / pv[:, LANES:])
            o_ref[p, r0:r0 + A_QBLK, :] = jnp.where(lane < HEAD_DIM, halves[0],
                                                    halves[1]).astype(BF16)
        return carry

    lax.fori_loop(0, n_pairs, pair, 0)


def _attn_a(qa, ka, va, bias_rows, batch, seq):
    n_pairs, n, _ = qa.shape
    blk = LEFT_CHUNKS * CHUNK
    nblk = seq // blk
    cur = lambda b, g: (0, b * nblk + g, 0)
    prev = lambda b, g: (0, b * nblk + jnp.maximum(g - 1, 0), 0)
    slab = (n_pairs, blk, LANES)
    return pl.pallas_call(
        _attn_a_kernel,
        grid=(batch, nblk),
        in_specs=[
            pl.BlockSpec(slab, cur),
            pl.BlockSpec(slab, prev),
            pl.BlockSpec(slab, cur),
            pl.BlockSpec(slab, prev),
            pl.BlockSpec(slab, cur),
            pl.BlockSpec(bias_rows.shape, lambda b, g: (0, 0, 0)),
        ],
        out_specs=pl.BlockSpec(slab, cur),
        out_shape=jax.ShapeDtypeStruct((n_pairs, n, LANES), BF16),
        scratch_shapes=[pltpu.VMEM((2 * blk, LANES), BF16), pltpu.VMEM((2 * blk, LANES), BF16)],
        compiler_params=_cparams(2),
        name="attn_a",
    )(qa, ka, ka, va, va, bias_rows)


def _rel_bias_rows(rel_table):
    t = rel_table.astype(F32) * LOG2_E
    far = t[:, 2 * MAX_REL:]
    n_far = LEFT_CHUNKS * CHUNK - MAX_REL
    row = jnp.concatenate([
        jnp.broadcast_to(far, (t.shape[0], n_far)),
        t[:, 2 * MAX_REL:0:-1],
        jnp.broadcast_to(far, (t.shape[0], A_ROLL - A_BAND)),
    ], axis=1)
    return row.reshape(t.shape[0], 1, A_ROLL)


def _attn_b_kernel(lam_ref, q_ref, k_ref, v_ref, g_ref, o_ref, q_sc, m_sc, acc_sc, *, out_scale):
    qi = pl.program_id(2)
    n_sub = B_TQ // B_SUB
    lane = lax.broadcasted_iota(jnp.int32, (B_TQ, LANES), 1)
    q = q_ref[...]
    q_sc[0] = jnp.where(lane < B_QK_DIM, q, jnp.zeros_like(q))
    q_sc[1] = jnp.where(lane >= B_QK_DIM, q, jnp.zeros_like(q))
    m_sc[...] = jnp.full(m_sc.shape, NEG_INF, F32)
    acc_sc[...] = jnp.zeros(acc_sc.shape, F32)
    ones = jnp.ones((B_TK, LANES), BF16)

    def update(sub, c, k, v_ext, keep):
        rows = pl.ds(sub * B_SUB, B_SUB)
        s = lax.dot_general(q_sc[c, rows, :], k, (((1,), (1,)), ((), ())),
                            preferred_element_type=F32)
        if keep is not None:
            s = jnp.where(keep, s, NEG_INF)
        m_prev = m_sc[c, rows, :]
        m_new = jnp.maximum(m_prev, jnp.max(s, axis=1, keepdims=True))
        alpha = jnp.exp2(m_prev - m_new)
        p = jnp.exp2(s - jnp.concatenate([m_new] * (s.shape[1] // LANES), axis=1))
        pv = jnp.dot(p.astype(BF16), v_ext, preferred_element_type=F32)
        acc_sc[c, rows, :] = jnp.concatenate([alpha, alpha], axis=1) * acc_sc[c, rows, :] + pv
        m_sc[c, rows, :] = m_new

    def full_block(j, carry):
        r0 = pl.multiple_of(j * B_TK, B_TK)
        k = k_ref[pl.ds(r0, B_TK), :]
        v_ext = jnp.concatenate([v_ref[pl.ds(r0, B_TK), :], ones], axis=1)
        for sub in range(n_sub):
            for c in range(2):
                update(sub, c, k, v_ext, None)
        return carry

    lax.fori_loop(0, qi, full_block, 0)

    r0 = pl.multiple_of(qi * B_TK, B_TK)
    for sub in range(n_sub):
        nk = (sub + 1) * B_SUB
        k = k_ref[pl.ds(r0, nk), :]
        v_ext = jnp.concatenate([v_ref[pl.ds(r0, nk), :], ones[:nk]], axis=1)
        row_c = lax.broadcasted_iota(jnp.int32, (B_SUB, nk), 0) // CHUNK + sub * (B_SUB // CHUNK)
        col_c = lax.broadcasted_iota(jnp.int32, (B_SUB, nk), 1) // CHUNK
        keep = col_c <= row_c
        for c in range(2):
            update(sub, c, k, v_ext, keep)

    lam = lam_ref[0]
    a0 = acc_sc[0]
    a1 = acc_sc[1]
    o = a0[:, :B_V_DIM] / a0[:, B_V_DIM:] - lam * (a1[:, :B_V_DIM] / a1[:, B_V_DIM:])
    y = o * lax.rsqrt(jnp.mean(o * o, axis=-1, keepdims=True) + EPS) * g_ref[...]
    o_ref[...] = (y * out_scale).astype(BF16)


def _attn_b(lam, qb, kb, vb, g_subln, batch, seq, out_scale):
    assert B_TQ == B_TK and B_V_DIM == LANES
    n = qb.shape[1]
    nq = seq // B_TQ
    q_map = lambda b, h, qi: (h, b * nq + qi, 0)
    kv_map = lambda b, h, qi: (h, b, 0)
    return pl.pallas_call(
        functools.partial(_attn_b_kernel, out_scale=out_scale),
        grid=(batch, B_HEADS, nq),
        in_specs=[
            pl.BlockSpec(memory_space=pltpu.SMEM),
            pl.BlockSpec((None, B_TQ, LANES), q_map),
            pl.BlockSpec((None, seq, LANES), kv_map),
            pl.BlockSpec((None, seq, LANES), kv_map),
            pl.BlockSpec((1, B_V_DIM), lambda b, h, qi: (0, 0)),
        ],
        out_specs=pl.BlockSpec((None, B_TQ, LANES), q_map),
        out_shape=jax.ShapeDtypeStruct((B_HEADS, n, LANES), BF16),
        scratch_shapes=[
            pltpu.VMEM((2, B_TQ, LANES), BF16),
            pltpu.VMEM((2, B_TQ, LANES), F32),
            pltpu.VMEM((2, B_TQ, 2 * LANES), F32),
        ],
        compiler_params=_cparams(3),
        name="attn_b",
    )(lam, qb, kb, vb, g_subln.reshape(1, B_V_DIM))


def _out_route_kernel(oa_ref, ob_ref, x_ref, mod_ref, wo_ref, g_ref, wr_ref, br_ref,
                      x1_ref, hp_ref, ti_ref, tw_ref, rk_ref, cnt_ref, tri_sc, carry_sc):
    i = pl.program_id(0)
    tm = x_ref.shape[0]

    @pl.when(i == 0)
    def _():
        r = lax.broadcasted_iota(jnp.int32, (tm, tm), 0)
        c = lax.broadcasted_iota(jnp.int32, (tm, tm), 1)
        tri_sc[...] = jnp.where(c < r, 1.0, 0.0).astype(BF16)
        carry_sc[...] = jnp.zeros(carry_sc.shape, F32)

    mod = mod_ref[...]
    o = jnp.concatenate([oa_ref[s] for s in range(oa_ref.shape[0])]
                        + [ob_ref[s] for s in range(ob_ref.shape[0])], axis=1)
    mix = jnp.dot(o, wo_ref[...], preferred_element_type=F32)
    x1 = x_ref[...] + mod[2:3, :] * mix
    x1_ref[...] = x1
    y = x1 * lax.rsqrt(jnp.mean(x1 * x1, axis=-1, keepdims=True) + EPS) * g_ref[...]
    h = y * (1.0 + mod[4:5, :]) + mod[3:4, :]
    hb = h.astype(BF16)

    half = D_MODEL // 2
    hi = lax.bitcast_convert_type(hb[:, :half].astype(F32), jnp.int32)
    lo = lax.bitcast_convert_type(hb[:, half:].astype(F32), jnp.int32)
    hp_ref[...] = hi | lax.shift_right_logical(lo, jnp.full(lo.shape, 16, jnp.int32))

    logits = jnp.dot(hb, wr_ref[...], preferred_element_type=F32) + br_ref[...]
    eid = lax.broadcasted_iota(jnp.int32, logits.shape, 1).astype(F32)
    work = logits
    vals, ids = [], []
    chosen = jnp.zeros(logits.shape, F32)
    for _ in range(TOP_K):
        v = jnp.max(work, axis=-1, keepdims=True)
        e = jnp.min(jnp.where(work == v, eid, float(N_EXPERTS)), axis=-1, keepdims=True)
        hit = eid == e
        vals.append(v)
        ids.append(e)
        chosen = jnp.where(hit, 1.0, chosen)
        work = jnp.where(hit, -jnp.inf, work)
    ex = [jnp.exp(v - vals[0]) for v in vals]
    den = ex[0] + ex[1] + ex[2] + ex[3]

    before = jnp.dot(tri_sc[...], chosen.astype(BF16), preferred_element_type=F32) + carry_sc[...]
    slot = lax.broadcasted_iota(jnp.int32, (tm, TOP_K), 1)
    ti = jnp.zeros((tm, TOP_K), F32)
    tw = jnp.zeros((tm, TOP_K), F32)
    rk = jnp.zeros((tm, TOP_K), F32)
    for kk in range(TOP_K):
        r_k = jnp.sum(jnp.where(eid == ids[kk], before, 0.0), axis=-1, keepdims=True)
        ti = jnp.where(slot == kk, ids[kk], ti)
        tw = jnp.where(slot == kk, ex[kk] / den, tw)
        rk = jnp.where(slot == kk, r_k, rk)
    ti_ref[...] = ti.astype(jnp.int32)
    tw_ref[...] = tw
    rk_ref[...] = rk.astype(jnp.int32)
    carry = carry_sc[...] + jnp.sum(chosen, axis=0, keepdims=True)
    carry_sc[...] = carry
    cnt_ref[...] = carry.astype(jnp.int32)


def _out_route(oa, ob, x2, mod, w_out_bf, g_ffn, w_router_bf, b_router, seq):
    n = x2.shape[0]
    tiles_per_seq = seq // TOK_TILE
    row = lambda i: (i, 0)
    fixed = lambda i: (0, 0)
    return pl.pallas_call(
        _out_route_kernel,
        grid=(n // TOK_TILE,),
        in_specs=[
            pl.BlockSpec((oa.shape[0], TOK_TILE, LANES), lambda i: (0, i, 0)),
            pl.BlockSpec((ob.shape[0], TOK_TILE, LANES), lambda i: (0, i, 0)),
            pl.BlockSpec((TOK_TILE, D_MODEL), row),
            pl.BlockSpec((None, SUBLANES, D_MODEL), lambda i: (i // tiles_per_seq, 0, 0)),
            pl.BlockSpec((D_MODEL, D_MODEL), fixed),
            pl.BlockSpec((1, D_MODEL), fixed),
            pl.BlockSpec((D_MODEL, N_EXPERTS), fixed),
            pl.BlockSpec((1, N_EXPERTS), fixed),
        ],
        out_specs=[
            pl.BlockSpec((TOK_TILE, D_MODEL), row),
            pl.BlockSpec((TOK_TILE, D_MODEL // 2), row),
            pl.BlockSpec((TOK_TILE, TOP_K), row),
            pl.BlockSpec((TOK_TILE, TOP_K), row),
            pl.BlockSpec((TOK_TILE, TOP_K), row),
            pl.BlockSpec((1, N_EXPERTS), fixed),
        ],
        out_shape=[
            jax.ShapeDtypeStruct((n, D_MODEL), F32),
            jax.ShapeDtypeStruct((n, D_MODEL // 2), jnp.int32),
            jax.ShapeDtypeStruct((n, TOP_K), jnp.int32),
            jax.ShapeDtypeStruct((n, TOP_K), F32),
            jax.ShapeDtypeStruct((n, TOP_K), jnp.int32),
            jax.ShapeDtypeStruct((1, N_EXPERTS), jnp.int32),
        ],
        scratch_shapes=[pltpu.VMEM((TOK_TILE, TOK_TILE), BF16), pltpu.VMEM((1, N_EXPERTS), F32)],
        compiler_params=_cparams(1, VMEM_LIMIT),
        name="out_route",
    )(oa, ob, x2, mod, w_out_bf, g_ffn.reshape(1, D_MODEL), w_router_bf,
      b_router.reshape(1, N_EXPERTS))


def _dispatch_kernel(pos_ref, hp_ref, xs_in_ref, xs_ref, sem):
    del xs_in_ref
    i = pl.program_id(0)
    tm = hp_ref.shape[0]
    base = i * tm * TOP_K

    def issue(t, carry):
        for kk in range(TOP_K):
            r = pos_ref[base + t * TOP_K + kk]
            pltpu.make_async_copy(hp_ref.at[pl.ds(t, 1), :], xs_ref.at[pl.ds(r, 1), :], sem).start()
        return carry

    lax.fori_loop(0, tm, issue, 0)
    for _ in range(TOP_K):
        pltpu.make_async_copy(hp_ref, xs_ref.at[pl.ds(0, tm), :], sem).wait()


def _dispatch(pos, hp, n_rows):
    n, width = hp.shape
    xs_init = jnp.zeros((n_rows, width), jnp.int32)
    return pl.pallas_call(
        _dispatch_kernel,
        grid_spec=pltpu.PrefetchScalarGridSpec(
            num_scalar_prefetch=1,
            grid=(n // TOK_TILE,),
            in_specs=[
                pl.BlockSpec((TOK_TILE, width), lambda i, pos: (i, 0)),
                pl.BlockSpec(memory_space=pl.ANY),
            ],
            out_specs=pl.BlockSpec(memory_space=pl.ANY),
            scratch_shapes=[pltpu.SemaphoreType.DMA(())],
        ),
        out_shape=jax.ShapeDtypeStruct((n_rows, width), jnp.int32),
        input_output_aliases={2: 0},
        compiler_params=_cparams(1),
        name="dispatch",
    )(pos, hp, xs_init)


def _experts_kernel(te_ref, first_ref, live_ref, blk_ref, xs_ref, wgu_ref, bgu_ref, wd_ref,
                    bd_ref, y_ref, wgu_sc, wd_sc):
    del te_ref, blk_ref
    j = pl.program_id(0)

    @pl.when(live_ref[j] == 1)
    def _():
        @pl.when(first_ref[j] == 1)
        def _():
            wgu_sc[...] = wgu_ref[...].astype(BF16)
            wd_sc[...] = wd_ref[...].astype(BF16)

        xw = xs_ref[...]
        hi = lax.bitcast_convert_type(xw & jnp.int32(-65536), F32).astype(BF16)
        lo = lax.bitcast_convert_type(lax.shift_left(xw, jnp.full(xw.shape, 16, jnp.int32)),
                                      F32).astype(BF16)
        x = jnp.concatenate([hi, lo], axis=1)
        acc = jnp.zeros((x.shape[0], D_MODEL), F32)
        for c in range(D_EXPERT // FF_CHUNK):
            lo_c, hi_c = c * FF_CHUNK, (c + 1) * FF_CHUNK
            gate = jnp.dot(x, wgu_sc[:, lo_c:hi_c], preferred_element_type=F32) + bgu_ref[:, lo_c:hi_c]
            up = (jnp.dot(x, wgu_sc[:, D_EXPERT + lo_c:D_EXPERT + hi_c], preferred_element_type=F32)
                  + bgu_ref[:, D_EXPERT + lo_c:D_EXPERT + hi_c])
            gate = jnp.minimum(gate, SWIGLU_LIMIT)
            up = jnp.clip(up, -SWIGLU_LIMIT, SWIGLU_LIMIT)
            act = (up + 1.0) * (gate * jax.nn.sigmoid(SWIGLU_ALPHA * gate))
            acc = acc + jnp.dot(act.astype(BF16), wd_sc[lo_c:hi_c, :], preferred_element_type=F32)
        y_ref[...] = acc + bd_ref[...]

    @pl.when(live_ref[j] == 0)
    def _():
        y_ref[...] = jnp.zeros(y_ref.shape, F32)


def _experts(tile_e, tile_first, tile_live, tile_blk, xs, w_gate_up, b_gate_up, w_down, b_down):
    n_rows, width = xs.shape
    n_tiles = n_rows // ROW_TILE
    by_tile = lambda j, te, fi, li, tb: (tb[j], 0)
    by_expert = lambda j, te, fi, li, tb: (te[j], 0, 0)
    return pl.pallas_call(
        _experts_kernel,
        grid_spec=pltpu.PrefetchScalarGridSpec(
            num_scalar_prefetch=4,
            grid=(n_tiles,),
            in_specs=[
                pl.BlockSpec((ROW_TILE, width), by_tile),
                pl.BlockSpec((None, D_MODEL, 2 * D_EXPERT), by_expert),
                pl.BlockSpec((None, 1, 2 * D_EXPERT), by_expert),
                pl.BlockSpec((None, D_EXPERT, D_MODEL), by_expert),
                pl.BlockSpec((None, 1, D_MODEL), by_expert),
            ],
            out_specs=pl.BlockSpec((ROW_TILE, D_MODEL), lambda j, te, fi, li, tb: (j, 0)),
            scratch_shapes=[
                pltpu.VMEM((D_MODEL, 2 * D_EXPERT), BF16),
                pltpu.VMEM((D_EXPERT, D_MODEL), BF16),
            ],
        ),
        out_shape=jax.ShapeDtypeStruct((n_rows, D_MODEL), F32),
        compiler_params=_cparams(1, VMEM_LIMIT),
        name="experts",
    )(tile_e, tile_first, tile_live, tile_blk, xs, w_gate_up,
      b_gate_up.reshape(N_EXPERTS, 1, 2 * D_EXPERT), w_down, b_down.reshape(N_EXPERTS, 1, D_MODEL))


def _combine_kernel(pos_ref, tw_ref, x1_ref, mod_ref, g_ref, y_ref, o_ref, ybuf, sem):
    i = pl.program_id(0)
    tm = x1_ref.shape[0]
    base = i * tm * TOP_K

    def issue(t, carry):
        for kk in range(TOP_K):
            r = pos_ref[base + t * TOP_K + kk]
            pltpu.make_async_copy(y_ref.at[pl.ds(r, 1), :], ybuf.at[kk, pl.ds(t, 1), :], sem).start()
        return carry

    lax.fori_loop(0, tm, issue, 0)
    for kk in range(TOP_K):
        pltpu.make_async_copy(y_ref.at[pl.ds(0, tm), :], ybuf.at[kk], sem).wait()

    tw = tw_ref[...]
    ffn = tw[:, 0:1] * ybuf[0]
    for kk in range(1, TOP_K):
        ffn = ffn + tw[:, kk:kk + 1] * ybuf[kk]
    x2 = x1_ref[...] + mod_ref[5:6, :] * ffn
    o_ref[...] = x2 * lax.rsqrt(jnp.mean(x2 * x2, axis=-1, keepdims=True) + EPS) * g_ref[...]


def _combine(pos, tw, x1, mod, g_final, y, seq):
    n = x1.shape[0]
    tiles_per_seq = seq // COMB_TILE
    row = lambda i, pos: (i, 0)
    return pl.pallas_call(
        _combine_kernel,
        grid_spec=pltpu.PrefetchScalarGridSpec(
            num_scalar_prefetch=1,
            grid=(n // COMB_TILE,),
            in_specs=[
                pl.BlockSpec((COMB_TILE, TOP_K), row),
                pl.BlockSpec((COMB_TILE, D_MODEL), row),
                pl.BlockSpec((None, SUBLANES, D_MODEL), lambda i, pos: (i // tiles_per_seq, 0, 0)),
                pl.BlockSpec((1, D_MODEL), lambda i, pos: (0, 0)),
                pl.BlockSpec(memory_space=pl.ANY),
            ],
            out_specs=pl.BlockSpec((COMB_TILE, D_MODEL), row),
            scratch_shapes=[
                pltpu.VMEM((TOP_K, COMB_TILE, D_MODEL), F32),
                pltpu.SemaphoreType.DMA(()),
            ],
        ),
        out_shape=jax.ShapeDtypeStruct((n, D_MODEL), F32),
        compiler_params=_cparams(1, VMEM_LIMIT),
        name="combine",
    )(pos, tw, x1, mod, g_final.reshape(1, D_MODEL), y)


def _rope_tables(positions):
    half = ROPE_DIM // 2
    inv_freq = ROPE_THETA ** (-jnp.arange(0, ROPE_DIM, 2, dtype=F32) / ROPE_DIM)
    ang = positions.reshape(-1).astype(F32)[:, None] * inv_freq
    cos, sin = jnp.cos(ang), jnp.sin(ang)
    n = ang.shape[0]
    rest = B_QK_DIM - ROPE_DIM
    c = jnp.concatenate([cos, cos, jnp.ones((n, rest), F32)], axis=1)
    m = jnp.concatenate([-sin, jnp.zeros((n, half + rest), F32)], axis=1)
    p = jnp.concatenate([jnp.zeros((n, half), F32), sin, jnp.zeros((n, rest), F32)], axis=1)
    rep = LANES // B_QK_DIM
    return jnp.tile(c, (1, rep)), jnp.tile(m, (1, rep)), jnp.tile(p, (1, rep))


def _routing_tables(top_i, rank, counts):
    n = top_i.shape[0]
    n_tiles = n * TOP_K // ROW_TILE + N_EXPERTS
    padded = (counts + ROW_TILE - 1) // ROW_TILE * ROW_TILE
    ends = jnp.cumsum(padded)
    starts = ends - padded
    onehot = top_i[..., None] == jnp.arange(N_EXPERTS, dtype=jnp.int32)
    pos = jnp.sum(jnp.where(onehot, starts, 0), axis=-1) + rank
    tile_row = jnp.arange(n_tiles, dtype=jnp.int32) * ROW_TILE
    live = tile_row < ends[-1]
    last_live = jnp.maximum(ends[-1] // ROW_TILE - 1, 0)
    tile_blk = jnp.minimum(jnp.arange(n_tiles, dtype=jnp.int32), last_live)
    tile_e = jnp.sum(tile_blk[:, None] * ROW_TILE >= ends[None, :], axis=1).astype(jnp.int32)
    tile_e = jnp.minimum(tile_e, N_EXPERTS - 1)
    prev_e = jnp.concatenate([jnp.full((1,), -1, jnp.int32), tile_e[:-1]])
    first = jnp.logical_and(live, tile_e != prev_e)
    return (pos.reshape(-1).astype(jnp.int32), tile_e, first.astype(jnp.int32),
            live.astype(jnp.int32), tile_blk, n_tiles * ROW_TILE)


def kernel(x, c, positions, w_ada, b_ada, g_mix_norm, w_in, rel_bias, lambda_q1, lambda_k1,
           lambda_q2, lambda_k2, g_subln, w_out, g_ffn_norm, w_router, b_router, w_gate_up,
           b_gate_up, w_down, b_down, g_final):
    batch, seq, _ = x.shape
    depth = w_ada.shape[0]
    assert depth == 1, "the combine kernel applies the final norm, so it must follow the only layer"
    n = batch * seq
    rope_c, rope_m, rope_p = _rope_tables(positions)
    x2 = x.reshape(n, D_MODEL)
    for l in range(depth):
        lambda_init = 0.8 - 0.6 * math.exp(-0.3 * l)
        mod = _ada(c, w_ada[l], b_ada[l])
        qa, ka, va, qb, kb, vb = _in_proj(x2, mod, g_mix_norm[l], w_in[l].astype(BF16),
                                          rope_c, rope_m, rope_p, seq)
        oa = _attn_a(qa, ka, va, _rel_bias_rows(rel_bias[l]), batch, seq)
        lam = (jnp.exp(jnp.sum(lambda_q1[l].astype(F32) * lambda_k1[l].astype(F32)))
               - jnp.exp(jnp.sum(lambda_q2[l].astype(F32) * lambda_k2[l].astype(F32)))
               + lambda_init).reshape(1)
        ob = _attn_b(lam, qb, kb, vb, g_subln[l], batch, seq, 1.0 - lambda_init)
        x1, hp, top_i, top_w, rank, counts = _out_route(
            oa, ob, x2, mod, w_out[l].astype(BF16), g_ffn_norm[l], w_router[l].astype(BF16),
            b_router[l], seq)
        pos, tile_e, tile_first, tile_live, tile_blk, n_rows = _routing_tables(
            top_i, rank, counts[0])
        xs = _dispatch(pos, hp, n_rows)
        y = _experts(tile_e, tile_first, tile_live, tile_blk, xs, w_gate_up[l], b_gate_up[l],
                     w_down[l], b_down[l])
        x2 = _combine(pos, top_w, x1, mod, g_final, y, seq)
    return x2.reshape(batch, seq, D_MODEL)
```

```python
import functools
import math

import jax
import jax.numpy as jnp
from jax import lax
from jax.experimental import pallas as pl
from jax.experimental.pallas import tpu as pltpu

D_MODEL = 1024
CHUNK = 64
HEAD_DIM = 64
A_HEADS = 8
A_WIDTH = A_HEADS * HEAD_DIM
LEFT_CHUNKS = 8
MAX_REL = 128
B_HEADS = 4
B_QK_DIM = HEAD_DIM
B_V_DIM = 2 * HEAD_DIM
B_WIDTH = B_HEADS * B_V_DIM
ROPE_THETA = 500000.0
ROPE_DIM = B_QK_DIM // 4
N_EXPERTS = 32
TOP_K = 4
D_EXPERT = D_MODEL
SWIGLU_LIMIT = 7.0
SWIGLU_ALPHA = 1.702
EPS = 1e-6
NEG_INF = -1e30
LOG2_E = math.log2(math.e)
N_MOD = 6

LANES = 128
SUBLANES = 8
VMEM_BYTES_V7X = 64 * 1024 * 1024
VMEM_LIMIT = VMEM_BYTES_V7X * 7 // 8

TOK_TILE = 512
A_QBLK = 2 * CHUNK
A_BAND = (LEFT_CHUNKS + 2) * CHUNK
A_ROLL = A_BAND + A_QBLK
B_TQ = 512
B_TK = 512
B_SUB = 128
ROW_TILE = 512
FF_CHUNK = 512
COMB_TILE = 256
COMB_UNROLL = 4

F32 = jnp.float32
BF16 = jnp.bfloat16


def _cparams(n_axes, vmem=None):
    return pltpu.CompilerParams(
        dimension_semantics=("arbitrary",) * n_axes,
        vmem_limit_bytes=vmem,
    )


def _ada_kernel(c_ref, w_ref, b_ref, o_ref):
    c = c_ref[...]
    act = c * jax.nn.sigmoid(c)
    o_ref[...] = jnp.dot(act, w_ref[...], preferred_element_type=F32,
                         precision=lax.Precision.HIGHEST) + b_ref[...]


def _ada(c, w_ada, b_ada):
    b = c.shape[0]
    rows = -(-b // SUBLANES) * SUBLANES
    c_pad = jnp.pad(c, ((0, rows - b), (0, 0)))
    n_out = w_ada.shape[1]
    out = pl.pallas_call(
        _ada_kernel,
        grid=(n_out // D_MODEL,),
        in_specs=[
            pl.BlockSpec((rows, D_MODEL), lambda j: (0, 0)),
            pl.BlockSpec((D_MODEL, D_MODEL), lambda j: (0, j)),
            pl.BlockSpec((1, D_MODEL), lambda j: (0, j)),
        ],
        out_specs=pl.BlockSpec((rows, D_MODEL), lambda j: (0, j)),
        out_shape=jax.ShapeDtypeStruct((rows, n_out), F32),
        compiler_params=_cparams(1),
        name="ada",
    )(c_pad, w_ada, b_ada.reshape(1, n_out))
    mod = out[:b].reshape(b, N_MOD, D_MODEL)
    return jnp.pad(mod, ((0, 0), (0, SUBLANES - N_MOD), (0, 0)))


def _in_proj_kernel(x_ref, mod_ref, g_ref, w_ref, rc_ref, rm_ref, rp_ref,
                    qa_ref, ka_ref, va_ref, qb_ref, kb_ref, vb_ref):
    x = x_ref[...]
    mod = mod_ref[...]
    y = x * lax.rsqrt(jnp.mean(x * x, axis=-1, keepdims=True) + EPS) * g_ref[...]
    h = (y * (1.0 + mod[1:2, :]) + mod[0:1, :]).astype(BF16)
    rc, rm, rp = rc_ref[...], rm_ref[...], rp_ref[...]
    q_scale = HEAD_DIM ** -0.5

    def rope(p):
        cols = []
        for s in range(p.shape[1] // LANES):
            v = p[:, s * LANES:(s + 1) * LANES]
            cols.append(v * rc + pltpu.roll(v, LANES - ROPE_DIM // 2, 1) * rm
                        + pltpu.roll(v, ROPE_DIM // 2, 1) * rp)
        return jnp.concatenate(cols, axis=1)

    outs = (qa_ref, ka_ref, va_ref, qb_ref, kb_ref, vb_ref)
    for j, o_ref in enumerate(outs):
        p = jnp.dot(h, w_ref[:, j * A_WIDTH:(j + 1) * A_WIDTH], preferred_element_type=F32)
        if j in (3, 4):
            p = rope(p)
        if j in (0, 3):
            p = p * (q_scale * LOG2_E)
        pb = p.astype(BF16)
        for s in range(A_WIDTH // LANES):
            o_ref[s] = pb[:, s * LANES:(s + 1) * LANES]


def _in_proj(x2, mod, g_mix, w_in_bf, rope_c, rope_m, rope_p, seq):
    n = x2.shape[0]
    tiles_per_seq = seq // TOK_TILE
    row = lambda i: (i, 0)
    fixed = lambda i: (0, 0)
    n_slabs = A_WIDTH // LANES
    out_sd = jax.ShapeDtypeStruct((n_slabs, n, LANES), BF16)
    return pl.pallas_call(
        _in_proj_kernel,
        grid=(n // TOK_TILE,),
        in_specs=[
            pl.BlockSpec((TOK_TILE, D_MODEL), row),
            pl.BlockSpec((None, SUBLANES, D_MODEL), lambda i: (i // tiles_per_seq, 0, 0)),
            pl.BlockSpec((1, D_MODEL), fixed),
            pl.BlockSpec(w_in_bf.shape, fixed),
            pl.BlockSpec((TOK_TILE, LANES), row),
            pl.BlockSpec((TOK_TILE, LANES), row),
            pl.BlockSpec((TOK_TILE, LANES), row),
        ],
        out_specs=[pl.BlockSpec((n_slabs, TOK_TILE, LANES), lambda i: (0, i, 0))] * 6,
        out_shape=[out_sd] * 6,
        compiler_params=_cparams(1, VMEM_LIMIT),
        name="in_proj",
    )(x2, mod, g_mix.reshape(1, D_MODEL), w_in_bf, rope_c, rope_m, rope_p)


def _attn_a_kernel(q_ref, kp_ref, kc_ref, vp_ref, vc_ref, bias_ref, o_ref, k_sc, v_sc):
    g = pl.program_id(1)
    n_pairs, blk, _ = q_ref.shape
    lane = lax.broadcasted_iota(jnp.int32, (A_QBLK, LANES), 1)
    col = lax.broadcasted_iota(jnp.int32, (A_QBLK, A_BAND), 1)
    q_chunk = lax.broadcasted_iota(jnp.int32, (A_QBLK, A_BAND), 0) // CHUNK
    k_chunk = col // CHUNK
    in_band = jnp.logical_and(k_chunk >= q_chunk, k_chunk <= q_chunk + LEFT_CHUNKS)
    ones = jnp.ones((A_BAND, LANES), BF16)

    def pair(p, carry):
        k_sc[0:blk, :] = kp_ref[p]
        k_sc[blk:2 * blk, :] = kc_ref[p]
        v_sc[0:blk, :] = vp_ref[p]
        v_sc[blk:2 * blk, :] = vc_ref[p]
        biases = []
        for hh in range(2):
            rolled = pltpu.roll(jnp.broadcast_to(bias_ref[2 * p + hh], (A_QBLK, A_ROLL)), 0, 1,
                                stride=1, stride_axis=0)
            biases.append(jnp.where(in_band, rolled[:, :A_BAND], NEG_INF))
        for m in range(blk // A_QBLK):
            r0 = m * A_QBLK
            q = q_ref[p, r0:r0 + A_QBLK, :]
            kband = k_sc[r0:r0 + A_BAND, :]
            v_ext = jnp.concatenate([v_sc[r0:r0 + A_BAND, :], ones], axis=1)
            valid = jnp.logical_or(g > 0, col + r0 >= blk)
            halves = []
            for hh in range(2):
                in_head = (lane < HEAD_DIM) if hh == 0 else (lane >= HEAD_DIM)
                qh = jnp.where(in_head, q, jnp.zeros_like(q))
                s = lax.dot_general(qh, kband, (((1,), (1,)), ((), ())),
                                    preferred_element_type=F32)
                s = jnp.where(valid, s + biases[hh], NEG_INF)
                pr = jnp.exp2(s - jnp.max(s, axis=1, keepdims=True))
                pv = jnp.dot(pr.astype(BF16), v_ext, preferred_element_type=F32)
                halves.append(pv[:, :LANES] / pv[:, LANES:])
            o_ref[p, r0:r0 + A_QBLK, :] = jnp.where(lane < HEAD_DIM, halves[0],
                                                    halves[1]).astype(BF16)
        return carry

    lax.fori_loop(0, n_pairs, pair, 0)


def _attn_a(qa, ka, va, bias_rows, batch, seq):
    n_pairs, n, _ = qa.shape
    blk = LEFT_CHUNKS * CHUNK
    nblk = seq // blk
    cur = lambda b, g: (0, b * nblk + g, 0)
    prev = lambda b, g: (0, b * nblk + jnp.maximum(g - 1, 0), 0)
    slab = (n_pairs, blk, LANES)
    return pl.pallas_call(
        _attn_a_kernel,
        grid=(batch, nblk),
        in_specs=[
            pl.BlockSpec(slab, cur),
            pl.BlockSpec(slab, prev),
            pl.BlockSpec(slab, cur),
            pl.BlockSpec(slab, prev),
            pl.BlockSpec(slab, cur),
            pl.BlockSpec(bias_rows.shape, lambda b, g: (0, 0, 0)),
        ],
        out_specs=pl.BlockSpec(slab, cur),
        out_shape=jax.ShapeDtypeStruct((n_pairs, n, LANES), BF16),
        scratch_shapes=[pltpu.VMEM((2 * blk, LANES), BF16), pltpu.VMEM((2 * blk, LANES), BF16)],
        compiler_params=_cparams(2),
        name="attn_a",
    )(qa, ka, ka, va, va, bias_rows)


def _rel_bias_rows(rel_table):
    t = rel_table.astype(F32) * LOG2_E
    far = t[:, 2 * MAX_REL:]
    n_far = LEFT_CHUNKS * CHUNK - MAX_REL
    row = jnp.concatenate([
        jnp.broadcast_to(far, (t.shape[0], n_far)),
        t[:, 2 * MAX_REL:0:-1],
        jnp.broadcast_to(far, (t.shape[0], A_ROLL - A_BAND)),
    ], axis=1)
    return row.reshape(t.shape[0], 1, A_ROLL)


def _attn_b_kernel(lam_ref, q_ref, k_ref, v_ref, g_ref, o_ref, q_sc, m_sc, acc_sc, *, out_scale):
    qi = pl.program_id(2)
    n_sub = B_TQ // B_SUB
    lane = lax.broadcasted_iota(jnp.int32, (B_TQ, LANES), 1)
    q = q_ref[...]
    q_sc[0] = jnp.where(lane < B_QK_DIM, q, jnp.zeros_like(q))
    q_sc[1] = jnp.where(lane >= B_QK_DIM, q, jnp.zeros_like(q))
    m_sc[...] = jnp.full(m_sc.shape, NEG_INF, F32)
    acc_sc[...] = jnp.zeros(acc_sc.shape, F32)
    ones = jnp.ones((B_TK, LANES), BF16)

    def update(sub, c, k, v_ext, keep):
        rows = pl.ds(sub * B_SUB, B_SUB)
        s = lax.dot_general(q_sc[c, rows, :], k, (((1,), (1,)), ((), ())),
                            preferred_element_type=F32)
        if keep is not None:
            s = jnp.where(keep, s, NEG_INF)
        m_prev = m_sc[c, rows, :]
        m_new = jnp.maximum(m_prev, jnp.max(s, axis=1, keepdims=True))
        alpha = jnp.exp2(m_prev - m_new)
        p = jnp.exp2(s - jnp.concatenate([m_new] * (s.shape[1] // LANES), axis=1))
        pv = jnp.dot(p.astype(BF16), v_ext, preferred_element_type=F32)
        acc_sc[c, rows, :] = jnp.concatenate([alpha, alpha], axis=1) * acc_sc[c, rows, :] + pv
        m_sc[c, rows, :] = m_new

    def full_block(j, carry):
        r0 = pl.multiple_of(j * B_TK, B_TK)
        k = k_ref[pl.ds(r0, B_TK), :]
        v_ext = jnp.concatenate([v_ref[pl.ds(r0, B_TK), :], ones], axis=1)
        for sub in range(n_sub):
            for c in range(2):
                update(sub, c, k, v_ext, None)
        return carry

    lax.fori_loop(0, qi, full_block, 0)

    r0 = pl.multiple_of(qi * B_TK, B_TK)
    for sub in range(n_sub):
        nk = (sub + 1) * B_SUB
        k = k_ref[pl.ds(r0, nk), :]
        v_ext = jnp.concatenate([v_ref[pl.ds(r0, nk), :], ones[:nk]], axis=1)
        row_c = lax.broadcasted_iota(jnp.int32, (B_SUB, nk), 0) // CHUNK + sub * (B_SUB // CHUNK)
        col_c = lax.broadcasted_iota(jnp.int32, (B_SUB, nk), 1) // CHUNK
        keep = col_c <= row_c
        for c in range(2):
            update(sub, c, k, v_ext, keep)

    lam = lam_ref[0]
    a0 = acc_sc[0]
    a1 = acc_sc[1]
    o = a0[:, :B_V_DIM] / a0[:, B_V_DIM:] - lam * (a1[:, :B_V_DIM] / a1[:, B_V_DIM:])
    y = o * lax.rsqrt(jnp.mean(o * o, axis=-1, keepdims=True) + EPS) * g_ref[...]
    o_ref[...] = (y * out_scale).astype(BF16)


def _attn_b(lam, qb, kb, vb, g_subln, batch, seq, out_scale):
    assert B_TQ == B_TK and B_V_DIM == LANES
    n = qb.shape[1]
    nq = seq // B_TQ
    q_map = lambda b, h, qi: (h, b * nq + qi, 0)
    kv_map = lambda b, h, qi: (h, b, 0)
    return pl.pallas_call(
        functools.partial(_attn_b_kernel, out_scale=out_scale),
        grid=(batch, B_HEADS, nq),
        in_specs=[
            pl.BlockSpec(memory_space=pltpu.SMEM),
            pl.BlockSpec((None, B_TQ, LANES), q_map),
            pl.BlockSpec((None, seq, LANES), kv_map),
            pl.BlockSpec((None, seq, LANES), kv_map),
            pl.BlockSpec((1, B_V_DIM), lambda b, h, qi: (0, 0)),
        ],
        out_specs=pl.BlockSpec((None, B_TQ, LANES), q_map),
        out_shape=jax.ShapeDtypeStruct((B_HEADS, n, LANES), BF16),
        scratch_shapes=[
            pltpu.VMEM((2, B_TQ, LANES), BF16),
            pltpu.VMEM((2, B_TQ, LANES), F32),
            pltpu.VMEM((2, B_TQ, 2 * LANES), F32),
        ],
        compiler_params=_cparams(3),
        name="attn_b",
    )(lam, qb, kb, vb, g_subln.reshape(1, B_V_DIM))


def _pack_bf16_pairs(v):
    half = v.shape[1] // 2
    vb = v.astype(BF16)
    hi = lax.bitcast_convert_type(vb[:, :half].astype(F32), jnp.int32)
    lo = lax.bitcast_convert_type(vb[:, half:].astype(F32), jnp.int32)
    return hi | lax.shift_right_logical(lo, jnp.full(lo.shape, 16, jnp.int32))


def _unpack_bf16_pairs(w):
    first = lax.bitcast_convert_type(w & jnp.int32(-65536), F32)
    second = lax.bitcast_convert_type(lax.shift_left(w, jnp.full(w.shape, 16, jnp.int32)), F32)
    return first, second


def _out_route_kernel(oa_ref, ob_ref, x_ref, mod_ref, wo_ref, g_ref, wr_ref, br_ref,
                      x1_ref, hp_ref, ti_ref, tw_ref, rk_ref, cnt_ref, tri_sc, carry_sc):
    i = pl.program_id(0)
    tm = x_ref.shape[0]

    @pl.when(i == 0)
    def _():
        r = lax.broadcasted_iota(jnp.int32, (tm, tm), 0)
        c = lax.broadcasted_iota(jnp.int32, (tm, tm), 1)
        tri_sc[...] = jnp.where(c < r, 1.0, 0.0).astype(BF16)
        carry_sc[...] = jnp.zeros(carry_sc.shape, F32)

    mod = mod_ref[...]
    o = jnp.concatenate([oa_ref[s] for s in range(oa_ref.shape[0])]
                        + [ob_ref[s] for s in range(ob_ref.shape[0])], axis=1)
    mix = jnp.dot(o, wo_ref[...], preferred_element_type=F32)
    x1 = x_ref[...] + mod[2:3, :] * mix
    x1_ref[...] = x1
    y = x1 * lax.rsqrt(jnp.mean(x1 * x1, axis=-1, keepdims=True) + EPS) * g_ref[...]
    h = y * (1.0 + mod[4:5, :]) + mod[3:4, :]
    hb = h.astype(BF16)
    hp_ref[...] = _pack_bf16_pairs(h)

    logits = jnp.dot(hb, wr_ref[...], preferred_element_type=F32) + br_ref[...]
    eid = lax.broadcasted_iota(jnp.int32, logits.shape, 1).astype(F32)
    work = logits
    vals, ids = [], []
    chosen = jnp.zeros(logits.shape, F32)
    for _ in range(TOP_K):
        v = jnp.max(work, axis=-1, keepdims=True)
        e = jnp.min(jnp.where(work == v, eid, float(N_EXPERTS)), axis=-1, keepdims=True)
        hit = eid == e
        vals.append(v)
        ids.append(e)
        chosen = jnp.where(hit, 1.0, chosen)
        work = jnp.where(hit, -jnp.inf, work)
    ex = [jnp.exp(v - vals[0]) for v in vals]
    den = ex[0] + ex[1] + ex[2] + ex[3]

    before = jnp.dot(tri_sc[...], chosen.astype(BF16), preferred_element_type=F32) + carry_sc[...]
    slot = lax.broadcasted_iota(jnp.int32, (tm, TOP_K), 1)
    ti = jnp.zeros((tm, TOP_K), F32)
    tw = jnp.zeros((tm, TOP_K), F32)
    rk = jnp.zeros((tm, TOP_K), F32)
    for kk in range(TOP_K):
        r_k = jnp.sum(jnp.where(eid == ids[kk], before, 0.0), axis=-1, keepdims=True)
        ti = jnp.where(slot == kk, ids[kk], ti)
        tw = jnp.where(slot == kk, ex[kk] / den, tw)
        rk = jnp.where(slot == kk, r_k, rk)
    ti_ref[...] = ti.astype(jnp.int32)
    tw_ref[...] = tw
    rk_ref[...] = rk.astype(jnp.int32)
    carry = carry_sc[...] + jnp.sum(chosen, axis=0, keepdims=True)
    carry_sc[...] = carry
    cnt_ref[...] = carry.astype(jnp.int32)


def _out_route(oa, ob, x2, mod, w_out_bf, g_ffn, w_router_bf, b_router, seq):
    n = x2.shape[0]
    tiles_per_seq = seq // TOK_TILE
    row = lambda i: (i, 0)
    fixed = lambda i: (0, 0)
    return pl.pallas_call(
        _out_route_kernel,
        grid=(n // TOK_TILE,),
        in_specs=[
            pl.BlockSpec((oa.shape[0], TOK_TILE, LANES), lambda i: (0, i, 0)),
            pl.BlockSpec((ob.shape[0], TOK_TILE, LANES), lambda i: (0, i, 0)),
            pl.BlockSpec((TOK_TILE, D_MODEL), row),
            pl.BlockSpec((None, SUBLANES, D_MODEL), lambda i: (i // tiles_per_seq, 0, 0)),
            pl.BlockSpec((D_MODEL, D_MODEL), fixed),
            pl.BlockSpec((1, D_MODEL), fixed),
            pl.BlockSpec((D_MODEL, N_EXPERTS), fixed),
            pl.BlockSpec((1, N_EXPERTS), fixed),
        ],
        out_specs=[
            pl.BlockSpec((TOK_TILE, D_MODEL), row),
            pl.BlockSpec((TOK_TILE, D_MODEL // 2), row),
            pl.BlockSpec((TOK_TILE, TOP_K), row),
            pl.BlockSpec((TOK_TILE, TOP_K), row),
            pl.BlockSpec((TOK_TILE, TOP_K), row),
            pl.BlockSpec((1, N_EXPERTS), fixed),
        ],
        out_shape=[
            jax.ShapeDtypeStruct((n, D_MODEL), F32),
            jax.ShapeDtypeStruct((n, D_MODEL // 2), jnp.int32),
            jax.ShapeDtypeStruct((n, TOP_K), jnp.int32),
            jax.ShapeDtypeStruct((n, TOP_K), F32),
            jax.ShapeDtypeStruct((n, TOP_K), jnp.int32),
            jax.ShapeDtypeStruct((1, N_EXPERTS), jnp.int32),
        ],
        scratch_shapes=[pltpu.VMEM((TOK_TILE, TOK_TILE), BF16), pltpu.VMEM((1, N_EXPERTS), F32)],
        compiler_params=_cparams(1, VMEM_LIMIT),
        name="out_route",
    )(oa, ob, x2, mod, w_out_bf, g_ffn.reshape(1, D_MODEL), w_router_bf,
      b_router.reshape(1, N_EXPERTS))


def _dispatch_kernel(pos_ref, hp_ref, xs_in_ref, xs_ref, sem):
    del xs_in_ref
    i = pl.program_id(0)
    tm = hp_ref.shape[0]
    base = i * tm * TOP_K

    def issue(t, carry):
        for kk in range(TOP_K):
            r = pos_ref[base + t * TOP_K + kk]
            pltpu.make_async_copy(hp_ref.at[pl.ds(t, 1), :], xs_ref.at[pl.ds(r, 1), :], sem).start()
        return carry

    lax.fori_loop(0, tm, issue, 0)
    for _ in range(TOP_K):
        pltpu.make_async_copy(hp_ref, xs_ref.at[pl.ds(0, tm), :], sem).wait()


def _dispatch(pos, hp, n_rows):
    n, width = hp.shape
    xs_init = jnp.zeros((n_rows, width), jnp.int32)
    return pl.pallas_call(
        _dispatch_kernel,
        grid_spec=pltpu.PrefetchScalarGridSpec(
            num_scalar_prefetch=1,
            grid=(n // TOK_TILE,),
            in_specs=[
                pl.BlockSpec((TOK_TILE, width), lambda i, pos: (i, 0)),
                pl.BlockSpec(memory_space=pl.ANY),
            ],
            out_specs=pl.BlockSpec(memory_space=pl.ANY),
            scratch_shapes=[pltpu.SemaphoreType.DMA(())],
        ),
        out_shape=jax.ShapeDtypeStruct((n_rows, width), jnp.int32),
        input_output_aliases={2: 0},
        compiler_params=_cparams(1),
        name="dispatch",
    )(pos, hp, xs_init)


def _experts_kernel(te_ref, first_ref, live_ref, blk_ref, xs_ref, wgu_ref, bgu_ref, wd_ref,
                    bd_ref, y_ref, wgu_sc, wd_sc):
    del te_ref, blk_ref
    j = pl.program_id(0)

    @pl.when(live_ref[j] == 1)
    def _():
        @pl.when(first_ref[j] == 1)
        def _():
            wgu_sc[...] = wgu_ref[...].astype(BF16)
            wd_sc[...] = wd_ref[...].astype(BF16)

        first, second = _unpack_bf16_pairs(xs_ref[...])
        x = jnp.concatenate([first.astype(BF16), second.astype(BF16)], axis=1)
        acc = jnp.zeros((x.shape[0], D_MODEL), F32)
        for c in range(D_EXPERT // FF_CHUNK):
            lo_c, hi_c = c * FF_CHUNK, (c + 1) * FF_CHUNK
            gate = jnp.dot(x, wgu_sc[:, lo_c:hi_c], preferred_element_type=F32) + bgu_ref[:, lo_c:hi_c]
            up = (jnp.dot(x, wgu_sc[:, D_EXPERT + lo_c:D_EXPERT + hi_c], preferred_element_type=F32)
                  + bgu_ref[:, D_EXPERT + lo_c:D_EXPERT + hi_c])
            gate = jnp.minimum(gate, SWIGLU_LIMIT)
            up = jnp.clip(up, -SWIGLU_LIMIT, SWIGLU_LIMIT)
            act = (up + 1.0) * (gate * jax.nn.sigmoid(SWIGLU_ALPHA * gate))
            acc = acc + jnp.dot(act.astype(BF16), wd_sc[lo_c:hi_c, :], preferred_element_type=F32)
        y_ref[...] = _pack_bf16_pairs(acc + bd_ref[...])

    @pl.when(live_ref[j] == 0)
    def _():
        y_ref[...] = jnp.zeros(y_ref.shape, jnp.int32)


def _experts(tile_e, tile_first, tile_live, tile_blk, xs, w_gate_up, b_gate_up, w_down, b_down):
    n_rows, width = xs.shape
    n_tiles = n_rows // ROW_TILE
    by_tile = lambda j, te, fi, li, tb: (tb[j], 0)
    by_expert = lambda j, te, fi, li, tb: (te[j], 0, 0)
    return pl.pallas_call(
        _experts_kernel,
        grid_spec=pltpu.PrefetchScalarGridSpec(
            num_scalar_prefetch=4,
            grid=(n_tiles,),
            in_specs=[
                pl.BlockSpec((ROW_TILE, width), by_tile),
                pl.BlockSpec((None, D_MODEL, 2 * D_EXPERT), by_expert),
                pl.BlockSpec((None, 1, 2 * D_EXPERT), by_expert),
                pl.BlockSpec((None, D_EXPERT, D_MODEL), by_expert),
                pl.BlockSpec((None, 1, D_MODEL), by_expert),
            ],
            out_specs=pl.BlockSpec((ROW_TILE, D_MODEL // 2), lambda j, te, fi, li, tb: (j, 0)),
            scratch_shapes=[
                pltpu.VMEM((D_MODEL, 2 * D_EXPERT), BF16),
                pltpu.VMEM((D_EXPERT, D_MODEL), BF16),
            ],
        ),
        out_shape=jax.ShapeDtypeStruct((n_rows, D_MODEL // 2), jnp.int32),
        compiler_params=_cparams(1, VMEM_LIMIT),
        name="experts",
    )(tile_e, tile_first, tile_live, tile_blk, xs, w_gate_up,
      b_gate_up.reshape(N_EXPERTS, 1, 2 * D_EXPERT), w_down, b_down.reshape(N_EXPERTS, 1, D_MODEL))


def _combine_kernel(pos_ref, tw_ref, x1_ref, mod_ref, g_ref, y_ref, o_ref, ybuf, sems):
    i = pl.program_id(0)
    tm = x1_ref.shape[0]
    width = y_ref.shape[1]

    def issue_tile(tile, slot):
        def body(g, carry):
            for u in range(COMB_UNROLL):
                t = g * COMB_UNROLL + u
                for kk in range(TOP_K):
                    r = pos_ref[(tile * tm + t) * TOP_K + kk]
                    pltpu.make_async_copy(
                        y_ref.at[pl.ds(r, 1), :],
                        ybuf.at[slot, pl.ds(t, 1), pl.ds(kk * width, width)],
                        sems.at[slot]).start()
            return carry

        lax.fori_loop(0, tm // COMB_UNROLL, body, 0)

    @pl.when(i == 0)
    def _():
        issue_tile(0, 0)

    @pl.when(i + 1 < pl.num_programs(0))
    def _():
        issue_tile(i + 1, (i + 1) % 2)

    slot = i % 2
    for kk in range(TOP_K):
        pltpu.make_async_copy(y_ref.at[pl.ds(0, tm), :],
                              ybuf.at[slot, :, pl.ds(kk * width, width)], sems.at[slot]).wait()

    tw = tw_ref[...]
    first = jnp.zeros((tm, width), F32)
    second = jnp.zeros((tm, width), F32)
    for kk in range(TOP_K):
        f_k, s_k = _unpack_bf16_pairs(ybuf[slot, :, kk * width:(kk + 1) * width])
        first = first + tw[:, kk:kk + 1] * f_k
        second = second + tw[:, kk:kk + 1] * s_k
    ffn = jnp.concatenate([first, second], axis=1)
    x2 = x1_ref[...] + mod_ref[5:6, :] * ffn
    o_ref[...] = x2 * lax.rsqrt(jnp.mean(x2 * x2, axis=-1, keepdims=True) + EPS) * g_ref[...]


def _combine(pos, tw, x1, mod, g_final, y, seq):
    n = x1.shape[0]
    tiles_per_seq = seq // COMB_TILE
    row = lambda i, pos: (i, 0)
    return pl.pallas_call(
        _combine_kernel,
        grid_spec=pltpu.PrefetchScalarGridSpec(
            num_scalar_prefetch=1,
            grid=(n // COMB_TILE,),
            in_specs=[
                pl.BlockSpec((COMB_TILE, TOP_K), row),
                pl.BlockSpec((COMB_TILE, D_MODEL), row),
                pl.BlockSpec((None, SUBLANES, D_MODEL), lambda i, pos: (i // tiles_per_seq, 0, 0)),
                pl.BlockSpec((1, D_MODEL), lambda i, pos: (0, 0)),
                pl.BlockSpec(memory_space=pl.ANY),
            ],
            out_specs=pl.BlockSpec((COMB_TILE, D_MODEL), row),
            scratch_shapes=[
                pltpu.VMEM((2, COMB_TILE, TOP_K * y.shape[1]), jnp.int32),
                pltpu.SemaphoreType.DMA((2,)),
            ],
        ),
        out_shape=jax.ShapeDtypeStruct((n, D_MODEL), F32),
        compiler_params=_cparams(1, VMEM_LIMIT),
        name="combine",
    )(pos, tw, x1, mod, g_final.reshape(1, D_MODEL), y)


def _rope_tables(positions):
    half = ROPE_DIM // 2
    inv_freq = ROPE_THETA ** (-jnp.arange(0, ROPE_DIM, 2, dtype=F32) / ROPE_DIM)
    ang = positions.reshape(-1).astype(F32)[:, None] * inv_freq
    cos, sin = jnp.cos(ang), jnp.sin(ang)
    n = ang.shape[0]
    rest = B_QK_DIM - ROPE_DIM
    c = jnp.concatenate([cos, cos, jnp.ones((n, rest), F32)], axis=1)
    m = jnp.concatenate([-sin, jnp.zeros((n, half + rest), F32)], axis=1)
    p = jnp.concatenate([jnp.zeros((n, half), F32), sin, jnp.zeros((n, rest), F32)], axis=1)
    rep = LANES // B_QK_DIM
    return jnp.tile(c, (1, rep)), jnp.tile(m, (1, rep)), jnp.tile(p, (1, rep))


def _routing_tables(top_i, rank, counts):
    n = top_i.shape[0]
    n_tiles = n * TOP_K // ROW_TILE + N_EXPERTS
    padded = (counts + ROW_TILE - 1) // ROW_TILE * ROW_TILE
    ends = jnp.cumsum(padded)
    starts = ends - padded
    onehot = top_i[..., None] == jnp.arange(N_EXPERTS, dtype=jnp.int32)
    pos = jnp.sum(jnp.where(onehot, starts, 0), axis=-1) + rank
    tile_row = jnp.arange(n_tiles, dtype=jnp.int32) * ROW_TILE
    live = tile_row < ends[-1]
    last_live = jnp.maximum(ends[-1] // ROW_TILE - 1, 0)
    tile_blk = jnp.minimum(jnp.arange(n_tiles, dtype=jnp.int32), last_live)
    tile_e = jnp.sum(tile_blk[:, None] * ROW_TILE >= ends[None, :], axis=1).astype(jnp.int32)
    tile_e = jnp.minimum(tile_e, N_EXPERTS - 1)
    prev_e = jnp.concatenate([jnp.full((1,), -1, jnp.int32), tile_e[:-1]])
    first = jnp.logical_and(live, tile_e != prev_e)
    return (pos.reshape(-1).astype(jnp.int32), tile_e, first.astype(jnp.int32),
            live.astype(jnp.int32), tile_blk, n_tiles * ROW_TILE)


def kernel(x, c, positions, w_ada, b_ada, g_mix_norm, w_in, rel_bias, lambda_q1, lambda_k1,
           lambda_q2, lambda_k2, g_subln, w_out, g_ffn_norm, w_router, b_router, w_gate_up,
           b_gate_up, w_down, b_down, g_final):
    batch, seq, _ = x.shape
    depth = w_ada.shape[0]
    assert depth == 1, "the combine kernel applies the final norm, so it must follow the only layer"
    n = batch * seq
    rope_c, rope_m, rope_p = _rope_tables(positions)
    x2 = x.reshape(n, D_MODEL)
    for l in range(depth):
        lambda_init = 0.8 - 0.6 * math.exp(-0.3 * l)
        mod = _ada(c, w_ada[l], b_ada[l])
        qa, ka, va, qb, kb, vb = _in_proj(x2, mod, g_mix_norm[l], w_in[l].astype(BF16),
                                          rope_c, rope_m, rope_p, seq)
        oa = _attn_a(qa, ka, va, _rel_bias_rows(rel_bias[l]), batch, seq)
        lam = (jnp.exp(jnp.sum(lambda_q1[l].astype(F32) * lambda_k1[l].astype(F32)))
               - jnp.exp(jnp.sum(lambda_q2[l].astype(F32) * lambda_k2[l].astype(F32)))
               + lambda_init).reshape(1)
        ob = _attn_b(lam, qb, kb, vb, g_subln[l], batch, seq, 1.0 - lambda_init)
        x1, hp, top_i, top_w, rank, counts = _out_route(
            oa, ob, x2, mod, w_out[l].astype(BF16), g_ffn_norm[l], w_router[l].astype(BF16),
            b_router[l], seq)
        pos, tile_e, tile_first, tile_live, tile_blk, n_rows = _routing_tables(
            top_i, rank, counts[0])
        xs = _dispatch(pos, hp, n_rows)
        y = _experts(tile_e, tile_first, tile_live, tile_blk, xs, w_gate_up[l], b_gate_up[l],
                     w_down[l], b_down[l])
        x2 = _combine(pos, top_w, x1, mod, g_final, y, seq)
    return x2.reshape(batch, seq, D_MODEL)
```

```python
import functools
import math

import jax
import jax.numpy as jnp
from jax import lax
from jax.experimental import pallas as pl
from jax.experimental.pallas import tpu as pltpu
from jax.experimental.pallas import tpu_sc as plsc

D_MODEL = 1024
CHUNK = 64
HEAD_DIM = 64
A_HEADS = 8
A_WIDTH = A_HEADS * HEAD_DIM
LEFT_CHUNKS = 8
MAX_REL = 128
B_HEADS = 4
B_QK_DIM = HEAD_DIM
B_V_DIM = 2 * HEAD_DIM
B_WIDTH = B_HEADS * B_V_DIM
ROPE_THETA = 500000.0
ROPE_DIM = B_QK_DIM // 4
N_EXPERTS = 32
TOP_K = 4
D_EXPERT = D_MODEL
SWIGLU_LIMIT = 7.0
SWIGLU_ALPHA = 1.702
EPS = 1e-6
NEG_INF = -1e30
LOG2_E = math.log2(math.e)
N_MOD = 6

LANES = 128
SUBLANES = 8
VMEM_BYTES_V7X = 64 * 1024 * 1024
VMEM_LIMIT = VMEM_BYTES_V7X * 7 // 8

TOK_TILE = 512
A_QBLK = 2 * CHUNK
A_BAND = (LEFT_CHUNKS + 2) * CHUNK
A_ROLL = A_BAND + A_QBLK
B_TQ = 512
B_TK = 512
B_SUB = 128
ROW_TILE = 512
FF_CHUNK = 512
COMB_TILE = 256
COMB_UNROLL = 4
SC_WINDOW = 128

F32 = jnp.float32
BF16 = jnp.bfloat16


def _cparams(n_axes, vmem=None):
    return pltpu.CompilerParams(
        dimension_semantics=("arbitrary",) * n_axes,
        vmem_limit_bytes=vmem,
    )


def _ada_kernel(c_ref, w_ref, b_ref, o_ref):
    c = c_ref[...]
    act = c * jax.nn.sigmoid(c)
    o_ref[...] = jnp.dot(act, w_ref[...], preferred_element_type=F32,
                         precision=lax.Precision.HIGHEST) + b_ref[...]


def _ada(c, w_ada, b_ada):
    b = c.shape[0]
    rows = -(-b // SUBLANES) * SUBLANES
    c_pad = jnp.pad(c, ((0, rows - b), (0, 0)))
    n_out = w_ada.shape[1]
    out = pl.pallas_call(
        _ada_kernel,
        grid=(n_out // D_MODEL,),
        in_specs=[
            pl.BlockSpec((rows, D_MODEL), lambda j: (0, 0)),
            pl.BlockSpec((D_MODEL, D_MODEL), lambda j: (0, j)),
            pl.BlockSpec((1, D_MODEL), lambda j: (0, j)),
        ],
        out_specs=pl.BlockSpec((rows, D_MODEL), lambda j: (0, j)),
        out_shape=jax.ShapeDtypeStruct((rows, n_out), F32),
        compiler_params=_cparams(1),
        name="ada",
    )(c_pad, w_ada, b_ada.reshape(1, n_out))
    mod = out[:b].reshape(b, N_MOD, D_MODEL)
    return jnp.pad(mod, ((0, 0), (0, SUBLANES - N_MOD), (0, 0)))


def _in_proj_kernel(x_ref, mod_ref, g_ref, w_ref, rc_ref, rm_ref, rp_ref,
                    qa_ref, ka_ref, va_ref, qb_ref, kb_ref, vb_ref):
    x = x_ref[...]
    mod = mod_ref[...]
    y = x * lax.rsqrt(jnp.mean(x * x, axis=-1, keepdims=True) + EPS) * g_ref[...]
    h = (y * (1.0 + mod[1:2, :]) + mod[0:1, :]).astype(BF16)
    rc, rm, rp = rc_ref[...], rm_ref[...], rp_ref[...]
    q_scale = HEAD_DIM ** -0.5

    def rope(p):
        cols = []
        for s in range(p.shape[1] // LANES):
            v = p[:, s * LANES:(s + 1) * LANES]
            cols.append(v * rc + pltpu.roll(v, LANES - ROPE_DIM // 2, 1) * rm
                        + pltpu.roll(v, ROPE_DIM // 2, 1) * rp)
        return jnp.concatenate(cols, axis=1)

    outs = (qa_ref, ka_ref, va_ref, qb_ref, kb_ref, vb_ref)
    for j, o_ref in enumerate(outs):
        p = jnp.dot(h, w_ref[:, j * A_WIDTH:(j + 1) * A_WIDTH], preferred_element_type=F32)
        if j in (3, 4):
            p = rope(p)
        if j in (0, 3):
            p = p * (q_scale * LOG2_E)
        pb = p.astype(BF16)
        for s in range(A_WIDTH // LANES):
            o_ref[s] = pb[:, s * LANES:(s + 1) * LANES]


def _in_proj(x2, mod, g_mix, w_in_bf, rope_c, rope_m, rope_p, seq):
    n = x2.shape[0]
    tiles_per_seq = seq // TOK_TILE
    row = lambda i: (i, 0)
    fixed = lambda i: (0, 0)
    n_slabs = A_WIDTH // LANES
    out_sd = jax.ShapeDtypeStruct((n_slabs, n, LANES), BF16)
    return pl.pallas_call(
        _in_proj_kernel,
        grid=(n // TOK_TILE,),
        in_specs=[
            pl.BlockSpec((TOK_TILE, D_MODEL), row),
            pl.BlockSpec((None, SUBLANES, D_MODEL), lambda i: (i // tiles_per_seq, 0, 0)),
            pl.BlockSpec((1, D_MODEL), fixed),
            pl.BlockSpec(w_in_bf.shape, fixed),
            pl.BlockSpec((TOK_TILE, LANES), row),
            pl.BlockSpec((TOK_TILE, LANES), row),
            pl.BlockSpec((TOK_TILE, LANES), row),
        ],
        out_specs=[pl.BlockSpec((n_slabs, TOK_TILE, LANES), lambda i: (0, i, 0))] * 6,
        out_shape=[out_sd] * 6,
        compiler_params=_cparams(1, VMEM_LIMIT),
        name="in_proj",
    )(x2, mod, g_mix.reshape(1, D_MODEL), w_in_bf, rope_c, rope_m, rope_p)


def _attn_a_kernel(q_ref, kp_ref, kc_ref, vp_ref, vc_ref, bias_ref, o_ref, k_sc, v_sc):
    g = pl.program_id(1)
    n_pairs, blk, _ = q_ref.shape
    lane = lax.broadcasted_iota(jnp.int32, (A_QBLK, LANES), 1)
    col = lax.broadcasted_iota(jnp.int32, (A_QBLK, A_BAND), 1)
    q_chunk = lax.broadcasted_iota(jnp.int32, (A_QBLK, A_BAND), 0) // CHUNK
    k_chunk = col // CHUNK
    in_band = jnp.logical_and(k_chunk >= q_chunk, k_chunk <= q_chunk + LEFT_CHUNKS)
    ones = jnp.ones((A_BAND, LANES), BF16)

    def pair(p, carry):
        k_sc[0:blk, :] = kp_ref[p]
        k_sc[blk:2 * blk, :] = kc_ref[p]
        v_sc[0:blk, :] = vp_ref[p]
        v_sc[blk:2 * blk, :] = vc_ref[p]
        biases = []
        for hh in range(2):
            rolled = pltpu.roll(jnp.broadcast_to(bias_ref[2 * p + hh], (A_QBLK, A_ROLL)), 0, 1,
                                stride=1, stride_axis=0)
            biases.append(jnp.where(in_band, rolled[:, :A_BAND], NEG_INF))
        for m in range(blk // A_QBLK):
            r0 = m * A_QBLK
            q = q_ref[p, r0:r0 + A_QBLK, :]
            kband = k_sc[r0:r0 + A_BAND, :]
            v_ext = jnp.concatenate([v_sc[r0:r0 + A_BAND, :], ones], axis=1)
            valid = jnp.logical_or(g > 0, col + r0 >= blk)
            halves = []
            for hh in range(2):
                in_head = (lane < HEAD_DIM) if hh == 0 else (lane >= HEAD_DIM)
                qh = jnp.where(in_head, q, jnp.zeros_like(q))
                s = lax.dot_general(qh, kband, (((1,), (1,)), ((), ())),
                                    preferred_element_type=F32)
                s = jnp.where(valid, s + biases[hh], NEG_INF)
                pr = jnp.exp2(s - jnp.max(s, axis=1, keepdims=True))
                pv = jnp.dot(pr.astype(BF16), v_ext, preferred_element_type=F32)
                halves.append(pv[:, :LANES] / pv[:, LANES:])
            o_ref[p, r0:r0 + A_QBLK, :] = jnp.where(lane < HEAD_DIM, halves[0],
                                                    halves[1]).astype(BF16)
        return carry

    lax.fori_loop(0, n_pairs, pair, 0)


def _attn_a(qa, ka, va, bias_rows, batch, seq):
    n_pairs, n, _ = qa.shape
    blk = LEFT_CHUNKS * CHUNK
    nblk = seq // blk
    cur = lambda b, g: (0, b * nblk + g, 0)
    prev = lambda b, g: (0, b * nblk + jnp.maximum(g - 1, 0), 0)
    slab = (n_pairs, blk, LANES)
    return pl.pallas_call(
        _attn_a_kernel,
        grid=(batch, nblk),
        in_specs=[
            pl.BlockSpec(slab, cur),
            pl.BlockSpec(slab, prev),
            pl.BlockSpec(slab, cur),
            pl.BlockSpec(slab, prev),
            pl.BlockSpec(slab, cur),
            pl.BlockSpec(bias_rows.shape, lambda b, g: (0, 0, 0)),
        ],
        out_specs=pl.BlockSpec(slab, cur),
        out_shape=jax.ShapeDtypeStruct((n_pairs, n, LANES), BF16),
        scratch_shapes=[pltpu.VMEM((2 * blk, LANES), BF16), pltpu.VMEM((2 * blk, LANES), BF16)],
        compiler_params=_cparams(2),
        name="attn_a",
    )(qa, ka, ka, va, va, bias_rows)


def _rel_bias_rows(rel_table):
    t = rel_table.astype(F32) * LOG2_E
    far = t[:, 2 * MAX_REL:]
    n_far = LEFT_CHUNKS * CHUNK - MAX_REL
    row = jnp.concatenate([
        jnp.broadcast_to(far, (t.shape[0], n_far)),
        t[:, 2 * MAX_REL:0:-1],
        jnp.broadcast_to(far, (t.shape[0], A_ROLL - A_BAND)),
    ], axis=1)
    return row.reshape(t.shape[0], 1, A_ROLL)


def _attn_b_kernel(lam_ref, q_ref, k_ref, v_ref, g_ref, o_ref, q_sc, m_sc, acc_sc, *, out_scale):
    qi = pl.program_id(2)
    n_sub = B_TQ // B_SUB
    lane = lax.broadcasted_iota(jnp.int32, (B_TQ, LANES), 1)
    q = q_ref[...]
    q_sc[0] = jnp.where(lane < B_QK_DIM, q, jnp.zeros_like(q))
    q_sc[1] = jnp.where(lane >= B_QK_DIM, q, jnp.zeros_like(q))
    m_sc[...] = jnp.full(m_sc.shape, NEG_INF, F32)
    acc_sc[...] = jnp.zeros(acc_sc.shape, F32)
    ones = jnp.ones((B_TK, LANES), BF16)

    def update(sub, c, k, v_ext, keep):
        rows = pl.ds(sub * B_SUB, B_SUB)
        s = lax.dot_general(q_sc[c, rows, :], k, (((1,), (1,)), ((), ())),
                            preferred_element_type=F32)
        if keep is not None:
            s = jnp.where(keep, s, NEG_INF)
        m_prev = m_sc[c, rows, :]
        m_new = jnp.maximum(m_prev, jnp.max(s, axis=1, keepdims=True))
        alpha = jnp.exp2(m_prev - m_new)
        p = jnp.exp2(s - jnp.concatenate([m_new] * (s.shape[1] // LANES), axis=1))
        pv = jnp.dot(p.astype(BF16), v_ext, preferred_element_type=F32)
        acc_sc[c, rows, :] = jnp.concatenate([alpha, alpha], axis=1) * acc_sc[c, rows, :] + pv
        m_sc[c, rows, :] = m_new

    def full_block(j, carry):
        r0 = pl.multiple_of(j * B_TK, B_TK)
        k = k_ref[pl.ds(r0, B_TK), :]
        v_ext = jnp.concatenate([v_ref[pl.ds(r0, B_TK), :], ones], axis=1)
        for sub in range(n_sub):
            for c in range(2):
                update(sub, c, k, v_ext, None)
        return carry

    lax.fori_loop(0, qi, full_block, 0)

    r0 = pl.multiple_of(qi * B_TK, B_TK)
    for sub in range(n_sub):
        nk = (sub + 1) * B_SUB
        k = k_ref[pl.ds(r0, nk), :]
        v_ext = jnp.concatenate([v_ref[pl.ds(r0, nk), :], ones[:nk]], axis=1)
        row_c = lax.broadcasted_iota(jnp.int32, (B_SUB, nk), 0) // CHUNK + sub * (B_SUB // CHUNK)
        col_c = lax.broadcasted_iota(jnp.int32, (B_SUB, nk), 1) // CHUNK
        keep = col_c <= row_c
        for c in range(2):
            update(sub, c, k, v_ext, keep)

    lam = lam_ref[0]
    a0 = acc_sc[0]
    a1 = acc_sc[1]
    o = a0[:, :B_V_DIM] / a0[:, B_V_DIM:] - lam * (a1[:, :B_V_DIM] / a1[:, B_V_DIM:])
    y = o * lax.rsqrt(jnp.mean(o * o, axis=-1, keepdims=True) + EPS) * g_ref[...]
    o_ref[...] = (y * out_scale).astype(BF16)


def _attn_b(lam, qb, kb, vb, g_subln, batch, seq, out_scale):
    assert B_TQ == B_TK and B_V_DIM == LANES
    n = qb.shape[1]
    nq = seq // B_TQ
    q_map = lambda b, h, qi: (h, b * nq + qi, 0)
    kv_map = lambda b, h, qi: (h, b, 0)
    return pl.pallas_call(
        functools.partial(_attn_b_kernel, out_scale=out_scale),
        grid=(batch, B_HEADS, nq),
        in_specs=[
            pl.BlockSpec(memory_space=pltpu.SMEM),
            pl.BlockSpec((None, B_TQ, LANES), q_map),
            pl.BlockSpec((None, seq, LANES), kv_map),
            pl.BlockSpec((None, seq, LANES), kv_map),
            pl.BlockSpec((1, B_V_DIM), lambda b, h, qi: (0, 0)),
        ],
        out_specs=pl.BlockSpec((None, B_TQ, LANES), q_map),
        out_shape=jax.ShapeDtypeStruct((B_HEADS, n, LANES), BF16),
        scratch_shapes=[
            pltpu.VMEM((2, B_TQ, LANES), BF16),
            pltpu.VMEM((2, B_TQ, LANES), F32),
            pltpu.VMEM((2, B_TQ, 2 * LANES), F32),
        ],
        compiler_params=_cparams(3),
        name="attn_b",
    )(lam, qb, kb, vb, g_subln.reshape(1, B_V_DIM))


def _pack_bf16_pairs(v):
    half = v.shape[1] // 2
    vb = v.astype(BF16)
    hi = lax.bitcast_convert_type(vb[:, :half].astype(F32), jnp.int32)
    lo = lax.bitcast_convert_type(vb[:, half:].astype(F32), jnp.int32)
    return hi | lax.shift_right_logical(lo, jnp.full(lo.shape, 16, jnp.int32))


def _unpack_bf16_pairs(w):
    first = lax.bitcast_convert_type(w & jnp.int32(-65536), F32)
    second = lax.bitcast_convert_type(lax.shift_left(w, jnp.full(w.shape, 16, jnp.int32)), F32)
    return first, second


def _out_route_kernel(oa_ref, ob_ref, x_ref, mod_ref, wo_ref, g_ref, wr_ref, br_ref,
                      x1_ref, hp_ref, ti_ref, tw_ref, rk_ref, cnt_ref, tri_sc, carry_sc):
    i = pl.program_id(0)
    tm = x_ref.shape[0]

    @pl.when(i == 0)
    def _():
        r = lax.broadcasted_iota(jnp.int32, (tm, tm), 0)
        c = lax.broadcasted_iota(jnp.int32, (tm, tm), 1)
        tri_sc[...] = jnp.where(c < r, 1.0, 0.0).astype(BF16)
        carry_sc[...] = jnp.zeros(carry_sc.shape, F32)

    mod = mod_ref[...]
    o = jnp.concatenate([oa_ref[s] for s in range(oa_ref.shape[0])]
                        + [ob_ref[s] for s in range(ob_ref.shape[0])], axis=1)
    mix = jnp.dot(o, wo_ref[...], preferred_element_type=F32)
    x1 = x_ref[...] + mod[2:3, :] * mix
    x1_ref[...] = x1
    y = x1 * lax.rsqrt(jnp.mean(x1 * x1, axis=-1, keepdims=True) + EPS) * g_ref[...]
    h = y * (1.0 + mod[4:5, :]) + mod[3:4, :]
    hb = h.astype(BF16)
    hp_ref[...] = _pack_bf16_pairs(h)

    logits = jnp.dot(hb, wr_ref[...], preferred_element_type=F32) + br_ref[...]
    eid = lax.broadcasted_iota(jnp.int32, logits.shape, 1).astype(F32)
    work = logits
    vals, ids = [], []
    chosen = jnp.zeros(logits.shape, F32)
    for _ in range(TOP_K):
        v = jnp.max(work, axis=-1, keepdims=True)
        e = jnp.min(jnp.where(work == v, eid, float(N_EXPERTS)), axis=-1, keepdims=True)
        hit = eid == e
        vals.append(v)
        ids.append(e)
        chosen = jnp.where(hit, 1.0, chosen)
        work = jnp.where(hit, -jnp.inf, work)
    ex = [jnp.exp(v - vals[0]) for v in vals]
    den = ex[0] + ex[1] + ex[2] + ex[3]

    before = jnp.dot(tri_sc[...], chosen.astype(BF16), preferred_element_type=F32) + carry_sc[...]
    slot = lax.broadcasted_iota(jnp.int32, (tm, TOP_K), 1)
    ti = jnp.zeros((tm, TOP_K), F32)
    tw = jnp.zeros((tm, TOP_K), F32)
    rk = jnp.zeros((tm, TOP_K), F32)
    for kk in range(TOP_K):
        r_k = jnp.sum(jnp.where(eid == ids[kk], before, 0.0), axis=-1, keepdims=True)
        ti = jnp.where(slot == kk, ids[kk], ti)
        tw = jnp.where(slot == kk, ex[kk] / den, tw)
        rk = jnp.where(slot == kk, r_k, rk)
    ti_ref[...] = ti.astype(jnp.int32)
    tw_ref[...] = tw
    rk_ref[...] = rk.astype(jnp.int32)
    carry = carry_sc[...] + jnp.sum(chosen, axis=0, keepdims=True)
    carry_sc[...] = carry
    cnt_ref[...] = carry.astype(jnp.int32)


def _out_route(oa, ob, x2, mod, w_out_bf, g_ffn, w_router_bf, b_router, seq):
    n = x2.shape[0]
    tiles_per_seq = seq // TOK_TILE
    row = lambda i: (i, 0)
    fixed = lambda i: (0, 0)
    return pl.pallas_call(
        _out_route_kernel,
        grid=(n // TOK_TILE,),
        in_specs=[
            pl.BlockSpec((oa.shape[0], TOK_TILE, LANES), lambda i: (0, i, 0)),
            pl.BlockSpec((ob.shape[0], TOK_TILE, LANES), lambda i: (0, i, 0)),
            pl.BlockSpec((TOK_TILE, D_MODEL), row),
            pl.BlockSpec((None, SUBLANES, D_MODEL), lambda i: (i // tiles_per_seq, 0, 0)),
            pl.BlockSpec((D_MODEL, D_MODEL), fixed),
            pl.BlockSpec((1, D_MODEL), fixed),
            pl.BlockSpec((D_MODEL, N_EXPERTS), fixed),
            pl.BlockSpec((1, N_EXPERTS), fixed),
        ],
        out_specs=[
            pl.BlockSpec((TOK_TILE, D_MODEL), row),
            pl.BlockSpec((TOK_TILE, D_MODEL // 2), row),
            pl.BlockSpec((TOK_TILE, TOP_K), row),
            pl.BlockSpec((TOK_TILE, TOP_K), row),
            pl.BlockSpec((TOK_TILE, TOP_K), row),
            pl.BlockSpec((1, N_EXPERTS), fixed),
        ],
        out_shape=[
            jax.ShapeDtypeStruct((n, D_MODEL), F32),
            jax.ShapeDtypeStruct((n, D_MODEL // 2), jnp.int32),
            jax.ShapeDtypeStruct((n, TOP_K), jnp.int32),
            jax.ShapeDtypeStruct((n, TOP_K), F32),
            jax.ShapeDtypeStruct((n, TOP_K), jnp.int32),
            jax.ShapeDtypeStruct((1, N_EXPERTS), jnp.int32),
        ],
        scratch_shapes=[pltpu.VMEM((TOK_TILE, TOK_TILE), BF16), pltpu.VMEM((1, N_EXPERTS), F32)],
        compiler_params=_cparams(1, VMEM_LIMIT),
        name="out_route",
    )(oa, ob, x2, mod, w_out_bf, g_ffn.reshape(1, D_MODEL), w_router_bf,
      b_router.reshape(1, N_EXPERTS))


def _dispatch_kernel(pos_ref, hp_ref, xs_in_ref, xs_ref, sem):
    del xs_in_ref
    i = pl.program_id(0)
    tm = hp_ref.shape[0]
    base = i * tm * TOP_K

    def issue(t, carry):
        for kk in range(TOP_K):
            r = pos_ref[base + t * TOP_K + kk]
            pltpu.make_async_copy(hp_ref.at[pl.ds(t, 1), :], xs_ref.at[pl.ds(r, 1), :], sem).start()
        return carry

    lax.fori_loop(0, tm, issue, 0)
    for _ in range(TOP_K):
        pltpu.make_async_copy(hp_ref, xs_ref.at[pl.ds(0, tm), :], sem).wait()


def _dispatch(pos, hp, n_rows):
    n, width = hp.shape
    xs_init = jnp.zeros((n_rows, width), jnp.int32)
    return pl.pallas_call(
        _dispatch_kernel,
        grid_spec=pltpu.PrefetchScalarGridSpec(
            num_scalar_prefetch=1,
            grid=(n // TOK_TILE,),
            in_specs=[
                pl.BlockSpec((TOK_TILE, width), lambda i, pos: (i, 0)),
                pl.BlockSpec(memory_space=pl.ANY),
            ],
            out_specs=pl.BlockSpec(memory_space=pl.ANY),
            scratch_shapes=[pltpu.SemaphoreType.DMA(())],
        ),
        out_shape=jax.ShapeDtypeStruct((n_rows, width), jnp.int32),
        input_output_aliases={2: 0},
        compiler_params=_cparams(1),
        name="dispatch",
    )(pos, hp, xs_init)


def _experts_kernel(te_ref, first_ref, live_ref, blk_ref, xs_ref, wgu_ref, bgu_ref, wd_ref,
                    bd_ref, y_ref, wgu_sc, wd_sc):
    del te_ref, blk_ref
    j = pl.program_id(0)

    @pl.when(live_ref[j] == 1)
    def _():
        @pl.when(first_ref[j] == 1)
        def _():
            wgu_sc[...] = wgu_ref[...].astype(BF16)
            wd_sc[...] = wd_ref[...].astype(BF16)

        first, second = _unpack_bf16_pairs(xs_ref[...])
        x = jnp.concatenate([first.astype(BF16), second.astype(BF16)], axis=1)
        acc = jnp.zeros((x.shape[0], D_MODEL), F32)
        for c in range(D_EXPERT // FF_CHUNK):
            lo_c, hi_c = c * FF_CHUNK, (c + 1) * FF_CHUNK
            gate = jnp.dot(x, wgu_sc[:, lo_c:hi_c], preferred_element_type=F32) + bgu_ref[:, lo_c:hi_c]
            up = (jnp.dot(x, wgu_sc[:, D_EXPERT + lo_c:D_EXPERT + hi_c], preferred_element_type=F32)
                  + bgu_ref[:, D_EXPERT + lo_c:D_EXPERT + hi_c])
            gate = jnp.minimum(gate, SWIGLU_LIMIT)
            up = jnp.clip(up, -SWIGLU_LIMIT, SWIGLU_LIMIT)
            act = (up + 1.0) * (gate * jax.nn.sigmoid(SWIGLU_ALPHA * gate))
            acc = acc + jnp.dot(act.astype(BF16), wd_sc[lo_c:hi_c, :], preferred_element_type=F32)
        y_ref[...] = _pack_bf16_pairs(acc + bd_ref[...])

    @pl.when(live_ref[j] == 0)
    def _():
        y_ref[...] = jnp.zeros(y_ref.shape, jnp.int32)


def _experts(tile_e, tile_first, tile_live, tile_blk, xs, w_gate_up, b_gate_up, w_down, b_down):
    n_rows, width = xs.shape
    n_tiles = n_rows // ROW_TILE
    by_tile = lambda j, te, fi, li, tb: (tb[j], 0)
    by_expert = lambda j, te, fi, li, tb: (te[j], 0, 0)
    return pl.pallas_call(
        _experts_kernel,
        grid_spec=pltpu.PrefetchScalarGridSpec(
            num_scalar_prefetch=4,
            grid=(n_tiles,),
            in_specs=[
                pl.BlockSpec((ROW_TILE, width), by_tile),
                pl.BlockSpec((None, D_MODEL, 2 * D_EXPERT), by_expert),
                pl.BlockSpec((None, 1, 2 * D_EXPERT), by_expert),
                pl.BlockSpec((None, D_EXPERT, D_MODEL), by_expert),
                pl.BlockSpec((None, 1, D_MODEL), by_expert),
            ],
            out_specs=pl.BlockSpec((ROW_TILE, D_MODEL // 2), lambda j, te, fi, li, tb: (j, 0)),
            scratch_shapes=[
                pltpu.VMEM((D_MODEL, 2 * D_EXPERT), BF16),
                pltpu.VMEM((D_EXPERT, D_MODEL), BF16),
            ],
        ),
        out_shape=jax.ShapeDtypeStruct((n_rows, D_MODEL // 2), jnp.int32),
        compiler_params=_cparams(1, VMEM_LIMIT),
        name="experts",
    )(tile_e, tile_first, tile_live, tile_blk, xs, w_gate_up,
      b_gate_up.reshape(N_EXPERTS, 1, 2 * D_EXPERT), w_down, b_down.reshape(N_EXPERTS, 1, D_MODEL))


def _combine_kernel(pos_ref, tw_ref, x1_ref, mod_ref, g_ref, y_ref, o_ref, ybuf, sems):
    i = pl.program_id(0)
    tm = x1_ref.shape[0]
    width = y_ref.shape[1]

    def issue_tile(tile, slot):
        def body(g, carry):
            for u in range(COMB_UNROLL):
                t = g * COMB_UNROLL + u
                for kk in range(TOP_K):
                    r = pos_ref[(tile * tm + t) * TOP_K + kk]
                    pltpu.make_async_copy(
                        y_ref.at[pl.ds(r, 1), :],
                        ybuf.at[slot, pl.ds(t, 1), pl.ds(kk * width, width)],
                        sems.at[slot]).start()
            return carry

        lax.fori_loop(0, tm // COMB_UNROLL, body, 0)

    @pl.when(i == 0)
    def _():
        issue_tile(0, 0)

    @pl.when(i + 1 < pl.num_programs(0))
    def _():
        issue_tile(i + 1, (i + 1) % 2)

    slot = i % 2
    for kk in range(TOP_K):
        pltpu.make_async_copy(y_ref.at[pl.ds(0, tm), :],
                              ybuf.at[slot, :, pl.ds(kk * width, width)], sems.at[slot]).wait()

    tw = tw_ref[...]
    first = jnp.zeros((tm, width), F32)
    second = jnp.zeros((tm, width), F32)
    for kk in range(TOP_K):
        f_k, s_k = _unpack_bf16_pairs(ybuf[slot, :, kk * width:(kk + 1) * width])
        first = first + tw[:, kk:kk + 1] * f_k
        second = second + tw[:, kk:kk + 1] * s_k
    ffn = jnp.concatenate([first, second], axis=1)
    x2 = x1_ref[...] + mod_ref[5:6, :] * ffn
    o_ref[...] = x2 * lax.rsqrt(jnp.mean(x2 * x2, axis=-1, keepdims=True) + EPS) * g_ref[...]


def _combine(pos, tw, x1, mod, g_final, y, seq):
    n = x1.shape[0]
    tiles_per_seq = seq // COMB_TILE
    row = lambda i, pos: (i, 0)
    return pl.pallas_call(
        _combine_kernel,
        grid_spec=pltpu.PrefetchScalarGridSpec(
            num_scalar_prefetch=1,
            grid=(n // COMB_TILE,),
            in_specs=[
                pl.BlockSpec((COMB_TILE, TOP_K), row),
                pl.BlockSpec((COMB_TILE, D_MODEL), row),
                pl.BlockSpec((None, SUBLANES, D_MODEL), lambda i, pos: (i // tiles_per_seq, 0, 0)),
                pl.BlockSpec((1, D_MODEL), lambda i, pos: (0, 0)),
                pl.BlockSpec(memory_space=pl.ANY),
            ],
            out_specs=pl.BlockSpec((COMB_TILE, D_MODEL), row),
            scratch_shapes=[
                pltpu.VMEM((2, COMB_TILE, TOP_K * y.shape[1]), jnp.int32),
                pltpu.SemaphoreType.DMA((2,)),
            ],
        ),
        out_shape=jax.ShapeDtypeStruct((n, D_MODEL), F32),
        compiler_params=_cparams(1, VMEM_LIMIT),
        name="combine",
    )(pos, tw, x1, mod, g_final.reshape(1, D_MODEL), y)


def _sc_gather_rows(table, idx):
    m = idx.shape[0]
    width = table.shape[1]
    mesh = plsc.VectorSubcoreMesh(core_axis_name="core", subcore_axis_name="subcore")
    n_workers = mesh.num_cores * mesh.num_subcores
    per_worker = m // n_workers
    assert per_worker * n_workers == m and per_worker % SC_WINDOW == 0

    @pl.kernel(
        out_type=jax.ShapeDtypeStruct((m, width), table.dtype),
        mesh=mesh,
        scratch_types=[
            pltpu.VMEM((SC_WINDOW,), jnp.int32),
            pltpu.VMEM((SC_WINDOW, width), table.dtype),
            pltpu.SemaphoreType.DMA,
        ],
    )
    def gather_kernel(table_hbm, idx_hbm, out_hbm, idx_v, rows_v, sem):
        worker = lax.axis_index("subcore") * mesh.num_cores + lax.axis_index("core")

        @pl.loop(0, per_worker // SC_WINDOW)
        def _(j):
            base = pl.multiple_of(worker * per_worker + j * SC_WINDOW, SC_WINDOW)
            pltpu.sync_copy(idx_hbm.at[pl.ds(base, SC_WINDOW)], idx_v)
            pltpu.async_copy(table_hbm.at[idx_v], rows_v, sem).wait()
            pltpu.sync_copy(rows_v, out_hbm.at[pl.ds(base, SC_WINDOW)])

    return gather_kernel(table, idx)


def _combine_dense_kernel(tw_ref, x1_ref, mod_ref, g_ref, y0_ref, y1_ref, y2_ref, y3_ref, o_ref):
    tw = tw_ref[...]
    first = second = None
    for kk, y_ref in enumerate((y0_ref, y1_ref, y2_ref, y3_ref)):
        f_k, s_k = _unpack_bf16_pairs(y_ref[...])
        w_k = tw[:, kk:kk + 1]
        first = w_k * f_k if kk == 0 else first + w_k * f_k
        second = w_k * s_k if kk == 0 else second + w_k * s_k
    ffn = jnp.concatenate([first, second], axis=1)
    x2 = x1_ref[...] + mod_ref[5:6, :] * ffn
    o_ref[...] = x2 * lax.rsqrt(jnp.mean(x2 * x2, axis=-1, keepdims=True) + EPS) * g_ref[...]


def _combine_dense(tw, x1, mod, g_final, yg, seq):
    n = x1.shape[0]
    width = yg.shape[1]
    tiles = n // TOK_TILE
    tiles_per_seq = seq // TOK_TILE
    row = lambda i: (i, 0)
    slot = lambda kk: pl.BlockSpec((TOK_TILE, width), lambda i: (kk * tiles + i, 0))
    return pl.pallas_call(
        _combine_dense_kernel,
        grid=(tiles,),
        in_specs=[
            pl.BlockSpec((TOK_TILE, TOP_K), row),
            pl.BlockSpec((TOK_TILE, D_MODEL), row),
            pl.BlockSpec((None, SUBLANES, D_MODEL), lambda i: (i // tiles_per_seq, 0, 0)),
            pl.BlockSpec((1, D_MODEL), lambda i: (0, 0)),
        ] + [slot(kk) for kk in range(TOP_K)],
        out_specs=pl.BlockSpec((TOK_TILE, D_MODEL), row),
        out_shape=jax.ShapeDtypeStruct((n, D_MODEL), F32),
        compiler_params=_cparams(1, VMEM_LIMIT),
        name="combine",
    )(tw, x1, mod, g_final.reshape(1, D_MODEL), yg, yg, yg, yg)


def _rope_tables(positions):
    half = ROPE_DIM // 2
    inv_freq = ROPE_THETA ** (-jnp.arange(0, ROPE_DIM, 2, dtype=F32) / ROPE_DIM)
    ang = positions.reshape(-1).astype(F32)[:, None] * inv_freq
    cos, sin = jnp.cos(ang), jnp.sin(ang)
    n = ang.shape[0]
    rest = B_QK_DIM - ROPE_DIM
    c = jnp.concatenate([cos, cos, jnp.ones((n, rest), F32)], axis=1)
    m = jnp.concatenate([-sin, jnp.zeros((n, half + rest), F32)], axis=1)
    p = jnp.concatenate([jnp.zeros((n, half), F32), sin, jnp.zeros((n, rest), F32)], axis=1)
    rep = LANES // B_QK_DIM
    return jnp.tile(c, (1, rep)), jnp.tile(m, (1, rep)), jnp.tile(p, (1, rep))


def _routing_tables(top_i, rank, counts):
    n = top_i.shape[0]
    n_tiles = n * TOP_K // ROW_TILE + N_EXPERTS
    padded = (counts + ROW_TILE - 1) // ROW_TILE * ROW_TILE
    ends = jnp.cumsum(padded)
    starts = ends - padded
    onehot = top_i[..., None] == jnp.arange(N_EXPERTS, dtype=jnp.int32)
    pos = jnp.sum(jnp.where(onehot, starts, 0), axis=-1) + rank
    tile_row = jnp.arange(n_tiles, dtype=jnp.int32) * ROW_TILE
    live = tile_row < ends[-1]
    last_live = jnp.maximum(ends[-1] // ROW_TILE - 1, 0)
    tile_blk = jnp.minimum(jnp.arange(n_tiles, dtype=jnp.int32), last_live)
    tile_e = jnp.sum(tile_blk[:, None] * ROW_TILE >= ends[None, :], axis=1).astype(jnp.int32)
    tile_e = jnp.minimum(tile_e, N_EXPERTS - 1)
    prev_e = jnp.concatenate([jnp.full((1,), -1, jnp.int32), tile_e[:-1]])
    first = jnp.logical_and(live, tile_e != prev_e)
    return (pos.reshape(-1).astype(jnp.int32), tile_e, first.astype(jnp.int32),
            live.astype(jnp.int32), tile_blk, n_tiles * ROW_TILE)


def kernel(x, c, positions, w_ada, b_ada, g_mix_norm, w_in, rel_bias, lambda_q1, lambda_k1,
           lambda_q2, lambda_k2, g_subln, w_out, g_ffn_norm, w_router, b_router, w_gate_up,
           b_gate_up, w_down, b_down, g_final):
    batch, seq, _ = x.shape
    depth = w_ada.shape[0]
    assert depth == 1, "the combine kernel applies the final norm, so it must follow the only layer"
    n = batch * seq
    rope_c, rope_m, rope_p = _rope_tables(positions)
    x2 = x.reshape(n, D_MODEL)
    for l in range(depth):
        lambda_init = 0.8 - 0.6 * math.exp(-0.3 * l)
        mod = _ada(c, w_ada[l], b_ada[l])
        qa, ka, va, qb, kb, vb = _in_proj(x2, mod, g_mix_norm[l], w_in[l].astype(BF16),
                                          rope_c, rope_m, rope_p, seq)
        oa = _attn_a(qa, ka, va, _rel_bias_rows(rel_bias[l]), batch, seq)
        lam = (jnp.exp(jnp.sum(lambda_q1[l].astype(F32) * lambda_k1[l].astype(F32)))
               - jnp.exp(jnp.sum(lambda_q2[l].astype(F32) * lambda_k2[l].astype(F32)))
               + lambda_init).reshape(1)
        ob = _attn_b(lam, qb, kb, vb, g_subln[l], batch, seq, 1.0 - lambda_init)
        x1, hp, top_i, top_w, rank, counts = _out_route(
            oa, ob, x2, mod, w_out[l].astype(BF16), g_ffn_norm[l], w_router[l].astype(BF16),
            b_router[l], seq)
        pos, tile_e, tile_first, tile_live, tile_blk, n_rows = _routing_tables(
            top_i, rank, counts[0])
        xs = _dispatch(pos, hp, n_rows)
        y = _experts(tile_e, tile_first, tile_live, tile_blk, xs, w_gate_up[l], b_gate_up[l],
                     w_down[l], b_down[l])
        pos_slot_major = pos.reshape(n, TOP_K).T.reshape(-1)
        yg = _sc_gather_rows(y, pos_slot_major)
        x2 = _combine_dense(top_w, x1, mod, g_final, yg, seq)
    return x2.reshape(batch, seq, D_MODEL)
```

```python
import functools
import math

import jax
import jax.numpy as jnp
from jax import lax
from jax.experimental import pallas as pl
from jax.experimental.pallas import tpu as pltpu
from jax.experimental.pallas import tpu_sc as plsc

D_MODEL = 1024
CHUNK = 64
HEAD_DIM = 64
A_HEADS = 8
A_WIDTH = A_HEADS * HEAD_DIM
LEFT_CHUNKS = 8
MAX_REL = 128
B_HEADS = 4
B_QK_DIM = HEAD_DIM
B_V_DIM = 2 * HEAD_DIM
B_WIDTH = B_HEADS * B_V_DIM
ROPE_THETA = 500000.0
ROPE_DIM = B_QK_DIM // 4
N_EXPERTS = 32
TOP_K = 4
D_EXPERT = D_MODEL
SWIGLU_LIMIT = 7.0
SWIGLU_ALPHA = 1.702
EPS = 1e-6
NEG_INF = -1e30
LOG2_E = math.log2(math.e)
N_MOD = 6

LANES = 128
SUBLANES = 8
VMEM_BYTES_V7X = 64 * 1024 * 1024
VMEM_LIMIT = VMEM_BYTES_V7X * 7 // 8

TOK_TILE = 512
A_QBLK = 2 * CHUNK
A_BAND = (LEFT_CHUNKS + 2) * CHUNK
A_ROLL = A_BAND + A_QBLK
B_TQ = 512
B_TK = 512
B_SUB = 128
ROW_TILE = 512
FF_CHUNK = 512
COMB_TILE = 256
COMB_UNROLL = 4
SC_WINDOW = 128

F32 = jnp.float32
BF16 = jnp.bfloat16


def _cparams(n_axes, vmem=None):
    return pltpu.CompilerParams(
        dimension_semantics=("arbitrary",) * n_axes,
        vmem_limit_bytes=vmem,
    )


def _ada_kernel(c_ref, w_ref, b_ref, o_ref):
    c = c_ref[...]
    act = c * jax.nn.sigmoid(c)
    o_ref[...] = jnp.dot(act, w_ref[...], preferred_element_type=F32,
                         precision=lax.Precision.HIGHEST) + b_ref[...]


def _ada(c, w_ada, b_ada):
    b = c.shape[0]
    rows = -(-b // SUBLANES) * SUBLANES
    c_pad = jnp.pad(c, ((0, rows - b), (0, 0)))
    n_out = w_ada.shape[1]
    out = pl.pallas_call(
        _ada_kernel,
        grid=(n_out // D_MODEL,),
        in_specs=[
            pl.BlockSpec((rows, D_MODEL), lambda j: (0, 0)),
            pl.BlockSpec((D_MODEL, D_MODEL), lambda j: (0, j)),
            pl.BlockSpec((1, D_MODEL), lambda j: (0, j)),
        ],
        out_specs=pl.BlockSpec((rows, D_MODEL), lambda j: (0, j)),
        out_shape=jax.ShapeDtypeStruct((rows, n_out), F32),
        compiler_params=_cparams(1),
        name="ada",
    )(c_pad, w_ada, b_ada.reshape(1, n_out))
    mod = out[:b].reshape(b, N_MOD, D_MODEL)
    return jnp.pad(mod, ((0, 0), (0, SUBLANES - N_MOD), (0, 0)))


def _in_proj_kernel(x_ref, mod_ref, g_ref, w_ref, rc_ref, rm_ref, rp_ref,
                    qa_ref, ka_ref, va_ref, qb_ref, kb_ref, vb_ref):
    x = x_ref[...]
    mod = mod_ref[...]
    y = x * lax.rsqrt(jnp.mean(x * x, axis=-1, keepdims=True) + EPS) * g_ref[...]
    h = (y * (1.0 + mod[1:2, :]) + mod[0:1, :]).astype(BF16)
    rc, rm, rp = rc_ref[...], rm_ref[...], rp_ref[...]
    q_scale = HEAD_DIM ** -0.5

    def rope(p):
        cols = []
        for s in range(p.shape[1] // LANES):
            v = p[:, s * LANES:(s + 1) * LANES]
            cols.append(v * rc + pltpu.roll(v, LANES - ROPE_DIM // 2, 1) * rm
                        + pltpu.roll(v, ROPE_DIM // 2, 1) * rp)
        return jnp.concatenate(cols, axis=1)

    outs = (qa_ref, ka_ref, va_ref, qb_ref, kb_ref, vb_ref)
    for j, o_ref in enumerate(outs):
        p = jnp.dot(h, w_ref[:, j * A_WIDTH:(j + 1) * A_WIDTH], preferred_element_type=F32)
        if j in (3, 4):
            p = rope(p)
        if j in (0, 3):
            p = p * (q_scale * LOG2_E)
        pb = p.astype(BF16)
        for s in range(A_WIDTH // LANES):
            o_ref[s] = pb[:, s * LANES:(s + 1) * LANES]


def _in_proj(x2, mod, g_mix, w_in_bf, rope_c, rope_m, rope_p, seq):
    n = x2.shape[0]
    tiles_per_seq = seq // TOK_TILE
    row = lambda i: (i, 0)
    fixed = lambda i: (0, 0)
    n_slabs = A_WIDTH // LANES
    out_sd = jax.ShapeDtypeStruct((n_slabs, n, LANES), BF16)
    return pl.pallas_call(
        _in_proj_kernel,
        grid=(n // TOK_TILE,),
        in_specs=[
            pl.BlockSpec((TOK_TILE, D_MODEL), row),
            pl.BlockSpec((None, SUBLANES, D_MODEL), lambda i: (i // tiles_per_seq, 0, 0)),
            pl.BlockSpec((1, D_MODEL), fixed),
            pl.BlockSpec(w_in_bf.shape, fixed),
            pl.BlockSpec((TOK_TILE, LANES), row),
            pl.BlockSpec((TOK_TILE, LANES), row),
            pl.BlockSpec((TOK_TILE, LANES), row),
        ],
        out_specs=[pl.BlockSpec((n_slabs, TOK_TILE, LANES), lambda i: (0, i, 0))] * 6,
        out_shape=[out_sd] * 6,
        compiler_params=_cparams(1, VMEM_LIMIT),
        name="in_proj",
    )(x2, mod, g_mix.reshape(1, D_MODEL), w_in_bf, rope_c, rope_m, rope_p)


def _attn_a_kernel(q_ref, kp_ref, kc_ref, vp_ref, vc_ref, bias_ref, o_ref, k_sc, v_sc):
    g = pl.program_id(1)
    n_pairs, blk, _ = q_ref.shape
    lane = lax.broadcasted_iota(jnp.int32, (A_QBLK, LANES), 1)
    col = lax.broadcasted_iota(jnp.int32, (A_QBLK, A_BAND), 1)
    q_chunk = lax.broadcasted_iota(jnp.int32, (A_QBLK, A_BAND), 0) // CHUNK
    k_chunk = col // CHUNK
    in_band = jnp.logical_and(k_chunk >= q_chunk, k_chunk <= q_chunk + LEFT_CHUNKS)
    ones = jnp.ones((A_BAND, LANES), BF16)

    def pair(p, carry):
        k_sc[0:blk, :] = kp_ref[p]
        k_sc[blk:2 * blk, :] = kc_ref[p]
        v_sc[0:blk, :] = vp_ref[p]
        v_sc[blk:2 * blk, :] = vc_ref[p]
        biases = []
        for hh in range(2):
            rolled = pltpu.roll(jnp.broadcast_to(bias_ref[2 * p + hh], (A_QBLK, A_ROLL)), 0, 1,
                                stride=1, stride_axis=0)
            biases.append(jnp.where(in_band, rolled[:, :A_BAND], NEG_INF))
        for m in range(blk // A_QBLK):
            r0 = m * A_QBLK
            q = q_ref[p, r0:r0 + A_QBLK, :]
            kband = k_sc[r0:r0 + A_BAND, :]
            v_ext = jnp.concatenate([v_sc[r0:r0 + A_BAND, :], ones], axis=1)
            valid = jnp.logical_or(g > 0, col + r0 >= blk)
            halves = []
            for hh in range(2):
                in_head = (lane < HEAD_DIM) if hh == 0 else (lane >= HEAD_DIM)
                qh = jnp.where(in_head, q, jnp.zeros_like(q))
                s = lax.dot_general(qh, kband, (((1,), (1,)), ((), ())),
                                    preferred_element_type=F32)
                s = jnp.where(valid, s + biases[hh], NEG_INF)
                pr = jnp.exp2(s - jnp.max(s, axis=1, keepdims=True))
                pv = jnp.dot(pr.astype(BF16), v_ext, preferred_element_type=F32)
                halves.append(pv[:, :LANES] / pv[:, LANES:])
            o_ref[p, r0:r0 + A_QBLK, :] = jnp.where(lane < HEAD_DIM, halves[0],
                                                    halves[1]).astype(BF16)
        return carry

    lax.fori_loop(0, n_pairs, pair, 0)


def _attn_a(qa, ka, va, bias_rows, batch, seq):
    n_pairs, n, _ = qa.shape
    blk = LEFT_CHUNKS * CHUNK
    nblk = seq // blk
    cur = lambda b, g: (0, b * nblk + g, 0)
    prev = lambda b, g: (0, b * nblk + jnp.maximum(g - 1, 0), 0)
    slab = (n_pairs, blk, LANES)
    return pl.pallas_call(
        _attn_a_kernel,
        grid=(batch, nblk),
        in_specs=[
            pl.BlockSpec(slab, cur),
            pl.BlockSpec(slab, prev),
            pl.BlockSpec(slab, cur),
            pl.BlockSpec(slab, prev),
            pl.BlockSpec(slab, cur),
            pl.BlockSpec(bias_rows.shape, lambda b, g: (0, 0, 0)),
        ],
        out_specs=pl.BlockSpec(slab, cur),
        out_shape=jax.ShapeDtypeStruct((n_pairs, n, LANES), BF16),
        scratch_shapes=[pltpu.VMEM((2 * blk, LANES), BF16), pltpu.VMEM((2 * blk, LANES), BF16)],
        compiler_params=_cparams(2),
        name="attn_a",
    )(qa, ka, ka, va, va, bias_rows)


def _rel_bias_rows(rel_table):
    t = rel_table.astype(F32) * LOG2_E
    far = t[:, 2 * MAX_REL:]
    n_far = LEFT_CHUNKS * CHUNK - MAX_REL
    row = jnp.concatenate([
        jnp.broadcast_to(far, (t.shape[0], n_far)),
        t[:, 2 * MAX_REL:0:-1],
        jnp.broadcast_to(far, (t.shape[0], A_ROLL - A_BAND)),
    ], axis=1)
    return row.reshape(t.shape[0], 1, A_ROLL)


def _attn_b_kernel(lam_ref, q_ref, k_ref, v_ref, g_ref, o_ref, q_sc, m_sc, acc_sc, *, out_scale):
    qi = pl.program_id(2)
    n_sub = B_TQ // B_SUB
    lane = lax.broadcasted_iota(jnp.int32, (B_TQ, LANES), 1)
    q = q_ref[...]
    q_sc[0] = jnp.where(lane < B_QK_DIM, q, jnp.zeros_like(q))
    q_sc[1] = jnp.where(lane >= B_QK_DIM, q, jnp.zeros_like(q))
    m_sc[...] = jnp.full(m_sc.shape, NEG_INF, F32)
    acc_sc[...] = jnp.zeros(acc_sc.shape, F32)
    ones = jnp.ones((B_TK, LANES), BF16)

    def update(sub, c, k, v_ext, keep):
        rows = pl.ds(sub * B_SUB, B_SUB)
        s = lax.dot_general(q_sc[c, rows, :], k, (((1,), (1,)), ((), ())),
                            preferred_element_type=F32)
        if keep is not None:
            s = jnp.where(keep, s, NEG_INF)
        m_prev = m_sc[c, rows, :]
        m_new = jnp.maximum(m_prev, jnp.max(s, axis=1, keepdims=True))
        alpha = jnp.exp2(m_prev - m_new)
        p = jnp.exp2(s - jnp.concatenate([m_new] * (s.shape[1] // LANES), axis=1))
        pv = jnp.dot(p.astype(BF16), v_ext, preferred_element_type=F32)
        acc_sc[c, rows, :] = jnp.concatenate([alpha, alpha], axis=1) * acc_sc[c, rows, :] + pv
        m_sc[c, rows, :] = m_new

    def full_block(j, carry):
        r0 = pl.multiple_of(j * B_TK, B_TK)
        k = k_ref[pl.ds(r0, B_TK), :]
        v_ext = jnp.concatenate([v_ref[pl.ds(r0, B_TK), :], ones], axis=1)
        for sub in range(n_sub):
            for c in range(2):
                update(sub, c, k, v_ext, None)
        return carry

    lax.fori_loop(0, qi, full_block, 0)

    r0 = pl.multiple_of(qi * B_TK, B_TK)
    for sub in range(n_sub):
        nk = (sub + 1) * B_SUB
        k = k_ref[pl.ds(r0, nk), :]
        v_ext = jnp.concatenate([v_ref[pl.ds(r0, nk), :], ones[:nk]], axis=1)
        row_c = lax.broadcasted_iota(jnp.int32, (B_SUB, nk), 0) // CHUNK + sub * (B_SUB // CHUNK)
        col_c = lax.broadcasted_iota(jnp.int32, (B_SUB, nk), 1) // CHUNK
        keep = col_c <= row_c
        for c in range(2):
            update(sub, c, k, v_ext, keep)

    lam = lam_ref[0]
    a0 = acc_sc[0]
    a1 = acc_sc[1]
    o = a0[:, :B_V_DIM] / a0[:, B_V_DIM:] - lam * (a1[:, :B_V_DIM] / a1[:, B_V_DIM:])
    y = o * lax.rsqrt(jnp.mean(o * o, axis=-1, keepdims=True) + EPS) * g_ref[...]
    o_ref[...] = (y * out_scale).astype(BF16)


def _attn_b(lam, qb, kb, vb, g_subln, batch, seq, out_scale):
    assert B_TQ == B_TK and B_V_DIM == LANES
    n = qb.shape[1]
    nq = seq // B_TQ
    q_map = lambda b, h, qi: (h, b * nq + qi, 0)
    kv_map = lambda b, h, qi: (h, b, 0)
    return pl.pallas_call(
        functools.partial(_attn_b_kernel, out_scale=out_scale),
        grid=(batch, B_HEADS, nq),
        in_specs=[
            pl.BlockSpec(memory_space=pltpu.SMEM),
            pl.BlockSpec((None, B_TQ, LANES), q_map),
            pl.BlockSpec((None, seq, LANES), kv_map),
            pl.BlockSpec((None, seq, LANES), kv_map),
            pl.BlockSpec((1, B_V_DIM), lambda b, h, qi: (0, 0)),
        ],
        out_specs=pl.BlockSpec((None, B_TQ, LANES), q_map),
        out_shape=jax.ShapeDtypeStruct((B_HEADS, n, LANES), BF16),
        scratch_shapes=[
            pltpu.VMEM((2, B_TQ, LANES), BF16),
            pltpu.VMEM((2, B_TQ, LANES), F32),
            pltpu.VMEM((2, B_TQ, 2 * LANES), F32),
        ],
        compiler_params=_cparams(3),
        name="attn_b",
    )(lam, qb, kb, vb, g_subln.reshape(1, B_V_DIM))


def _pack_bf16_pairs(v):
    half = v.shape[1] // 2
    vb = v.astype(BF16)
    hi = lax.bitcast_convert_type(vb[:, :half].astype(F32), jnp.int32)
    lo = lax.bitcast_convert_type(vb[:, half:].astype(F32), jnp.int32)
    return hi | lax.shift_right_logical(lo, jnp.full(lo.shape, 16, jnp.int32))


def _unpack_bf16_pairs(w):
    first = lax.bitcast_convert_type(w & jnp.int32(-65536), F32)
    second = lax.bitcast_convert_type(lax.shift_left(w, jnp.full(w.shape, 16, jnp.int32)), F32)
    return first, second


def _out_route_kernel(oa_ref, ob_ref, x_ref, mod_ref, wo_ref, g_ref, wr_ref, br_ref,
                      x1_ref, hp_ref, ti_ref, tw_ref, rk_ref, cnt_ref, tri_sc, carry_sc):
    i = pl.program_id(0)
    tm = x_ref.shape[0]

    @pl.when(i == 0)
    def _():
        r = lax.broadcasted_iota(jnp.int32, (tm, tm), 0)
        c = lax.broadcasted_iota(jnp.int32, (tm, tm), 1)
        tri_sc[...] = jnp.where(c < r, 1.0, 0.0).astype(BF16)
        carry_sc[...] = jnp.zeros(carry_sc.shape, F32)

    mod = mod_ref[...]
    o = jnp.concatenate([oa_ref[s] for s in range(oa_ref.shape[0])]
                        + [ob_ref[s] for s in range(ob_ref.shape[0])], axis=1)
    mix = jnp.dot(o, wo_ref[...], preferred_element_type=F32)
    x1 = x_ref[...] + mod[2:3, :] * mix
    x1_ref[...] = x1
    y = x1 * lax.rsqrt(jnp.mean(x1 * x1, axis=-1, keepdims=True) + EPS) * g_ref[...]
    h = y * (1.0 + mod[4:5, :]) + mod[3:4, :]
    hb = h.astype(BF16)
    hp_ref[...] = _pack_bf16_pairs(h)

    logits = jnp.dot(hb, wr_ref[...], preferred_element_type=F32) + br_ref[...]
    eid = lax.broadcasted_iota(jnp.int32, logits.shape, 1).astype(F32)
    work = logits
    vals, ids = [], []
    chosen = jnp.zeros(logits.shape, F32)
    for _ in range(TOP_K):
        v = jnp.max(work, axis=-1, keepdims=True)
        e = jnp.min(jnp.where(work == v, eid, float(N_EXPERTS)), axis=-1, keepdims=True)
        hit = eid == e
        vals.append(v)
        ids.append(e)
        chosen = jnp.where(hit, 1.0, chosen)
        work = jnp.where(hit, -jnp.inf, work)
    ex = [jnp.exp(v - vals[0]) for v in vals]
    den = ex[0] + ex[1] + ex[2] + ex[3]

    before = jnp.dot(tri_sc[...], chosen.astype(BF16), preferred_element_type=F32) + carry_sc[...]
    slot = lax.broadcasted_iota(jnp.int32, (tm, TOP_K), 1)
    ti = jnp.zeros((tm, TOP_K), F32)
    tw = jnp.zeros((tm, TOP_K), F32)
    rk = jnp.zeros((tm, TOP_K), F32)
    for kk in range(TOP_K):
        r_k = jnp.sum(jnp.where(eid == ids[kk], before, 0.0), axis=-1, keepdims=True)
        ti = jnp.where(slot == kk, ids[kk], ti)
        tw = jnp.where(slot == kk, ex[kk] / den, tw)
        rk = jnp.where(slot == kk, r_k, rk)
    ti_ref[...] = ti.astype(jnp.int32)
    tw_ref[...] = tw
    rk_ref[...] = rk.astype(jnp.int32)
    carry = carry_sc[...] + jnp.sum(chosen, axis=0, keepdims=True)
    carry_sc[...] = carry
    cnt_ref[...] = carry.astype(jnp.int32)


def _out_route(oa, ob, x2, mod, w_out_bf, g_ffn, w_router_bf, b_router, seq):
    n = x2.shape[0]
    tiles_per_seq = seq // TOK_TILE
    row = lambda i: (i, 0)
    fixed = lambda i: (0, 0)
    return pl.pallas_call(
        _out_route_kernel,
        grid=(n // TOK_TILE,),
        in_specs=[
            pl.BlockSpec((oa.shape[0], TOK_TILE, LANES), lambda i: (0, i, 0)),
            pl.BlockSpec((ob.shape[0], TOK_TILE, LANES), lambda i: (0, i, 0)),
            pl.BlockSpec((TOK_TILE, D_MODEL), row),
            pl.BlockSpec((None, SUBLANES, D_MODEL), lambda i: (i // tiles_per_seq, 0, 0)),
            pl.BlockSpec((D_MODEL, D_MODEL), fixed),
            pl.BlockSpec((1, D_MODEL), fixed),
            pl.BlockSpec((D_MODEL, N_EXPERTS), fixed),
            pl.BlockSpec((1, N_EXPERTS), fixed),
        ],
        out_specs=[
            pl.BlockSpec((TOK_TILE, D_MODEL), row),
            pl.BlockSpec((TOK_TILE, D_MODEL // 2), row),
            pl.BlockSpec((TOK_TILE, TOP_K), row),
            pl.BlockSpec((TOK_TILE, TOP_K), row),
            pl.BlockSpec((TOK_TILE, TOP_K), row),
            pl.BlockSpec((1, N_EXPERTS), fixed),
        ],
        out_shape=[
            jax.ShapeDtypeStruct((n, D_MODEL), F32),
            jax.ShapeDtypeStruct((n, D_MODEL // 2), jnp.int32),
            jax.ShapeDtypeStruct((n, TOP_K), jnp.int32),
            jax.ShapeDtypeStruct((n, TOP_K), F32),
            jax.ShapeDtypeStruct((n, TOP_K), jnp.int32),
            jax.ShapeDtypeStruct((1, N_EXPERTS), jnp.int32),
        ],
        scratch_shapes=[pltpu.VMEM((TOK_TILE, TOK_TILE), BF16), pltpu.VMEM((1, N_EXPERTS), F32)],
        compiler_params=_cparams(1, VMEM_LIMIT),
        name="out_route",
    )(oa, ob, x2, mod, w_out_bf, g_ffn.reshape(1, D_MODEL), w_router_bf,
      b_router.reshape(1, N_EXPERTS))


def _dispatch_kernel(pos_ref, hp_ref, xs_in_ref, xs_ref, sem):
    del xs_in_ref
    i = pl.program_id(0)
    tm = hp_ref.shape[0]
    base = i * tm * TOP_K

    def issue(t, carry):
        for kk in range(TOP_K):
            r = pos_ref[base + t * TOP_K + kk]
            pltpu.make_async_copy(hp_ref.at[pl.ds(t, 1), :], xs_ref.at[pl.ds(r, 1), :], sem).start()
        return carry

    lax.fori_loop(0, tm, issue, 0)
    for _ in range(TOP_K):
        pltpu.make_async_copy(hp_ref, xs_ref.at[pl.ds(0, tm), :], sem).wait()


def _dispatch(pos, hp, n_rows):
    n, width = hp.shape
    xs_init = jnp.zeros((n_rows, width), jnp.int32)
    return pl.pallas_call(
        _dispatch_kernel,
        grid_spec=pltpu.PrefetchScalarGridSpec(
            num_scalar_prefetch=1,
            grid=(n // TOK_TILE,),
            in_specs=[
                pl.BlockSpec((TOK_TILE, width), lambda i, pos: (i, 0)),
                pl.BlockSpec(memory_space=pl.ANY),
            ],
            out_specs=pl.BlockSpec(memory_space=pl.ANY),
            scratch_shapes=[pltpu.SemaphoreType.DMA(())],
        ),
        out_shape=jax.ShapeDtypeStruct((n_rows, width), jnp.int32),
        input_output_aliases={2: 0},
        compiler_params=_cparams(1),
        name="dispatch",
    )(pos, hp, xs_init)


def _experts_kernel(te_ref, first_ref, live_ref, blk_ref, xs_ref, wgu_ref, bgu_ref, wd_ref,
                    bd_ref, y_ref, wgu_sc, wd_sc):
    del te_ref, blk_ref
    j = pl.program_id(0)

    @pl.when(live_ref[j] == 1)
    def _():
        @pl.when(first_ref[j] == 1)
        def _():
            wgu_sc[...] = wgu_ref[...].astype(BF16)
            wd_sc[...] = wd_ref[...].astype(BF16)

        first, second = _unpack_bf16_pairs(xs_ref[...])
        x = jnp.concatenate([first.astype(BF16), second.astype(BF16)], axis=1)
        acc = jnp.zeros((x.shape[0], D_MODEL), F32)
        for c in range(D_EXPERT // FF_CHUNK):
            lo_c, hi_c = c * FF_CHUNK, (c + 1) * FF_CHUNK
            gate = jnp.dot(x, wgu_sc[:, lo_c:hi_c], preferred_element_type=F32) + bgu_ref[:, lo_c:hi_c]
            up = (jnp.dot(x, wgu_sc[:, D_EXPERT + lo_c:D_EXPERT + hi_c], preferred_element_type=F32)
                  + bgu_ref[:, D_EXPERT + lo_c:D_EXPERT + hi_c])
            gate = jnp.minimum(gate, SWIGLU_LIMIT)
            up = jnp.clip(up, -SWIGLU_LIMIT, SWIGLU_LIMIT)
            act = (up + 1.0) * (gate * jax.nn.sigmoid(SWIGLU_ALPHA * gate))
            acc = acc + jnp.dot(act.astype(BF16), wd_sc[lo_c:hi_c, :], preferred_element_type=F32)
        y_ref[...] = _pack_bf16_pairs(acc + bd_ref[...])

    @pl.when(live_ref[j] == 0)
    def _():
        y_ref[...] = jnp.zeros(y_ref.shape, jnp.int32)


def _experts(tile_e, tile_first, tile_live, tile_blk, xs, w_gate_up, b_gate_up, w_down, b_down):
    n_rows, width = xs.shape
    n_tiles = n_rows // ROW_TILE
    by_tile = lambda j, te, fi, li, tb: (tb[j], 0)
    by_expert = lambda j, te, fi, li, tb: (te[j], 0, 0)
    return pl.pallas_call(
        _experts_kernel,
        grid_spec=pltpu.PrefetchScalarGridSpec(
            num_scalar_prefetch=4,
            grid=(n_tiles,),
            in_specs=[
                pl.BlockSpec((ROW_TILE, width), by_tile),
                pl.BlockSpec((None, D_MODEL, 2 * D_EXPERT), by_expert),
                pl.BlockSpec((None, 1, 2 * D_EXPERT), by_expert),
                pl.BlockSpec((None, D_EXPERT, D_MODEL), by_expert),
                pl.BlockSpec((None, 1, D_MODEL), by_expert),
            ],
            out_specs=pl.BlockSpec((ROW_TILE, D_MODEL // 2), lambda j, te, fi, li, tb: (j, 0)),
            scratch_shapes=[
                pltpu.VMEM((D_MODEL, 2 * D_EXPERT), BF16),
                pltpu.VMEM((D_EXPERT, D_MODEL), BF16),
            ],
        ),
        out_shape=jax.ShapeDtypeStruct((n_rows, D_MODEL // 2), jnp.int32),
        compiler_params=_cparams(1, VMEM_LIMIT),
        name="experts",
    )(tile_e, tile_first, tile_live, tile_blk, xs, w_gate_up,
      b_gate_up.reshape(N_EXPERTS, 1, 2 * D_EXPERT), w_down, b_down.reshape(N_EXPERTS, 1, D_MODEL))


def _combine_kernel(pos_ref, tw_ref, x1_ref, mod_ref, g_ref, y_ref, o_ref, ybuf, sems):
    i = pl.program_id(0)
    tm = x1_ref.shape[0]
    width = y_ref.shape[1]

    def issue_tile(tile, slot):
        def body(g, carry):
            for u in range(COMB_UNROLL):
                t = g * COMB_UNROLL + u
                for kk in range(TOP_K):
                    r = pos_ref[(tile * tm + t) * TOP_K + kk]
                    pltpu.make_async_copy(
                        y_ref.at[pl.ds(r, 1), :],
                        ybuf.at[slot, pl.ds(t, 1), pl.ds(kk * width, width)],
                        sems.at[slot]).start()
            return carry

        lax.fori_loop(0, tm // COMB_UNROLL, body, 0)

    @pl.when(i == 0)
    def _():
        issue_tile(0, 0)

    @pl.when(i + 1 < pl.num_programs(0))
    def _():
        issue_tile(i + 1, (i + 1) % 2)

    slot = i % 2
    for kk in range(TOP_K):
        pltpu.make_async_copy(y_ref.at[pl.ds(0, tm), :],
                              ybuf.at[slot, :, pl.ds(kk * width, width)], sems.at[slot]).wait()

    tw = tw_ref[...]
    first = jnp.zeros((tm, width), F32)
    second = jnp.zeros((tm, width), F32)
    for kk in range(TOP_K):
        f_k, s_k = _unpack_bf16_pairs(ybuf[slot, :, kk * width:(kk + 1) * width])
        first = first + tw[:, kk:kk + 1] * f_k
        second = second + tw[:, kk:kk + 1] * s_k
    ffn = jnp.concatenate([first, second], axis=1)
    x2 = x1_ref[...] + mod_ref[5:6, :] * ffn
    o_ref[...] = x2 * lax.rsqrt(jnp.mean(x2 * x2, axis=-1, keepdims=True) + EPS) * g_ref[...]


def _combine(pos, tw, x1, mod, g_final, y, seq):
    n = x1.shape[0]
    tiles_per_seq = seq // COMB_TILE
    row = lambda i, pos: (i, 0)
    return pl.pallas_call(
        _combine_kernel,
        grid_spec=pltpu.PrefetchScalarGridSpec(
            num_scalar_prefetch=1,
            grid=(n // COMB_TILE,),
            in_specs=[
                pl.BlockSpec((COMB_TILE, TOP_K), row),
                pl.BlockSpec((COMB_TILE, D_MODEL), row),
                pl.BlockSpec((None, SUBLANES, D_MODEL), lambda i, pos: (i // tiles_per_seq, 0, 0)),
                pl.BlockSpec((1, D_MODEL), lambda i, pos: (0, 0)),
                pl.BlockSpec(memory_space=pl.ANY),
            ],
            out_specs=pl.BlockSpec((COMB_TILE, D_MODEL), row),
            scratch_shapes=[
                pltpu.VMEM((2, COMB_TILE, TOP_K * y.shape[1]), jnp.int32),
                pltpu.SemaphoreType.DMA((2,)),
            ],
        ),
        out_shape=jax.ShapeDtypeStruct((n, D_MODEL), F32),
        compiler_params=_cparams(1, VMEM_LIMIT),
        name="combine",
    )(pos, tw, x1, mod, g_final.reshape(1, D_MODEL), y)


def _sc_gather_rows(table, idx):
    m = idx.shape[0]
    width = table.shape[1]
    mesh = plsc.VectorSubcoreMesh(core_axis_name="core", subcore_axis_name="subcore")
    n_workers = mesh.num_cores * mesh.num_subcores
    per_worker = m // n_workers
    assert per_worker * n_workers == m and per_worker % SC_WINDOW == 0

    @pl.kernel(
        out_type=jax.ShapeDtypeStruct((m, width), table.dtype),
        mesh=mesh,
        scratch_types=[
            pltpu.VMEM((SC_WINDOW,), jnp.int32),
            pltpu.VMEM((SC_WINDOW, width), table.dtype),
            pltpu.SemaphoreType.DMA,
        ],
    )
    def gather_kernel(table_hbm, idx_hbm, out_hbm, idx_v, rows_v, sem):
        worker = lax.axis_index("subcore") * mesh.num_cores + lax.axis_index("core")

        @pl.loop(0, per_worker // SC_WINDOW)
        def _(j):
            base = pl.multiple_of(worker * per_worker + j * SC_WINDOW, SC_WINDOW)
            pltpu.sync_copy(idx_hbm.at[pl.ds(base, SC_WINDOW)], idx_v)
            pltpu.async_copy(table_hbm.at[idx_v], rows_v, sem).wait()
            pltpu.sync_copy(rows_v, out_hbm.at[pl.ds(base, SC_WINDOW)])

    return gather_kernel(table, idx)


def _sc_move_rows(table, src, dst):
    m = src.shape[0]
    width = table.shape[1]
    mesh = plsc.VectorSubcoreMesh(core_axis_name="core", subcore_axis_name="subcore")
    n_workers = mesh.num_cores * mesh.num_subcores
    per_worker = m // n_workers
    assert per_worker * n_workers == m and per_worker % SC_WINDOW == 0

    @pl.kernel(
        out_type=jax.ShapeDtypeStruct((m, width), table.dtype),
        mesh=mesh,
        scratch_types=[
            pltpu.VMEM((SC_WINDOW,), jnp.int32),
            pltpu.VMEM((SC_WINDOW,), jnp.int32),
            pltpu.VMEM((SC_WINDOW, width), table.dtype),
            pltpu.SemaphoreType.DMA,
        ],
    )
    def move_kernel(table_hbm, src_hbm, dst_hbm, out_hbm, src_v, dst_v, rows_v, sem):
        worker = lax.axis_index("subcore") * mesh.num_cores + lax.axis_index("core")

        @pl.loop(0, per_worker // SC_WINDOW)
        def _(j):
            base = pl.multiple_of(worker * per_worker + j * SC_WINDOW, SC_WINDOW)
            pltpu.sync_copy(src_hbm.at[pl.ds(base, SC_WINDOW)], src_v)
            pltpu.sync_copy(dst_hbm.at[pl.ds(base, SC_WINDOW)], dst_v)
            pltpu.async_copy(table_hbm.at[src_v], rows_v, sem).wait()
            pltpu.async_copy(rows_v, out_hbm.at[dst_v], sem).wait()

    return move_kernel(table, src, dst)


def _combine_dense_kernel(tw_ref, x1_ref, mod_ref, g_ref, y0_ref, y1_ref, y2_ref, y3_ref, o_ref):
    tw = tw_ref[...]
    first = second = None
    for kk, y_ref in enumerate((y0_ref, y1_ref, y2_ref, y3_ref)):
        f_k, s_k = _unpack_bf16_pairs(y_ref[...])
        w_k = tw[:, kk:kk + 1]
        first = w_k * f_k if kk == 0 else first + w_k * f_k
        second = w_k * s_k if kk == 0 else second + w_k * s_k
    ffn = jnp.concatenate([first, second], axis=1)
    x2 = x1_ref[...] + mod_ref[5:6, :] * ffn
    o_ref[...] = x2 * lax.rsqrt(jnp.mean(x2 * x2, axis=-1, keepdims=True) + EPS) * g_ref[...]


def _combine_dense(tw, x1, mod, g_final, yg, seq):
    n = x1.shape[0]
    width = yg.shape[1]
    tiles = n // TOK_TILE
    tiles_per_seq = seq // TOK_TILE
    row = lambda i: (i, 0)
    slot = lambda kk: pl.BlockSpec((TOK_TILE, width), lambda i: (kk * tiles + i, 0))
    return pl.pallas_call(
        _combine_dense_kernel,
        grid=(tiles,),
        in_specs=[
            pl.BlockSpec((TOK_TILE, TOP_K), row),
            pl.BlockSpec((TOK_TILE, D_MODEL), row),
            pl.BlockSpec((None, SUBLANES, D_MODEL), lambda i: (i // tiles_per_seq, 0, 0)),
            pl.BlockSpec((1, D_MODEL), lambda i: (0, 0)),
        ] + [slot(kk) for kk in range(TOP_K)],
        out_specs=pl.BlockSpec((TOK_TILE, D_MODEL), row),
        out_shape=jax.ShapeDtypeStruct((n, D_MODEL), F32),
        compiler_params=_cparams(1, VMEM_LIMIT),
        name="combine",
    )(tw, x1, mod, g_final.reshape(1, D_MODEL), yg, yg, yg, yg)


def _rope_tables(positions):
    half = ROPE_DIM // 2
    inv_freq = ROPE_THETA ** (-jnp.arange(0, ROPE_DIM, 2, dtype=F32) / ROPE_DIM)
    ang = positions.reshape(-1).astype(F32)[:, None] * inv_freq
    cos, sin = jnp.cos(ang), jnp.sin(ang)
    n = ang.shape[0]
    rest = B_QK_DIM - ROPE_DIM
    c = jnp.concatenate([cos, cos, jnp.ones((n, rest), F32)], axis=1)
    m = jnp.concatenate([-sin, jnp.zeros((n, half + rest), F32)], axis=1)
    p = jnp.concatenate([jnp.zeros((n, half), F32), sin, jnp.zeros((n, rest), F32)], axis=1)
    rep = LANES // B_QK_DIM
    return jnp.tile(c, (1, rep)), jnp.tile(m, (1, rep)), jnp.tile(p, (1, rep))


def _routing_tables(top_i, rank, counts):
    n = top_i.shape[0]
    n_tiles = n * TOP_K // ROW_TILE + N_EXPERTS
    padded = (counts + ROW_TILE - 1) // ROW_TILE * ROW_TILE
    ends = jnp.cumsum(padded)
    starts = ends - padded
    onehot = top_i[..., None] == jnp.arange(N_EXPERTS, dtype=jnp.int32)
    pos = jnp.sum(jnp.where(onehot, starts, 0), axis=-1) + rank
    tile_row = jnp.arange(n_tiles, dtype=jnp.int32) * ROW_TILE
    live = tile_row < ends[-1]
    last_live = jnp.maximum(ends[-1] // ROW_TILE - 1, 0)
    tile_blk = jnp.minimum(jnp.arange(n_tiles, dtype=jnp.int32), last_live)
    tile_e = jnp.sum(tile_blk[:, None] * ROW_TILE >= ends[None, :], axis=1).astype(jnp.int32)
    tile_e = jnp.minimum(tile_e, N_EXPERTS - 1)
    prev_e = jnp.concatenate([jnp.full((1,), -1, jnp.int32), tile_e[:-1]])
    first = jnp.logical_and(live, tile_e != prev_e)
    n_rows = n_tiles * ROW_TILE
    n_fill = n_rows - n * TOP_K
    gap_start = jnp.concatenate([starts + counts, ends[-1:]])
    gap_len = jnp.concatenate([padded - counts, n_rows - ends[-1:]])
    gap_end = jnp.cumsum(gap_len)
    m = jnp.arange(n_fill, dtype=jnp.int32)
    in_gap = jnp.logical_and(m[:, None] >= (gap_end - gap_len)[None, :], m[:, None] < gap_end[None, :])
    fill_dst = m + jnp.sum(jnp.where(in_gap, (gap_start - (gap_end - gap_len))[None, :], 0), axis=1)
    pos = pos.reshape(-1).astype(jnp.int32)
    move_src = jnp.concatenate([jnp.arange(n * TOP_K, dtype=jnp.int32) // TOP_K, m % n])
    move_dst = jnp.concatenate([pos, fill_dst.astype(jnp.int32)])
    return (pos, move_src, move_dst, tile_e, first.astype(jnp.int32), live.astype(jnp.int32),
            tile_blk)


def kernel(x, c, positions, w_ada, b_ada, g_mix_norm, w_in, rel_bias, lambda_q1, lambda_k1,
           lambda_q2, lambda_k2, g_subln, w_out, g_ffn_norm, w_router, b_router, w_gate_up,
           b_gate_up, w_down, b_down, g_final):
    batch, seq, _ = x.shape
    depth = w_ada.shape[0]
    assert depth == 1, "the combine kernel applies the final norm, so it must follow the only layer"
    n = batch * seq
    rope_c, rope_m, rope_p = _rope_tables(positions)
    x2 = x.reshape(n, D_MODEL)
    for l in range(depth):
        lambda_init = 0.8 - 0.6 * math.exp(-0.3 * l)
        mod = _ada(c, w_ada[l], b_ada[l])
        qa, ka, va, qb, kb, vb = _in_proj(x2, mod, g_mix_norm[l], w_in[l].astype(BF16),
                                          rope_c, rope_m, rope_p, seq)
        oa = _attn_a(qa, ka, va, _rel_bias_rows(rel_bias[l]), batch, seq)
        lam = (jnp.exp(jnp.sum(lambda_q1[l].astype(F32) * lambda_k1[l].astype(F32)))
               - jnp.exp(jnp.sum(lambda_q2[l].astype(F32) * lambda_k2[l].astype(F32)))
               + lambda_init).reshape(1)
        ob = _attn_b(lam, qb, kb, vb, g_subln[l], batch, seq, 1.0 - lambda_init)
        x1, hp, top_i, top_w, rank, counts = _out_route(
            oa, ob, x2, mod, w_out[l].astype(BF16), g_ffn_norm[l], w_router[l].astype(BF16),
            b_router[l], seq)
        pos, move_src, move_dst, tile_e, tile_first, tile_live, tile_blk = _routing_tables(
            top_i, rank, counts[0])
        xs = _sc_move_rows(hp, move_src, move_dst)
        y = _experts(tile_e, tile_first, tile_live, tile_blk, xs, w_gate_up[l], b_gate_up[l],
                     w_down[l], b_down[l])
        pos_slot_major = pos.reshape(n, TOP_K).T.reshape(-1)
        yg = _sc_gather_rows(y, pos_slot_major)
        x2 = _combine_dense(top_w, x1, mod, g_final, yg, seq)
    return x2.reshape(batch, seq, D_MODEL)
```

```python
import functools
import math

import jax
import jax.numpy as jnp
from jax import lax
from jax.experimental import pallas as pl
from jax.experimental.pallas import tpu as pltpu
from jax.experimental.pallas import tpu_sc as plsc

D_MODEL = 1024
CHUNK = 64
HEAD_DIM = 64
A_HEADS = 8
A_WIDTH = A_HEADS * HEAD_DIM
LEFT_CHUNKS = 8
MAX_REL = 128
B_HEADS = 4
B_QK_DIM = HEAD_DIM
B_V_DIM = 2 * HEAD_DIM
B_WIDTH = B_HEADS * B_V_DIM
ROPE_THETA = 500000.0
ROPE_DIM = B_QK_DIM // 4
N_EXPERTS = 32
TOP_K = 4
D_EXPERT = D_MODEL
SWIGLU_LIMIT = 7.0
SWIGLU_ALPHA = 1.702
EPS = 1e-6
NEG_INF = -1e30
LOG2_E = math.log2(math.e)
N_MOD = 6

LANES = 128
SUBLANES = 8
VMEM_BYTES_V7X = 64 * 1024 * 1024
VMEM_LIMIT = VMEM_BYTES_V7X * 7 // 8

TOK_TILE = 512
A_QBLK = 2 * CHUNK
A_BAND = (LEFT_CHUNKS + 2) * CHUNK
A_ROLL = A_BAND + A_QBLK
B_TQ = 512
B_TK = 512
B_SUB = 128
ROW_TILE = 512
FF_CHUNK = 512
SC_WINDOW = 128

F32 = jnp.float32
BF16 = jnp.bfloat16


def _cparams(n_axes, vmem=None):
    return pltpu.CompilerParams(
        dimension_semantics=("arbitrary",) * n_axes,
        vmem_limit_bytes=vmem,
    )


def _ada_kernel(c_ref, w_ref, b_ref, o_ref):
    c = c_ref[...]
    act = c * jax.nn.sigmoid(c)
    o_ref[...] = jnp.dot(act, w_ref[...], preferred_element_type=F32,
                         precision=lax.Precision.HIGHEST) + b_ref[...]


def _ada(c, w_ada, b_ada):
    b = c.shape[0]
    rows = -(-b // SUBLANES) * SUBLANES
    c_pad = jnp.pad(c, ((0, rows - b), (0, 0)))
    n_out = w_ada.shape[1]
    out = pl.pallas_call(
        _ada_kernel,
        grid=(n_out // D_MODEL,),
        in_specs=[
            pl.BlockSpec((rows, D_MODEL), lambda j: (0, 0)),
            pl.BlockSpec((D_MODEL, D_MODEL), lambda j: (0, j)),
            pl.BlockSpec((1, D_MODEL), lambda j: (0, j)),
        ],
        out_specs=pl.BlockSpec((rows, D_MODEL), lambda j: (0, j)),
        out_shape=jax.ShapeDtypeStruct((rows, n_out), F32),
        compiler_params=_cparams(1),
        name="ada",
    )(c_pad, w_ada, b_ada.reshape(1, n_out))
    mod = out[:b].reshape(b, N_MOD, D_MODEL)
    return jnp.pad(mod, ((0, 0), (0, SUBLANES - N_MOD), (0, 0)))


def _in_proj_kernel(x_ref, mod_ref, g_ref, w_ref, rc_ref, rm_ref, rp_ref,
                    qa_ref, ka_ref, va_ref, qb_ref, kb_ref, vb_ref):
    x = x_ref[...]
    mod = mod_ref[...]
    y = x * lax.rsqrt(jnp.mean(x * x, axis=-1, keepdims=True) + EPS) * g_ref[...]
    h = (y * (1.0 + mod[1:2, :]) + mod[0:1, :]).astype(BF16)
    rc, rm, rp = rc_ref[...], rm_ref[...], rp_ref[...]
    q_scale = HEAD_DIM ** -0.5

    def rope(p):
        cols = []
        for s in range(p.shape[1] // LANES):
            v = p[:, s * LANES:(s + 1) * LANES]
            cols.append(v * rc + pltpu.roll(v, LANES - ROPE_DIM // 2, 1) * rm
                        + pltpu.roll(v, ROPE_DIM // 2, 1) * rp)
        return jnp.concatenate(cols, axis=1)

    outs = (qa_ref, ka_ref, va_ref, qb_ref, kb_ref, vb_ref)
    for j, o_ref in enumerate(outs):
        p = jnp.dot(h, w_ref[:, j * A_WIDTH:(j + 1) * A_WIDTH], preferred_element_type=F32)
        if j in (3, 4):
            p = rope(p)
        if j in (0, 3):
            p = p * (q_scale * LOG2_E)
        pb = p.astype(BF16)
        for s in range(A_WIDTH // LANES):
            o_ref[s] = pb[:, s * LANES:(s + 1) * LANES]


def _in_proj(x2, mod, g_mix, w_in_bf, rope_c, rope_m, rope_p, seq):
    n = x2.shape[0]
    tiles_per_seq = seq // TOK_TILE
    row = lambda i: (i, 0)
    fixed = lambda i: (0, 0)
    n_slabs = A_WIDTH // LANES
    out_sd = jax.ShapeDtypeStruct((n_slabs, n, LANES), BF16)
    return pl.pallas_call(
        _in_proj_kernel,
        grid=(n // TOK_TILE,),
        in_specs=[
            pl.BlockSpec((TOK_TILE, D_MODEL), row),
            pl.BlockSpec((None, SUBLANES, D_MODEL), lambda i: (i // tiles_per_seq, 0, 0)),
            pl.BlockSpec((1, D_MODEL), fixed),
            pl.BlockSpec(w_in_bf.shape, fixed),
            pl.BlockSpec((TOK_TILE, LANES), row),
            pl.BlockSpec((TOK_TILE, LANES), row),
            pl.BlockSpec((TOK_TILE, LANES), row),
        ],
        out_specs=[pl.BlockSpec((n_slabs, TOK_TILE, LANES), lambda i: (0, i, 0))] * 6,
        out_shape=[out_sd] * 6,
        compiler_params=_cparams(1, VMEM_LIMIT),
        name="in_proj",
    )(x2, mod, g_mix.reshape(1, D_MODEL), w_in_bf, rope_c, rope_m, rope_p)


def _attn_a_kernel(q_ref, kp_ref, kc_ref, vp_ref, vc_ref, bias_ref, o_ref, k_sc, v_sc):
    g = pl.program_id(1)
    n_pairs, blk, _ = q_ref.shape
    lane = lax.broadcasted_iota(jnp.int32, (A_QBLK, LANES), 1)
    col = lax.broadcasted_iota(jnp.int32, (A_QBLK, A_BAND), 1)
    q_chunk = lax.broadcasted_iota(jnp.int32, (A_QBLK, A_BAND), 0) // CHUNK
    k_chunk = col // CHUNK
    in_band = jnp.logical_and(k_chunk >= q_chunk, k_chunk <= q_chunk + LEFT_CHUNKS)
    ones = jnp.ones((A_BAND, LANES), BF16)

    def pair(p, carry):
        k_sc[0:blk, :] = kp_ref[p]
        k_sc[blk:2 * blk, :] = kc_ref[p]
        v_sc[0:blk, :] = vp_ref[p]
        v_sc[blk:2 * blk, :] = vc_ref[p]
        biases = []
        for hh in range(2):
            rolled = pltpu.roll(jnp.broadcast_to(bias_ref[2 * p + hh], (A_QBLK, A_ROLL)), 0, 1,
                                stride=1, stride_axis=0)
            biases.append(jnp.where(in_band, rolled[:, :A_BAND], NEG_INF))
        for m in range(blk // A_QBLK):
            r0 = m * A_QBLK
            q = q_ref[p, r0:r0 + A_QBLK, :]
            kband = k_sc[r0:r0 + A_BAND, :]
            v_ext = jnp.concatenate([v_sc[r0:r0 + A_BAND, :], ones], axis=1)
            valid = jnp.logical_or(g > 0, col + r0 >= blk)
            halves = []
            for hh in range(2):
                in_head = (lane < HEAD_DIM) if hh == 0 else (lane >= HEAD_DIM)
                qh = jnp.where(in_head, q, jnp.zeros_like(q))
                s = lax.dot_general(qh, kband, (((1,), (1,)), ((), ())),
                                    preferred_element_type=F32)
                s = jnp.where(valid, s + biases[hh], NEG_INF)
                pr = jnp.exp2(s - jnp.max(s, axis=1, keepdims=True))
                pv = jnp.dot(pr.astype(BF16), v_ext, preferred_element_type=F32)
                halves.append(pv[:, :LANES] / pv[:, LANES:])
            o_ref[p, r0:r0 + A_QBLK, :] = jnp.where(lane < HEAD_DIM, halves[0],
                                                    halves[1]).astype(BF16)
        return carry

    lax.fori_loop(0, n_pairs, pair, 0)


def _attn_a(qa, ka, va, bias_rows, batch, seq):
    n_pairs, n, _ = qa.shape
    blk = LEFT_CHUNKS * CHUNK
    nblk = seq // blk
    cur = lambda b, g: (0, b * nblk + g, 0)
    prev = lambda b, g: (0, b * nblk + jnp.maximum(g - 1, 0), 0)
    slab = (n_pairs, blk, LANES)
    return pl.pallas_call(
        _attn_a_kernel,
        grid=(batch, nblk),
        in_specs=[
            pl.BlockSpec(slab, cur),
            pl.BlockSpec(slab, prev),
            pl.BlockSpec(slab, cur),
            pl.BlockSpec(slab, prev),
            pl.BlockSpec(slab, cur),
            pl.BlockSpec(bias_rows.shape, lambda b, g: (0, 0, 0)),
        ],
        out_specs=pl.BlockSpec(slab, cur),
        out_shape=jax.ShapeDtypeStruct((n_pairs, n, LANES), BF16),
        scratch_shapes=[pltpu.VMEM((2 * blk, LANES), BF16), pltpu.VMEM((2 * blk, LANES), BF16)],
        compiler_params=_cparams(2),
        name="attn_a",
    )(qa, ka, ka, va, va, bias_rows)


def _rel_bias_rows(rel_table):
    t = rel_table.astype(F32) * LOG2_E
    far = t[:, 2 * MAX_REL:]
    n_far = LEFT_CHUNKS * CHUNK - MAX_REL
    row = jnp.concatenate([
        jnp.broadcast_to(far, (t.shape[0], n_far)),
        t[:, 2 * MAX_REL:0:-1],
        jnp.broadcast_to(far, (t.shape[0], A_ROLL - A_BAND)),
    ], axis=1)
    return row.reshape(t.shape[0], 1, A_ROLL)


def _attn_b_kernel(lam_ref, q_ref, k_ref, v_ref, g_ref, o_ref, q_sc, s_sc, m_sc, acc_sc, *,
                   out_scale):
    qi = pl.program_id(2)
    n_sub = B_TQ // B_SUB
    lane = lax.broadcasted_iota(jnp.int32, (B_TQ, LANES), 1)
    q = q_ref[...]
    q_sc[0] = jnp.where(lane < B_QK_DIM, q, jnp.zeros_like(q))
    q_sc[1] = jnp.where(lane >= B_QK_DIM, q, jnp.zeros_like(q))
    m_sc[...] = jnp.full(m_sc.shape, NEG_INF, F32)
    acc_sc[...] = jnp.zeros(acc_sc.shape, F32)
    ones = jnp.ones((B_TK, LANES), BF16)

    def score_rows(k, slot, sub, c):
        rows = pl.ds(sub * B_SUB, B_SUB)
        s_sc[slot, c, rows, :] = lax.dot_general(q_sc[c, rows, :], k, (((1,), (1,)), ((), ())),
                                                 preferred_element_type=F32)

    def update_rows(s, v_ext, sub, c):
        rows = pl.ds(sub * B_SUB, B_SUB)
        m_prev = m_sc[c, rows, :]
        m_new = jnp.maximum(m_prev, jnp.max(s, axis=1, keepdims=True))
        alpha = jnp.exp2(m_prev - m_new)
        p = jnp.exp2(s - jnp.concatenate([m_new] * (s.shape[1] // LANES), axis=1))
        pv = jnp.dot(p.astype(BF16), v_ext, preferred_element_type=F32)
        acc_sc[c, rows, :] = jnp.concatenate([alpha, alpha], axis=1) * acc_sc[c, rows, :] + pv
        m_sc[c, rows, :] = m_new

    def key_block(blk):
        return k_ref[pl.ds(pl.multiple_of(blk * B_TK, B_TK), B_TK), :]

    def value_block(blk):
        return jnp.concatenate([v_ref[pl.ds(pl.multiple_of(blk * B_TK, B_TK), B_TK), :], ones], axis=1)

    def step(blk, slot, next_blk):
        v_ext = value_block(blk)
        k_next = key_block(next_blk)
        for sub in range(n_sub):
            for c in range(2):
                update_rows(s_sc[slot, c, pl.ds(sub * B_SUB, B_SUB), :], v_ext, sub, c)
                score_rows(k_next, 1 - slot, sub, c)

    def diagonal_step(slot):
        v_ext = value_block(qi)
        col_c = lax.broadcasted_iota(jnp.int32, (B_SUB, B_TK), 1) // CHUNK
        row_c = lax.broadcasted_iota(jnp.int32, (B_SUB, B_TK), 0) // CHUNK
        for sub in range(n_sub):
            keep = col_c <= row_c + sub * (B_SUB // CHUNK)
            for c in range(2):
                s = s_sc[slot, c, pl.ds(sub * B_SUB, B_SUB), :]
                update_rows(jnp.where(keep, s, NEG_INF), v_ext, sub, c)

    k0 = key_block(0)
    for sub in range(n_sub):
        for c in range(2):
            score_rows(k0, 0, sub, c)

    def pair(p, carry):
        first = 2 * p
        step(first, 0, first + 1)
        step(first + 1, 1, first + 2)
        return carry

    lax.fori_loop(0, qi // 2, pair, 0)

    @pl.when(qi % 2 == 0)
    def _():
        diagonal_step(0)

    @pl.when(qi % 2 == 1)
    def _():
        step(qi - 1, 0, qi)
        diagonal_step(1)

    lam = lam_ref[0]
    a0 = acc_sc[0]
    a1 = acc_sc[1]
    o = a0[:, :B_V_DIM] / a0[:, B_V_DIM:] - lam * (a1[:, :B_V_DIM] / a1[:, B_V_DIM:])
    y = o * lax.rsqrt(jnp.mean(o * o, axis=-1, keepdims=True) + EPS) * g_ref[...]
    o_ref[...] = (y * out_scale).astype(BF16)


def _attn_b(lam, qb, kb, vb, g_subln, batch, seq, out_scale):
    assert B_TQ == B_TK and B_V_DIM == LANES
    n = qb.shape[1]
    nq = seq // B_TQ
    q_map = lambda b, h, qi: (h, b * nq + qi, 0)
    kv_map = lambda b, h, qi: (h, b, 0)
    return pl.pallas_call(
        functools.partial(_attn_b_kernel, out_scale=out_scale),
        grid=(batch, B_HEADS, nq),
        in_specs=[
            pl.BlockSpec(memory_space=pltpu.SMEM),
            pl.BlockSpec((None, B_TQ, LANES), q_map),
            pl.BlockSpec((None, seq, LANES), kv_map),
            pl.BlockSpec((None, seq, LANES), kv_map),
            pl.BlockSpec((1, B_V_DIM), lambda b, h, qi: (0, 0)),
        ],
        out_specs=pl.BlockSpec((None, B_TQ, LANES), q_map),
        out_shape=jax.ShapeDtypeStruct((B_HEADS, n, LANES), BF16),
        scratch_shapes=[
            pltpu.VMEM((2, B_TQ, LANES), BF16),
            pltpu.VMEM((2, 2, B_TQ, B_TK), F32),
            pltpu.VMEM((2, B_TQ, LANES), F32),
            pltpu.VMEM((2, B_TQ, 2 * LANES), F32),
        ],
        compiler_params=_cparams(3),
        name="attn_b",
    )(lam, qb, kb, vb, g_subln.reshape(1, B_V_DIM))


def _pack_bf16_pairs(v):
    half = v.shape[1] // 2
    vb = v.astype(BF16)
    hi = lax.bitcast_convert_type(vb[:, :half].astype(F32), jnp.int32)
    lo = lax.bitcast_convert_type(vb[:, half:].astype(F32), jnp.int32)
    return hi | lax.shift_right_logical(lo, jnp.full(lo.shape, 16, jnp.int32))


def _unpack_bf16_pairs(w):
    first = lax.bitcast_convert_type(w & jnp.int32(-65536), F32)
    second = lax.bitcast_convert_type(lax.shift_left(w, jnp.full(w.shape, 16, jnp.int32)), F32)
    return first, second


def _out_route_kernel(oa_ref, ob_ref, x_ref, mod_ref, wo_ref, g_ref, wr_ref, br_ref,
                      x1_ref, hp_ref, ti_ref, tw_ref, rk_ref, cnt_ref, tri_sc, carry_sc):
    i = pl.program_id(0)
    tm = x_ref.shape[0]

    @pl.when(i == 0)
    def _():
        r = lax.broadcasted_iota(jnp.int32, (tm, tm), 0)
        c = lax.broadcasted_iota(jnp.int32, (tm, tm), 1)
        tri_sc[...] = jnp.where(c < r, 1.0, 0.0).astype(BF16)
        carry_sc[...] = jnp.zeros(carry_sc.shape, F32)

    mod = mod_ref[...]
    o = jnp.concatenate([oa_ref[s] for s in range(oa_ref.shape[0])]
                        + [ob_ref[s] for s in range(ob_ref.shape[0])], axis=1)
    mix = jnp.dot(o, wo_ref[...], preferred_element_type=F32)
    x1 = x_ref[...] + mod[2:3, :] * mix
    x1_ref[...] = x1
    y = x1 * lax.rsqrt(jnp.mean(x1 * x1, axis=-1, keepdims=True) + EPS) * g_ref[...]
    h = y * (1.0 + mod[4:5, :]) + mod[3:4, :]
    hb = h.astype(BF16)
    hp_ref[...] = _pack_bf16_pairs(h)

    logits = jnp.dot(hb, wr_ref[...], preferred_element_type=F32) + br_ref[...]
    eid = lax.broadcasted_iota(jnp.int32, logits.shape, 1).astype(F32)
    work = logits
    vals, ids = [], []
    chosen = jnp.zeros(logits.shape, F32)
    for _ in range(TOP_K):
        v = jnp.max(work, axis=-1, keepdims=True)
        e = jnp.min(jnp.where(work == v, eid, float(N_EXPERTS)), axis=-1, keepdims=True)
        hit = eid == e
        vals.append(v)
        ids.append(e)
        chosen = jnp.where(hit, 1.0, chosen)
        work = jnp.where(hit, -jnp.inf, work)
    ex = [jnp.exp(v - vals[0]) for v in vals]
    den = ex[0] + ex[1] + ex[2] + ex[3]

    before = jnp.dot(tri_sc[...], chosen.astype(BF16), preferred_element_type=F32) + carry_sc[...]
    slot = lax.broadcasted_iota(jnp.int32, (tm, TOP_K), 1)
    ti = jnp.zeros((tm, TOP_K), F32)
    tw = jnp.zeros((tm, TOP_K), F32)
    rk = jnp.zeros((tm, TOP_K), F32)
    for kk in range(TOP_K):
        r_k = jnp.sum(jnp.where(eid == ids[kk], before, 0.0), axis=-1, keepdims=True)
        ti = jnp.where(slot == kk, ids[kk], ti)
        tw = jnp.where(slot == kk, ex[kk] / den, tw)
        rk = jnp.where(slot == kk, r_k, rk)
    ti_ref[...] = ti.astype(jnp.int32)
    tw_ref[...] = tw
    rk_ref[...] = rk.astype(jnp.int32)
    carry = carry_sc[...] + jnp.sum(chosen, axis=0, keepdims=True)
    carry_sc[...] = carry
    cnt_ref[...] = carry.astype(jnp.int32)


def _out_route(oa, ob, x2, mod, w_out_bf, g_ffn, w_router_bf, b_router, seq):
    n = x2.shape[0]
    tiles_per_seq = seq // TOK_TILE
    row = lambda i: (i, 0)
    fixed = lambda i: (0, 0)
    return pl.pallas_call(
        _out_route_kernel,
        grid=(n // TOK_TILE,),
        in_specs=[
            pl.BlockSpec((oa.shape[0], TOK_TILE, LANES), lambda i: (0, i, 0)),
            pl.BlockSpec((ob.shape[0], TOK_TILE, LANES), lambda i: (0, i, 0)),
            pl.BlockSpec((TOK_TILE, D_MODEL), row),
            pl.BlockSpec((None, SUBLANES, D_MODEL), lambda i: (i // tiles_per_seq, 0, 0)),
            pl.BlockSpec((D_MODEL, D_MODEL), fixed),
            pl.BlockSpec((1, D_MODEL), fixed),
            pl.BlockSpec((D_MODEL, N_EXPERTS), fixed),
            pl.BlockSpec((1, N_EXPERTS), fixed),
        ],
        out_specs=[
            pl.BlockSpec((TOK_TILE, D_MODEL), row),
            pl.BlockSpec((TOK_TILE, D_MODEL // 2), row),
            pl.BlockSpec((TOK_TILE, TOP_K), row),
            pl.BlockSpec((TOK_TILE, TOP_K), row),
            pl.BlockSpec((TOK_TILE, TOP_K), row),
            pl.BlockSpec((1, N_EXPERTS), fixed),
        ],
        out_shape=[
            jax.ShapeDtypeStruct((n, D_MODEL), F32),
            jax.ShapeDtypeStruct((n, D_MODEL // 2), jnp.int32),
            jax.ShapeDtypeStruct((n, TOP_K), jnp.int32),
            jax.ShapeDtypeStruct((n, TOP_K), F32),
            jax.ShapeDtypeStruct((n, TOP_K), jnp.int32),
            jax.ShapeDtypeStruct((1, N_EXPERTS), jnp.int32),
        ],
        scratch_shapes=[pltpu.VMEM((TOK_TILE, TOK_TILE), BF16), pltpu.VMEM((1, N_EXPERTS), F32)],
        compiler_params=_cparams(1, VMEM_LIMIT),
        name="out_route",
    )(oa, ob, x2, mod, w_out_bf, g_ffn.reshape(1, D_MODEL), w_router_bf,
      b_router.reshape(1, N_EXPERTS))


def _experts_kernel(te_ref, first_ref, live_ref, blk_ref, xs_ref, wgu_ref, bgu_ref, wd_ref,
                    bd_ref, y_ref, wgu_sc, wd_sc):
    del te_ref, blk_ref
    j = pl.program_id(0)

    @pl.when(live_ref[j] == 1)
    def _():
        @pl.when(first_ref[j] == 1)
        def _():
            wgu_sc[...] = wgu_ref[...].astype(BF16)
            wd_sc[...] = wd_ref[...].astype(BF16)

        first, second = _unpack_bf16_pairs(xs_ref[...])
        x = jnp.concatenate([first.astype(BF16), second.astype(BF16)], axis=1)
        acc = jnp.zeros((x.shape[0], D_MODEL), F32)
        for c in range(D_EXPERT // FF_CHUNK):
            lo_c, hi_c = c * FF_CHUNK, (c + 1) * FF_CHUNK
            gate = jnp.dot(x, wgu_sc[:, lo_c:hi_c], preferred_element_type=F32) + bgu_ref[:, lo_c:hi_c]
            up = (jnp.dot(x, wgu_sc[:, D_EXPERT + lo_c:D_EXPERT + hi_c], preferred_element_type=F32)
                  + bgu_ref[:, D_EXPERT + lo_c:D_EXPERT + hi_c])
            gate = jnp.minimum(gate, SWIGLU_LIMIT)
            up = jnp.clip(up, -SWIGLU_LIMIT, SWIGLU_LIMIT)
            act = (up + 1.0) * (gate * jax.nn.sigmoid(SWIGLU_ALPHA * gate))
            acc = acc + jnp.dot(act.astype(BF16), wd_sc[lo_c:hi_c, :], preferred_element_type=F32)
        y_ref[...] = _pack_bf16_pairs(acc + bd_ref[...])

    @pl.when(live_ref[j] == 0)
    def _():
        y_ref[...] = jnp.zeros(y_ref.shape, jnp.int32)


def _experts(tile_e, tile_first, tile_live, tile_blk, xs, w_gate_up, b_gate_up, w_down, b_down):
    n_rows, width = xs.shape
    n_tiles = n_rows // ROW_TILE
    by_tile = lambda j, te, fi, li, tb: (tb[j], 0)
    by_expert = lambda j, te, fi, li, tb: (te[j], 0, 0)
    return pl.pallas_call(
        _experts_kernel,
        grid_spec=pltpu.PrefetchScalarGridSpec(
            num_scalar_prefetch=4,
            grid=(n_tiles,),
            in_specs=[
                pl.BlockSpec((ROW_TILE, width), by_tile),
                pl.BlockSpec((None, D_MODEL, 2 * D_EXPERT), by_expert),
                pl.BlockSpec((None, 1, 2 * D_EXPERT), by_expert),
                pl.BlockSpec((None, D_EXPERT, D_MODEL), by_expert),
                pl.BlockSpec((None, 1, D_MODEL), by_expert),
            ],
            out_specs=pl.BlockSpec((ROW_TILE, D_MODEL // 2), lambda j, te, fi, li, tb: (j, 0)),
            scratch_shapes=[
                pltpu.VMEM((D_MODEL, 2 * D_EXPERT), BF16),
                pltpu.VMEM((D_EXPERT, D_MODEL), BF16),
            ],
        ),
        out_shape=jax.ShapeDtypeStruct((n_rows, D_MODEL // 2), jnp.int32),
        compiler_params=_cparams(1, VMEM_LIMIT),
        name="experts",
    )(tile_e, tile_first, tile_live, tile_blk, xs, w_gate_up,
      b_gate_up.reshape(N_EXPERTS, 1, 2 * D_EXPERT), w_down, b_down.reshape(N_EXPERTS, 1, D_MODEL))


def _sc_gather_rows(table, idx):
    m = idx.shape[0]
    width = table.shape[1]
    mesh = plsc.VectorSubcoreMesh(core_axis_name="core", subcore_axis_name="subcore")
    n_workers = mesh.num_cores * mesh.num_subcores
    per_worker = m // n_workers
    assert per_worker * n_workers == m and per_worker % SC_WINDOW == 0

    @pl.kernel(
        out_type=jax.ShapeDtypeStruct((m, width), table.dtype),
        mesh=mesh,
        scratch_types=[
            pltpu.VMEM((SC_WINDOW,), jnp.int32),
            pltpu.VMEM((SC_WINDOW, width), table.dtype),
            pltpu.SemaphoreType.DMA,
        ],
    )
    def gather_kernel(table_hbm, idx_hbm, out_hbm, idx_v, rows_v, sem):
        worker = lax.axis_index("subcore") * mesh.num_cores + lax.axis_index("core")

        @pl.loop(0, per_worker // SC_WINDOW)
        def _(j):
            base = pl.multiple_of(worker * per_worker + j * SC_WINDOW, SC_WINDOW)
            pltpu.sync_copy(idx_hbm.at[pl.ds(base, SC_WINDOW)], idx_v)
            pltpu.async_copy(table_hbm.at[idx_v], rows_v, sem).wait()
            pltpu.sync_copy(rows_v, out_hbm.at[pl.ds(base, SC_WINDOW)])

    return gather_kernel(table, idx)


def _sc_move_rows(table, src, dst):
    m = src.shape[0]
    width = table.shape[1]
    mesh = plsc.VectorSubcoreMesh(core_axis_name="core", subcore_axis_name="subcore")
    n_workers = mesh.num_cores * mesh.num_subcores
    per_worker = m // n_workers
    assert per_worker * n_workers == m and per_worker % SC_WINDOW == 0

    @pl.kernel(
        out_type=jax.ShapeDtypeStruct((m, width), table.dtype),
        mesh=mesh,
        scratch_types=[
            pltpu.VMEM((SC_WINDOW,), jnp.int32),
            pltpu.VMEM((SC_WINDOW,), jnp.int32),
            pltpu.VMEM((SC_WINDOW, width), table.dtype),
            pltpu.SemaphoreType.DMA,
        ],
    )
    def move_kernel(table_hbm, src_hbm, dst_hbm, out_hbm, src_v, dst_v, rows_v, sem):
        worker = lax.axis_index("subcore") * mesh.num_cores + lax.axis_index("core")

        @pl.loop(0, per_worker // SC_WINDOW)
        def _(j):
            base = pl.multiple_of(worker * per_worker + j * SC_WINDOW, SC_WINDOW)
            pltpu.sync_copy(src_hbm.at[pl.ds(base, SC_WINDOW)], src_v)
            pltpu.sync_copy(dst_hbm.at[pl.ds(base, SC_WINDOW)], dst_v)
            pltpu.async_copy(table_hbm.at[src_v], rows_v, sem).wait()
            pltpu.async_copy(rows_v, out_hbm.at[dst_v], sem).wait()

    return move_kernel(table, src, dst)


def _combine_dense_kernel(tw_ref, x1_ref, mod_ref, g_ref, y0_ref, y1_ref, y2_ref, y3_ref, o_ref):
    tw = tw_ref[...]
    first = second = None
    for kk, y_ref in enumerate((y0_ref, y1_ref, y2_ref, y3_ref)):
        f_k, s_k = _unpack_bf16_pairs(y_ref[...])
        w_k = tw[:, kk:kk + 1]
        first = w_k * f_k if kk == 0 else first + w_k * f_k
        second = w_k * s_k if kk == 0 else second + w_k * s_k
    ffn = jnp.concatenate([first, second], axis=1)
    x2 = x1_ref[...] + mod_ref[5:6, :] * ffn
    o_ref[...] = x2 * lax.rsqrt(jnp.mean(x2 * x2, axis=-1, keepdims=True) + EPS) * g_ref[...]


def _combine_dense(tw, x1, mod, g_final, yg, seq):
    n = x1.shape[0]
    width = yg.shape[1]
    tiles = n // TOK_TILE
    tiles_per_seq = seq // TOK_TILE
    row = lambda i: (i, 0)
    slot = lambda kk: pl.BlockSpec((TOK_TILE, width), lambda i: (kk * tiles + i, 0))
    return pl.pallas_call(
        _combine_dense_kernel,
        grid=(tiles,),
        in_specs=[
            pl.BlockSpec((TOK_TILE, TOP_K), row),
            pl.BlockSpec((TOK_TILE, D_MODEL), row),
            pl.BlockSpec((None, SUBLANES, D_MODEL), lambda i: (i // tiles_per_seq, 0, 0)),
            pl.BlockSpec((1, D_MODEL), lambda i: (0, 0)),
        ] + [slot(kk) for kk in range(TOP_K)],
        out_specs=pl.BlockSpec((TOK_TILE, D_MODEL), row),
        out_shape=jax.ShapeDtypeStruct((n, D_MODEL), F32),
        compiler_params=_cparams(1, VMEM_LIMIT),
        name="combine",
    )(tw, x1, mod, g_final.reshape(1, D_MODEL), yg, yg, yg, yg)


def _rope_tables(positions):
    half = ROPE_DIM // 2
    inv_freq = ROPE_THETA ** (-jnp.arange(0, ROPE_DIM, 2, dtype=F32) / ROPE_DIM)
    ang = positions.reshape(-1).astype(F32)[:, None] * inv_freq
    cos, sin = jnp.cos(ang), jnp.sin(ang)
    n = ang.shape[0]
    rest = B_QK_DIM - ROPE_DIM
    c = jnp.concatenate([cos, cos, jnp.ones((n, rest), F32)], axis=1)
    m = jnp.concatenate([-sin, jnp.zeros((n, half + rest), F32)], axis=1)
    p = jnp.concatenate([jnp.zeros((n, half), F32), sin, jnp.zeros((n, rest), F32)], axis=1)
    rep = LANES // B_QK_DIM
    return jnp.tile(c, (1, rep)), jnp.tile(m, (1, rep)), jnp.tile(p, (1, rep))


def _routing_tables(top_i, rank, counts):
    n = top_i.shape[0]
    n_tiles = n * TOP_K // ROW_TILE + N_EXPERTS
    padded = (counts + ROW_TILE - 1) // ROW_TILE * ROW_TILE
    ends = jnp.cumsum(padded)
    starts = ends - padded
    onehot = top_i[..., None] == jnp.arange(N_EXPERTS, dtype=jnp.int32)
    pos = jnp.sum(jnp.where(onehot, starts, 0), axis=-1) + rank
    tile_row = jnp.arange(n_tiles, dtype=jnp.int32) * ROW_TILE
    live = tile_row < ends[-1]
    last_live = jnp.maximum(ends[-1] // ROW_TILE - 1, 0)
    tile_blk = jnp.minimum(jnp.arange(n_tiles, dtype=jnp.int32), last_live)
    tile_e = jnp.sum(tile_blk[:, None] * ROW_TILE >= ends[None, :], axis=1).astype(jnp.int32)
    tile_e = jnp.minimum(tile_e, N_EXPERTS - 1)
    prev_e = jnp.concatenate([jnp.full((1,), -1, jnp.int32), tile_e[:-1]])
    first = jnp.logical_and(live, tile_e != prev_e)
    n_rows = n_tiles * ROW_TILE
    n_fill = n_rows - n * TOP_K
    gap_start = jnp.concatenate([starts + counts, ends[-1:]])
    gap_len = jnp.concatenate([padded - counts, n_rows - ends[-1:]])
    gap_end = jnp.cumsum(gap_len)
    m = jnp.arange(n_fill, dtype=jnp.int32)
    in_gap = jnp.logical_and(m[:, None] >= (gap_end - gap_len)[None, :], m[:, None] < gap_end[None, :])
    fill_dst = m + jnp.sum(jnp.where(in_gap, (gap_start - (gap_end - gap_len))[None, :], 0), axis=1)
    pos = pos.reshape(-1).astype(jnp.int32)
    move_src = jnp.concatenate([jnp.arange(n * TOP_K, dtype=jnp.int32) // TOP_K, m % n])
    move_dst = jnp.concatenate([pos, fill_dst.astype(jnp.int32)])
    return (pos, move_src, move_dst, tile_e, first.astype(jnp.int32), live.astype(jnp.int32),
            tile_blk)


def kernel(x, c, positions, w_ada, b_ada, g_mix_norm, w_in, rel_bias, lambda_q1, lambda_k1,
           lambda_q2, lambda_k2, g_subln, w_out, g_ffn_norm, w_router, b_router, w_gate_up,
           b_gate_up, w_down, b_down, g_final):
    batch, seq, _ = x.shape
    depth = w_ada.shape[0]
    assert depth == 1, "the combine kernel applies the final norm, so it must follow the only layer"
    n = batch * seq
    rope_c, rope_m, rope_p = _rope_tables(positions)
    x2 = x.reshape(n, D_MODEL)
    for l in range(depth):
        lambda_init = 0.8 - 0.6 * math.exp(-0.3 * l)
        mod = _ada(c, w_ada[l], b_ada[l])
        qa, ka, va, qb, kb, vb = _in_proj(x2, mod, g_mix_norm[l], w_in[l].astype(BF16),
                                          rope_c, rope_m, rope_p, seq)
        oa = _attn_a(qa, ka, va, _rel_bias_rows(rel_bias[l]), batch, seq)
        lam = (jnp.exp(jnp.sum(lambda_q1[l].astype(F32) * lambda_k1[l].astype(F32)))
               - jnp.exp(jnp.sum(lambda_q2[l].astype(F32) * lambda_k2[l].astype(F32)))
               + lambda_init).reshape(1)
        ob = _attn_b(lam, qb, kb, vb, g_subln[l], batch, seq, 1.0 - lambda_init)
        x1, hp, top_i, top_w, rank, counts = _out_route(
            oa, ob, x2, mod, w_out[l].astype(BF16), g_ffn_norm[l], w_router[l].astype(BF16),
            b_router[l], seq)
        pos, move_src, move_dst, tile_e, tile_first, tile_live, tile_blk = _routing_tables(
            top_i, rank, counts[0])
        xs = _sc_move_rows(hp, move_src, move_dst)
        y = _experts(tile_e, tile_first, tile_live, tile_blk, xs, w_gate_up[l], b_gate_up[l],
                     w_down[l], b_down[l])
        pos_slot_major = pos.reshape(n, TOP_K).T.reshape(-1)
        yg = _sc_gather_rows(y, pos_slot_major)
        x2 = _combine_dense(top_w, x1, mod, g_final, yg, seq)
    return x2.reshape(batch, seq, D_MODEL)
```

```python
import functools
import math

import jax
import jax.numpy as jnp
from jax import lax
from jax.experimental import pallas as pl
from jax.experimental.pallas import tpu as pltpu
from jax.experimental.pallas import tpu_sc as plsc

D_MODEL = 1024
CHUNK = 64
HEAD_DIM = 64
A_HEADS = 8
A_WIDTH = A_HEADS * HEAD_DIM
LEFT_CHUNKS = 8
MAX_REL = 128
B_HEADS = 4
B_QK_DIM = HEAD_DIM
B_V_DIM = 2 * HEAD_DIM
B_WIDTH = B_HEADS * B_V_DIM
ROPE_THETA = 500000.0
ROPE_DIM = B_QK_DIM // 4
N_EXPERTS = 32
TOP_K = 4
D_EXPERT = D_MODEL
SWIGLU_LIMIT = 7.0
SWIGLU_ALPHA = 1.702
EPS = 1e-6
NEG_INF = -1e30
LOG2_E = math.log2(math.e)
N_MOD = 6

LANES = 128
SUBLANES = 8
VMEM_BYTES_V7X = 64 * 1024 * 1024
VMEM_LIMIT = VMEM_BYTES_V7X * 7 // 8

TOK_TILE = 512
A_QBLK = 2 * CHUNK
A_BAND = (LEFT_CHUNKS + 2) * CHUNK
A_ROLL = A_BAND + A_QBLK
B_TQ = 512
B_TK = 512
B_SUB = 128
ROW_TILE = 512
FF_CHUNK = 512
SC_WINDOW = 128

F32 = jnp.float32
BF16 = jnp.bfloat16


def _cparams(n_axes, vmem=None):
    return pltpu.CompilerParams(
        dimension_semantics=("arbitrary",) * n_axes,
        vmem_limit_bytes=vmem,
    )


def _ada_kernel(c_ref, w_ref, b_ref, o_ref):
    c = c_ref[...]
    act = c * jax.nn.sigmoid(c)
    o_ref[...] = jnp.dot(act, w_ref[...], preferred_element_type=F32,
                         precision=lax.Precision.HIGHEST) + b_ref[...]


def _ada(c, w_ada, b_ada):
    b = c.shape[0]
    rows = -(-b // SUBLANES) * SUBLANES
    c_pad = jnp.pad(c, ((0, rows - b), (0, 0)))
    n_out = w_ada.shape[1]
    out = pl.pallas_call(
        _ada_kernel,
        grid=(n_out // D_MODEL,),
        in_specs=[
            pl.BlockSpec((rows, D_MODEL), lambda j: (0, 0)),
            pl.BlockSpec((D_MODEL, D_MODEL), lambda j: (0, j)),
            pl.BlockSpec((1, D_MODEL), lambda j: (0, j)),
        ],
        out_specs=pl.BlockSpec((rows, D_MODEL), lambda j: (0, j)),
        out_shape=jax.ShapeDtypeStruct((rows, n_out), F32),
        compiler_params=_cparams(1),
        name="ada",
    )(c_pad, w_ada, b_ada.reshape(1, n_out))
    mod = out[:b].reshape(b, N_MOD, D_MODEL)
    return jnp.pad(mod, ((0, 0), (0, SUBLANES - N_MOD), (0, 0)))


def _in_proj_kernel(x_ref, mod_ref, g_ref, w_ref, rc_ref, rm_ref, rp_ref,
                    qa_ref, ka_ref, va_ref, qb_ref, kb_ref, vb_ref):
    x = x_ref[...]
    mod = mod_ref[...]
    y = x * lax.rsqrt(jnp.mean(x * x, axis=-1, keepdims=True) + EPS) * g_ref[...]
    h = (y * (1.0 + mod[1:2, :]) + mod[0:1, :]).astype(BF16)
    rc, rm, rp = rc_ref[...], rm_ref[...], rp_ref[...]
    q_scale = HEAD_DIM ** -0.5

    def rope(p):
        cols = []
        for s in range(p.shape[1] // LANES):
            v = p[:, s * LANES:(s + 1) * LANES]
            cols.append(v * rc + pltpu.roll(v, LANES - ROPE_DIM // 2, 1) * rm
                        + pltpu.roll(v, ROPE_DIM // 2, 1) * rp)
        return jnp.concatenate(cols, axis=1)

    outs = (qa_ref, ka_ref, va_ref, qb_ref, kb_ref, vb_ref)
    for j, o_ref in enumerate(outs):
        p = jnp.dot(h, w_ref[:, j * A_WIDTH:(j + 1) * A_WIDTH], preferred_element_type=F32)
        if j in (3, 4):
            p = rope(p)
        if j in (0, 3):
            p = p * (q_scale * LOG2_E)
        pb = p.astype(BF16)
        for s in range(A_WIDTH // LANES):
            o_ref[s] = pb[:, s * LANES:(s + 1) * LANES]


def _in_proj(x2, mod, g_mix, w_in_bf, rope_c, rope_m, rope_p, seq):
    n = x2.shape[0]
    tiles_per_seq = seq // TOK_TILE
    row = lambda i: (i, 0)
    fixed = lambda i: (0, 0)
    n_slabs = A_WIDTH // LANES
    out_sd = jax.ShapeDtypeStruct((n_slabs, n, LANES), BF16)
    return pl.pallas_call(
        _in_proj_kernel,
        grid=(n // TOK_TILE,),
        in_specs=[
            pl.BlockSpec((TOK_TILE, D_MODEL), row),
            pl.BlockSpec((None, SUBLANES, D_MODEL), lambda i: (i // tiles_per_seq, 0, 0)),
            pl.BlockSpec((1, D_MODEL), fixed),
            pl.BlockSpec(w_in_bf.shape, fixed),
            pl.BlockSpec((TOK_TILE, LANES), row),
            pl.BlockSpec((TOK_TILE, LANES), row),
            pl.BlockSpec((TOK_TILE, LANES), row),
        ],
        out_specs=[pl.BlockSpec((n_slabs, TOK_TILE, LANES), lambda i: (0, i, 0))] * 6,
        out_shape=[out_sd] * 6,
        compiler_params=_cparams(1, VMEM_LIMIT),
        name="in_proj",
    )(x2, mod, g_mix.reshape(1, D_MODEL), w_in_bf, rope_c, rope_m, rope_p)


def _attn_a_kernel(q_ref, kp_ref, kc_ref, vp_ref, vc_ref, bias_ref, o_ref, k_sc, v_sc, bias_sc):
    g = pl.program_id(1)
    n_pairs, blk, _ = q_ref.shape
    lane = lax.broadcasted_iota(jnp.int32, (A_QBLK, LANES), 1)
    col = lax.broadcasted_iota(jnp.int32, (A_QBLK, A_BAND), 1)
    ones = jnp.ones((A_BAND, LANES), BF16)

    @pl.when(jnp.logical_and(pl.program_id(0) == 0, g == 0))
    def _():
        q_chunk = lax.broadcasted_iota(jnp.int32, (A_QBLK, A_BAND), 0) // CHUNK
        k_chunk = col // CHUNK
        in_band = jnp.logical_and(k_chunk >= q_chunk, k_chunk <= q_chunk + LEFT_CHUNKS)
        for h in range(bias_ref.shape[0]):
            rolled = pltpu.roll(jnp.broadcast_to(bias_ref[h], (A_QBLK, A_ROLL)), 0, 1,
                                stride=1, stride_axis=0)
            bias_sc[h] = jnp.where(in_band, rolled[:, :A_BAND], NEG_INF)

    def pair(p):
        k_buf = k_sc.at[p % 2]
        v_buf = v_sc.at[p % 2]
        k_buf[0:blk, :] = kp_ref[p]
        k_buf[blk:2 * blk, :] = kc_ref[p]
        v_buf[0:blk, :] = vp_ref[p]
        v_buf[blk:2 * blk, :] = vc_ref[p]

        def scores(m, hh):
            r0 = m * A_QBLK
            q = q_ref[p, r0:r0 + A_QBLK, :]
            in_head = (lane < HEAD_DIM) if hh == 0 else (lane >= HEAD_DIM)
            qh = jnp.where(in_head, q, jnp.zeros_like(q))
            return lax.dot_general(qh, k_buf[r0:r0 + A_BAND, :], (((1,), (1,)), ((), ())),
                                   preferred_element_type=F32)

        chains = [(m, hh) for m in range(blk // A_QBLK) for hh in range(2)]
        s_next = scores(*chains[0])
        halves = []
        for i, (m, hh) in enumerate(chains):
            r0 = m * A_QBLK
            s = s_next
            if i + 1 < len(chains):
                s_next = scores(*chains[i + 1])
            valid = jnp.logical_or(g > 0, col + r0 >= blk)
            s = jnp.where(valid, s + bias_sc[2 * p + hh], NEG_INF)
            pr = jnp.exp2(s - jnp.max(s, axis=1, keepdims=True))
            v_ext = jnp.concatenate([v_buf[r0:r0 + A_BAND, :], ones], axis=1)
            pv = jnp.dot(pr.astype(BF16), v_ext, preferred_element_type=F32)
            halves.append(pv[:, :LANES] / pv[:, LANES:])
            if hh == 1:
                o_ref[p, r0:r0 + A_QBLK, :] = jnp.where(lane < HEAD_DIM, halves[0],
                                                        halves[1]).astype(BF16)
                halves = []

    for p in range(n_pairs):
        pair(p)


def _attn_a(qa, ka, va, bias_rows, batch, seq):
    n_pairs, n, _ = qa.shape
    blk = LEFT_CHUNKS * CHUNK
    nblk = seq // blk
    cur = lambda b, g: (0, b * nblk + g, 0)
    prev = lambda b, g: (0, b * nblk + jnp.maximum(g - 1, 0), 0)
    slab = (n_pairs, blk, LANES)
    return pl.pallas_call(
        _attn_a_kernel,
        grid=(batch, nblk),
        in_specs=[
            pl.BlockSpec(slab, cur),
            pl.BlockSpec(slab, prev),
            pl.BlockSpec(slab, cur),
            pl.BlockSpec(slab, prev),
            pl.BlockSpec(slab, cur),
            pl.BlockSpec(bias_rows.shape, lambda b, g: (0, 0, 0)),
        ],
        out_specs=pl.BlockSpec(slab, cur),
        out_shape=jax.ShapeDtypeStruct((n_pairs, n, LANES), BF16),
        scratch_shapes=[
            pltpu.VMEM((2, 2 * blk, LANES), BF16),
            pltpu.VMEM((2, 2 * blk, LANES), BF16),
            pltpu.VMEM((bias_rows.shape[0], A_QBLK, A_BAND), F32),
        ],
        compiler_params=_cparams(2),
        name="attn_a",
    )(qa, ka, ka, va, va, bias_rows)


def _rel_bias_rows(rel_table):
    t = rel_table.astype(F32) * LOG2_E
    far = t[:, 2 * MAX_REL:]
    n_far = LEFT_CHUNKS * CHUNK - MAX_REL
    row = jnp.concatenate([
        jnp.broadcast_to(far, (t.shape[0], n_far)),
        t[:, 2 * MAX_REL:0:-1],
        jnp.broadcast_to(far, (t.shape[0], A_ROLL - A_BAND)),
    ], axis=1)
    return row.reshape(t.shape[0], 1, A_ROLL)


def _attn_b_kernel(lam_ref, q_ref, k_ref, v_ref, g_ref, o_ref, q_sc, s_sc, m_sc, acc_sc, *,
                   out_scale):
    qi = pl.program_id(2)
    n_sub = B_TQ // B_SUB
    lane = lax.broadcasted_iota(jnp.int32, (B_TQ, LANES), 1)
    q = q_ref[...]
    q_sc[0] = jnp.where(lane < B_QK_DIM, q, jnp.zeros_like(q))
    q_sc[1] = jnp.where(lane >= B_QK_DIM, q, jnp.zeros_like(q))
    m_sc[...] = jnp.full(m_sc.shape, NEG_INF, F32)
    acc_sc[...] = jnp.zeros(acc_sc.shape, F32)
    ones = jnp.ones((B_TK, LANES), BF16)

    def score_rows(k, slot, sub, c):
        rows = pl.ds(sub * B_SUB, B_SUB)
        s_sc[slot, c, rows, :] = lax.dot_general(q_sc[c, rows, :], k, (((1,), (1,)), ((), ())),
                                                 preferred_element_type=F32)

    def update_rows(s, v_ext, sub, c):
        rows = pl.ds(sub * B_SUB, B_SUB)
        m_prev = m_sc[c, rows, :]
        m_new = jnp.maximum(m_prev, jnp.max(s, axis=1, keepdims=True))
        alpha = jnp.exp2(m_prev - m_new)
        p = jnp.exp2(s - jnp.concatenate([m_new] * (s.shape[1] // LANES), axis=1))
        pv = jnp.dot(p.astype(BF16), v_ext, preferred_element_type=F32)
        acc_sc[c, rows, :] = jnp.concatenate([alpha, alpha], axis=1) * acc_sc[c, rows, :] + pv
        m_sc[c, rows, :] = m_new

    def key_block(blk):
        return k_ref[pl.ds(pl.multiple_of(blk * B_TK, B_TK), B_TK), :]

    def value_block(blk):
        return jnp.concatenate([v_ref[pl.ds(pl.multiple_of(blk * B_TK, B_TK), B_TK), :], ones], axis=1)

    def step(blk, slot, next_blk):
        v_ext = value_block(blk)
        k_next = key_block(next_blk)
        for sub in range(n_sub):
            for c in range(2):
                update_rows(s_sc[slot, c, pl.ds(sub * B_SUB, B_SUB), :], v_ext, sub, c)
                score_rows(k_next, 1 - slot, sub, c)

    def diagonal_step(slot):
        v_ext = value_block(qi)
        col_c = lax.broadcasted_iota(jnp.int32, (B_SUB, B_TK), 1) // CHUNK
        row_c = lax.broadcasted_iota(jnp.int32, (B_SUB, B_TK), 0) // CHUNK
        for sub in range(n_sub):
            keep = col_c <= row_c + sub * (B_SUB // CHUNK)
            for c in range(2):
                s = s_sc[slot, c, pl.ds(sub * B_SUB, B_SUB), :]
                update_rows(jnp.where(keep, s, NEG_INF), v_ext, sub, c)

    k0 = key_block(0)
    for sub in range(n_sub):
        for c in range(2):
            score_rows(k0, 0, sub, c)

    def pair(p, carry):
        first = 2 * p
        step(first, 0, first + 1)
        step(first + 1, 1, first + 2)
        return carry

    lax.fori_loop(0, qi // 2, pair, 0)

    @pl.when(qi % 2 == 0)
    def _():
        diagonal_step(0)

    @pl.when(qi % 2 == 1)
    def _():
        step(qi - 1, 0, qi)
        diagonal_step(1)

    lam = lam_ref[0]
    a0 = acc_sc[0]
    a1 = acc_sc[1]
    o = a0[:, :B_V_DIM] / a0[:, B_V_DIM:] - lam * (a1[:, :B_V_DIM] / a1[:, B_V_DIM:])
    y = o * lax.rsqrt(jnp.mean(o * o, axis=-1, keepdims=True) + EPS) * g_ref[...]
    o_ref[...] = (y * out_scale).astype(BF16)


def _attn_b(lam, qb, kb, vb, g_subln, batch, seq, out_scale):
    assert B_TQ == B_TK and B_V_DIM == LANES
    n = qb.shape[1]
    nq = seq // B_TQ
    q_map = lambda b, h, qi: (h, b * nq + qi, 0)
    kv_map = lambda b, h, qi: (h, b, 0)
    return pl.pallas_call(
        functools.partial(_attn_b_kernel, out_scale=out_scale),
        grid=(batch, B_HEADS, nq),
        in_specs=[
            pl.BlockSpec(memory_space=pltpu.SMEM),
            pl.BlockSpec((None, B_TQ, LANES), q_map),
            pl.BlockSpec((None, seq, LANES), kv_map),
            pl.BlockSpec((None, seq, LANES), kv_map),
            pl.BlockSpec((1, B_V_DIM), lambda b, h, qi: (0, 0)),
        ],
        out_specs=pl.BlockSpec((None, B_TQ, LANES), q_map),
        out_shape=jax.ShapeDtypeStruct((B_HEADS, n, LANES), BF16),
        scratch_shapes=[
            pltpu.VMEM((2, B_TQ, LANES), BF16),
            pltpu.VMEM((2, 2, B_TQ, B_TK), F32),
            pltpu.VMEM((2, B_TQ, LANES), F32),
            pltpu.VMEM((2, B_TQ, 2 * LANES), F32),
        ],
        compiler_params=_cparams(3),
        name="attn_b",
    )(lam, qb, kb, vb, g_subln.reshape(1, B_V_DIM))


def _pack_bf16_pairs(v):
    half = v.shape[1] // 2
    vb = v.astype(BF16)
    hi = lax.bitcast_convert_type(vb[:, :half].astype(F32), jnp.int32)
    lo = lax.bitcast_convert_type(vb[:, half:].astype(F32), jnp.int32)
    return hi | lax.shift_right_logical(lo, jnp.full(lo.shape, 16, jnp.int32))


def _unpack_bf16_pairs(w):
    first = lax.bitcast_convert_type(w & jnp.int32(-65536), F32)
    second = lax.bitcast_convert_type(lax.shift_left(w, jnp.full(w.shape, 16, jnp.int32)), F32)
    return first, second


def _out_route_kernel(oa_ref, ob_ref, x_ref, mod_ref, wo_ref, g_ref, wr_ref, br_ref,
                      x1_ref, hp_ref, ti_ref, tw_ref, rk_ref, cnt_ref, tri_sc, carry_sc):
    i = pl.program_id(0)
    tm = x_ref.shape[0]

    @pl.when(i == 0)
    def _():
        r = lax.broadcasted_iota(jnp.int32, (tm, tm), 0)
        c = lax.broadcasted_iota(jnp.int32, (tm, tm), 1)
        tri_sc[...] = jnp.where(c < r, 1.0, 0.0).astype(BF16)
        carry_sc[...] = jnp.zeros(carry_sc.shape, F32)

    mod = mod_ref[...]
    o = jnp.concatenate([oa_ref[s] for s in range(oa_ref.shape[0])]
                        + [ob_ref[s] for s in range(ob_ref.shape[0])], axis=1)
    mix = jnp.dot(o, wo_ref[...], preferred_element_type=F32)
    x1 = x_ref[...] + mod[2:3, :] * mix
    x1_ref[...] = x1
    y = x1 * lax.rsqrt(jnp.mean(x1 * x1, axis=-1, keepdims=True) + EPS) * g_ref[...]
    h = y * (1.0 + mod[4:5, :]) + mod[3:4, :]
    hb = h.astype(BF16)
    hp_ref[...] = _pack_bf16_pairs(h)

    logits = jnp.dot(hb, wr_ref[...], preferred_element_type=F32) + br_ref[...]
    eid = lax.broadcasted_iota(jnp.int32, logits.shape, 1).astype(F32)
    work = logits
    vals, ids = [], []
    chosen = jnp.zeros(logits.shape, F32)
    for _ in range(TOP_K):
        v = jnp.max(work, axis=-1, keepdims=True)
        e = jnp.min(jnp.where(work == v, eid, float(N_EXPERTS)), axis=-1, keepdims=True)
        hit = eid == e
        vals.append(v)
        ids.append(e)
        chosen = jnp.where(hit, 1.0, chosen)
        work = jnp.where(hit, -jnp.inf, work)
    ex = [jnp.exp(v - vals[0]) for v in vals]
    den = ex[0] + ex[1] + ex[2] + ex[3]

    before = jnp.dot(tri_sc[...], chosen.astype(BF16), preferred_element_type=F32) + carry_sc[...]
    slot = lax.broadcasted_iota(jnp.int32, (tm, TOP_K), 1)
    ti = jnp.zeros((tm, TOP_K), F32)
    tw = jnp.zeros((tm, TOP_K), F32)
    rk = jnp.zeros((tm, TOP_K), F32)
    for kk in range(TOP_K):
        r_k = jnp.sum(jnp.where(eid == ids[kk], before, 0.0), axis=-1, keepdims=True)
        ti = jnp.where(slot == kk, ids[kk], ti)
        tw = jnp.where(slot == kk, ex[kk] / den, tw)
        rk = jnp.where(slot == kk, r_k, rk)
    ti_ref[...] = ti.astype(jnp.int32)
    tw_ref[...] = tw
    rk_ref[...] = rk.astype(jnp.int32)
    carry = carry_sc[...] + jnp.sum(chosen, axis=0, keepdims=True)
    carry_sc[...] = carry
    cnt_ref[...] = carry.astype(jnp.int32)


def _out_route(oa, ob, x2, mod, w_out_bf, g_ffn, w_router_bf, b_router, seq):
    n = x2.shape[0]
    tiles_per_seq = seq // TOK_TILE
    row = lambda i: (i, 0)
    fixed = lambda i: (0, 0)
    return pl.pallas_call(
        _out_route_kernel,
        grid=(n // TOK_TILE,),
        in_specs=[
            pl.BlockSpec((oa.shape[0], TOK_TILE, LANES), lambda i: (0, i, 0)),
            pl.BlockSpec((ob.shape[0], TOK_TILE, LANES), lambda i: (0, i, 0)),
            pl.BlockSpec((TOK_TILE, D_MODEL), row),
            pl.BlockSpec((None, SUBLANES, D_MODEL), lambda i: (i // tiles_per_seq, 0, 0)),
            pl.BlockSpec((D_MODEL, D_MODEL), fixed),
            pl.BlockSpec((1, D_MODEL), fixed),
            pl.BlockSpec((D_MODEL, N_EXPERTS), fixed),
            pl.BlockSpec((1, N_EXPERTS), fixed),
        ],
        out_specs=[
            pl.BlockSpec((TOK_TILE, D_MODEL), row),
            pl.BlockSpec((TOK_TILE, D_MODEL // 2), row),
            pl.BlockSpec((TOK_TILE, TOP_K), row),
            pl.BlockSpec((TOK_TILE, TOP_K), row),
            pl.BlockSpec((TOK_TILE, TOP_K), row),
            pl.BlockSpec((1, N_EXPERTS), fixed),
        ],
        out_shape=[
            jax.ShapeDtypeStruct((n, D_MODEL), F32),
            jax.ShapeDtypeStruct((n, D_MODEL // 2), jnp.int32),
            jax.ShapeDtypeStruct((n, TOP_K), jnp.int32),
            jax.ShapeDtypeStruct((n, TOP_K), F32),
            jax.ShapeDtypeStruct((n, TOP_K), jnp.int32),
            jax.ShapeDtypeStruct((1, N_EXPERTS), jnp.int32),
        ],
        scratch_shapes=[pltpu.VMEM((TOK_TILE, TOK_TILE), BF16), pltpu.VMEM((1, N_EXPERTS), F32)],
        compiler_params=_cparams(1, VMEM_LIMIT),
        name="out_route",
    )(oa, ob, x2, mod, w_out_bf, g_ffn.reshape(1, D_MODEL), w_router_bf,
      b_router.reshape(1, N_EXPERTS))


def _experts_kernel(te_ref, first_ref, live_ref, blk_ref, xs_ref, wgu_ref, bgu_ref, wd_ref,
                    bd_ref, y_ref, wgu_sc, wd_sc):
    del te_ref, blk_ref
    j = pl.program_id(0)

    @pl.when(live_ref[j] == 1)
    def _():
        @pl.when(first_ref[j] == 1)
        def _():
            wgu_sc[...] = wgu_ref[...].astype(BF16)
            wd_sc[...] = wd_ref[...].astype(BF16)

        first, second = _unpack_bf16_pairs(xs_ref[...])
        x = jnp.concatenate([first.astype(BF16), second.astype(BF16)], axis=1)
        acc = jnp.zeros((x.shape[0], D_MODEL), F32)
        for c in range(D_EXPERT // FF_CHUNK):
            lo_c, hi_c = c * FF_CHUNK, (c + 1) * FF_CHUNK
            gate = jnp.dot(x, wgu_sc[:, lo_c:hi_c], preferred_element_type=F32) + bgu_ref[:, lo_c:hi_c]
            up = (jnp.dot(x, wgu_sc[:, D_EXPERT + lo_c:D_EXPERT + hi_c], preferred_element_type=F32)
                  + bgu_ref[:, D_EXPERT + lo_c:D_EXPERT + hi_c])
            gate = jnp.minimum(gate, SWIGLU_LIMIT)
            up = jnp.clip(up, -SWIGLU_LIMIT, SWIGLU_LIMIT)
            act = (up + 1.0) * (gate * jax.nn.sigmoid(SWIGLU_ALPHA * gate))
            acc = acc + jnp.dot(act.astype(BF16), wd_sc[lo_c:hi_c, :], preferred_element_type=F32)
        y_ref[...] = _pack_bf16_pairs(acc + bd_ref[...])

    @pl.when(live_ref[j] == 0)
    def _():
        y_ref[...] = jnp.zeros(y_ref.shape, jnp.int32)


def _experts(tile_e, tile_first, tile_live, tile_blk, xs, w_gate_up, b_gate_up, w_down, b_down):
    n_rows, width = xs.shape
    n_tiles = n_rows // ROW_TILE
    by_tile = lambda j, te, fi, li, tb: (tb[j], 0)
    by_expert = lambda j, te, fi, li, tb: (te[j], 0, 0)
    return pl.pallas_call(
        _experts_kernel,
        grid_spec=pltpu.PrefetchScalarGridSpec(
            num_scalar_prefetch=4,
            grid=(n_tiles,),
            in_specs=[
                pl.BlockSpec((ROW_TILE, width), by_tile),
                pl.BlockSpec((None, D_MODEL, 2 * D_EXPERT), by_expert),
                pl.BlockSpec((None, 1, 2 * D_EXPERT), by_expert),
                pl.BlockSpec((None, D_EXPERT, D_MODEL), by_expert),
                pl.BlockSpec((None, 1, D_MODEL), by_expert),
            ],
            out_specs=pl.BlockSpec((ROW_TILE, D_MODEL // 2), lambda j, te, fi, li, tb: (j, 0)),
            scratch_shapes=[
                pltpu.VMEM((D_MODEL, 2 * D_EXPERT), BF16),
                pltpu.VMEM((D_EXPERT, D_MODEL), BF16),
            ],
        ),
        out_shape=jax.ShapeDtypeStruct((n_rows, D_MODEL // 2), jnp.int32),
        compiler_params=_cparams(1, VMEM_LIMIT),
        name="experts",
    )(tile_e, tile_first, tile_live, tile_blk, xs, w_gate_up,
      b_gate_up.reshape(N_EXPERTS, 1, 2 * D_EXPERT), w_down, b_down.reshape(N_EXPERTS, 1, D_MODEL))


def _sc_gather_rows(table, idx):
    m = idx.shape[0]
    width = table.shape[1]
    mesh = plsc.VectorSubcoreMesh(core_axis_name="core", subcore_axis_name="subcore")
    n_workers = mesh.num_cores * mesh.num_subcores
    per_worker = m // n_workers
    assert per_worker * n_workers == m and per_worker % SC_WINDOW == 0

    @pl.kernel(
        out_type=jax.ShapeDtypeStruct((m, width), table.dtype),
        mesh=mesh,
        scratch_types=[
            pltpu.VMEM((SC_WINDOW,), jnp.int32),
            pltpu.VMEM((SC_WINDOW, width), table.dtype),
            pltpu.SemaphoreType.DMA,
        ],
    )
    def gather_kernel(table_hbm, idx_hbm, out_hbm, idx_v, rows_v, sem):
        worker = lax.axis_index("subcore") * mesh.num_cores + lax.axis_index("core")

        @pl.loop(0, per_worker // SC_WINDOW)
        def _(j):
            base = pl.multiple_of(worker * per_worker + j * SC_WINDOW, SC_WINDOW)
            pltpu.sync_copy(idx_hbm.at[pl.ds(base, SC_WINDOW)], idx_v)
            pltpu.async_copy(table_hbm.at[idx_v], rows_v, sem).wait()
            pltpu.sync_copy(rows_v, out_hbm.at[pl.ds(base, SC_WINDOW)])

    return gather_kernel(table, idx)


def _sc_move_rows(table, src, dst):
    m = src.shape[0]
    width = table.shape[1]
    mesh = plsc.VectorSubcoreMesh(core_axis_name="core", subcore_axis_name="subcore")
    n_workers = mesh.num_cores * mesh.num_subcores
    per_worker = m // n_workers
    assert per_worker * n_workers == m and per_worker % SC_WINDOW == 0

    @pl.kernel(
        out_type=jax.ShapeDtypeStruct((m, width), table.dtype),
        mesh=mesh,
        scratch_types=[
            pltpu.VMEM((SC_WINDOW,), jnp.int32),
            pltpu.VMEM((SC_WINDOW,), jnp.int32),
            pltpu.VMEM((SC_WINDOW, width), table.dtype),
            pltpu.SemaphoreType.DMA,
        ],
    )
    def move_kernel(table_hbm, src_hbm, dst_hbm, out_hbm, src_v, dst_v, rows_v, sem):
        worker = lax.axis_index("subcore") * mesh.num_cores + lax.axis_index("core")

        @pl.loop(0, per_worker // SC_WINDOW)
        def _(j):
            base = pl.multiple_of(worker * per_worker + j * SC_WINDOW, SC_WINDOW)
            pltpu.sync_copy(src_hbm.at[pl.ds(base, SC_WINDOW)], src_v)
            pltpu.sync_copy(dst_hbm.at[pl.ds(base, SC_WINDOW)], dst_v)
            pltpu.async_copy(table_hbm.at[src_v], rows_v, sem).wait()
            pltpu.async_copy(rows_v, out_hbm.at[dst_v], sem).wait()

    return move_kernel(table, src, dst)


def _combine_dense_kernel(tw_ref, x1_ref, mod_ref, g_ref, y0_ref, y1_ref, y2_ref, y3_ref, o_ref):
    tw = tw_ref[...]
    first = second = None
    for kk, y_ref in enumerate((y0_ref, y1_ref, y2_ref, y3_ref)):
        f_k, s_k = _unpack_bf16_pairs(y_ref[...])
        w_k = tw[:, kk:kk + 1]
        first = w_k * f_k if kk == 0 else first + w_k * f_k
        second = w_k * s_k if kk == 0 else second + w_k * s_k
    ffn = jnp.concatenate([first, second], axis=1)
    x2 = x1_ref[...] + mod_ref[5:6, :] * ffn
    o_ref[...] = x2 * lax.rsqrt(jnp.mean(x2 * x2, axis=-1, keepdims=True) + EPS) * g_ref[...]


def _combine_dense(tw, x1, mod, g_final, yg, seq):
    n = x1.shape[0]
    width = yg.shape[1]
    tiles = n // TOK_TILE
    tiles_per_seq = seq // TOK_TILE
    row = lambda i: (i, 0)
    slot = lambda kk: pl.BlockSpec((TOK_TILE, width), lambda i: (kk * tiles + i, 0))
    return pl.pallas_call(
        _combine_dense_kernel,
        grid=(tiles,),
        in_specs=[
            pl.BlockSpec((TOK_TILE, TOP_K), row),
            pl.BlockSpec((TOK_TILE, D_MODEL), row),
            pl.BlockSpec((None, SUBLANES, D_MODEL), lambda i: (i // tiles_per_seq, 0, 0)),
            pl.BlockSpec((1, D_MODEL), lambda i: (0, 0)),
        ] + [slot(kk) for kk in range(TOP_K)],
        out_specs=pl.BlockSpec((TOK_TILE, D_MODEL), row),
        out_shape=jax.ShapeDtypeStruct((n, D_MODEL), F32),
        compiler_params=_cparams(1, VMEM_LIMIT),
        name="combine",
    )(tw, x1, mod, g_final.reshape(1, D_MODEL), yg, yg, yg, yg)


def _rope_tables(positions):
    half = ROPE_DIM // 2
    inv_freq = ROPE_THETA ** (-jnp.arange(0, ROPE_DIM, 2, dtype=F32) / ROPE_DIM)
    ang = positions.reshape(-1).astype(F32)[:, None] * inv_freq
    cos, sin = jnp.cos(ang), jnp.sin(ang)
    n = ang.shape[0]
    rest = B_QK_DIM - ROPE_DIM
    c = jnp.concatenate([cos, cos, jnp.ones((n, rest), F32)], axis=1)
    m = jnp.concatenate([-sin, jnp.zeros((n, half + rest), F32)], axis=1)
    p = jnp.concatenate([jnp.zeros((n, half), F32), sin, jnp.zeros((n, rest), F32)], axis=1)
    rep = LANES // B_QK_DIM
    return jnp.tile(c, (1, rep)), jnp.tile(m, (1, rep)), jnp.tile(p, (1, rep))


def _routing_tables(top_i, rank, counts):
    n = top_i.shape[0]
    n_tiles = n * TOP_K // ROW_TILE + N_EXPERTS
    padded = (counts + ROW_TILE - 1) // ROW_TILE * ROW_TILE
    ends = jnp.cumsum(padded)
    starts = ends - padded
    onehot = top_i[..., None] == jnp.arange(N_EXPERTS, dtype=jnp.int32)
    pos = jnp.sum(jnp.where(onehot, starts, 0), axis=-1) + rank
    tile_row = jnp.arange(n_tiles, dtype=jnp.int32) * ROW_TILE
    live = tile_row < ends[-1]
    last_live = jnp.maximum(ends[-1] // ROW_TILE - 1, 0)
    tile_blk = jnp.minimum(jnp.arange(n_tiles, dtype=jnp.int32), last_live)
    tile_e = jnp.sum(tile_blk[:, None] * ROW_TILE >= ends[None, :], axis=1).astype(jnp.int32)
    tile_e = jnp.minimum(tile_e, N_EXPERTS - 1)
    prev_e = jnp.concatenate([jnp.full((1,), -1, jnp.int32), tile_e[:-1]])
    first = jnp.logical_and(live, tile_e != prev_e)
    n_rows = n_tiles * ROW_TILE
    n_fill = n_rows - n * TOP_K
    gap_start = jnp.concatenate([starts + counts, ends[-1:]])
    gap_len = jnp.concatenate([padded - counts, n_rows - ends[-1:]])
    gap_end = jnp.cumsum(gap_len)
    m = jnp.arange(n_fill, dtype=jnp.int32)
    in_gap = jnp.logical_and(m[:, None] >= (gap_end - gap_len)[None, :], m[:, None] < gap_end[None, :])
    fill_dst = m + jnp.sum(jnp.where(in_gap, (gap_start - (gap_end - gap_len))[None, :], 0), axis=1)
    pos = pos.reshape(-1).astype(jnp.int32)
    move_src = jnp.concatenate([jnp.arange(n * TOP_K, dtype=jnp.int32) // TOP_K, m % n])
    move_dst = jnp.concatenate([pos, fill_dst.astype(jnp.int32)])
    return (pos, move_src, move_dst, tile_e, first.astype(jnp.int32), live.astype(jnp.int32),
            tile_blk)


def kernel(x, c, positions, w_ada, b_ada, g_mix_norm, w_in, rel_bias, lambda_q1, lambda_k1,
           lambda_q2, lambda_k2, g_subln, w_out, g_ffn_norm, w_router, b_router, w_gate_up,
           b_gate_up, w_down, b_down, g_final):
    batch, seq, _ = x.shape
    depth = w_ada.shape[0]
    assert depth == 1, "the combine kernel applies the final norm, so it must follow the only layer"
    n = batch * seq
    rope_c, rope_m, rope_p = _rope_tables(positions)
    x2 = x.reshape(n, D_MODEL)
    for l in range(depth):
        lambda_init = 0.8 - 0.6 * math.exp(-0.3 * l)
        mod = _ada(c, w_ada[l], b_ada[l])
        qa, ka, va, qb, kb, vb = _in_proj(x2, mod, g_mix_norm[l], w_in[l].astype(BF16),
                                          rope_c, rope_m, rope_p, seq)
        oa = _attn_a(qa, ka, va, _rel_bias_rows(rel_bias[l]), batch, seq)
        lam = (jnp.exp(jnp.sum(lambda_q1[l].astype(F32) * lambda_k1[l].astype(F32)))
               - jnp.exp(jnp.sum(lambda_q2[l].astype(F32) * lambda_k2[l].astype(F32)))
               + lambda_init).reshape(1)
        ob = _attn_b(lam, qb, kb, vb, g_subln[l], batch, seq, 1.0 - lambda_init)
        x1, hp, top_i, top_w, rank, counts = _out_route(
            oa, ob, x2, mod, w_out[l].astype(BF16), g_ffn_norm[l], w_router[l].astype(BF16),
            b_router[l], seq)
        pos, move_src, move_dst, tile_e, tile_first, tile_live, tile_blk = _routing_tables(
            top_i, rank, counts[0])
        xs = _sc_move_rows(hp, move_src, move_dst)
        y = _experts(tile_e, tile_first, tile_live, tile_blk, xs, w_gate_up[l], b_gate_up[l],
                     w_down[l], b_down[l])
        pos_slot_major = pos.reshape(n, TOP_K).T.reshape(-1)
        yg = _sc_gather_rows(y, pos_slot_major)
        x2 = _combine_dense(top_w, x1, mod, g_final, yg, seq)
    return x2.reshape(batch, seq, D_MODEL)
```

```python
import functools
import math

import jax
import jax.numpy as jnp
from jax import lax
from jax.experimental import pallas as pl
from jax.experimental.pallas import tpu as pltpu
from jax.experimental.pallas import tpu_sc as plsc

D_MODEL = 1024
CHUNK = 64
HEAD_DIM = 64
A_HEADS = 8
A_WIDTH = A_HEADS * HEAD_DIM
LEFT_CHUNKS = 8
MAX_REL = 128
B_HEADS = 4
B_QK_DIM = HEAD_DIM
B_V_DIM = 2 * HEAD_DIM
B_WIDTH = B_HEADS * B_V_DIM
ROPE_THETA = 500000.0
ROPE_DIM = B_QK_DIM // 4
N_EXPERTS = 32
TOP_K = 4
D_EXPERT = D_MODEL
SWIGLU_LIMIT = 7.0
SWIGLU_ALPHA = 1.702
EPS = 1e-6
NEG_INF = -1e30
LOG2_E = math.log2(math.e)
N_MOD = 6

LANES = 128
SUBLANES = 8
VMEM_BYTES_V7X = 64 * 1024 * 1024
VMEM_LIMIT = VMEM_BYTES_V7X * 7 // 8

TOK_TILE = 512
A_QBLK = 2 * CHUNK
A_BAND = (LEFT_CHUNKS + 2) * CHUNK
A_ROLL = A_BAND + A_QBLK
B_TQ = 512
B_TK = 512
B_SUB = 128
ROW_TILE = 512
FF_CHUNK = 512
SC_WINDOW = 128

F32 = jnp.float32
BF16 = jnp.bfloat16


def _cparams(n_axes, vmem=None):
    return pltpu.CompilerParams(
        dimension_semantics=("arbitrary",) * n_axes,
        vmem_limit_bytes=vmem,
    )


def _ada_kernel(c_ref, w_ref, b_ref, o_ref):
    c = c_ref[...]
    act = c * jax.nn.sigmoid(c)
    o_ref[...] = jnp.dot(act, w_ref[...], preferred_element_type=F32,
                         precision=lax.Precision.HIGHEST) + b_ref[...]


def _ada(c, w_ada, b_ada):
    b = c.shape[0]
    rows = -(-b // SUBLANES) * SUBLANES
    c_pad = jnp.pad(c, ((0, rows - b), (0, 0)))
    n_out = w_ada.shape[1]
    out = pl.pallas_call(
        _ada_kernel,
        grid=(n_out // D_MODEL,),
        in_specs=[
            pl.BlockSpec((rows, D_MODEL), lambda j: (0, 0)),
            pl.BlockSpec((D_MODEL, D_MODEL), lambda j: (0, j)),
            pl.BlockSpec((1, D_MODEL), lambda j: (0, j)),
        ],
        out_specs=pl.BlockSpec((rows, D_MODEL), lambda j: (0, j)),
        out_shape=jax.ShapeDtypeStruct((rows, n_out), F32),
        compiler_params=_cparams(1),
        name="ada",
    )(c_pad, w_ada, b_ada.reshape(1, n_out))
    mod = out[:b].reshape(b, N_MOD, D_MODEL)
    return jnp.pad(mod, ((0, 0), (0, SUBLANES - N_MOD), (0, 0)))


def _in_proj_kernel(x_ref, mod_ref, g_ref, w_ref, rc_ref, rm_ref, rp_ref,
                    qa_ref, ka_ref, va_ref, qb_ref, kb_ref, vb_ref):
    x = x_ref[...]
    mod = mod_ref[...]
    y = x * lax.rsqrt(jnp.mean(x * x, axis=-1, keepdims=True) + EPS) * g_ref[...]
    h = (y * (1.0 + mod[1:2, :]) + mod[0:1, :]).astype(BF16)
    rc, rm, rp = rc_ref[...], rm_ref[...], rp_ref[...]
    q_scale = HEAD_DIM ** -0.5

    def rope(p):
        cols = []
        for s in range(p.shape[1] // LANES):
            v = p[:, s * LANES:(s + 1) * LANES]
            cols.append(v * rc + pltpu.roll(v, LANES - ROPE_DIM // 2, 1) * rm
                        + pltpu.roll(v, ROPE_DIM // 2, 1) * rp)
        return jnp.concatenate(cols, axis=1)

    outs = (qa_ref, ka_ref, va_ref, qb_ref, kb_ref, vb_ref)
    for j, o_ref in enumerate(outs):
        p = jnp.dot(h, w_ref[:, j * A_WIDTH:(j + 1) * A_WIDTH], preferred_element_type=F32)
        if j in (3, 4):
            p = rope(p)
        if j in (0, 3):
            p = p * (q_scale * LOG2_E)
        pb = p.astype(BF16)
        for s in range(A_WIDTH // LANES):
            o_ref[s] = pb[:, s * LANES:(s + 1) * LANES]


def _in_proj(x2, mod, g_mix, w_in_bf, rope_c, rope_m, rope_p, seq):
    n = x2.shape[0]
    tiles_per_seq = seq // TOK_TILE
    row = lambda i: (i, 0)
    fixed = lambda i: (0, 0)
    n_slabs = A_WIDTH // LANES
    out_sd = jax.ShapeDtypeStruct((n_slabs, n, LANES), BF16)
    return pl.pallas_call(
        _in_proj_kernel,
        grid=(n // TOK_TILE,),
        in_specs=[
            pl.BlockSpec((TOK_TILE, D_MODEL), row),
            pl.BlockSpec((None, SUBLANES, D_MODEL), lambda i: (i // tiles_per_seq, 0, 0)),
            pl.BlockSpec((1, D_MODEL), fixed),
            pl.BlockSpec(w_in_bf.shape, fixed),
            pl.BlockSpec((TOK_TILE, LANES), row),
            pl.BlockSpec((TOK_TILE, LANES), row),
            pl.BlockSpec((TOK_TILE, LANES), row),
        ],
        out_specs=[pl.BlockSpec((n_slabs, TOK_TILE, LANES), lambda i: (0, i, 0))] * 6,
        out_shape=[out_sd] * 6,
        compiler_params=_cparams(1, VMEM_LIMIT),
        name="in_proj",
    )(x2, mod, g_mix.reshape(1, D_MODEL), w_in_bf, rope_c, rope_m, rope_p)


def _attn_a_kernel(q_ref, kp_ref, kc_ref, vp_ref, vc_ref, bias_ref, o_ref, k_sc, v_sc, bias_sc):
    g = pl.program_id(1)
    n_pairs, blk, _ = q_ref.shape
    lane = lax.broadcasted_iota(jnp.int32, (A_QBLK, LANES), 1)
    col = lax.broadcasted_iota(jnp.int32, (A_QBLK, A_BAND), 1)
    ones = jnp.ones((A_BAND, LANES), BF16)

    @pl.when(jnp.logical_and(pl.program_id(0) == 0, g == 0))
    def _():
        q_chunk = lax.broadcasted_iota(jnp.int32, (A_QBLK, A_BAND), 0) // CHUNK
        k_chunk = col // CHUNK
        in_band = jnp.logical_and(k_chunk >= q_chunk, k_chunk <= q_chunk + LEFT_CHUNKS)
        for h in range(bias_ref.shape[0]):
            rolled = pltpu.roll(jnp.broadcast_to(bias_ref[h], (A_QBLK, A_ROLL)), 0, 1,
                                stride=1, stride_axis=0)
            bias_sc[h] = jnp.where(in_band, rolled[:, :A_BAND], NEG_INF)

    def pair(p):
        k_buf = k_sc.at[p % 2]
        v_buf = v_sc.at[p % 2]
        k_buf[0:blk, :] = kp_ref[p]
        k_buf[blk:2 * blk, :] = kc_ref[p]
        v_buf[0:blk, :] = vp_ref[p]
        v_buf[blk:2 * blk, :] = vc_ref[p]

        def scores(m, hh):
            r0 = m * A_QBLK
            q = q_ref[p, r0:r0 + A_QBLK, :]
            in_head = (lane < HEAD_DIM) if hh == 0 else (lane >= HEAD_DIM)
            qh = jnp.where(in_head, q, jnp.zeros_like(q))
            return lax.dot_general(qh, k_buf[r0:r0 + A_BAND, :], (((1,), (1,)), ((), ())),
                                   preferred_element_type=F32)

        chains = [(m, hh) for m in range(blk // A_QBLK) for hh in range(2)]
        s_next = scores(*chains[0])
        halves = []
        for i, (m, hh) in enumerate(chains):
            r0 = m * A_QBLK
            s = s_next
            if i + 1 < len(chains):
                s_next = scores(*chains[i + 1])
            valid = jnp.logical_or(g > 0, col + r0 >= blk)
            s = jnp.where(valid, s + bias_sc[2 * p + hh], NEG_INF)
            pr = jnp.exp2(s - jnp.max(s, axis=1, keepdims=True))
            v_ext = jnp.concatenate([v_buf[r0:r0 + A_BAND, :], ones], axis=1)
            pv = jnp.dot(pr.astype(BF16), v_ext, preferred_element_type=F32)
            halves.append(pv[:, :LANES] / pv[:, LANES:])
            if hh == 1:
                o_ref[p, r0:r0 + A_QBLK, :] = jnp.where(lane < HEAD_DIM, halves[0],
                                                        halves[1]).astype(BF16)
                halves = []

    for p in range(n_pairs):
        pair(p)


def _attn_a(qa, ka, va, bias_rows, batch, seq):
    n_pairs, n, _ = qa.shape
    blk = LEFT_CHUNKS * CHUNK
    nblk = seq // blk
    cur = lambda b, g: (0, b * nblk + g, 0)
    prev = lambda b, g: (0, b * nblk + jnp.maximum(g - 1, 0), 0)
    slab = (n_pairs, blk, LANES)
    return pl.pallas_call(
        _attn_a_kernel,
        grid=(batch, nblk),
        in_specs=[
            pl.BlockSpec(slab, cur),
            pl.BlockSpec(slab, prev),
            pl.BlockSpec(slab, cur),
            pl.BlockSpec(slab, prev),
            pl.BlockSpec(slab, cur),
            pl.BlockSpec(bias_rows.shape, lambda b, g: (0, 0, 0)),
        ],
        out_specs=pl.BlockSpec(slab, cur),
        out_shape=jax.ShapeDtypeStruct((n_pairs, n, LANES), BF16),
        scratch_shapes=[
            pltpu.VMEM((2, 2 * blk, LANES), BF16),
            pltpu.VMEM((2, 2 * blk, LANES), BF16),
            pltpu.VMEM((bias_rows.shape[0], A_QBLK, A_BAND), F32),
        ],
        compiler_params=_cparams(2),
        name="attn_a",
    )(qa, ka, ka, va, va, bias_rows)


def _rel_bias_rows(rel_table):
    t = rel_table.astype(F32) * LOG2_E
    far = t[:, 2 * MAX_REL:]
    n_far = LEFT_CHUNKS * CHUNK - MAX_REL
    row = jnp.concatenate([
        jnp.broadcast_to(far, (t.shape[0], n_far)),
        t[:, 2 * MAX_REL:0:-1],
        jnp.broadcast_to(far, (t.shape[0], A_ROLL - A_BAND)),
    ], axis=1)
    return row.reshape(t.shape[0], 1, A_ROLL)


def _attn_b_kernel(lam_ref, q_ref, k_ref, v_ref, g_ref, o_ref, q_sc, s_sc, m_sc, acc_sc, *,
                   out_scale):
    qi = pl.program_id(2)
    n_sub = B_TQ // B_SUB
    lane = lax.broadcasted_iota(jnp.int32, (B_TQ, LANES), 1)
    q = q_ref[...]
    q_sc[0] = jnp.where(lane < B_QK_DIM, q, jnp.zeros_like(q))
    q_sc[1] = jnp.where(lane >= B_QK_DIM, q, jnp.zeros_like(q))
    m_sc[...] = jnp.full(m_sc.shape, NEG_INF, F32)
    acc_sc[...] = jnp.zeros(acc_sc.shape, F32)
    ones = jnp.ones((B_TK, LANES), BF16)

    def score_rows(k, slot, sub, c):
        rows = pl.ds(sub * B_SUB, B_SUB)
        s_sc[slot, c, rows, :] = lax.dot_general(q_sc[c, rows, :], k, (((1,), (1,)), ((), ())),
                                                 preferred_element_type=F32)

    def update_rows(s, v_ext, sub, c):
        rows = pl.ds(sub * B_SUB, B_SUB)
        m_prev = m_sc[c, rows, :]
        m_new = jnp.maximum(m_prev, jnp.max(s, axis=1, keepdims=True))
        alpha = jnp.exp2(m_prev - m_new)
        p = jnp.exp2(s - jnp.concatenate([m_new] * (s.shape[1] // LANES), axis=1))
        pv = jnp.dot(p.astype(BF16), v_ext, preferred_element_type=F32)
        acc_sc[c, rows, :] = jnp.concatenate([alpha, alpha], axis=1) * acc_sc[c, rows, :] + pv
        m_sc[c, rows, :] = m_new

    def key_block(blk):
        return k_ref[pl.ds(pl.multiple_of(blk * B_TK, B_TK), B_TK), :]

    def value_block(blk):
        return jnp.concatenate([v_ref[pl.ds(pl.multiple_of(blk * B_TK, B_TK), B_TK), :], ones], axis=1)

    def step(blk, slot, next_blk):
        v_ext = value_block(blk)
        k_next = key_block(next_blk)
        for sub in range(n_sub):
            for c in range(2):
                update_rows(s_sc[slot, c, pl.ds(sub * B_SUB, B_SUB), :], v_ext, sub, c)
                score_rows(k_next, 1 - slot, sub, c)

    def diagonal_step(slot):
        v_ext = value_block(qi)
        col_c = lax.broadcasted_iota(jnp.int32, (B_SUB, B_TK), 1) // CHUNK
        row_c = lax.broadcasted_iota(jnp.int32, (B_SUB, B_TK), 0) // CHUNK
        for sub in range(n_sub):
            keep = col_c <= row_c + sub * (B_SUB // CHUNK)
            for c in range(2):
                s = s_sc[slot, c, pl.ds(sub * B_SUB, B_SUB), :]
                update_rows(jnp.where(keep, s, NEG_INF), v_ext, sub, c)

    k0 = key_block(0)
    for sub in range(n_sub):
        for c in range(2):
            score_rows(k0, 0, sub, c)

    def pair(p, carry):
        first = 2 * p
        step(first, 0, first + 1)
        step(first + 1, 1, first + 2)
        return carry

    lax.fori_loop(0, qi // 2, pair, 0)

    @pl.when(qi % 2 == 0)
    def _():
        diagonal_step(0)

    @pl.when(qi % 2 == 1)
    def _():
        step(qi - 1, 0, qi)
        diagonal_step(1)

    lam = lam_ref[0]
    a0 = acc_sc[0]
    a1 = acc_sc[1]
    o = a0[:, :B_V_DIM] / a0[:, B_V_DIM:] - lam * (a1[:, :B_V_DIM] / a1[:, B_V_DIM:])
    y = o * lax.rsqrt(jnp.mean(o * o, axis=-1, keepdims=True) + EPS) * g_ref[...]
    o_ref[...] = (y * out_scale).astype(BF16)


def _attn_b(lam, qb, kb, vb, g_subln, batch, seq, out_scale):
    assert B_TQ == B_TK and B_V_DIM == LANES
    n = qb.shape[1]
    nq = seq // B_TQ
    q_map = lambda b, h, qi: (h, b * nq + qi, 0)
    kv_map = lambda b, h, qi: (h, b, 0)
    return pl.pallas_call(
        functools.partial(_attn_b_kernel, out_scale=out_scale),
        grid=(batch, B_HEADS, nq),
        in_specs=[
            pl.BlockSpec(memory_space=pltpu.SMEM),
            pl.BlockSpec((None, B_TQ, LANES), q_map),
            pl.BlockSpec((None, seq, LANES), kv_map),
            pl.BlockSpec((None, seq, LANES), kv_map),
            pl.BlockSpec((1, B_V_DIM), lambda b, h, qi: (0, 0)),
        ],
        out_specs=pl.BlockSpec((None, B_TQ, LANES), q_map),
        out_shape=jax.ShapeDtypeStruct((B_HEADS, n, LANES), BF16),
        scratch_shapes=[
            pltpu.VMEM((2, B_TQ, LANES), BF16),
            pltpu.VMEM((2, 2, B_TQ, B_TK), F32),
            pltpu.VMEM((2, B_TQ, LANES), F32),
            pltpu.VMEM((2, B_TQ, 2 * LANES), F32),
        ],
        compiler_params=_cparams(3),
        name="attn_b",
    )(lam, qb, kb, vb, g_subln.reshape(1, B_V_DIM))


def _pack_bf16_pairs(v):
    half = v.shape[1] // 2
    vb = v.astype(BF16)
    hi = lax.bitcast_convert_type(vb[:, :half].astype(F32), jnp.int32)
    lo = lax.bitcast_convert_type(vb[:, half:].astype(F32), jnp.int32)
    return hi | lax.shift_right_logical(lo, jnp.full(lo.shape, 16, jnp.int32))


def _unpack_bf16_pairs(w):
    first = lax.bitcast_convert_type(w & jnp.int32(-65536), F32)
    second = lax.bitcast_convert_type(lax.shift_left(w, jnp.full(w.shape, 16, jnp.int32)), F32)
    return first, second


def _out_route_kernel(oa_ref, ob_ref, x_ref, mod_ref, wo_ref, g_ref, wr_ref, br_ref,
                      x1_ref, hp_ref, ti_ref, tw_ref, rk_ref, cnt_ref, tri_sc, carry_sc):
    i = pl.program_id(0)
    tm = x_ref.shape[0]

    @pl.when(i == 0)
    def _():
        r = lax.broadcasted_iota(jnp.int32, (tm, tm), 0)
        c = lax.broadcasted_iota(jnp.int32, (tm, tm), 1)
        tri_sc[...] = jnp.where(r < c, 1.0, 0.0).astype(BF16)
        carry_sc[...] = jnp.zeros(carry_sc.shape, F32)

    mod = mod_ref[...]
    o = jnp.concatenate([oa_ref[s] for s in range(oa_ref.shape[0])]
                        + [ob_ref[s] for s in range(ob_ref.shape[0])], axis=1)
    mix = jnp.dot(o, wo_ref[...], preferred_element_type=F32)
    x1 = x_ref[...] + mod[2:3, :] * mix
    x1_ref[...] = x1
    y = x1 * lax.rsqrt(jnp.mean(x1 * x1, axis=-1, keepdims=True) + EPS) * g_ref[...]
    h = y * (1.0 + mod[4:5, :]) + mod[3:4, :]
    hb = h.astype(BF16)
    hp_ref[...] = _pack_bf16_pairs(h)

    logits = lax.dot_general(wr_ref[...], hb, (((1,), (1,)), ((), ())),
                             preferred_element_type=F32) + br_ref[...]
    eid = lax.broadcasted_iota(jnp.int32, logits.shape, 0).astype(F32)
    work = logits
    vals, ids = [], []
    chosen = jnp.zeros(logits.shape, F32)
    for _ in range(TOP_K):
        v = jnp.max(work, axis=0, keepdims=True)
        e = jnp.min(jnp.where(work == v, eid, float(N_EXPERTS)), axis=0, keepdims=True)
        hit = eid == e
        vals.append(v)
        ids.append(e)
        chosen = jnp.where(hit, 1.0, chosen)
        work = jnp.where(hit, -jnp.inf, work)
    ex = [jnp.exp(v - vals[0]) for v in vals]
    den = ex[0] + ex[1] + ex[2] + ex[3]

    before = jnp.dot(chosen.astype(BF16), tri_sc[...], preferred_element_type=F32) + carry_sc[...]
    slot = lax.broadcasted_iota(jnp.int32, (SUBLANES, tm), 0)
    ti = jnp.zeros((SUBLANES, tm), F32)
    tw = jnp.zeros((SUBLANES, tm), F32)
    rk = jnp.zeros((SUBLANES, tm), F32)
    for kk in range(TOP_K):
        r_k = jnp.sum(jnp.where(eid == ids[kk], before, 0.0), axis=0, keepdims=True)
        ti = jnp.where(slot == kk, ids[kk], ti)
        tw = jnp.where(slot == kk, ex[kk] / den, tw)
        rk = jnp.where(slot == kk, r_k, rk)
    ti_ref[...] = ti.astype(jnp.int32)
    tw_ref[...] = tw
    rk_ref[...] = rk.astype(jnp.int32)
    carry = carry_sc[...] + jnp.sum(chosen, axis=1, keepdims=True)
    carry_sc[...] = carry
    cnt_ref[...] = jnp.broadcast_to(carry, cnt_ref.shape).astype(jnp.int32)


def _out_route(oa, ob, x2, mod, w_out_bf, g_ffn, w_router_bf, b_router, seq):
    n = x2.shape[0]
    tiles_per_seq = seq // TOK_TILE
    row = lambda i: (i, 0)
    by_lane = lambda i: (0, i)
    fixed = lambda i: (0, 0)
    return pl.pallas_call(
        _out_route_kernel,
        grid=(n // TOK_TILE,),
        in_specs=[
            pl.BlockSpec((oa.shape[0], TOK_TILE, LANES), lambda i: (0, i, 0)),
            pl.BlockSpec((ob.shape[0], TOK_TILE, LANES), lambda i: (0, i, 0)),
            pl.BlockSpec((TOK_TILE, D_MODEL), row),
            pl.BlockSpec((None, SUBLANES, D_MODEL), lambda i: (i // tiles_per_seq, 0, 0)),
            pl.BlockSpec((D_MODEL, D_MODEL), fixed),
            pl.BlockSpec((1, D_MODEL), fixed),
            pl.BlockSpec((N_EXPERTS, D_MODEL), fixed),
            pl.BlockSpec((N_EXPERTS, 1), fixed),
        ],
        out_specs=[
            pl.BlockSpec((TOK_TILE, D_MODEL), row),
            pl.BlockSpec((TOK_TILE, D_MODEL // 2), row),
            pl.BlockSpec((SUBLANES, TOK_TILE), by_lane),
            pl.BlockSpec((SUBLANES, TOK_TILE), by_lane),
            pl.BlockSpec((SUBLANES, TOK_TILE), by_lane),
            pl.BlockSpec((N_EXPERTS, LANES), fixed),
        ],
        out_shape=[
            jax.ShapeDtypeStruct((n, D_MODEL), F32),
            jax.ShapeDtypeStruct((n, D_MODEL // 2), jnp.int32),
            jax.ShapeDtypeStruct((SUBLANES, n), jnp.int32),
            jax.ShapeDtypeStruct((SUBLANES, n), F32),
            jax.ShapeDtypeStruct((SUBLANES, n), jnp.int32),
            jax.ShapeDtypeStruct((N_EXPERTS, LANES), jnp.int32),
        ],
        scratch_shapes=[pltpu.VMEM((TOK_TILE, TOK_TILE), BF16), pltpu.VMEM((N_EXPERTS, 1), F32)],
        compiler_params=_cparams(1, VMEM_LIMIT),
        name="out_route",
    )(oa, ob, x2, mod, w_out_bf, g_ffn.reshape(1, D_MODEL), w_router_bf.T,
      b_router.reshape(N_EXPERTS, 1))


def _experts_kernel(te_ref, first_ref, live_ref, blk_ref, xs_ref, wgu_ref, bgu_ref, wd_ref,
                    bd_ref, y_ref, wgu_sc, wd_sc):
    del te_ref, blk_ref
    j = pl.program_id(0)

    @pl.when(live_ref[j] == 1)
    def _():
        @pl.when(first_ref[j] == 1)
        def _():
            wgu_sc[...] = wgu_ref[...].astype(BF16)
            wd_sc[...] = wd_ref[...].astype(BF16)

        first, second = _unpack_bf16_pairs(xs_ref[...])
        x = jnp.concatenate([first.astype(BF16), second.astype(BF16)], axis=1)
        acc = jnp.zeros((x.shape[0], D_MODEL), F32)
        for c in range(D_EXPERT // FF_CHUNK):
            lo_c, hi_c = c * FF_CHUNK, (c + 1) * FF_CHUNK
            gate = jnp.dot(x, wgu_sc[:, lo_c:hi_c], preferred_element_type=F32) + bgu_ref[:, lo_c:hi_c]
            up = (jnp.dot(x, wgu_sc[:, D_EXPERT + lo_c:D_EXPERT + hi_c], preferred_element_type=F32)
                  + bgu_ref[:, D_EXPERT + lo_c:D_EXPERT + hi_c])
            gate = jnp.minimum(gate, SWIGLU_LIMIT)
            up = jnp.clip(up, -SWIGLU_LIMIT, SWIGLU_LIMIT)
            act = (up + 1.0) * (gate * jax.nn.sigmoid(SWIGLU_ALPHA * gate))
            acc = acc + jnp.dot(act.astype(BF16), wd_sc[lo_c:hi_c, :], preferred_element_type=F32)
        y_ref[...] = _pack_bf16_pairs(acc + bd_ref[...])

    @pl.when(live_ref[j] == 0)
    def _():
        y_ref[...] = jnp.zeros(y_ref.shape, jnp.int32)


def _experts(tile_e, tile_first, tile_live, tile_blk, xs, w_gate_up, b_gate_up, w_down, b_down):
    n_rows, width = xs.shape
    n_tiles = n_rows // ROW_TILE
    by_tile = lambda j, te, fi, li, tb: (tb[j], 0)
    by_expert = lambda j, te, fi, li, tb: (te[j], 0, 0)
    return pl.pallas_call(
        _experts_kernel,
        grid_spec=pltpu.PrefetchScalarGridSpec(
            num_scalar_prefetch=4,
            grid=(n_tiles,),
            in_specs=[
                pl.BlockSpec((ROW_TILE, width), by_tile),
                pl.BlockSpec((None, D_MODEL, 2 * D_EXPERT), by_expert),
                pl.BlockSpec((None, 1, 2 * D_EXPERT), by_expert),
                pl.BlockSpec((None, D_EXPERT, D_MODEL), by_expert),
                pl.BlockSpec((None, 1, D_MODEL), by_expert),
            ],
            out_specs=pl.BlockSpec((ROW_TILE, D_MODEL // 2), lambda j, te, fi, li, tb: (j, 0)),
            scratch_shapes=[
                pltpu.VMEM((D_MODEL, 2 * D_EXPERT), BF16),
                pltpu.VMEM((D_EXPERT, D_MODEL), BF16),
            ],
        ),
        out_shape=jax.ShapeDtypeStruct((n_rows, D_MODEL // 2), jnp.int32),
        compiler_params=_cparams(1, VMEM_LIMIT),
        name="experts",
    )(tile_e, tile_first, tile_live, tile_blk, xs, w_gate_up,
      b_gate_up.reshape(N_EXPERTS, 1, 2 * D_EXPERT), w_down, b_down.reshape(N_EXPERTS, 1, D_MODEL))


def _sc_gather_rows(table, idx):
    m = idx.shape[0]
    width = table.shape[1]
    mesh = plsc.VectorSubcoreMesh(core_axis_name="core", subcore_axis_name="subcore")
    n_workers = mesh.num_cores * mesh.num_subcores
    per_worker = m // n_workers
    assert per_worker * n_workers == m and per_worker % SC_WINDOW == 0

    @pl.kernel(
        out_type=jax.ShapeDtypeStruct((m, width), table.dtype),
        mesh=mesh,
        scratch_types=[
            pltpu.VMEM((SC_WINDOW,), jnp.int32),
            pltpu.VMEM((SC_WINDOW, width), table.dtype),
            pltpu.SemaphoreType.DMA,
        ],
    )
    def gather_kernel(table_hbm, idx_hbm, out_hbm, idx_v, rows_v, sem):
        worker = lax.axis_index("subcore") * mesh.num_cores + lax.axis_index("core")

        @pl.loop(0, per_worker // SC_WINDOW)
        def _(j):
            base = pl.multiple_of(worker * per_worker + j * SC_WINDOW, SC_WINDOW)
            pltpu.sync_copy(idx_hbm.at[pl.ds(base, SC_WINDOW)], idx_v)
            pltpu.async_copy(table_hbm.at[idx_v], rows_v, sem).wait()
            pltpu.sync_copy(rows_v, out_hbm.at[pl.ds(base, SC_WINDOW)])

    return gather_kernel(table, idx)


def _sc_move_rows(table, src, dst):
    m = src.shape[0]
    width = table.shape[1]
    mesh = plsc.VectorSubcoreMesh(core_axis_name="core", subcore_axis_name="subcore")
    n_workers = mesh.num_cores * mesh.num_subcores
    per_worker = m // n_workers
    assert per_worker * n_workers == m and per_worker % SC_WINDOW == 0

    @pl.kernel(
        out_type=jax.ShapeDtypeStruct((m, width), table.dtype),
        mesh=mesh,
        scratch_types=[
            pltpu.VMEM((SC_WINDOW,), jnp.int32),
            pltpu.VMEM((SC_WINDOW,), jnp.int32),
            pltpu.VMEM((SC_WINDOW, width), table.dtype),
            pltpu.SemaphoreType.DMA,
        ],
    )
    def move_kernel(table_hbm, src_hbm, dst_hbm, out_hbm, src_v, dst_v, rows_v, sem):
        worker = lax.axis_index("subcore") * mesh.num_cores + lax.axis_index("core")

        @pl.loop(0, per_worker // SC_WINDOW)
        def _(j):
            base = pl.multiple_of(worker * per_worker + j * SC_WINDOW, SC_WINDOW)
            pltpu.sync_copy(src_hbm.at[pl.ds(base, SC_WINDOW)], src_v)
            pltpu.sync_copy(dst_hbm.at[pl.ds(base, SC_WINDOW)], dst_v)
            pltpu.async_copy(table_hbm.at[src_v], rows_v, sem).wait()
            pltpu.async_copy(rows_v, out_hbm.at[dst_v], sem).wait()

    return move_kernel(table, src, dst)


def _combine_dense_kernel(tw_ref, x1_ref, mod_ref, g_ref, y0_ref, y1_ref, y2_ref, y3_ref, o_ref):
    tm = x1_ref.shape[0]
    tw = jnp.concatenate([tw_ref[...], jnp.zeros((LANES - SUBLANES, tm), F32)], axis=0).T
    first = second = None
    for kk, y_ref in enumerate((y0_ref, y1_ref, y2_ref, y3_ref)):
        f_k, s_k = _unpack_bf16_pairs(y_ref[...])
        w_k = tw[:, kk:kk + 1]
        first = w_k * f_k if kk == 0 else first + w_k * f_k
        second = w_k * s_k if kk == 0 else second + w_k * s_k
    ffn = jnp.concatenate([first, second], axis=1)
    x2 = x1_ref[...] + mod_ref[5:6, :] * ffn
    o_ref[...] = x2 * lax.rsqrt(jnp.mean(x2 * x2, axis=-1, keepdims=True) + EPS) * g_ref[...]


def _combine_dense(tw, x1, mod, g_final, yg, seq):
    n = x1.shape[0]
    width = yg.shape[1]
    tiles = n // TOK_TILE
    tiles_per_seq = seq // TOK_TILE
    row = lambda i: (i, 0)
    slot = lambda kk: pl.BlockSpec((TOK_TILE, width), lambda i: (kk * tiles + i, 0))
    return pl.pallas_call(
        _combine_dense_kernel,
        grid=(tiles,),
        in_specs=[
            pl.BlockSpec((SUBLANES, TOK_TILE), lambda i: (0, i)),
            pl.BlockSpec((TOK_TILE, D_MODEL), row),
            pl.BlockSpec((None, SUBLANES, D_MODEL), lambda i: (i // tiles_per_seq, 0, 0)),
            pl.BlockSpec((1, D_MODEL), lambda i: (0, 0)),
        ] + [slot(kk) for kk in range(TOP_K)],
        out_specs=pl.BlockSpec((TOK_TILE, D_MODEL), row),
        out_shape=jax.ShapeDtypeStruct((n, D_MODEL), F32),
        compiler_params=_cparams(1, VMEM_LIMIT),
        name="combine",
    )(tw, x1, mod, g_final.reshape(1, D_MODEL), yg, yg, yg, yg)


def _rope_tables(positions):
    half = ROPE_DIM // 2
    inv_freq = ROPE_THETA ** (-jnp.arange(0, ROPE_DIM, 2, dtype=F32) / ROPE_DIM)
    ang = positions.reshape(-1).astype(F32)[:, None] * inv_freq
    cos, sin = jnp.cos(ang), jnp.sin(ang)
    n = ang.shape[0]
    rest = B_QK_DIM - ROPE_DIM
    c = jnp.concatenate([cos, cos, jnp.ones((n, rest), F32)], axis=1)
    m = jnp.concatenate([-sin, jnp.zeros((n, half + rest), F32)], axis=1)
    p = jnp.concatenate([jnp.zeros((n, half), F32), sin, jnp.zeros((n, rest), F32)], axis=1)
    rep = LANES // B_QK_DIM
    return jnp.tile(c, (1, rep)), jnp.tile(m, (1, rep)), jnp.tile(p, (1, rep))


def _routing_tables(top_i, rank, counts):
    top_i = top_i[:TOP_K]
    rank = rank[:TOP_K]
    n = top_i.shape[1]
    n_tiles = n * TOP_K // ROW_TILE + N_EXPERTS
    padded = (counts + ROW_TILE - 1) // ROW_TILE * ROW_TILE
    ends = jnp.cumsum(padded)
    starts = ends - padded
    pos = rank
    for e in range(N_EXPERTS):
        pos = pos + jnp.where(top_i == e, starts[e], 0)
    tile_row = jnp.arange(n_tiles, dtype=jnp.int32) * ROW_TILE
    live = tile_row < ends[-1]
    last_live = jnp.maximum(ends[-1] // ROW_TILE - 1, 0)
    tile_blk = jnp.minimum(jnp.arange(n_tiles, dtype=jnp.int32), last_live)
    tile_e = jnp.sum(tile_blk[:, None] * ROW_TILE >= ends[None, :], axis=1).astype(jnp.int32)
    tile_e = jnp.minimum(tile_e, N_EXPERTS - 1)
    prev_e = jnp.concatenate([jnp.full((1,), -1, jnp.int32), tile_e[:-1]])
    first = jnp.logical_and(live, tile_e != prev_e)
    n_rows = n_tiles * ROW_TILE
    n_fill = n_rows - n * TOP_K
    gap_start = jnp.concatenate([starts + counts, ends[-1:]])
    gap_len = jnp.concatenate([padded - counts, n_rows - ends[-1:]])
    gap_end = jnp.cumsum(gap_len)
    m = jnp.arange(n_fill, dtype=jnp.int32)
    in_gap = jnp.logical_and(m[:, None] >= (gap_end - gap_len)[None, :], m[:, None] < gap_end[None, :])
    fill_dst = m + jnp.sum(jnp.where(in_gap, (gap_start - (gap_end - gap_len))[None, :], 0), axis=1)
    pos = pos.reshape(-1).astype(jnp.int32)
    move_src = jnp.concatenate([jnp.arange(n * TOP_K, dtype=jnp.int32) % n, m % n])
    move_dst = jnp.concatenate([pos, fill_dst.astype(jnp.int32)])
    return (pos, move_src, move_dst, tile_e, first.astype(jnp.int32), live.astype(jnp.int32),
            tile_blk)


def kernel(x, c, positions, w_ada, b_ada, g_mix_norm, w_in, rel_bias, lambda_q1, lambda_k1,
           lambda_q2, lambda_k2, g_subln, w_out, g_ffn_norm, w_router, b_router, w_gate_up,
           b_gate_up, w_down, b_down, g_final):
    batch, seq, _ = x.shape
    depth = w_ada.shape[0]
    assert depth == 1, "the combine kernel applies the final norm, so it must follow the only layer"
    n = batch * seq
    rope_c, rope_m, rope_p = _rope_tables(positions)
    x2 = x.reshape(n, D_MODEL)
    for l in range(depth):
        lambda_init = 0.8 - 0.6 * math.exp(-0.3 * l)
        mod = _ada(c, w_ada[l], b_ada[l])
        qa, ka, va, qb, kb, vb = _in_proj(x2, mod, g_mix_norm[l], w_in[l].astype(BF16),
                                          rope_c, rope_m, rope_p, seq)
        oa = _attn_a(qa, ka, va, _rel_bias_rows(rel_bias[l]), batch, seq)
        lam = (jnp.exp(jnp.sum(lambda_q1[l].astype(F32) * lambda_k1[l].astype(F32)))
               - jnp.exp(jnp.sum(lambda_q2[l].astype(F32) * lambda_k2[l].astype(F32)))
               + lambda_init).reshape(1)
        ob = _attn_b(lam, qb, kb, vb, g_subln[l], batch, seq, 1.0 - lambda_init)
        x1, hp, top_i, top_w, rank, counts = _out_route(
            oa, ob, x2, mod, w_out[l].astype(BF16), g_ffn_norm[l], w_router[l].astype(BF16),
            b_router[l], seq)
        pos, move_src, move_dst, tile_e, tile_first, tile_live, tile_blk = _routing_tables(
            top_i, rank, counts[:, 0])
        xs = _sc_move_rows(hp, move_src, move_dst)
        y = _experts(tile_e, tile_first, tile_live, tile_blk, xs, w_gate_up[l], b_gate_up[l],
                     w_down[l], b_down[l])
        yg = _sc_gather_rows(y, pos)
        x2 = _combine_dense(top_w, x1, mod, g_final, yg, seq)
    return x2.reshape(batch, seq, D_MODEL)
```

```python
import functools
import math

import jax
import jax.numpy as jnp
from jax import lax
from jax.experimental import pallas as pl
from jax.experimental.pallas import tpu as pltpu
from jax.experimental.pallas import tpu_sc as plsc

D_MODEL = 1024
CHUNK = 64
HEAD_DIM = 64
A_HEADS = 8
A_WIDTH = A_HEADS * HEAD_DIM
LEFT_CHUNKS = 8
MAX_REL = 128
B_HEADS = 4
B_QK_DIM = HEAD_DIM
B_V_DIM = 2 * HEAD_DIM
B_WIDTH = B_HEADS * B_V_DIM
ROPE_THETA = 500000.0
ROPE_DIM = B_QK_DIM // 4
N_EXPERTS = 32
TOP_K = 4
D_EXPERT = D_MODEL
SWIGLU_LIMIT = 7.0
SWIGLU_ALPHA = 1.702
EPS = 1e-6
NEG_INF = -1e30
LOG2_E = math.log2(math.e)
N_MOD = 6

LANES = 128
SUBLANES = 8
VMEM_BYTES_V7X = 64 * 1024 * 1024
VMEM_LIMIT = VMEM_BYTES_V7X * 7 // 8

TOK_TILE = 512
A_QBLK = 2 * CHUNK
A_BAND = (LEFT_CHUNKS + 2) * CHUNK
A_ROLL = A_BAND + A_QBLK
B_TQ = 512
B_TK = 512
B_SUB = 128
ROW_TILE = 512
FF_CHUNK = 512
SC_WINDOW = 128

F32 = jnp.float32
BF16 = jnp.bfloat16


def _cparams(n_axes, vmem=None):
    return pltpu.CompilerParams(
        dimension_semantics=("arbitrary",) * n_axes,
        vmem_limit_bytes=vmem,
    )


def _ada_kernel(c_ref, w_ref, b_ref, o_ref):
    c = c_ref[...]
    act = c * jax.nn.sigmoid(c)
    o_ref[...] = jnp.dot(act, w_ref[...], preferred_element_type=F32,
                         precision=lax.Precision.HIGHEST) + b_ref[...]


def _ada(c, w_ada, b_ada):
    b = c.shape[0]
    rows = -(-b // SUBLANES) * SUBLANES
    c_pad = jnp.pad(c, ((0, rows - b), (0, 0)))
    n_out = w_ada.shape[1]
    out = pl.pallas_call(
        _ada_kernel,
        grid=(n_out // D_MODEL,),
        in_specs=[
            pl.BlockSpec((rows, D_MODEL), lambda j: (0, 0)),
            pl.BlockSpec((D_MODEL, D_MODEL), lambda j: (0, j)),
            pl.BlockSpec((1, D_MODEL), lambda j: (0, j)),
        ],
        out_specs=pl.BlockSpec((rows, D_MODEL), lambda j: (0, j)),
        out_shape=jax.ShapeDtypeStruct((rows, n_out), F32),
        compiler_params=_cparams(1),
        name="ada",
    )(c_pad, w_ada, b_ada.reshape(1, n_out))
    mod = out[:b].reshape(b, N_MOD, D_MODEL)
    return jnp.pad(mod, ((0, 0), (0, SUBLANES - N_MOD), (0, 0)))


def _in_proj_kernel(x_ref, mod_ref, g_ref, w_ref, rc_ref, rm_ref, rp_ref,
                    qa_ref, ka_ref, va_ref, qb_ref, kb_ref, vb_ref):
    x = x_ref[...]
    mod = mod_ref[...]
    y = x * lax.rsqrt(jnp.mean(x * x, axis=-1, keepdims=True) + EPS) * g_ref[...]
    h = (y * (1.0 + mod[1:2, :]) + mod[0:1, :]).astype(BF16)
    rc, rm, rp = rc_ref[...], rm_ref[...], rp_ref[...]
    q_scale = HEAD_DIM ** -0.5

    def rope(p):
        cols = []
        for s in range(p.shape[1] // LANES):
            v = p[:, s * LANES:(s + 1) * LANES]
            cols.append(v * rc + pltpu.roll(v, LANES - ROPE_DIM // 2, 1) * rm
                        + pltpu.roll(v, ROPE_DIM // 2, 1) * rp)
        return jnp.concatenate(cols, axis=1)

    outs = (qa_ref, ka_ref, va_ref, qb_ref, kb_ref, vb_ref)
    for j, o_ref in enumerate(outs):
        p = jnp.dot(h, w_ref[:, j * A_WIDTH:(j + 1) * A_WIDTH], preferred_element_type=F32)
        if j in (3, 4):
            p = rope(p)
        if j in (0, 3):
            p = p * (q_scale * LOG2_E)
        pb = p.astype(BF16)
        for s in range(A_WIDTH // LANES):
            o_ref[s] = pb[:, s * LANES:(s + 1) * LANES]


def _in_proj(x2, mod, g_mix, w_in_bf, rope_c, rope_m, rope_p, seq):
    n = x2.shape[0]
    tiles_per_seq = seq // TOK_TILE
    row = lambda i: (i, 0)
    fixed = lambda i: (0, 0)
    n_slabs = A_WIDTH // LANES
    out_sd = jax.ShapeDtypeStruct((n_slabs, n, LANES), BF16)
    return pl.pallas_call(
        _in_proj_kernel,
        grid=(n // TOK_TILE,),
        in_specs=[
            pl.BlockSpec((TOK_TILE, D_MODEL), row),
            pl.BlockSpec((None, SUBLANES, D_MODEL), lambda i: (i // tiles_per_seq, 0, 0)),
            pl.BlockSpec((1, D_MODEL), fixed),
            pl.BlockSpec(w_in_bf.shape, fixed),
            pl.BlockSpec((TOK_TILE, LANES), row),
            pl.BlockSpec((TOK_TILE, LANES), row),
            pl.BlockSpec((TOK_TILE, LANES), row),
        ],
        out_specs=[pl.BlockSpec((n_slabs, TOK_TILE, LANES), lambda i: (0, i, 0))] * 6,
        out_shape=[out_sd] * 6,
        compiler_params=_cparams(1, VMEM_LIMIT),
        name="in_proj",
    )(x2, mod, g_mix.reshape(1, D_MODEL), w_in_bf, rope_c, rope_m, rope_p)


def _attn_a_kernel(q_ref, kp_ref, kc_ref, vp_ref, vc_ref, bias_ref, o_ref, k_sc, v_sc, bias_sc):
    g = pl.program_id(1)
    n_pairs, blk, _ = q_ref.shape
    lane = lax.broadcasted_iota(jnp.int32, (A_QBLK, LANES), 1)
    col = lax.broadcasted_iota(jnp.int32, (A_QBLK, A_BAND), 1)
    ones = jnp.ones((A_BAND, LANES), BF16)

    @pl.when(jnp.logical_and(pl.program_id(0) == 0, g == 0))
    def _():
        q_chunk = lax.broadcasted_iota(jnp.int32, (A_QBLK, A_BAND), 0) // CHUNK
        k_chunk = col // CHUNK
        in_band = jnp.logical_and(k_chunk >= q_chunk, k_chunk <= q_chunk + LEFT_CHUNKS)
        for h in range(bias_ref.shape[0]):
            rolled = pltpu.roll(jnp.broadcast_to(bias_ref[h], (A_QBLK, A_ROLL)), 0, 1,
                                stride=1, stride_axis=0)
            bias_sc[h] = jnp.where(in_band, rolled[:, :A_BAND], NEG_INF)

    def pair(p):
        k_buf = k_sc.at[p % 2]
        v_buf = v_sc.at[p % 2]
        k_buf[0:blk, :] = kp_ref[p]
        k_buf[blk:2 * blk, :] = kc_ref[p]
        v_buf[0:blk, :] = vp_ref[p]
        v_buf[blk:2 * blk, :] = vc_ref[p]

        def scores(m, hh):
            r0 = m * A_QBLK
            q = q_ref[p, r0:r0 + A_QBLK, :]
            in_head = (lane < HEAD_DIM) if hh == 0 else (lane >= HEAD_DIM)
            qh = jnp.where(in_head, q, jnp.zeros_like(q))
            return lax.dot_general(qh, k_buf[r0:r0 + A_BAND, :], (((1,), (1,)), ((), ())),
                                   preferred_element_type=F32)

        chains = [(m, hh) for m in range(blk // A_QBLK) for hh in range(2)]
        s_next = scores(*chains[0])
        halves = []
        for i, (m, hh) in enumerate(chains):
            r0 = m * A_QBLK
            s = s_next
            if i + 1 < len(chains):
                s_next = scores(*chains[i + 1])
            valid = jnp.logical_or(g > 0, col + r0 >= blk)
            s = jnp.where(valid, s + bias_sc[2 * p + hh], NEG_INF)
            pr = jnp.exp2(s - jnp.max(s, axis=1, keepdims=True))
            v_ext = jnp.concatenate([v_buf[r0:r0 + A_BAND, :], ones], axis=1)
            pv = jnp.dot(pr.astype(BF16), v_ext, preferred_element_type=F32)
            halves.append(pv[:, :LANES] / pv[:, LANES:])
            if hh == 1:
                o_ref[p, r0:r0 + A_QBLK, :] = jnp.where(lane < HEAD_DIM, halves[0],
                                                        halves[1]).astype(BF16)
                halves = []

    for p in range(n_pairs):
        pair(p)


def _attn_a(qa, ka, va, bias_rows, batch, seq):
    n_pairs, n, _ = qa.shape
    blk = LEFT_CHUNKS * CHUNK
    nblk = seq // blk
    cur = lambda b, g: (0, b * nblk + g, 0)
    prev = lambda b, g: (0, b * nblk + jnp.maximum(g - 1, 0), 0)
    slab = (n_pairs, blk, LANES)
    return pl.pallas_call(
        _attn_a_kernel,
        grid=(batch, nblk),
        in_specs=[
            pl.BlockSpec(slab, cur),
            pl.BlockSpec(slab, prev),
            pl.BlockSpec(slab, cur),
            pl.BlockSpec(slab, prev),
            pl.BlockSpec(slab, cur),
            pl.BlockSpec(bias_rows.shape, lambda b, g: (0, 0, 0)),
        ],
        out_specs=pl.BlockSpec(slab, cur),
        out_shape=jax.ShapeDtypeStruct((n_pairs, n, LANES), BF16),
        scratch_shapes=[
            pltpu.VMEM((2, 2 * blk, LANES), BF16),
            pltpu.VMEM((2, 2 * blk, LANES), BF16),
            pltpu.VMEM((bias_rows.shape[0], A_QBLK, A_BAND), F32),
        ],
        compiler_params=_cparams(2),
        name="attn_a",
    )(qa, ka, ka, va, va, bias_rows)


def _rel_bias_rows(rel_table):
    t = rel_table.astype(F32) * LOG2_E
    far = t[:, 2 * MAX_REL:]
    n_far = LEFT_CHUNKS * CHUNK - MAX_REL
    row = jnp.concatenate([
        jnp.broadcast_to(far, (t.shape[0], n_far)),
        t[:, 2 * MAX_REL:0:-1],
        jnp.broadcast_to(far, (t.shape[0], A_ROLL - A_BAND)),
    ], axis=1)
    return row.reshape(t.shape[0], 1, A_ROLL)


def _attn_b_kernel(lam_ref, q_ref, k_ref, v_ref, g_ref, o_ref, q_sc, s_sc, m_sc, acc_sc, *,
                   out_scale):
    qi = pl.program_id(2)
    n_sub = B_TQ // B_SUB
    lane = lax.broadcasted_iota(jnp.int32, (B_TQ, LANES), 1)
    q = q_ref[...]
    q_sc[0] = jnp.where(lane < B_QK_DIM, q, jnp.zeros_like(q))
    q_sc[1] = jnp.where(lane >= B_QK_DIM, q, jnp.zeros_like(q))
    m_sc[...] = jnp.full(m_sc.shape, NEG_INF, F32)
    acc_sc[...] = jnp.zeros(acc_sc.shape, F32)
    ones = jnp.ones((B_TK, LANES), BF16)

    def score_rows(k, slot, sub, c):
        rows = pl.ds(sub * B_SUB, B_SUB)
        s_sc[slot, c, rows, :] = lax.dot_general(q_sc[c, rows, :], k, (((1,), (1,)), ((), ())),
                                                 preferred_element_type=F32)

    def update_rows(s, v_ext, sub, c):
        rows = pl.ds(sub * B_SUB, B_SUB)
        m_prev = m_sc[c, rows, :]
        m_new = jnp.maximum(m_prev, jnp.max(s, axis=1, keepdims=True))
        alpha = jnp.exp2(m_prev - m_new)
        p = jnp.exp2(s - jnp.concatenate([m_new] * (s.shape[1] // LANES), axis=1))
        pv = jnp.dot(p.astype(BF16), v_ext, preferred_element_type=F32)
        acc_sc[c, rows, :] = jnp.concatenate([alpha, alpha], axis=1) * acc_sc[c, rows, :] + pv
        m_sc[c, rows, :] = m_new

    def key_block(blk):
        return k_ref[pl.ds(pl.multiple_of(blk * B_TK, B_TK), B_TK), :]

    def value_block(blk):
        return jnp.concatenate([v_ref[pl.ds(pl.multiple_of(blk * B_TK, B_TK), B_TK), :], ones], axis=1)

    def step(blk, slot, next_blk):
        v_ext = value_block(blk)
        k_next = key_block(next_blk)
        for sub in range(n_sub):
            for c in range(2):
                update_rows(s_sc[slot, c, pl.ds(sub * B_SUB, B_SUB), :], v_ext, sub, c)
                score_rows(k_next, 1 - slot, sub, c)

    def diagonal_step(slot):
        v_ext = value_block(qi)
        col_c = lax.broadcasted_iota(jnp.int32, (B_SUB, B_TK), 1) // CHUNK
        row_c = lax.broadcasted_iota(jnp.int32, (B_SUB, B_TK), 0) // CHUNK
        for sub in range(n_sub):
            keep = col_c <= row_c + sub * (B_SUB // CHUNK)
            for c in range(2):
                s = s_sc[slot, c, pl.ds(sub * B_SUB, B_SUB), :]
                update_rows(jnp.where(keep, s, NEG_INF), v_ext, sub, c)

    k0 = key_block(0)
    for sub in range(n_sub):
        for c in range(2):
            score_rows(k0, 0, sub, c)

    def pair(p, carry):
        first = 2 * p
        step(first, 0, first + 1)
        step(first + 1, 1, first + 2)
        return carry

    lax.fori_loop(0, qi // 2, pair, 0)

    @pl.when(qi % 2 == 0)
    def _():
        diagonal_step(0)

    @pl.when(qi % 2 == 1)
    def _():
        step(qi - 1, 0, qi)
        diagonal_step(1)

    lam = lam_ref[0]
    a0 = acc_sc[0]
    a1 = acc_sc[1]
    o = a0[:, :B_V_DIM] / a0[:, B_V_DIM:] - lam * (a1[:, :B_V_DIM] / a1[:, B_V_DIM:])
    y = o * lax.rsqrt(jnp.mean(o * o, axis=-1, keepdims=True) + EPS) * g_ref[...]
    o_ref[...] = (y * out_scale).astype(BF16)


def _attn_b(lam, qb, kb, vb, g_subln, batch, seq, out_scale):
    assert B_TQ == B_TK and B_V_DIM == LANES
    n = qb.shape[1]
    nq = seq // B_TQ
    q_map = lambda b, h, qi: (h, b * nq + qi, 0)
    kv_map = lambda b, h, qi: (h, b, 0)
    return pl.pallas_call(
        functools.partial(_attn_b_kernel, out_scale=out_scale),
        grid=(batch, B_HEADS, nq),
        in_specs=[
            pl.BlockSpec(memory_space=pltpu.SMEM),
            pl.BlockSpec((None, B_TQ, LANES), q_map),
            pl.BlockSpec((None, seq, LANES), kv_map),
            pl.BlockSpec((None, seq, LANES), kv_map),
            pl.BlockSpec((1, B_V_DIM), lambda b, h, qi: (0, 0)),
        ],
        out_specs=pl.BlockSpec((None, B_TQ, LANES), q_map),
        out_shape=jax.ShapeDtypeStruct((B_HEADS, n, LANES), BF16),
        scratch_shapes=[
            pltpu.VMEM((2, B_TQ, LANES), BF16),
            pltpu.VMEM((2, 2, B_TQ, B_TK), F32),
            pltpu.VMEM((2, B_TQ, LANES), F32),
            pltpu.VMEM((2, B_TQ, 2 * LANES), F32),
        ],
        compiler_params=_cparams(3),
        name="attn_b",
    )(lam, qb, kb, vb, g_subln.reshape(1, B_V_DIM))


def _pack_bf16_pairs(v):
    half = v.shape[1] // 2
    vb = v.astype(BF16)
    hi = lax.bitcast_convert_type(vb[:, :half].astype(F32), jnp.int32)
    lo = lax.bitcast_convert_type(vb[:, half:].astype(F32), jnp.int32)
    return hi | lax.shift_right_logical(lo, jnp.full(lo.shape, 16, jnp.int32))


def _unpack_bf16_pairs(w):
    first = lax.bitcast_convert_type(w & jnp.int32(-65536), F32)
    second = lax.bitcast_convert_type(lax.shift_left(w, jnp.full(w.shape, 16, jnp.int32)), F32)
    return first, second


def _out_route_kernel(oa_ref, ob_ref, x_ref, mod_ref, wo_ref, g_ref, wr_ref, br_ref,
                      x1_ref, hp_ref, ti_ref, tw_ref, rk_ref, cnt_ref, tri_sc, carry_sc):
    i = pl.program_id(0)
    tm = x_ref.shape[0]

    @pl.when(i == 0)
    def _():
        r = lax.broadcasted_iota(jnp.int32, (tm, tm), 0)
        c = lax.broadcasted_iota(jnp.int32, (tm, tm), 1)
        tri_sc[...] = jnp.where(r < c, 1.0, 0.0).astype(BF16)
        carry_sc[...] = jnp.zeros(carry_sc.shape, F32)

    mod = mod_ref[...]
    o = jnp.concatenate([oa_ref[s] for s in range(oa_ref.shape[0])]
                        + [ob_ref[s] for s in range(ob_ref.shape[0])], axis=1)
    mix = jnp.dot(o, wo_ref[...], preferred_element_type=F32)
    x1 = x_ref[...] + mod[2:3, :] * mix
    x1_ref[...] = x1
    y = x1 * lax.rsqrt(jnp.mean(x1 * x1, axis=-1, keepdims=True) + EPS) * g_ref[...]
    h = y * (1.0 + mod[4:5, :]) + mod[3:4, :]
    hb = h.astype(BF16)
    hp_ref[...] = _pack_bf16_pairs(h)

    logits = lax.dot_general(wr_ref[...], hb, (((1,), (1,)), ((), ())),
                             preferred_element_type=F32) + br_ref[...]
    eid = lax.broadcasted_iota(jnp.int32, logits.shape, 0).astype(F32)
    work = logits
    vals, ids = [], []
    chosen = jnp.zeros(logits.shape, F32)
    for _ in range(TOP_K):
        v = jnp.max(work, axis=0, keepdims=True)
        e = jnp.min(jnp.where(work == v, eid, float(N_EXPERTS)), axis=0, keepdims=True)
        hit = eid == e
        vals.append(v)
        ids.append(e)
        chosen = jnp.where(hit, 1.0, chosen)
        work = jnp.where(hit, -jnp.inf, work)
    ex = [jnp.exp(v - vals[0]) for v in vals]
    den = ex[0] + ex[1] + ex[2] + ex[3]

    before = jnp.dot(chosen.astype(BF16), tri_sc[...], preferred_element_type=F32) + carry_sc[...]
    slot = lax.broadcasted_iota(jnp.int32, (SUBLANES, tm), 0)
    ti = jnp.zeros((SUBLANES, tm), F32)
    tw = jnp.zeros((SUBLANES, tm), F32)
    rk = jnp.zeros((SUBLANES, tm), F32)
    for kk in range(TOP_K):
        r_k = jnp.sum(jnp.where(eid == ids[kk], before, 0.0), axis=0, keepdims=True)
        ti = jnp.where(slot == kk, ids[kk], ti)
        tw = jnp.where(slot == kk, ex[kk] / den, tw)
        rk = jnp.where(slot == kk, r_k, rk)
    ti_ref[...] = ti.astype(jnp.int32)
    tw_ref[...] = tw
    rk_ref[...] = rk.astype(jnp.int32)
    carry = carry_sc[...] + jnp.sum(chosen, axis=1, keepdims=True)
    carry_sc[...] = carry
    cnt_ref[...] = jnp.broadcast_to(carry, cnt_ref.shape).astype(jnp.int32)


def _out_route(oa, ob, x2, mod, w_out_bf, g_ffn, w_router_bf, b_router, seq):
    n = x2.shape[0]
    tiles_per_seq = seq // TOK_TILE
    row = lambda i: (i, 0)
    by_lane = lambda i: (0, i)
    fixed = lambda i: (0, 0)
    return pl.pallas_call(
        _out_route_kernel,
        grid=(n // TOK_TILE,),
        in_specs=[
            pl.BlockSpec((oa.shape[0], TOK_TILE, LANES), lambda i: (0, i, 0)),
            pl.BlockSpec((ob.shape[0], TOK_TILE, LANES), lambda i: (0, i, 0)),
            pl.BlockSpec((TOK_TILE, D_MODEL), row),
            pl.BlockSpec((None, SUBLANES, D_MODEL), lambda i: (i // tiles_per_seq, 0, 0)),
            pl.BlockSpec((D_MODEL, D_MODEL), fixed),
            pl.BlockSpec((1, D_MODEL), fixed),
            pl.BlockSpec((N_EXPERTS, D_MODEL), fixed),
            pl.BlockSpec((N_EXPERTS, 1), fixed),
        ],
        out_specs=[
            pl.BlockSpec((TOK_TILE, D_MODEL), row),
            pl.BlockSpec((TOK_TILE, D_MODEL // 2), row),
            pl.BlockSpec((SUBLANES, TOK_TILE), by_lane),
            pl.BlockSpec((SUBLANES, TOK_TILE), by_lane),
            pl.BlockSpec((SUBLANES, TOK_TILE), by_lane),
            pl.BlockSpec((N_EXPERTS, LANES), fixed),
        ],
        out_shape=[
            jax.ShapeDtypeStruct((n, D_MODEL), F32),
            jax.ShapeDtypeStruct((n, D_MODEL // 2), jnp.int32),
            jax.ShapeDtypeStruct((SUBLANES, n), jnp.int32),
            jax.ShapeDtypeStruct((SUBLANES, n), F32),
            jax.ShapeDtypeStruct((SUBLANES, n), jnp.int32),
            jax.ShapeDtypeStruct((N_EXPERTS, LANES), jnp.int32),
        ],
        scratch_shapes=[pltpu.VMEM((TOK_TILE, TOK_TILE), BF16), pltpu.VMEM((N_EXPERTS, 1), F32)],
        compiler_params=_cparams(1, VMEM_LIMIT),
        name="out_route",
    )(oa, ob, x2, mod, w_out_bf, g_ffn.reshape(1, D_MODEL), w_router_bf.T,
      b_router.reshape(N_EXPERTS, 1))


def _experts_kernel(start_ref, count_ref, xs_ref, wgu_ref, bgu_ref, wd_ref, bd_ref, y_ref,
                    wgu_sc, wd_sc, x_buf, y_buf, x_sem, y_sem):
    e = pl.program_id(0)
    first_tile = start_ref[e]
    n_tiles = count_ref[e]
    n_live = start_ref[N_EXPERTS]

    def x_copy(g, slot):
        return pltpu.make_async_copy(xs_ref.at[pl.ds(g * ROW_TILE, ROW_TILE), :], x_buf.at[slot],
                                     x_sem.at[slot])

    def y_copy(g, slot):
        return pltpu.make_async_copy(y_buf.at[slot], y_ref.at[pl.ds(g * ROW_TILE, ROW_TILE), :],
                                     y_sem.at[slot])

    @pl.when(jnp.logical_and(e == 0, n_live > 0))
    def _():
        x_copy(0, 0).start()

    @pl.when(n_tiles > 0)
    def _():
        wgu_sc[...] = wgu_ref[...].astype(BF16)
        wd_sc[...] = wd_ref[...].astype(BF16)

    def tile(j, carry):
        g = first_tile + j
        slot = g % 2
        x_copy(g, slot).wait()

        @pl.when(g + 1 < n_live)
        def _():
            x_copy(g + 1, 1 - slot).start()

        first, second = _unpack_bf16_pairs(x_buf[slot])
        x = jnp.concatenate([first.astype(BF16), second.astype(BF16)], axis=1)
        acc = jnp.zeros((x.shape[0], D_MODEL), F32)
        for c in range(D_EXPERT // FF_CHUNK):
            lo_c, hi_c = c * FF_CHUNK, (c + 1) * FF_CHUNK
            gate = jnp.dot(x, wgu_sc[:, lo_c:hi_c], preferred_element_type=F32) + bgu_ref[:, lo_c:hi_c]
            up = (jnp.dot(x, wgu_sc[:, D_EXPERT + lo_c:D_EXPERT + hi_c], preferred_element_type=F32)
                  + bgu_ref[:, D_EXPERT + lo_c:D_EXPERT + hi_c])
            gate = jnp.minimum(gate, SWIGLU_LIMIT)
            up = jnp.clip(up, -SWIGLU_LIMIT, SWIGLU_LIMIT)
            act = (up + 1.0) * (gate * jax.nn.sigmoid(SWIGLU_ALPHA * gate))
            acc = acc + jnp.dot(act.astype(BF16), wd_sc[lo_c:hi_c, :], preferred_element_type=F32)

        @pl.when(g >= 2)
        def _():
            y_copy(g - 2, slot).wait()

        y_buf[slot] = _pack_bf16_pairs(acc + bd_ref[...])
        y_copy(g, slot).start()
        return carry

    lax.fori_loop(0, n_tiles, tile, 0)

    @pl.when(e == pl.num_programs(0) - 1)
    def _():
        for back in (2, 1):
            @pl.when(n_live >= back)
            def _():
                y_copy(n_live - back, (n_live - back) % 2).wait()

        y_buf[0] = jnp.zeros(y_buf.shape[1:], jnp.int32)

        def clear(g, carry):
            y_copy(g, 0).start()
            y_copy(g, 0).wait()
            return carry

        lax.fori_loop(n_live, y_ref.shape[0] // ROW_TILE, clear, 0)


def _experts(tile_start, tile_count, xs, w_gate_up, b_gate_up, w_down, b_down):
    n_rows, width = xs.shape
    by_expert = lambda e, ts, tc: (e, 0, 0)
    return pl.pallas_call(
        _experts_kernel,
        grid_spec=pltpu.PrefetchScalarGridSpec(
            num_scalar_prefetch=2,
            grid=(N_EXPERTS,),
            in_specs=[
                pl.BlockSpec(memory_space=pl.ANY),
                pl.BlockSpec((None, D_MODEL, 2 * D_EXPERT), by_expert),
                pl.BlockSpec((None, 1, 2 * D_EXPERT), by_expert),
                pl.BlockSpec((None, D_EXPERT, D_MODEL), by_expert),
                pl.BlockSpec((None, 1, D_MODEL), by_expert),
            ],
            out_specs=pl.BlockSpec(memory_space=pl.ANY),
            scratch_shapes=[
                pltpu.VMEM((D_MODEL, 2 * D_EXPERT), BF16),
                pltpu.VMEM((D_EXPERT, D_MODEL), BF16),
                pltpu.VMEM((2, ROW_TILE, width), jnp.int32),
                pltpu.VMEM((2, ROW_TILE, D_MODEL // 2), jnp.int32),
                pltpu.SemaphoreType.DMA((2,)),
                pltpu.SemaphoreType.DMA((2,)),
            ],
        ),
        out_shape=jax.ShapeDtypeStruct((n_rows, D_MODEL // 2), jnp.int32),
        compiler_params=_cparams(1, VMEM_LIMIT),
        name="experts",
    )(tile_start, tile_count, xs, w_gate_up, b_gate_up.reshape(N_EXPERTS, 1, 2 * D_EXPERT),
      w_down, b_down.reshape(N_EXPERTS, 1, D_MODEL))


def _sc_gather_rows(table, idx):
    m = idx.shape[0]
    width = table.shape[1]
    mesh = plsc.VectorSubcoreMesh(core_axis_name="core", subcore_axis_name="subcore")
    n_workers = mesh.num_cores * mesh.num_subcores
    per_worker = m // n_workers
    assert per_worker * n_workers == m and per_worker % SC_WINDOW == 0

    @pl.kernel(
        out_type=jax.ShapeDtypeStruct((m, width), table.dtype),
        mesh=mesh,
        scratch_types=[
            pltpu.VMEM((SC_WINDOW,), jnp.int32),
            pltpu.VMEM((SC_WINDOW, width), table.dtype),
            pltpu.SemaphoreType.DMA,
        ],
    )
    def gather_kernel(table_hbm, idx_hbm, out_hbm, idx_v, rows_v, sem):
        worker = lax.axis_index("subcore") * mesh.num_cores + lax.axis_index("core")

        @pl.loop(0, per_worker // SC_WINDOW)
        def _(j):
            base = pl.multiple_of(worker * per_worker + j * SC_WINDOW, SC_WINDOW)
            pltpu.sync_copy(idx_hbm.at[pl.ds(base, SC_WINDOW)], idx_v)
            pltpu.async_copy(table_hbm.at[idx_v], rows_v, sem).wait()
            pltpu.sync_copy(rows_v, out_hbm.at[pl.ds(base, SC_WINDOW)])

    return gather_kernel(table, idx)


def _sc_move_rows(table, src, dst):
    m = src.shape[0]
    width = table.shape[1]
    mesh = plsc.VectorSubcoreMesh(core_axis_name="core", subcore_axis_name="subcore")
    n_workers = mesh.num_cores * mesh.num_subcores
    per_worker = m // n_workers
    assert per_worker * n_workers == m and per_worker % SC_WINDOW == 0

    @pl.kernel(
        out_type=jax.ShapeDtypeStruct((m, width), table.dtype),
        mesh=mesh,
        scratch_types=[
            pltpu.VMEM((SC_WINDOW,), jnp.int32),
            pltpu.VMEM((SC_WINDOW,), jnp.int32),
            pltpu.VMEM((SC_WINDOW, width), table.dtype),
            pltpu.SemaphoreType.DMA,
        ],
    )
    def move_kernel(table_hbm, src_hbm, dst_hbm, out_hbm, src_v, dst_v, rows_v, sem):
        worker = lax.axis_index("subcore") * mesh.num_cores + lax.axis_index("core")

        @pl.loop(0, per_worker // SC_WINDOW)
        def _(j):
            base = pl.multiple_of(worker * per_worker + j * SC_WINDOW, SC_WINDOW)
            pltpu.sync_copy(src_hbm.at[pl.ds(base, SC_WINDOW)], src_v)
            pltpu.sync_copy(dst_hbm.at[pl.ds(base, SC_WINDOW)], dst_v)
            pltpu.async_copy(table_hbm.at[src_v], rows_v, sem).wait()
            pltpu.async_copy(rows_v, out_hbm.at[dst_v], sem).wait()

    return move_kernel(table, src, dst)


def _combine_dense_kernel(tw_ref, x1_ref, mod_ref, g_ref, y0_ref, y1_ref, y2_ref, y3_ref, o_ref):
    tm = x1_ref.shape[0]
    tw = jnp.concatenate([tw_ref[...], jnp.zeros((LANES - SUBLANES, tm), F32)], axis=0).T
    first = second = None
    for kk, y_ref in enumerate((y0_ref, y1_ref, y2_ref, y3_ref)):
        f_k, s_k = _unpack_bf16_pairs(y_ref[...])
        w_k = tw[:, kk:kk + 1]
        first = w_k * f_k if kk == 0 else first + w_k * f_k
        second = w_k * s_k if kk == 0 else second + w_k * s_k
    ffn = jnp.concatenate([first, second], axis=1)
    x2 = x1_ref[...] + mod_ref[5:6, :] * ffn
    o_ref[...] = x2 * lax.rsqrt(jnp.mean(x2 * x2, axis=-1, keepdims=True) + EPS) * g_ref[...]


def _combine_dense(tw, x1, mod, g_final, yg, seq):
    n = x1.shape[0]
    width = yg.shape[1]
    tiles = n // TOK_TILE
    tiles_per_seq = seq // TOK_TILE
    row = lambda i: (i, 0)
    slot = lambda kk: pl.BlockSpec((TOK_TILE, width), lambda i: (kk * tiles + i, 0))
    return pl.pallas_call(
        _combine_dense_kernel,
        grid=(tiles,),
        in_specs=[
            pl.BlockSpec((SUBLANES, TOK_TILE), lambda i: (0, i)),
            pl.BlockSpec((TOK_TILE, D_MODEL), row),
            pl.BlockSpec((None, SUBLANES, D_MODEL), lambda i: (i // tiles_per_seq, 0, 0)),
            pl.BlockSpec((1, D_MODEL), lambda i: (0, 0)),
        ] + [slot(kk) for kk in range(TOP_K)],
        out_specs=pl.BlockSpec((TOK_TILE, D_MODEL), row),
        out_shape=jax.ShapeDtypeStruct((n, D_MODEL), F32),
        compiler_params=_cparams(1, VMEM_LIMIT),
        name="combine",
    )(tw, x1, mod, g_final.reshape(1, D_MODEL), yg, yg, yg, yg)


def _rope_tables(positions):
    half = ROPE_DIM // 2
    inv_freq = ROPE_THETA ** (-jnp.arange(0, ROPE_DIM, 2, dtype=F32) / ROPE_DIM)
    lane = jnp.arange(LANES) % B_QK_DIM
    freq = inv_freq[lane % half][None, :]
    ang = positions.reshape(-1).astype(F32)[:, None] * freq
    cos, sin = jnp.cos(ang), jnp.sin(ang)
    lane = lane[None, :]
    c = jnp.where(lane < ROPE_DIM, cos, 1.0)
    m = jnp.where(lane < half, -sin, 0.0)
    p = jnp.where(jnp.logical_and(lane >= half, lane < ROPE_DIM), sin, 0.0)
    return c, m, p


def _routing_tables(top_i, rank, counts):
    n = top_i.shape[1]
    top_i = jnp.concatenate([top_i[kk] for kk in range(TOP_K)])
    rank = jnp.concatenate([rank[kk] for kk in range(TOP_K)])
    n_tiles = n * TOP_K // ROW_TILE + N_EXPERTS
    padded = (counts + ROW_TILE - 1) // ROW_TILE * ROW_TILE
    ends = jnp.cumsum(padded)
    starts = ends - padded
    pos = rank
    for e in range(N_EXPERTS):
        pos = pos + jnp.where(top_i == e, starts[e], 0)
    tile_count = (padded // ROW_TILE).astype(jnp.int32)
    tile_start = jnp.concatenate([starts, ends[-1:]]).astype(jnp.int32) // ROW_TILE
    n_rows = n_tiles * ROW_TILE
    n_fill = n_rows - n * TOP_K
    gap_start = jnp.concatenate([starts + counts, ends[-1:]])
    gap_len = jnp.concatenate([padded - counts, n_rows - ends[-1:]])
    gap_end = jnp.cumsum(gap_len)
    m = jnp.arange(n_fill, dtype=jnp.int32)
    in_gap = jnp.logical_and(m[:, None] >= (gap_end - gap_len)[None, :], m[:, None] < gap_end[None, :])
    fill_dst = m + jnp.sum(jnp.where(in_gap, (gap_start - (gap_end - gap_len))[None, :], 0), axis=1)
    pos = pos.astype(jnp.int32)
    move_src = jnp.concatenate([jnp.arange(n * TOP_K, dtype=jnp.int32) % n, m % n])
    move_dst = jnp.concatenate([pos, fill_dst.astype(jnp.int32)])
    return pos, move_src, move_dst, tile_start, tile_count


def kernel(x, c, positions, w_ada, b_ada, g_mix_norm, w_in, rel_bias, lambda_q1, lambda_k1,
           lambda_q2, lambda_k2, g_subln, w_out, g_ffn_norm, w_router, b_router, w_gate_up,
           b_gate_up, w_down, b_down, g_final):
    batch, seq, _ = x.shape
    depth = w_ada.shape[0]
    assert depth == 1, "the combine kernel applies the final norm, so it must follow the only layer"
    n = batch * seq
    rope_c, rope_m, rope_p = _rope_tables(positions)
    x2 = x.reshape(n, D_MODEL)
    for l in range(depth):
        lambda_init = 0.8 - 0.6 * math.exp(-0.3 * l)
        mod = _ada(c, w_ada[l], b_ada[l])
        qa, ka, va, qb, kb, vb = _in_proj(x2, mod, g_mix_norm[l], w_in[l].astype(BF16),
                                          rope_c, rope_m, rope_p, seq)
        oa = _attn_a(qa, ka, va, _rel_bias_rows(rel_bias[l]), batch, seq)
        lam = (jnp.exp(jnp.sum(lambda_q1[l].astype(F32) * lambda_k1[l].astype(F32)))
               - jnp.exp(jnp.sum(lambda_q2[l].astype(F32) * lambda_k2[l].astype(F32)))
               + lambda_init).reshape(1)
        ob = _attn_b(lam, qb, kb, vb, g_subln[l], batch, seq, 1.0 - lambda_init)
        x1, hp, top_i, top_w, rank, counts = _out_route(
            oa, ob, x2, mod, w_out[l].astype(BF16), g_ffn_norm[l], w_router[l].astype(BF16),
            b_router[l], seq)
        pos, move_src, move_dst, tile_start, tile_count = _routing_tables(
            top_i, rank, counts[:, 0])
        xs = _sc_move_rows(hp, move_src, move_dst)
        y = _experts(tile_start, tile_count, xs, w_gate_up[l], b_gate_up[l], w_down[l], b_down[l])
        yg = _sc_gather_rows(y, pos)
        x2 = _combine_dense(top_w, x1, mod, g_final, yg, seq)
    return x2.reshape(batch, seq, D_MODEL)
```

```python
import functools
import math

import jax
import jax.numpy as jnp
import numpy as np
from jax import lax
from jax.experimental import pallas as pl
from jax.experimental.pallas import tpu as pltpu
from jax.experimental.pallas import tpu_sc as plsc

D_MODEL = 1024
CHUNK = 64
HEAD_DIM = 64
A_HEADS = 8
A_WIDTH = A_HEADS * HEAD_DIM
LEFT_CHUNKS = 8
MAX_REL = 128
B_HEADS = 4
B_QK_DIM = HEAD_DIM
B_V_DIM = 2 * HEAD_DIM
B_WIDTH = B_HEADS * B_V_DIM
ROPE_THETA = 500000.0
ROPE_DIM = B_QK_DIM // 4
N_EXPERTS = 32
TOP_K = 4
D_EXPERT = D_MODEL
SWIGLU_LIMIT = 7.0
SWIGLU_ALPHA = 1.702
EPS = 1e-6
NEG_INF = -1e30
LOG2_E = math.log2(math.e)
N_MOD = 6

LANES = 128
SUBLANES = 8
VMEM_BYTES_V7X = 64 * 1024 * 1024
VMEM_LIMIT = VMEM_BYTES_V7X * 7 // 8

TOK_TILE = 512
A_QBLK = 2 * CHUNK
A_BAND = (LEFT_CHUNKS + 2) * CHUNK
A_ROLL = A_BAND + A_QBLK
B_TQ = 512
B_TK = 512
B_SUB = 128
ROW_TILE = 512
FF_CHUNK = 512
SC_WINDOW = 128

F32 = jnp.float32
BF16 = jnp.bfloat16


def _cparams(n_axes, vmem=None):
    return pltpu.CompilerParams(
        dimension_semantics=("arbitrary",) * n_axes,
        vmem_limit_bytes=vmem,
    )


def _ada_kernel(c_ref, w_ref, b_ref, o_ref):
    c = c_ref[...]
    act = c * jax.nn.sigmoid(c)
    o_ref[...] = jnp.dot(act, w_ref[...], preferred_element_type=F32,
                         precision=lax.Precision.HIGHEST) + b_ref[...]


def _ada(c, w_ada, b_ada):
    b = c.shape[0]
    rows = -(-b // SUBLANES) * SUBLANES
    c_pad = jnp.pad(c, ((0, rows - b), (0, 0)))
    n_out = w_ada.shape[1]
    out = pl.pallas_call(
        _ada_kernel,
        grid=(n_out // D_MODEL,),
        in_specs=[
            pl.BlockSpec((rows, D_MODEL), lambda j: (0, 0)),
            pl.BlockSpec((D_MODEL, D_MODEL), lambda j: (0, j)),
            pl.BlockSpec((1, D_MODEL), lambda j: (0, j)),
        ],
        out_specs=pl.BlockSpec((rows, D_MODEL), lambda j: (0, j)),
        out_shape=jax.ShapeDtypeStruct((rows, n_out), F32),
        compiler_params=_cparams(1),
        name="ada",
    )(c_pad, w_ada, b_ada.reshape(1, n_out))
    mod = out[:b].reshape(b, N_MOD, D_MODEL)
    return jnp.pad(mod, ((0, 0), (0, SUBLANES - N_MOD), (0, 0)))


def _in_proj_kernel(x_ref, mod_ref, g_ref, w_ref, rc_ref, rm_ref, rp_ref,
                    qa_ref, ka_ref, va_ref, qb_ref, kb_ref, vb_ref):
    x = x_ref[...]
    mod = mod_ref[...]
    y = x * lax.rsqrt(jnp.mean(x * x, axis=-1, keepdims=True) + EPS) * g_ref[...]
    h = (y * (1.0 + mod[1:2, :]) + mod[0:1, :]).astype(BF16)
    rc, rm, rp = rc_ref[...], rm_ref[...], rp_ref[...]
    q_scale = HEAD_DIM ** -0.5

    def rope(p):
        cols = []
        for s in range(p.shape[1] // LANES):
            v = p[:, s * LANES:(s + 1) * LANES]
            cols.append(v * rc + pltpu.roll(v, LANES - ROPE_DIM // 2, 1) * rm
                        + pltpu.roll(v, ROPE_DIM // 2, 1) * rp)
        return jnp.concatenate(cols, axis=1)

    outs = (qa_ref, ka_ref, va_ref, qb_ref, kb_ref, vb_ref)
    for j, o_ref in enumerate(outs):
        p = jnp.dot(h, w_ref[:, j * A_WIDTH:(j + 1) * A_WIDTH], preferred_element_type=F32)
        if j in (3, 4):
            p = rope(p)
        if j in (0, 3):
            p = p * (q_scale * LOG2_E)
        pb = p.astype(BF16)
        for s in range(A_WIDTH // LANES):
            o_ref[s] = pb[:, s * LANES:(s + 1) * LANES]


def _in_proj(x2, mod, g_mix, w_in_bf, rope_c, rope_m, rope_p, seq):
    n = x2.shape[0]
    tiles_per_seq = seq // TOK_TILE
    row = lambda i: (i, 0)
    fixed = lambda i: (0, 0)
    n_slabs = A_WIDTH // LANES
    out_sd = jax.ShapeDtypeStruct((n_slabs, n, LANES), BF16)
    return pl.pallas_call(
        _in_proj_kernel,
        grid=(n // TOK_TILE,),
        in_specs=[
            pl.BlockSpec((TOK_TILE, D_MODEL), row),
            pl.BlockSpec((None, SUBLANES, D_MODEL), lambda i: (i // tiles_per_seq, 0, 0)),
            pl.BlockSpec((1, D_MODEL), fixed),
            pl.BlockSpec(w_in_bf.shape, fixed),
            pl.BlockSpec((TOK_TILE, LANES), row),
            pl.BlockSpec((TOK_TILE, LANES), row),
            pl.BlockSpec((TOK_TILE, LANES), row),
        ],
        out_specs=[pl.BlockSpec((n_slabs, TOK_TILE, LANES), lambda i: (0, i, 0))] * 6,
        out_shape=[out_sd] * 6,
        compiler_params=_cparams(1, VMEM_LIMIT),
        name="in_proj",
    )(x2, mod, g_mix.reshape(1, D_MODEL), w_in_bf, rope_c, rope_m, rope_p)


def _attn_a_kernel(q_ref, kp_ref, kc_ref, vp_ref, vc_ref, bias_ref, o_ref, k_sc, v_sc, bias_sc):
    g = pl.program_id(1)
    n_pairs, blk, _ = q_ref.shape
    lane = lax.broadcasted_iota(jnp.int32, (A_QBLK, LANES), 1)
    col = lax.broadcasted_iota(jnp.int32, (A_QBLK, A_BAND), 1)
    ones = jnp.ones((A_BAND, LANES), BF16)

    @pl.when(jnp.logical_and(pl.program_id(0) == 0, g == 0))
    def _():
        q_chunk = lax.broadcasted_iota(jnp.int32, (A_QBLK, A_BAND), 0) // CHUNK
        k_chunk = col // CHUNK
        in_band = jnp.logical_and(k_chunk >= q_chunk, k_chunk <= q_chunk + LEFT_CHUNKS)
        for h in range(bias_ref.shape[0]):
            rolled = pltpu.roll(jnp.broadcast_to(bias_ref[h], (A_QBLK, A_ROLL)), 0, 1,
                                stride=1, stride_axis=0)
            bias_sc[h] = jnp.where(in_band, rolled[:, :A_BAND], NEG_INF)

    def pair(p):
        k_buf = k_sc.at[p % 2]
        v_buf = v_sc.at[p % 2]
        k_buf[0:blk, :] = kp_ref[p]
        k_buf[blk:2 * blk, :] = kc_ref[p]
        v_buf[0:blk, :] = vp_ref[p]
        v_buf[blk:2 * blk, :] = vc_ref[p]

        def scores(m, hh):
            r0 = m * A_QBLK
            q = q_ref[p, r0:r0 + A_QBLK, :]
            in_head = (lane < HEAD_DIM) if hh == 0 else (lane >= HEAD_DIM)
            qh = jnp.where(in_head, q, jnp.zeros_like(q))
            return lax.dot_general(qh, k_buf[r0:r0 + A_BAND, :], (((1,), (1,)), ((), ())),
                                   preferred_element_type=F32)

        chains = [(m, hh) for m in range(blk // A_QBLK) for hh in range(2)]
        s_next = scores(*chains[0])
        halves = []
        for i, (m, hh) in enumerate(chains):
            r0 = m * A_QBLK
            s = s_next
            if i + 1 < len(chains):
                s_next = scores(*chains[i + 1])
            valid = jnp.logical_or(g > 0, col + r0 >= blk)
            s = jnp.where(valid, s + bias_sc[2 * p + hh], NEG_INF)
            pr = jnp.exp2(s - jnp.max(s, axis=1, keepdims=True))
            v_ext = jnp.concatenate([v_buf[r0:r0 + A_BAND, :], ones], axis=1)
            pv = jnp.dot(pr.astype(BF16), v_ext, preferred_element_type=F32)
            halves.append(pv[:, :LANES] / pv[:, LANES:])
            if hh == 1:
                o_ref[p, r0:r0 + A_QBLK, :] = jnp.where(lane < HEAD_DIM, halves[0],
                                                        halves[1]).astype(BF16)
                halves = []

    for p in range(n_pairs):
        pair(p)


def _attn_a(qa, ka, va, bias_rows, batch, seq):
    n_pairs, n, _ = qa.shape
    blk = LEFT_CHUNKS * CHUNK
    nblk = seq // blk
    cur = lambda b, g: (0, b * nblk + g, 0)
    prev = lambda b, g: (0, b * nblk + jnp.maximum(g - 1, 0), 0)
    slab = (n_pairs, blk, LANES)
    return pl.pallas_call(
        _attn_a_kernel,
        grid=(batch, nblk),
        in_specs=[
            pl.BlockSpec(slab, cur),
            pl.BlockSpec(slab, prev),
            pl.BlockSpec(slab, cur),
            pl.BlockSpec(slab, prev),
            pl.BlockSpec(slab, cur),
            pl.BlockSpec(bias_rows.shape, lambda b, g: (0, 0, 0)),
        ],
        out_specs=pl.BlockSpec(slab, cur),
        out_shape=jax.ShapeDtypeStruct((n_pairs, n, LANES), BF16),
        scratch_shapes=[
            pltpu.VMEM((2, 2 * blk, LANES), BF16),
            pltpu.VMEM((2, 2 * blk, LANES), BF16),
            pltpu.VMEM((bias_rows.shape[0], A_QBLK, A_BAND), F32),
        ],
        compiler_params=_cparams(2),
        name="attn_a",
    )(qa, ka, ka, va, va, bias_rows)


def _rel_bias_rows(rel_table):
    t = rel_table.astype(F32) * LOG2_E
    far = t[:, 2 * MAX_REL:]
    n_far = LEFT_CHUNKS * CHUNK - MAX_REL
    row = jnp.concatenate([
        jnp.broadcast_to(far, (t.shape[0], n_far)),
        t[:, 2 * MAX_REL:0:-1],
        jnp.broadcast_to(far, (t.shape[0], A_ROLL - A_BAND)),
    ], axis=1)
    return row.reshape(t.shape[0], 1, A_ROLL)


def _attn_b_kernel(lam_ref, q_ref, k_ref, v_ref, g_ref, o_ref, q_sc, s_sc, m_sc, acc_sc, *,
                   out_scale):
    qi = pl.program_id(2)
    n_sub = B_TQ // B_SUB
    lane = lax.broadcasted_iota(jnp.int32, (B_TQ, LANES), 1)
    q = q_ref[...]
    q_sc[0] = jnp.where(lane < B_QK_DIM, q, jnp.zeros_like(q))
    q_sc[1] = jnp.where(lane >= B_QK_DIM, q, jnp.zeros_like(q))
    m_sc[...] = jnp.full(m_sc.shape, NEG_INF, F32)
    acc_sc[...] = jnp.zeros(acc_sc.shape, F32)
    ones = jnp.ones((B_TK, LANES), BF16)

    def score_rows(k, slot, sub, c):
        rows = pl.ds(sub * B_SUB, B_SUB)
        s_sc[slot, c, rows, :] = lax.dot_general(q_sc[c, rows, :], k, (((1,), (1,)), ((), ())),
                                                 preferred_element_type=F32)

    def update_rows(s, v_ext, sub, c):
        rows = pl.ds(sub * B_SUB, B_SUB)
        m_prev = m_sc[c, rows, :]
        m_new = jnp.maximum(m_prev, jnp.max(s, axis=1, keepdims=True))
        alpha = jnp.exp2(m_prev - m_new)
        p = jnp.exp2(s - jnp.concatenate([m_new] * (s.shape[1] // LANES), axis=1))
        pv = jnp.dot(p.astype(BF16), v_ext, preferred_element_type=F32)
        acc_sc[c, rows, :] = jnp.concatenate([alpha, alpha], axis=1) * acc_sc[c, rows, :] + pv
        m_sc[c, rows, :] = m_new

    def key_block(blk):
        return k_ref[pl.ds(pl.multiple_of(blk * B_TK, B_TK), B_TK), :]

    def value_block(blk):
        return jnp.concatenate([v_ref[pl.ds(pl.multiple_of(blk * B_TK, B_TK), B_TK), :], ones], axis=1)

    def step(blk, slot, next_blk):
        v_ext = value_block(blk)
        k_next = key_block(next_blk)
        for sub in range(n_sub):
            for c in range(2):
                update_rows(s_sc[slot, c, pl.ds(sub * B_SUB, B_SUB), :], v_ext, sub, c)
                score_rows(k_next, 1 - slot, sub, c)

    def diagonal_step(slot):
        v_ext = value_block(qi)
        col_c = lax.broadcasted_iota(jnp.int32, (B_SUB, B_TK), 1) // CHUNK
        row_c = lax.broadcasted_iota(jnp.int32, (B_SUB, B_TK), 0) // CHUNK
        for sub in range(n_sub):
            keep = col_c <= row_c + sub * (B_SUB // CHUNK)
            for c in range(2):
                s = s_sc[slot, c, pl.ds(sub * B_SUB, B_SUB), :]
                update_rows(jnp.where(keep, s, NEG_INF), v_ext, sub, c)

    k0 = key_block(0)
    for sub in range(n_sub):
        for c in range(2):
            score_rows(k0, 0, sub, c)

    def pair(p, carry):
        first = 2 * p
        step(first, 0, first + 1)
        step(first + 1, 1, first + 2)
        return carry

    lax.fori_loop(0, qi // 2, pair, 0)

    @pl.when(qi % 2 == 0)
    def _():
        diagonal_step(0)

    @pl.when(qi % 2 == 1)
    def _():
        step(qi - 1, 0, qi)
        diagonal_step(1)

    lam = lam_ref[0]
    a0 = acc_sc[0]
    a1 = acc_sc[1]
    o = a0[:, :B_V_DIM] / a0[:, B_V_DIM:] - lam * (a1[:, :B_V_DIM] / a1[:, B_V_DIM:])
    y = o * lax.rsqrt(jnp.mean(o * o, axis=-1, keepdims=True) + EPS) * g_ref[...]
    o_ref[...] = (y * out_scale).astype(BF16)


def _attn_b(lam, qb, kb, vb, g_subln, batch, seq, out_scale):
    assert B_TQ == B_TK and B_V_DIM == LANES
    n = qb.shape[1]
    nq = seq // B_TQ
    q_map = lambda b, h, qi: (h, b * nq + qi, 0)
    kv_map = lambda b, h, qi: (h, b, 0)
    return pl.pallas_call(
        functools.partial(_attn_b_kernel, out_scale=out_scale),
        grid=(batch, B_HEADS, nq),
        in_specs=[
            pl.BlockSpec(memory_space=pltpu.SMEM),
            pl.BlockSpec((None, B_TQ, LANES), q_map),
            pl.BlockSpec((None, seq, LANES), kv_map),
            pl.BlockSpec((None, seq, LANES), kv_map),
            pl.BlockSpec((1, B_V_DIM), lambda b, h, qi: (0, 0)),
        ],
        out_specs=pl.BlockSpec((None, B_TQ, LANES), q_map),
        out_shape=jax.ShapeDtypeStruct((B_HEADS, n, LANES), BF16),
        scratch_shapes=[
            pltpu.VMEM((2, B_TQ, LANES), BF16),
            pltpu.VMEM((2, 2, B_TQ, B_TK), F32),
            pltpu.VMEM((2, B_TQ, LANES), F32),
            pltpu.VMEM((2, B_TQ, 2 * LANES), F32),
        ],
        compiler_params=_cparams(3),
        name="attn_b",
    )(lam, qb, kb, vb, g_subln.reshape(1, B_V_DIM))


def _pack_bf16_pairs(v):
    half = v.shape[1] // 2
    vb = v.astype(BF16)
    hi = lax.bitcast_convert_type(vb[:, :half].astype(F32), jnp.int32)
    lo = lax.bitcast_convert_type(vb[:, half:].astype(F32), jnp.int32)
    return hi | lax.shift_right_logical(lo, jnp.full(lo.shape, 16, jnp.int32))


def _unpack_bf16_pairs(w):
    first = lax.bitcast_convert_type(w & jnp.int32(-65536), F32)
    second = lax.bitcast_convert_type(lax.shift_left(w, jnp.full(w.shape, 16, jnp.int32)), F32)
    return first, second


def _out_route_kernel(oa_ref, ob_ref, x_ref, mod_ref, wo_ref, g_ref, wr_ref, br_ref,
                      x1_ref, hp_ref, ti_ref, tw_ref, rk_ref, cnt_ref, tri_sc, carry_sc):
    i = pl.program_id(0)
    tm = x_ref.shape[0]

    @pl.when(i == 0)
    def _():
        r = lax.broadcasted_iota(jnp.int32, (tm, tm), 0)
        c = lax.broadcasted_iota(jnp.int32, (tm, tm), 1)
        tri_sc[...] = jnp.where(r < c, 1.0, 0.0).astype(BF16)
        carry_sc[...] = jnp.zeros(carry_sc.shape, F32)

    mod = mod_ref[...]
    o = jnp.concatenate([oa_ref[s] for s in range(oa_ref.shape[0])]
                        + [ob_ref[s] for s in range(ob_ref.shape[0])], axis=1)
    mix = jnp.dot(o, wo_ref[...], preferred_element_type=F32)
    x1 = x_ref[...] + mod[2:3, :] * mix
    x1_ref[...] = x1
    y = x1 * lax.rsqrt(jnp.mean(x1 * x1, axis=-1, keepdims=True) + EPS) * g_ref[...]
    h = y * (1.0 + mod[4:5, :]) + mod[3:4, :]
    hb = h.astype(BF16)
    hp_ref[...] = _pack_bf16_pairs(h)

    logits = lax.dot_general(wr_ref[...], hb, (((1,), (1,)), ((), ())),
                             preferred_element_type=F32) + br_ref[...]
    eid = lax.broadcasted_iota(jnp.int32, logits.shape, 0).astype(F32)
    work = logits
    vals, ids = [], []
    chosen = jnp.zeros(logits.shape, F32)
    for _ in range(TOP_K):
        v = jnp.max(work, axis=0, keepdims=True)
        e = jnp.min(jnp.where(work == v, eid, float(N_EXPERTS)), axis=0, keepdims=True)
        hit = eid == e
        vals.append(v)
        ids.append(e)
        chosen = jnp.where(hit, 1.0, chosen)
        work = jnp.where(hit, -jnp.inf, work)
    ex = [jnp.exp(v - vals[0]) for v in vals]
    den = ex[0] + ex[1] + ex[2] + ex[3]

    before = jnp.dot(chosen.astype(BF16), tri_sc[...], preferred_element_type=F32) + carry_sc[...]
    slot = lax.broadcasted_iota(jnp.int32, (SUBLANES, tm), 0)
    ti = jnp.zeros((SUBLANES, tm), F32)
    tw = jnp.zeros((SUBLANES, tm), F32)
    rk = jnp.zeros((SUBLANES, tm), F32)
    for kk in range(TOP_K):
        r_k = jnp.sum(jnp.where(eid == ids[kk], before, 0.0), axis=0, keepdims=True)
        ti = jnp.where(slot == kk, ids[kk], ti)
        tw = jnp.where(slot == kk, ex[kk] / den, tw)
        rk = jnp.where(slot == kk, r_k, rk)
    ti_ref[...] = ti.astype(jnp.int32)
    tw_ref[...] = tw
    rk_ref[...] = rk.astype(jnp.int32)
    carry = carry_sc[...] + jnp.sum(chosen, axis=1, keepdims=True)
    carry_sc[...] = carry
    cnt_ref[...] = jnp.broadcast_to(carry, cnt_ref.shape).astype(jnp.int32)


def _out_route(oa, ob, x2, mod, w_out_bf, g_ffn, w_router_bf, b_router, seq):
    n = x2.shape[0]
    tiles_per_seq = seq // TOK_TILE
    row = lambda i: (i, 0)
    by_lane = lambda i: (0, i)
    fixed = lambda i: (0, 0)
    return pl.pallas_call(
        _out_route_kernel,
        grid=(n // TOK_TILE,),
        in_specs=[
            pl.BlockSpec((oa.shape[0], TOK_TILE, LANES), lambda i: (0, i, 0)),
            pl.BlockSpec((ob.shape[0], TOK_TILE, LANES), lambda i: (0, i, 0)),
            pl.BlockSpec((TOK_TILE, D_MODEL), row),
            pl.BlockSpec((None, SUBLANES, D_MODEL), lambda i: (i // tiles_per_seq, 0, 0)),
            pl.BlockSpec((D_MODEL, D_MODEL), fixed),
            pl.BlockSpec((1, D_MODEL), fixed),
            pl.BlockSpec((N_EXPERTS, D_MODEL), fixed),
            pl.BlockSpec((N_EXPERTS, 1), fixed),
        ],
        out_specs=[
            pl.BlockSpec((TOK_TILE, D_MODEL), row),
            pl.BlockSpec((TOK_TILE, D_MODEL // 2), row),
            pl.BlockSpec((SUBLANES, TOK_TILE), by_lane),
            pl.BlockSpec((SUBLANES, TOK_TILE), by_lane),
            pl.BlockSpec((SUBLANES, TOK_TILE), by_lane),
            pl.BlockSpec((N_EXPERTS, LANES), fixed),
        ],
        out_shape=[
            jax.ShapeDtypeStruct((n, D_MODEL), F32),
            jax.ShapeDtypeStruct((n, D_MODEL // 2), jnp.int32),
            jax.ShapeDtypeStruct((SUBLANES, n), jnp.int32),
            jax.ShapeDtypeStruct((SUBLANES, n), F32),
            jax.ShapeDtypeStruct((SUBLANES, n), jnp.int32),
            jax.ShapeDtypeStruct((N_EXPERTS, LANES), jnp.int32),
        ],
        scratch_shapes=[pltpu.VMEM((TOK_TILE, TOK_TILE), BF16), pltpu.VMEM((N_EXPERTS, 1), F32)],
        compiler_params=_cparams(1, VMEM_LIMIT),
        name="out_route",
    )(oa, ob, x2, mod, w_out_bf, g_ffn.reshape(1, D_MODEL), w_router_bf.T,
      b_router.reshape(N_EXPERTS, 1))


def _experts_kernel(start_ref, count_ref, xs_ref, wgu_ref, bgu_ref, wd_ref, bd_ref, y_ref,
                    wgu_sc, wd_sc, x_buf, y_buf, x_sem, y_sem):
    e = pl.program_id(0)
    first_tile = start_ref[e]
    n_tiles = count_ref[e]
    n_live = start_ref[N_EXPERTS]

    def x_copy(g, slot):
        return pltpu.make_async_copy(xs_ref.at[pl.ds(g * ROW_TILE, ROW_TILE), :], x_buf.at[slot],
                                     x_sem.at[slot])

    def y_copy(g, slot):
        return pltpu.make_async_copy(y_buf.at[slot], y_ref.at[pl.ds(g * ROW_TILE, ROW_TILE), :],
                                     y_sem.at[slot])

    @pl.when(jnp.logical_and(e == 0, n_live > 0))
    def _():
        x_copy(0, 0).start()

    @pl.when(n_tiles > 0)
    def _():
        wgu_sc[...] = wgu_ref[...].astype(BF16)
        wd_sc[...] = wd_ref[...].astype(BF16)

    def tile(j, carry):
        g = first_tile + j
        slot = g % 2
        x_copy(g, slot).wait()

        @pl.when(g + 1 < n_live)
        def _():
            x_copy(g + 1, 1 - slot).start()

        first, second = _unpack_bf16_pairs(x_buf[slot])
        x = jnp.concatenate([first.astype(BF16), second.astype(BF16)], axis=1)
        acc = jnp.zeros((x.shape[0], D_MODEL), F32)
        for c in range(D_EXPERT // FF_CHUNK):
            lo_c, hi_c = c * FF_CHUNK, (c + 1) * FF_CHUNK
            gate = jnp.dot(x, wgu_sc[:, lo_c:hi_c], preferred_element_type=F32) + bgu_ref[:, lo_c:hi_c]
            up = (jnp.dot(x, wgu_sc[:, D_EXPERT + lo_c:D_EXPERT + hi_c], preferred_element_type=F32)
                  + bgu_ref[:, D_EXPERT + lo_c:D_EXPERT + hi_c])
            gate = jnp.minimum(gate, SWIGLU_LIMIT)
            up = jnp.clip(up, -SWIGLU_LIMIT, SWIGLU_LIMIT)
            act = (up + 1.0) * (gate * jax.nn.sigmoid(SWIGLU_ALPHA * gate))
            acc = acc + jnp.dot(act.astype(BF16), wd_sc[lo_c:hi_c, :], preferred_element_type=F32)

        @pl.when(g >= 2)
        def _():
            y_copy(g - 2, slot).wait()

        y_buf[slot] = _pack_bf16_pairs(acc + bd_ref[...])
        y_copy(g, slot).start()
        return carry

    lax.fori_loop(0, n_tiles, tile, 0)

    @pl.when(e == pl.num_programs(0) - 1)
    def _():
        for back in (2, 1):
            @pl.when(n_live >= back)
            def _():
                y_copy(n_live - back, (n_live - back) % 2).wait()

        y_buf[0] = jnp.zeros(y_buf.shape[1:], jnp.int32)

        def clear(g, carry):
            y_copy(g, 0).start()
            y_copy(g, 0).wait()
            return carry

        lax.fori_loop(n_live, y_ref.shape[0] // ROW_TILE, clear, 0)


def _experts(tile_start, tile_count, xs, w_gate_up, b_gate_up, w_down, b_down):
    n_rows, width = xs.shape
    by_expert = lambda e, ts, tc: (e, 0, 0)
    return pl.pallas_call(
        _experts_kernel,
        grid_spec=pltpu.PrefetchScalarGridSpec(
            num_scalar_prefetch=2,
            grid=(N_EXPERTS,),
            in_specs=[
                pl.BlockSpec(memory_space=pl.ANY),
                pl.BlockSpec((None, D_MODEL, 2 * D_EXPERT), by_expert),
                pl.BlockSpec((None, 1, 2 * D_EXPERT), by_expert),
                pl.BlockSpec((None, D_EXPERT, D_MODEL), by_expert),
                pl.BlockSpec((None, 1, D_MODEL), by_expert),
            ],
            out_specs=pl.BlockSpec(memory_space=pl.ANY),
            scratch_shapes=[
                pltpu.VMEM((D_MODEL, 2 * D_EXPERT), BF16),
                pltpu.VMEM((D_EXPERT, D_MODEL), BF16),
                pltpu.VMEM((2, ROW_TILE, width), jnp.int32),
                pltpu.VMEM((2, ROW_TILE, D_MODEL // 2), jnp.int32),
                pltpu.SemaphoreType.DMA((2,)),
                pltpu.SemaphoreType.DMA((2,)),
            ],
        ),
        out_shape=jax.ShapeDtypeStruct((n_rows, D_MODEL // 2), jnp.int32),
        compiler_params=_cparams(1, VMEM_LIMIT),
        name="experts",
    )(tile_start, tile_count, xs, w_gate_up, b_gate_up.reshape(N_EXPERTS, 1, 2 * D_EXPERT),
      w_down, b_down.reshape(N_EXPERTS, 1, D_MODEL))


def _sc_gather_rows(table, idx):
    m = idx.shape[0]
    width = table.shape[1]
    mesh = plsc.VectorSubcoreMesh(core_axis_name="core", subcore_axis_name="subcore")
    n_workers = mesh.num_cores * mesh.num_subcores
    per_worker = m // n_workers
    assert per_worker * n_workers == m and per_worker % SC_WINDOW == 0

    @pl.kernel(
        out_type=jax.ShapeDtypeStruct((m, width), table.dtype),
        mesh=mesh,
        scratch_types=[
            pltpu.VMEM((SC_WINDOW,), jnp.int32),
            pltpu.VMEM((SC_WINDOW, width), table.dtype),
            pltpu.SemaphoreType.DMA,
        ],
    )
    def gather_kernel(table_hbm, idx_hbm, out_hbm, idx_v, rows_v, sem):
        worker = lax.axis_index("subcore") * mesh.num_cores + lax.axis_index("core")

        @pl.loop(0, per_worker // SC_WINDOW)
        def _(j):
            base = pl.multiple_of(worker * per_worker + j * SC_WINDOW, SC_WINDOW)
            pltpu.sync_copy(idx_hbm.at[pl.ds(base, SC_WINDOW)], idx_v)
            pltpu.async_copy(table_hbm.at[idx_v], rows_v, sem).wait()
            pltpu.sync_copy(rows_v, out_hbm.at[pl.ds(base, SC_WINDOW)])

    return gather_kernel(table, idx)


def _sc_move_rows(table, src, dst):
    m = src.shape[0]
    width = table.shape[1]
    mesh = plsc.VectorSubcoreMesh(core_axis_name="core", subcore_axis_name="subcore")
    n_workers = mesh.num_cores * mesh.num_subcores
    per_worker = m // n_workers
    assert per_worker * n_workers == m and per_worker % SC_WINDOW == 0

    @pl.kernel(
        out_type=jax.ShapeDtypeStruct((m, width), table.dtype),
        mesh=mesh,
        scratch_types=[
            pltpu.VMEM((SC_WINDOW,), jnp.int32),
            pltpu.VMEM((SC_WINDOW,), jnp.int32),
            pltpu.VMEM((SC_WINDOW, width), table.dtype),
            pltpu.SemaphoreType.DMA,
        ],
    )
    def move_kernel(table_hbm, src_hbm, dst_hbm, out_hbm, src_v, dst_v, rows_v, sem):
        worker = lax.axis_index("subcore") * mesh.num_cores + lax.axis_index("core")

        @pl.loop(0, per_worker // SC_WINDOW)
        def _(j):
            base = pl.multiple_of(worker * per_worker + j * SC_WINDOW, SC_WINDOW)
            pltpu.sync_copy(src_hbm.at[pl.ds(base, SC_WINDOW)], src_v)
            pltpu.sync_copy(dst_hbm.at[pl.ds(base, SC_WINDOW)], dst_v)
            pltpu.async_copy(table_hbm.at[src_v], rows_v, sem).wait()
            pltpu.async_copy(rows_v, out_hbm.at[dst_v], sem).wait()

    return move_kernel(table, src, dst)


def _combine_dense_kernel(tw_ref, x1_ref, mod_ref, g_ref, y0_ref, y1_ref, y2_ref, y3_ref, o_ref):
    tm = x1_ref.shape[0]
    tw = jnp.concatenate([tw_ref[...], jnp.zeros((LANES - SUBLANES, tm), F32)], axis=0).T
    first = second = None
    for kk, y_ref in enumerate((y0_ref, y1_ref, y2_ref, y3_ref)):
        f_k, s_k = _unpack_bf16_pairs(y_ref[...])
        w_k = tw[:, kk:kk + 1]
        first = w_k * f_k if kk == 0 else first + w_k * f_k
        second = w_k * s_k if kk == 0 else second + w_k * s_k
    ffn = jnp.concatenate([first, second], axis=1)
    x2 = x1_ref[...] + mod_ref[5:6, :] * ffn
    o_ref[...] = x2 * lax.rsqrt(jnp.mean(x2 * x2, axis=-1, keepdims=True) + EPS) * g_ref[...]


def _combine_dense(tw, x1, mod, g_final, yg, seq):
    n = x1.shape[0]
    width = yg.shape[1]
    tiles = n // TOK_TILE
    tiles_per_seq = seq // TOK_TILE
    row = lambda i: (i, 0)
    slot = lambda kk: pl.BlockSpec((TOK_TILE, width), lambda i: (kk * tiles + i, 0))
    return pl.pallas_call(
        _combine_dense_kernel,
        grid=(tiles,),
        in_specs=[
            pl.BlockSpec((SUBLANES, TOK_TILE), lambda i: (0, i)),
            pl.BlockSpec((TOK_TILE, D_MODEL), row),
            pl.BlockSpec((None, SUBLANES, D_MODEL), lambda i: (i // tiles_per_seq, 0, 0)),
            pl.BlockSpec((1, D_MODEL), lambda i: (0, 0)),
        ] + [slot(kk) for kk in range(TOP_K)],
        out_specs=pl.BlockSpec((TOK_TILE, D_MODEL), row),
        out_shape=jax.ShapeDtypeStruct((n, D_MODEL), F32),
        compiler_params=_cparams(1, VMEM_LIMIT),
        name="combine",
    )(tw, x1, mod, g_final.reshape(1, D_MODEL), yg, yg, yg, yg)


def _rope_tables(positions):
    half = ROPE_DIM // 2
    inv_freq = ROPE_THETA ** (-jnp.arange(0, ROPE_DIM, 2, dtype=F32) / ROPE_DIM)
    ang = positions.reshape(-1).astype(F32)[:, None] * inv_freq
    base = jnp.concatenate([jnp.cos(ang), jnp.sin(ang), jnp.ones_like(ang[:, :1])], axis=1)
    lane = np.arange(LANES) % B_QK_DIM
    expand = np.zeros((3, 2 * half + 1, LANES), np.float32)
    for j in range(LANES):
        if lane[j] < ROPE_DIM:
            expand[0, lane[j] % half, j] = 1.0
            if lane[j] < half:
                expand[1, half + lane[j], j] = -1.0
            else:
                expand[2, half + lane[j] - half, j] = 1.0
        else:
            expand[0, 2 * half, j] = 1.0
    tables = jnp.dot(base, jnp.asarray(expand.transpose(1, 0, 2).reshape(2 * half + 1, 3 * LANES)),
                     precision=lax.Precision.HIGHEST)
    return tables[:, :LANES], tables[:, LANES:2 * LANES], tables[:, 2 * LANES:]


def _positions_kernel(starts_ref, ti_ref, rk_ref, pos_ref):
    ids = ti_ref[...]
    pos = rk_ref[...]
    for e in range(N_EXPERTS):
        pos = pos + jnp.where(ids == e, starts_ref[e], 0)
    pos_ref[...] = pos


def _positions(starts, top_i, rank):
    shape = top_i.shape
    return pl.pallas_call(
        _positions_kernel,
        in_specs=[
            pl.BlockSpec(memory_space=pltpu.SMEM),
            pl.BlockSpec(shape, lambda: (0, 0)),
            pl.BlockSpec(shape, lambda: (0, 0)),
        ],
        out_specs=pl.BlockSpec(shape, lambda: (0, 0)),
        out_shape=jax.ShapeDtypeStruct(shape, jnp.int32),
        name="positions",
    )(starts, top_i, rank)


def _routing_tables(top_i, rank, counts):
    n = top_i.shape[1]
    n_tiles = n * TOP_K // ROW_TILE + N_EXPERTS
    padded = (counts + ROW_TILE - 1) // ROW_TILE * ROW_TILE
    ends = jnp.cumsum(padded)
    starts = ends - padded
    pos = _positions(starts.astype(jnp.int32), top_i, rank)
    pos = jnp.concatenate([pos[kk] for kk in range(TOP_K)])
    tile_count = (padded // ROW_TILE).astype(jnp.int32)
    tile_start = jnp.concatenate([starts, ends[-1:]]).astype(jnp.int32) // ROW_TILE
    n_rows = n_tiles * ROW_TILE
    n_fill = n_rows - n * TOP_K
    gap_start = jnp.concatenate([starts + counts, ends[-1:]])
    gap_len = jnp.concatenate([padded - counts, n_rows - ends[-1:]])
    gap_end = jnp.cumsum(gap_len)
    m = jnp.arange(n_fill, dtype=jnp.int32)
    in_gap = jnp.logical_and(m[:, None] >= (gap_end - gap_len)[None, :], m[:, None] < gap_end[None, :])
    fill_dst = m + jnp.sum(jnp.where(in_gap, (gap_start - (gap_end - gap_len))[None, :], 0), axis=1)
    pos = pos.astype(jnp.int32)
    move_src = jnp.concatenate([jnp.arange(n * TOP_K, dtype=jnp.int32) % n, m % n])
    move_dst = jnp.concatenate([pos, fill_dst.astype(jnp.int32)])
    return pos, move_src, move_dst, tile_start, tile_count


def kernel(x, c, positions, w_ada, b_ada, g_mix_norm, w_in, rel_bias, lambda_q1, lambda_k1,
           lambda_q2, lambda_k2, g_subln, w_out, g_ffn_norm, w_router, b_router, w_gate_up,
           b_gate_up, w_down, b_down, g_final):
    batch, seq, _ = x.shape
    depth = w_ada.shape[0]
    assert depth == 1, "the combine kernel applies the final norm, so it must follow the only layer"
    n = batch * seq
    rope_c, rope_m, rope_p = _rope_tables(positions)
    x2 = x.reshape(n, D_MODEL)
    for l in range(depth):
        lambda_init = 0.8 - 0.6 * math.exp(-0.3 * l)
        mod = _ada(c, w_ada[l], b_ada[l])
        qa, ka, va, qb, kb, vb = _in_proj(x2, mod, g_mix_norm[l], w_in[l].astype(BF16),
                                          rope_c, rope_m, rope_p, seq)
        oa = _attn_a(qa, ka, va, _rel_bias_rows(rel_bias[l]), batch, seq)
        lam = (jnp.exp(jnp.sum(lambda_q1[l].astype(F32) * lambda_k1[l].astype(F32)))
               - jnp.exp(jnp.sum(lambda_q2[l].astype(F32) * lambda_k2[l].astype(F32)))
               + lambda_init).reshape(1)
        ob = _attn_b(lam, qb, kb, vb, g_subln[l], batch, seq, 1.0 - lambda_init)
        x1, hp, top_i, top_w, rank, counts = _out_route(
            oa, ob, x2, mod, w_out[l].astype(BF16), g_ffn_norm[l], w_router[l].astype(BF16),
            b_router[l], seq)
        pos, move_src, move_dst, tile_start, tile_count = _routing_tables(
            top_i, rank, counts[:, 0])
        xs = _sc_move_rows(hp, move_src, move_dst)
        y = _experts(tile_start, tile_count, xs, w_gate_up[l], b_gate_up[l], w_down[l], b_down[l])
        yg = _sc_gather_rows(y, pos)
        x2 = _combine_dense(top_w, x1, mod, g_final, yg, seq)
    return x2.reshape(batch, seq, D_MODEL)
```

```python
import functools
import math

import jax
import jax.numpy as jnp
from jax import lax
from jax.experimental import pallas as pl
from jax.experimental.pallas import tpu as pltpu
from jax.experimental.pallas import tpu_sc as plsc

D_MODEL = 1024
CHUNK = 64
HEAD_DIM = 64
A_HEADS = 8
A_WIDTH = A_HEADS * HEAD_DIM
LEFT_CHUNKS = 8
MAX_REL = 128
B_HEADS = 4
B_QK_DIM = HEAD_DIM
B_V_DIM = 2 * HEAD_DIM
B_WIDTH = B_HEADS * B_V_DIM
ROPE_THETA = 500000.0
ROPE_DIM = B_QK_DIM // 4
N_EXPERTS = 32
TOP_K = 4
D_EXPERT = D_MODEL
SWIGLU_LIMIT = 7.0
SWIGLU_ALPHA = 1.702
EPS = 1e-6
NEG_INF = -1e30
LOG2_E = math.log2(math.e)
N_MOD = 6

LANES = 128
SUBLANES = 8
VMEM_BYTES_V7X = 64 * 1024 * 1024
VMEM_LIMIT = VMEM_BYTES_V7X * 7 // 8

TOK_TILE = 512
A_QBLK = 2 * CHUNK
A_BAND = (LEFT_CHUNKS + 2) * CHUNK
A_ROLL = A_BAND + A_QBLK
B_TQ = 512
B_TK = 512
B_SUB = 128
ROW_TILE = 512
FF_CHUNK = 512
SC_WINDOW = 128

F32 = jnp.float32
BF16 = jnp.bfloat16


def _cparams(n_axes, vmem=None):
    return pltpu.CompilerParams(
        dimension_semantics=("arbitrary",) * n_axes,
        vmem_limit_bytes=vmem,
    )


def _ada_kernel(c_ref, w_ref, b_ref, o_ref):
    c = c_ref[...]
    act = c * jax.nn.sigmoid(c)
    o_ref[...] = jnp.dot(act, w_ref[...], preferred_element_type=F32,
                         precision=lax.Precision.HIGHEST) + b_ref[...]


def _ada(c, w_ada, b_ada):
    b = c.shape[0]
    rows = -(-b // SUBLANES) * SUBLANES
    c_pad = jnp.pad(c, ((0, rows - b), (0, 0)))
    n_out = w_ada.shape[1]
    out = pl.pallas_call(
        _ada_kernel,
        grid=(n_out // D_MODEL,),
        in_specs=[
            pl.BlockSpec((rows, D_MODEL), lambda j: (0, 0)),
            pl.BlockSpec((D_MODEL, D_MODEL), lambda j: (0, j)),
            pl.BlockSpec((1, D_MODEL), lambda j: (0, j)),
        ],
        out_specs=pl.BlockSpec((rows, D_MODEL), lambda j: (0, j)),
        out_shape=jax.ShapeDtypeStruct((rows, n_out), F32),
        compiler_params=_cparams(1),
        name="ada",
    )(c_pad, w_ada, b_ada.reshape(1, n_out))
    mod = out[:b].reshape(b, N_MOD, D_MODEL)
    return jnp.pad(mod, ((0, 0), (0, SUBLANES - N_MOD), (0, 0)))


def _in_proj_kernel(x_ref, mod_ref, g_ref, w_ref, rb_ref,
                    qa_ref, ka_ref, va_ref, qb_ref, kb_ref, vb_ref):
    x = x_ref[...]
    mod = mod_ref[...]
    y = x * lax.rsqrt(jnp.mean(x * x, axis=-1, keepdims=True) + EPS) * g_ref[...]
    h = (y * (1.0 + mod[1:2, :]) + mod[0:1, :]).astype(BF16)
    q_scale = HEAD_DIM ** -0.5

    half = ROPE_DIM // 2
    rb = rb_ref[...]
    lane = lax.broadcasted_iota(jnp.int32, rb.shape, 1)
    cos_lo = jnp.where(lane < half, rb, 0.0)
    sin_hi = jnp.where(jnp.logical_and(lane >= half, lane < ROPE_DIM), rb, 0.0)
    cos_pair = cos_lo + pltpu.roll(cos_lo, half, 1)
    rc = (cos_pair + pltpu.roll(cos_pair, B_QK_DIM, 1)
          + jnp.where(lane % B_QK_DIM >= ROPE_DIM, 1.0, 0.0))
    sin_lo = pltpu.roll(sin_hi, LANES - half, 1)
    rm = -(sin_lo + pltpu.roll(sin_lo, B_QK_DIM, 1))
    rp = sin_hi + pltpu.roll(sin_hi, B_QK_DIM, 1)

    def rope(p):
        cols = []
        for s in range(p.shape[1] // LANES):
            v = p[:, s * LANES:(s + 1) * LANES]
            cols.append(v * rc + pltpu.roll(v, LANES - ROPE_DIM // 2, 1) * rm
                        + pltpu.roll(v, ROPE_DIM // 2, 1) * rp)
        return jnp.concatenate(cols, axis=1)

    outs = (qa_ref, ka_ref, va_ref, qb_ref, kb_ref, vb_ref)
    for j, o_ref in enumerate(outs):
        p = jnp.dot(h, w_ref[:, j * A_WIDTH:(j + 1) * A_WIDTH], preferred_element_type=F32)
        if j in (3, 4):
            p = rope(p)
        if j in (0, 3):
            p = p * (q_scale * LOG2_E)
        pb = p.astype(BF16)
        for s in range(A_WIDTH // LANES):
            o_ref[s] = pb[:, s * LANES:(s + 1) * LANES]


def _in_proj(x2, mod, g_mix, w_in_bf, rope_base, seq):
    n = x2.shape[0]
    tiles_per_seq = seq // TOK_TILE
    row = lambda i: (i, 0)
    fixed = lambda i: (0, 0)
    n_slabs = A_WIDTH // LANES
    out_sd = jax.ShapeDtypeStruct((n_slabs, n, LANES), BF16)
    return pl.pallas_call(
        _in_proj_kernel,
        grid=(n // TOK_TILE,),
        in_specs=[
            pl.BlockSpec((TOK_TILE, D_MODEL), row),
            pl.BlockSpec((None, SUBLANES, D_MODEL), lambda i: (i // tiles_per_seq, 0, 0)),
            pl.BlockSpec((1, D_MODEL), fixed),
            pl.BlockSpec(w_in_bf.shape, fixed),
            pl.BlockSpec((TOK_TILE, LANES), row),
        ],
        out_specs=[pl.BlockSpec((n_slabs, TOK_TILE, LANES), lambda i: (0, i, 0))] * 6,
        out_shape=[out_sd] * 6,
        compiler_params=_cparams(1, VMEM_LIMIT),
        name="in_proj",
    )(x2, mod, g_mix.reshape(1, D_MODEL), w_in_bf, rope_base)


def _attn_a_kernel(q_ref, kp_ref, kc_ref, vp_ref, vc_ref, bias_ref, o_ref, k_sc, v_sc, bias_sc):
    g = pl.program_id(1)
    n_pairs, blk, _ = q_ref.shape
    lane = lax.broadcasted_iota(jnp.int32, (A_QBLK, LANES), 1)
    col = lax.broadcasted_iota(jnp.int32, (A_QBLK, A_BAND), 1)
    ones = jnp.ones((A_BAND, LANES), BF16)

    @pl.when(jnp.logical_and(pl.program_id(0) == 0, g == 0))
    def _():
        q_chunk = lax.broadcasted_iota(jnp.int32, (A_QBLK, A_BAND), 0) // CHUNK
        k_chunk = col // CHUNK
        in_band = jnp.logical_and(k_chunk >= q_chunk, k_chunk <= q_chunk + LEFT_CHUNKS)
        for h in range(bias_ref.shape[0]):
            rolled = pltpu.roll(jnp.broadcast_to(bias_ref[h], (A_QBLK, A_ROLL)), 0, 1,
                                stride=1, stride_axis=0)
            bias_sc[h] = jnp.where(in_band, rolled[:, :A_BAND], NEG_INF)

    def pair(p):
        k_buf = k_sc.at[p % 2]
        v_buf = v_sc.at[p % 2]
        k_buf[0:blk, :] = kp_ref[p]
        k_buf[blk:2 * blk, :] = kc_ref[p]
        v_buf[0:blk, :] = vp_ref[p]
        v_buf[blk:2 * blk, :] = vc_ref[p]

        def scores(m, hh):
            r0 = m * A_QBLK
            q = q_ref[p, r0:r0 + A_QBLK, :]
            in_head = (lane < HEAD_DIM) if hh == 0 else (lane >= HEAD_DIM)
            qh = jnp.where(in_head, q, jnp.zeros_like(q))
            return lax.dot_general(qh, k_buf[r0:r0 + A_BAND, :], (((1,), (1,)), ((), ())),
                                   preferred_element_type=F32)

        chains = [(m, hh) for m in range(blk // A_QBLK) for hh in range(2)]
        s_next = scores(*chains[0])
        halves = []
        for i, (m, hh) in enumerate(chains):
            r0 = m * A_QBLK
            s = s_next
            if i + 1 < len(chains):
                s_next = scores(*chains[i + 1])
            valid = jnp.logical_or(g > 0, col + r0 >= blk)
            s = jnp.where(valid, s + bias_sc[2 * p + hh], NEG_INF)
            pr = jnp.exp2(s - jnp.max(s, axis=1, keepdims=True))
            v_ext = jnp.concatenate([v_buf[r0:r0 + A_BAND, :], ones], axis=1)
            pv = jnp.dot(pr.astype(BF16), v_ext, preferred_element_type=F32)
            halves.append(pv[:, :LANES] / pv[:, LANES:])
            if hh == 1:
                o_ref[p, r0:r0 + A_QBLK, :] = jnp.where(lane < HEAD_DIM, halves[0],
                                                        halves[1]).astype(BF16)
                halves = []

    for p in range(n_pairs):
        pair(p)


def _attn_a(qa, ka, va, bias_rows, batch, seq):
    n_pairs, n, _ = qa.shape
    blk = LEFT_CHUNKS * CHUNK
    nblk = seq // blk
    cur = lambda b, g: (0, b * nblk + g, 0)
    prev = lambda b, g: (0, b * nblk + jnp.maximum(g - 1, 0), 0)
    slab = (n_pairs, blk, LANES)
    return pl.pallas_call(
        _attn_a_kernel,
        grid=(batch, nblk),
        in_specs=[
            pl.BlockSpec(slab, cur),
            pl.BlockSpec(slab, prev),
            pl.BlockSpec(slab, cur),
            pl.BlockSpec(slab, prev),
            pl.BlockSpec(slab, cur),
            pl.BlockSpec(bias_rows.shape, lambda b, g: (0, 0, 0)),
        ],
        out_specs=pl.BlockSpec(slab, cur),
        out_shape=jax.ShapeDtypeStruct((n_pairs, n, LANES), BF16),
        scratch_shapes=[
            pltpu.VMEM((2, 2 * blk, LANES), BF16),
            pltpu.VMEM((2, 2 * blk, LANES), BF16),
            pltpu.VMEM((bias_rows.shape[0], A_QBLK, A_BAND), F32),
        ],
        compiler_params=_cparams(2),
        name="attn_a",
    )(qa, ka, ka, va, va, bias_rows)


def _rel_bias_rows(rel_table):
    t = rel_table.astype(F32) * LOG2_E
    far = t[:, 2 * MAX_REL:]
    n_far = LEFT_CHUNKS * CHUNK - MAX_REL
    row = jnp.concatenate([
        jnp.broadcast_to(far, (t.shape[0], n_far)),
        t[:, 2 * MAX_REL:0:-1],
        jnp.broadcast_to(far, (t.shape[0], A_ROLL - A_BAND)),
    ], axis=1)
    return row.reshape(t.shape[0], 1, A_ROLL)


def _attn_b_kernel(lam_ref, q_ref, k_ref, v_ref, g_ref, o_ref, q_sc, s_sc, m_sc, acc_sc, *,
                   out_scale):
    qi = pl.program_id(2)
    n_sub = B_TQ // B_SUB
    lane = lax.broadcasted_iota(jnp.int32, (B_TQ, LANES), 1)
    q = q_ref[...]
    q_sc[0] = jnp.where(lane < B_QK_DIM, q, jnp.zeros_like(q))
    q_sc[1] = jnp.where(lane >= B_QK_DIM, q, jnp.zeros_like(q))
    m_sc[...] = jnp.full(m_sc.shape, NEG_INF, F32)
    acc_sc[...] = jnp.zeros(acc_sc.shape, F32)
    ones = jnp.ones((B_TK, LANES), BF16)

    def score_rows(k, slot, sub, c):
        rows = pl.ds(sub * B_SUB, B_SUB)
        s_sc[slot, c, rows, :] = lax.dot_general(q_sc[c, rows, :], k, (((1,), (1,)), ((), ())),
                                                 preferred_element_type=F32)

    def update_rows(s, v_ext, sub, c):
        rows = pl.ds(sub * B_SUB, B_SUB)
        m_prev = m_sc[c, rows, :]
        m_new = jnp.maximum(m_prev, jnp.max(s, axis=1, keepdims=True))
        alpha = jnp.exp2(m_prev - m_new)
        p = jnp.exp2(s - jnp.concatenate([m_new] * (s.shape[1] // LANES), axis=1))
        pv = jnp.dot(p.astype(BF16), v_ext, preferred_element_type=F32)
        acc_sc[c, rows, :] = jnp.concatenate([alpha, alpha], axis=1) * acc_sc[c, rows, :] + pv
        m_sc[c, rows, :] = m_new

    def key_block(blk):
        return k_ref[pl.ds(pl.multiple_of(blk * B_TK, B_TK), B_TK), :]

    def value_block(blk):
        return jnp.concatenate([v_ref[pl.ds(pl.multiple_of(blk * B_TK, B_TK), B_TK), :], ones], axis=1)

    def step(blk, slot, next_blk):
        v_ext = value_block(blk)
        k_next = key_block(next_blk)
        for sub in range(n_sub):
            for c in range(2):
                update_rows(s_sc[slot, c, pl.ds(sub * B_SUB, B_SUB), :], v_ext, sub, c)
                score_rows(k_next, 1 - slot, sub, c)

    def diagonal_step(slot):
        v_ext = value_block(qi)
        col_c = lax.broadcasted_iota(jnp.int32, (B_SUB, B_TK), 1) // CHUNK
        row_c = lax.broadcasted_iota(jnp.int32, (B_SUB, B_TK), 0) // CHUNK
        for sub in range(n_sub):
            keep = col_c <= row_c + sub * (B_SUB // CHUNK)
            for c in range(2):
                s = s_sc[slot, c, pl.ds(sub * B_SUB, B_SUB), :]
                update_rows(jnp.where(keep, s, NEG_INF), v_ext, sub, c)

    k0 = key_block(0)
    for sub in range(n_sub):
        for c in range(2):
            score_rows(k0, 0, sub, c)

    def pair(p, carry):
        first = 2 * p
        step(first, 0, first + 1)
        step(first + 1, 1, first + 2)
        return carry

    lax.fori_loop(0, qi // 2, pair, 0)

    @pl.when(qi % 2 == 0)
    def _():
        diagonal_step(0)

    @pl.when(qi % 2 == 1)
    def _():
        step(qi - 1, 0, qi)
        diagonal_step(1)

    lam = lam_ref[0]
    a0 = acc_sc[0]
    a1 = acc_sc[1]
    o = a0[:, :B_V_DIM] / a0[:, B_V_DIM:] - lam * (a1[:, :B_V_DIM] / a1[:, B_V_DIM:])
    y = o * lax.rsqrt(jnp.mean(o * o, axis=-1, keepdims=True) + EPS) * g_ref[...]
    o_ref[...] = (y * out_scale).astype(BF16)


def _attn_b(lam, qb, kb, vb, g_subln, batch, seq, out_scale):
    assert B_TQ == B_TK and B_V_DIM == LANES
    n = qb.shape[1]
    nq = seq // B_TQ
    q_map = lambda b, h, qi: (h, b * nq + qi, 0)
    kv_map = lambda b, h, qi: (h, b, 0)
    return pl.pallas_call(
        functools.partial(_attn_b_kernel, out_scale=out_scale),
        grid=(batch, B_HEADS, nq),
        in_specs=[
            pl.BlockSpec(memory_space=pltpu.SMEM),
            pl.BlockSpec((None, B_TQ, LANES), q_map),
            pl.BlockSpec((None, seq, LANES), kv_map),
            pl.BlockSpec((None, seq, LANES), kv_map),
            pl.BlockSpec((1, B_V_DIM), lambda b, h, qi: (0, 0)),
        ],
        out_specs=pl.BlockSpec((None, B_TQ, LANES), q_map),
        out_shape=jax.ShapeDtypeStruct((B_HEADS, n, LANES), BF16),
        scratch_shapes=[
            pltpu.VMEM((2, B_TQ, LANES), BF16),
            pltpu.VMEM((2, 2, B_TQ, B_TK), F32),
            pltpu.VMEM((2, B_TQ, LANES), F32),
            pltpu.VMEM((2, B_TQ, 2 * LANES), F32),
        ],
        compiler_params=_cparams(3),
        name="attn_b",
    )(lam, qb, kb, vb, g_subln.reshape(1, B_V_DIM))


def _pack_bf16_pairs(v):
    half = v.shape[1] // 2
    vb = v.astype(BF16)
    hi = lax.bitcast_convert_type(vb[:, :half].astype(F32), jnp.int32)
    lo = lax.bitcast_convert_type(vb[:, half:].astype(F32), jnp.int32)
    return hi | lax.shift_right_logical(lo, jnp.full(lo.shape, 16, jnp.int32))


def _unpack_bf16_pairs(w):
    first = lax.bitcast_convert_type(w & jnp.int32(-65536), F32)
    second = lax.bitcast_convert_type(lax.shift_left(w, jnp.full(w.shape, 16, jnp.int32)), F32)
    return first, second


def _out_route_kernel(oa_ref, ob_ref, x_ref, mod_ref, wo_ref, g_ref, wr_ref, br_ref,
                      x1_ref, hp_ref, ti_ref, tw_ref, rk_ref, cnt_ref, tri_sc, carry_sc):
    i = pl.program_id(0)
    tm = x_ref.shape[0]

    @pl.when(i == 0)
    def _():
        r = lax.broadcasted_iota(jnp.int32, (tm, tm), 0)
        c = lax.broadcasted_iota(jnp.int32, (tm, tm), 1)
        tri_sc[...] = jnp.where(r < c, 1.0, 0.0).astype(BF16)
        carry_sc[...] = jnp.zeros(carry_sc.shape, F32)

    mod = mod_ref[...]
    o = jnp.concatenate([oa_ref[s] for s in range(oa_ref.shape[0])]
                        + [ob_ref[s] for s in range(ob_ref.shape[0])], axis=1)
    mix = jnp.dot(o, wo_ref[...], preferred_element_type=F32)
    x1 = x_ref[...] + mod[2:3, :] * mix
    x1_ref[...] = x1
    y = x1 * lax.rsqrt(jnp.mean(x1 * x1, axis=-1, keepdims=True) + EPS) * g_ref[...]
    h = y * (1.0 + mod[4:5, :]) + mod[3:4, :]
    hb = h.astype(BF16)
    hp_ref[...] = _pack_bf16_pairs(h)

    logits = lax.dot_general(wr_ref[...], hb, (((1,), (1,)), ((), ())),
                             preferred_element_type=F32) + br_ref[...]
    eid = lax.broadcasted_iota(jnp.int32, logits.shape, 0).astype(F32)
    work = logits
    vals, ids = [], []
    chosen = jnp.zeros(logits.shape, F32)
    for _ in range(TOP_K):
        v = jnp.max(work, axis=0, keepdims=True)
        e = jnp.min(jnp.where(work == v, eid, float(N_EXPERTS)), axis=0, keepdims=True)
        hit = eid == e
        vals.append(v)
        ids.append(e)
        chosen = jnp.where(hit, 1.0, chosen)
        work = jnp.where(hit, -jnp.inf, work)
    ex = [jnp.exp(v - vals[0]) for v in vals]
    den = ex[0] + ex[1] + ex[2] + ex[3]

    before = jnp.dot(chosen.astype(BF16), tri_sc[...], preferred_element_type=F32) + carry_sc[...]
    slot = lax.broadcasted_iota(jnp.int32, (SUBLANES, tm), 0)
    ti = jnp.zeros((SUBLANES, tm), F32)
    tw = jnp.zeros((SUBLANES, tm), F32)
    rk = jnp.zeros((SUBLANES, tm), F32)
    for kk in range(TOP_K):
        r_k = jnp.sum(jnp.where(eid == ids[kk], before, 0.0), axis=0, keepdims=True)
        ti = jnp.where(slot == kk, ids[kk], ti)
        tw = jnp.where(slot == kk, ex[kk] / den, tw)
        rk = jnp.where(slot == kk, r_k, rk)
    ti_ref[...] = ti.astype(jnp.int32)
    tw_ref[...] = tw
    rk_ref[...] = rk.astype(jnp.int32)
    carry = carry_sc[...] + jnp.sum(chosen, axis=1, keepdims=True)
    carry_sc[...] = carry
    cnt_ref[...] = jnp.broadcast_to(carry, cnt_ref.shape).astype(jnp.int32)


def _out_route(oa, ob, x2, mod, w_out_bf, g_ffn, w_router_bf, b_router, seq):
    n = x2.shape[0]
    tiles_per_seq = seq // TOK_TILE
    row = lambda i: (i, 0)
    by_lane = lambda i: (0, i)
    fixed = lambda i: (0, 0)
    return pl.pallas_call(
        _out_route_kernel,
        grid=(n // TOK_TILE,),
        in_specs=[
            pl.BlockSpec((oa.shape[0], TOK_TILE, LANES), lambda i: (0, i, 0)),
            pl.BlockSpec((ob.shape[0], TOK_TILE, LANES), lambda i: (0, i, 0)),
            pl.BlockSpec((TOK_TILE, D_MODEL), row),
            pl.BlockSpec((None, SUBLANES, D_MODEL), lambda i: (i // tiles_per_seq, 0, 0)),
            pl.BlockSpec((D_MODEL, D_MODEL), fixed),
            pl.BlockSpec((1, D_MODEL), fixed),
            pl.BlockSpec((N_EXPERTS, D_MODEL), fixed),
            pl.BlockSpec((N_EXPERTS, 1), fixed),
        ],
        out_specs=[
            pl.BlockSpec((TOK_TILE, D_MODEL), row),
            pl.BlockSpec((TOK_TILE, D_MODEL // 2), row),
            pl.BlockSpec((SUBLANES, TOK_TILE), by_lane),
            pl.BlockSpec((SUBLANES, TOK_TILE), by_lane),
            pl.BlockSpec((SUBLANES, TOK_TILE), by_lane),
            pl.BlockSpec((N_EXPERTS, LANES), fixed),
        ],
        out_shape=[
            jax.ShapeDtypeStruct((n, D_MODEL), F32),
            jax.ShapeDtypeStruct((n, D_MODEL // 2), jnp.int32),
            jax.ShapeDtypeStruct((SUBLANES, n), jnp.int32),
            jax.ShapeDtypeStruct((SUBLANES, n), F32),
            jax.ShapeDtypeStruct((SUBLANES, n), jnp.int32),
            jax.ShapeDtypeStruct((N_EXPERTS, LANES), jnp.int32),
        ],
        scratch_shapes=[pltpu.VMEM((TOK_TILE, TOK_TILE), BF16), pltpu.VMEM((N_EXPERTS, 1), F32)],
        compiler_params=_cparams(1, VMEM_LIMIT),
        name="out_route",
    )(oa, ob, x2, mod, w_out_bf, g_ffn.reshape(1, D_MODEL), w_router_bf.T,
      b_router.reshape(N_EXPERTS, 1))


def _experts_kernel(start_ref, count_ref, xs_ref, wgu_ref, bgu_ref, wd_ref, bd_ref, y_ref,
                    wgu_sc, wd_sc, x_buf, y_buf, x_sem, y_sem):
    e = pl.program_id(0)
    first_tile = start_ref[e]
    n_tiles = count_ref[e]
    n_live = start_ref[N_EXPERTS]

    def x_copy(g, slot):
        return pltpu.make_async_copy(xs_ref.at[pl.ds(g * ROW_TILE, ROW_TILE), :], x_buf.at[slot],
                                     x_sem.at[slot])

    def y_copy(g, slot):
        return pltpu.make_async_copy(y_buf.at[slot], y_ref.at[pl.ds(g * ROW_TILE, ROW_TILE), :],
                                     y_sem.at[slot])

    @pl.when(jnp.logical_and(e == 0, n_live > 0))
    def _():
        x_copy(0, 0).start()

    @pl.when(n_tiles > 0)
    def _():
        wgu_sc[...] = wgu_ref[...].astype(BF16)
        wd_sc[...] = wd_ref[...].astype(BF16)

    def tile(j, carry):
        g = first_tile + j
        slot = g % 2
        x_copy(g, slot).wait()

        @pl.when(g + 1 < n_live)
        def _():
            x_copy(g + 1, 1 - slot).start()

        first, second = _unpack_bf16_pairs(x_buf[slot])
        x = jnp.concatenate([first.astype(BF16), second.astype(BF16)], axis=1)
        acc = jnp.zeros((x.shape[0], D_MODEL), F32)
        for c in range(D_EXPERT // FF_CHUNK):
            lo_c, hi_c = c * FF_CHUNK, (c + 1) * FF_CHUNK
            gate = jnp.dot(x, wgu_sc[:, lo_c:hi_c], preferred_element_type=F32) + bgu_ref[:, lo_c:hi_c]
            up = (jnp.dot(x, wgu_sc[:, D_EXPERT + lo_c:D_EXPERT + hi_c], preferred_element_type=F32)
                  + bgu_ref[:, D_EXPERT + lo_c:D_EXPERT + hi_c])
            gate = jnp.minimum(gate, SWIGLU_LIMIT)
            up = jnp.clip(up, -SWIGLU_LIMIT, SWIGLU_LIMIT)
            act = (up + 1.0) * (gate * jax.nn.sigmoid(SWIGLU_ALPHA * gate))
            acc = acc + jnp.dot(act.astype(BF16), wd_sc[lo_c:hi_c, :], preferred_element_type=F32)

        @pl.when(g >= 2)
        def _():
            y_copy(g - 2, slot).wait()

        y_buf[slot] = _pack_bf16_pairs(acc + bd_ref[...])
        y_copy(g, slot).start()
        return carry

    lax.fori_loop(0, n_tiles, tile, 0)

    @pl.when(e == pl.num_programs(0) - 1)
    def _():
        for back in (2, 1):
            @pl.when(n_live >= back)
            def _():
                y_copy(n_live - back, (n_live - back) % 2).wait()

        y_buf[0] = jnp.zeros(y_buf.shape[1:], jnp.int32)

        def clear(g, carry):
            y_copy(g, 0).start()
            y_copy(g, 0).wait()
            return carry

        lax.fori_loop(n_live, y_ref.shape[0] // ROW_TILE, clear, 0)


def _experts(tile_start, tile_count, xs, w_gate_up, b_gate_up, w_down, b_down):
    n_rows, width = xs.shape
    by_expert = lambda e, ts, tc: (e, 0, 0)
    return pl.pallas_call(
        _experts_kernel,
        grid_spec=pltpu.PrefetchScalarGridSpec(
            num_scalar_prefetch=2,
            grid=(N_EXPERTS,),
            in_specs=[
                pl.BlockSpec(memory_space=pl.ANY),
                pl.BlockSpec((None, D_MODEL, 2 * D_EXPERT), by_expert),
                pl.BlockSpec((None, 1, 2 * D_EXPERT), by_expert),
                pl.BlockSpec((None, D_EXPERT, D_MODEL), by_expert),
                pl.BlockSpec((None, 1, D_MODEL), by_expert),
            ],
            out_specs=pl.BlockSpec(memory_space=pl.ANY),
            scratch_shapes=[
                pltpu.VMEM((D_MODEL, 2 * D_EXPERT), BF16),
                pltpu.VMEM((D_EXPERT, D_MODEL), BF16),
                pltpu.VMEM((2, ROW_TILE, width), jnp.int32),
                pltpu.VMEM((2, ROW_TILE, D_MODEL // 2), jnp.int32),
                pltpu.SemaphoreType.DMA((2,)),
                pltpu.SemaphoreType.DMA((2,)),
            ],
        ),
        out_shape=jax.ShapeDtypeStruct((n_rows, D_MODEL // 2), jnp.int32),
        compiler_params=_cparams(1, VMEM_LIMIT),
        name="experts",
    )(tile_start, tile_count, xs, w_gate_up, b_gate_up.reshape(N_EXPERTS, 1, 2 * D_EXPERT),
      w_down, b_down.reshape(N_EXPERTS, 1, D_MODEL))


def _sc_gather_rows(table, idx):
    m = idx.shape[0]
    width = table.shape[1]
    mesh = plsc.VectorSubcoreMesh(core_axis_name="core", subcore_axis_name="subcore")
    n_workers = mesh.num_cores * mesh.num_subcores
    per_worker = m // n_workers
    assert per_worker * n_workers == m and per_worker % SC_WINDOW == 0

    @pl.kernel(
        out_type=jax.ShapeDtypeStruct((m, width), table.dtype),
        mesh=mesh,
        scratch_types=[
            pltpu.VMEM((SC_WINDOW,), jnp.int32),
            pltpu.VMEM((SC_WINDOW, width), table.dtype),
            pltpu.SemaphoreType.DMA,
        ],
    )
    def gather_kernel(table_hbm, idx_hbm, out_hbm, idx_v, rows_v, sem):
        worker = lax.axis_index("subcore") * mesh.num_cores + lax.axis_index("core")

        @pl.loop(0, per_worker // SC_WINDOW)
        def _(j):
            base = pl.multiple_of(worker * per_worker + j * SC_WINDOW, SC_WINDOW)
            pltpu.sync_copy(idx_hbm.at[pl.ds(base, SC_WINDOW)], idx_v)
            pltpu.async_copy(table_hbm.at[idx_v], rows_v, sem).wait()
            pltpu.sync_copy(rows_v, out_hbm.at[pl.ds(base, SC_WINDOW)])

    return gather_kernel(table, idx)


def _sc_move_rows(table, src, dst):
    m = src.shape[0]
    width = table.shape[1]
    mesh = plsc.VectorSubcoreMesh(core_axis_name="core", subcore_axis_name="subcore")
    n_workers = mesh.num_cores * mesh.num_subcores
    per_worker = m // n_workers
    assert per_worker * n_workers == m and per_worker % SC_WINDOW == 0

    @pl.kernel(
        out_type=jax.ShapeDtypeStruct((m, width), table.dtype),
        mesh=mesh,
        scratch_types=[
            pltpu.VMEM((SC_WINDOW,), jnp.int32),
            pltpu.VMEM((SC_WINDOW,), jnp.int32),
            pltpu.VMEM((SC_WINDOW, width), table.dtype),
            pltpu.SemaphoreType.DMA,
        ],
    )
    def move_kernel(table_hbm, src_hbm, dst_hbm, out_hbm, src_v, dst_v, rows_v, sem):
        worker = lax.axis_index("subcore") * mesh.num_cores + lax.axis_index("core")

        @pl.loop(0, per_worker // SC_WINDOW)
        def _(j):
            base = pl.multiple_of(worker * per_worker + j * SC_WINDOW, SC_WINDOW)
            pltpu.sync_copy(src_hbm.at[pl.ds(base, SC_WINDOW)], src_v)
            pltpu.sync_copy(dst_hbm.at[pl.ds(base, SC_WINDOW)], dst_v)
            pltpu.async_copy(table_hbm.at[src_v], rows_v, sem).wait()
            pltpu.async_copy(rows_v, out_hbm.at[dst_v], sem).wait()

    return move_kernel(table, src, dst)


def _combine_dense_kernel(tw_ref, x1_ref, mod_ref, g_ref, y0_ref, y1_ref, y2_ref, y3_ref, o_ref):
    tm = x1_ref.shape[0]
    tw = jnp.concatenate([tw_ref[...], jnp.zeros((LANES - SUBLANES, tm), F32)], axis=0).T
    first = second = None
    for kk, y_ref in enumerate((y0_ref, y1_ref, y2_ref, y3_ref)):
        f_k, s_k = _unpack_bf16_pairs(y_ref[...])
        w_k = tw[:, kk:kk + 1]
        first = w_k * f_k if kk == 0 else first + w_k * f_k
        second = w_k * s_k if kk == 0 else second + w_k * s_k
    ffn = jnp.concatenate([first, second], axis=1)
    x2 = x1_ref[...] + mod_ref[5:6, :] * ffn
    o_ref[...] = x2 * lax.rsqrt(jnp.mean(x2 * x2, axis=-1, keepdims=True) + EPS) * g_ref[...]


def _combine_dense(tw, x1, mod, g_final, yg, seq):
    n = x1.shape[0]
    width = yg.shape[1]
    tiles = n // TOK_TILE
    tiles_per_seq = seq // TOK_TILE
    row = lambda i: (i, 0)
    slot = lambda kk: pl.BlockSpec((TOK_TILE, width), lambda i: (kk * tiles + i, 0))
    return pl.pallas_call(
        _combine_dense_kernel,
        grid=(tiles,),
        in_specs=[
            pl.BlockSpec((SUBLANES, TOK_TILE), lambda i: (0, i)),
            pl.BlockSpec((TOK_TILE, D_MODEL), row),
            pl.BlockSpec((None, SUBLANES, D_MODEL), lambda i: (i // tiles_per_seq, 0, 0)),
            pl.BlockSpec((1, D_MODEL), lambda i: (0, 0)),
        ] + [slot(kk) for kk in range(TOP_K)],
        out_specs=pl.BlockSpec((TOK_TILE, D_MODEL), row),
        out_shape=jax.ShapeDtypeStruct((n, D_MODEL), F32),
        compiler_params=_cparams(1, VMEM_LIMIT),
        name="combine",
    )(tw, x1, mod, g_final.reshape(1, D_MODEL), yg, yg, yg, yg)


def _rope_base(positions):
    inv_freq = ROPE_THETA ** (-jnp.arange(0, ROPE_DIM, 2, dtype=F32) / ROPE_DIM)
    ang = positions.reshape(-1).astype(F32)[:, None] * inv_freq
    pad = jnp.zeros((ang.shape[0], LANES - ROPE_DIM), F32)
    return jnp.concatenate([jnp.cos(ang), jnp.sin(ang), pad], axis=1)


def _positions_kernel(starts_ref, ti_ref, rk_ref, pos_ref):
    ids = ti_ref[...]
    pos = rk_ref[...]
    for e in range(N_EXPERTS):
        pos = pos + jnp.where(ids == e, starts_ref[e], 0)
    pos_ref[...] = pos


def _positions(starts, top_i, rank):
    shape = top_i.shape
    return pl.pallas_call(
        _positions_kernel,
        in_specs=[
            pl.BlockSpec(memory_space=pltpu.SMEM),
            pl.BlockSpec(shape, lambda: (0, 0)),
            pl.BlockSpec(shape, lambda: (0, 0)),
        ],
        out_specs=pl.BlockSpec(shape, lambda: (0, 0)),
        out_shape=jax.ShapeDtypeStruct(shape, jnp.int32),
        name="positions",
    )(starts, top_i, rank)


def _routing_tables(top_i, rank, counts):
    n = top_i.shape[1]
    n_tiles = n * TOP_K // ROW_TILE + N_EXPERTS
    padded = (counts + ROW_TILE - 1) // ROW_TILE * ROW_TILE
    ends = jnp.cumsum(padded)
    starts = ends - padded
    pos = _positions(starts.astype(jnp.int32), top_i, rank)
    pos = jnp.concatenate([pos[kk] for kk in range(TOP_K)])
    tile_count = (padded // ROW_TILE).astype(jnp.int32)
    tile_start = jnp.concatenate([starts, ends[-1:]]).astype(jnp.int32) // ROW_TILE
    n_rows = n_tiles * ROW_TILE
    n_fill = n_rows - n * TOP_K
    gap_start = jnp.concatenate([starts + counts, ends[-1:]])
    gap_len = jnp.concatenate([padded - counts, n_rows - ends[-1:]])
    gap_end = jnp.cumsum(gap_len)
    m = jnp.arange(n_fill, dtype=jnp.int32)
    in_gap = jnp.logical_and(m[:, None] >= (gap_end - gap_len)[None, :], m[:, None] < gap_end[None, :])
    fill_dst = m + jnp.sum(jnp.where(in_gap, (gap_start - (gap_end - gap_len))[None, :], 0), axis=1)
    pos = pos.astype(jnp.int32)
    move_src = jnp.concatenate([jnp.arange(n * TOP_K, dtype=jnp.int32) % n, m % n])
    move_dst = jnp.concatenate([pos, fill_dst.astype(jnp.int32)])
    return pos, move_src, move_dst, tile_start, tile_count


def kernel(x, c, positions, w_ada, b_ada, g_mix_norm, w_in, rel_bias, lambda_q1, lambda_k1,
           lambda_q2, lambda_k2, g_subln, w_out, g_ffn_norm, w_router, b_router, w_gate_up,
           b_gate_up, w_down, b_down, g_final):
    batch, seq, _ = x.shape
    depth = w_ada.shape[0]
    assert depth == 1, "the combine kernel applies the final norm, so it must follow the only layer"
    n = batch * seq
    rope_base = _rope_base(positions)
    x2 = x.reshape(n, D_MODEL)
    for l in range(depth):
        lambda_init = 0.8 - 0.6 * math.exp(-0.3 * l)
        mod = _ada(c, w_ada[l], b_ada[l])
        qa, ka, va, qb, kb, vb = _in_proj(x2, mod, g_mix_norm[l], w_in[l].astype(BF16),
                                          rope_base, seq)
        oa = _attn_a(qa, ka, va, _rel_bias_rows(rel_bias[l]), batch, seq)
        lam = (jnp.exp(jnp.sum(lambda_q1[l].astype(F32) * lambda_k1[l].astype(F32)))
               - jnp.exp(jnp.sum(lambda_q2[l].astype(F32) * lambda_k2[l].astype(F32)))
               + lambda_init).reshape(1)
        ob = _attn_b(lam, qb, kb, vb, g_subln[l], batch, seq, 1.0 - lambda_init)
        x1, hp, top_i, top_w, rank, counts = _out_route(
            oa, ob, x2, mod, w_out[l].astype(BF16), g_ffn_norm[l], w_router[l].astype(BF16),
            b_router[l], seq)
        pos, move_src, move_dst, tile_start, tile_count = _routing_tables(
            top_i, rank, counts[:, 0])
        xs = _sc_move_rows(hp, move_src, move_dst)
        y = _experts(tile_start, tile_count, xs, w_gate_up[l], b_gate_up[l], w_down[l], b_down[l])
        yg = _sc_gather_rows(y, pos)
        x2 = _combine_dense(top_w, x1, mod, g_final, yg, seq)
    return x2.reshape(batch, seq, D_MODEL)
```

```python
import functools
import math

import jax
import jax.numpy as jnp
from jax import lax
from jax.experimental import pallas as pl
from jax.experimental.pallas import tpu as pltpu
from jax.experimental.pallas import tpu_sc as plsc

D_MODEL = 1024
CHUNK = 64
HEAD_DIM = 64
A_HEADS = 8
A_WIDTH = A_HEADS * HEAD_DIM
LEFT_CHUNKS = 8
MAX_REL = 128
B_HEADS = 4
B_QK_DIM = HEAD_DIM
B_V_DIM = 2 * HEAD_DIM
B_WIDTH = B_HEADS * B_V_DIM
ROPE_THETA = 500000.0
ROPE_DIM = B_QK_DIM // 4
N_EXPERTS = 32
TOP_K = 4
D_EXPERT = D_MODEL
SWIGLU_LIMIT = 7.0
SWIGLU_ALPHA = 1.702
EPS = 1e-6
NEG_INF = -1e30
LOG2_E = math.log2(math.e)
N_MOD = 6

LANES = 128
SUBLANES = 8
VMEM_BYTES_V7X = 64 * 1024 * 1024
VMEM_LIMIT = VMEM_BYTES_V7X * 7 // 8

TOK_TILE = 512
A_QBLK = 2 * CHUNK
A_BAND = (LEFT_CHUNKS + 2) * CHUNK
A_ROLL = A_BAND + A_QBLK
B_TQ = 512
B_TK = 512
B_SUB = 128
ROW_TILE = 512
FF_CHUNK = 512
SC_WINDOW = 128

F32 = jnp.float32
BF16 = jnp.bfloat16


def _cparams(n_axes, vmem=None):
    return pltpu.CompilerParams(
        dimension_semantics=("arbitrary",) * n_axes,
        vmem_limit_bytes=vmem,
    )


def _ada_kernel(c_ref, w_ref, b_ref, o_ref):
    c = c_ref[...]
    act = c * jax.nn.sigmoid(c)
    o_ref[...] = jnp.dot(act, w_ref[...], preferred_element_type=F32,
                         precision=lax.Precision.HIGHEST) + b_ref[...]


def _ada(c, w_ada, b_ada):
    b = c.shape[0]
    rows = -(-b // SUBLANES) * SUBLANES
    c_pad = jnp.pad(c, ((0, rows - b), (0, 0)))
    n_out = w_ada.shape[1]
    out = pl.pallas_call(
        _ada_kernel,
        grid=(n_out // D_MODEL,),
        in_specs=[
            pl.BlockSpec((rows, D_MODEL), lambda j: (0, 0)),
            pl.BlockSpec((D_MODEL, D_MODEL), lambda j: (0, j)),
            pl.BlockSpec((1, D_MODEL), lambda j: (0, j)),
        ],
        out_specs=pl.BlockSpec((rows, D_MODEL), lambda j: (0, j)),
        out_shape=jax.ShapeDtypeStruct((rows, n_out), F32),
        compiler_params=_cparams(1),
        name="ada",
    )(c_pad, w_ada, b_ada.reshape(1, n_out))
    mod = out[:b].reshape(b, N_MOD, D_MODEL)
    return jnp.pad(mod, ((0, 0), (0, SUBLANES - N_MOD), (0, 0)))


def _in_proj_kernel(x_ref, mod_ref, g_ref, w_ref, rb_ref,
                    qa_ref, ka_ref, va_ref, qb_ref, kb_ref, vb_ref):
    x = x_ref[...]
    mod = mod_ref[...]
    y = x * lax.rsqrt(jnp.mean(x * x, axis=-1, keepdims=True) + EPS) * g_ref[...]
    h = (y * (1.0 + mod[1:2, :]) + mod[0:1, :]).astype(BF16)
    q_scale = HEAD_DIM ** -0.5

    half = ROPE_DIM // 2
    rb = rb_ref[...]
    lane = lax.broadcasted_iota(jnp.int32, rb.shape, 1)
    cos_lo = jnp.where(lane < half, rb, 0.0)
    sin_hi = jnp.where(jnp.logical_and(lane >= half, lane < ROPE_DIM), rb, 0.0)
    cos_pair = cos_lo + pltpu.roll(cos_lo, half, 1)
    rc = (cos_pair + pltpu.roll(cos_pair, B_QK_DIM, 1)
          + jnp.where(lane % B_QK_DIM >= ROPE_DIM, 1.0, 0.0))
    sin_lo = pltpu.roll(sin_hi, LANES - half, 1)
    rm = -(sin_lo + pltpu.roll(sin_lo, B_QK_DIM, 1))
    rp = sin_hi + pltpu.roll(sin_hi, B_QK_DIM, 1)

    def rope(p):
        cols = []
        for s in range(p.shape[1] // LANES):
            v = p[:, s * LANES:(s + 1) * LANES]
            cols.append(v * rc + pltpu.roll(v, LANES - ROPE_DIM // 2, 1) * rm
                        + pltpu.roll(v, ROPE_DIM // 2, 1) * rp)
        return jnp.concatenate(cols, axis=1)

    outs = (qa_ref, ka_ref, va_ref, qb_ref, kb_ref, vb_ref)
    for j, o_ref in enumerate(outs):
        p = jnp.dot(h, w_ref[:, j * A_WIDTH:(j + 1) * A_WIDTH], preferred_element_type=F32)
        if j in (3, 4):
            p = rope(p)
        if j in (0, 3):
            p = p * (q_scale * LOG2_E)
        pb = p.astype(BF16)
        for s in range(A_WIDTH // LANES):
            o_ref[s] = pb[:, s * LANES:(s + 1) * LANES]


def _in_proj(x2, mod, g_mix, w_in_bf, rope_base, seq):
    n = x2.shape[0]
    tiles_per_seq = seq // TOK_TILE
    row = lambda i: (i, 0)
    fixed = lambda i: (0, 0)
    n_slabs = A_WIDTH // LANES
    out_sd = jax.ShapeDtypeStruct((n_slabs, n, LANES), BF16)
    return pl.pallas_call(
        _in_proj_kernel,
        grid=(n // TOK_TILE,),
        in_specs=[
            pl.BlockSpec((TOK_TILE, D_MODEL), row),
            pl.BlockSpec((None, SUBLANES, D_MODEL), lambda i: (i // tiles_per_seq, 0, 0)),
            pl.BlockSpec((1, D_MODEL), fixed),
            pl.BlockSpec(w_in_bf.shape, fixed),
            pl.BlockSpec((TOK_TILE, LANES), row),
        ],
        out_specs=[pl.BlockSpec((n_slabs, TOK_TILE, LANES), lambda i: (0, i, 0))] * 6,
        out_shape=[out_sd] * 6,
        compiler_params=_cparams(1, VMEM_LIMIT),
        name="in_proj",
    )(x2, mod, g_mix.reshape(1, D_MODEL), w_in_bf, rope_base)


def _attn_a_kernel(q_ref, kp_ref, kc_ref, vp_ref, vc_ref, bias_ref, o_ref, k_sc, v_sc, bias_sc):
    g = pl.program_id(1)
    n_pairs, blk, _ = q_ref.shape
    lane = lax.broadcasted_iota(jnp.int32, (A_QBLK, LANES), 1)
    col = lax.broadcasted_iota(jnp.int32, (A_QBLK, A_BAND), 1)
    ones = jnp.ones((A_BAND, LANES), BF16)

    @pl.when(jnp.logical_and(pl.program_id(0) == 0, g == 0))
    def _():
        q_chunk = lax.broadcasted_iota(jnp.int32, (A_QBLK, A_BAND), 0) // CHUNK
        k_chunk = col // CHUNK
        in_band = jnp.logical_and(k_chunk >= q_chunk, k_chunk <= q_chunk + LEFT_CHUNKS)
        for h in range(bias_ref.shape[0]):
            rolled = pltpu.roll(jnp.broadcast_to(bias_ref[h], (A_QBLK, A_ROLL)), 0, 1,
                                stride=1, stride_axis=0)
            bias_sc[h] = jnp.where(in_band, rolled[:, :A_BAND], NEG_INF)

    def pair(p):
        k_buf = k_sc.at[p % 2]
        v_buf = v_sc.at[p % 2]
        k_buf[0:blk, :] = kp_ref[p]
        k_buf[blk:2 * blk, :] = kc_ref[p]
        v_buf[0:blk, :] = vp_ref[p]
        v_buf[blk:2 * blk, :] = vc_ref[p]

        def scores(m, hh):
            r0 = m * A_QBLK
            q = q_ref[p, r0:r0 + A_QBLK, :]
            in_head = (lane < HEAD_DIM) if hh == 0 else (lane >= HEAD_DIM)
            qh = jnp.where(in_head, q, jnp.zeros_like(q))
            return lax.dot_general(qh, k_buf[r0:r0 + A_BAND, :], (((1,), (1,)), ((), ())),
                                   preferred_element_type=F32)

        chains = [(m, hh) for m in range(blk // A_QBLK) for hh in range(2)]
        s_next = scores(*chains[0])
        halves = []
        for i, (m, hh) in enumerate(chains):
            r0 = m * A_QBLK
            s = s_next
            if i + 1 < len(chains):
                s_next = scores(*chains[i + 1])
            valid = jnp.logical_or(g > 0, col + r0 >= blk)
            s = jnp.where(valid, s + bias_sc[2 * p + hh], NEG_INF)
            pr = jnp.exp2(s - jnp.max(s, axis=1, keepdims=True))
            v_ext = jnp.concatenate([v_buf[r0:r0 + A_BAND, :], ones], axis=1)
            pv = jnp.dot(pr.astype(BF16), v_ext, preferred_element_type=F32)
            halves.append(pv[:, :LANES] / pv[:, LANES:])
            if hh == 1:
                o_ref[p, r0:r0 + A_QBLK, :] = jnp.where(lane < HEAD_DIM, halves[0],
                                                        halves[1]).astype(BF16)
                halves = []

    for p in range(n_pairs):
        pair(p)


def _attn_a(qa, ka, va, bias_rows, batch, seq):
    n_pairs, n, _ = qa.shape
    blk = LEFT_CHUNKS * CHUNK
    nblk = seq // blk
    cur = lambda b, g: (0, b * nblk + g, 0)
    prev = lambda b, g: (0, b * nblk + jnp.maximum(g - 1, 0), 0)
    slab = (n_pairs, blk, LANES)
    return pl.pallas_call(
        _attn_a_kernel,
        grid=(batch, nblk),
        in_specs=[
            pl.BlockSpec(slab, cur),
            pl.BlockSpec(slab, prev),
            pl.BlockSpec(slab, cur),
            pl.BlockSpec(slab, prev),
            pl.BlockSpec(slab, cur),
            pl.BlockSpec(bias_rows.shape, lambda b, g: (0, 0, 0)),
        ],
        out_specs=pl.BlockSpec(slab, cur),
        out_shape=jax.ShapeDtypeStruct((n_pairs, n, LANES), BF16),
        scratch_shapes=[
            pltpu.VMEM((2, 2 * blk, LANES), BF16),
            pltpu.VMEM((2, 2 * blk, LANES), BF16),
            pltpu.VMEM((bias_rows.shape[0], A_QBLK, A_BAND), F32),
        ],
        compiler_params=_cparams(2),
        name="attn_a",
    )(qa, ka, ka, va, va, bias_rows)


def _rel_bias_rows(rel_table):
    t = rel_table.astype(F32) * LOG2_E
    far = t[:, 2 * MAX_REL:]
    n_far = LEFT_CHUNKS * CHUNK - MAX_REL
    row = jnp.concatenate([
        jnp.broadcast_to(far, (t.shape[0], n_far)),
        t[:, 2 * MAX_REL:0:-1],
        jnp.broadcast_to(far, (t.shape[0], A_ROLL - A_BAND)),
    ], axis=1)
    return row.reshape(t.shape[0], 1, A_ROLL)


def _attn_b_kernel(lam_ref, q_ref, k_ref, v_ref, g_ref, o_ref, q_sc, s_sc, m_sc, acc_sc, *,
                   out_scale):
    qi = pl.program_id(2)
    n_sub = B_TQ // B_SUB
    lane = lax.broadcasted_iota(jnp.int32, (B_TQ, LANES), 1)
    q = q_ref[...]
    q_sc[0] = jnp.where(lane < B_QK_DIM, q, jnp.zeros_like(q))
    q_sc[1] = jnp.where(lane >= B_QK_DIM, q, jnp.zeros_like(q))
    m_sc[...] = jnp.full(m_sc.shape, NEG_INF, F32)
    acc_sc[...] = jnp.zeros(acc_sc.shape, F32)
    ones = jnp.ones((B_TK, LANES), BF16)

    def score_rows(k, slot, sub, c):
        rows = pl.ds(sub * B_SUB, B_SUB)
        s_sc[slot, c, rows, :] = lax.dot_general(q_sc[c, rows, :], k, (((1,), (1,)), ((), ())),
                                                 preferred_element_type=F32)

    def update_rows(s, v_ext, sub, c):
        rows = pl.ds(sub * B_SUB, B_SUB)
        m_prev = m_sc[c, rows, :]
        m_new = jnp.maximum(m_prev, jnp.max(s, axis=1, keepdims=True))
        alpha = jnp.exp2(m_prev - m_new)
        p = jnp.exp2(s - jnp.concatenate([m_new] * (s.shape[1] // LANES), axis=1))
        pv = jnp.dot(p.astype(BF16), v_ext, preferred_element_type=F32)
        acc_sc[c, rows, :] = jnp.concatenate([alpha, alpha], axis=1) * acc_sc[c, rows, :] + pv
        m_sc[c, rows, :] = m_new

    def key_block(blk):
        return k_ref[pl.ds(pl.multiple_of(blk * B_TK, B_TK), B_TK), :]

    def value_block(blk):
        return jnp.concatenate([v_ref[pl.ds(pl.multiple_of(blk * B_TK, B_TK), B_TK), :], ones], axis=1)

    def step(blk, slot, next_blk):
        v_ext = value_block(blk)
        k_next = key_block(next_blk)
        for sub in range(n_sub):
            for c in range(2):
                update_rows(s_sc[slot, c, pl.ds(sub * B_SUB, B_SUB), :], v_ext, sub, c)
                score_rows(k_next, 1 - slot, sub, c)

    def diagonal_step(slot):
        v_ext = value_block(qi)
        col_c = lax.broadcasted_iota(jnp.int32, (B_SUB, B_TK), 1) // CHUNK
        row_c = lax.broadcasted_iota(jnp.int32, (B_SUB, B_TK), 0) // CHUNK
        for sub in range(n_sub):
            keep = col_c <= row_c + sub * (B_SUB // CHUNK)
            for c in range(2):
                s = s_sc[slot, c, pl.ds(sub * B_SUB, B_SUB), :]
                update_rows(jnp.where(keep, s, NEG_INF), v_ext, sub, c)

    k0 = key_block(0)
    for sub in range(n_sub):
        for c in range(2):
            score_rows(k0, 0, sub, c)

    def pair(p, carry):
        first = 2 * p
        step(first, 0, first + 1)
        step(first + 1, 1, first + 2)
        return carry

    lax.fori_loop(0, qi // 2, pair, 0)

    @pl.when(qi % 2 == 0)
    def _():
        diagonal_step(0)

    @pl.when(qi % 2 == 1)
    def _():
        step(qi - 1, 0, qi)
        diagonal_step(1)

    lam = lam_ref[0]
    a0 = acc_sc[0]
    a1 = acc_sc[1]
    o = a0[:, :B_V_DIM] / a0[:, B_V_DIM:] - lam * (a1[:, :B_V_DIM] / a1[:, B_V_DIM:])
    y = o * lax.rsqrt(jnp.mean(o * o, axis=-1, keepdims=True) + EPS) * g_ref[...]
    o_ref[...] = (y * out_scale).astype(BF16)


def _attn_b(lam, qb, kb, vb, g_subln, batch, seq, out_scale):
    assert B_TQ == B_TK and B_V_DIM == LANES
    n = qb.shape[1]
    nq = seq // B_TQ
    q_map = lambda b, h, qi: (h, b * nq + qi, 0)
    kv_map = lambda b, h, qi: (h, b, 0)
    return pl.pallas_call(
        functools.partial(_attn_b_kernel, out_scale=out_scale),
        grid=(batch, B_HEADS, nq),
        in_specs=[
            pl.BlockSpec(memory_space=pltpu.SMEM),
            pl.BlockSpec((None, B_TQ, LANES), q_map),
            pl.BlockSpec((None, seq, LANES), kv_map),
            pl.BlockSpec((None, seq, LANES), kv_map),
            pl.BlockSpec((1, B_V_DIM), lambda b, h, qi: (0, 0)),
        ],
        out_specs=pl.BlockSpec((None, B_TQ, LANES), q_map),
        out_shape=jax.ShapeDtypeStruct((B_HEADS, n, LANES), BF16),
        scratch_shapes=[
            pltpu.VMEM((2, B_TQ, LANES), BF16),
            pltpu.VMEM((2, 2, B_TQ, B_TK), F32),
            pltpu.VMEM((2, B_TQ, LANES), F32),
            pltpu.VMEM((2, B_TQ, 2 * LANES), F32),
        ],
        compiler_params=_cparams(3),
        name="attn_b",
    )(lam, qb, kb, vb, g_subln.reshape(1, B_V_DIM))


def _pack_bf16_pairs(v):
    half = v.shape[1] // 2
    vb = v.astype(BF16)
    hi = lax.bitcast_convert_type(vb[:, :half].astype(F32), jnp.int32)
    lo = lax.bitcast_convert_type(vb[:, half:].astype(F32), jnp.int32)
    return hi | lax.shift_right_logical(lo, jnp.full(lo.shape, 16, jnp.int32))


def _unpack_bf16_pairs(w):
    first = lax.bitcast_convert_type(w & jnp.int32(-65536), F32)
    second = lax.bitcast_convert_type(lax.shift_left(w, jnp.full(w.shape, 16, jnp.int32)), F32)
    return first, second


def _out_route_kernel(oa_ref, ob_ref, x_ref, mod_ref, wo_ref, g_ref, wr_ref, br_ref,
                      x1_ref, hp_ref, ti_ref, tw_ref, rk_ref, cnt_ref, tri_sc, carry_sc):
    i = pl.program_id(0)
    tm = x_ref.shape[0]

    @pl.when(i == 0)
    def _():
        r = lax.broadcasted_iota(jnp.int32, (tm, tm), 0)
        c = lax.broadcasted_iota(jnp.int32, (tm, tm), 1)
        tri_sc[...] = jnp.where(r < c, 1.0, 0.0).astype(BF16)
        carry_sc[...] = jnp.zeros(carry_sc.shape, F32)

    mod = mod_ref[...]
    o = jnp.concatenate([oa_ref[s] for s in range(oa_ref.shape[0])]
                        + [ob_ref[s] for s in range(ob_ref.shape[0])], axis=1)
    mix = jnp.dot(o, wo_ref[...], preferred_element_type=F32)
    x1 = x_ref[...] + mod[2:3, :] * mix
    x1_ref[...] = x1
    y = x1 * lax.rsqrt(jnp.mean(x1 * x1, axis=-1, keepdims=True) + EPS) * g_ref[...]
    h = y * (1.0 + mod[4:5, :]) + mod[3:4, :]
    hb = h.astype(BF16)
    hp_ref[...] = _pack_bf16_pairs(h)

    logits = lax.dot_general(wr_ref[...], hb, (((1,), (1,)), ((), ())),
                             preferred_element_type=F32) + br_ref[...]
    eid = lax.broadcasted_iota(jnp.int32, logits.shape, 0).astype(F32)
    work = logits
    vals, ids = [], []
    chosen = jnp.zeros(logits.shape, F32)
    for _ in range(TOP_K):
        v = jnp.max(work, axis=0, keepdims=True)
        e = jnp.min(jnp.where(work == v, eid, float(N_EXPERTS)), axis=0, keepdims=True)
        hit = eid == e
        vals.append(v)
        ids.append(e)
        chosen = jnp.where(hit, 1.0, chosen)
        work = jnp.where(hit, -jnp.inf, work)
    ex = [jnp.exp(v - vals[0]) for v in vals]
    den = ex[0] + ex[1] + ex[2] + ex[3]

    before = jnp.dot(chosen.astype(BF16), tri_sc[...], preferred_element_type=F32) + carry_sc[...]
    slot = lax.broadcasted_iota(jnp.int32, (SUBLANES, tm), 0)
    ti = jnp.zeros((SUBLANES, tm), F32)
    tw = jnp.zeros((SUBLANES, tm), F32)
    rk = jnp.zeros((SUBLANES, tm), F32)
    for kk in range(TOP_K):
        r_k = jnp.sum(jnp.where(eid == ids[kk], before, 0.0), axis=0, keepdims=True)
        ti = jnp.where(slot == kk, ids[kk], ti)
        tw = jnp.where(slot == kk, ex[kk] / den, tw)
        rk = jnp.where(slot == kk, r_k, rk)
    ti_ref[...] = ti.astype(jnp.int32)
    tw_ref[...] = tw
    rk_ref[...] = rk.astype(jnp.int32)
    carry = carry_sc[...] + jnp.sum(chosen, axis=1, keepdims=True)
    carry_sc[...] = carry
    cnt_ref[...] = jnp.broadcast_to(carry, cnt_ref.shape).astype(jnp.int32)


def _out_route(oa, ob, x2, mod, w_out_bf, g_ffn, w_router_bf, b_router, seq):
    n = x2.shape[0]
    tiles_per_seq = seq // TOK_TILE
    row = lambda i: (i, 0)
    by_lane = lambda i: (0, i)
    fixed = lambda i: (0, 0)
    return pl.pallas_call(
        _out_route_kernel,
        grid=(n // TOK_TILE,),
        in_specs=[
            pl.BlockSpec((oa.shape[0], TOK_TILE, LANES), lambda i: (0, i, 0)),
            pl.BlockSpec((ob.shape[0], TOK_TILE, LANES), lambda i: (0, i, 0)),
            pl.BlockSpec((TOK_TILE, D_MODEL), row),
            pl.BlockSpec((None, SUBLANES, D_MODEL), lambda i: (i // tiles_per_seq, 0, 0)),
            pl.BlockSpec((D_MODEL, D_MODEL), fixed),
            pl.BlockSpec((1, D_MODEL), fixed),
            pl.BlockSpec((N_EXPERTS, D_MODEL), fixed),
            pl.BlockSpec((N_EXPERTS, 1), fixed),
        ],
        out_specs=[
            pl.BlockSpec((TOK_TILE, D_MODEL), row),
            pl.BlockSpec((TOK_TILE, D_MODEL // 2), row),
            pl.BlockSpec((SUBLANES, TOK_TILE), by_lane),
            pl.BlockSpec((SUBLANES, TOK_TILE), by_lane),
            pl.BlockSpec((SUBLANES, TOK_TILE), by_lane),
            pl.BlockSpec((N_EXPERTS, LANES), fixed),
        ],
        out_shape=[
            jax.ShapeDtypeStruct((n, D_MODEL), F32),
            jax.ShapeDtypeStruct((n, D_MODEL // 2), jnp.int32),
            jax.ShapeDtypeStruct((SUBLANES, n), jnp.int32),
            jax.ShapeDtypeStruct((SUBLANES, n), F32),
            jax.ShapeDtypeStruct((SUBLANES, n), jnp.int32),
            jax.ShapeDtypeStruct((N_EXPERTS, LANES), jnp.int32),
        ],
        scratch_shapes=[pltpu.VMEM((TOK_TILE, TOK_TILE), BF16), pltpu.VMEM((N_EXPERTS, 1), F32)],
        compiler_params=_cparams(1, VMEM_LIMIT),
        name="out_route",
    )(oa, ob, x2, mod, w_out_bf, g_ffn.reshape(1, D_MODEL), w_router_bf.T,
      b_router.reshape(N_EXPERTS, 1))


def _experts_kernel(start_ref, count_ref, xs_ref, wgu_ref, bgu_ref, wd_ref, bd_ref, y_ref,
                    wgu_sc, wd_sc, x_buf, y_buf, x_sem, y_sem):
    e = pl.program_id(0)
    first_tile = start_ref[e]
    n_tiles = count_ref[e]
    n_live = start_ref[N_EXPERTS]

    def x_copy(g, slot):
        return pltpu.make_async_copy(xs_ref.at[pl.ds(g * ROW_TILE, ROW_TILE), :], x_buf.at[slot],
                                     x_sem.at[slot])

    def y_copy(g, slot):
        return pltpu.make_async_copy(y_buf.at[slot], y_ref.at[pl.ds(g * ROW_TILE, ROW_TILE), :],
                                     y_sem.at[slot])

    @pl.when(jnp.logical_and(e == 0, n_live > 0))
    def _():
        x_copy(0, 0).start()

    @pl.when(n_tiles > 0)
    def _():
        wgu_sc[...] = wgu_ref[...].astype(BF16)
        wd_sc[...] = wd_ref[...].astype(BF16)

    def tile(j, carry):
        g = first_tile + j
        slot = g % 2
        x_copy(g, slot).wait()

        @pl.when(g + 1 < n_live)
        def _():
            x_copy(g + 1, 1 - slot).start()

        first, second = _unpack_bf16_pairs(x_buf[slot])
        x = jnp.concatenate([first.astype(BF16), second.astype(BF16)], axis=1)
        acc = jnp.zeros((x.shape[0], D_MODEL), F32)
        for c in range(D_EXPERT // FF_CHUNK):
            lo_c, hi_c = c * FF_CHUNK, (c + 1) * FF_CHUNK
            gate = jnp.dot(x, wgu_sc[:, lo_c:hi_c], preferred_element_type=F32) + bgu_ref[:, lo_c:hi_c]
            up = (jnp.dot(x, wgu_sc[:, D_EXPERT + lo_c:D_EXPERT + hi_c], preferred_element_type=F32)
                  + bgu_ref[:, D_EXPERT + lo_c:D_EXPERT + hi_c])
            gate = jnp.minimum(gate, SWIGLU_LIMIT)
            up = jnp.clip(up, -SWIGLU_LIMIT, SWIGLU_LIMIT)
            act = (up + 1.0) * (gate * jax.nn.sigmoid(SWIGLU_ALPHA * gate))
            acc = acc + jnp.dot(act.astype(BF16), wd_sc[lo_c:hi_c, :], preferred_element_type=F32)

        @pl.when(g >= 2)
        def _():
            y_copy(g - 2, slot).wait()

        y_buf[slot] = _pack_bf16_pairs(acc + bd_ref[...])
        y_copy(g, slot).start()
        return carry

    lax.fori_loop(0, n_tiles, tile, 0)

    @pl.when(e == pl.num_programs(0) - 1)
    def _():
        for back in (2, 1):
            @pl.when(n_live >= back)
            def _():
                y_copy(n_live - back, (n_live - back) % 2).wait()

        y_buf[0] = jnp.zeros(y_buf.shape[1:], jnp.int32)

        def clear(g, carry):
            y_copy(g, 0).start()
            y_copy(g, 0).wait()
            return carry

        lax.fori_loop(n_live, y_ref.shape[0] // ROW_TILE, clear, 0)


def _experts(tile_start, tile_count, xs, w_gate_up, b_gate_up, w_down, b_down):
    n_rows, width = xs.shape
    by_expert = lambda e, ts, tc: (e, 0, 0)
    return pl.pallas_call(
        _experts_kernel,
        grid_spec=pltpu.PrefetchScalarGridSpec(
            num_scalar_prefetch=2,
            grid=(N_EXPERTS,),
            in_specs=[
                pl.BlockSpec(memory_space=pl.ANY),
                pl.BlockSpec((None, D_MODEL, 2 * D_EXPERT), by_expert),
                pl.BlockSpec((None, 1, 2 * D_EXPERT), by_expert),
                pl.BlockSpec((None, D_EXPERT, D_MODEL), by_expert),
                pl.BlockSpec((None, 1, D_MODEL), by_expert),
            ],
            out_specs=pl.BlockSpec(memory_space=pl.ANY),
            scratch_shapes=[
                pltpu.VMEM((D_MODEL, 2 * D_EXPERT), BF16),
                pltpu.VMEM((D_EXPERT, D_MODEL), BF16),
                pltpu.VMEM((2, ROW_TILE, width), jnp.int32),
                pltpu.VMEM((2, ROW_TILE, D_MODEL // 2), jnp.int32),
                pltpu.SemaphoreType.DMA((2,)),
                pltpu.SemaphoreType.DMA((2,)),
            ],
        ),
        out_shape=jax.ShapeDtypeStruct((n_rows, D_MODEL // 2), jnp.int32),
        compiler_params=_cparams(1, VMEM_LIMIT),
        name="experts",
    )(tile_start, tile_count, xs, w_gate_up, b_gate_up.reshape(N_EXPERTS, 1, 2 * D_EXPERT),
      w_down, b_down.reshape(N_EXPERTS, 1, D_MODEL))


def _sc_gather_rows(table, idx):
    m = idx.shape[0]
    width = table.shape[1]
    mesh = plsc.VectorSubcoreMesh(core_axis_name="core", subcore_axis_name="subcore")
    n_workers = mesh.num_cores * mesh.num_subcores
    per_worker = m // n_workers
    assert per_worker * n_workers == m and per_worker % SC_WINDOW == 0

    @pl.kernel(
        out_type=jax.ShapeDtypeStruct((m, width), table.dtype),
        mesh=mesh,
        scratch_types=[
            pltpu.VMEM((SC_WINDOW,), jnp.int32),
            pltpu.VMEM((SC_WINDOW, width), table.dtype),
            pltpu.SemaphoreType.DMA,
        ],
    )
    def gather_kernel(table_hbm, idx_hbm, out_hbm, idx_v, rows_v, sem):
        worker = lax.axis_index("subcore") * mesh.num_cores + lax.axis_index("core")

        @pl.loop(0, per_worker // SC_WINDOW)
        def _(j):
            base = pl.multiple_of(worker * per_worker + j * SC_WINDOW, SC_WINDOW)
            pltpu.sync_copy(idx_hbm.at[pl.ds(base, SC_WINDOW)], idx_v)
            pltpu.async_copy(table_hbm.at[idx_v], rows_v, sem).wait()
            pltpu.sync_copy(rows_v, out_hbm.at[pl.ds(base, SC_WINDOW)])

    return gather_kernel(table, idx)


def _sc_move_rows(table, src, dst):
    m = src.shape[0]
    width = table.shape[1]
    mesh = plsc.VectorSubcoreMesh(core_axis_name="core", subcore_axis_name="subcore")
    n_workers = mesh.num_cores * mesh.num_subcores
    per_worker = m // n_workers
    assert per_worker * n_workers == m and per_worker % SC_WINDOW == 0

    @pl.kernel(
        out_type=jax.ShapeDtypeStruct((m, width), table.dtype),
        mesh=mesh,
        scratch_types=[
            pltpu.VMEM((SC_WINDOW,), jnp.int32),
            pltpu.VMEM((SC_WINDOW,), jnp.int32),
            pltpu.VMEM((SC_WINDOW, width), table.dtype),
            pltpu.SemaphoreType.DMA,
        ],
    )
    def move_kernel(table_hbm, src_hbm, dst_hbm, out_hbm, src_v, dst_v, rows_v, sem):
        worker = lax.axis_index("subcore") * mesh.num_cores + lax.axis_index("core")

        @pl.loop(0, per_worker // SC_WINDOW)
        def _(j):
            base = pl.multiple_of(worker * per_worker + j * SC_WINDOW, SC_WINDOW)
            pltpu.sync_copy(src_hbm.at[pl.ds(base, SC_WINDOW)], src_v)
            pltpu.sync_copy(dst_hbm.at[pl.ds(base, SC_WINDOW)], dst_v)
            pltpu.async_copy(table_hbm.at[src_v], rows_v, sem).wait()
            pltpu.async_copy(rows_v, out_hbm.at[dst_v], sem).wait()

    return move_kernel(table, src, dst)


def _combine_dense_kernel(tw_ref, x1_ref, mod_ref, g_ref, y0_ref, y1_ref, y2_ref, y3_ref, o_ref):
    tm = x1_ref.shape[0]
    tw = jnp.concatenate([tw_ref[...], jnp.zeros((LANES - SUBLANES, tm), F32)], axis=0).T
    first = second = None
    for kk, y_ref in enumerate((y0_ref, y1_ref, y2_ref, y3_ref)):
        f_k, s_k = _unpack_bf16_pairs(y_ref[...])
        w_k = tw[:, kk:kk + 1]
        first = w_k * f_k if kk == 0 else first + w_k * f_k
        second = w_k * s_k if kk == 0 else second + w_k * s_k
    ffn = jnp.concatenate([first, second], axis=1)
    x2 = x1_ref[...] + mod_ref[5:6, :] * ffn
    o_ref[...] = x2 * lax.rsqrt(jnp.mean(x2 * x2, axis=-1, keepdims=True) + EPS) * g_ref[...]


def _combine_dense(tw, x1, mod, g_final, yg, seq):
    n = x1.shape[0]
    width = yg.shape[1]
    tiles = n // TOK_TILE
    tiles_per_seq = seq // TOK_TILE
    row = lambda i: (i, 0)
    slot = lambda kk: pl.BlockSpec((TOK_TILE, width), lambda i: (kk * tiles + i, 0))
    return pl.pallas_call(
        _combine_dense_kernel,
        grid=(tiles,),
        in_specs=[
            pl.BlockSpec((SUBLANES, TOK_TILE), lambda i: (0, i)),
            pl.BlockSpec((TOK_TILE, D_MODEL), row),
            pl.BlockSpec((None, SUBLANES, D_MODEL), lambda i: (i // tiles_per_seq, 0, 0)),
            pl.BlockSpec((1, D_MODEL), lambda i: (0, 0)),
        ] + [slot(kk) for kk in range(TOP_K)],
        out_specs=pl.BlockSpec((TOK_TILE, D_MODEL), row),
        out_shape=jax.ShapeDtypeStruct((n, D_MODEL), F32),
        compiler_params=_cparams(1, VMEM_LIMIT),
        name="combine",
    )(tw, x1, mod, g_final.reshape(1, D_MODEL), yg, yg, yg, yg)


def _rope_base(positions):
    half = ROPE_DIM // 2
    inv_freq = ROPE_THETA ** (-jnp.arange(0, ROPE_DIM, 2, dtype=F32) / ROPE_DIM)
    n = positions.size
    pos = jnp.broadcast_to(positions.reshape(n, 1).astype(F32), (n, half)).reshape(-1, LANES)
    ang = pos * jnp.tile(inv_freq, LANES // half)
    cos, sin = lax.optimization_barrier((jnp.cos(ang), jnp.sin(ang)))
    pad = jnp.zeros((n, LANES - ROPE_DIM), F32)
    return jnp.concatenate([cos.reshape(n, half), sin.reshape(n, half), pad], axis=1)


def _positions_kernel(starts_ref, ti_ref, rk_ref, pos_ref):
    ids = ti_ref[...]
    pos = rk_ref[...]
    for e in range(N_EXPERTS):
        pos = pos + jnp.where(ids == e, starts_ref[e], 0)
    pos_ref[...] = pos


def _positions(starts, top_i, rank):
    shape = top_i.shape
    return pl.pallas_call(
        _positions_kernel,
        in_specs=[
            pl.BlockSpec(memory_space=pltpu.SMEM),
            pl.BlockSpec(shape, lambda: (0, 0)),
            pl.BlockSpec(shape, lambda: (0, 0)),
        ],
        out_specs=pl.BlockSpec(shape, lambda: (0, 0)),
        out_shape=jax.ShapeDtypeStruct(shape, jnp.int32),
        name="positions",
    )(starts, top_i, rank)


def _routing_tables(top_i, rank, counts):
    n = top_i.shape[1]
    n_tiles = n * TOP_K // ROW_TILE + N_EXPERTS
    padded = (counts + ROW_TILE - 1) // ROW_TILE * ROW_TILE
    ends = jnp.cumsum(padded)
    starts = ends - padded
    pos = _positions(starts.astype(jnp.int32), top_i, rank)
    pos = jnp.concatenate([pos[kk] for kk in range(TOP_K)])
    tile_count = (padded // ROW_TILE).astype(jnp.int32)
    tile_start = jnp.concatenate([starts, ends[-1:]]).astype(jnp.int32) // ROW_TILE
    n_rows = n_tiles * ROW_TILE
    n_fill = n_rows - n * TOP_K
    gap_start = jnp.concatenate([starts + counts, ends[-1:]])
    gap_len = jnp.concatenate([padded - counts, n_rows - ends[-1:]])
    gap_end = jnp.cumsum(gap_len)
    m = jnp.arange(n_fill, dtype=jnp.int32)
    in_gap = jnp.logical_and(m[:, None] >= (gap_end - gap_len)[None, :], m[:, None] < gap_end[None, :])
    fill_dst = m + jnp.sum(jnp.where(in_gap, (gap_start - (gap_end - gap_len))[None, :], 0), axis=1)
    pos = pos.astype(jnp.int32)
    move_src = jnp.concatenate([jnp.arange(n * TOP_K, dtype=jnp.int32) % n, m % n])
    move_dst = jnp.concatenate([pos, fill_dst.astype(jnp.int32)])
    return pos, move_src, move_dst, tile_start, tile_count


def kernel(x, c, positions, w_ada, b_ada, g_mix_norm, w_in, rel_bias, lambda_q1, lambda_k1,
           lambda_q2, lambda_k2, g_subln, w_out, g_ffn_norm, w_router, b_router, w_gate_up,
           b_gate_up, w_down, b_down, g_final):
    batch, seq, _ = x.shape
    depth = w_ada.shape[0]
    assert depth == 1, "the combine kernel applies the final norm, so it must follow the only layer"
    n = batch * seq
    rope_base = _rope_base(positions)
    x2 = x.reshape(n, D_MODEL)
    for l in range(depth):
        lambda_init = 0.8 - 0.6 * math.exp(-0.3 * l)
        mod = _ada(c, w_ada[l], b_ada[l])
        qa, ka, va, qb, kb, vb = _in_proj(x2, mod, g_mix_norm[l], w_in[l].astype(BF16),
                                          rope_base, seq)
        oa = _attn_a(qa, ka, va, _rel_bias_rows(rel_bias[l]), batch, seq)
        lam = (jnp.exp(jnp.sum(lambda_q1[l].astype(F32) * lambda_k1[l].astype(F32)))
               - jnp.exp(jnp.sum(lambda_q2[l].astype(F32) * lambda_k2[l].astype(F32)))
               + lambda_init).reshape(1)
        ob = _attn_b(lam, qb, kb, vb, g_subln[l], batch, seq, 1.0 - lambda_init)
        x1, hp, top_i, top_w, rank, counts = _out_route(
            oa, ob, x2, mod, w_out[l].astype(BF16), g_ffn_norm[l], w_router[l].astype(BF16),
            b_router[l], seq)
        pos, move_src, move_dst, tile_start, tile_count = _routing_tables(
            top_i, rank, counts[:, 0])
        xs = _sc_move_rows(hp, move_src, move_dst)
        y = _experts(tile_start, tile_count, xs, w_gate_up[l], b_gate_up[l], w_down[l], b_down[l])
        yg = _sc_gather_rows(y, pos)
        x2 = _combine_dense(top_w, x1, mod, g_final, yg, seq)
    return x2.reshape(batch, seq, D_MODEL)
```

```python
import functools
import math

import jax
import jax.numpy as jnp
from jax import lax
from jax.experimental import pallas as pl
from jax.experimental.pallas import tpu as pltpu
from jax.experimental.pallas import tpu_sc as plsc

D_MODEL = 1024
CHUNK = 64
HEAD_DIM = 64
A_HEADS = 8
A_WIDTH = A_HEADS * HEAD_DIM
LEFT_CHUNKS = 8
MAX_REL = 128
B_HEADS = 4
B_QK_DIM = HEAD_DIM
B_V_DIM = 2 * HEAD_DIM
B_WIDTH = B_HEADS * B_V_DIM
ROPE_THETA = 500000.0
ROPE_DIM = B_QK_DIM // 4
N_EXPERTS = 32
TOP_K = 4
D_EXPERT = D_MODEL
SWIGLU_LIMIT = 7.0
SWIGLU_ALPHA = 1.702
EPS = 1e-6
NEG_INF = -1e30
LOG2_E = math.log2(math.e)
N_MOD = 6

LANES = 128
SUBLANES = 8
VMEM_BYTES_V7X = 64 * 1024 * 1024
VMEM_LIMIT = VMEM_BYTES_V7X * 7 // 8

TOK_TILE = 512
A_QBLK = 2 * CHUNK
A_BAND = (LEFT_CHUNKS + 2) * CHUNK
A_ROLL = A_BAND + A_QBLK
B_TQ = 512
B_TK = 512
B_SUB = 128
ROW_TILE = 512
FF_CHUNK = 512
EXPERT_ROWS = 16384
SC_WINDOW = 128

F32 = jnp.float32
BF16 = jnp.bfloat16


def _cparams(n_axes, vmem=None):
    return pltpu.CompilerParams(
        dimension_semantics=("arbitrary",) * n_axes,
        vmem_limit_bytes=vmem,
    )


def _ada_kernel(c_ref, w_ref, b_ref, o_ref):
    c = c_ref[...]
    act = c * jax.nn.sigmoid(c)
    o_ref[...] = jnp.dot(act, w_ref[...], preferred_element_type=F32,
                         precision=lax.Precision.HIGHEST) + b_ref[...]


def _ada(c, w_ada, b_ada):
    b = c.shape[0]
    rows = -(-b // SUBLANES) * SUBLANES
    c_pad = jnp.pad(c, ((0, rows - b), (0, 0)))
    n_out = w_ada.shape[1]
    out = pl.pallas_call(
        _ada_kernel,
        grid=(n_out // D_MODEL,),
        in_specs=[
            pl.BlockSpec((rows, D_MODEL), lambda j: (0, 0)),
            pl.BlockSpec((D_MODEL, D_MODEL), lambda j: (0, j)),
            pl.BlockSpec((1, D_MODEL), lambda j: (0, j)),
        ],
        out_specs=pl.BlockSpec((rows, D_MODEL), lambda j: (0, j)),
        out_shape=jax.ShapeDtypeStruct((rows, n_out), F32),
        compiler_params=_cparams(1),
        name="ada",
    )(c_pad, w_ada, b_ada.reshape(1, n_out))
    mod = out[:b].reshape(b, N_MOD, D_MODEL)
    return jnp.pad(mod, ((0, 0), (0, SUBLANES - N_MOD), (0, 0)))


def _in_proj_kernel(x_ref, mod_ref, g_ref, w_ref, rb_ref,
                    qa_ref, ka_ref, va_ref, qb_ref, kb_ref, vb_ref):
    x = x_ref[...]
    mod = mod_ref[...]
    y = x * lax.rsqrt(jnp.mean(x * x, axis=-1, keepdims=True) + EPS) * g_ref[...]
    h = (y * (1.0 + mod[1:2, :]) + mod[0:1, :]).astype(BF16)
    q_scale = HEAD_DIM ** -0.5

    half = ROPE_DIM // 2
    rb = rb_ref[...]
    lane = lax.broadcasted_iota(jnp.int32, rb.shape, 1)
    cos_lo = jnp.where(lane < half, rb, 0.0)
    sin_hi = jnp.where(jnp.logical_and(lane >= half, lane < ROPE_DIM), rb, 0.0)
    cos_pair = cos_lo + pltpu.roll(cos_lo, half, 1)
    rc = (cos_pair + pltpu.roll(cos_pair, B_QK_DIM, 1)
          + jnp.where(lane % B_QK_DIM >= ROPE_DIM, 1.0, 0.0))
    sin_lo = pltpu.roll(sin_hi, LANES - half, 1)
    rm = -(sin_lo + pltpu.roll(sin_lo, B_QK_DIM, 1))
    rp = sin_hi + pltpu.roll(sin_hi, B_QK_DIM, 1)

    def rope(p):
        cols = []
        for s in range(p.shape[1] // LANES):
            v = p[:, s * LANES:(s + 1) * LANES]
            cols.append(v * rc + pltpu.roll(v, LANES - ROPE_DIM // 2, 1) * rm
                        + pltpu.roll(v, ROPE_DIM // 2, 1) * rp)
        return jnp.concatenate(cols, axis=1)

    outs = (qa_ref, ka_ref, va_ref, qb_ref, kb_ref, vb_ref)
    for j, o_ref in enumerate(outs):
        p = jnp.dot(h, w_ref[:, j * A_WIDTH:(j + 1) * A_WIDTH], preferred_element_type=F32)
        if j in (3, 4):
            p = rope(p)
        if j in (0, 3):
            p = p * (q_scale * LOG2_E)
        pb = p.astype(BF16)
        for s in range(A_WIDTH // LANES):
            o_ref[s] = pb[:, s * LANES:(s + 1) * LANES]


def _in_proj(x2, mod, g_mix, w_in_bf, rope_base, seq):
    n = x2.shape[0]
    tiles_per_seq = seq // TOK_TILE
    row = lambda i: (i, 0)
    fixed = lambda i: (0, 0)
    n_slabs = A_WIDTH // LANES
    out_sd = jax.ShapeDtypeStruct((n_slabs, n, LANES), BF16)
    return pl.pallas_call(
        _in_proj_kernel,
        grid=(n // TOK_TILE,),
        in_specs=[
            pl.BlockSpec((TOK_TILE, D_MODEL), row),
            pl.BlockSpec((None, SUBLANES, D_MODEL), lambda i: (i // tiles_per_seq, 0, 0)),
            pl.BlockSpec((1, D_MODEL), fixed),
            pl.BlockSpec(w_in_bf.shape, fixed),
            pl.BlockSpec((TOK_TILE, LANES), row),
        ],
        out_specs=[pl.BlockSpec((n_slabs, TOK_TILE, LANES), lambda i: (0, i, 0))] * 6,
        out_shape=[out_sd] * 6,
        compiler_params=_cparams(1, VMEM_LIMIT),
        name="in_proj",
    )(x2, mod, g_mix.reshape(1, D_MODEL), w_in_bf, rope_base)


def _attn_a_kernel(q_ref, kp_ref, kc_ref, vp_ref, vc_ref, bias_ref, o_ref, k_sc, v_sc, bias_sc):
    g = pl.program_id(1)
    n_pairs, blk, _ = q_ref.shape
    lane = lax.broadcasted_iota(jnp.int32, (A_QBLK, LANES), 1)
    col = lax.broadcasted_iota(jnp.int32, (A_QBLK, A_BAND), 1)
    ones = jnp.ones((A_BAND, LANES), BF16)

    @pl.when(jnp.logical_and(pl.program_id(0) == 0, g == 0))
    def _():
        q_chunk = lax.broadcasted_iota(jnp.int32, (A_QBLK, A_BAND), 0) // CHUNK
        k_chunk = col // CHUNK
        in_band = jnp.logical_and(k_chunk >= q_chunk, k_chunk <= q_chunk + LEFT_CHUNKS)
        for h in range(bias_ref.shape[0]):
            rolled = pltpu.roll(jnp.broadcast_to(bias_ref[h], (A_QBLK, A_ROLL)), 0, 1,
                                stride=1, stride_axis=0)
            bias_sc[h] = jnp.where(in_band, rolled[:, :A_BAND], NEG_INF)

    def pair(p):
        k_buf = k_sc.at[p % 2]
        v_buf = v_sc.at[p % 2]
        k_buf[0:blk, :] = kp_ref[p]
        k_buf[blk:2 * blk, :] = kc_ref[p]
        v_buf[0:blk, :] = vp_ref[p]
        v_buf[blk:2 * blk, :] = vc_ref[p]

        def scores(m, hh):
            r0 = m * A_QBLK
            q = q_ref[p, r0:r0 + A_QBLK, :]
            in_head = (lane < HEAD_DIM) if hh == 0 else (lane >= HEAD_DIM)
            qh = jnp.where(in_head, q, jnp.zeros_like(q))
            return lax.dot_general(qh, k_buf[r0:r0 + A_BAND, :], (((1,), (1,)), ((), ())),
                                   preferred_element_type=F32)

        chains = [(m, hh) for m in range(blk // A_QBLK) for hh in range(2)]
        s_next = scores(*chains[0])
        halves = []
        for i, (m, hh) in enumerate(chains):
            r0 = m * A_QBLK
            s = s_next
            if i + 1 < len(chains):
                s_next = scores(*chains[i + 1])
            valid = jnp.logical_or(g > 0, col + r0 >= blk)
            s = jnp.where(valid, s + bias_sc[2 * p + hh], NEG_INF)
            pr = jnp.exp2(s - jnp.max(s, axis=1, keepdims=True))
            v_ext = jnp.concatenate([v_buf[r0:r0 + A_BAND, :], ones], axis=1)
            pv = jnp.dot(pr.astype(BF16), v_ext, preferred_element_type=F32)
            halves.append(pv[:, :LANES] / pv[:, LANES:])
            if hh == 1:
                o_ref[p, r0:r0 + A_QBLK, :] = jnp.where(lane < HEAD_DIM, halves[0],
                                                        halves[1]).astype(BF16)
                halves = []

    for p in range(n_pairs):
        pair(p)


def _attn_a(qa, ka, va, bias_rows, batch, seq):
    n_pairs, n, _ = qa.shape
    blk = LEFT_CHUNKS * CHUNK
    nblk = seq // blk
    cur = lambda b, g: (0, b * nblk + g, 0)
    prev = lambda b, g: (0, b * nblk + jnp.maximum(g - 1, 0), 0)
    slab = (n_pairs, blk, LANES)
    return pl.pallas_call(
        _attn_a_kernel,
        grid=(batch, nblk),
        in_specs=[
            pl.BlockSpec(slab, cur),
            pl.BlockSpec(slab, prev),
            pl.BlockSpec(slab, cur),
            pl.BlockSpec(slab, prev),
            pl.BlockSpec(slab, cur),
            pl.BlockSpec(bias_rows.shape, lambda b, g: (0, 0, 0)),
        ],
        out_specs=pl.BlockSpec(slab, cur),
        out_shape=jax.ShapeDtypeStruct((n_pairs, n, LANES), BF16),
        scratch_shapes=[
            pltpu.VMEM((2, 2 * blk, LANES), BF16),
            pltpu.VMEM((2, 2 * blk, LANES), BF16),
            pltpu.VMEM((bias_rows.shape[0], A_QBLK, A_BAND), F32),
        ],
        compiler_params=_cparams(2),
        name="attn_a",
    )(qa, ka, ka, va, va, bias_rows)


def _rel_bias_rows(rel_table):
    t = rel_table.astype(F32) * LOG2_E
    far = t[:, 2 * MAX_REL:]
    n_far = LEFT_CHUNKS * CHUNK - MAX_REL
    row = jnp.concatenate([
        jnp.broadcast_to(far, (t.shape[0], n_far)),
        t[:, 2 * MAX_REL:0:-1],
        jnp.broadcast_to(far, (t.shape[0], A_ROLL - A_BAND)),
    ], axis=1)
    return row.reshape(t.shape[0], 1, A_ROLL)


def _attn_b_kernel(lam_ref, q_ref, k_ref, v_ref, g_ref, o_ref, q_sc, s_sc, m_sc, acc_sc, *,
                   out_scale):
    qi = pl.program_id(2)
    n_sub = B_TQ // B_SUB
    lane = lax.broadcasted_iota(jnp.int32, (B_TQ, LANES), 1)
    q = q_ref[...]
    q_sc[0] = jnp.where(lane < B_QK_DIM, q, jnp.zeros_like(q))
    q_sc[1] = jnp.where(lane >= B_QK_DIM, q, jnp.zeros_like(q))
    m_sc[...] = jnp.full(m_sc.shape, NEG_INF, F32)
    acc_sc[...] = jnp.zeros(acc_sc.shape, F32)
    ones = jnp.ones((B_TK, LANES), BF16)

    def score_rows(k, slot, sub, c):
        rows = pl.ds(sub * B_SUB, B_SUB)
        s_sc[slot, c, rows, :] = lax.dot_general(q_sc[c, rows, :], k, (((1,), (1,)), ((), ())),
                                                 preferred_element_type=F32)

    def update_rows(s, v_ext, sub, c):
        rows = pl.ds(sub * B_SUB, B_SUB)
        m_prev = m_sc[c, rows, :]
        m_new = jnp.maximum(m_prev, jnp.max(s, axis=1, keepdims=True))
        alpha = jnp.exp2(m_prev - m_new)
        p = jnp.exp2(s - jnp.concatenate([m_new] * (s.shape[1] // LANES), axis=1))
        pv = jnp.dot(p.astype(BF16), v_ext, preferred_element_type=F32)
        acc_sc[c, rows, :] = jnp.concatenate([alpha, alpha], axis=1) * acc_sc[c, rows, :] + pv
        m_sc[c, rows, :] = m_new

    def key_block(blk):
        return k_ref[pl.ds(pl.multiple_of(blk * B_TK, B_TK), B_TK), :]

    def value_block(blk):
        return jnp.concatenate([v_ref[pl.ds(pl.multiple_of(blk * B_TK, B_TK), B_TK), :], ones], axis=1)

    def step(blk, slot, next_blk):
        v_ext = value_block(blk)
        k_next = key_block(next_blk)
        for sub in range(n_sub):
            for c in range(2):
                update_rows(s_sc[slot, c, pl.ds(sub * B_SUB, B_SUB), :], v_ext, sub, c)
                score_rows(k_next, 1 - slot, sub, c)

    def diagonal_step(slot):
        v_ext = value_block(qi)
        col_c = lax.broadcasted_iota(jnp.int32, (B_SUB, B_TK), 1) // CHUNK
        row_c = lax.broadcasted_iota(jnp.int32, (B_SUB, B_TK), 0) // CHUNK
        for sub in range(n_sub):
            keep = col_c <= row_c + sub * (B_SUB // CHUNK)
            for c in range(2):
                s = s_sc[slot, c, pl.ds(sub * B_SUB, B_SUB), :]
                update_rows(jnp.where(keep, s, NEG_INF), v_ext, sub, c)

    k0 = key_block(0)
    for sub in range(n_sub):
        for c in range(2):
            score_rows(k0, 0, sub, c)

    def pair(p, carry):
        first = 2 * p
        step(first, 0, first + 1)
        step(first + 1, 1, first + 2)
        return carry

    lax.fori_loop(0, qi // 2, pair, 0)

    @pl.when(qi % 2 == 0)
    def _():
        diagonal_step(0)

    @pl.when(qi % 2 == 1)
    def _():
        step(qi - 1, 0, qi)
        diagonal_step(1)

    lam = lam_ref[0]
    a0 = acc_sc[0]
    a1 = acc_sc[1]
    o = a0[:, :B_V_DIM] / a0[:, B_V_DIM:] - lam * (a1[:, :B_V_DIM] / a1[:, B_V_DIM:])
    y = o * lax.rsqrt(jnp.mean(o * o, axis=-1, keepdims=True) + EPS) * g_ref[...]
    o_ref[...] = (y * out_scale).astype(BF16)


def _attn_b(lam, qb, kb, vb, g_subln, batch, seq, out_scale):
    assert B_TQ == B_TK and B_V_DIM == LANES
    n = qb.shape[1]
    nq = seq // B_TQ
    q_map = lambda b, h, qi: (h, b * nq + qi, 0)
    kv_map = lambda b, h, qi: (h, b, 0)
    return pl.pallas_call(
        functools.partial(_attn_b_kernel, out_scale=out_scale),
        grid=(batch, B_HEADS, nq),
        in_specs=[
            pl.BlockSpec(memory_space=pltpu.SMEM),
            pl.BlockSpec((None, B_TQ, LANES), q_map),
            pl.BlockSpec((None, seq, LANES), kv_map),
            pl.BlockSpec((None, seq, LANES), kv_map),
            pl.BlockSpec((1, B_V_DIM), lambda b, h, qi: (0, 0)),
        ],
        out_specs=pl.BlockSpec((None, B_TQ, LANES), q_map),
        out_shape=jax.ShapeDtypeStruct((B_HEADS, n, LANES), BF16),
        scratch_shapes=[
            pltpu.VMEM((2, B_TQ, LANES), BF16),
            pltpu.VMEM((2, 2, B_TQ, B_TK), F32),
            pltpu.VMEM((2, B_TQ, LANES), F32),
            pltpu.VMEM((2, B_TQ, 2 * LANES), F32),
        ],
        compiler_params=_cparams(3),
        name="attn_b",
    )(lam, qb, kb, vb, g_subln.reshape(1, B_V_DIM))


def _pack_bf16_pairs(v):
    half = v.shape[1] // 2
    vb = v.astype(BF16)
    hi = lax.bitcast_convert_type(vb[:, :half].astype(F32), jnp.int32)
    lo = lax.bitcast_convert_type(vb[:, half:].astype(F32), jnp.int32)
    return hi | lax.shift_right_logical(lo, jnp.full(lo.shape, 16, jnp.int32))


def _unpack_bf16_pairs(w):
    first = lax.bitcast_convert_type(w & jnp.int32(-65536), F32)
    second = lax.bitcast_convert_type(lax.shift_left(w, jnp.full(w.shape, 16, jnp.int32)), F32)
    return first, second


def _out_route_kernel(oa_ref, ob_ref, x_ref, mod_ref, wo_ref, g_ref, wr_ref, br_ref, cin_ref,
                      x1_in_ref, x1_ref, hp_ref, ti_ref, tw_ref, rk_ref, cnt_ref, tri_sc, carry_sc):
    del x1_in_ref
    i = pl.program_id(0)
    tm = x_ref.shape[0]

    @pl.when(i == 0)
    def _():
        r = lax.broadcasted_iota(jnp.int32, (tm, tm), 0)
        c = lax.broadcasted_iota(jnp.int32, (tm, tm), 1)
        tri_sc[...] = jnp.where(r < c, 1.0, 0.0).astype(BF16)
        carry_sc[...] = cin_ref[...]

    mod = mod_ref[...]
    o = jnp.concatenate([oa_ref[s] for s in range(oa_ref.shape[0])]
                        + [ob_ref[s] for s in range(ob_ref.shape[0])], axis=1)
    mix = jnp.dot(o, wo_ref[...], preferred_element_type=F32)
    x1 = x_ref[...] + mod[2:3, :] * mix
    x1_ref[...] = x1
    y = x1 * lax.rsqrt(jnp.mean(x1 * x1, axis=-1, keepdims=True) + EPS) * g_ref[...]
    h = y * (1.0 + mod[4:5, :]) + mod[3:4, :]
    hb = h.astype(BF16)
    hp_ref[...] = _pack_bf16_pairs(h)

    logits = lax.dot_general(wr_ref[...], hb, (((1,), (1,)), ((), ())),
                             preferred_element_type=F32) + br_ref[...]
    eid = lax.broadcasted_iota(jnp.int32, logits.shape, 0).astype(F32)
    work = logits
    vals, ids = [], []
    chosen = jnp.zeros(logits.shape, F32)
    for _ in range(TOP_K):
        v = jnp.max(work, axis=0, keepdims=True)
        e = jnp.min(jnp.where(work == v, eid, float(N_EXPERTS)), axis=0, keepdims=True)
        hit = eid == e
        vals.append(v)
        ids.append(e)
        chosen = jnp.where(hit, 1.0, chosen)
        work = jnp.where(hit, -jnp.inf, work)
    ex = [jnp.exp(v - vals[0]) for v in vals]
    den = ex[0] + ex[1] + ex[2] + ex[3]

    before = jnp.dot(chosen.astype(BF16), tri_sc[...], preferred_element_type=F32) + carry_sc[...]
    slot = lax.broadcasted_iota(jnp.int32, (SUBLANES, tm), 0)
    ti = jnp.zeros((SUBLANES, tm), F32)
    tw = jnp.zeros((SUBLANES, tm), F32)
    rk = jnp.zeros((SUBLANES, tm), F32)
    for kk in range(TOP_K):
        r_k = jnp.sum(jnp.where(eid == ids[kk], before, 0.0), axis=0, keepdims=True)
        ti = jnp.where(slot == kk, ids[kk], ti)
        tw = jnp.where(slot == kk, ex[kk] / den, tw)
        rk = jnp.where(slot == kk, r_k, rk)
    ti_ref[...] = ti.astype(jnp.int32)
    tw_ref[...] = tw
    rk_ref[...] = rk.astype(jnp.int32)
    carry = carry_sc[...] + jnp.sum(chosen, axis=1, keepdims=True)
    carry_sc[...] = carry
    cnt_ref[...] = carry


def _out_route(b, oa, ob, x2, mod, w_out_bf, g_ffn, w_router_bf, b_router, counts_in, x1_buf, seq):
    n = x2.shape[0]
    tiles = seq // TOK_TILE
    row = lambda i: (b * tiles + i, 0)
    local = lambda i: (i, 0)
    by_lane = lambda i: (0, i)
    fixed = lambda i: (0, 0)
    return pl.pallas_call(
        _out_route_kernel,
        grid=(tiles,),
        in_specs=[
            pl.BlockSpec((oa.shape[0], TOK_TILE, LANES), lambda i: (0, b * tiles + i, 0)),
            pl.BlockSpec((ob.shape[0], TOK_TILE, LANES), lambda i: (0, b * tiles + i, 0)),
            pl.BlockSpec((TOK_TILE, D_MODEL), row),
            pl.BlockSpec((None, SUBLANES, D_MODEL), lambda i: (b, 0, 0)),
            pl.BlockSpec((D_MODEL, D_MODEL), fixed),
            pl.BlockSpec((1, D_MODEL), fixed),
            pl.BlockSpec((N_EXPERTS, D_MODEL), fixed),
            pl.BlockSpec((N_EXPERTS, 1), fixed),
            pl.BlockSpec((N_EXPERTS, 1), fixed),
            pl.BlockSpec(memory_space=pl.ANY),
        ],
        out_specs=[
            pl.BlockSpec((TOK_TILE, D_MODEL), row),
            pl.BlockSpec((TOK_TILE, D_MODEL // 2), local),
            pl.BlockSpec((SUBLANES, TOK_TILE), by_lane),
            pl.BlockSpec((SUBLANES, TOK_TILE), by_lane),
            pl.BlockSpec((SUBLANES, TOK_TILE), by_lane),
            pl.BlockSpec((N_EXPERTS, 1), fixed),
        ],
        out_shape=[
            jax.ShapeDtypeStruct((n, D_MODEL), F32),
            jax.ShapeDtypeStruct((seq, D_MODEL // 2), jnp.int32),
            jax.ShapeDtypeStruct((SUBLANES, seq), jnp.int32),
            jax.ShapeDtypeStruct((SUBLANES, seq), F32),
            jax.ShapeDtypeStruct((SUBLANES, seq), jnp.int32),
            jax.ShapeDtypeStruct((N_EXPERTS, 1), F32),
        ],
        input_output_aliases={9: 0},
        scratch_shapes=[pltpu.VMEM((TOK_TILE, TOK_TILE), BF16), pltpu.VMEM((N_EXPERTS, 1), F32)],
        compiler_params=_cparams(1, VMEM_LIMIT),
        name="out_route",
    )(oa, ob, x2, mod, w_out_bf, g_ffn.reshape(1, D_MODEL), w_router_bf.T,
      b_router.reshape(N_EXPERTS, 1), counts_in, x1_buf)


def _experts_kernel(tiles_ref, rows_ref, base_ref, next_ref, xs_ref, wgu_ref, bgu_ref, wd_ref,
                    bd_ref, y_ref, wgu_sc, wd_sc, x_buf, y_buf, x_sem, y_sem):
    e = pl.program_id(0)
    n_tiles = tiles_ref[e]
    n_rows = rows_ref[e]
    n_live = base_ref[N_EXPERTS]
    region = e * EXPERT_ROWS

    def x_copy(row, slot):
        return pltpu.make_async_copy(xs_ref.at[pl.ds(row, ROW_TILE), :], x_buf.at[slot],
                                     x_sem.at[slot])

    def y_copy(row, slot):
        return pltpu.make_async_copy(y_buf.at[slot], y_ref.at[pl.ds(row, ROW_TILE), :],
                                     y_sem.at[slot])

    @pl.when(jnp.logical_and(e == 0, n_live > 0))
    def _():
        x_copy(pl.multiple_of(next_ref[N_EXPERTS], ROW_TILE), 0).start()

    @pl.when(n_tiles > 0)
    def _():
        wgu_sc[...] = wgu_ref[...].astype(BF16)
        wd_sc[...] = wd_ref[...].astype(BF16)

    def tile(j, carry):
        g = base_ref[e] + j
        slot = g % 2
        row = pl.multiple_of(region + j * ROW_TILE, ROW_TILE)
        x_copy(row, slot).wait()
        next_row = jnp.where(j + 1 < n_tiles, row + ROW_TILE, next_ref[e])

        @pl.when(g + 1 < n_live)
        def _():
            x_copy(pl.multiple_of(next_row, ROW_TILE), 1 - slot).start()

        used = lax.broadcasted_iota(jnp.int32, (ROW_TILE, 1), 0) < n_rows - j * ROW_TILE
        first, second = _unpack_bf16_pairs(jnp.where(used, x_buf[slot], 0))
        x = jnp.concatenate([first.astype(BF16), second.astype(BF16)], axis=1)
        acc = jnp.zeros((x.shape[0], D_MODEL), F32)
        for c in range(D_EXPERT // FF_CHUNK):
            lo_c, hi_c = c * FF_CHUNK, (c + 1) * FF_CHUNK
            gate = jnp.dot(x, wgu_sc[:, lo_c:hi_c], preferred_element_type=F32) + bgu_ref[:, lo_c:hi_c]
            up = (jnp.dot(x, wgu_sc[:, D_EXPERT + lo_c:D_EXPERT + hi_c], preferred_element_type=F32)
                  + bgu_ref[:, D_EXPERT + lo_c:D_EXPERT + hi_c])
            gate = jnp.minimum(gate, SWIGLU_LIMIT)
            up = jnp.clip(up, -SWIGLU_LIMIT, SWIGLU_LIMIT)
            act = (up + 1.0) * (gate * jax.nn.sigmoid(SWIGLU_ALPHA * gate))
            acc = acc + jnp.dot(act.astype(BF16), wd_sc[lo_c:hi_c, :], preferred_element_type=F32)

        @pl.when(g >= 2)
        def _():
            y_copy(0, slot).wait()

        y_buf[slot] = _pack_bf16_pairs(acc + bd_ref[...])
        y_copy(row, slot).start()
        return carry

    lax.fori_loop(0, n_tiles, tile, 0)

    @pl.when(e == pl.num_programs(0) - 1)
    def _():
        for back in (2, 1):
            @pl.when(n_live >= back)
            def _():
                y_copy(0, (n_live - back) % 2).wait()


def _experts(counts, xs, w_gate_up, b_gate_up, w_down, b_down):
    n_rows, width = xs.shape
    tiles = (counts + ROW_TILE - 1) // ROW_TILE
    base = jnp.concatenate([jnp.zeros((1,), jnp.int32), jnp.cumsum(tiles)]).astype(jnp.int32)
    region = jnp.arange(N_EXPERTS, dtype=jnp.int32) * EXPERT_ROWS
    later = jnp.arange(N_EXPERTS)[None, :] > jnp.arange(-1, N_EXPERTS)[:, None]
    cand = jnp.where(jnp.logical_and(later, (tiles > 0)[None, :]), region[None, :], n_rows)
    nxt = jnp.min(cand, axis=1)
    next_row = jnp.concatenate([nxt[1:], nxt[:1]]).astype(jnp.int32)
    by_expert = lambda e, *_: (e, 0, 0)
    return pl.pallas_call(
        _experts_kernel,
        grid_spec=pltpu.PrefetchScalarGridSpec(
            num_scalar_prefetch=4,
            grid=(N_EXPERTS,),
            in_specs=[
                pl.BlockSpec(memory_space=pl.ANY),
                pl.BlockSpec((None, D_MODEL, 2 * D_EXPERT), by_expert),
                pl.BlockSpec((None, 1, 2 * D_EXPERT), by_expert),
                pl.BlockSpec((None, D_EXPERT, D_MODEL), by_expert),
                pl.BlockSpec((None, 1, D_MODEL), by_expert),
            ],
            out_specs=pl.BlockSpec(memory_space=pl.ANY),
            scratch_shapes=[
                pltpu.VMEM((D_MODEL, 2 * D_EXPERT), BF16),
                pltpu.VMEM((D_EXPERT, D_MODEL), BF16),
                pltpu.VMEM((2, ROW_TILE, width), jnp.int32),
                pltpu.VMEM((2, ROW_TILE, D_MODEL // 2), jnp.int32),
                pltpu.SemaphoreType.DMA((2,)),
                pltpu.SemaphoreType.DMA((2,)),
            ],
        ),
        out_shape=jax.ShapeDtypeStruct((n_rows, D_MODEL // 2), jnp.int32),
        compiler_params=_cparams(1, VMEM_LIMIT),
        name="experts",
    )(tiles.astype(jnp.int32), counts.astype(jnp.int32), base, next_row, xs, w_gate_up,
      b_gate_up.reshape(N_EXPERTS, 1, 2 * D_EXPERT), w_down, b_down.reshape(N_EXPERTS, 1, D_MODEL))


def _sc_gather_rows(table, idx):
    m = idx.shape[0]
    width = table.shape[1]
    mesh = plsc.VectorSubcoreMesh(core_axis_name="core", subcore_axis_name="subcore")
    n_workers = mesh.num_cores * mesh.num_subcores
    per_worker = m // n_workers
    assert per_worker * n_workers == m and per_worker % SC_WINDOW == 0

    @pl.kernel(
        out_type=jax.ShapeDtypeStruct((m, width), table.dtype),
        mesh=mesh,
        scratch_types=[
            pltpu.VMEM((SC_WINDOW,), jnp.int32),
            pltpu.VMEM((SC_WINDOW, width), table.dtype),
            pltpu.SemaphoreType.DMA,
        ],
    )
    def gather_kernel(table_hbm, idx_hbm, out_hbm, idx_v, rows_v, sem):
        worker = lax.axis_index("subcore") * mesh.num_cores + lax.axis_index("core")

        @pl.loop(0, per_worker // SC_WINDOW)
        def _(j):
            base = pl.multiple_of(worker * per_worker + j * SC_WINDOW, SC_WINDOW)
            pltpu.sync_copy(idx_hbm.at[pl.ds(base, SC_WINDOW)], idx_v)
            pltpu.async_copy(table_hbm.at[idx_v], rows_v, sem).wait()
            pltpu.sync_copy(rows_v, out_hbm.at[pl.ds(base, SC_WINDOW)])

    return gather_kernel(table, idx)


def _sc_scatter_rows_into(table, dst, out_ref, reps):
    m, width = table.shape
    mesh = plsc.VectorSubcoreMesh(core_axis_name="core", subcore_axis_name="subcore")
    n_workers = mesh.num_cores * mesh.num_subcores
    per_worker = m // n_workers
    assert per_worker * n_workers == m and per_worker % SC_WINDOW == 0 and dst.shape == (reps * m,)

    @pl.kernel(
        out_type=(),
        mesh=mesh,
        scratch_types=[
            pltpu.VMEM((SC_WINDOW,), jnp.int32),
            pltpu.VMEM((SC_WINDOW, width), table.dtype),
            pltpu.SemaphoreType.DMA,
        ],
    )
    def scatter_kernel(table_hbm, dst_hbm, out_hbm, dst_v, rows_v, sem):
        worker = lax.axis_index("subcore") * mesh.num_cores + lax.axis_index("core")

        @pl.loop(0, per_worker // SC_WINDOW)
        def _(j):
            base = pl.multiple_of(worker * per_worker + j * SC_WINDOW, SC_WINDOW)
            pltpu.sync_copy(table_hbm.at[pl.ds(base, SC_WINDOW)], rows_v)
            for k in range(reps):
                pltpu.sync_copy(dst_hbm.at[pl.ds(k * m + base, SC_WINDOW)], dst_v)
                pltpu.async_copy(rows_v, out_hbm.at[dst_v], sem).wait()

    scatter_kernel(table, dst, out_ref)


def _combine_dense_kernel(tw_ref, x1_ref, mod_ref, g_ref, y0_ref, y1_ref, y2_ref, y3_ref, o_ref):
    tm = x1_ref.shape[0]
    tw = jnp.concatenate([tw_ref[...], jnp.zeros((LANES - SUBLANES, tm), F32)], axis=0).T
    first = second = None
    for kk, y_ref in enumerate((y0_ref, y1_ref, y2_ref, y3_ref)):
        f_k, s_k = _unpack_bf16_pairs(y_ref[...])
        w_k = tw[:, kk:kk + 1]
        first = w_k * f_k if kk == 0 else first + w_k * f_k
        second = w_k * s_k if kk == 0 else second + w_k * s_k
    ffn = jnp.concatenate([first, second], axis=1)
    x2 = x1_ref[...] + mod_ref[5:6, :] * ffn
    o_ref[...] = x2 * lax.rsqrt(jnp.mean(x2 * x2, axis=-1, keepdims=True) + EPS) * g_ref[...]


def _combine_dense(tw, x1, mod, g_final, yg, seq):
    n = x1.shape[0]
    width = yg.shape[1]
    tiles = n // TOK_TILE
    tiles_per_seq = seq // TOK_TILE
    row = lambda i: (i, 0)
    slot = lambda kk: pl.BlockSpec((TOK_TILE, width), lambda i: (kk * tiles + i, 0))
    return pl.pallas_call(
        _combine_dense_kernel,
        grid=(tiles,),
        in_specs=[
            pl.BlockSpec((SUBLANES, TOK_TILE), lambda i: (0, i)),
            pl.BlockSpec((TOK_TILE, D_MODEL), row),
            pl.BlockSpec((None, SUBLANES, D_MODEL), lambda i: (i // tiles_per_seq, 0, 0)),
            pl.BlockSpec((1, D_MODEL), lambda i: (0, 0)),
        ] + [slot(kk) for kk in range(TOP_K)],
        out_specs=pl.BlockSpec((TOK_TILE, D_MODEL), row),
        out_shape=jax.ShapeDtypeStruct((n, D_MODEL), F32),
        compiler_params=_cparams(1, VMEM_LIMIT),
        name="combine",
    )(tw, x1, mod, g_final.reshape(1, D_MODEL), yg, yg, yg, yg)


def _rope_base(positions):
    half = ROPE_DIM // 2
    inv_freq = ROPE_THETA ** (-jnp.arange(0, ROPE_DIM, 2, dtype=F32) / ROPE_DIM)
    n = positions.size
    pos = jnp.broadcast_to(positions.reshape(n, 1).astype(F32), (n, half)).reshape(-1, LANES)
    ang = pos * jnp.tile(inv_freq, LANES // half)
    cos, sin = lax.optimization_barrier((jnp.cos(ang), jnp.sin(ang)))
    pad = jnp.zeros((n, LANES - ROPE_DIM), F32)
    return jnp.concatenate([cos.reshape(n, half), sin.reshape(n, half), pad], axis=1)


def kernel(x, c, positions, w_ada, b_ada, g_mix_norm, w_in, rel_bias, lambda_q1, lambda_k1,
           lambda_q2, lambda_k2, g_subln, w_out, g_ffn_norm, w_router, b_router, w_gate_up,
           b_gate_up, w_down, b_down, g_final):
    batch, seq, _ = x.shape
    depth = w_ada.shape[0]
    assert depth == 1, "the combine kernel applies the final norm, so it must follow the only layer"
    n = batch * seq
    rope_base = _rope_base(positions)
    x2 = x.reshape(n, D_MODEL)
    for l in range(depth):
        lambda_init = 0.8 - 0.6 * math.exp(-0.3 * l)
        mod = _ada(c, w_ada[l], b_ada[l])
        qa, ka, va, qb, kb, vb = _in_proj(x2, mod, g_mix_norm[l], w_in[l].astype(BF16),
                                          rope_base, seq)
        oa = _attn_a(qa, ka, va, _rel_bias_rows(rel_bias[l]), batch, seq)
        lam = (jnp.exp(jnp.sum(lambda_q1[l].astype(F32) * lambda_k1[l].astype(F32)))
               - jnp.exp(jnp.sum(lambda_q2[l].astype(F32) * lambda_k2[l].astype(F32)))
               + lambda_init).reshape(1)
        ob = _attn_b(lam, qb, kb, vb, g_subln[l], batch, seq, 1.0 - lambda_init)
        w_out_bf = w_out[l].astype(BF16)
        w_router_bf = w_router[l].astype(BF16)
        xs_ref = jax.new_ref(lax.empty((N_EXPERTS * EXPERT_ROWS, D_MODEL // 2), jnp.int32))
        x1 = lax.empty((n, D_MODEL), F32)
        counts = jnp.zeros((N_EXPERTS, 1), F32)
        pos_rows, weight_rows = [], []
        for b in range(batch):
            x1, hp, top_i, top_w, rank, counts = _out_route(
                b, oa, ob, x2, mod, w_out_bf, g_ffn_norm[l], w_router_bf, b_router[l], counts, x1,
                seq)
            pos = top_i * EXPERT_ROWS + rank
            pos_rows.append(pos)
            weight_rows.append(top_w)
            dst = jnp.concatenate([pos[kk] for kk in range(TOP_K)])
            _sc_scatter_rows_into(hp, dst, xs_ref, TOP_K)
        xs = jax.freeze(xs_ref)
        y = _experts(counts[:, 0].astype(jnp.int32), xs, w_gate_up[l], b_gate_up[l], w_down[l],
                     b_down[l])
        pos = jnp.concatenate(pos_rows, axis=1)
        yg = _sc_gather_rows(y, jnp.concatenate([pos[kk] for kk in range(TOP_K)]))
        x2 = _combine_dense(jnp.concatenate(weight_rows, axis=1), x1, mod, g_final, yg, seq)
    return x2.reshape(batch, seq, D_MODEL)
```

```python
import functools
import math

import jax
import jax.numpy as jnp
from jax import lax
from jax.experimental import pallas as pl
from jax.experimental.pallas import tpu as pltpu
from jax.experimental.pallas import tpu_sc as plsc

D_MODEL = 1024
CHUNK = 64
HEAD_DIM = 64
A_HEADS = 8
A_WIDTH = A_HEADS * HEAD_DIM
LEFT_CHUNKS = 8
MAX_REL = 128
B_HEADS = 4
B_QK_DIM = HEAD_DIM
B_V_DIM = 2 * HEAD_DIM
B_WIDTH = B_HEADS * B_V_DIM
ROPE_THETA = 500000.0
ROPE_DIM = B_QK_DIM // 4
N_EXPERTS = 32
TOP_K = 4
D_EXPERT = D_MODEL
SWIGLU_LIMIT = 7.0
SWIGLU_ALPHA = 1.702
EPS = 1e-6
NEG_INF = -1e30
LOG2_E = math.log2(math.e)
N_MOD = 6

LANES = 128
SUBLANES = 8
VMEM_BYTES_V7X = 64 * 1024 * 1024
VMEM_LIMIT = VMEM_BYTES_V7X * 7 // 8

TOK_TILE = 512
A_QBLK = 2 * CHUNK
A_BAND = (LEFT_CHUNKS + 2) * CHUNK
A_ROLL = A_BAND + A_QBLK
B_TQ = 512
B_TK = 512
B_SUB = 128
ROW_TILE = 512
FF_CHUNK = 512
EXPERT_ROWS = 16384
SC_WINDOW = 128

F32 = jnp.float32
BF16 = jnp.bfloat16


def _cparams(n_axes, vmem=None):
    return pltpu.CompilerParams(
        dimension_semantics=("arbitrary",) * n_axes,
        vmem_limit_bytes=vmem,
    )


def _ada_kernel(c_ref, w_ref, b_ref, o_ref):
    c = c_ref[...]
    act = c * jax.nn.sigmoid(c)
    o_ref[...] = jnp.dot(act, w_ref[...], preferred_element_type=F32,
                         precision=lax.Precision.HIGHEST) + b_ref[...]


def _ada(c, w_ada, b_ada):
    b = c.shape[0]
    rows = -(-b // SUBLANES) * SUBLANES
    c_pad = jnp.pad(c, ((0, rows - b), (0, 0)))
    n_out = w_ada.shape[1]
    out = pl.pallas_call(
        _ada_kernel,
        grid=(n_out // D_MODEL,),
        in_specs=[
            pl.BlockSpec((rows, D_MODEL), lambda j: (0, 0)),
            pl.BlockSpec((D_MODEL, D_MODEL), lambda j: (0, j)),
            pl.BlockSpec((1, D_MODEL), lambda j: (0, j)),
        ],
        out_specs=pl.BlockSpec((rows, D_MODEL), lambda j: (0, j)),
        out_shape=jax.ShapeDtypeStruct((rows, n_out), F32),
        compiler_params=_cparams(1),
        name="ada",
    )(c_pad, w_ada, b_ada.reshape(1, n_out))
    mod = out[:b].reshape(b, N_MOD, D_MODEL)
    return jnp.pad(mod, ((0, 0), (0, SUBLANES - N_MOD), (0, 0)))


def _in_proj_kernel(x_ref, mod_ref, g_ref, w_ref, rb_ref,
                    qa_ref, ka_ref, va_ref, qb_ref, kb_ref, vb_ref):
    x = x_ref[...]
    mod = mod_ref[...]
    y = x * lax.rsqrt(jnp.mean(x * x, axis=-1, keepdims=True) + EPS) * g_ref[...]
    h = (y * (1.0 + mod[1:2, :]) + mod[0:1, :]).astype(BF16)
    q_scale = HEAD_DIM ** -0.5

    half = ROPE_DIM // 2
    rb = rb_ref[...]
    lane = lax.broadcasted_iota(jnp.int32, rb.shape, 1)
    cos_lo = jnp.where(lane < half, rb, 0.0)
    sin_hi = jnp.where(jnp.logical_and(lane >= half, lane < ROPE_DIM), rb, 0.0)
    cos_pair = cos_lo + pltpu.roll(cos_lo, half, 1)
    rc = (cos_pair + pltpu.roll(cos_pair, B_QK_DIM, 1)
          + jnp.where(lane % B_QK_DIM >= ROPE_DIM, 1.0, 0.0))
    sin_lo = pltpu.roll(sin_hi, LANES - half, 1)
    rm = -(sin_lo + pltpu.roll(sin_lo, B_QK_DIM, 1))
    rp = sin_hi + pltpu.roll(sin_hi, B_QK_DIM, 1)

    def rope(p):
        cols = []
        for s in range(p.shape[1] // LANES):
            v = p[:, s * LANES:(s + 1) * LANES]
            cols.append(v * rc + pltpu.roll(v, LANES - ROPE_DIM // 2, 1) * rm
                        + pltpu.roll(v, ROPE_DIM // 2, 1) * rp)
        return jnp.concatenate(cols, axis=1)

    outs = (qa_ref, ka_ref, va_ref, qb_ref, kb_ref, vb_ref)
    for j, o_ref in enumerate(outs):
        p = jnp.dot(h, w_ref[:, j * A_WIDTH:(j + 1) * A_WIDTH], preferred_element_type=F32)
        if j in (3, 4):
            p = rope(p)
        if j in (0, 3):
            p = p * (q_scale * LOG2_E)
        pb = p.astype(BF16)
        for s in range(A_WIDTH // LANES):
            o_ref[s] = pb[:, s * LANES:(s + 1) * LANES]


def _in_proj(x2, mod, g_mix, w_in_bf, rope_base, seq):
    n = x2.shape[0]
    tiles_per_seq = seq // TOK_TILE
    row = lambda i: (i, 0)
    fixed = lambda i: (0, 0)
    n_slabs = A_WIDTH // LANES
    out_sd = jax.ShapeDtypeStruct((n_slabs, n, LANES), BF16)
    return pl.pallas_call(
        _in_proj_kernel,
        grid=(n // TOK_TILE,),
        in_specs=[
            pl.BlockSpec((TOK_TILE, D_MODEL), row),
            pl.BlockSpec((None, SUBLANES, D_MODEL), lambda i: (i // tiles_per_seq, 0, 0)),
            pl.BlockSpec((1, D_MODEL), fixed),
            pl.BlockSpec(w_in_bf.shape, fixed),
            pl.BlockSpec((TOK_TILE, LANES), row),
        ],
        out_specs=[pl.BlockSpec((n_slabs, TOK_TILE, LANES), lambda i: (0, i, 0))] * 6,
        out_shape=[out_sd] * 6,
        compiler_params=_cparams(1, VMEM_LIMIT),
        name="in_proj",
    )(x2, mod, g_mix.reshape(1, D_MODEL), w_in_bf, rope_base)


def _attn_a_kernel(q_ref, kp_ref, kc_ref, vp_ref, vc_ref, bias_ref, o_ref, k_sc, v_sc, bias_sc):
    g = pl.program_id(1)
    n_pairs, blk, _ = q_ref.shape
    lane = lax.broadcasted_iota(jnp.int32, (A_QBLK, LANES), 1)
    col = lax.broadcasted_iota(jnp.int32, (A_QBLK, A_BAND), 1)
    ones = jnp.ones((A_BAND, LANES), BF16)

    @pl.when(jnp.logical_and(pl.program_id(0) == 0, g == 0))
    def _():
        q_chunk = lax.broadcasted_iota(jnp.int32, (A_QBLK, A_BAND), 0) // CHUNK
        k_chunk = col // CHUNK
        in_band = jnp.logical_and(k_chunk >= q_chunk, k_chunk <= q_chunk + LEFT_CHUNKS)
        for h in range(bias_ref.shape[0]):
            rolled = pltpu.roll(jnp.broadcast_to(bias_ref[h], (A_QBLK, A_ROLL)), 0, 1,
                                stride=1, stride_axis=0)
            bias_sc[h] = jnp.where(in_band, rolled[:, :A_BAND], NEG_INF)

    def pair(p):
        k_buf = k_sc.at[p % 2]
        v_buf = v_sc.at[p % 2]
        k_buf[0:blk, :] = kp_ref[p]
        k_buf[blk:2 * blk, :] = kc_ref[p]
        v_buf[0:blk, :] = vp_ref[p]
        v_buf[blk:2 * blk, :] = vc_ref[p]

        def scores(m, hh):
            r0 = m * A_QBLK
            q = q_ref[p, r0:r0 + A_QBLK, :]
            in_head = (lane < HEAD_DIM) if hh == 0 else (lane >= HEAD_DIM)
            qh = jnp.where(in_head, q, jnp.zeros_like(q))
            return lax.dot_general(qh, k_buf[r0:r0 + A_BAND, :], (((1,), (1,)), ((), ())),
                                   preferred_element_type=F32)

        chains = [(m, hh) for m in range(blk // A_QBLK) for hh in range(2)]
        s_next = scores(*chains[0])
        halves = []
        for i, (m, hh) in enumerate(chains):
            r0 = m * A_QBLK
            s = s_next
            if i + 1 < len(chains):
                s_next = scores(*chains[i + 1])
            valid = jnp.logical_or(g > 0, col + r0 >= blk)
            s = jnp.where(valid, s + bias_sc[2 * p + hh], NEG_INF)
            pr = jnp.exp2(s - jnp.max(s, axis=1, keepdims=True))
            v_ext = jnp.concatenate([v_buf[r0:r0 + A_BAND, :], ones], axis=1)
            pv = jnp.dot(pr.astype(BF16), v_ext, preferred_element_type=F32)
            halves.append(pv[:, :LANES] / pv[:, LANES:])
            if hh == 1:
                o_ref[p, r0:r0 + A_QBLK, :] = jnp.where(lane < HEAD_DIM, halves[0],
                                                        halves[1]).astype(BF16)
                halves = []

    for p in range(n_pairs):
        pair(p)


def _attn_a(qa, ka, va, bias_rows, batch, seq):
    n_pairs, n, _ = qa.shape
    blk = LEFT_CHUNKS * CHUNK
    nblk = seq // blk
    cur = lambda b, g: (0, b * nblk + g, 0)
    prev = lambda b, g: (0, b * nblk + jnp.maximum(g - 1, 0), 0)
    slab = (n_pairs, blk, LANES)
    return pl.pallas_call(
        _attn_a_kernel,
        grid=(batch, nblk),
        in_specs=[
            pl.BlockSpec(slab, cur),
            pl.BlockSpec(slab, prev),
            pl.BlockSpec(slab, cur),
            pl.BlockSpec(slab, prev),
            pl.BlockSpec(slab, cur),
            pl.BlockSpec(bias_rows.shape, lambda b, g: (0, 0, 0)),
        ],
        out_specs=pl.BlockSpec(slab, cur),
        out_shape=jax.ShapeDtypeStruct((n_pairs, n, LANES), BF16),
        scratch_shapes=[
            pltpu.VMEM((2, 2 * blk, LANES), BF16),
            pltpu.VMEM((2, 2 * blk, LANES), BF16),
            pltpu.VMEM((bias_rows.shape[0], A_QBLK, A_BAND), F32),
        ],
        compiler_params=_cparams(2),
        name="attn_a",
    )(qa, ka, ka, va, va, bias_rows)


def _rel_bias_rows(rel_table):
    t = rel_table.astype(F32) * LOG2_E
    far = t[:, 2 * MAX_REL:]
    n_far = LEFT_CHUNKS * CHUNK - MAX_REL
    row = jnp.concatenate([
        jnp.broadcast_to(far, (t.shape[0], n_far)),
        t[:, 2 * MAX_REL:0:-1],
        jnp.broadcast_to(far, (t.shape[0], A_ROLL - A_BAND)),
    ], axis=1)
    return row.reshape(t.shape[0], 1, A_ROLL)


def _attn_b_kernel(lam_ref, q_ref, k_ref, v_ref, g_ref, o_ref, q_sc, s_sc, m_sc, acc_sc, *,
                   out_scale):
    qi = pl.program_id(2)
    n_sub = B_TQ // B_SUB
    lane = lax.broadcasted_iota(jnp.int32, (B_TQ, LANES), 1)
    q = q_ref[...]
    q_sc[0] = jnp.where(lane < B_QK_DIM, q, jnp.zeros_like(q))
    q_sc[1] = jnp.where(lane >= B_QK_DIM, q, jnp.zeros_like(q))
    m_sc[...] = jnp.full(m_sc.shape, NEG_INF, F32)
    acc_sc[...] = jnp.zeros(acc_sc.shape, F32)
    ones = jnp.ones((B_TK, LANES), BF16)

    def score_rows(k, slot, sub, c):
        rows = pl.ds(sub * B_SUB, B_SUB)
        s_sc[slot, c, rows, :] = lax.dot_general(q_sc[c, rows, :], k, (((1,), (1,)), ((), ())),
                                                 preferred_element_type=F32)

    def update_rows(s, v_ext, sub, c):
        rows = pl.ds(sub * B_SUB, B_SUB)
        m_prev = m_sc[c, rows, :]
        m_new = jnp.maximum(m_prev, jnp.max(s, axis=1, keepdims=True))
        alpha = jnp.exp2(m_prev - m_new)
        p = jnp.exp2(s - jnp.concatenate([m_new] * (s.shape[1] // LANES), axis=1))
        pv = jnp.dot(p.astype(BF16), v_ext, preferred_element_type=F32)
        acc_sc[c, rows, :] = jnp.concatenate([alpha, alpha], axis=1) * acc_sc[c, rows, :] + pv
        m_sc[c, rows, :] = m_new

    def key_block(blk):
        return k_ref[pl.ds(pl.multiple_of(blk * B_TK, B_TK), B_TK), :]

    def value_block(blk):
        return jnp.concatenate([v_ref[pl.ds(pl.multiple_of(blk * B_TK, B_TK), B_TK), :], ones], axis=1)

    def step(blk, slot, next_blk):
        v_ext = value_block(blk)
        k_next = key_block(next_blk)
        for sub in range(n_sub):
            for c in range(2):
                update_rows(s_sc[slot, c, pl.ds(sub * B_SUB, B_SUB), :], v_ext, sub, c)
                score_rows(k_next, 1 - slot, sub, c)

    def diagonal_step(slot):
        v_ext = value_block(qi)
        col_c = lax.broadcasted_iota(jnp.int32, (B_SUB, B_TK), 1) // CHUNK
        row_c = lax.broadcasted_iota(jnp.int32, (B_SUB, B_TK), 0) // CHUNK
        for sub in range(n_sub):
            keep = col_c <= row_c + sub * (B_SUB // CHUNK)
            for c in range(2):
                s = s_sc[slot, c, pl.ds(sub * B_SUB, B_SUB), :]
                update_rows(jnp.where(keep, s, NEG_INF), v_ext, sub, c)

    k0 = key_block(0)
    for sub in range(n_sub):
        for c in range(2):
            score_rows(k0, 0, sub, c)

    def pair(p, carry):
        first = 2 * p
        step(first, 0, first + 1)
        step(first + 1, 1, first + 2)
        return carry

    lax.fori_loop(0, qi // 2, pair, 0)

    @pl.when(qi % 2 == 0)
    def _():
        diagonal_step(0)

    @pl.when(qi % 2 == 1)
    def _():
        step(qi - 1, 0, qi)
        diagonal_step(1)

    lam = lam_ref[0]
    a0 = acc_sc[0]
    a1 = acc_sc[1]
    o = a0[:, :B_V_DIM] / a0[:, B_V_DIM:] - lam * (a1[:, :B_V_DIM] / a1[:, B_V_DIM:])
    y = o * lax.rsqrt(jnp.mean(o * o, axis=-1, keepdims=True) + EPS) * g_ref[...]
    o_ref[...] = (y * out_scale).astype(BF16)


def _attn_b(lam, qb, kb, vb, g_subln, batch, seq, out_scale):
    assert B_TQ == B_TK and B_V_DIM == LANES
    n = qb.shape[1]
    nq = seq // B_TQ
    q_map = lambda b, h, qi: (h, b * nq + qi, 0)
    kv_map = lambda b, h, qi: (h, b, 0)
    return pl.pallas_call(
        functools.partial(_attn_b_kernel, out_scale=out_scale),
        grid=(batch, B_HEADS, nq),
        in_specs=[
            pl.BlockSpec(memory_space=pltpu.SMEM),
            pl.BlockSpec((None, B_TQ, LANES), q_map),
            pl.BlockSpec((None, seq, LANES), kv_map),
            pl.BlockSpec((None, seq, LANES), kv_map),
            pl.BlockSpec((1, B_V_DIM), lambda b, h, qi: (0, 0)),
        ],
        out_specs=pl.BlockSpec((None, B_TQ, LANES), q_map),
        out_shape=jax.ShapeDtypeStruct((B_HEADS, n, LANES), BF16),
        scratch_shapes=[
            pltpu.VMEM((2, B_TQ, LANES), BF16),
            pltpu.VMEM((2, 2, B_TQ, B_TK), F32),
            pltpu.VMEM((2, B_TQ, LANES), F32),
            pltpu.VMEM((2, B_TQ, 2 * LANES), F32),
        ],
        compiler_params=_cparams(3),
        name="attn_b",
    )(lam, qb, kb, vb, g_subln.reshape(1, B_V_DIM))


def _pack_bf16_pairs(v):
    half = v.shape[1] // 2
    vb = v.astype(BF16)
    hi = lax.bitcast_convert_type(vb[:, :half].astype(F32), jnp.int32)
    lo = lax.bitcast_convert_type(vb[:, half:].astype(F32), jnp.int32)
    return hi | lax.shift_right_logical(lo, jnp.full(lo.shape, 16, jnp.int32))


def _unpack_bf16_pairs(w):
    first = lax.bitcast_convert_type(w & jnp.int32(-65536), F32)
    second = lax.bitcast_convert_type(lax.shift_left(w, jnp.full(w.shape, 16, jnp.int32)), F32)
    return first, second


def _out_route_kernel(oa_ref, ob_ref, x_ref, mod_ref, wo_ref, g_ref, wr_ref, br_ref, cin_ref,
                      x1_in_ref, x1_ref, hp_ref, ti_ref, tw_ref, rk_ref, cnt_ref, tri_sc, carry_sc):
    del x1_in_ref
    i = pl.program_id(0)
    tm = x_ref.shape[0]

    @pl.when(i == 0)
    def _():
        r = lax.broadcasted_iota(jnp.int32, (tm, tm), 0)
        c = lax.broadcasted_iota(jnp.int32, (tm, tm), 1)
        tri_sc[...] = jnp.where(r < c, 1.0, 0.0).astype(BF16)
        carry_sc[...] = cin_ref[...]

    mod = mod_ref[...]
    o = jnp.concatenate([oa_ref[s] for s in range(oa_ref.shape[0])]
                        + [ob_ref[s] for s in range(ob_ref.shape[0])], axis=1)
    mix = jnp.dot(o, wo_ref[...], preferred_element_type=F32)
    x1 = x_ref[...] + mod[2:3, :] * mix
    x1_ref[...] = x1
    y = x1 * lax.rsqrt(jnp.mean(x1 * x1, axis=-1, keepdims=True) + EPS) * g_ref[...]
    h = y * (1.0 + mod[4:5, :]) + mod[3:4, :]
    hb = h.astype(BF16)
    hp_ref[...] = _pack_bf16_pairs(h)

    logits = lax.dot_general(wr_ref[...], hb, (((1,), (1,)), ((), ())),
                             preferred_element_type=F32) + br_ref[...]
    eid = lax.broadcasted_iota(jnp.int32, logits.shape, 0).astype(F32)
    work = logits
    vals, ids = [], []
    chosen = jnp.zeros(logits.shape, F32)
    for _ in range(TOP_K):
        v = jnp.max(work, axis=0, keepdims=True)
        e = jnp.min(jnp.where(work == v, eid, float(N_EXPERTS)), axis=0, keepdims=True)
        hit = eid == e
        vals.append(v)
        ids.append(e)
        chosen = jnp.where(hit, 1.0, chosen)
        work = jnp.where(hit, -jnp.inf, work)
    ex = [jnp.exp(v - vals[0]) for v in vals]
    den = ex[0] + ex[1] + ex[2] + ex[3]

    before = jnp.dot(chosen.astype(BF16), tri_sc[...], preferred_element_type=F32) + carry_sc[...]
    slot = lax.broadcasted_iota(jnp.int32, (SUBLANES, tm), 0)
    ti = jnp.zeros((SUBLANES, tm), F32)
    tw = jnp.zeros((SUBLANES, tm), F32)
    rk = jnp.zeros((SUBLANES, tm), F32)
    for kk in range(TOP_K):
        r_k = jnp.sum(jnp.where(eid == ids[kk], before, 0.0), axis=0, keepdims=True)
        ti = jnp.where(slot == kk, ids[kk], ti)
        tw = jnp.where(slot == kk, ex[kk] / den, tw)
        rk = jnp.where(slot == kk, r_k, rk)
    ti_ref[...] = ti.astype(jnp.int32)
    tw_ref[...] = tw
    rk_ref[...] = rk.astype(jnp.int32)
    carry = carry_sc[...] + jnp.sum(chosen, axis=1, keepdims=True)
    carry_sc[...] = carry
    cnt_ref[...] = carry


def _out_route(b, oa, ob, x2, mod, w_out_bf, g_ffn, w_router_bf, b_router, counts_in, x1_buf, seq):
    n = x2.shape[0]
    tiles = seq // TOK_TILE
    row = lambda i: (b * tiles + i, 0)
    local = lambda i: (i, 0)
    by_lane = lambda i: (0, i)
    fixed = lambda i: (0, 0)
    return pl.pallas_call(
        _out_route_kernel,
        grid=(tiles,),
        in_specs=[
            pl.BlockSpec((oa.shape[0], TOK_TILE, LANES), lambda i: (0, b * tiles + i, 0)),
            pl.BlockSpec((ob.shape[0], TOK_TILE, LANES), lambda i: (0, b * tiles + i, 0)),
            pl.BlockSpec((TOK_TILE, D_MODEL), row),
            pl.BlockSpec((None, SUBLANES, D_MODEL), lambda i: (b, 0, 0)),
            pl.BlockSpec((D_MODEL, D_MODEL), fixed),
            pl.BlockSpec((1, D_MODEL), fixed),
            pl.BlockSpec((N_EXPERTS, D_MODEL), fixed),
            pl.BlockSpec((N_EXPERTS, 1), fixed),
            pl.BlockSpec((N_EXPERTS, 1), fixed),
            pl.BlockSpec(memory_space=pl.ANY),
        ],
        out_specs=[
            pl.BlockSpec((TOK_TILE, D_MODEL), row),
            pl.BlockSpec((TOK_TILE, D_MODEL // 2), local),
            pl.BlockSpec((SUBLANES, TOK_TILE), by_lane),
            pl.BlockSpec((SUBLANES, TOK_TILE), by_lane),
            pl.BlockSpec((SUBLANES, TOK_TILE), by_lane),
            pl.BlockSpec((N_EXPERTS, 1), fixed),
        ],
        out_shape=[
            jax.ShapeDtypeStruct((n, D_MODEL), F32),
            jax.ShapeDtypeStruct((seq, D_MODEL // 2), jnp.int32),
            jax.ShapeDtypeStruct((SUBLANES, seq), jnp.int32),
            jax.ShapeDtypeStruct((SUBLANES, seq), F32),
            jax.ShapeDtypeStruct((SUBLANES, seq), jnp.int32),
            jax.ShapeDtypeStruct((N_EXPERTS, 1), F32),
        ],
        input_output_aliases={9: 0},
        scratch_shapes=[pltpu.VMEM((TOK_TILE, TOK_TILE), BF16), pltpu.VMEM((N_EXPERTS, 1), F32)],
        compiler_params=_cparams(1, VMEM_LIMIT),
        name="out_route",
    )(oa, ob, x2, mod, w_out_bf, g_ffn.reshape(1, D_MODEL), w_router_bf.T,
      b_router.reshape(N_EXPERTS, 1), counts_in, x1_buf)


def _experts_kernel(tiles_ref, rows_ref, base_ref, next_ref, xs_ref, wgu_ref, bgu_ref, wd_ref,
                    bd_ref, y_ref, wgu_sc, wd_sc, x_buf, y_buf, x_sem, y_sem):
    e = pl.program_id(0)
    n_tiles = tiles_ref[e]
    n_rows = rows_ref[e]
    n_live = base_ref[N_EXPERTS]
    region = e * EXPERT_ROWS

    def x_copy(row, slot):
        return pltpu.make_async_copy(xs_ref.at[pl.ds(row, ROW_TILE), :], x_buf.at[slot],
                                     x_sem.at[slot])

    def y_copy(row, slot):
        return pltpu.make_async_copy(y_buf.at[slot], y_ref.at[pl.ds(row, ROW_TILE), :],
                                     y_sem.at[slot])

    @pl.when(jnp.logical_and(e == 0, n_live > 0))
    def _():
        x_copy(pl.multiple_of(next_ref[N_EXPERTS], ROW_TILE), 0).start()

    @pl.when(n_tiles > 0)
    def _():
        wgu_sc[...] = wgu_ref[...].astype(BF16)
        wd_sc[...] = wd_ref[...].astype(BF16)

    def tile(j, carry):
        g = base_ref[e] + j
        slot = g % 2
        row = pl.multiple_of(region + j * ROW_TILE, ROW_TILE)
        x_copy(row, slot).wait()
        next_row = jnp.where(j + 1 < n_tiles, row + ROW_TILE, next_ref[e])

        @pl.when(g + 1 < n_live)
        def _():
            x_copy(pl.multiple_of(next_row, ROW_TILE), 1 - slot).start()

        used = lax.broadcasted_iota(jnp.int32, (ROW_TILE, 1), 0) < n_rows - j * ROW_TILE
        first, second = _unpack_bf16_pairs(jnp.where(used, x_buf[slot], 0))
        x = jnp.concatenate([first.astype(BF16), second.astype(BF16)], axis=1)
        acc = jnp.zeros((x.shape[0], D_MODEL), F32)
        for c in range(D_EXPERT // FF_CHUNK):
            lo_c, hi_c = c * FF_CHUNK, (c + 1) * FF_CHUNK
            gate = jnp.dot(x, wgu_sc[:, lo_c:hi_c], preferred_element_type=F32) + bgu_ref[:, lo_c:hi_c]
            up = (jnp.dot(x, wgu_sc[:, D_EXPERT + lo_c:D_EXPERT + hi_c], preferred_element_type=F32)
                  + bgu_ref[:, D_EXPERT + lo_c:D_EXPERT + hi_c])
            gate = jnp.minimum(gate, SWIGLU_LIMIT)
            up = jnp.clip(up, -SWIGLU_LIMIT, SWIGLU_LIMIT)
            act = (up + 1.0) * (gate * jax.nn.sigmoid(SWIGLU_ALPHA * gate))
            acc = acc + jnp.dot(act.astype(BF16), wd_sc[lo_c:hi_c, :], preferred_element_type=F32)

        @pl.when(g >= 2)
        def _():
            y_copy(0, slot).wait()

        y_buf[slot] = _pack_bf16_pairs(acc + bd_ref[...])
        y_copy(row, slot).start()
        return carry

    lax.fori_loop(0, n_tiles, tile, 0)

    @pl.when(e == pl.num_programs(0) - 1)
    def _():
        for back in (2, 1):
            @pl.when(n_live >= back)
            def _():
                y_copy(0, (n_live - back) % 2).wait()


def _experts(counts, xs, w_gate_up, b_gate_up, w_down, b_down):
    n_rows, width = xs.shape
    tiles = (counts + ROW_TILE - 1) // ROW_TILE
    base = jnp.concatenate([jnp.zeros((1,), jnp.int32), jnp.cumsum(tiles)]).astype(jnp.int32)
    region = jnp.arange(N_EXPERTS, dtype=jnp.int32) * EXPERT_ROWS
    later = jnp.arange(N_EXPERTS)[None, :] > jnp.arange(-1, N_EXPERTS)[:, None]
    cand = jnp.where(jnp.logical_and(later, (tiles > 0)[None, :]), region[None, :], n_rows)
    nxt = jnp.min(cand, axis=1)
    next_row = jnp.concatenate([nxt[1:], nxt[:1]]).astype(jnp.int32)
    by_expert = lambda e, *_: (e, 0, 0)
    return pl.pallas_call(
        _experts_kernel,
        grid_spec=pltpu.PrefetchScalarGridSpec(
            num_scalar_prefetch=4,
            grid=(N_EXPERTS,),
            in_specs=[
                pl.BlockSpec(memory_space=pl.ANY),
                pl.BlockSpec((None, D_MODEL, 2 * D_EXPERT), by_expert),
                pl.BlockSpec((None, 1, 2 * D_EXPERT), by_expert),
                pl.BlockSpec((None, D_EXPERT, D_MODEL), by_expert),
                pl.BlockSpec((None, 1, D_MODEL), by_expert),
            ],
            out_specs=pl.BlockSpec(memory_space=pl.ANY),
            scratch_shapes=[
                pltpu.VMEM((D_MODEL, 2 * D_EXPERT), BF16),
                pltpu.VMEM((D_EXPERT, D_MODEL), BF16),
                pltpu.VMEM((2, ROW_TILE, width), jnp.int32),
                pltpu.VMEM((2, ROW_TILE, D_MODEL // 2), jnp.int32),
                pltpu.SemaphoreType.DMA((2,)),
                pltpu.SemaphoreType.DMA((2,)),
            ],
        ),
        out_shape=jax.ShapeDtypeStruct((n_rows, D_MODEL // 2), jnp.int32),
        compiler_params=_cparams(1, VMEM_LIMIT),
        name="experts",
    )(tiles.astype(jnp.int32), counts.astype(jnp.int32), base, next_row, xs, w_gate_up,
      b_gate_up.reshape(N_EXPERTS, 1, 2 * D_EXPERT), w_down, b_down.reshape(N_EXPERTS, 1, D_MODEL))


def _sc_gather_rows(table, idx):
    m = idx.shape[0]
    width = table.shape[1]
    mesh = plsc.VectorSubcoreMesh(core_axis_name="core", subcore_axis_name="subcore")
    n_workers = mesh.num_cores * mesh.num_subcores
    per_worker = m // n_workers
    assert per_worker * n_workers == m and per_worker % SC_WINDOW == 0

    @pl.kernel(
        out_type=jax.ShapeDtypeStruct((m, width), table.dtype),
        mesh=mesh,
        scratch_types=[
            pltpu.VMEM((SC_WINDOW,), jnp.int32),
            pltpu.VMEM((SC_WINDOW, width), table.dtype),
            pltpu.SemaphoreType.DMA,
        ],
    )
    def gather_kernel(table_hbm, idx_hbm, out_hbm, idx_v, rows_v, sem):
        worker = lax.axis_index("subcore") * mesh.num_cores + lax.axis_index("core")

        @pl.loop(0, per_worker // SC_WINDOW)
        def _(j):
            base = pl.multiple_of(worker * per_worker + j * SC_WINDOW, SC_WINDOW)
            pltpu.sync_copy(idx_hbm.at[pl.ds(base, SC_WINDOW)], idx_v)
            pltpu.async_copy(table_hbm.at[idx_v], rows_v, sem).wait()
            pltpu.sync_copy(rows_v, out_hbm.at[pl.ds(base, SC_WINDOW)])

    return gather_kernel(table, idx)


def _sc_scatter_rows_into(table, dst, out_ref, reps):
    m, width = table.shape
    mesh = plsc.VectorSubcoreMesh(core_axis_name="core", subcore_axis_name="subcore")
    n_workers = mesh.num_cores * mesh.num_subcores
    per_worker = m // n_workers
    assert per_worker * n_workers == m and per_worker % SC_WINDOW == 0 and dst.shape == (reps * m,)

    @pl.kernel(
        out_type=(),
        mesh=mesh,
        scratch_types=[
            pltpu.VMEM((SC_WINDOW,), jnp.int32),
            pltpu.VMEM((SC_WINDOW, width), table.dtype),
            pltpu.SemaphoreType.DMA,
        ],
    )
    def scatter_kernel(table_hbm, dst_hbm, out_hbm, dst_v, rows_v, sem):
        worker = lax.axis_index("subcore") * mesh.num_cores + lax.axis_index("core")

        @pl.loop(0, per_worker // SC_WINDOW)
        def _(j):
            base = pl.multiple_of(worker * per_worker + j * SC_WINDOW, SC_WINDOW)
            pltpu.sync_copy(table_hbm.at[pl.ds(base, SC_WINDOW)], rows_v)
            for k in range(reps):
                pltpu.sync_copy(dst_hbm.at[pl.ds(k * m + base, SC_WINDOW)], dst_v)
                pltpu.async_copy(rows_v, out_hbm.at[dst_v], sem).wait()

    scatter_kernel(table, dst, out_ref)


def _combine_dense_kernel(tw_ref, x1_ref, mod_ref, g_ref, y0_ref, y1_ref, y2_ref, y3_ref, o_in_ref,
                          o_ref):
    del o_in_ref
    tm = x1_ref.shape[0]
    tw = jnp.concatenate([tw_ref[...], jnp.zeros((LANES - SUBLANES, tm), F32)], axis=0).T
    first = second = None
    for kk, y_ref in enumerate((y0_ref, y1_ref, y2_ref, y3_ref)):
        f_k, s_k = _unpack_bf16_pairs(y_ref[...])
        w_k = tw[:, kk:kk + 1]
        first = w_k * f_k if kk == 0 else first + w_k * f_k
        second = w_k * s_k if kk == 0 else second + w_k * s_k
    ffn = jnp.concatenate([first, second], axis=1)
    x2 = x1_ref[...] + mod_ref[5:6, :] * ffn
    o_ref[...] = x2 * lax.rsqrt(jnp.mean(x2 * x2, axis=-1, keepdims=True) + EPS) * g_ref[...]


def _combine_dense(b, tw, x1, mod, g_final, yg, out_buf, seq):
    n = x1.shape[0]
    width = yg.shape[1]
    tiles = seq // TOK_TILE
    row = lambda i: (b * tiles + i, 0)
    slot = lambda kk: pl.BlockSpec((TOK_TILE, width), lambda i: (kk * tiles + i, 0))
    return pl.pallas_call(
        _combine_dense_kernel,
        grid=(tiles,),
        in_specs=[
            pl.BlockSpec((SUBLANES, TOK_TILE), lambda i: (0, i)),
            pl.BlockSpec((TOK_TILE, D_MODEL), row),
            pl.BlockSpec((None, SUBLANES, D_MODEL), lambda i: (b, 0, 0)),
            pl.BlockSpec((1, D_MODEL), lambda i: (0, 0)),
        ] + [slot(kk) for kk in range(TOP_K)] + [pl.BlockSpec(memory_space=pl.ANY)],
        out_specs=pl.BlockSpec((TOK_TILE, D_MODEL), row),
        out_shape=jax.ShapeDtypeStruct((n, D_MODEL), F32),
        input_output_aliases={4 + TOP_K: 0},
        compiler_params=_cparams(1, VMEM_LIMIT),
        name="combine",
    )(tw, x1, mod, g_final.reshape(1, D_MODEL), yg, yg, yg, yg, out_buf)


def _rope_base(positions):
    half = ROPE_DIM // 2
    inv_freq = ROPE_THETA ** (-jnp.arange(0, ROPE_DIM, 2, dtype=F32) / ROPE_DIM)
    n = positions.size
    pos = jnp.broadcast_to(positions.reshape(n, 1).astype(F32), (n, half)).reshape(-1, LANES)
    ang = pos * jnp.tile(inv_freq, LANES // half)
    cos, sin = lax.optimization_barrier((jnp.cos(ang), jnp.sin(ang)))
    pad = jnp.zeros((n, LANES - ROPE_DIM), F32)
    return jnp.concatenate([cos.reshape(n, half), sin.reshape(n, half), pad], axis=1)


def kernel(x, c, positions, w_ada, b_ada, g_mix_norm, w_in, rel_bias, lambda_q1, lambda_k1,
           lambda_q2, lambda_k2, g_subln, w_out, g_ffn_norm, w_router, b_router, w_gate_up,
           b_gate_up, w_down, b_down, g_final):
    batch, seq, _ = x.shape
    depth = w_ada.shape[0]
    assert depth == 1, "the combine kernel applies the final norm, so it must follow the only layer"
    n = batch * seq
    rope_base = _rope_base(positions)
    x2 = x.reshape(n, D_MODEL)
    for l in range(depth):
        lambda_init = 0.8 - 0.6 * math.exp(-0.3 * l)
        mod = _ada(c, w_ada[l], b_ada[l])
        qa, ka, va, qb, kb, vb = _in_proj(x2, mod, g_mix_norm[l], w_in[l].astype(BF16),
                                          rope_base, seq)
        oa = _attn_a(qa, ka, va, _rel_bias_rows(rel_bias[l]), batch, seq)
        lam = (jnp.exp(jnp.sum(lambda_q1[l].astype(F32) * lambda_k1[l].astype(F32)))
               - jnp.exp(jnp.sum(lambda_q2[l].astype(F32) * lambda_k2[l].astype(F32)))
               + lambda_init).reshape(1)
        ob = _attn_b(lam, qb, kb, vb, g_subln[l], batch, seq, 1.0 - lambda_init)
        w_out_bf = w_out[l].astype(BF16)
        w_router_bf = w_router[l].astype(BF16)
        xs_ref = jax.new_ref(lax.empty((N_EXPERTS * EXPERT_ROWS, D_MODEL // 2), jnp.int32))
        x1 = lax.empty((n, D_MODEL), F32)
        counts = jnp.zeros((N_EXPERTS, 1), F32)
        pos_rows, weight_rows = [], []
        for b in range(batch):
            x1, hp, top_i, top_w, rank, counts = _out_route(
                b, oa, ob, x2, mod, w_out_bf, g_ffn_norm[l], w_router_bf, b_router[l], counts, x1,
                seq)
            pos = top_i * EXPERT_ROWS + rank
            pos_rows.append(pos)
            weight_rows.append(top_w)
            dst = jnp.concatenate([pos[kk] for kk in range(TOP_K)])
            _sc_scatter_rows_into(hp, dst, xs_ref, TOP_K)
        xs = jax.freeze(xs_ref)
        y = _experts(counts[:, 0].astype(jnp.int32), xs, w_gate_up[l], b_gate_up[l], w_down[l],
                     b_down[l])
        x2 = lax.empty((n, D_MODEL), F32)
        for b in range(batch):
            yg = _sc_gather_rows(y, jnp.concatenate([pos_rows[b][kk] for kk in range(TOP_K)]))
            x2 = _combine_dense(b, weight_rows[b], x1, mod, g_final, yg, x2, seq)
    return x2.reshape(batch, seq, D_MODEL)
```

```python
import functools
import math

import jax
import jax.numpy as jnp
from jax import lax
from jax.experimental import pallas as pl
from jax.experimental.pallas import tpu as pltpu
from jax.experimental.pallas import tpu_sc as plsc

D_MODEL = 1024
CHUNK = 64
HEAD_DIM = 64
A_HEADS = 8
A_WIDTH = A_HEADS * HEAD_DIM
LEFT_CHUNKS = 8
MAX_REL = 128
B_HEADS = 4
B_QK_DIM = HEAD_DIM
B_V_DIM = 2 * HEAD_DIM
B_WIDTH = B_HEADS * B_V_DIM
ROPE_THETA = 500000.0
ROPE_DIM = B_QK_DIM // 4
N_EXPERTS = 32
TOP_K = 4
D_EXPERT = D_MODEL
SWIGLU_LIMIT = 7.0
SWIGLU_ALPHA = 1.702
EPS = 1e-6
NEG_INF = -1e30
LOG2_E = math.log2(math.e)
N_MOD = 6

LANES = 128
SUBLANES = 8
VMEM_BYTES_V7X = 64 * 1024 * 1024
VMEM_LIMIT = VMEM_BYTES_V7X * 7 // 8

TOK_TILE = 512
A_QBLK = 2 * CHUNK
A_BAND = (LEFT_CHUNKS + 2) * CHUNK
A_ROLL = A_BAND + A_QBLK
B_TQ = 512
B_TK = 512
B_SUB = 128
ROW_TILE = 512
ROW_STEP = 128
FF_CHUNK = 512
EXPERT_ROWS = 16384
SC_WINDOW = 128

F32 = jnp.float32
BF16 = jnp.bfloat16


def _cparams(n_axes, vmem=None):
    return pltpu.CompilerParams(
        dimension_semantics=("arbitrary",) * n_axes,
        vmem_limit_bytes=vmem,
    )


def _ada_kernel(c_ref, w_ref, b_ref, o_ref):
    c = c_ref[...]
    act = c * jax.nn.sigmoid(c)
    o_ref[...] = jnp.dot(act, w_ref[...], preferred_element_type=F32,
                         precision=lax.Precision.HIGHEST) + b_ref[...]


def _ada(c, w_ada, b_ada):
    b = c.shape[0]
    rows = -(-b // SUBLANES) * SUBLANES
    c_pad = jnp.pad(c, ((0, rows - b), (0, 0)))
    n_out = w_ada.shape[1]
    out = pl.pallas_call(
        _ada_kernel,
        grid=(n_out // D_MODEL,),
        in_specs=[
            pl.BlockSpec((rows, D_MODEL), lambda j: (0, 0)),
            pl.BlockSpec((D_MODEL, D_MODEL), lambda j: (0, j)),
            pl.BlockSpec((1, D_MODEL), lambda j: (0, j)),
        ],
        out_specs=pl.BlockSpec((rows, D_MODEL), lambda j: (0, j)),
        out_shape=jax.ShapeDtypeStruct((rows, n_out), F32),
        compiler_params=_cparams(1),
        name="ada",
    )(c_pad, w_ada, b_ada.reshape(1, n_out))
    mod = out[:b].reshape(b, N_MOD, D_MODEL)
    return jnp.pad(mod, ((0, 0), (0, SUBLANES - N_MOD), (0, 0)))


def _in_proj_kernel(x_ref, mod_ref, g_ref, w_ref, rb_ref,
                    qa_ref, ka_ref, va_ref, qb_ref, kb_ref, vb_ref):
    x = x_ref[...]
    mod = mod_ref[...]
    y = x * lax.rsqrt(jnp.mean(x * x, axis=-1, keepdims=True) + EPS) * g_ref[...]
    h = (y * (1.0 + mod[1:2, :]) + mod[0:1, :]).astype(BF16)
    q_scale = HEAD_DIM ** -0.5

    half = ROPE_DIM // 2
    rb = rb_ref[...]
    lane = lax.broadcasted_iota(jnp.int32, rb.shape, 1)
    cos_lo = jnp.where(lane < half, rb, 0.0)
    sin_hi = jnp.where(jnp.logical_and(lane >= half, lane < ROPE_DIM), rb, 0.0)
    cos_pair = cos_lo + pltpu.roll(cos_lo, half, 1)
    rc = (cos_pair + pltpu.roll(cos_pair, B_QK_DIM, 1)
          + jnp.where(lane % B_QK_DIM >= ROPE_DIM, 1.0, 0.0))
    sin_lo = pltpu.roll(sin_hi, LANES - half, 1)
    rm = -(sin_lo + pltpu.roll(sin_lo, B_QK_DIM, 1))
    rp = sin_hi + pltpu.roll(sin_hi, B_QK_DIM, 1)

    def rope(p):
        cols = []
        for s in range(p.shape[1] // LANES):
            v = p[:, s * LANES:(s + 1) * LANES]
            cols.append(v * rc + pltpu.roll(v, LANES - ROPE_DIM // 2, 1) * rm
                        + pltpu.roll(v, ROPE_DIM // 2, 1) * rp)
        return jnp.concatenate(cols, axis=1)

    outs = (qa_ref, ka_ref, va_ref, qb_ref, kb_ref, vb_ref)
    for j, o_ref in enumerate(outs):
        p = jnp.dot(h, w_ref[:, j * A_WIDTH:(j + 1) * A_WIDTH], preferred_element_type=F32)
        if j in (3, 4):
            p = rope(p)
        if j in (0, 3):
            p = p * (q_scale * LOG2_E)
        pb = p.astype(BF16)
        for s in range(A_WIDTH // LANES):
            o_ref[s] = pb[:, s * LANES:(s + 1) * LANES]


def _in_proj(x2, mod, g_mix, w_in_bf, rope_base, seq):
    n = x2.shape[0]
    tiles_per_seq = seq // TOK_TILE
    row = lambda i: (i, 0)
    fixed = lambda i: (0, 0)
    n_slabs = A_WIDTH // LANES
    out_sd = jax.ShapeDtypeStruct((n_slabs, n, LANES), BF16)
    return pl.pallas_call(
        _in_proj_kernel,
        grid=(n // TOK_TILE,),
        in_specs=[
            pl.BlockSpec((TOK_TILE, D_MODEL), row),
            pl.BlockSpec((None, SUBLANES, D_MODEL), lambda i: (i // tiles_per_seq, 0, 0)),
            pl.BlockSpec((1, D_MODEL), fixed),
            pl.BlockSpec(w_in_bf.shape, fixed),
            pl.BlockSpec((TOK_TILE, LANES), row),
        ],
        out_specs=[pl.BlockSpec((n_slabs, TOK_TILE, LANES), lambda i: (0, i, 0))] * 6,
        out_shape=[out_sd] * 6,
        compiler_params=_cparams(1, VMEM_LIMIT),
        name="in_proj",
    )(x2, mod, g_mix.reshape(1, D_MODEL), w_in_bf, rope_base)


def _attn_a_kernel(q_ref, kp_ref, kc_ref, vp_ref, vc_ref, bias_ref, o_ref, k_sc, v_sc, bias_sc):
    g = pl.program_id(1)
    n_pairs, blk, _ = q_ref.shape
    lane = lax.broadcasted_iota(jnp.int32, (A_QBLK, LANES), 1)
    col = lax.broadcasted_iota(jnp.int32, (A_QBLK, A_BAND), 1)
    ones = jnp.ones((A_BAND, LANES), BF16)

    @pl.when(jnp.logical_and(pl.program_id(0) == 0, g == 0))
    def _():
        q_chunk = lax.broadcasted_iota(jnp.int32, (A_QBLK, A_BAND), 0) // CHUNK
        k_chunk = col // CHUNK
        in_band = jnp.logical_and(k_chunk >= q_chunk, k_chunk <= q_chunk + LEFT_CHUNKS)
        for h in range(bias_ref.shape[0]):
            rolled = pltpu.roll(jnp.broadcast_to(bias_ref[h], (A_QBLK, A_ROLL)), 0, 1,
                                stride=1, stride_axis=0)
            bias_sc[h] = jnp.where(in_band, rolled[:, :A_BAND], NEG_INF)

    def pair(p):
        k_buf = k_sc.at[p % 2]
        v_buf = v_sc.at[p % 2]
        k_buf[0:blk, :] = kp_ref[p]
        k_buf[blk:2 * blk, :] = kc_ref[p]
        v_buf[0:blk, :] = vp_ref[p]
        v_buf[blk:2 * blk, :] = vc_ref[p]

        def scores(m, hh):
            r0 = m * A_QBLK
            q = q_ref[p, r0:r0 + A_QBLK, :]
            in_head = (lane < HEAD_DIM) if hh == 0 else (lane >= HEAD_DIM)
            qh = jnp.where(in_head, q, jnp.zeros_like(q))
            return lax.dot_general(qh, k_buf[r0:r0 + A_BAND, :], (((1,), (1,)), ((), ())),
                                   preferred_element_type=F32)

        chains = [(m, hh) for m in range(blk // A_QBLK) for hh in range(2)]
        s_next = scores(*chains[0])
        halves = []
        for i, (m, hh) in enumerate(chains):
            r0 = m * A_QBLK
            s = s_next
            if i + 1 < len(chains):
                s_next = scores(*chains[i + 1])
            valid = jnp.logical_or(g > 0, col + r0 >= blk)
            s = jnp.where(valid, s + bias_sc[2 * p + hh], NEG_INF)
            pr = jnp.exp2(s - jnp.max(s, axis=1, keepdims=True))
            v_ext = jnp.concatenate([v_buf[r0:r0 + A_BAND, :], ones], axis=1)
            pv = jnp.dot(pr.astype(BF16), v_ext, preferred_element_type=F32)
            halves.append(pv[:, :LANES] / pv[:, LANES:])
            if hh == 1:
                o_ref[p, r0:r0 + A_QBLK, :] = jnp.where(lane < HEAD_DIM, halves[0],
                                                        halves[1]).astype(BF16)
                halves = []

    for p in range(n_pairs):
        pair(p)


def _attn_a(qa, ka, va, bias_rows, batch, seq):
    n_pairs, n, _ = qa.shape
    blk = LEFT_CHUNKS * CHUNK
    nblk = seq // blk
    cur = lambda b, g: (0, b * nblk + g, 0)
    prev = lambda b, g: (0, b * nblk + jnp.maximum(g - 1, 0), 0)
    slab = (n_pairs, blk, LANES)
    return pl.pallas_call(
        _attn_a_kernel,
        grid=(batch, nblk),
        in_specs=[
            pl.BlockSpec(slab, cur),
            pl.BlockSpec(slab, prev),
            pl.BlockSpec(slab, cur),
            pl.BlockSpec(slab, prev),
            pl.BlockSpec(slab, cur),
            pl.BlockSpec(bias_rows.shape, lambda b, g: (0, 0, 0)),
        ],
        out_specs=pl.BlockSpec(slab, cur),
        out_shape=jax.ShapeDtypeStruct((n_pairs, n, LANES), BF16),
        scratch_shapes=[
            pltpu.VMEM((2, 2 * blk, LANES), BF16),
            pltpu.VMEM((2, 2 * blk, LANES), BF16),
            pltpu.VMEM((bias_rows.shape[0], A_QBLK, A_BAND), F32),
        ],
        compiler_params=_cparams(2),
        name="attn_a",
    )(qa, ka, ka, va, va, bias_rows)


def _rel_bias_rows(rel_table):
    t = rel_table.astype(F32) * LOG2_E
    far = t[:, 2 * MAX_REL:]
    n_far = LEFT_CHUNKS * CHUNK - MAX_REL
    row = jnp.concatenate([
        jnp.broadcast_to(far, (t.shape[0], n_far)),
        t[:, 2 * MAX_REL:0:-1],
        jnp.broadcast_to(far, (t.shape[0], A_ROLL - A_BAND)),
    ], axis=1)
    return row.reshape(t.shape[0], 1, A_ROLL)


def _attn_b_kernel(lam_ref, q_ref, k_ref, v_ref, g_ref, o_ref, q_sc, s_sc, m_sc, acc_sc, *,
                   out_scale):
    qi = pl.program_id(2)
    n_sub = B_TQ // B_SUB
    lane = lax.broadcasted_iota(jnp.int32, (B_TQ, LANES), 1)
    q = q_ref[...]
    q_sc[0] = jnp.where(lane < B_QK_DIM, q, jnp.zeros_like(q))
    q_sc[1] = jnp.where(lane >= B_QK_DIM, q, jnp.zeros_like(q))
    m_sc[...] = jnp.full(m_sc.shape, NEG_INF, F32)
    acc_sc[...] = jnp.zeros(acc_sc.shape, F32)
    ones = jnp.ones((B_TK, LANES), BF16)

    def score_rows(k, slot, sub, c):
        rows = pl.ds(sub * B_SUB, B_SUB)
        s_sc[slot, c, rows, :] = lax.dot_general(q_sc[c, rows, :], k, (((1,), (1,)), ((), ())),
                                                 preferred_element_type=F32)

    def update_rows(s, v_ext, sub, c):
        rows = pl.ds(sub * B_SUB, B_SUB)
        m_prev = m_sc[c, rows, :]
        m_new = jnp.maximum(m_prev, jnp.max(s, axis=1, keepdims=True))
        alpha = jnp.exp2(m_prev - m_new)
        p = jnp.exp2(s - jnp.concatenate([m_new] * (s.shape[1] // LANES), axis=1))
        pv = jnp.dot(p.astype(BF16), v_ext, preferred_element_type=F32)
        acc_sc[c, rows, :] = jnp.concatenate([alpha, alpha], axis=1) * acc_sc[c, rows, :] + pv
        m_sc[c, rows, :] = m_new

    def key_block(blk):
        return k_ref[pl.ds(pl.multiple_of(blk * B_TK, B_TK), B_TK), :]

    def value_block(blk):
        return jnp.concatenate([v_ref[pl.ds(pl.multiple_of(blk * B_TK, B_TK), B_TK), :], ones], axis=1)

    def step(blk, slot, next_blk):
        v_ext = value_block(blk)
        k_next = key_block(next_blk)
        for sub in range(n_sub):
            for c in range(2):
                update_rows(s_sc[slot, c, pl.ds(sub * B_SUB, B_SUB), :], v_ext, sub, c)
                score_rows(k_next, 1 - slot, sub, c)

    def diagonal_step(slot):
        v_ext = value_block(qi)
        col_c = lax.broadcasted_iota(jnp.int32, (B_SUB, B_TK), 1) // CHUNK
        row_c = lax.broadcasted_iota(jnp.int32, (B_SUB, B_TK), 0) // CHUNK
        for sub in range(n_sub):
            keep = col_c <= row_c + sub * (B_SUB // CHUNK)
            for c in range(2):
                s = s_sc[slot, c, pl.ds(sub * B_SUB, B_SUB), :]
                update_rows(jnp.where(keep, s, NEG_INF), v_ext, sub, c)

    k0 = key_block(0)
    for sub in range(n_sub):
        for c in range(2):
            score_rows(k0, 0, sub, c)

    def pair(p, carry):
        first = 2 * p
        step(first, 0, first + 1)
        step(first + 1, 1, first + 2)
        return carry

    lax.fori_loop(0, qi // 2, pair, 0)

    @pl.when(qi % 2 == 0)
    def _():
        diagonal_step(0)

    @pl.when(qi % 2 == 1)
    def _():
        step(qi - 1, 0, qi)
        diagonal_step(1)

    lam = lam_ref[0]
    a0 = acc_sc[0]
    a1 = acc_sc[1]
    o = a0[:, :B_V_DIM] / a0[:, B_V_DIM:] - lam * (a1[:, :B_V_DIM] / a1[:, B_V_DIM:])
    y = o * lax.rsqrt(jnp.mean(o * o, axis=-1, keepdims=True) + EPS) * g_ref[...]
    o_ref[...] = (y * out_scale).astype(BF16)


def _attn_b(lam, qb, kb, vb, g_subln, batch, seq, out_scale):
    assert B_TQ == B_TK and B_V_DIM == LANES
    n = qb.shape[1]
    nq = seq // B_TQ
    q_map = lambda b, h, qi: (h, b * nq + qi, 0)
    kv_map = lambda b, h, qi: (h, b, 0)
    return pl.pallas_call(
        functools.partial(_attn_b_kernel, out_scale=out_scale),
        grid=(batch, B_HEADS, nq),
        in_specs=[
            pl.BlockSpec(memory_space=pltpu.SMEM),
            pl.BlockSpec((None, B_TQ, LANES), q_map),
            pl.BlockSpec((None, seq, LANES), kv_map),
            pl.BlockSpec((None, seq, LANES), kv_map),
            pl.BlockSpec((1, B_V_DIM), lambda b, h, qi: (0, 0)),
        ],
        out_specs=pl.BlockSpec((None, B_TQ, LANES), q_map),
        out_shape=jax.ShapeDtypeStruct((B_HEADS, n, LANES), BF16),
        scratch_shapes=[
            pltpu.VMEM((2, B_TQ, LANES), BF16),
            pltpu.VMEM((2, 2, B_TQ, B_TK), F32),
            pltpu.VMEM((2, B_TQ, LANES), F32),
            pltpu.VMEM((2, B_TQ, 2 * LANES), F32),
        ],
        compiler_params=_cparams(3),
        name="attn_b",
    )(lam, qb, kb, vb, g_subln.reshape(1, B_V_DIM))


def _pack_bf16_pairs(v):
    half = v.shape[1] // 2
    vb = v.astype(BF16)
    hi = lax.bitcast_convert_type(vb[:, :half].astype(F32), jnp.int32)
    lo = lax.bitcast_convert_type(vb[:, half:].astype(F32), jnp.int32)
    return hi | lax.shift_right_logical(lo, jnp.full(lo.shape, 16, jnp.int32))


def _unpack_bf16_pairs(w):
    first = lax.bitcast_convert_type(w & jnp.int32(-65536), F32)
    second = lax.bitcast_convert_type(lax.shift_left(w, jnp.full(w.shape, 16, jnp.int32)), F32)
    return first, second


def _out_route_kernel(oa_ref, ob_ref, x_ref, mod_ref, wo_ref, g_ref, wr_ref, br_ref, cin_ref,
                      x1_in_ref, x1_ref, hp_ref, ti_ref, tw_ref, rk_ref, cnt_ref, tri_sc, carry_sc):
    del x1_in_ref
    i = pl.program_id(0)
    tm = x_ref.shape[0]

    @pl.when(i == 0)
    def _():
        r = lax.broadcasted_iota(jnp.int32, (tm, tm), 0)
        c = lax.broadcasted_iota(jnp.int32, (tm, tm), 1)
        tri_sc[...] = jnp.where(r < c, 1.0, 0.0).astype(BF16)
        carry_sc[...] = cin_ref[...]

    mod = mod_ref[...]
    o = jnp.concatenate([oa_ref[s] for s in range(oa_ref.shape[0])]
                        + [ob_ref[s] for s in range(ob_ref.shape[0])], axis=1)
    mix = jnp.dot(o, wo_ref[...], preferred_element_type=F32)
    x1 = x_ref[...] + mod[2:3, :] * mix
    x1_ref[...] = x1
    y = x1 * lax.rsqrt(jnp.mean(x1 * x1, axis=-1, keepdims=True) + EPS) * g_ref[...]
    h = y * (1.0 + mod[4:5, :]) + mod[3:4, :]
    hb = h.astype(BF16)
    hp_ref[...] = _pack_bf16_pairs(h)

    logits = lax.dot_general(wr_ref[...], hb, (((1,), (1,)), ((), ())),
                             preferred_element_type=F32) + br_ref[...]
    eid = lax.broadcasted_iota(jnp.int32, logits.shape, 0).astype(F32)
    work = logits
    vals, ids = [], []
    chosen = jnp.zeros(logits.shape, F32)
    for _ in range(TOP_K):
        v = jnp.max(work, axis=0, keepdims=True)
        e = jnp.min(jnp.where(work == v, eid, float(N_EXPERTS)), axis=0, keepdims=True)
        hit = eid == e
        vals.append(v)
        ids.append(e)
        chosen = jnp.where(hit, 1.0, chosen)
        work = jnp.where(hit, -jnp.inf, work)
    ex = [jnp.exp(v - vals[0]) for v in vals]
    den = ex[0] + ex[1] + ex[2] + ex[3]

    before = jnp.dot(chosen.astype(BF16), tri_sc[...], preferred_element_type=F32) + carry_sc[...]
    slot = lax.broadcasted_iota(jnp.int32, (SUBLANES, tm), 0)
    ti = jnp.zeros((SUBLANES, tm), F32)
    tw = jnp.zeros((SUBLANES, tm), F32)
    rk = jnp.zeros((SUBLANES, tm), F32)
    for kk in range(TOP_K):
        r_k = jnp.sum(jnp.where(eid == ids[kk], before, 0.0), axis=0, keepdims=True)
        ti = jnp.where(slot == kk, ids[kk], ti)
        tw = jnp.where(slot == kk, ex[kk] / den, tw)
        rk = jnp.where(slot == kk, r_k, rk)
    ti_ref[...] = ti.astype(jnp.int32)
    tw_ref[...] = tw
    rk_ref[...] = rk.astype(jnp.int32)
    carry = carry_sc[...] + jnp.sum(chosen, axis=1, keepdims=True)
    carry_sc[...] = carry
    cnt_ref[...] = carry


def _out_route(b, oa, ob, x2, mod, w_out_bf, g_ffn, w_router_bf, b_router, counts_in, x1_buf, seq):
    n = x2.shape[0]
    tiles = seq // TOK_TILE
    row = lambda i: (b * tiles + i, 0)
    local = lambda i: (i, 0)
    by_lane = lambda i: (0, i)
    fixed = lambda i: (0, 0)
    return pl.pallas_call(
        _out_route_kernel,
        grid=(tiles,),
        in_specs=[
            pl.BlockSpec((oa.shape[0], TOK_TILE, LANES), lambda i: (0, b * tiles + i, 0)),
            pl.BlockSpec((ob.shape[0], TOK_TILE, LANES), lambda i: (0, b * tiles + i, 0)),
            pl.BlockSpec((TOK_TILE, D_MODEL), row),
            pl.BlockSpec((None, SUBLANES, D_MODEL), lambda i: (b, 0, 0)),
            pl.BlockSpec((D_MODEL, D_MODEL), fixed),
            pl.BlockSpec((1, D_MODEL), fixed),
            pl.BlockSpec((N_EXPERTS, D_MODEL), fixed),
            pl.BlockSpec((N_EXPERTS, 1), fixed),
            pl.BlockSpec((N_EXPERTS, 1), fixed),
            pl.BlockSpec(memory_space=pl.ANY),
        ],
        out_specs=[
            pl.BlockSpec((TOK_TILE, D_MODEL), row),
            pl.BlockSpec((TOK_TILE, D_MODEL // 2), local),
            pl.BlockSpec((SUBLANES, TOK_TILE), by_lane),
            pl.BlockSpec((SUBLANES, TOK_TILE), by_lane),
            pl.BlockSpec((SUBLANES, TOK_TILE), by_lane),
            pl.BlockSpec((N_EXPERTS, 1), fixed),
        ],
        out_shape=[
            jax.ShapeDtypeStruct((n, D_MODEL), F32),
            jax.ShapeDtypeStruct((seq, D_MODEL // 2), jnp.int32),
            jax.ShapeDtypeStruct((SUBLANES, seq), jnp.int32),
            jax.ShapeDtypeStruct((SUBLANES, seq), F32),
            jax.ShapeDtypeStruct((SUBLANES, seq), jnp.int32),
            jax.ShapeDtypeStruct((N_EXPERTS, 1), F32),
        ],
        input_output_aliases={9: 0},
        scratch_shapes=[pltpu.VMEM((TOK_TILE, TOK_TILE), BF16), pltpu.VMEM((N_EXPERTS, 1), F32)],
        compiler_params=_cparams(1, VMEM_LIMIT),
        name="out_route",
    )(oa, ob, x2, mod, w_out_bf, g_ffn.reshape(1, D_MODEL), w_router_bf.T,
      b_router.reshape(N_EXPERTS, 1), counts_in, x1_buf)


def _experts_kernel(tiles_ref, rows_ref, base_ref, next_ref, xs_ref, wgu_ref, bgu_ref, wd_ref,
                    bd_ref, y_ref, wgu_sc, wd_sc, x_buf, y_buf, x_sem, y_sem):
    e = pl.program_id(0)
    n_tiles = tiles_ref[e]
    n_rows = rows_ref[e]
    n_live = base_ref[N_EXPERTS]
    region = e * EXPERT_ROWS

    def x_copy(row, slot):
        return pltpu.make_async_copy(xs_ref.at[pl.ds(row, ROW_TILE), :], x_buf.at[slot],
                                     x_sem.at[slot])

    def y_copy(row, slot):
        return pltpu.make_async_copy(y_buf.at[slot], y_ref.at[pl.ds(row, ROW_TILE), :],
                                     y_sem.at[slot])

    @pl.when(e == 0)
    def _():
        y_buf[...] = jnp.zeros(y_buf.shape, jnp.int32)

        @pl.when(n_live > 0)
        def _():
            x_copy(pl.multiple_of(next_ref[N_EXPERTS], ROW_TILE), 0).start()

    @pl.when(n_tiles > 0)
    def _():
        wgu_sc[...] = wgu_ref[...].astype(BF16)
        wd_sc[...] = wd_ref[...].astype(BF16)

    def tile(j, carry):
        g = base_ref[e] + j
        slot = g % 2
        row = pl.multiple_of(region + j * ROW_TILE, ROW_TILE)
        x_copy(row, slot).wait()
        next_row = jnp.where(j + 1 < n_tiles, row + ROW_TILE, next_ref[e])

        @pl.when(g + 1 < n_live)
        def _():
            x_copy(pl.multiple_of(next_row, ROW_TILE), 1 - slot).start()

        @pl.when(g >= 2)
        def _():
            y_copy(0, slot).wait()

        needed = jnp.minimum(n_rows - j * ROW_TILE, ROW_TILE)
        for rows in range(ROW_STEP, ROW_TILE + 1, ROW_STEP):
            @pl.when(jnp.logical_and(needed > rows - ROW_STEP, needed <= rows))
            def _(rows=rows):
                used = lax.broadcasted_iota(jnp.int32, (rows, 1), 0) < needed
                first, second = _unpack_bf16_pairs(jnp.where(used, x_buf[slot, :rows, :], 0))
                x = jnp.concatenate([first.astype(BF16), second.astype(BF16)], axis=1)
                acc = jnp.zeros((rows, D_MODEL), F32)
                for c in range(D_EXPERT // FF_CHUNK):
                    lo_c, hi_c = c * FF_CHUNK, (c + 1) * FF_CHUNK
                    gate = (jnp.dot(x, wgu_sc[:, lo_c:hi_c], preferred_element_type=F32)
                            + bgu_ref[:, lo_c:hi_c])
                    up = (jnp.dot(x, wgu_sc[:, D_EXPERT + lo_c:D_EXPERT + hi_c],
                                  preferred_element_type=F32)
                          + bgu_ref[:, D_EXPERT + lo_c:D_EXPERT + hi_c])
                    gate = jnp.minimum(gate, SWIGLU_LIMIT)
                    up = jnp.clip(up, -SWIGLU_LIMIT, SWIGLU_LIMIT)
                    act = (up + 1.0) * (gate * jax.nn.sigmoid(SWIGLU_ALPHA * gate))
                    acc = acc + jnp.dot(act.astype(BF16), wd_sc[lo_c:hi_c, :],
                                        preferred_element_type=F32)
                y_buf[slot, :rows, :] = _pack_bf16_pairs(acc + bd_ref[...])

        y_copy(row, slot).start()
        return carry

    lax.fori_loop(0, n_tiles, tile, 0)

    @pl.when(e == pl.num_programs(0) - 1)
    def _():
        for back in (2, 1):
            @pl.when(n_live >= back)
            def _():
                y_copy(0, (n_live - back) % 2).wait()


def _experts(counts, xs, w_gate_up, b_gate_up, w_down, b_down):
    n_rows, width = xs.shape
    tiles = (counts + ROW_TILE - 1) // ROW_TILE
    base = jnp.concatenate([jnp.zeros((1,), jnp.int32), jnp.cumsum(tiles)]).astype(jnp.int32)
    region = jnp.arange(N_EXPERTS, dtype=jnp.int32) * EXPERT_ROWS
    later = jnp.arange(N_EXPERTS)[None, :] > jnp.arange(-1, N_EXPERTS)[:, None]
    cand = jnp.where(jnp.logical_and(later, (tiles > 0)[None, :]), region[None, :], n_rows)
    nxt = jnp.min(cand, axis=1)
    next_row = jnp.concatenate([nxt[1:], nxt[:1]]).astype(jnp.int32)
    by_expert = lambda e, *_: (e, 0, 0)
    return pl.pallas_call(
        _experts_kernel,
        grid_spec=pltpu.PrefetchScalarGridSpec(
            num_scalar_prefetch=4,
            grid=(N_EXPERTS,),
            in_specs=[
                pl.BlockSpec(memory_space=pl.ANY),
                pl.BlockSpec((None, D_MODEL, 2 * D_EXPERT), by_expert),
                pl.BlockSpec((None, 1, 2 * D_EXPERT), by_expert),
                pl.BlockSpec((None, D_EXPERT, D_MODEL), by_expert),
                pl.BlockSpec((None, 1, D_MODEL), by_expert),
            ],
            out_specs=pl.BlockSpec(memory_space=pl.ANY),
            scratch_shapes=[
                pltpu.VMEM((D_MODEL, 2 * D_EXPERT), BF16),
                pltpu.VMEM((D_EXPERT, D_MODEL), BF16),
                pltpu.VMEM((2, ROW_TILE, width), jnp.int32),
                pltpu.VMEM((2, ROW_TILE, D_MODEL // 2), jnp.int32),
                pltpu.SemaphoreType.DMA((2,)),
                pltpu.SemaphoreType.DMA((2,)),
            ],
        ),
        out_shape=jax.ShapeDtypeStruct((n_rows, D_MODEL // 2), jnp.int32),
        compiler_params=_cparams(1, VMEM_LIMIT),
        name="experts",
    )(tiles.astype(jnp.int32), counts.astype(jnp.int32), base, next_row, xs, w_gate_up,
      b_gate_up.reshape(N_EXPERTS, 1, 2 * D_EXPERT), w_down, b_down.reshape(N_EXPERTS, 1, D_MODEL))


def _sc_gather_rows(table, idx):
    m = idx.shape[0]
    width = table.shape[1]
    mesh = plsc.VectorSubcoreMesh(core_axis_name="core", subcore_axis_name="subcore")
    n_workers = mesh.num_cores * mesh.num_subcores
    per_worker = m // n_workers
    assert per_worker * n_workers == m and per_worker % SC_WINDOW == 0

    @pl.kernel(
        out_type=jax.ShapeDtypeStruct((m, width), table.dtype),
        mesh=mesh,
        scratch_types=[
            pltpu.VMEM((SC_WINDOW,), jnp.int32),
            pltpu.VMEM((SC_WINDOW, width), table.dtype),
            pltpu.SemaphoreType.DMA,
        ],
    )
    def gather_kernel(table_hbm, idx_hbm, out_hbm, idx_v, rows_v, sem):
        worker = lax.axis_index("subcore") * mesh.num_cores + lax.axis_index("core")

        @pl.loop(0, per_worker // SC_WINDOW)
        def _(j):
            base = pl.multiple_of(worker * per_worker + j * SC_WINDOW, SC_WINDOW)
            pltpu.sync_copy(idx_hbm.at[pl.ds(base, SC_WINDOW)], idx_v)
            pltpu.async_copy(table_hbm.at[idx_v], rows_v, sem).wait()
            pltpu.sync_copy(rows_v, out_hbm.at[pl.ds(base, SC_WINDOW)])

    return gather_kernel(table, idx)


def _sc_scatter_rows_into(table, dst, out_ref, reps):
    m, width = table.shape
    mesh = plsc.VectorSubcoreMesh(core_axis_name="core", subcore_axis_name="subcore")
    n_workers = mesh.num_cores * mesh.num_subcores
    per_worker = m // n_workers
    assert per_worker * n_workers == m and per_worker % SC_WINDOW == 0 and dst.shape == (reps * m,)

    @pl.kernel(
        out_type=(),
        mesh=mesh,
        scratch_types=[
            pltpu.VMEM((SC_WINDOW,), jnp.int32),
            pltpu.VMEM((SC_WINDOW, width), table.dtype),
            pltpu.SemaphoreType.DMA,
        ],
    )
    def scatter_kernel(table_hbm, dst_hbm, out_hbm, dst_v, rows_v, sem):
        worker = lax.axis_index("subcore") * mesh.num_cores + lax.axis_index("core")

        @pl.loop(0, per_worker // SC_WINDOW)
        def _(j):
            base = pl.multiple_of(worker * per_worker + j * SC_WINDOW, SC_WINDOW)
            pltpu.sync_copy(table_hbm.at[pl.ds(base, SC_WINDOW)], rows_v)
            for k in range(reps):
                pltpu.sync_copy(dst_hbm.at[pl.ds(k * m + base, SC_WINDOW)], dst_v)
                pltpu.async_copy(rows_v, out_hbm.at[dst_v], sem).wait()

    scatter_kernel(table, dst, out_ref)


def _combine_dense_kernel(tw_ref, x1_ref, mod_ref, g_ref, y0_ref, y1_ref, y2_ref, y3_ref, o_in_ref,
                          o_ref):
    del o_in_ref
    tm = x1_ref.shape[0]
    tw = jnp.concatenate([tw_ref[...], jnp.zeros((LANES - SUBLANES, tm), F32)], axis=0).T
    first = second = None
    for kk, y_ref in enumerate((y0_ref, y1_ref, y2_ref, y3_ref)):
        f_k, s_k = _unpack_bf16_pairs(y_ref[...])
        w_k = tw[:, kk:kk + 1]
        first = w_k * f_k if kk == 0 else first + w_k * f_k
        second = w_k * s_k if kk == 0 else second + w_k * s_k
    ffn = jnp.concatenate([first, second], axis=1)
    x2 = x1_ref[...] + mod_ref[5:6, :] * ffn
    o_ref[...] = x2 * lax.rsqrt(jnp.mean(x2 * x2, axis=-1, keepdims=True) + EPS) * g_ref[...]


def _combine_dense(b, tw, x1, mod, g_final, yg, out_buf, seq):
    n = x1.shape[0]
    width = yg.shape[1]
    tiles = seq // TOK_TILE
    row = lambda i: (b * tiles + i, 0)
    slot = lambda kk: pl.BlockSpec((TOK_TILE, width), lambda i: (kk * tiles + i, 0))
    return pl.pallas_call(
        _combine_dense_kernel,
        grid=(tiles,),
        in_specs=[
            pl.BlockSpec((SUBLANES, TOK_TILE), lambda i: (0, i)),
            pl.BlockSpec((TOK_TILE, D_MODEL), row),
            pl.BlockSpec((None, SUBLANES, D_MODEL), lambda i: (b, 0, 0)),
            pl.BlockSpec((1, D_MODEL), lambda i: (0, 0)),
        ] + [slot(kk) for kk in range(TOP_K)] + [pl.BlockSpec(memory_space=pl.ANY)],
        out_specs=pl.BlockSpec((TOK_TILE, D_MODEL), row),
        out_shape=jax.ShapeDtypeStruct((n, D_MODEL), F32),
        input_output_aliases={4 + TOP_K: 0},
        compiler_params=_cparams(1, VMEM_LIMIT),
        name="combine",
    )(tw, x1, mod, g_final.reshape(1, D_MODEL), yg, yg, yg, yg, out_buf)


def _rope_base(positions):
    half = ROPE_DIM // 2
    inv_freq = ROPE_THETA ** (-jnp.arange(0, ROPE_DIM, 2, dtype=F32) / ROPE_DIM)
    n = positions.size
    pos = jnp.broadcast_to(positions.reshape(n, 1).astype(F32), (n, half)).reshape(-1, LANES)
    ang = pos * jnp.tile(inv_freq, LANES // half)
    cos, sin = lax.optimization_barrier((jnp.cos(ang), jnp.sin(ang)))
    pad = jnp.zeros((n, LANES - ROPE_DIM), F32)
    return jnp.concatenate([cos.reshape(n, half), sin.reshape(n, half), pad], axis=1)


def kernel(x, c, positions, w_ada, b_ada, g_mix_norm, w_in, rel_bias, lambda_q1, lambda_k1,
           lambda_q2, lambda_k2, g_subln, w_out, g_ffn_norm, w_router, b_router, w_gate_up,
           b_gate_up, w_down, b_down, g_final):
    batch, seq, _ = x.shape
    depth = w_ada.shape[0]
    assert depth == 1, "the combine kernel applies the final norm, so it must follow the only layer"
    n = batch * seq
    rope_base = _rope_base(positions)
    x2 = x.reshape(n, D_MODEL)
    for l in range(depth):
        lambda_init = 0.8 - 0.6 * math.exp(-0.3 * l)
        mod = _ada(c, w_ada[l], b_ada[l])
        qa, ka, va, qb, kb, vb = _in_proj(x2, mod, g_mix_norm[l], w_in[l].astype(BF16),
                                          rope_base, seq)
        oa = _attn_a(qa, ka, va, _rel_bias_rows(rel_bias[l]), batch, seq)
        lam = (jnp.exp(jnp.sum(lambda_q1[l].astype(F32) * lambda_k1[l].astype(F32)))
               - jnp.exp(jnp.sum(lambda_q2[l].astype(F32) * lambda_k2[l].astype(F32)))
               + lambda_init).reshape(1)
        ob = _attn_b(lam, qb, kb, vb, g_subln[l], batch, seq, 1.0 - lambda_init)
        w_out_bf = w_out[l].astype(BF16)
        w_router_bf = w_router[l].astype(BF16)
        xs_ref = jax.new_ref(lax.empty((N_EXPERTS * EXPERT_ROWS, D_MODEL // 2), jnp.int32))
        x1 = lax.empty((n, D_MODEL), F32)
        counts = jnp.zeros((N_EXPERTS, 1), F32)
        pos_rows, weight_rows = [], []
        for b in range(batch):
            x1, hp, top_i, top_w, rank, counts = _out_route(
                b, oa, ob, x2, mod, w_out_bf, g_ffn_norm[l], w_router_bf, b_router[l], counts, x1,
                seq)
            pos = top_i * EXPERT_ROWS + rank
            pos_rows.append(pos)
            weight_rows.append(top_w)
            dst = jnp.concatenate([pos[kk] for kk in range(TOP_K)])
            _sc_scatter_rows_into(hp, dst, xs_ref, TOP_K)
        xs = jax.freeze(xs_ref)
        y = _experts(counts[:, 0].astype(jnp.int32), xs, w_gate_up[l], b_gate_up[l], w_down[l],
                     b_down[l])
        x2 = lax.empty((n, D_MODEL), F32)
        for b in range(batch):
            yg = _sc_gather_rows(y, jnp.concatenate([pos_rows[b][kk] for kk in range(TOP_K)]))
            x2 = _combine_dense(b, weight_rows[b], x1, mod, g_final, yg, x2, seq)
    return x2.reshape(batch, seq, D_MODEL)
```

```python
import functools
import math

import jax
import jax.numpy as jnp
from jax import lax
from jax.experimental import pallas as pl
from jax.experimental.pallas import tpu as pltpu
from jax.experimental.pallas import tpu_sc as plsc

D_MODEL = 1024
CHUNK = 64
HEAD_DIM = 64
A_HEADS = 8
A_WIDTH = A_HEADS * HEAD_DIM
LEFT_CHUNKS = 8
MAX_REL = 128
B_HEADS = 4
B_QK_DIM = HEAD_DIM
B_V_DIM = 2 * HEAD_DIM
B_WIDTH = B_HEADS * B_V_DIM
ROPE_THETA = 500000.0
ROPE_DIM = B_QK_DIM // 4
N_EXPERTS = 32
TOP_K = 4
D_EXPERT = D_MODEL
SWIGLU_LIMIT = 7.0
SWIGLU_ALPHA = 1.702
EPS = 1e-6
NEG_INF = -1e30
LOG2_E = math.log2(math.e)
N_MOD = 6

LANES = 128
SUBLANES = 8
VMEM_BYTES_V7X = 64 * 1024 * 1024
VMEM_LIMIT = VMEM_BYTES_V7X * 7 // 8

TOK_TILE = 512
A_QBLK = 2 * CHUNK
A_BAND = (LEFT_CHUNKS + 2) * CHUNK
A_ROLL = A_BAND + A_QBLK
B_TQ = 512
B_TK = 512
B_SUB = 128
ROW_TILE = 512
ROW_STEP = 128
FF_CHUNK = 512
EXPERT_ROWS = 16384
SC_WINDOW = 128

F32 = jnp.float32
BF16 = jnp.bfloat16


def _cparams(n_axes, vmem=None):
    return pltpu.CompilerParams(
        dimension_semantics=("arbitrary",) * n_axes,
        vmem_limit_bytes=vmem,
    )


def _ada_kernel(c_ref, w_ref, b_ref, o_ref):
    c = c_ref[...]
    act = c * jax.nn.sigmoid(c)
    o_ref[...] = jnp.dot(act, w_ref[...], preferred_element_type=F32,
                         precision=lax.Precision.HIGHEST) + b_ref[...]


def _ada(c, w_ada, b_ada):
    b = c.shape[0]
    rows = -(-b // SUBLANES) * SUBLANES
    c_pad = jnp.pad(c, ((0, rows - b), (0, 0)))
    n_out = w_ada.shape[1]
    out = pl.pallas_call(
        _ada_kernel,
        grid=(n_out // D_MODEL,),
        in_specs=[
            pl.BlockSpec((rows, D_MODEL), lambda j: (0, 0)),
            pl.BlockSpec((D_MODEL, D_MODEL), lambda j: (0, j)),
            pl.BlockSpec((1, D_MODEL), lambda j: (0, j)),
        ],
        out_specs=pl.BlockSpec((rows, D_MODEL), lambda j: (0, j)),
        out_shape=jax.ShapeDtypeStruct((rows, n_out), F32),
        compiler_params=_cparams(1),
        name="ada",
    )(c_pad, w_ada, b_ada.reshape(1, n_out))
    mod = out[:b].reshape(b, N_MOD, D_MODEL)
    return jnp.pad(mod, ((0, 0), (0, SUBLANES - N_MOD), (0, 0)))


def _in_proj_kernel(x_ref, mod_ref, g_ref, w_ref, rb_ref,
                    qa_ref, ka_ref, va_ref, qb_ref, kb_ref, vb_ref):
    x = x_ref[...]
    mod = mod_ref[...]
    y = x * lax.rsqrt(jnp.mean(x * x, axis=-1, keepdims=True) + EPS) * g_ref[...]
    h = (y * (1.0 + mod[1:2, :]) + mod[0:1, :]).astype(BF16)
    q_scale = HEAD_DIM ** -0.5

    half = ROPE_DIM // 2
    rb = rb_ref[...]
    lane = lax.broadcasted_iota(jnp.int32, rb.shape, 1)
    cos_lo = jnp.where(lane < half, rb, 0.0)
    sin_hi = jnp.where(jnp.logical_and(lane >= half, lane < ROPE_DIM), rb, 0.0)
    cos_pair = cos_lo + pltpu.roll(cos_lo, half, 1)
    rc = (cos_pair + pltpu.roll(cos_pair, B_QK_DIM, 1)
          + jnp.where(lane % B_QK_DIM >= ROPE_DIM, 1.0, 0.0))
    sin_lo = pltpu.roll(sin_hi, LANES - half, 1)
    rm = -(sin_lo + pltpu.roll(sin_lo, B_QK_DIM, 1))
    rp = sin_hi + pltpu.roll(sin_hi, B_QK_DIM, 1)

    def rope(p):
        cols = []
        for s in range(p.shape[1] // LANES):
            v = p[:, s * LANES:(s + 1) * LANES]
            cols.append(v * rc + pltpu.roll(v, LANES - ROPE_DIM // 2, 1) * rm
                        + pltpu.roll(v, ROPE_DIM // 2, 1) * rp)
        return jnp.concatenate(cols, axis=1)

    outs = (qa_ref, ka_ref, va_ref, qb_ref, kb_ref, vb_ref)
    for j, o_ref in enumerate(outs):
        p = jnp.dot(h, w_ref[:, j * A_WIDTH:(j + 1) * A_WIDTH], preferred_element_type=F32)
        if j in (3, 4):
            p = rope(p)
        if j in (0, 3):
            p = p * (q_scale * LOG2_E)
        pb = p.astype(BF16)
        for s in range(A_WIDTH // LANES):
            o_ref[s] = pb[:, s * LANES:(s + 1) * LANES]


def _in_proj(x2, mod, g_mix, w_in_bf, rope_base, seq):
    n = x2.shape[0]
    tiles_per_seq = seq // TOK_TILE
    row = lambda i: (i, 0)
    fixed = lambda i: (0, 0)
    n_slabs = A_WIDTH // LANES
    out_sd = jax.ShapeDtypeStruct((n_slabs, n, LANES), BF16)
    return pl.pallas_call(
        _in_proj_kernel,
        grid=(n // TOK_TILE,),
        in_specs=[
            pl.BlockSpec((TOK_TILE, D_MODEL), row),
            pl.BlockSpec((None, SUBLANES, D_MODEL), lambda i: (i // tiles_per_seq, 0, 0)),
            pl.BlockSpec((1, D_MODEL), fixed),
            pl.BlockSpec(w_in_bf.shape, fixed),
            pl.BlockSpec((TOK_TILE, LANES), row),
        ],
        out_specs=[pl.BlockSpec((n_slabs, TOK_TILE, LANES), lambda i: (0, i, 0))] * 6,
        out_shape=[out_sd] * 6,
        compiler_params=_cparams(1, VMEM_LIMIT),
        name="in_proj",
    )(x2, mod, g_mix.reshape(1, D_MODEL), w_in_bf, rope_base)


def _attn_a_kernel(q_ref, kp_ref, kc_ref, vp_ref, vc_ref, bias_ref, o_ref, k_sc, v_sc, bias_sc):
    g = pl.program_id(1)
    n_pairs, blk, _ = q_ref.shape
    lane = lax.broadcasted_iota(jnp.int32, (A_QBLK, LANES), 1)
    col = lax.broadcasted_iota(jnp.int32, (A_QBLK, A_BAND), 1)
    ones = jnp.ones((A_BAND, LANES), BF16)

    @pl.when(jnp.logical_and(pl.program_id(0) == 0, g == 0))
    def _():
        q_chunk = lax.broadcasted_iota(jnp.int32, (A_QBLK, A_BAND), 0) // CHUNK
        k_chunk = col // CHUNK
        in_band = jnp.logical_and(k_chunk >= q_chunk, k_chunk <= q_chunk + LEFT_CHUNKS)
        for h in range(bias_ref.shape[0]):
            rolled = pltpu.roll(jnp.broadcast_to(bias_ref[h], (A_QBLK, A_ROLL)), 0, 1,
                                stride=1, stride_axis=0)
            bias_sc[h] = jnp.where(in_band, rolled[:, :A_BAND], NEG_INF)

    def pair(p):
        k_buf = k_sc.at[p % 2]
        v_buf = v_sc.at[p % 2]
        k_buf[0:blk, :] = kp_ref[p]
        k_buf[blk:2 * blk, :] = kc_ref[p]
        v_buf[0:blk, :] = vp_ref[p]
        v_buf[blk:2 * blk, :] = vc_ref[p]

        def scores(m, hh):
            r0 = m * A_QBLK
            q = q_ref[p, r0:r0 + A_QBLK, :]
            in_head = (lane < HEAD_DIM) if hh == 0 else (lane >= HEAD_DIM)
            qh = jnp.where(in_head, q, jnp.zeros_like(q))
            return lax.dot_general(qh, k_buf[r0:r0 + A_BAND, :], (((1,), (1,)), ((), ())),
                                   preferred_element_type=F32)

        chains = [(m, hh) for m in range(blk // A_QBLK) for hh in range(2)]
        s_next = scores(*chains[0])
        halves = []
        for i, (m, hh) in enumerate(chains):
            r0 = m * A_QBLK
            s = s_next
            if i + 1 < len(chains):
                s_next = scores(*chains[i + 1])
            valid = jnp.logical_or(g > 0, col + r0 >= blk)
            s = jnp.where(valid, s + bias_sc[2 * p + hh], NEG_INF)
            pr = jnp.exp2(s - jnp.max(s, axis=1, keepdims=True))
            v_ext = jnp.concatenate([v_buf[r0:r0 + A_BAND, :], ones], axis=1)
            pv = jnp.dot(pr.astype(BF16), v_ext, preferred_element_type=F32)
            halves.append(pv[:, :LANES] / pv[:, LANES:])
            if hh == 1:
                o_ref[p, r0:r0 + A_QBLK, :] = jnp.where(lane < HEAD_DIM, halves[0],
                                                        halves[1]).astype(BF16)
                halves = []

    for p in range(n_pairs):
        pair(p)


def _attn_a(qa, ka, va, bias_rows, batch, seq):
    n_pairs, n, _ = qa.shape
    blk = LEFT_CHUNKS * CHUNK
    nblk = seq // blk
    cur = lambda b, g: (0, b * nblk + g, 0)
    prev = lambda b, g: (0, b * nblk + jnp.maximum(g - 1, 0), 0)
    slab = (n_pairs, blk, LANES)
    return pl.pallas_call(
        _attn_a_kernel,
        grid=(batch, nblk),
        in_specs=[
            pl.BlockSpec(slab, cur),
            pl.BlockSpec(slab, prev),
            pl.BlockSpec(slab, cur),
            pl.BlockSpec(slab, prev),
            pl.BlockSpec(slab, cur),
            pl.BlockSpec(bias_rows.shape, lambda b, g: (0, 0, 0)),
        ],
        out_specs=pl.BlockSpec(slab, cur),
        out_shape=jax.ShapeDtypeStruct((n_pairs, n, LANES), BF16),
        scratch_shapes=[
            pltpu.VMEM((2, 2 * blk, LANES), BF16),
            pltpu.VMEM((2, 2 * blk, LANES), BF16),
            pltpu.VMEM((bias_rows.shape[0], A_QBLK, A_BAND), F32),
        ],
        compiler_params=_cparams(2),
        name="attn_a",
    )(qa, ka, ka, va, va, bias_rows)


def _rel_bias_rows(rel_table):
    t = rel_table.astype(F32) * LOG2_E
    far = t[:, 2 * MAX_REL:]
    n_far = LEFT_CHUNKS * CHUNK - MAX_REL
    row = jnp.concatenate([
        jnp.broadcast_to(far, (t.shape[0], n_far)),
        t[:, 2 * MAX_REL:0:-1],
        jnp.broadcast_to(far, (t.shape[0], A_ROLL - A_BAND)),
    ], axis=1)
    return row.reshape(t.shape[0], 1, A_ROLL)


def _attn_b_kernel(lam_ref, q_ref, k_ref, v_ref, g_ref, o_ref, q_sc, s_sc, m_sc, acc_sc, *,
                   out_scale):
    qi = pl.program_id(2)
    n_sub = B_TQ // B_SUB
    lane = lax.broadcasted_iota(jnp.int32, (B_TQ, LANES), 1)
    q = q_ref[...]
    q_sc[0] = jnp.where(lane < B_QK_DIM, q, jnp.zeros_like(q))
    q_sc[1] = jnp.where(lane >= B_QK_DIM, q, jnp.zeros_like(q))
    m_sc[...] = jnp.full(m_sc.shape, NEG_INF, F32)
    acc_sc[...] = jnp.zeros(acc_sc.shape, F32)
    ones = jnp.ones((B_TK, LANES), BF16)

    def score_rows(k, slot, sub, c):
        rows = pl.ds(sub * B_SUB, B_SUB)
        s_sc[slot, c, rows, :] = lax.dot_general(q_sc[c, rows, :], k, (((1,), (1,)), ((), ())),
                                                 preferred_element_type=F32)

    def update_rows(s, v_ext, sub, c):
        rows = pl.ds(sub * B_SUB, B_SUB)
        m_prev = m_sc[c, rows, :]
        m_new = jnp.maximum(m_prev, jnp.max(s, axis=1, keepdims=True))
        alpha = jnp.exp2(m_prev - m_new)
        p = jnp.exp2(s - jnp.concatenate([m_new] * (s.shape[1] // LANES), axis=1))
        pv = jnp.dot(p.astype(BF16), v_ext, preferred_element_type=F32)
        acc_sc[c, rows, :] = jnp.concatenate([alpha, alpha], axis=1) * acc_sc[c, rows, :] + pv
        m_sc[c, rows, :] = m_new

    def key_block(blk):
        return k_ref[pl.ds(pl.multiple_of(blk * B_TK, B_TK), B_TK), :]

    def value_block(blk):
        return jnp.concatenate([v_ref[pl.ds(pl.multiple_of(blk * B_TK, B_TK), B_TK), :], ones], axis=1)

    def step(blk, slot, next_blk):
        v_ext = value_block(blk)
        k_next = key_block(next_blk)
        for sub in range(n_sub):
            for c in range(2):
                update_rows(s_sc[slot, c, pl.ds(sub * B_SUB, B_SUB), :], v_ext, sub, c)
                score_rows(k_next, 1 - slot, sub, c)

    def diagonal_step(slot):
        v_ext = value_block(qi)
        for sub in range(n_sub):
            n_keys = (sub + 1) * B_SUB
            col_c = lax.broadcasted_iota(jnp.int32, (B_SUB, n_keys), 1) // CHUNK
            row_c = lax.broadcasted_iota(jnp.int32, (B_SUB, n_keys), 0) // CHUNK
            keep = col_c <= row_c + sub * (B_SUB // CHUNK)
            for c in range(2):
                s = s_sc[slot, c, pl.ds(sub * B_SUB, B_SUB), :n_keys]
                update_rows(jnp.where(keep, s, NEG_INF), v_ext[:n_keys], sub, c)

    k0 = key_block(0)
    for sub in range(n_sub):
        for c in range(2):
            score_rows(k0, 0, sub, c)

    def pair(p, carry):
        first = 2 * p
        step(first, 0, first + 1)
        step(first + 1, 1, first + 2)
        return carry

    lax.fori_loop(0, qi // 2, pair, 0)

    @pl.when(qi % 2 == 0)
    def _():
        diagonal_step(0)

    @pl.when(qi % 2 == 1)
    def _():
        step(qi - 1, 0, qi)
        diagonal_step(1)

    lam = lam_ref[0]
    a0 = acc_sc[0]
    a1 = acc_sc[1]
    o = a0[:, :B_V_DIM] / a0[:, B_V_DIM:] - lam * (a1[:, :B_V_DIM] / a1[:, B_V_DIM:])
    y = o * lax.rsqrt(jnp.mean(o * o, axis=-1, keepdims=True) + EPS) * g_ref[...]
    o_ref[...] = (y * out_scale).astype(BF16)


def _attn_b(lam, qb, kb, vb, g_subln, batch, seq, out_scale):
    assert B_TQ == B_TK and B_V_DIM == LANES
    n = qb.shape[1]
    nq = seq // B_TQ
    q_map = lambda b, h, qi: (h, b * nq + qi, 0)
    kv_map = lambda b, h, qi: (h, b, 0)
    return pl.pallas_call(
        functools.partial(_attn_b_kernel, out_scale=out_scale),
        grid=(batch, B_HEADS, nq),
        in_specs=[
            pl.BlockSpec(memory_space=pltpu.SMEM),
            pl.BlockSpec((None, B_TQ, LANES), q_map),
            pl.BlockSpec((None, seq, LANES), kv_map),
            pl.BlockSpec((None, seq, LANES), kv_map),
            pl.BlockSpec((1, B_V_DIM), lambda b, h, qi: (0, 0)),
        ],
        out_specs=pl.BlockSpec((None, B_TQ, LANES), q_map),
        out_shape=jax.ShapeDtypeStruct((B_HEADS, n, LANES), BF16),
        scratch_shapes=[
            pltpu.VMEM((2, B_TQ, LANES), BF16),
            pltpu.VMEM((2, 2, B_TQ, B_TK), F32),
            pltpu.VMEM((2, B_TQ, LANES), F32),
            pltpu.VMEM((2, B_TQ, 2 * LANES), F32),
        ],
        compiler_params=_cparams(3),
        name="attn_b",
    )(lam, qb, kb, vb, g_subln.reshape(1, B_V_DIM))


def _pack_bf16_pairs(v):
    half = v.shape[1] // 2
    vb = v.astype(BF16)
    hi = lax.bitcast_convert_type(vb[:, :half].astype(F32), jnp.int32)
    lo = lax.bitcast_convert_type(vb[:, half:].astype(F32), jnp.int32)
    return hi | lax.shift_right_logical(lo, jnp.full(lo.shape, 16, jnp.int32))


def _unpack_bf16_pairs(w):
    first = lax.bitcast_convert_type(w & jnp.int32(-65536), F32)
    second = lax.bitcast_convert_type(lax.shift_left(w, jnp.full(w.shape, 16, jnp.int32)), F32)
    return first, second


def _out_route_kernel(oa_ref, ob_ref, x_ref, mod_ref, wo_ref, g_ref, wr_ref, br_ref, cin_ref,
                      x1_in_ref, x1_ref, hp_ref, ti_ref, tw_ref, rk_ref, cnt_ref, tri_sc, carry_sc):
    del x1_in_ref
    i = pl.program_id(0)
    tm = x_ref.shape[0]

    @pl.when(i == 0)
    def _():
        r = lax.broadcasted_iota(jnp.int32, (tm, tm), 0)
        c = lax.broadcasted_iota(jnp.int32, (tm, tm), 1)
        tri_sc[...] = jnp.where(r < c, 1.0, 0.0).astype(BF16)
        carry_sc[...] = cin_ref[...]

    mod = mod_ref[...]
    o = jnp.concatenate([oa_ref[s] for s in range(oa_ref.shape[0])]
                        + [ob_ref[s] for s in range(ob_ref.shape[0])], axis=1)
    mix = jnp.dot(o, wo_ref[...], preferred_element_type=F32)
    x1 = x_ref[...] + mod[2:3, :] * mix
    x1_ref[...] = x1
    y = x1 * lax.rsqrt(jnp.mean(x1 * x1, axis=-1, keepdims=True) + EPS) * g_ref[...]
    h = y * (1.0 + mod[4:5, :]) + mod[3:4, :]
    hb = h.astype(BF16)
    hp_ref[...] = _pack_bf16_pairs(h)

    logits = lax.dot_general(wr_ref[...], hb, (((1,), (1,)), ((), ())),
                             preferred_element_type=F32) + br_ref[...]
    eid = lax.broadcasted_iota(jnp.int32, logits.shape, 0).astype(F32)
    work = logits
    vals, ids = [], []
    chosen = jnp.zeros(logits.shape, F32)
    for _ in range(TOP_K):
        v = jnp.max(work, axis=0, keepdims=True)
        e = jnp.min(jnp.where(work == v, eid, float(N_EXPERTS)), axis=0, keepdims=True)
        hit = eid == e
        vals.append(v)
        ids.append(e)
        chosen = jnp.where(hit, 1.0, chosen)
        work = jnp.where(hit, -jnp.inf, work)
    ex = [jnp.exp(v - vals[0]) for v in vals]
    den = ex[0] + ex[1] + ex[2] + ex[3]

    before = jnp.dot(chosen.astype(BF16), tri_sc[...], preferred_element_type=F32) + carry_sc[...]
    slot = lax.broadcasted_iota(jnp.int32, (SUBLANES, tm), 0)
    ti = jnp.zeros((SUBLANES, tm), F32)
    tw = jnp.zeros((SUBLANES, tm), F32)
    rk = jnp.zeros((SUBLANES, tm), F32)
    for kk in range(TOP_K):
        r_k = jnp.sum(jnp.where(eid == ids[kk], before, 0.0), axis=0, keepdims=True)
        ti = jnp.where(slot == kk, ids[kk], ti)
        tw = jnp.where(slot == kk, ex[kk] / den, tw)
        rk = jnp.where(slot == kk, r_k, rk)
    ti_ref[...] = ti.astype(jnp.int32)
    tw_ref[...] = tw
    rk_ref[...] = rk.astype(jnp.int32)
    carry = carry_sc[...] + jnp.sum(chosen, axis=1, keepdims=True)
    carry_sc[...] = carry
    cnt_ref[...] = carry


def _out_route(b, oa, ob, x2, mod, w_out_bf, g_ffn, w_router_bf, b_router, counts_in, x1_buf, seq):
    n = x2.shape[0]
    tiles = seq // TOK_TILE
    row = lambda i: (b * tiles + i, 0)
    local = lambda i: (i, 0)
    by_lane = lambda i: (0, i)
    fixed = lambda i: (0, 0)
    return pl.pallas_call(
        _out_route_kernel,
        grid=(tiles,),
        in_specs=[
            pl.BlockSpec((oa.shape[0], TOK_TILE, LANES), lambda i: (0, b * tiles + i, 0)),
            pl.BlockSpec((ob.shape[0], TOK_TILE, LANES), lambda i: (0, b * tiles + i, 0)),
            pl.BlockSpec((TOK_TILE, D_MODEL), row),
            pl.BlockSpec((None, SUBLANES, D_MODEL), lambda i: (b, 0, 0)),
            pl.BlockSpec((D_MODEL, D_MODEL), fixed),
            pl.BlockSpec((1, D_MODEL), fixed),
            pl.BlockSpec((N_EXPERTS, D_MODEL), fixed),
            pl.BlockSpec((N_EXPERTS, 1), fixed),
            pl.BlockSpec((N_EXPERTS, 1), fixed),
            pl.BlockSpec(memory_space=pl.ANY),
        ],
        out_specs=[
            pl.BlockSpec((TOK_TILE, D_MODEL), row),
            pl.BlockSpec((TOK_TILE, D_MODEL // 2), local),
            pl.BlockSpec((SUBLANES, TOK_TILE), by_lane),
            pl.BlockSpec((SUBLANES, TOK_TILE), by_lane),
            pl.BlockSpec((SUBLANES, TOK_TILE), by_lane),
            pl.BlockSpec((N_EXPERTS, 1), fixed),
        ],
        out_shape=[
            jax.ShapeDtypeStruct((n, D_MODEL), F32),
            jax.ShapeDtypeStruct((seq, D_MODEL // 2), jnp.int32),
            jax.ShapeDtypeStruct((SUBLANES, seq), jnp.int32),
            jax.ShapeDtypeStruct((SUBLANES, seq), F32),
            jax.ShapeDtypeStruct((SUBLANES, seq), jnp.int32),
            jax.ShapeDtypeStruct((N_EXPERTS, 1), F32),
        ],
        input_output_aliases={9: 0},
        scratch_shapes=[pltpu.VMEM((TOK_TILE, TOK_TILE), BF16), pltpu.VMEM((N_EXPERTS, 1), F32)],
        compiler_params=_cparams(1, VMEM_LIMIT),
        name="out_route",
    )(oa, ob, x2, mod, w_out_bf, g_ffn.reshape(1, D_MODEL), w_router_bf.T,
      b_router.reshape(N_EXPERTS, 1), counts_in, x1_buf)


def _experts_kernel(tiles_ref, rows_ref, base_ref, next_ref, xs_ref, wgu_ref, bgu_ref, wd_ref,
                    bd_ref, y_ref, wgu_sc, wd_sc, x_buf, y_buf, x_sem, y_sem):
    e = pl.program_id(0)
    n_tiles = tiles_ref[e]
    n_rows = rows_ref[e]
    n_live = base_ref[N_EXPERTS]
    region = e * EXPERT_ROWS

    def x_copy(row, slot):
        return pltpu.make_async_copy(xs_ref.at[pl.ds(row, ROW_TILE), :], x_buf.at[slot],
                                     x_sem.at[slot])

    def y_copy(row, slot):
        return pltpu.make_async_copy(y_buf.at[slot], y_ref.at[pl.ds(row, ROW_TILE), :],
                                     y_sem.at[slot])

    @pl.when(e == 0)
    def _():
        y_buf[...] = jnp.zeros(y_buf.shape, jnp.int32)

        @pl.when(n_live > 0)
        def _():
            x_copy(pl.multiple_of(next_ref[N_EXPERTS], ROW_TILE), 0).start()

    @pl.when(n_tiles > 0)
    def _():
        wgu_sc[...] = wgu_ref[...].astype(BF16)
        wd_sc[...] = wd_ref[...].astype(BF16)

    def tile(j, carry):
        g = base_ref[e] + j
        slot = g % 2
        row = pl.multiple_of(region + j * ROW_TILE, ROW_TILE)
        x_copy(row, slot).wait()
        next_row = jnp.where(j + 1 < n_tiles, row + ROW_TILE, next_ref[e])

        @pl.when(g + 1 < n_live)
        def _():
            x_copy(pl.multiple_of(next_row, ROW_TILE), 1 - slot).start()

        @pl.when(g >= 2)
        def _():
            y_copy(0, slot).wait()

        needed = jnp.minimum(n_rows - j * ROW_TILE, ROW_TILE)
        for rows in range(ROW_STEP, ROW_TILE + 1, ROW_STEP):
            @pl.when(jnp.logical_and(needed > rows - ROW_STEP, needed <= rows))
            def _(rows=rows):
                used = lax.broadcasted_iota(jnp.int32, (rows, 1), 0) < needed
                first, second = _unpack_bf16_pairs(jnp.where(used, x_buf[slot, :rows, :], 0))
                x = jnp.concatenate([first.astype(BF16), second.astype(BF16)], axis=1)
                acc = jnp.zeros((rows, D_MODEL), F32)
                for c in range(D_EXPERT // FF_CHUNK):
                    lo_c, hi_c = c * FF_CHUNK, (c + 1) * FF_CHUNK
                    gate = (jnp.dot(x, wgu_sc[:, lo_c:hi_c], preferred_element_type=F32)
                            + bgu_ref[:, lo_c:hi_c])
                    up = (jnp.dot(x, wgu_sc[:, D_EXPERT + lo_c:D_EXPERT + hi_c],
                                  preferred_element_type=F32)
                          + bgu_ref[:, D_EXPERT + lo_c:D_EXPERT + hi_c])
                    gate = jnp.minimum(gate, SWIGLU_LIMIT)
                    up = jnp.clip(up, -SWIGLU_LIMIT, SWIGLU_LIMIT)
                    act = (up + 1.0) * (gate * jax.nn.sigmoid(SWIGLU_ALPHA * gate))
                    acc = acc + jnp.dot(act.astype(BF16), wd_sc[lo_c:hi_c, :],
                                        preferred_element_type=F32)
                y_buf[slot, :rows, :] = _pack_bf16_pairs(acc + bd_ref[...])

        y_copy(row, slot).start()
        return carry

    lax.fori_loop(0, n_tiles, tile, 0)

    @pl.when(e == pl.num_programs(0) - 1)
    def _():
        for back in (2, 1):
            @pl.when(n_live >= back)
            def _():
                y_copy(0, (n_live - back) % 2).wait()


def _experts(counts, xs, w_gate_up, b_gate_up, w_down, b_down):
    n_rows, width = xs.shape
    tiles = (counts + ROW_TILE - 1) // ROW_TILE
    base = jnp.concatenate([jnp.zeros((1,), jnp.int32), jnp.cumsum(tiles)]).astype(jnp.int32)
    region = jnp.arange(N_EXPERTS, dtype=jnp.int32) * EXPERT_ROWS
    later = jnp.arange(N_EXPERTS)[None, :] > jnp.arange(-1, N_EXPERTS)[:, None]
    cand = jnp.where(jnp.logical_and(later, (tiles > 0)[None, :]), region[None, :], n_rows)
    nxt = jnp.min(cand, axis=1)
    next_row = jnp.concatenate([nxt[1:], nxt[:1]]).astype(jnp.int32)
    by_expert = lambda e, *_: (e, 0, 0)
    return pl.pallas_call(
        _experts_kernel,
        grid_spec=pltpu.PrefetchScalarGridSpec(
            num_scalar_prefetch=4,
            grid=(N_EXPERTS,),
            in_specs=[
                pl.BlockSpec(memory_space=pl.ANY),
                pl.BlockSpec((None, D_MODEL, 2 * D_EXPERT), by_expert),
                pl.BlockSpec((None, 1, 2 * D_EXPERT), by_expert),
                pl.BlockSpec((None, D_EXPERT, D_MODEL), by_expert),
                pl.BlockSpec((None, 1, D_MODEL), by_expert),
            ],
            out_specs=pl.BlockSpec(memory_space=pl.ANY),
            scratch_shapes=[
                pltpu.VMEM((D_MODEL, 2 * D_EXPERT), BF16),
                pltpu.VMEM((D_EXPERT, D_MODEL), BF16),
                pltpu.VMEM((2, ROW_TILE, width), jnp.int32),
                pltpu.VMEM((2, ROW_TILE, D_MODEL // 2), jnp.int32),
                pltpu.SemaphoreType.DMA((2,)),
                pltpu.SemaphoreType.DMA((2,)),
            ],
        ),
        out_shape=jax.ShapeDtypeStruct((n_rows, D_MODEL // 2), jnp.int32),
        compiler_params=_cparams(1, VMEM_LIMIT),
        name="experts",
    )(tiles.astype(jnp.int32), counts.astype(jnp.int32), base, next_row, xs, w_gate_up,
      b_gate_up.reshape(N_EXPERTS, 1, 2 * D_EXPERT), w_down, b_down.reshape(N_EXPERTS, 1, D_MODEL))


def _sc_gather_rows(table, idx):
    m = idx.shape[0]
    width = table.shape[1]
    mesh = plsc.VectorSubcoreMesh(core_axis_name="core", subcore_axis_name="subcore")
    n_workers = mesh.num_cores * mesh.num_subcores
    per_worker = m // n_workers
    assert per_worker * n_workers == m and per_worker % SC_WINDOW == 0

    @pl.kernel(
        out_type=jax.ShapeDtypeStruct((m, width), table.dtype),
        mesh=mesh,
        scratch_types=[
            pltpu.VMEM((SC_WINDOW,), jnp.int32),
            pltpu.VMEM((SC_WINDOW, width), table.dtype),
            pltpu.SemaphoreType.DMA,
        ],
    )
    def gather_kernel(table_hbm, idx_hbm, out_hbm, idx_v, rows_v, sem):
        worker = lax.axis_index("subcore") * mesh.num_cores + lax.axis_index("core")

        @pl.loop(0, per_worker // SC_WINDOW)
        def _(j):
            base = pl.multiple_of(worker * per_worker + j * SC_WINDOW, SC_WINDOW)
            pltpu.sync_copy(idx_hbm.at[pl.ds(base, SC_WINDOW)], idx_v)
            pltpu.async_copy(table_hbm.at[idx_v], rows_v, sem).wait()
            pltpu.sync_copy(rows_v, out_hbm.at[pl.ds(base, SC_WINDOW)])

    return gather_kernel(table, idx)


def _sc_scatter_rows_into(table, dst, out_ref, reps):
    m, width = table.shape
    mesh = plsc.VectorSubcoreMesh(core_axis_name="core", subcore_axis_name="subcore")
    n_workers = mesh.num_cores * mesh.num_subcores
    per_worker = m // n_workers
    assert per_worker * n_workers == m and per_worker % SC_WINDOW == 0 and dst.shape == (reps * m,)

    @pl.kernel(
        out_type=(),
        mesh=mesh,
        scratch_types=[
            pltpu.VMEM((SC_WINDOW,), jnp.int32),
            pltpu.VMEM((SC_WINDOW, width), table.dtype),
            pltpu.SemaphoreType.DMA,
        ],
    )
    def scatter_kernel(table_hbm, dst_hbm, out_hbm, dst_v, rows_v, sem):
        worker = lax.axis_index("subcore") * mesh.num_cores + lax.axis_index("core")

        @pl.loop(0, per_worker // SC_WINDOW)
        def _(j):
            base = pl.multiple_of(worker * per_worker + j * SC_WINDOW, SC_WINDOW)
            pltpu.sync_copy(table_hbm.at[pl.ds(base, SC_WINDOW)], rows_v)
            for k in range(reps):
                pltpu.sync_copy(dst_hbm.at[pl.ds(k * m + base, SC_WINDOW)], dst_v)
                pltpu.async_copy(rows_v, out_hbm.at[dst_v], sem).wait()

    scatter_kernel(table, dst, out_ref)


def _combine_dense_kernel(tw_ref, x1_ref, mod_ref, g_ref, y0_ref, y1_ref, y2_ref, y3_ref, o_in_ref,
                          o_ref):
    del o_in_ref
    tm = x1_ref.shape[0]
    tw = jnp.concatenate([tw_ref[...], jnp.zeros((LANES - SUBLANES, tm), F32)], axis=0).T
    first = second = None
    for kk, y_ref in enumerate((y0_ref, y1_ref, y2_ref, y3_ref)):
        f_k, s_k = _unpack_bf16_pairs(y_ref[...])
        w_k = tw[:, kk:kk + 1]
        first = w_k * f_k if kk == 0 else first + w_k * f_k
        second = w_k * s_k if kk == 0 else second + w_k * s_k
    ffn = jnp.concatenate([first, second], axis=1)
    x2 = x1_ref[...] + mod_ref[5:6, :] * ffn
    o_ref[...] = x2 * lax.rsqrt(jnp.mean(x2 * x2, axis=-1, keepdims=True) + EPS) * g_ref[...]


def _combine_dense(b, tw, x1, mod, g_final, yg, out_buf, seq):
    n = x1.shape[0]
    width = yg.shape[1]
    tiles = seq // TOK_TILE
    row = lambda i: (b * tiles + i, 0)
    slot = lambda kk: pl.BlockSpec((TOK_TILE, width), lambda i: (kk * tiles + i, 0))
    return pl.pallas_call(
        _combine_dense_kernel,
        grid=(tiles,),
        in_specs=[
            pl.BlockSpec((SUBLANES, TOK_TILE), lambda i: (0, i)),
            pl.BlockSpec((TOK_TILE, D_MODEL), row),
            pl.BlockSpec((None, SUBLANES, D_MODEL), lambda i: (b, 0, 0)),
            pl.BlockSpec((1, D_MODEL), lambda i: (0, 0)),
        ] + [slot(kk) for kk in range(TOP_K)] + [pl.BlockSpec(memory_space=pl.ANY)],
        out_specs=pl.BlockSpec((TOK_TILE, D_MODEL), row),
        out_shape=jax.ShapeDtypeStruct((n, D_MODEL), F32),
        input_output_aliases={4 + TOP_K: 0},
        compiler_params=_cparams(1, VMEM_LIMIT),
        name="combine",
    )(tw, x1, mod, g_final.reshape(1, D_MODEL), yg, yg, yg, yg, out_buf)


def _rope_base(positions):
    half = ROPE_DIM // 2
    inv_freq = ROPE_THETA ** (-jnp.arange(0, ROPE_DIM, 2, dtype=F32) / ROPE_DIM)
    n = positions.size
    pos = jnp.broadcast_to(positions.reshape(n, 1).astype(F32), (n, half)).reshape(-1, LANES)
    ang = pos * jnp.tile(inv_freq, LANES // half)
    cos, sin = lax.optimization_barrier((jnp.cos(ang), jnp.sin(ang)))
    pad = jnp.zeros((n, LANES - ROPE_DIM), F32)
    return jnp.concatenate([cos.reshape(n, half), sin.reshape(n, half), pad], axis=1)


def kernel(x, c, positions, w_ada, b_ada, g_mix_norm, w_in, rel_bias, lambda_q1, lambda_k1,
           lambda_q2, lambda_k2, g_subln, w_out, g_ffn_norm, w_router, b_router, w_gate_up,
           b_gate_up, w_down, b_down, g_final):
    batch, seq, _ = x.shape
    depth = w_ada.shape[0]
    assert depth == 1, "the combine kernel applies the final norm, so it must follow the only layer"
    n = batch * seq
    rope_base = _rope_base(positions)
    x2 = x.reshape(n, D_MODEL)
    for l in range(depth):
        lambda_init = 0.8 - 0.6 * math.exp(-0.3 * l)
        mod = _ada(c, w_ada[l], b_ada[l])
        qa, ka, va, qb, kb, vb = _in_proj(x2, mod, g_mix_norm[l], w_in[l].astype(BF16),
                                          rope_base, seq)
        oa = _attn_a(qa, ka, va, _rel_bias_rows(rel_bias[l]), batch, seq)
        lam = (jnp.exp(jnp.sum(lambda_q1[l].astype(F32) * lambda_k1[l].astype(F32)))
               - jnp.exp(jnp.sum(lambda_q2[l].astype(F32) * lambda_k2[l].astype(F32)))
               + lambda_init).reshape(1)
        ob = _attn_b(lam, qb, kb, vb, g_subln[l], batch, seq, 1.0 - lambda_init)
        w_out_bf = w_out[l].astype(BF16)
        w_router_bf = w_router[l].astype(BF16)
        xs_ref = jax.new_ref(lax.empty((N_EXPERTS * EXPERT_ROWS, D_MODEL // 2), jnp.int32))
        x1 = lax.empty((n, D_MODEL), F32)
        counts = jnp.zeros((N_EXPERTS, 1), F32)
        pos_rows, weight_rows = [], []
        for b in range(batch):
            x1, hp, top_i, top_w, rank, counts = _out_route(
                b, oa, ob, x2, mod, w_out_bf, g_ffn_norm[l], w_router_bf, b_router[l], counts, x1,
                seq)
            pos = top_i * EXPERT_ROWS + rank
            pos_rows.append(pos)
            weight_rows.append(top_w)
            dst = jnp.concatenate([pos[kk] for kk in range(TOP_K)])
            _sc_scatter_rows_into(hp, dst, xs_ref, TOP_K)
        xs = jax.freeze(xs_ref)
        y = _experts(counts[:, 0].astype(jnp.int32), xs, w_gate_up[l], b_gate_up[l], w_down[l],
                     b_down[l])
        x2 = lax.empty((n, D_MODEL), F32)
        for b in range(batch):
            yg = _sc_gather_rows(y, jnp.concatenate([pos_rows[b][kk] for kk in range(TOP_K)]))
            x2 = _combine_dense(b, weight_rows[b], x1, mod, g_final, yg, x2, seq)
    return x2.reshape(batch, seq, D_MODEL)
```

```python
import functools
import math

import jax
import jax.numpy as jnp
from jax import lax
from jax.experimental import pallas as pl
from jax.experimental.pallas import tpu as pltpu
from jax.experimental.pallas import tpu_sc as plsc

D_MODEL = 1024
CHUNK = 64
HEAD_DIM = 64
A_HEADS = 8
A_WIDTH = A_HEADS * HEAD_DIM
LEFT_CHUNKS = 8
MAX_REL = 128
B_HEADS = 4
B_QK_DIM = HEAD_DIM
B_V_DIM = 2 * HEAD_DIM
B_WIDTH = B_HEADS * B_V_DIM
ROPE_THETA = 500000.0
ROPE_DIM = B_QK_DIM // 4
N_EXPERTS = 32
TOP_K = 4
D_EXPERT = D_MODEL
SWIGLU_LIMIT = 7.0
SWIGLU_ALPHA = 1.702
EPS = 1e-6
NEG_INF = -1e30
LOG2_E = math.log2(math.e)
N_MOD = 6

LANES = 128
SUBLANES = 8
VMEM_BYTES_V7X = 64 * 1024 * 1024
VMEM_LIMIT = VMEM_BYTES_V7X * 7 // 8

TOK_TILE = 512
A_QBLK = 2 * CHUNK
A_BAND = (LEFT_CHUNKS + 2) * CHUNK
A_ROLL = A_BAND + A_QBLK
B_TQ = 512
B_TK = 512
B_SUB = 128
ROW_TILE = 512
ROW_STEP = 128
FF_CHUNK = 512
EXPERT_ROWS = 16384
SC_WINDOW = 128

F32 = jnp.float32
BF16 = jnp.bfloat16


def _cparams(n_axes, vmem=None):
    return pltpu.CompilerParams(
        dimension_semantics=("arbitrary",) * n_axes,
        vmem_limit_bytes=vmem,
    )


def _ada_kernel(c_ref, w_ref, b_ref, o_ref):
    c = c_ref[...]
    act = c * jax.nn.sigmoid(c)
    o_ref[...] = jnp.dot(act, w_ref[...], preferred_element_type=F32,
                         precision=lax.Precision.HIGHEST) + b_ref[...]


def _ada(c, w_ada, b_ada):
    b = c.shape[0]
    rows = -(-b // SUBLANES) * SUBLANES
    c_pad = jnp.pad(c, ((0, rows - b), (0, 0)))
    n_out = w_ada.shape[1]
    out = pl.pallas_call(
        _ada_kernel,
        grid=(n_out // D_MODEL,),
        in_specs=[
            pl.BlockSpec((rows, D_MODEL), lambda j: (0, 0)),
            pl.BlockSpec((D_MODEL, D_MODEL), lambda j: (0, j)),
            pl.BlockSpec((1, D_MODEL), lambda j: (0, j)),
        ],
        out_specs=pl.BlockSpec((rows, D_MODEL), lambda j: (0, j)),
        out_shape=jax.ShapeDtypeStruct((rows, n_out), F32),
        compiler_params=_cparams(1),
        name="ada",
    )(c_pad, w_ada, b_ada.reshape(1, n_out))
    mod = out[:b].reshape(b, N_MOD, D_MODEL)
    return jnp.pad(mod, ((0, 0), (0, SUBLANES - N_MOD), (0, 0)))


def _in_proj_kernel(x_ref, mod_ref, g_ref, w_ref, rb_ref,
                    qa_ref, ka_ref, va_ref, qb_ref, kb_ref, vb_ref):
    x = x_ref[...]
    mod = mod_ref[...]
    y = x * lax.rsqrt(jnp.mean(x * x, axis=-1, keepdims=True) + EPS) * g_ref[...]
    h = (y * (1.0 + mod[1:2, :]) + mod[0:1, :]).astype(BF16)
    q_scale = HEAD_DIM ** -0.5

    half = ROPE_DIM // 2
    rb = rb_ref[...]
    lane = lax.broadcasted_iota(jnp.int32, rb.shape, 1)
    cos_lo = jnp.where(lane < half, rb, 0.0)
    sin_hi = jnp.where(jnp.logical_and(lane >= half, lane < ROPE_DIM), rb, 0.0)
    cos_pair = cos_lo + pltpu.roll(cos_lo, half, 1)
    rc = (cos_pair + pltpu.roll(cos_pair, B_QK_DIM, 1)
          + jnp.where(lane % B_QK_DIM >= ROPE_DIM, 1.0, 0.0))
    sin_lo = pltpu.roll(sin_hi, LANES - half, 1)
    rm = -(sin_lo + pltpu.roll(sin_lo, B_QK_DIM, 1))
    rp = sin_hi + pltpu.roll(sin_hi, B_QK_DIM, 1)

    def rope(p):
        cols = []
        for s in range(p.shape[1] // LANES):
            v = p[:, s * LANES:(s + 1) * LANES]
            cols.append(v * rc + pltpu.roll(v, LANES - ROPE_DIM // 2, 1) * rm
                        + pltpu.roll(v, ROPE_DIM // 2, 1) * rp)
        return jnp.concatenate(cols, axis=1)

    outs = (qa_ref, ka_ref, va_ref, qb_ref, kb_ref, vb_ref)
    for j, o_ref in enumerate(outs):
        p = jnp.dot(h, w_ref[:, j * A_WIDTH:(j + 1) * A_WIDTH], preferred_element_type=F32)
        if j in (3, 4):
            p = rope(p)
        if j in (0, 3):
            p = p * (q_scale * LOG2_E)
        pb = p.astype(BF16)
        for s in range(A_WIDTH // LANES):
            o_ref[s] = pb[:, s * LANES:(s + 1) * LANES]


def _in_proj(x2, mod, g_mix, w_in_bf, rope_base, seq):
    n = x2.shape[0]
    tiles_per_seq = seq // TOK_TILE
    row = lambda i: (i, 0)
    fixed = lambda i: (0, 0)
    n_slabs = A_WIDTH // LANES
    out_sd = jax.ShapeDtypeStruct((n_slabs, n, LANES), BF16)
    return pl.pallas_call(
        _in_proj_kernel,
        grid=(n // TOK_TILE,),
        in_specs=[
            pl.BlockSpec((TOK_TILE, D_MODEL), row),
            pl.BlockSpec((None, SUBLANES, D_MODEL), lambda i: (i // tiles_per_seq, 0, 0)),
            pl.BlockSpec((1, D_MODEL), fixed),
            pl.BlockSpec(w_in_bf.shape, fixed),
            pl.BlockSpec((TOK_TILE, LANES), row),
        ],
        out_specs=[pl.BlockSpec((n_slabs, TOK_TILE, LANES), lambda i: (0, i, 0))] * 6,
        out_shape=[out_sd] * 6,
        compiler_params=_cparams(1, VMEM_LIMIT),
        name="in_proj",
    )(x2, mod, g_mix.reshape(1, D_MODEL), w_in_bf, rope_base)


def _attn_a_kernel(q_ref, kp_ref, kc_ref, vp_ref, vc_ref, bias_ref, o_ref, k_sc, v_sc, bias_sc):
    g = pl.program_id(1)
    n_pairs, blk, _ = q_ref.shape
    lane = lax.broadcasted_iota(jnp.int32, (A_QBLK, LANES), 1)
    col = lax.broadcasted_iota(jnp.int32, (A_QBLK, A_BAND), 1)
    ones = jnp.ones((A_BAND, LANES), BF16)

    @pl.when(jnp.logical_and(pl.program_id(0) == 0, g == 0))
    def _():
        q_chunk = lax.broadcasted_iota(jnp.int32, (A_QBLK, A_BAND), 0) // CHUNK
        k_chunk = col // CHUNK
        in_band = jnp.logical_and(k_chunk >= q_chunk, k_chunk <= q_chunk + LEFT_CHUNKS)
        for h in range(bias_ref.shape[0]):
            rolled = pltpu.roll(jnp.broadcast_to(bias_ref[h], (A_QBLK, A_ROLL)), 0, 1,
                                stride=1, stride_axis=0)
            bias_sc[h] = jnp.where(in_band, rolled[:, :A_BAND], NEG_INF)

    def pair(p):
        k_buf = k_sc.at[p % 2]
        v_buf = v_sc.at[p % 2]
        k_buf[0:blk, :] = kp_ref[p]
        k_buf[blk:2 * blk, :] = kc_ref[p]
        v_buf[0:blk, :] = vp_ref[p]
        v_buf[blk:2 * blk, :] = vc_ref[p]

        def scores(m, hh):
            r0 = m * A_QBLK
            q = q_ref[p, r0:r0 + A_QBLK, :]
            in_head = (lane < HEAD_DIM) if hh == 0 else (lane >= HEAD_DIM)
            qh = jnp.where(in_head, q, jnp.zeros_like(q))
            return lax.dot_general(qh, k_buf[r0:r0 + A_BAND, :], (((1,), (1,)), ((), ())),
                                   preferred_element_type=F32)

        chains = [(m, hh) for m in range(blk // A_QBLK) for hh in range(2)]
        s_next = scores(*chains[0])
        halves = []
        for i, (m, hh) in enumerate(chains):
            r0 = m * A_QBLK
            s = s_next
            if i + 1 < len(chains):
                s_next = scores(*chains[i + 1])
            valid = jnp.logical_or(g > 0, col + r0 >= blk)
            s = jnp.where(valid, s + bias_sc[2 * p + hh], NEG_INF)
            pr = jnp.exp2(s - jnp.max(s, axis=1, keepdims=True))
            v_ext = jnp.concatenate([v_buf[r0:r0 + A_BAND, :], ones], axis=1)
            pv = jnp.dot(pr.astype(BF16), v_ext, preferred_element_type=F32)
            halves.append(pv[:, :LANES] / pv[:, LANES:])
            if hh == 1:
                o_ref[p, r0:r0 + A_QBLK, :] = jnp.where(lane < HEAD_DIM, halves[0],
                                                        halves[1]).astype(BF16)
                halves = []

    for p in range(n_pairs):
        pair(p)


def _attn_a(qa, ka, va, bias_rows, batch, seq):
    n_pairs, n, _ = qa.shape
    blk = LEFT_CHUNKS * CHUNK
    nblk = seq // blk
    cur = lambda b, g: (0, b * nblk + g, 0)
    prev = lambda b, g: (0, b * nblk + jnp.maximum(g - 1, 0), 0)
    slab = (n_pairs, blk, LANES)
    return pl.pallas_call(
        _attn_a_kernel,
        grid=(batch, nblk),
        in_specs=[
            pl.BlockSpec(slab, cur),
            pl.BlockSpec(slab, prev),
            pl.BlockSpec(slab, cur),
            pl.BlockSpec(slab, prev),
            pl.BlockSpec(slab, cur),
            pl.BlockSpec(bias_rows.shape, lambda b, g: (0, 0, 0)),
        ],
        out_specs=pl.BlockSpec(slab, cur),
        out_shape=jax.ShapeDtypeStruct((n_pairs, n, LANES), BF16),
        scratch_shapes=[
            pltpu.VMEM((2, 2 * blk, LANES), BF16),
            pltpu.VMEM((2, 2 * blk, LANES), BF16),
            pltpu.VMEM((bias_rows.shape[0], A_QBLK, A_BAND), F32),
        ],
        compiler_params=_cparams(2),
        name="attn_a",
    )(qa, ka, ka, va, va, bias_rows)


def _rel_bias_rows(rel_table):
    t = rel_table.astype(F32) * LOG2_E
    far = t[:, 2 * MAX_REL:]
    n_far = LEFT_CHUNKS * CHUNK - MAX_REL
    row = jnp.concatenate([
        jnp.broadcast_to(far, (t.shape[0], n_far)),
        t[:, 2 * MAX_REL:0:-1],
        jnp.broadcast_to(far, (t.shape[0], A_ROLL - A_BAND)),
    ], axis=1)
    return row.reshape(t.shape[0], 1, A_ROLL)


def _attn_b_kernel(lam_ref, q_ref, k_ref, v_ref, g_ref, o_ref, q_sc, qn_sc, s_sc, m_sc, acc_sc, *,
                   out_scale):
    n_tiles = q_ref.shape[0] // B_TQ
    n_sub = B_TQ // B_SUB
    first_slot = 2
    lane = lax.broadcasted_iota(jnp.int32, (B_TQ, LANES), 1)
    ones = jnp.ones((B_TK, LANES), BF16)
    lam = lam_ref[0]

    def split_components(t, dst):
        q = q_ref[pl.ds(pl.multiple_of(t * B_TQ, B_TQ), B_TQ), :]
        dst[0] = jnp.where(lane < B_QK_DIM, q, jnp.zeros_like(q))
        dst[1] = jnp.where(lane >= B_QK_DIM, q, jnp.zeros_like(q))

    def score_rows(src, k, slot, sub, c):
        rows = pl.ds(sub * B_SUB, B_SUB)
        s_sc[slot, c, rows, :] = lax.dot_general(src[c, rows, :], k, (((1,), (1,)), ((), ())),
                                                 preferred_element_type=F32)

    def update_rows(s, v_ext, sub, c):
        rows = pl.ds(sub * B_SUB, B_SUB)
        m_prev = m_sc[c, rows, :]
        m_new = jnp.maximum(m_prev, jnp.max(s, axis=1, keepdims=True))
        alpha = jnp.exp2(m_prev - m_new)
        p = jnp.exp2(s - jnp.concatenate([m_new] * (s.shape[1] // LANES), axis=1))
        pv = jnp.dot(p.astype(BF16), v_ext, preferred_element_type=F32)
        acc_sc[c, rows, :] = jnp.concatenate([alpha, alpha], axis=1) * acc_sc[c, rows, :] + pv
        m_sc[c, rows, :] = m_new

    def key_block(blk):
        return k_ref[pl.ds(pl.multiple_of(blk * B_TK, B_TK), B_TK), :]

    def value_block(blk):
        return jnp.concatenate([v_ref[pl.ds(pl.multiple_of(blk * B_TK, B_TK), B_TK), :], ones], axis=1)

    def step(blk, slot, next_blk, next_slot):
        v_ext = value_block(blk)
        k_next = key_block(next_blk)
        for sub in range(n_sub):
            for c in range(2):
                update_rows(s_sc[slot, c, pl.ds(sub * B_SUB, B_SUB), :], v_ext, sub, c)
                score_rows(q_sc, k_next, next_slot, sub, c)

    def last_step(t, slot):
        v_ext = value_block(t)
        k0 = key_block(0)
        col_c = lax.broadcasted_iota(jnp.int32, (B_SUB, B_TK), 1) // CHUNK
        row_c = lax.broadcasted_iota(jnp.int32, (B_SUB, B_TK), 0) // CHUNK
        for sub in range(n_sub):
            keep = col_c <= row_c + sub * (B_SUB // CHUNK)
            for c in range(2):
                s = s_sc[slot, c, pl.ds(sub * B_SUB, B_SUB), :]
                update_rows(jnp.where(keep, s, NEG_INF), v_ext, sub, c)
                score_rows(qn_sc, k0, first_slot, sub, c)

    def start_tile(t):
        split_components(t, q_sc)
        split_components(jnp.minimum(t + 1, n_tiles - 1), qn_sc)
        m_sc[...] = jnp.full(m_sc.shape, NEG_INF, F32)
        acc_sc[...] = jnp.zeros(acc_sc.shape, F32)

    def finish_tile(t):
        a0 = acc_sc[0]
        a1 = acc_sc[1]
        o = a0[:, :B_V_DIM] / a0[:, B_V_DIM:] - lam * (a1[:, :B_V_DIM] / a1[:, B_V_DIM:])
        y = o * lax.rsqrt(jnp.mean(o * o, axis=-1, keepdims=True) + EPS) * g_ref[...]
        o_ref[pl.ds(pl.multiple_of(t * B_TQ, B_TQ), B_TQ), :] = (y * out_scale).astype(BF16)

    start_tile(0)
    k0 = key_block(0)
    for sub in range(n_sub):
        for c in range(2):
            score_rows(q_sc, k0, first_slot, sub, c)
    s_sc[0] = s_sc[first_slot]
    last_step(0, 0)
    finish_tile(0)

    def tile(t, carry):
        start_tile(t)
        step(0, first_slot, 1, 1)

        def pair(p, carry):
            odd = 2 * p + 1
            step(odd, 1, odd + 1, 0)
            step(odd + 1, 0, odd + 2, 1)
            return carry

        lax.fori_loop(0, (t - 1) // 2, pair, 0)

        @pl.when(t % 2 == 1)
        def _():
            last_step(t, 1)

        @pl.when(t % 2 == 0)
        def _():
            step(t - 1, 1, t, 0)
            last_step(t, 0)

        finish_tile(t)
        return carry

    lax.fori_loop(1, n_tiles, tile, 0)


def _attn_b(lam, qb, kb, vb, g_subln, batch, seq, out_scale):
    assert B_TQ == B_TK and B_V_DIM == LANES
    n = qb.shape[1]
    by_seq = lambda b, h: (h, b, 0)
    return pl.pallas_call(
        functools.partial(_attn_b_kernel, out_scale=out_scale),
        grid=(batch, B_HEADS),
        in_specs=[
            pl.BlockSpec(memory_space=pltpu.SMEM),
            pl.BlockSpec((None, seq, LANES), by_seq),
            pl.BlockSpec((None, seq, LANES), by_seq),
            pl.BlockSpec((None, seq, LANES), by_seq),
            pl.BlockSpec((1, B_V_DIM), lambda b, h: (0, 0)),
        ],
        out_specs=pl.BlockSpec((None, seq, LANES), by_seq),
        out_shape=jax.ShapeDtypeStruct((B_HEADS, n, LANES), BF16),
        scratch_shapes=[
            pltpu.VMEM((2, B_TQ, LANES), BF16),
            pltpu.VMEM((2, B_TQ, LANES), BF16),
            pltpu.VMEM((3, 2, B_TQ, B_TK), F32),
            pltpu.VMEM((2, B_TQ, LANES), F32),
            pltpu.VMEM((2, B_TQ, 2 * LANES), F32),
        ],
        compiler_params=_cparams(2),
        name="attn_b",
    )(lam, qb, kb, vb, g_subln.reshape(1, B_V_DIM))


def _pack_bf16_pairs(v):
    half = v.shape[1] // 2
    vb = v.astype(BF16)
    hi = lax.bitcast_convert_type(vb[:, :half].astype(F32), jnp.int32)
    lo = lax.bitcast_convert_type(vb[:, half:].astype(F32), jnp.int32)
    return hi | lax.shift_right_logical(lo, jnp.full(lo.shape, 16, jnp.int32))


def _unpack_bf16_pairs(w):
    first = lax.bitcast_convert_type(w & jnp.int32(-65536), F32)
    second = lax.bitcast_convert_type(lax.shift_left(w, jnp.full(w.shape, 16, jnp.int32)), F32)
    return first, second


def _out_route_kernel(oa_ref, ob_ref, x_ref, mod_ref, wo_ref, g_ref, wr_ref, br_ref, cin_ref,
                      x1_in_ref, x1_ref, hp_ref, ti_ref, tw_ref, rk_ref, cnt_ref, tri_sc, carry_sc):
    del x1_in_ref
    i = pl.program_id(0)
    tm = x_ref.shape[0]

    @pl.when(i == 0)
    def _():
        r = lax.broadcasted_iota(jnp.int32, (tm, tm), 0)
        c = lax.broadcasted_iota(jnp.int32, (tm, tm), 1)
        tri_sc[...] = jnp.where(r < c, 1.0, 0.0).astype(BF16)
        carry_sc[...] = cin_ref[...]

    mod = mod_ref[...]
    o = jnp.concatenate([oa_ref[s] for s in range(oa_ref.shape[0])]
                        + [ob_ref[s] for s in range(ob_ref.shape[0])], axis=1)
    mix = jnp.dot(o, wo_ref[...], preferred_element_type=F32)
    x1 = x_ref[...] + mod[2:3, :] * mix
    x1_ref[...] = x1
    y = x1 * lax.rsqrt(jnp.mean(x1 * x1, axis=-1, keepdims=True) + EPS) * g_ref[...]
    h = y * (1.0 + mod[4:5, :]) + mod[3:4, :]
    hb = h.astype(BF16)
    hp_ref[...] = _pack_bf16_pairs(h)

    logits = lax.dot_general(wr_ref[...], hb, (((1,), (1,)), ((), ())),
                             preferred_element_type=F32) + br_ref[...]
    eid = lax.broadcasted_iota(jnp.int32, logits.shape, 0).astype(F32)
    work = logits
    vals, ids = [], []
    chosen = jnp.zeros(logits.shape, F32)
    for _ in range(TOP_K):
        v = jnp.max(work, axis=0, keepdims=True)
        e = jnp.min(jnp.where(work == v, eid, float(N_EXPERTS)), axis=0, keepdims=True)
        hit = eid == e
        vals.append(v)
        ids.append(e)
        chosen = jnp.where(hit, 1.0, chosen)
        work = jnp.where(hit, -jnp.inf, work)
    ex = [jnp.exp(v - vals[0]) for v in vals]
    den = ex[0] + ex[1] + ex[2] + ex[3]

    before = jnp.dot(chosen.astype(BF16), tri_sc[...], preferred_element_type=F32) + carry_sc[...]
    slot = lax.broadcasted_iota(jnp.int32, (SUBLANES, tm), 0)
    ti = jnp.zeros((SUBLANES, tm), F32)
    tw = jnp.zeros((SUBLANES, tm), F32)
    rk = jnp.zeros((SUBLANES, tm), F32)
    for kk in range(TOP_K):
        r_k = jnp.sum(jnp.where(eid == ids[kk], before, 0.0), axis=0, keepdims=True)
        ti = jnp.where(slot == kk, ids[kk], ti)
        tw = jnp.where(slot == kk, ex[kk] / den, tw)
        rk = jnp.where(slot == kk, r_k, rk)
    ti_ref[...] = ti.astype(jnp.int32)
    tw_ref[...] = tw
    rk_ref[...] = rk.astype(jnp.int32)
    carry = carry_sc[...] + jnp.sum(chosen, axis=1, keepdims=True)
    carry_sc[...] = carry
    cnt_ref[...] = carry


def _out_route(b, oa, ob, x2, mod, w_out_bf, g_ffn, w_router_bf, b_router, counts_in, x1_buf, seq):
    n = x2.shape[0]
    tiles = seq // TOK_TILE
    row = lambda i: (b * tiles + i, 0)
    local = lambda i: (i, 0)
    by_lane = lambda i: (0, i)
    fixed = lambda i: (0, 0)
    return pl.pallas_call(
        _out_route_kernel,
        grid=(tiles,),
        in_specs=[
            pl.BlockSpec((oa.shape[0], TOK_TILE, LANES), lambda i: (0, b * tiles + i, 0)),
            pl.BlockSpec((ob.shape[0], TOK_TILE, LANES), lambda i: (0, b * tiles + i, 0)),
            pl.BlockSpec((TOK_TILE, D_MODEL), row),
            pl.BlockSpec((None, SUBLANES, D_MODEL), lambda i: (b, 0, 0)),
            pl.BlockSpec((D_MODEL, D_MODEL), fixed),
            pl.BlockSpec((1, D_MODEL), fixed),
            pl.BlockSpec((N_EXPERTS, D_MODEL), fixed),
            pl.BlockSpec((N_EXPERTS, 1), fixed),
            pl.BlockSpec((N_EXPERTS, 1), fixed),
            pl.BlockSpec(memory_space=pl.ANY),
        ],
        out_specs=[
            pl.BlockSpec((TOK_TILE, D_MODEL), row),
            pl.BlockSpec((TOK_TILE, D_MODEL // 2), local),
            pl.BlockSpec((SUBLANES, TOK_TILE), by_lane),
            pl.BlockSpec((SUBLANES, TOK_TILE), by_lane),
            pl.BlockSpec((SUBLANES, TOK_TILE), by_lane),
            pl.BlockSpec((N_EXPERTS, 1), fixed),
        ],
        out_shape=[
            jax.ShapeDtypeStruct((n, D_MODEL), F32),
            jax.ShapeDtypeStruct((seq, D_MODEL // 2), jnp.int32),
            jax.ShapeDtypeStruct((SUBLANES, seq), jnp.int32),
            jax.ShapeDtypeStruct((SUBLANES, seq), F32),
            jax.ShapeDtypeStruct((SUBLANES, seq), jnp.int32),
            jax.ShapeDtypeStruct((N_EXPERTS, 1), F32),
        ],
        input_output_aliases={9: 0},
        scratch_shapes=[pltpu.VMEM((TOK_TILE, TOK_TILE), BF16), pltpu.VMEM((N_EXPERTS, 1), F32)],
        compiler_params=_cparams(1, VMEM_LIMIT),
        name="out_route",
    )(oa, ob, x2, mod, w_out_bf, g_ffn.reshape(1, D_MODEL), w_router_bf.T,
      b_router.reshape(N_EXPERTS, 1), counts_in, x1_buf)


def _experts_kernel(tiles_ref, rows_ref, base_ref, next_ref, xs_ref, wgu_ref, bgu_ref, wd_ref,
                    bd_ref, y_ref, wgu_sc, wd_sc, x_buf, y_buf, x_sem, y_sem):
    e = pl.program_id(0)
    n_tiles = tiles_ref[e]
    n_rows = rows_ref[e]
    n_live = base_ref[N_EXPERTS]
    region = e * EXPERT_ROWS

    def x_copy(row, slot):
        return pltpu.make_async_copy(xs_ref.at[pl.ds(row, ROW_TILE), :], x_buf.at[slot],
                                     x_sem.at[slot])

    def y_copy(row, slot):
        return pltpu.make_async_copy(y_buf.at[slot], y_ref.at[pl.ds(row, ROW_TILE), :],
                                     y_sem.at[slot])

    @pl.when(e == 0)
    def _():
        y_buf[...] = jnp.zeros(y_buf.shape, jnp.int32)

        @pl.when(n_live > 0)
        def _():
            x_copy(pl.multiple_of(next_ref[N_EXPERTS], ROW_TILE), 0).start()

    @pl.when(n_tiles > 0)
    def _():
        wgu_sc[...] = wgu_ref[...].astype(BF16)
        wd_sc[...] = wd_ref[...].astype(BF16)

    def tile(j, carry):
        g = base_ref[e] + j
        slot = g % 2
        row = pl.multiple_of(region + j * ROW_TILE, ROW_TILE)
        x_copy(row, slot).wait()
        next_row = jnp.where(j + 1 < n_tiles, row + ROW_TILE, next_ref[e])

        @pl.when(g + 1 < n_live)
        def _():
            x_copy(pl.multiple_of(next_row, ROW_TILE), 1 - slot).start()

        @pl.when(g >= 2)
        def _():
            y_copy(0, slot).wait()

        needed = jnp.minimum(n_rows - j * ROW_TILE, ROW_TILE)
        for rows in range(ROW_STEP, ROW_TILE + 1, ROW_STEP):
            @pl.when(jnp.logical_and(needed > rows - ROW_STEP, needed <= rows))
            def _(rows=rows):
                used = lax.broadcasted_iota(jnp.int32, (rows, 1), 0) < needed
                first, second = _unpack_bf16_pairs(jnp.where(used, x_buf[slot, :rows, :], 0))
                x = jnp.concatenate([first.astype(BF16), second.astype(BF16)], axis=1)
                acc = jnp.zeros((rows, D_MODEL), F32)
                for c in range(D_EXPERT // FF_CHUNK):
                    lo_c, hi_c = c * FF_CHUNK, (c + 1) * FF_CHUNK
                    gate = (jnp.dot(x, wgu_sc[:, lo_c:hi_c], preferred_element_type=F32)
                            + bgu_ref[:, lo_c:hi_c])
                    up = (jnp.dot(x, wgu_sc[:, D_EXPERT + lo_c:D_EXPERT + hi_c],
                                  preferred_element_type=F32)
                          + bgu_ref[:, D_EXPERT + lo_c:D_EXPERT + hi_c])
                    gate = jnp.minimum(gate, SWIGLU_LIMIT)
                    up = jnp.clip(up, -SWIGLU_LIMIT, SWIGLU_LIMIT)
                    act = (up + 1.0) * (gate * jax.nn.sigmoid(SWIGLU_ALPHA * gate))
                    acc = acc + jnp.dot(act.astype(BF16), wd_sc[lo_c:hi_c, :],
                                        preferred_element_type=F32)
                y_buf[slot, :rows, :] = _pack_bf16_pairs(acc + bd_ref[...])

        y_copy(row, slot).start()
        return carry

    lax.fori_loop(0, n_tiles, tile, 0)

    @pl.when(e == pl.num_programs(0) - 1)
    def _():
        for back in (2, 1):
            @pl.when(n_live >= back)
            def _():
                y_copy(0, (n_live - back) % 2).wait()


def _experts(counts, xs, w_gate_up, b_gate_up, w_down, b_down):
    n_rows, width = xs.shape
    tiles = (counts + ROW_TILE - 1) // ROW_TILE
    base = jnp.concatenate([jnp.zeros((1,), jnp.int32), jnp.cumsum(tiles)]).astype(jnp.int32)
    region = jnp.arange(N_EXPERTS, dtype=jnp.int32) * EXPERT_ROWS
    later = jnp.arange(N_EXPERTS)[None, :] > jnp.arange(-1, N_EXPERTS)[:, None]
    cand = jnp.where(jnp.logical_and(later, (tiles > 0)[None, :]), region[None, :], n_rows)
    nxt = jnp.min(cand, axis=1)
    next_row = jnp.concatenate([nxt[1:], nxt[:1]]).astype(jnp.int32)
    by_expert = lambda e, *_: (e, 0, 0)
    return pl.pallas_call(
        _experts_kernel,
        grid_spec=pltpu.PrefetchScalarGridSpec(
            num_scalar_prefetch=4,
            grid=(N_EXPERTS,),
            in_specs=[
                pl.BlockSpec(memory_space=pl.ANY),
                pl.BlockSpec((None, D_MODEL, 2 * D_EXPERT), by_expert),
                pl.BlockSpec((None, 1, 2 * D_EXPERT), by_expert),
                pl.BlockSpec((None, D_EXPERT, D_MODEL), by_expert),
                pl.BlockSpec((None, 1, D_MODEL), by_expert),
            ],
            out_specs=pl.BlockSpec(memory_space=pl.ANY),
            scratch_shapes=[
                pltpu.VMEM((D_MODEL, 2 * D_EXPERT), BF16),
                pltpu.VMEM((D_EXPERT, D_MODEL), BF16),
                pltpu.VMEM((2, ROW_TILE, width), jnp.int32),
                pltpu.VMEM((2, ROW_TILE, D_MODEL // 2), jnp.int32),
                pltpu.SemaphoreType.DMA((2,)),
                pltpu.SemaphoreType.DMA((2,)),
            ],
        ),
        out_shape=jax.ShapeDtypeStruct((n_rows, D_MODEL // 2), jnp.int32),
        compiler_params=_cparams(1, VMEM_LIMIT),
        name="experts",
    )(tiles.astype(jnp.int32), counts.astype(jnp.int32), base, next_row, xs, w_gate_up,
      b_gate_up.reshape(N_EXPERTS, 1, 2 * D_EXPERT), w_down, b_down.reshape(N_EXPERTS, 1, D_MODEL))


def _sc_gather_rows(table, idx):
    m = idx.shape[0]
    width = table.shape[1]
    mesh = plsc.VectorSubcoreMesh(core_axis_name="core", subcore_axis_name="subcore")
    n_workers = mesh.num_cores * mesh.num_subcores
    per_worker = m // n_workers
    assert per_worker * n_workers == m and per_worker % SC_WINDOW == 0

    @pl.kernel(
        out_type=jax.ShapeDtypeStruct((m, width), table.dtype),
        mesh=mesh,
        scratch_types=[
            pltpu.VMEM((SC_WINDOW,), jnp.int32),
            pltpu.VMEM((SC_WINDOW, width), table.dtype),
            pltpu.SemaphoreType.DMA,
        ],
    )
    def gather_kernel(table_hbm, idx_hbm, out_hbm, idx_v, rows_v, sem):
        worker = lax.axis_index("subcore") * mesh.num_cores + lax.axis_index("core")

        @pl.loop(0, per_worker // SC_WINDOW)
        def _(j):
            base = pl.multiple_of(worker * per_worker + j * SC_WINDOW, SC_WINDOW)
            pltpu.sync_copy(idx_hbm.at[pl.ds(base, SC_WINDOW)], idx_v)
            pltpu.async_copy(table_hbm.at[idx_v], rows_v, sem).wait()
            pltpu.sync_copy(rows_v, out_hbm.at[pl.ds(base, SC_WINDOW)])

    return gather_kernel(table, idx)


def _sc_scatter_rows_into(table, dst, out_ref, reps):
    m, width = table.shape
    mesh = plsc.VectorSubcoreMesh(core_axis_name="core", subcore_axis_name="subcore")
    n_workers = mesh.num_cores * mesh.num_subcores
    per_worker = m // n_workers
    assert per_worker * n_workers == m and per_worker % SC_WINDOW == 0 and dst.shape == (reps * m,)

    @pl.kernel(
        out_type=(),
        mesh=mesh,
        scratch_types=[
            pltpu.VMEM((SC_WINDOW,), jnp.int32),
            pltpu.VMEM((SC_WINDOW, width), table.dtype),
            pltpu.SemaphoreType.DMA,
        ],
    )
    def scatter_kernel(table_hbm, dst_hbm, out_hbm, dst_v, rows_v, sem):
        worker = lax.axis_index("subcore") * mesh.num_cores + lax.axis_index("core")

        @pl.loop(0, per_worker // SC_WINDOW)
        def _(j):
            base = pl.multiple_of(worker * per_worker + j * SC_WINDOW, SC_WINDOW)
            pltpu.sync_copy(table_hbm.at[pl.ds(base, SC_WINDOW)], rows_v)
            for k in range(reps):
                pltpu.sync_copy(dst_hbm.at[pl.ds(k * m + base, SC_WINDOW)], dst_v)
                pltpu.async_copy(rows_v, out_hbm.at[dst_v], sem).wait()

    scatter_kernel(table, dst, out_ref)


def _combine_dense_kernel(tw_ref, x1_ref, mod_ref, g_ref, y0_ref, y1_ref, y2_ref, y3_ref, o_in_ref,
                          o_ref):
    del o_in_ref
    tm = x1_ref.shape[0]
    tw = jnp.concatenate([tw_ref[...], jnp.zeros((LANES - SUBLANES, tm), F32)], axis=0).T
    first = second = None
    for kk, y_ref in enumerate((y0_ref, y1_ref, y2_ref, y3_ref)):
        f_k, s_k = _unpack_bf16_pairs(y_ref[...])
        w_k = tw[:, kk:kk + 1]
        first = w_k * f_k if kk == 0 else first + w_k * f_k
        second = w_k * s_k if kk == 0 else second + w_k * s_k
    ffn = jnp.concatenate([first, second], axis=1)
    x2 = x1_ref[...] + mod_ref[5:6, :] * ffn
    o_ref[...] = x2 * lax.rsqrt(jnp.mean(x2 * x2, axis=-1, keepdims=True) + EPS) * g_ref[...]


def _combine_dense(b, tw, x1, mod, g_final, yg, out_buf, seq):
    n = x1.shape[0]
    width = yg.shape[1]
    tiles = seq // TOK_TILE
    row = lambda i: (b * tiles + i, 0)
    slot = lambda kk: pl.BlockSpec((TOK_TILE, width), lambda i: (kk * tiles + i, 0))
    return pl.pallas_call(
        _combine_dense_kernel,
        grid=(tiles,),
        in_specs=[
            pl.BlockSpec((SUBLANES, TOK_TILE), lambda i: (0, i)),
            pl.BlockSpec((TOK_TILE, D_MODEL), row),
            pl.BlockSpec((None, SUBLANES, D_MODEL), lambda i: (b, 0, 0)),
            pl.BlockSpec((1, D_MODEL), lambda i: (0, 0)),
        ] + [slot(kk) for kk in range(TOP_K)] + [pl.BlockSpec(memory_space=pl.ANY)],
        out_specs=pl.BlockSpec((TOK_TILE, D_MODEL), row),
        out_shape=jax.ShapeDtypeStruct((n, D_MODEL), F32),
        input_output_aliases={4 + TOP_K: 0},
        compiler_params=_cparams(1, VMEM_LIMIT),
        name="combine",
    )(tw, x1, mod, g_final.reshape(1, D_MODEL), yg, yg, yg, yg, out_buf)


def _rope_base(positions):
    half = ROPE_DIM // 2
    inv_freq = ROPE_THETA ** (-jnp.arange(0, ROPE_DIM, 2, dtype=F32) / ROPE_DIM)
    n = positions.size
    pos = jnp.broadcast_to(positions.reshape(n, 1).astype(F32), (n, half)).reshape(-1, LANES)
    ang = pos * jnp.tile(inv_freq, LANES // half)
    cos, sin = lax.optimization_barrier((jnp.cos(ang), jnp.sin(ang)))
    pad = jnp.zeros((n, LANES - ROPE_DIM), F32)
    return jnp.concatenate([cos.reshape(n, half), sin.reshape(n, half), pad], axis=1)


def kernel(x, c, positions, w_ada, b_ada, g_mix_norm, w_in, rel_bias, lambda_q1, lambda_k1,
           lambda_q2, lambda_k2, g_subln, w_out, g_ffn_norm, w_router, b_router, w_gate_up,
           b_gate_up, w_down, b_down, g_final):
    batch, seq, _ = x.shape
    depth = w_ada.shape[0]
    assert depth == 1, "the combine kernel applies the final norm, so it must follow the only layer"
    n = batch * seq
    rope_base = _rope_base(positions)
    x2 = x.reshape(n, D_MODEL)
    for l in range(depth):
        lambda_init = 0.8 - 0.6 * math.exp(-0.3 * l)
        mod = _ada(c, w_ada[l], b_ada[l])
        qa, ka, va, qb, kb, vb = _in_proj(x2, mod, g_mix_norm[l], w_in[l].astype(BF16),
                                          rope_base, seq)
        oa = _attn_a(qa, ka, va, _rel_bias_rows(rel_bias[l]), batch, seq)
        lam = (jnp.exp(jnp.sum(lambda_q1[l].astype(F32) * lambda_k1[l].astype(F32)))
               - jnp.exp(jnp.sum(lambda_q2[l].astype(F32) * lambda_k2[l].astype(F32)))
               + lambda_init).reshape(1)
        ob = _attn_b(lam, qb, kb, vb, g_subln[l], batch, seq, 1.0 - lambda_init)
        w_out_bf = w_out[l].astype(BF16)
        w_router_bf = w_router[l].astype(BF16)
        xs_ref = jax.new_ref(lax.empty((N_EXPERTS * EXPERT_ROWS, D_MODEL // 2), jnp.int32))
        x1 = lax.empty((n, D_MODEL), F32)
        counts = jnp.zeros((N_EXPERTS, 1), F32)
        pos_rows, weight_rows = [], []
        for b in range(batch):
            x1, hp, top_i, top_w, rank, counts = _out_route(
                b, oa, ob, x2, mod, w_out_bf, g_ffn_norm[l], w_router_bf, b_router[l], counts, x1,
                seq)
            pos = top_i * EXPERT_ROWS + rank
            pos_rows.append(pos)
            weight_rows.append(top_w)
            dst = jnp.concatenate([pos[kk] for kk in range(TOP_K)])
            _sc_scatter_rows_into(hp, dst, xs_ref, TOP_K)
        xs = jax.freeze(xs_ref)
        y = _experts(counts[:, 0].astype(jnp.int32), xs, w_gate_up[l], b_gate_up[l], w_down[l],
                     b_down[l])
        x2 = lax.empty((n, D_MODEL), F32)
        for b in range(batch):
            yg = _sc_gather_rows(y, jnp.concatenate([pos_rows[b][kk] for kk in range(TOP_K)]))
            x2 = _combine_dense(b, weight_rows[b], x1, mod, g_final, yg, x2, seq)
    return x2.reshape(batch, seq, D_MODEL)
```

```python
import functools
import math

import jax
import jax.numpy as jnp
from jax import lax
from jax.experimental import pallas as pl
from jax.experimental.pallas import tpu as pltpu
from jax.experimental.pallas import tpu_sc as plsc

D_MODEL = 1024
CHUNK = 64
HEAD_DIM = 64
A_HEADS = 8
A_WIDTH = A_HEADS * HEAD_DIM
LEFT_CHUNKS = 8
MAX_REL = 128
B_HEADS = 4
B_QK_DIM = HEAD_DIM
B_V_DIM = 2 * HEAD_DIM
B_WIDTH = B_HEADS * B_V_DIM
ROPE_THETA = 500000.0
ROPE_DIM = B_QK_DIM // 4
N_EXPERTS = 32
TOP_K = 4
D_EXPERT = D_MODEL
SWIGLU_LIMIT = 7.0
SWIGLU_ALPHA = 1.702
EPS = 1e-6
NEG_INF = -1e30
LOG2_E = math.log2(math.e)
N_MOD = 6

LANES = 128
SUBLANES = 8
VMEM_BYTES_V7X = 64 * 1024 * 1024
VMEM_LIMIT = VMEM_BYTES_V7X * 7 // 8

TOK_TILE = 512
A_QBLK = 2 * CHUNK
A_BAND = (LEFT_CHUNKS + 2) * CHUNK
A_ROLL = A_BAND + A_QBLK
B_TQ = 512
B_TK = 512
B_SUB = 128
ROW_TILE = 512
ROW_STEP = 128
FF_CHUNK = 512
EXPERT_ROWS = 16384
SC_WINDOW = 128

F32 = jnp.float32
BF16 = jnp.bfloat16


def _cparams(n_axes, vmem=None):
    return pltpu.CompilerParams(
        dimension_semantics=("arbitrary",) * n_axes,
        vmem_limit_bytes=vmem,
    )


def _ada_kernel(c_ref, w_ref, b_ref, o_ref):
    c = c_ref[...]
    act = c * jax.nn.sigmoid(c)
    o_ref[...] = jnp.dot(act, w_ref[...], preferred_element_type=F32,
                         precision=lax.Precision.HIGHEST) + b_ref[...]


def _ada(c, w_ada, b_ada):
    b = c.shape[0]
    rows = -(-b // SUBLANES) * SUBLANES
    c_pad = jnp.pad(c, ((0, rows - b), (0, 0)))
    n_out = w_ada.shape[1]
    out = pl.pallas_call(
        _ada_kernel,
        grid=(n_out // D_MODEL,),
        in_specs=[
            pl.BlockSpec((rows, D_MODEL), lambda j: (0, 0)),
            pl.BlockSpec((D_MODEL, D_MODEL), lambda j: (0, j)),
            pl.BlockSpec((1, D_MODEL), lambda j: (0, j)),
        ],
        out_specs=pl.BlockSpec((rows, D_MODEL), lambda j: (0, j)),
        out_shape=jax.ShapeDtypeStruct((rows, n_out), F32),
        compiler_params=_cparams(1),
        name="ada",
    )(c_pad, w_ada, b_ada.reshape(1, n_out))
    mod = out[:b].reshape(b, N_MOD, D_MODEL)
    return jnp.pad(mod, ((0, 0), (0, SUBLANES - N_MOD), (0, 0)))


def _in_proj_kernel(x_ref, mod_ref, g_ref, w_ref, rb_ref,
                    qa_ref, ka_ref, va_ref, qb_ref, kb_ref, vb_ref):
    x = x_ref[...]
    mod = mod_ref[...]
    y = x * lax.rsqrt(jnp.mean(x * x, axis=-1, keepdims=True) + EPS) * g_ref[...]
    h = (y * (1.0 + mod[1:2, :]) + mod[0:1, :]).astype(BF16)
    q_scale = HEAD_DIM ** -0.5

    half = ROPE_DIM // 2
    rb = rb_ref[...]
    lane = lax.broadcasted_iota(jnp.int32, rb.shape, 1)
    cos_lo = jnp.where(lane < half, rb, 0.0)
    sin_hi = jnp.where(jnp.logical_and(lane >= half, lane < ROPE_DIM), rb, 0.0)
    cos_pair = cos_lo + pltpu.roll(cos_lo, half, 1)
    rc = (cos_pair + pltpu.roll(cos_pair, B_QK_DIM, 1)
          + jnp.where(lane % B_QK_DIM >= ROPE_DIM, 1.0, 0.0))
    sin_lo = pltpu.roll(sin_hi, LANES - half, 1)
    rm = -(sin_lo + pltpu.roll(sin_lo, B_QK_DIM, 1))
    rp = sin_hi + pltpu.roll(sin_hi, B_QK_DIM, 1)

    def rope(p):
        cols = []
        for s in range(p.shape[1] // LANES):
            v = p[:, s * LANES:(s + 1) * LANES]
            cols.append(v * rc + pltpu.roll(v, LANES - ROPE_DIM // 2, 1) * rm
                        + pltpu.roll(v, ROPE_DIM // 2, 1) * rp)
        return jnp.concatenate(cols, axis=1)

    outs = (qa_ref, ka_ref, va_ref, qb_ref, kb_ref, vb_ref)
    for j, o_ref in enumerate(outs):
        p = jnp.dot(h, w_ref[:, j * A_WIDTH:(j + 1) * A_WIDTH], preferred_element_type=F32)
        if j in (3, 4):
            p = rope(p)
        if j in (0, 3):
            p = p * (q_scale * LOG2_E)
        pb = p.astype(BF16)
        for s in range(A_WIDTH // LANES):
            o_ref[s] = pb[:, s * LANES:(s + 1) * LANES]


def _in_proj(x2, mod, g_mix, w_in_bf, rope_base, seq):
    n = x2.shape[0]
    tiles_per_seq = seq // TOK_TILE
    row = lambda i: (i, 0)
    fixed = lambda i: (0, 0)
    n_slabs = A_WIDTH // LANES
    out_sd = jax.ShapeDtypeStruct((n_slabs, n, LANES), BF16)
    return pl.pallas_call(
        _in_proj_kernel,
        grid=(n // TOK_TILE,),
        in_specs=[
            pl.BlockSpec((TOK_TILE, D_MODEL), row),
            pl.BlockSpec((None, SUBLANES, D_MODEL), lambda i: (i // tiles_per_seq, 0, 0)),
            pl.BlockSpec((1, D_MODEL), fixed),
            pl.BlockSpec(w_in_bf.shape, fixed),
            pl.BlockSpec((TOK_TILE, LANES), row),
        ],
        out_specs=[pl.BlockSpec((n_slabs, TOK_TILE, LANES), lambda i: (0, i, 0))] * 6,
        out_shape=[out_sd] * 6,
        compiler_params=_cparams(1, VMEM_LIMIT),
        name="in_proj",
    )(x2, mod, g_mix.reshape(1, D_MODEL), w_in_bf, rope_base)


def _attn_a_kernel(q_ref, kp_ref, kc_ref, vp_ref, vc_ref, bias_ref, o_ref, k_sc, v_sc, bias_sc):
    g = pl.program_id(0)
    n_pairs, blk, _ = q_ref.shape
    lane = lax.broadcasted_iota(jnp.int32, (A_QBLK, LANES), 1)
    col = lax.broadcasted_iota(jnp.int32, (A_QBLK, A_BAND), 1)
    ones = jnp.ones((A_BAND, LANES), BF16)

    @pl.when(g == 0)
    def _():
        q_chunk = lax.broadcasted_iota(jnp.int32, (A_QBLK, A_BAND), 0) // CHUNK
        k_chunk = col // CHUNK
        in_band = jnp.logical_and(k_chunk >= q_chunk, k_chunk <= q_chunk + LEFT_CHUNKS)
        for h in range(bias_ref.shape[0]):
            rolled = pltpu.roll(jnp.broadcast_to(bias_ref[h], (A_QBLK, A_ROLL)), 0, 1,
                                stride=1, stride_axis=0)
            bias_sc[h] = jnp.where(in_band, rolled[:, :A_BAND], NEG_INF)

    def pair(p):
        k_buf = k_sc.at[p % 2]
        v_buf = v_sc.at[p % 2]
        k_buf[0:blk, :] = kp_ref[p]
        k_buf[blk:2 * blk, :] = kc_ref[p]
        v_buf[0:blk, :] = vp_ref[p]
        v_buf[blk:2 * blk, :] = vc_ref[p]

        def scores(m, hh):
            r0 = m * A_QBLK
            q = q_ref[p, r0:r0 + A_QBLK, :]
            in_head = (lane < HEAD_DIM) if hh == 0 else (lane >= HEAD_DIM)
            qh = jnp.where(in_head, q, jnp.zeros_like(q))
            return lax.dot_general(qh, k_buf[r0:r0 + A_BAND, :], (((1,), (1,)), ((), ())),
                                   preferred_element_type=F32)

        chains = [(m, hh) for m in range(blk // A_QBLK) for hh in range(2)]
        s_next = scores(*chains[0])
        halves = []
        for i, (m, hh) in enumerate(chains):
            r0 = m * A_QBLK
            s = s_next
            if i + 1 < len(chains):
                s_next = scores(*chains[i + 1])
            valid = jnp.logical_or(g > 0, col + r0 >= blk)
            s = jnp.where(valid, s + bias_sc[2 * p + hh], NEG_INF)
            pr = jnp.exp2(s - jnp.max(s, axis=1, keepdims=True))
            v_ext = jnp.concatenate([v_buf[r0:r0 + A_BAND, :], ones], axis=1)
            pv = jnp.dot(pr.astype(BF16), v_ext, preferred_element_type=F32)
            halves.append(pv[:, :LANES] / pv[:, LANES:])
            if hh == 1:
                o_ref[p, r0:r0 + A_QBLK, :] = jnp.where(lane < HEAD_DIM, halves[0],
                                                        halves[1]).astype(BF16)
                halves = []

    for p in range(n_pairs):
        pair(p)


def _attn_a(b, qa, ka, va, bias_rows, seq):
    n_pairs = qa.shape[0]
    blk = LEFT_CHUNKS * CHUNK
    nblk = seq // blk
    cur = lambda g: (0, b * nblk + g, 0)
    prev = lambda g: (0, b * nblk + jnp.maximum(g - 1, 0), 0)
    slab = (n_pairs, blk, LANES)
    return pl.pallas_call(
        _attn_a_kernel,
        grid=(nblk,),
        in_specs=[
            pl.BlockSpec(slab, cur),
            pl.BlockSpec(slab, prev),
            pl.BlockSpec(slab, cur),
            pl.BlockSpec(slab, prev),
            pl.BlockSpec(slab, cur),
            pl.BlockSpec(bias_rows.shape, lambda g: (0, 0, 0)),
        ],
        out_specs=pl.BlockSpec(slab, lambda g: (0, g, 0)),
        out_shape=jax.ShapeDtypeStruct((n_pairs, seq, LANES), BF16),
        scratch_shapes=[
            pltpu.VMEM((2, 2 * blk, LANES), BF16),
            pltpu.VMEM((2, 2 * blk, LANES), BF16),
            pltpu.VMEM((bias_rows.shape[0], A_QBLK, A_BAND), F32),
        ],
        compiler_params=_cparams(1),
        name="attn_a",
    )(qa, ka, ka, va, va, bias_rows)


def _rel_bias_rows(rel_table):
    t = rel_table.astype(F32) * LOG2_E
    far = t[:, 2 * MAX_REL:]
    n_far = LEFT_CHUNKS * CHUNK - MAX_REL
    row = jnp.concatenate([
        jnp.broadcast_to(far, (t.shape[0], n_far)),
        t[:, 2 * MAX_REL:0:-1],
        jnp.broadcast_to(far, (t.shape[0], A_ROLL - A_BAND)),
    ], axis=1)
    return row.reshape(t.shape[0], 1, A_ROLL)


def _attn_b_kernel(lam_ref, q_ref, k_ref, v_ref, g_ref, o_ref, q_sc, qn_sc, s_sc, m_sc, acc_sc, *,
                   out_scale):
    n_tiles = q_ref.shape[0] // B_TQ
    n_sub = B_TQ // B_SUB
    first_slot = 2
    lane = lax.broadcasted_iota(jnp.int32, (B_TQ, LANES), 1)
    ones = jnp.ones((B_TK, LANES), BF16)
    lam = lam_ref[0]

    def split_components(t, dst):
        q = q_ref[pl.ds(pl.multiple_of(t * B_TQ, B_TQ), B_TQ), :]
        dst[0] = jnp.where(lane < B_QK_DIM, q, jnp.zeros_like(q))
        dst[1] = jnp.where(lane >= B_QK_DIM, q, jnp.zeros_like(q))

    def score_rows(src, k, slot, sub, c):
        rows = pl.ds(sub * B_SUB, B_SUB)
        s_sc[slot, c, rows, :] = lax.dot_general(src[c, rows, :], k, (((1,), (1,)), ((), ())),
                                                 preferred_element_type=F32)

    def update_rows(s, v_ext, sub, c):
        rows = pl.ds(sub * B_SUB, B_SUB)
        m_prev = m_sc[c, rows, :]
        m_new = jnp.maximum(m_prev, jnp.max(s, axis=1, keepdims=True))
        alpha = jnp.exp2(m_prev - m_new)
        p = jnp.exp2(s - jnp.concatenate([m_new] * (s.shape[1] // LANES), axis=1))
        pv = jnp.dot(p.astype(BF16), v_ext, preferred_element_type=F32)
        acc_sc[c, rows, :] = jnp.concatenate([alpha, alpha], axis=1) * acc_sc[c, rows, :] + pv
        m_sc[c, rows, :] = m_new

    def key_block(blk):
        return k_ref[pl.ds(pl.multiple_of(blk * B_TK, B_TK), B_TK), :]

    def value_block(blk):
        return jnp.concatenate([v_ref[pl.ds(pl.multiple_of(blk * B_TK, B_TK), B_TK), :], ones], axis=1)

    def step(blk, slot, next_blk, next_slot):
        v_ext = value_block(blk)
        k_next = key_block(next_blk)
        for sub in range(n_sub):
            for c in range(2):
                update_rows(s_sc[slot, c, pl.ds(sub * B_SUB, B_SUB), :], v_ext, sub, c)
                score_rows(q_sc, k_next, next_slot, sub, c)

    def last_step(t, slot):
        v_ext = value_block(t)
        k0 = key_block(0)
        col_c = lax.broadcasted_iota(jnp.int32, (B_SUB, B_TK), 1) // CHUNK
        row_c = lax.broadcasted_iota(jnp.int32, (B_SUB, B_TK), 0) // CHUNK
        for sub in range(n_sub):
            keep = col_c <= row_c + sub * (B_SUB // CHUNK)
            for c in range(2):
                s = s_sc[slot, c, pl.ds(sub * B_SUB, B_SUB), :]
                update_rows(jnp.where(keep, s, NEG_INF), v_ext, sub, c)
                score_rows(qn_sc, k0, first_slot, sub, c)

    def start_tile(t):
        split_components(t, q_sc)
        split_components(jnp.minimum(t + 1, n_tiles - 1), qn_sc)
        m_sc[...] = jnp.full(m_sc.shape, NEG_INF, F32)
        acc_sc[...] = jnp.zeros(acc_sc.shape, F32)

    def finish_tile(t):
        a0 = acc_sc[0]
        a1 = acc_sc[1]
        o = a0[:, :B_V_DIM] / a0[:, B_V_DIM:] - lam * (a1[:, :B_V_DIM] / a1[:, B_V_DIM:])
        y = o * lax.rsqrt(jnp.mean(o * o, axis=-1, keepdims=True) + EPS) * g_ref[...]
        o_ref[pl.ds(pl.multiple_of(t * B_TQ, B_TQ), B_TQ), :] = (y * out_scale).astype(BF16)

    start_tile(0)
    k0 = key_block(0)
    for sub in range(n_sub):
        for c in range(2):
            score_rows(q_sc, k0, first_slot, sub, c)
    s_sc[0] = s_sc[first_slot]
    last_step(0, 0)
    finish_tile(0)

    def tile(t, carry):
        start_tile(t)
        step(0, first_slot, 1, 1)

        def pair(p, carry):
            odd = 2 * p + 1
            step(odd, 1, odd + 1, 0)
            step(odd + 1, 0, odd + 2, 1)
            return carry

        lax.fori_loop(0, (t - 1) // 2, pair, 0)

        @pl.when(t % 2 == 1)
        def _():
            last_step(t, 1)

        @pl.when(t % 2 == 0)
        def _():
            step(t - 1, 1, t, 0)
            last_step(t, 0)

        finish_tile(t)
        return carry

    lax.fori_loop(1, n_tiles, tile, 0)


def _attn_b(b, lam, qb, kb, vb, g_subln, seq, out_scale):
    assert B_TQ == B_TK and B_V_DIM == LANES
    by_head = lambda h: (h, b, 0)
    return pl.pallas_call(
        functools.partial(_attn_b_kernel, out_scale=out_scale),
        grid=(B_HEADS,),
        in_specs=[
            pl.BlockSpec(memory_space=pltpu.SMEM),
            pl.BlockSpec((None, seq, LANES), by_head),
            pl.BlockSpec((None, seq, LANES), by_head),
            pl.BlockSpec((None, seq, LANES), by_head),
            pl.BlockSpec((1, B_V_DIM), lambda h: (0, 0)),
        ],
        out_specs=pl.BlockSpec((None, seq, LANES), lambda h: (h, 0, 0)),
        out_shape=jax.ShapeDtypeStruct((B_HEADS, seq, LANES), BF16),
        scratch_shapes=[
            pltpu.VMEM((2, B_TQ, LANES), BF16),
            pltpu.VMEM((2, B_TQ, LANES), BF16),
            pltpu.VMEM((3, 2, B_TQ, B_TK), F32),
            pltpu.VMEM((2, B_TQ, LANES), F32),
            pltpu.VMEM((2, B_TQ, 2 * LANES), F32),
        ],
        compiler_params=_cparams(1),
        name="attn_b",
    )(lam, qb, kb, vb, g_subln.reshape(1, B_V_DIM))


def _pack_bf16_pairs(v):
    half = v.shape[1] // 2
    vb = v.astype(BF16)
    hi = lax.bitcast_convert_type(vb[:, :half].astype(F32), jnp.int32)
    lo = lax.bitcast_convert_type(vb[:, half:].astype(F32), jnp.int32)
    return hi | lax.shift_right_logical(lo, jnp.full(lo.shape, 16, jnp.int32))


def _unpack_bf16_pairs(w):
    first = lax.bitcast_convert_type(w & jnp.int32(-65536), F32)
    second = lax.bitcast_convert_type(lax.shift_left(w, jnp.full(w.shape, 16, jnp.int32)), F32)
    return first, second


def _out_route_kernel(oa_ref, ob_ref, x_ref, mod_ref, wo_ref, g_ref, wr_ref, br_ref, cin_ref,
                      x1_in_ref, x1_ref, hp_ref, ti_ref, tw_ref, rk_ref, cnt_ref, tri_sc, carry_sc):
    del x1_in_ref
    i = pl.program_id(0)
    tm = x_ref.shape[0]

    @pl.when(i == 0)
    def _():
        r = lax.broadcasted_iota(jnp.int32, (tm, tm), 0)
        c = lax.broadcasted_iota(jnp.int32, (tm, tm), 1)
        tri_sc[...] = jnp.where(r < c, 1.0, 0.0).astype(BF16)
        carry_sc[...] = cin_ref[...]

    mod = mod_ref[...]
    o = jnp.concatenate([oa_ref[s] for s in range(oa_ref.shape[0])]
                        + [ob_ref[s] for s in range(ob_ref.shape[0])], axis=1)
    mix = jnp.dot(o, wo_ref[...], preferred_element_type=F32)
    x1 = x_ref[...] + mod[2:3, :] * mix
    x1_ref[...] = x1
    y = x1 * lax.rsqrt(jnp.mean(x1 * x1, axis=-1, keepdims=True) + EPS) * g_ref[...]
    h = y * (1.0 + mod[4:5, :]) + mod[3:4, :]
    hb = h.astype(BF16)
    hp_ref[...] = _pack_bf16_pairs(h)

    logits = lax.dot_general(wr_ref[...], hb, (((1,), (1,)), ((), ())),
                             preferred_element_type=F32) + br_ref[...]
    eid = lax.broadcasted_iota(jnp.int32, logits.shape, 0).astype(F32)
    work = logits
    vals, ids = [], []
    chosen = jnp.zeros(logits.shape, F32)
    for _ in range(TOP_K):
        v = jnp.max(work, axis=0, keepdims=True)
        e = jnp.min(jnp.where(work == v, eid, float(N_EXPERTS)), axis=0, keepdims=True)
        hit = eid == e
        vals.append(v)
        ids.append(e)
        chosen = jnp.where(hit, 1.0, chosen)
        work = jnp.where(hit, -jnp.inf, work)
    ex = [jnp.exp(v - vals[0]) for v in vals]
    den = ex[0] + ex[1] + ex[2] + ex[3]

    before = jnp.dot(chosen.astype(BF16), tri_sc[...], preferred_element_type=F32) + carry_sc[...]
    slot = lax.broadcasted_iota(jnp.int32, (SUBLANES, tm), 0)
    ti = jnp.zeros((SUBLANES, tm), F32)
    tw = jnp.zeros((SUBLANES, tm), F32)
    rk = jnp.zeros((SUBLANES, tm), F32)
    for kk in range(TOP_K):
        r_k = jnp.sum(jnp.where(eid == ids[kk], before, 0.0), axis=0, keepdims=True)
        ti = jnp.where(slot == kk, ids[kk], ti)
        tw = jnp.where(slot == kk, ex[kk] / den, tw)
        rk = jnp.where(slot == kk, r_k, rk)
    ti_ref[...] = ti.astype(jnp.int32)
    tw_ref[...] = tw
    rk_ref[...] = rk.astype(jnp.int32)
    carry = carry_sc[...] + jnp.sum(chosen, axis=1, keepdims=True)
    carry_sc[...] = carry
    cnt_ref[...] = carry


def _out_route(b, oa, ob, x2, mod, w_out_bf, g_ffn, w_router_bf, b_router, counts_in, x1_buf, seq):
    n = x2.shape[0]
    tiles = seq // TOK_TILE
    row = lambda i: (b * tiles + i, 0)
    local = lambda i: (i, 0)
    by_lane = lambda i: (0, i)
    fixed = lambda i: (0, 0)
    return pl.pallas_call(
        _out_route_kernel,
        grid=(tiles,),
        in_specs=[
            pl.BlockSpec((oa.shape[0], TOK_TILE, LANES), lambda i: (0, i, 0)),
            pl.BlockSpec((ob.shape[0], TOK_TILE, LANES), lambda i: (0, i, 0)),
            pl.BlockSpec((TOK_TILE, D_MODEL), row),
            pl.BlockSpec((None, SUBLANES, D_MODEL), lambda i: (b, 0, 0)),
            pl.BlockSpec((D_MODEL, D_MODEL), fixed),
            pl.BlockSpec((1, D_MODEL), fixed),
            pl.BlockSpec((N_EXPERTS, D_MODEL), fixed),
            pl.BlockSpec((N_EXPERTS, 1), fixed),
            pl.BlockSpec((N_EXPERTS, 1), fixed),
            pl.BlockSpec(memory_space=pl.ANY),
        ],
        out_specs=[
            pl.BlockSpec((TOK_TILE, D_MODEL), row),
            pl.BlockSpec((TOK_TILE, D_MODEL // 2), local),
            pl.BlockSpec((SUBLANES, TOK_TILE), by_lane),
            pl.BlockSpec((SUBLANES, TOK_TILE), by_lane),
            pl.BlockSpec((SUBLANES, TOK_TILE), by_lane),
            pl.BlockSpec((N_EXPERTS, 1), fixed),
        ],
        out_shape=[
            jax.ShapeDtypeStruct((n, D_MODEL), F32),
            jax.ShapeDtypeStruct((seq, D_MODEL // 2), jnp.int32),
            jax.ShapeDtypeStruct((SUBLANES, seq), jnp.int32),
            jax.ShapeDtypeStruct((SUBLANES, seq), F32),
            jax.ShapeDtypeStruct((SUBLANES, seq), jnp.int32),
            jax.ShapeDtypeStruct((N_EXPERTS, 1), F32),
        ],
        input_output_aliases={9: 0},
        scratch_shapes=[pltpu.VMEM((TOK_TILE, TOK_TILE), BF16), pltpu.VMEM((N_EXPERTS, 1), F32)],
        compiler_params=_cparams(1, VMEM_LIMIT),
        name="out_route",
    )(oa, ob, x2, mod, w_out_bf, g_ffn.reshape(1, D_MODEL), w_router_bf.T,
      b_router.reshape(N_EXPERTS, 1), counts_in, x1_buf)


def _experts_kernel(tiles_ref, rows_ref, base_ref, next_ref, xs_ref, wgu_ref, bgu_ref, wd_ref,
                    bd_ref, y_ref, wgu_sc, wd_sc, x_buf, y_buf, x_sem, y_sem):
    e = pl.program_id(0)
    n_tiles = tiles_ref[e]
    n_rows = rows_ref[e]
    n_live = base_ref[N_EXPERTS]
    region = e * EXPERT_ROWS

    def x_copy(row, slot):
        return pltpu.make_async_copy(xs_ref.at[pl.ds(row, ROW_TILE), :], x_buf.at[slot],
                                     x_sem.at[slot])

    def y_copy(row, slot):
        return pltpu.make_async_copy(y_buf.at[slot], y_ref.at[pl.ds(row, ROW_TILE), :],
                                     y_sem.at[slot])

    @pl.when(e == 0)
    def _():
        y_buf[...] = jnp.zeros(y_buf.shape, jnp.int32)

        @pl.when(n_live > 0)
        def _():
            x_copy(pl.multiple_of(next_ref[N_EXPERTS], ROW_TILE), 0).start()

    @pl.when(n_tiles > 0)
    def _():
        wgu_sc[...] = wgu_ref[...].astype(BF16)
        wd_sc[...] = wd_ref[...].astype(BF16)

    def tile(j, carry):
        g = base_ref[e] + j
        slot = g % 2
        row = pl.multiple_of(region + j * ROW_TILE, ROW_TILE)
        x_copy(row, slot).wait()
        next_row = jnp.where(j + 1 < n_tiles, row + ROW_TILE, next_ref[e])

        @pl.when(g + 1 < n_live)
        def _():
            x_copy(pl.multiple_of(next_row, ROW_TILE), 1 - slot).start()

        @pl.when(g >= 2)
        def _():
            y_copy(0, slot).wait()

        needed = jnp.minimum(n_rows - j * ROW_TILE, ROW_TILE)
        for rows in range(ROW_STEP, ROW_TILE + 1, ROW_STEP):
            @pl.when(jnp.logical_and(needed > rows - ROW_STEP, needed <= rows))
            def _(rows=rows):
                used = lax.broadcasted_iota(jnp.int32, (rows, 1), 0) < needed
                first, second = _unpack_bf16_pairs(jnp.where(used, x_buf[slot, :rows, :], 0))
                x = jnp.concatenate([first.astype(BF16), second.astype(BF16)], axis=1)
                acc = jnp.zeros((rows, D_MODEL), F32)
                for c in range(D_EXPERT // FF_CHUNK):
                    lo_c, hi_c = c * FF_CHUNK, (c + 1) * FF_CHUNK
                    gate = (jnp.dot(x, wgu_sc[:, lo_c:hi_c], preferred_element_type=F32)
                            + bgu_ref[:, lo_c:hi_c])
                    up = (jnp.dot(x, wgu_sc[:, D_EXPERT + lo_c:D_EXPERT + hi_c],
                                  preferred_element_type=F32)
                          + bgu_ref[:, D_EXPERT + lo_c:D_EXPERT + hi_c])
                    gate = jnp.minimum(gate, SWIGLU_LIMIT)
                    up = jnp.clip(up, -SWIGLU_LIMIT, SWIGLU_LIMIT)
                    act = (up + 1.0) * (gate * jax.nn.sigmoid(SWIGLU_ALPHA * gate))
                    acc = acc + jnp.dot(act.astype(BF16), wd_sc[lo_c:hi_c, :],
                                        preferred_element_type=F32)
                y_buf[slot, :rows, :] = _pack_bf16_pairs(acc + bd_ref[...])

        y_copy(row, slot).start()
        return carry

    lax.fori_loop(0, n_tiles, tile, 0)

    @pl.when(e == pl.num_programs(0) - 1)
    def _():
        for back in (2, 1):
            @pl.when(n_live >= back)
            def _():
                y_copy(0, (n_live - back) % 2).wait()


def _experts(counts, xs, w_gate_up, b_gate_up, w_down, b_down):
    n_rows, width = xs.shape
    tiles = (counts + ROW_TILE - 1) // ROW_TILE
    base = jnp.concatenate([jnp.zeros((1,), jnp.int32), jnp.cumsum(tiles)]).astype(jnp.int32)
    region = jnp.arange(N_EXPERTS, dtype=jnp.int32) * EXPERT_ROWS
    later = jnp.arange(N_EXPERTS)[None, :] > jnp.arange(-1, N_EXPERTS)[:, None]
    cand = jnp.where(jnp.logical_and(later, (tiles > 0)[None, :]), region[None, :], n_rows)
    nxt = jnp.min(cand, axis=1)
    next_row = jnp.concatenate([nxt[1:], nxt[:1]]).astype(jnp.int32)
    by_expert = lambda e, *_: (e, 0, 0)
    return pl.pallas_call(
        _experts_kernel,
        grid_spec=pltpu.PrefetchScalarGridSpec(
            num_scalar_prefetch=4,
            grid=(N_EXPERTS,),
            in_specs=[
                pl.BlockSpec(memory_space=pl.ANY),
                pl.BlockSpec((None, D_MODEL, 2 * D_EXPERT), by_expert),
                pl.BlockSpec((None, 1, 2 * D_EXPERT), by_expert),
                pl.BlockSpec((None, D_EXPERT, D_MODEL), by_expert),
                pl.BlockSpec((None, 1, D_MODEL), by_expert),
            ],
            out_specs=pl.BlockSpec(memory_space=pl.ANY),
            scratch_shapes=[
                pltpu.VMEM((D_MODEL, 2 * D_EXPERT), BF16),
                pltpu.VMEM((D_EXPERT, D_MODEL), BF16),
                pltpu.VMEM((2, ROW_TILE, width), jnp.int32),
                pltpu.VMEM((2, ROW_TILE, D_MODEL // 2), jnp.int32),
                pltpu.SemaphoreType.DMA((2,)),
                pltpu.SemaphoreType.DMA((2,)),
            ],
        ),
        out_shape=jax.ShapeDtypeStruct((n_rows, D_MODEL // 2), jnp.int32),
        compiler_params=_cparams(1, VMEM_LIMIT),
        name="experts",
    )(tiles.astype(jnp.int32), counts.astype(jnp.int32), base, next_row, xs, w_gate_up,
      b_gate_up.reshape(N_EXPERTS, 1, 2 * D_EXPERT), w_down, b_down.reshape(N_EXPERTS, 1, D_MODEL))


def _sc_gather_rows(table, idx):
    m = idx.shape[0]
    width = table.shape[1]
    mesh = plsc.VectorSubcoreMesh(core_axis_name="core", subcore_axis_name="subcore")
    n_workers = mesh.num_cores * mesh.num_subcores
    per_worker = m // n_workers
    assert per_worker * n_workers == m and per_worker % SC_WINDOW == 0

    @pl.kernel(
        out_type=jax.ShapeDtypeStruct((m, width), table.dtype),
        mesh=mesh,
        scratch_types=[
            pltpu.VMEM((SC_WINDOW,), jnp.int32),
            pltpu.VMEM((SC_WINDOW, width), table.dtype),
            pltpu.SemaphoreType.DMA,
        ],
    )
    def gather_kernel(table_hbm, idx_hbm, out_hbm, idx_v, rows_v, sem):
        worker = lax.axis_index("subcore") * mesh.num_cores + lax.axis_index("core")

        @pl.loop(0, per_worker // SC_WINDOW)
        def _(j):
            base = pl.multiple_of(worker * per_worker + j * SC_WINDOW, SC_WINDOW)
            pltpu.sync_copy(idx_hbm.at[pl.ds(base, SC_WINDOW)], idx_v)
            pltpu.async_copy(table_hbm.at[idx_v], rows_v, sem).wait()
            pltpu.sync_copy(rows_v, out_hbm.at[pl.ds(base, SC_WINDOW)])

    return gather_kernel(table, idx)


def _sc_scatter_rows_into(table, dst, out_ref, reps):
    m, width = table.shape
    mesh = plsc.VectorSubcoreMesh(core_axis_name="core", subcore_axis_name="subcore")
    n_workers = mesh.num_cores * mesh.num_subcores
    per_worker = m // n_workers
    assert per_worker * n_workers == m and per_worker % SC_WINDOW == 0 and dst.shape == (reps * m,)

    @pl.kernel(
        out_type=(),
        mesh=mesh,
        scratch_types=[
            pltpu.VMEM((SC_WINDOW,), jnp.int32),
            pltpu.VMEM((SC_WINDOW, width), table.dtype),
            pltpu.SemaphoreType.DMA,
        ],
    )
    def scatter_kernel(table_hbm, dst_hbm, out_hbm, dst_v, rows_v, sem):
        worker = lax.axis_index("subcore") * mesh.num_cores + lax.axis_index("core")

        @pl.loop(0, per_worker // SC_WINDOW)
        def _(j):
            base = pl.multiple_of(worker * per_worker + j * SC_WINDOW, SC_WINDOW)
            pltpu.sync_copy(table_hbm.at[pl.ds(base, SC_WINDOW)], rows_v)
            for k in range(reps):
                pltpu.sync_copy(dst_hbm.at[pl.ds(k * m + base, SC_WINDOW)], dst_v)
                pltpu.async_copy(rows_v, out_hbm.at[dst_v], sem).wait()

    scatter_kernel(table, dst, out_ref)


def _combine_dense_kernel(tw_ref, x1_ref, mod_ref, g_ref, y0_ref, y1_ref, y2_ref, y3_ref, o_in_ref,
                          o_ref):
    del o_in_ref
    tm = x1_ref.shape[0]
    tw = jnp.concatenate([tw_ref[...], jnp.zeros((LANES - SUBLANES, tm), F32)], axis=0).T
    first = second = None
    for kk, y_ref in enumerate((y0_ref, y1_ref, y2_ref, y3_ref)):
        f_k, s_k = _unpack_bf16_pairs(y_ref[...])
        w_k = tw[:, kk:kk + 1]
        first = w_k * f_k if kk == 0 else first + w_k * f_k
        second = w_k * s_k if kk == 0 else second + w_k * s_k
    ffn = jnp.concatenate([first, second], axis=1)
    x2 = x1_ref[...] + mod_ref[5:6, :] * ffn
    o_ref[...] = x2 * lax.rsqrt(jnp.mean(x2 * x2, axis=-1, keepdims=True) + EPS) * g_ref[...]


def _combine_dense(b, tw, x1, mod, g_final, yg, out_buf, seq):
    n = x1.shape[0]
    width = yg.shape[1]
    tiles = seq // TOK_TILE
    row = lambda i: (b * tiles + i, 0)
    slot = lambda kk: pl.BlockSpec((TOK_TILE, width), lambda i: (kk * tiles + i, 0))
    return pl.pallas_call(
        _combine_dense_kernel,
        grid=(tiles,),
        in_specs=[
            pl.BlockSpec((SUBLANES, TOK_TILE), lambda i: (0, i)),
            pl.BlockSpec((TOK_TILE, D_MODEL), row),
            pl.BlockSpec((None, SUBLANES, D_MODEL), lambda i: (b, 0, 0)),
            pl.BlockSpec((1, D_MODEL), lambda i: (0, 0)),
        ] + [slot(kk) for kk in range(TOP_K)] + [pl.BlockSpec(memory_space=pl.ANY)],
        out_specs=pl.BlockSpec((TOK_TILE, D_MODEL), row),
        out_shape=jax.ShapeDtypeStruct((n, D_MODEL), F32),
        input_output_aliases={4 + TOP_K: 0},
        compiler_params=_cparams(1, VMEM_LIMIT),
        name="combine",
    )(tw, x1, mod, g_final.reshape(1, D_MODEL), yg, yg, yg, yg, out_buf)


def _rope_base(positions):
    half = ROPE_DIM // 2
    inv_freq = ROPE_THETA ** (-jnp.arange(0, ROPE_DIM, 2, dtype=F32) / ROPE_DIM)
    n = positions.size
    pos = jnp.broadcast_to(positions.reshape(n, 1).astype(F32), (n, half)).reshape(-1, LANES)
    ang = pos * jnp.tile(inv_freq, LANES // half)
    cos, sin = lax.optimization_barrier((jnp.cos(ang), jnp.sin(ang)))
    pad = jnp.zeros((n, LANES - ROPE_DIM), F32)
    return jnp.concatenate([cos.reshape(n, half), sin.reshape(n, half), pad], axis=1)


def kernel(x, c, positions, w_ada, b_ada, g_mix_norm, w_in, rel_bias, lambda_q1, lambda_k1,
           lambda_q2, lambda_k2, g_subln, w_out, g_ffn_norm, w_router, b_router, w_gate_up,
           b_gate_up, w_down, b_down, g_final):
    batch, seq, _ = x.shape
    depth = w_ada.shape[0]
    assert depth == 1, "the combine kernel applies the final norm, so it must follow the only layer"
    n = batch * seq
    rope_base = _rope_base(positions)
    x2 = x.reshape(n, D_MODEL)
    for l in range(depth):
        lambda_init = 0.8 - 0.6 * math.exp(-0.3 * l)
        mod = _ada(c, w_ada[l], b_ada[l])
        qa, ka, va, qb, kb, vb = _in_proj(x2, mod, g_mix_norm[l], w_in[l].astype(BF16),
                                          rope_base, seq)
        bias_rows = _rel_bias_rows(rel_bias[l])
        lam = (jnp.exp(jnp.sum(lambda_q1[l].astype(F32) * lambda_k1[l].astype(F32)))
               - jnp.exp(jnp.sum(lambda_q2[l].astype(F32) * lambda_k2[l].astype(F32)))
               + lambda_init).reshape(1)
        w_out_bf = w_out[l].astype(BF16)
        w_router_bf = w_router[l].astype(BF16)
        xs_ref = jax.new_ref(lax.empty((N_EXPERTS * EXPERT_ROWS, D_MODEL // 2), jnp.int32))
        x1 = lax.empty((n, D_MODEL), F32)
        counts = jnp.zeros((N_EXPERTS, 1), F32)
        pos_rows, weight_rows = [], []
        for b in range(batch):
            oa = _attn_a(b, qa, ka, va, bias_rows, seq)
            ob = _attn_b(b, lam, qb, kb, vb, g_subln[l], seq, 1.0 - lambda_init)
            x1, hp, top_i, top_w, rank, counts = _out_route(
                b, oa, ob, x2, mod, w_out_bf, g_ffn_norm[l], w_router_bf, b_router[l], counts, x1,
                seq)
            pos = top_i * EXPERT_ROWS + rank
            pos_rows.append(pos)
            weight_rows.append(top_w)
            dst = jnp.concatenate([pos[kk] for kk in range(TOP_K)])
            _sc_scatter_rows_into(hp, dst, xs_ref, TOP_K)
        xs = jax.freeze(xs_ref)
        y = _experts(counts[:, 0].astype(jnp.int32), xs, w_gate_up[l], b_gate_up[l], w_down[l],
                     b_down[l])
        x2 = lax.empty((n, D_MODEL), F32)
        for b in range(batch):
            yg = _sc_gather_rows(y, jnp.concatenate([pos_rows[b][kk] for kk in range(TOP_K)]))
            x2 = _combine_dense(b, weight_rows[b], x1, mod, g_final, yg, x2, seq)
    return x2.reshape(batch, seq, D_MODEL)
```

```python
import functools
import math

import jax
import jax.numpy as jnp
from jax import lax
from jax.experimental import pallas as pl
from jax.experimental.pallas import tpu as pltpu
from jax.experimental.pallas import tpu_sc as plsc

D_MODEL = 1024
CHUNK = 64
HEAD_DIM = 64
A_HEADS = 8
A_WIDTH = A_HEADS * HEAD_DIM
LEFT_CHUNKS = 8
MAX_REL = 128
B_HEADS = 4
B_QK_DIM = HEAD_DIM
B_V_DIM = 2 * HEAD_DIM
B_WIDTH = B_HEADS * B_V_DIM
ROPE_THETA = 500000.0
ROPE_DIM = B_QK_DIM // 4
N_EXPERTS = 32
TOP_K = 4
D_EXPERT = D_MODEL
SWIGLU_LIMIT = 7.0
SWIGLU_ALPHA = 1.702
EPS = 1e-6
NEG_INF = -1e30
LOG2_E = math.log2(math.e)
N_MOD = 6

LANES = 128
SUBLANES = 8
VMEM_BYTES_V7X = 64 * 1024 * 1024
VMEM_LIMIT = VMEM_BYTES_V7X * 7 // 8

TOK_TILE = 512
A_QBLK = 2 * CHUNK
A_BAND = (LEFT_CHUNKS + 2) * CHUNK
A_ROLL = A_BAND + A_QBLK
B_TQ = 512
B_TK = 512
B_SUB = 128
ROW_TILE = 512
ROW_STEP = 128
FF_CHUNK = 512
EXPERT_ROWS = 16384
SC_WINDOW = 128

F32 = jnp.float32
BF16 = jnp.bfloat16


def _cparams(n_axes, vmem=None):
    return pltpu.CompilerParams(
        dimension_semantics=("arbitrary",) * n_axes,
        vmem_limit_bytes=vmem,
    )


def _ada_kernel(c_ref, w_ref, b_ref, o_ref):
    c = c_ref[...]
    act = c * jax.nn.sigmoid(c)
    o_ref[...] = jnp.dot(act, w_ref[...], preferred_element_type=F32,
                         precision=lax.Precision.HIGHEST) + b_ref[...]


def _ada(c, w_ada, b_ada):
    b = c.shape[0]
    rows = -(-b // SUBLANES) * SUBLANES
    c_pad = jnp.pad(c, ((0, rows - b), (0, 0)))
    n_out = w_ada.shape[1]
    out = pl.pallas_call(
        _ada_kernel,
        grid=(n_out // D_MODEL,),
        in_specs=[
            pl.BlockSpec((rows, D_MODEL), lambda j: (0, 0)),
            pl.BlockSpec((D_MODEL, D_MODEL), lambda j: (0, j)),
            pl.BlockSpec((1, D_MODEL), lambda j: (0, j)),
        ],
        out_specs=pl.BlockSpec((rows, D_MODEL), lambda j: (0, j)),
        out_shape=jax.ShapeDtypeStruct((rows, n_out), F32),
        compiler_params=_cparams(1),
        name="ada",
    )(c_pad, w_ada, b_ada.reshape(1, n_out))
    mod = out[:b].reshape(b, N_MOD, D_MODEL)
    return jnp.pad(mod, ((0, 0), (0, SUBLANES - N_MOD), (0, 0)))


def _in_proj_kernel(x_ref, mod_ref, g_ref, w_ref, rb_ref,
                    qa_ref, ka_ref, va_ref, qb_ref, kb_ref, vb_ref):
    x = x_ref[...]
    mod = mod_ref[...]
    y = x * lax.rsqrt(jnp.mean(x * x, axis=-1, keepdims=True) + EPS) * g_ref[...]
    h = (y * (1.0 + mod[1:2, :]) + mod[0:1, :]).astype(BF16)
    q_scale = HEAD_DIM ** -0.5

    half = ROPE_DIM // 2
    rb = rb_ref[...]
    lane = lax.broadcasted_iota(jnp.int32, rb.shape, 1)
    cos_lo = jnp.where(lane < half, rb, 0.0)
    sin_hi = jnp.where(jnp.logical_and(lane >= half, lane < ROPE_DIM), rb, 0.0)
    cos_pair = cos_lo + pltpu.roll(cos_lo, half, 1)
    rc = (cos_pair + pltpu.roll(cos_pair, B_QK_DIM, 1)
          + jnp.where(lane % B_QK_DIM >= ROPE_DIM, 1.0, 0.0))
    sin_lo = pltpu.roll(sin_hi, LANES - half, 1)
    rm = -(sin_lo + pltpu.roll(sin_lo, B_QK_DIM, 1))
    rp = sin_hi + pltpu.roll(sin_hi, B_QK_DIM, 1)

    def rope(p):
        cols = []
        for s in range(p.shape[1] // LANES):
            v = p[:, s * LANES:(s + 1) * LANES]
            cols.append(v * rc + pltpu.roll(v, LANES - ROPE_DIM // 2, 1) * rm
                        + pltpu.roll(v, ROPE_DIM // 2, 1) * rp)
        return jnp.concatenate(cols, axis=1)

    outs = (qa_ref, ka_ref, va_ref, qb_ref, kb_ref, vb_ref)
    for j, o_ref in enumerate(outs):
        p = jnp.dot(h, w_ref[:, j * A_WIDTH:(j + 1) * A_WIDTH], preferred_element_type=F32)
        if j in (3, 4):
            p = rope(p)
        if j in (0, 3):
            p = p * (q_scale * LOG2_E)
        pb = p.astype(BF16)
        for s in range(A_WIDTH // LANES):
            o_ref[s] = pb[:, s * LANES:(s + 1) * LANES]


def _in_proj(x2, mod, g_mix, w_in_bf, rope_base, seq):
    n = x2.shape[0]
    tiles_per_seq = seq // TOK_TILE
    row = lambda i: (i, 0)
    fixed = lambda i: (0, 0)
    n_slabs = A_WIDTH // LANES
    out_sd = jax.ShapeDtypeStruct((n_slabs, n, LANES), BF16)
    return pl.pallas_call(
        _in_proj_kernel,
        grid=(n // TOK_TILE,),
        in_specs=[
            pl.BlockSpec((TOK_TILE, D_MODEL), row),
            pl.BlockSpec((None, SUBLANES, D_MODEL), lambda i: (i // tiles_per_seq, 0, 0)),
            pl.BlockSpec((1, D_MODEL), fixed),
            pl.BlockSpec(w_in_bf.shape, fixed),
            pl.BlockSpec((TOK_TILE, LANES), row),
        ],
        out_specs=[pl.BlockSpec((n_slabs, TOK_TILE, LANES), lambda i: (0, i, 0))] * 6,
        out_shape=[out_sd] * 6,
        compiler_params=_cparams(1, VMEM_LIMIT),
        name="in_proj",
    )(x2, mod, g_mix.reshape(1, D_MODEL), w_in_bf, rope_base)


def _attn_a_kernel(q_ref, kp_ref, kc_ref, vp_ref, vc_ref, bias_ref, o_ref, k_sc, v_sc, bias_sc):
    g = pl.program_id(0)
    n_pairs, blk, _ = q_ref.shape
    lane = lax.broadcasted_iota(jnp.int32, (A_QBLK, LANES), 1)
    col = lax.broadcasted_iota(jnp.int32, (A_QBLK, A_BAND), 1)
    ones = jnp.ones((A_BAND, LANES), BF16)

    @pl.when(g == 0)
    def _():
        q_chunk = lax.broadcasted_iota(jnp.int32, (A_QBLK, A_BAND), 0) // CHUNK
        k_chunk = col // CHUNK
        in_band = jnp.logical_and(k_chunk >= q_chunk, k_chunk <= q_chunk + LEFT_CHUNKS)
        for h in range(bias_ref.shape[0]):
            rolled = pltpu.roll(jnp.broadcast_to(bias_ref[h], (A_QBLK, A_ROLL)), 0, 1,
                                stride=1, stride_axis=0)
            bias_sc[h] = jnp.where(in_band, rolled[:, :A_BAND], NEG_INF)

    def pair(p):
        k_buf = k_sc.at[p % 2]
        v_buf = v_sc.at[p % 2]
        k_buf[0:blk, :] = kp_ref[p]
        k_buf[blk:2 * blk, :] = kc_ref[p]
        v_buf[0:blk, :] = vp_ref[p]
        v_buf[blk:2 * blk, :] = vc_ref[p]

        def scores(m, hh):
            r0 = m * A_QBLK
            q = q_ref[p, r0:r0 + A_QBLK, :]
            in_head = (lane < HEAD_DIM) if hh == 0 else (lane >= HEAD_DIM)
            qh = jnp.where(in_head, q, jnp.zeros_like(q))
            return lax.dot_general(qh, k_buf[r0:r0 + A_BAND, :], (((1,), (1,)), ((), ())),
                                   preferred_element_type=F32)

        chains = [(m, hh) for m in range(blk // A_QBLK) for hh in range(2)]
        s_next = scores(*chains[0])
        halves = []
        for i, (m, hh) in enumerate(chains):
            r0 = m * A_QBLK
            s = s_next
            if i + 1 < len(chains):
                s_next = scores(*chains[i + 1])
            valid = jnp.logical_or(g > 0, col + r0 >= blk)
            s = jnp.where(valid, s + bias_sc[2 * p + hh], NEG_INF)
            pr = jnp.exp2(s - jnp.max(s, axis=1, keepdims=True))
            v_ext = jnp.concatenate([v_buf[r0:r0 + A_BAND, :], ones], axis=1)
            pv = jnp.dot(pr.astype(BF16), v_ext, preferred_element_type=F32)
            halves.append(pv[:, :LANES] / pv[:, LANES:])
            if hh == 1:
                o_ref[p, r0:r0 + A_QBLK, :] = jnp.where(lane < HEAD_DIM, halves[0],
                                                        halves[1]).astype(BF16)
                halves = []

    for p in range(n_pairs):
        pair(p)


def _attn_a(b, qa, ka, va, bias_rows, seq):
    n_pairs = qa.shape[0]
    blk = LEFT_CHUNKS * CHUNK
    nblk = seq // blk
    cur = lambda g: (0, b * nblk + g, 0)
    prev = lambda g: (0, b * nblk + jnp.maximum(g - 1, 0), 0)
    slab = (n_pairs, blk, LANES)
    return pl.pallas_call(
        _attn_a_kernel,
        grid=(nblk,),
        in_specs=[
            pl.BlockSpec(slab, cur),
            pl.BlockSpec(slab, prev),
            pl.BlockSpec(slab, cur),
            pl.BlockSpec(slab, prev),
            pl.BlockSpec(slab, cur),
            pl.BlockSpec(bias_rows.shape, lambda g: (0, 0, 0)),
        ],
        out_specs=pl.BlockSpec(slab, lambda g: (0, g, 0)),
        out_shape=jax.ShapeDtypeStruct((n_pairs, seq, LANES), BF16),
        scratch_shapes=[
            pltpu.VMEM((2, 2 * blk, LANES), BF16),
            pltpu.VMEM((2, 2 * blk, LANES), BF16),
            pltpu.VMEM((bias_rows.shape[0], A_QBLK, A_BAND), F32),
        ],
        compiler_params=_cparams(1),
        name="attn_a",
    )(qa, ka, ka, va, va, bias_rows)


def _rel_bias_rows(rel_table):
    t = rel_table.astype(F32) * LOG2_E
    far = t[:, 2 * MAX_REL:]
    n_far = LEFT_CHUNKS * CHUNK - MAX_REL
    row = jnp.concatenate([
        jnp.broadcast_to(far, (t.shape[0], n_far)),
        t[:, 2 * MAX_REL:0:-1],
        jnp.broadcast_to(far, (t.shape[0], A_ROLL - A_BAND)),
    ], axis=1)
    return row.reshape(t.shape[0], 1, A_ROLL)


def _attn_b_kernel(lam_ref, q_ref, k_ref, v_ref, g_ref, o_ref, q_sc, qn_sc, s_sc, m_sc, acc_sc, *,
                   out_scale):
    n_tiles = q_ref.shape[0] // B_TQ
    n_sub = B_TQ // B_SUB
    first_slot = 2
    lane = lax.broadcasted_iota(jnp.int32, (B_TQ, LANES), 1)
    ones = jnp.ones((B_TK, LANES), BF16)
    lam = lam_ref[0]

    def split_components(t, dst):
        q = q_ref[pl.ds(pl.multiple_of(t * B_TQ, B_TQ), B_TQ), :]
        dst[0] = jnp.where(lane < B_QK_DIM, q, jnp.zeros_like(q))
        dst[1] = jnp.where(lane >= B_QK_DIM, q, jnp.zeros_like(q))

    def score_rows(src, k, slot, sub, c):
        rows = pl.ds(sub * B_SUB, B_SUB)
        s_sc[slot, c, rows, :] = lax.dot_general(src[c, rows, :], k, (((1,), (1,)), ((), ())),
                                                 preferred_element_type=F32)

    def update_rows(s, v_ext, sub, c):
        rows = pl.ds(sub * B_SUB, B_SUB)
        m_prev = m_sc[c, rows, :]
        m_new = jnp.maximum(m_prev, jnp.max(s, axis=1, keepdims=True))
        alpha = jnp.exp2(m_prev - m_new)
        p = jnp.exp2(s - jnp.concatenate([m_new] * (s.shape[1] // LANES), axis=1))
        pv = jnp.dot(p.astype(BF16), v_ext, preferred_element_type=F32)
        acc_sc[c, rows, :] = jnp.concatenate([alpha, alpha], axis=1) * acc_sc[c, rows, :] + pv
        m_sc[c, rows, :] = m_new

    def key_block(blk):
        return k_ref[pl.ds(pl.multiple_of(blk * B_TK, B_TK), B_TK), :]

    def value_block(blk):
        return jnp.concatenate([v_ref[pl.ds(pl.multiple_of(blk * B_TK, B_TK), B_TK), :], ones], axis=1)

    def step(blk, slot, next_blk, next_slot):
        v_ext = value_block(blk)
        k_next = key_block(next_blk)
        for sub in range(n_sub):
            for c in range(2):
                update_rows(s_sc[slot, c, pl.ds(sub * B_SUB, B_SUB), :], v_ext, sub, c)
                score_rows(q_sc, k_next, next_slot, sub, c)

    def last_step(t, slot):
        v_ext = value_block(t)
        k0 = key_block(0)
        col_c = lax.broadcasted_iota(jnp.int32, (B_SUB, B_TK), 1) // CHUNK
        row_c = lax.broadcasted_iota(jnp.int32, (B_SUB, B_TK), 0) // CHUNK
        for sub in range(n_sub):
            keep = col_c <= row_c + sub * (B_SUB // CHUNK)
            for c in range(2):
                s = s_sc[slot, c, pl.ds(sub * B_SUB, B_SUB), :]
                update_rows(jnp.where(keep, s, NEG_INF), v_ext, sub, c)
                score_rows(qn_sc, k0, first_slot, sub, c)

    def start_tile(t):
        split_components(t, q_sc)
        split_components(jnp.minimum(t + 1, n_tiles - 1), qn_sc)
        m_sc[...] = jnp.full(m_sc.shape, NEG_INF, F32)
        acc_sc[...] = jnp.zeros(acc_sc.shape, F32)

    def finish_tile(t):
        a0 = acc_sc[0]
        a1 = acc_sc[1]
        o = a0[:, :B_V_DIM] / a0[:, B_V_DIM:] - lam * (a1[:, :B_V_DIM] / a1[:, B_V_DIM:])
        y = o * lax.rsqrt(jnp.mean(o * o, axis=-1, keepdims=True) + EPS) * g_ref[...]
        o_ref[pl.ds(pl.multiple_of(t * B_TQ, B_TQ), B_TQ), :] = (y * out_scale).astype(BF16)

    start_tile(0)
    k0 = key_block(0)
    for sub in range(n_sub):
        for c in range(2):
            score_rows(q_sc, k0, first_slot, sub, c)
    s_sc[0] = s_sc[first_slot]
    last_step(0, 0)
    finish_tile(0)

    def tile(t, carry):
        start_tile(t)
        step(0, first_slot, 1, 1)

        def pair(p, carry):
            odd = 2 * p + 1
            step(odd, 1, odd + 1, 0)
            step(odd + 1, 0, odd + 2, 1)
            return carry

        lax.fori_loop(0, (t - 1) // 2, pair, 0)

        @pl.when(t % 2 == 1)
        def _():
            last_step(t, 1)

        @pl.when(t % 2 == 0)
        def _():
            step(t - 1, 1, t, 0)
            last_step(t, 0)

        finish_tile(t)
        return carry

    lax.fori_loop(1, n_tiles, tile, 0)


def _attn_b(b, lam, qb, kb, vb, g_subln, seq, out_scale):
    assert B_TQ == B_TK and B_V_DIM == LANES
    by_head = lambda h: (h, b, 0)
    return pl.pallas_call(
        functools.partial(_attn_b_kernel, out_scale=out_scale),
        grid=(B_HEADS,),
        in_specs=[
            pl.BlockSpec(memory_space=pltpu.SMEM),
            pl.BlockSpec((None, seq, LANES), by_head),
            pl.BlockSpec((None, seq, LANES), by_head),
            pl.BlockSpec((None, seq, LANES), by_head),
            pl.BlockSpec((1, B_V_DIM), lambda h: (0, 0)),
        ],
        out_specs=pl.BlockSpec((None, seq, LANES), lambda h: (h, 0, 0)),
        out_shape=jax.ShapeDtypeStruct((B_HEADS, seq, LANES), BF16),
        scratch_shapes=[
            pltpu.VMEM((2, B_TQ, LANES), BF16),
            pltpu.VMEM((2, B_TQ, LANES), BF16),
            pltpu.VMEM((3, 2, B_TQ, B_TK), F32),
            pltpu.VMEM((2, B_TQ, LANES), F32),
            pltpu.VMEM((2, B_TQ, 2 * LANES), F32),
        ],
        compiler_params=_cparams(1),
        name="attn_b",
    )(lam, qb, kb, vb, g_subln.reshape(1, B_V_DIM))


def _pack_bf16_pairs(v):
    half = v.shape[1] // 2
    vb = v.astype(BF16)
    hi = lax.bitcast_convert_type(vb[:, :half].astype(F32), jnp.int32)
    lo = lax.bitcast_convert_type(vb[:, half:].astype(F32), jnp.int32)
    return hi | lax.shift_right_logical(lo, jnp.full(lo.shape, 16, jnp.int32))


def _unpack_bf16_pairs(w):
    first = lax.bitcast_convert_type(w & jnp.int32(-65536), F32)
    second = lax.bitcast_convert_type(lax.shift_left(w, jnp.full(w.shape, 16, jnp.int32)), F32)
    return first, second


def _out_route_kernel(oa_ref, ob_ref, x_ref, mod_ref, wo_ref, g_ref, wr_ref, br_ref, cin_ref,
                      x1_in_ref, x1_ref, hp_ref, ti_ref, tw_ref, rk_ref, cnt_ref, tri_sc, carry_sc):
    del x1_in_ref
    i = pl.program_id(0)
    tm = x_ref.shape[0]

    @pl.when(i == 0)
    def _():
        r = lax.broadcasted_iota(jnp.int32, (tm, tm), 0)
        c = lax.broadcasted_iota(jnp.int32, (tm, tm), 1)
        tri_sc[...] = jnp.where(r < c, 1.0, 0.0).astype(BF16)
        carry_sc[...] = cin_ref[...]

    mod = mod_ref[...]
    o = jnp.concatenate([oa_ref[s] for s in range(oa_ref.shape[0])]
                        + [ob_ref[s] for s in range(ob_ref.shape[0])], axis=1)
    mix = jnp.dot(o, wo_ref[...], preferred_element_type=F32)
    x1 = x_ref[...] + mod[2:3, :] * mix
    x1_ref[...] = x1
    y = x1 * lax.rsqrt(jnp.mean(x1 * x1, axis=-1, keepdims=True) + EPS) * g_ref[...]
    h = y * (1.0 + mod[4:5, :]) + mod[3:4, :]
    hb = h.astype(BF16)
    hp_ref[...] = _pack_bf16_pairs(h)

    logits = lax.dot_general(wr_ref[...], hb, (((1,), (1,)), ((), ())),
                             preferred_element_type=F32) + br_ref[...]
    eid = lax.broadcasted_iota(jnp.int32, logits.shape, 0).astype(F32)
    work = logits
    vals, ids = [], []
    chosen = jnp.zeros(logits.shape, F32)
    for _ in range(TOP_K):
        v = jnp.max(work, axis=0, keepdims=True)
        e = jnp.min(jnp.where(work == v, eid, float(N_EXPERTS)), axis=0, keepdims=True)
        hit = eid == e
        vals.append(v)
        ids.append(e)
        chosen = jnp.where(hit, 1.0, chosen)
        work = jnp.where(hit, -jnp.inf, work)
    ex = [jnp.exp(v - vals[0]) for v in vals]
    den = ex[0] + ex[1] + ex[2] + ex[3]

    before = jnp.dot(chosen.astype(BF16), tri_sc[...], preferred_element_type=F32) + carry_sc[...]
    slot = lax.broadcasted_iota(jnp.int32, (SUBLANES, tm), 0)
    ti = jnp.zeros((SUBLANES, tm), F32)
    tw = jnp.zeros((SUBLANES, tm), F32)
    rk = jnp.zeros((SUBLANES, tm), F32)
    for kk in range(TOP_K):
        r_k = jnp.sum(jnp.where(eid == ids[kk], before, 0.0), axis=0, keepdims=True)
        ti = jnp.where(slot == kk, ids[kk], ti)
        tw = jnp.where(slot == kk, ex[kk] / den, tw)
        rk = jnp.where(slot == kk, r_k, rk)
    ti_ref[...] = ti.astype(jnp.int32)
    tw_ref[...] = tw
    rk_ref[...] = rk.astype(jnp.int32)
    carry = carry_sc[...] + jnp.sum(chosen, axis=1, keepdims=True)
    carry_sc[...] = carry
    cnt_ref[...] = carry


def _out_route(b, oa, ob, x2, mod, w_out_bf, g_ffn, w_router_bf, b_router, counts_in, x1_buf, seq):
    n = x2.shape[0]
    tiles = seq // TOK_TILE
    row = lambda i: (b * tiles + i, 0)
    local = lambda i: (i, 0)
    by_lane = lambda i: (0, i)
    fixed = lambda i: (0, 0)
    return pl.pallas_call(
        _out_route_kernel,
        grid=(tiles,),
        in_specs=[
            pl.BlockSpec((oa.shape[0], TOK_TILE, LANES), lambda i: (0, i, 0)),
            pl.BlockSpec((ob.shape[0], TOK_TILE, LANES), lambda i: (0, i, 0)),
            pl.BlockSpec((TOK_TILE, D_MODEL), row),
            pl.BlockSpec((None, SUBLANES, D_MODEL), lambda i: (b, 0, 0)),
            pl.BlockSpec((D_MODEL, D_MODEL), fixed),
            pl.BlockSpec((1, D_MODEL), fixed),
            pl.BlockSpec((N_EXPERTS, D_MODEL), fixed),
            pl.BlockSpec((N_EXPERTS, 1), fixed),
            pl.BlockSpec((N_EXPERTS, 1), fixed),
            pl.BlockSpec(memory_space=pl.ANY),
        ],
        out_specs=[
            pl.BlockSpec((TOK_TILE, D_MODEL), row),
            pl.BlockSpec((TOK_TILE, D_MODEL // 2), local),
            pl.BlockSpec((SUBLANES, TOK_TILE), by_lane),
            pl.BlockSpec((SUBLANES, TOK_TILE), by_lane),
            pl.BlockSpec((SUBLANES, TOK_TILE), by_lane),
            pl.BlockSpec((N_EXPERTS, 1), fixed),
        ],
        out_shape=[
            jax.ShapeDtypeStruct((n, D_MODEL), F32),
            jax.ShapeDtypeStruct((seq, D_MODEL // 2), jnp.int32),
            jax.ShapeDtypeStruct((SUBLANES, seq), jnp.int32),
            jax.ShapeDtypeStruct((SUBLANES, seq), F32),
            jax.ShapeDtypeStruct((SUBLANES, seq), jnp.int32),
            jax.ShapeDtypeStruct((N_EXPERTS, 1), F32),
        ],
        input_output_aliases={9: 0},
        scratch_shapes=[pltpu.VMEM((TOK_TILE, TOK_TILE), BF16), pltpu.VMEM((N_EXPERTS, 1), F32)],
        compiler_params=_cparams(1, VMEM_LIMIT),
        name="out_route",
    )(oa, ob, x2, mod, w_out_bf, g_ffn.reshape(1, D_MODEL), w_router_bf.T,
      b_router.reshape(N_EXPERTS, 1), counts_in, x1_buf)


def _experts_kernel(tiles_ref, rows_ref, base_ref, next_ref, xs_ref, wgu_ref, bgu_ref, wd_ref,
                    bd_ref, y_ref, wgu_sc, wd_sc, x_buf, y_buf, x_sem, y_sem):
    e = pl.program_id(0)
    n_tiles = tiles_ref[e]
    n_rows = rows_ref[e]
    n_live = base_ref[N_EXPERTS]
    region = e * EXPERT_ROWS

    def x_copy(row, slot):
        return pltpu.make_async_copy(xs_ref.at[pl.ds(row, ROW_TILE), :], x_buf.at[slot],
                                     x_sem.at[slot])

    def y_copy(row, slot):
        return pltpu.make_async_copy(y_buf.at[slot], y_ref.at[pl.ds(row, ROW_TILE), :],
                                     y_sem.at[slot])

    @pl.when(e == 0)
    def _():
        y_buf[...] = jnp.zeros(y_buf.shape, jnp.int32)

        @pl.when(n_live > 0)
        def _():
            x_copy(pl.multiple_of(next_ref[N_EXPERTS], ROW_TILE), 0).start()

    @pl.when(n_tiles > 0)
    def _():
        wgu_sc[...] = wgu_ref[...].astype(BF16)
        wd_sc[...] = wd_ref[...].astype(BF16)

    def tile(j, carry):
        g = base_ref[e] + j
        slot = g % 2
        row = pl.multiple_of(region + j * ROW_TILE, ROW_TILE)
        x_copy(row, slot).wait()
        next_row = jnp.where(j + 1 < n_tiles, row + ROW_TILE, next_ref[e])

        @pl.when(g + 1 < n_live)
        def _():
            x_copy(pl.multiple_of(next_row, ROW_TILE), 1 - slot).start()

        @pl.when(g >= 2)
        def _():
            y_copy(0, slot).wait()

        needed = jnp.minimum(n_rows - j * ROW_TILE, ROW_TILE)
        for rows in range(ROW_STEP, ROW_TILE + 1, ROW_STEP):
            @pl.when(jnp.logical_and(needed > rows - ROW_STEP, needed <= rows))
            def _(rows=rows):
                used = lax.broadcasted_iota(jnp.int32, (rows, 1), 0) < needed
                first, second = _unpack_bf16_pairs(jnp.where(used, x_buf[slot, :rows, :], 0))
                x = jnp.concatenate([first.astype(BF16), second.astype(BF16)], axis=1)
                acc = jnp.zeros((rows, D_MODEL), F32)
                for c in range(D_EXPERT // FF_CHUNK):
                    lo_c, hi_c = c * FF_CHUNK, (c + 1) * FF_CHUNK
                    gate = (jnp.dot(x, wgu_sc[:, lo_c:hi_c], preferred_element_type=F32)
                            + bgu_ref[:, lo_c:hi_c])
                    up = (jnp.dot(x, wgu_sc[:, D_EXPERT + lo_c:D_EXPERT + hi_c],
                                  preferred_element_type=F32)
                          + bgu_ref[:, D_EXPERT + lo_c:D_EXPERT + hi_c])
                    gate = jnp.minimum(gate, SWIGLU_LIMIT)
                    up = jnp.clip(up, -SWIGLU_LIMIT, SWIGLU_LIMIT)
                    act = (up + 1.0) * (gate * jax.nn.sigmoid(SWIGLU_ALPHA * gate))
                    acc = acc + jnp.dot(act.astype(BF16), wd_sc[lo_c:hi_c, :],
                                        preferred_element_type=F32)
                y_buf[slot, :rows, :] = _pack_bf16_pairs(acc + bd_ref[...])

        y_copy(row, slot).start()
        return carry

    lax.fori_loop(0, n_tiles, tile, 0)

    @pl.when(e == pl.num_programs(0) - 1)
    def _():
        for back in (2, 1):
            @pl.when(n_live >= back)
            def _():
                y_copy(0, (n_live - back) % 2).wait()


def _experts(counts, xs, w_gate_up, b_gate_up, w_down, b_down):
    n_rows, width = xs.shape
    tiles = (counts + ROW_TILE - 1) // ROW_TILE
    base = jnp.concatenate([jnp.zeros((1,), jnp.int32), jnp.cumsum(tiles)]).astype(jnp.int32)
    region = jnp.arange(N_EXPERTS, dtype=jnp.int32) * EXPERT_ROWS
    later = jnp.arange(N_EXPERTS)[None, :] > jnp.arange(-1, N_EXPERTS)[:, None]
    cand = jnp.where(jnp.logical_and(later, (tiles > 0)[None, :]), region[None, :], n_rows)
    nxt = jnp.min(cand, axis=1)
    next_row = jnp.concatenate([nxt[1:], nxt[:1]]).astype(jnp.int32)
    by_expert = lambda e, *_: (e, 0, 0)
    return pl.pallas_call(
        _experts_kernel,
        grid_spec=pltpu.PrefetchScalarGridSpec(
            num_scalar_prefetch=4,
            grid=(N_EXPERTS,),
            in_specs=[
                pl.BlockSpec(memory_space=pl.ANY),
                pl.BlockSpec((None, D_MODEL, 2 * D_EXPERT), by_expert),
                pl.BlockSpec((None, 1, 2 * D_EXPERT), by_expert),
                pl.BlockSpec((None, D_EXPERT, D_MODEL), by_expert),
                pl.BlockSpec((None, 1, D_MODEL), by_expert),
            ],
            out_specs=pl.BlockSpec(memory_space=pl.ANY),
            scratch_shapes=[
                pltpu.VMEM((D_MODEL, 2 * D_EXPERT), BF16),
                pltpu.VMEM((D_EXPERT, D_MODEL), BF16),
                pltpu.VMEM((2, ROW_TILE, width), jnp.int32),
                pltpu.VMEM((2, ROW_TILE, D_MODEL // 2), jnp.int32),
                pltpu.SemaphoreType.DMA((2,)),
                pltpu.SemaphoreType.DMA((2,)),
            ],
        ),
        out_shape=jax.ShapeDtypeStruct((n_rows, D_MODEL // 2), jnp.int32),
        compiler_params=_cparams(1, VMEM_LIMIT),
        name="experts",
    )(tiles.astype(jnp.int32), counts.astype(jnp.int32), base, next_row, xs, w_gate_up,
      b_gate_up.reshape(N_EXPERTS, 1, 2 * D_EXPERT), w_down, b_down.reshape(N_EXPERTS, 1, D_MODEL))


def _sc_gather_rows(table, idx):
    m = idx.shape[0]
    width = table.shape[1]
    mesh = plsc.VectorSubcoreMesh(core_axis_name="core", subcore_axis_name="subcore")
    n_workers = mesh.num_cores * mesh.num_subcores
    per_worker = m // n_workers
    assert per_worker * n_workers == m and per_worker % SC_WINDOW == 0

    @pl.kernel(
        out_type=jax.ShapeDtypeStruct((m, width), table.dtype),
        mesh=mesh,
        scratch_types=[
            pltpu.VMEM((SC_WINDOW,), jnp.int32),
            pltpu.VMEM((SC_WINDOW, width), table.dtype),
            pltpu.SemaphoreType.DMA,
        ],
    )
    def gather_kernel(table_hbm, idx_hbm, out_hbm, idx_v, rows_v, sem):
        worker = lax.axis_index("subcore") * mesh.num_cores + lax.axis_index("core")

        @pl.loop(0, per_worker // SC_WINDOW)
        def _(j):
            base = pl.multiple_of(worker * per_worker + j * SC_WINDOW, SC_WINDOW)
            pltpu.sync_copy(idx_hbm.at[pl.ds(base, SC_WINDOW)], idx_v)
            pltpu.async_copy(table_hbm.at[idx_v], rows_v, sem).wait()
            pltpu.sync_copy(rows_v, out_hbm.at[pl.ds(base, SC_WINDOW)])

    return gather_kernel(table, idx)


def _sc_scatter_rows_into(table, dst, out_ref, reps):
    m, width = table.shape
    mesh = plsc.VectorSubcoreMesh(core_axis_name="core", subcore_axis_name="subcore")
    n_workers = mesh.num_cores * mesh.num_subcores
    per_worker = m // n_workers
    assert per_worker * n_workers == m and per_worker % SC_WINDOW == 0 and dst.shape == (reps * m,)

    @pl.kernel(
        out_type=(),
        mesh=mesh,
        scratch_types=[
            pltpu.VMEM((SC_WINDOW,), jnp.int32),
            pltpu.VMEM((SC_WINDOW, width), table.dtype),
            pltpu.SemaphoreType.DMA,
        ],
    )
    def scatter_kernel(table_hbm, dst_hbm, out_hbm, dst_v, rows_v, sem):
        worker = lax.axis_index("subcore") * mesh.num_cores + lax.axis_index("core")

        @pl.loop(0, per_worker // SC_WINDOW)
        def _(j):
            base = pl.multiple_of(worker * per_worker + j * SC_WINDOW, SC_WINDOW)
            pltpu.sync_copy(table_hbm.at[pl.ds(base, SC_WINDOW)], rows_v)
            for k in range(reps):
                pltpu.sync_copy(dst_hbm.at[pl.ds(k * m + base, SC_WINDOW)], dst_v)
                pltpu.async_copy(rows_v, out_hbm.at[dst_v], sem).wait()

    scatter_kernel(table, dst, out_ref)


def _combine_dense_kernel(tw_ref, x1_ref, mod_ref, g_ref, y0_ref, y1_ref, y2_ref, y3_ref, o_in_ref,
                          o_ref):
    del o_in_ref
    tm = x1_ref.shape[0]
    tw = jnp.concatenate([tw_ref[...], jnp.zeros((LANES - SUBLANES, tm), F32)], axis=0).T
    first = second = None
    for kk, y_ref in enumerate((y0_ref, y1_ref, y2_ref, y3_ref)):
        f_k, s_k = _unpack_bf16_pairs(y_ref[...])
        w_k = tw[:, kk:kk + 1]
        first = w_k * f_k if kk == 0 else first + w_k * f_k
        second = w_k * s_k if kk == 0 else second + w_k * s_k
    ffn = jnp.concatenate([first, second], axis=1)
    x2 = x1_ref[...] + mod_ref[5:6, :] * ffn
    o_ref[...] = x2 * lax.rsqrt(jnp.mean(x2 * x2, axis=-1, keepdims=True) + EPS) * g_ref[...]


def _combine_dense(b, tw, x1, mod, g_final, yg, out_buf, seq):
    n = x1.shape[0]
    width = yg.shape[1]
    tiles = seq // TOK_TILE
    row = lambda i: (b * tiles + i, 0)
    slot = lambda kk: pl.BlockSpec((TOK_TILE, width), lambda i: (kk * tiles + i, 0))
    return pl.pallas_call(
        _combine_dense_kernel,
        grid=(tiles,),
        in_specs=[
            pl.BlockSpec((SUBLANES, TOK_TILE), lambda i: (0, i)),
            pl.BlockSpec((TOK_TILE, D_MODEL), row),
            pl.BlockSpec((None, SUBLANES, D_MODEL), lambda i: (b, 0, 0)),
            pl.BlockSpec((1, D_MODEL), lambda i: (0, 0)),
        ] + [slot(kk) for kk in range(TOP_K)] + [pl.BlockSpec(memory_space=pl.ANY)],
        out_specs=pl.BlockSpec((TOK_TILE, D_MODEL), row),
        out_shape=jax.ShapeDtypeStruct((n, D_MODEL), F32),
        input_output_aliases={4 + TOP_K: 0},
        compiler_params=_cparams(1, VMEM_LIMIT),
        name="combine",
    )(tw, x1, mod, g_final.reshape(1, D_MODEL), yg, yg, yg, yg, out_buf)


def _rope_base(positions):
    half = ROPE_DIM // 2
    inv_freq = ROPE_THETA ** (-jnp.arange(0, ROPE_DIM, 2, dtype=F32) / ROPE_DIM)
    n = positions.size
    pos = jnp.broadcast_to(positions.reshape(n, 1).astype(F32), (n, half)).reshape(-1, LANES)
    ang = pos * jnp.tile(inv_freq, LANES // half)
    cos, sin = lax.optimization_barrier((jnp.cos(ang), jnp.sin(ang)))
    pad = jnp.zeros((n, LANES - ROPE_DIM), F32)
    return jnp.concatenate([cos.reshape(n, half), sin.reshape(n, half), pad], axis=1)


def kernel(x, c, positions, w_ada, b_ada, g_mix_norm, w_in, rel_bias, lambda_q1, lambda_k1,
           lambda_q2, lambda_k2, g_subln, w_out, g_ffn_norm, w_router, b_router, w_gate_up,
           b_gate_up, w_down, b_down, g_final):
    batch, seq, _ = x.shape
    depth = w_ada.shape[0]
    assert depth == 1, "the combine kernel applies the final norm, so it must follow the only layer"
    n = batch * seq
    assert n <= EXPERT_ROWS and EXPERT_ROWS % ROW_TILE == 0
    rope_base = _rope_base(positions)
    x2 = x.reshape(n, D_MODEL)
    for l in range(depth):
        lambda_init = 0.8 - 0.6 * math.exp(-0.3 * l)
        mod = _ada(c, w_ada[l], b_ada[l])
        qa, ka, va, qb, kb, vb = _in_proj(x2, mod, g_mix_norm[l], w_in[l].astype(BF16),
                                          rope_base, seq)
        bias_rows = _rel_bias_rows(rel_bias[l])
        lam = (jnp.exp(jnp.sum(lambda_q1[l].astype(F32) * lambda_k1[l].astype(F32)))
               - jnp.exp(jnp.sum(lambda_q2[l].astype(F32) * lambda_k2[l].astype(F32)))
               + lambda_init).reshape(1)
        w_out_bf = w_out[l].astype(BF16)
        w_router_bf = w_router[l].astype(BF16)
        xs_ref = jax.new_ref(lax.empty((N_EXPERTS * EXPERT_ROWS, D_MODEL // 2), jnp.int32))
        x1 = lax.empty((n, D_MODEL), F32)
        counts = jnp.zeros((N_EXPERTS, 1), F32)
        pos_rows, weight_rows = [], []
        for b in range(batch):
            oa = _attn_a(b, qa, ka, va, bias_rows, seq)
            ob = _attn_b(b, lam, qb, kb, vb, g_subln[l], seq, 1.0 - lambda_init)
            x1, hp, top_i, top_w, rank, counts = _out_route(
                b, oa, ob, x2, mod, w_out_bf, g_ffn_norm[l], w_router_bf, b_router[l], counts, x1,
                seq)
            pos = top_i * EXPERT_ROWS + rank
            pos_rows.append(pos)
            weight_rows.append(top_w)
            dst = jnp.concatenate([pos[kk] for kk in range(TOP_K)])
            _sc_scatter_rows_into(hp, dst, xs_ref, TOP_K)
        xs = jax.freeze(xs_ref)
        y = _experts(counts[:, 0].astype(jnp.int32), xs, w_gate_up[l], b_gate_up[l], w_down[l],
                     b_down[l])
        x2 = lax.empty((n, D_MODEL), F32)
        for b in range(batch):
            yg = _sc_gather_rows(y, jnp.concatenate([pos_rows[b][kk] for kk in range(TOP_K)]))
            x2 = _combine_dense(b, weight_rows[b], x1, mod, g_final, yg, x2, seq)
    return x2.reshape(batch, seq, D_MODEL)
```

```python
import functools
import math

import jax
import jax.numpy as jnp
from jax import lax
from jax.experimental import pallas as pl
from jax.experimental.pallas import tpu as pltpu
from jax.experimental.pallas import tpu_sc as plsc

D_MODEL = 1024
CHUNK = 64
HEAD_DIM = 64
A_HEADS = 8
A_WIDTH = A_HEADS * HEAD_DIM
LEFT_CHUNKS = 8
MAX_REL = 128
B_HEADS = 4
B_QK_DIM = HEAD_DIM
B_V_DIM = 2 * HEAD_DIM
B_WIDTH = B_HEADS * B_V_DIM
ROPE_THETA = 500000.0
ROPE_DIM = B_QK_DIM // 4
N_EXPERTS = 32
TOP_K = 4
D_EXPERT = D_MODEL
SWIGLU_LIMIT = 7.0
SWIGLU_ALPHA = 1.702
EPS = 1e-6
NEG_INF = -1e30
LOG2_E = math.log2(math.e)
N_MOD = 6

LANES = 128
SUBLANES = 8
VMEM_BYTES_V7X = 64 * 1024 * 1024
VMEM_LIMIT = VMEM_BYTES_V7X * 7 // 8

TOK_TILE = 512
A_QBLK = 2 * CHUNK
A_BAND = (LEFT_CHUNKS + 2) * CHUNK
A_ROLL = A_BAND + A_QBLK
B_TQ = 512
B_TK = 512
B_SUB = 128
ROW_TILE = 512
ROW_STEP = 128
FF_CHUNK = 512
EXPERT_ROWS = 16384
SC_WINDOW = 128
SC_HALF_WINDOW = 64

F32 = jnp.float32
BF16 = jnp.bfloat16


def _cparams(n_axes, vmem=None):
    return pltpu.CompilerParams(
        dimension_semantics=("arbitrary",) * n_axes,
        vmem_limit_bytes=vmem,
    )


def _ada_kernel(c_ref, w_ref, b_ref, o_ref):
    c = c_ref[...]
    act = c * jax.nn.sigmoid(c)
    o_ref[...] = jnp.dot(act, w_ref[...], preferred_element_type=F32,
                         precision=lax.Precision.HIGHEST) + b_ref[...]


def _ada(c, w_ada, b_ada):
    b = c.shape[0]
    rows = -(-b // SUBLANES) * SUBLANES
    c_pad = jnp.pad(c, ((0, rows - b), (0, 0)))
    n_out = w_ada.shape[1]
    out = pl.pallas_call(
        _ada_kernel,
        grid=(n_out // D_MODEL,),
        in_specs=[
            pl.BlockSpec((rows, D_MODEL), lambda j: (0, 0)),
            pl.BlockSpec((D_MODEL, D_MODEL), lambda j: (0, j)),
            pl.BlockSpec((1, D_MODEL), lambda j: (0, j)),
        ],
        out_specs=pl.BlockSpec((rows, D_MODEL), lambda j: (0, j)),
        out_shape=jax.ShapeDtypeStruct((rows, n_out), F32),
        compiler_params=_cparams(1),
        name="ada",
    )(c_pad, w_ada, b_ada.reshape(1, n_out))
    mod = out[:b].reshape(b, N_MOD, D_MODEL)
    return jnp.pad(mod, ((0, 0), (0, SUBLANES - N_MOD), (0, 0)))


def _in_proj_kernel(x_ref, mod_ref, g_ref, w_ref, rb_ref,
                    qa_ref, ka_ref, va_ref, qb_ref, kb_ref, vb_ref):
    x = x_ref[...]
    mod = mod_ref[...]
    y = x * lax.rsqrt(jnp.mean(x * x, axis=-1, keepdims=True) + EPS) * g_ref[...]
    h = (y * (1.0 + mod[1:2, :]) + mod[0:1, :]).astype(BF16)
    q_scale = HEAD_DIM ** -0.5

    half = ROPE_DIM // 2
    rb = rb_ref[...]
    lane = lax.broadcasted_iota(jnp.int32, rb.shape, 1)
    cos_lo = jnp.where(lane < half, rb, 0.0)
    sin_hi = jnp.where(jnp.logical_and(lane >= half, lane < ROPE_DIM), rb, 0.0)
    cos_pair = cos_lo + pltpu.roll(cos_lo, half, 1)
    rc = (cos_pair + pltpu.roll(cos_pair, B_QK_DIM, 1)
          + jnp.where(lane % B_QK_DIM >= ROPE_DIM, 1.0, 0.0))
    sin_lo = pltpu.roll(sin_hi, LANES - half, 1)
    rm = -(sin_lo + pltpu.roll(sin_lo, B_QK_DIM, 1))
    rp = sin_hi + pltpu.roll(sin_hi, B_QK_DIM, 1)

    def rope(p):
        cols = []
        for s in range(p.shape[1] // LANES):
            v = p[:, s * LANES:(s + 1) * LANES]
            cols.append(v * rc + pltpu.roll(v, LANES - ROPE_DIM // 2, 1) * rm
                        + pltpu.roll(v, ROPE_DIM // 2, 1) * rp)
        return jnp.concatenate(cols, axis=1)

    outs = (qa_ref, ka_ref, va_ref, qb_ref, kb_ref, vb_ref)
    for j, o_ref in enumerate(outs):
        p = jnp.dot(h, w_ref[:, j * A_WIDTH:(j + 1) * A_WIDTH], preferred_element_type=F32)
        if j in (3, 4):
            p = rope(p)
        if j in (0, 3):
            p = p * (q_scale * LOG2_E)
        pb = p.astype(BF16)
        for s in range(A_WIDTH // LANES):
            o_ref[s] = pb[:, s * LANES:(s + 1) * LANES]


def _in_proj(x2, mod, g_mix, w_in_bf, rope_base, seq):
    n = x2.shape[0]
    tiles_per_seq = seq // TOK_TILE
    row = lambda i: (i, 0)
    fixed = lambda i: (0, 0)
    n_slabs = A_WIDTH // LANES
    out_sd = jax.ShapeDtypeStruct((n_slabs, n, LANES), BF16)
    return pl.pallas_call(
        _in_proj_kernel,
        grid=(n // TOK_TILE,),
        in_specs=[
            pl.BlockSpec((TOK_TILE, D_MODEL), row),
            pl.BlockSpec((None, SUBLANES, D_MODEL), lambda i: (i // tiles_per_seq, 0, 0)),
            pl.BlockSpec((1, D_MODEL), fixed),
            pl.BlockSpec(w_in_bf.shape, fixed),
            pl.BlockSpec((TOK_TILE, LANES), row),
        ],
        out_specs=[pl.BlockSpec((n_slabs, TOK_TILE, LANES), lambda i: (0, i, 0))] * 6,
        out_shape=[out_sd] * 6,
        compiler_params=_cparams(1, VMEM_LIMIT),
        name="in_proj",
    )(x2, mod, g_mix.reshape(1, D_MODEL), w_in_bf, rope_base)


def _attn_a_kernel(q_ref, kp_ref, kc_ref, vp_ref, vc_ref, bias_ref, o_ref, k_sc, v_sc, bias_sc):
    g = pl.program_id(0)
    n_pairs, blk, _ = q_ref.shape
    lane = lax.broadcasted_iota(jnp.int32, (A_QBLK, LANES), 1)
    col = lax.broadcasted_iota(jnp.int32, (A_QBLK, A_BAND), 1)
    ones = jnp.ones((A_BAND, LANES), BF16)

    @pl.when(g == 0)
    def _():
        q_chunk = lax.broadcasted_iota(jnp.int32, (A_QBLK, A_BAND), 0) // CHUNK
        k_chunk = col // CHUNK
        in_band = jnp.logical_and(k_chunk >= q_chunk, k_chunk <= q_chunk + LEFT_CHUNKS)
        for h in range(bias_ref.shape[0]):
            rolled = pltpu.roll(jnp.broadcast_to(bias_ref[h], (A_QBLK, A_ROLL)), 0, 1,
                                stride=1, stride_axis=0)
            bias_sc[h] = jnp.where(in_band, rolled[:, :A_BAND], NEG_INF)

    def pair(p):
        k_buf = k_sc.at[p % 2]
        v_buf = v_sc.at[p % 2]
        k_buf[0:blk, :] = kp_ref[p]
        k_buf[blk:2 * blk, :] = kc_ref[p]
        v_buf[0:blk, :] = vp_ref[p]
        v_buf[blk:2 * blk, :] = vc_ref[p]

        def scores(m, hh):
            r0 = m * A_QBLK
            q = q_ref[p, r0:r0 + A_QBLK, :]
            in_head = (lane < HEAD_DIM) if hh == 0 else (lane >= HEAD_DIM)
            qh = jnp.where(in_head, q, jnp.zeros_like(q))
            return lax.dot_general(qh, k_buf[r0:r0 + A_BAND, :], (((1,), (1,)), ((), ())),
                                   preferred_element_type=F32)

        chains = [(m, hh) for m in range(blk // A_QBLK) for hh in range(2)]
        s_next = scores(*chains[0])
        halves = []
        for i, (m, hh) in enumerate(chains):
            r0 = m * A_QBLK
            s = s_next
            if i + 1 < len(chains):
                s_next = scores(*chains[i + 1])
            valid = jnp.logical_or(g > 0, col + r0 >= blk)
            s = jnp.where(valid, s + bias_sc[2 * p + hh], NEG_INF)
            pr = jnp.exp2(s - jnp.max(s, axis=1, keepdims=True))
            v_ext = jnp.concatenate([v_buf[r0:r0 + A_BAND, :], ones], axis=1)
            pv = jnp.dot(pr.astype(BF16), v_ext, preferred_element_type=F32)
            halves.append(pv[:, :LANES] / pv[:, LANES:])
            if hh == 1:
                o_ref[p, r0:r0 + A_QBLK, :] = jnp.where(lane < HEAD_DIM, halves[0],
                                                        halves[1]).astype(BF16)
                halves = []

    for p in range(n_pairs):
        pair(p)


def _attn_a(b, qa, ka, va, bias_rows, seq):
    n_pairs = qa.shape[0]
    blk = LEFT_CHUNKS * CHUNK
    nblk = seq // blk
    cur = lambda g: (0, b * nblk + g, 0)
    prev = lambda g: (0, b * nblk + jnp.maximum(g - 1, 0), 0)
    slab = (n_pairs, blk, LANES)
    return pl.pallas_call(
        _attn_a_kernel,
        grid=(nblk,),
        in_specs=[
            pl.BlockSpec(slab, cur),
            pl.BlockSpec(slab, prev),
            pl.BlockSpec(slab, cur),
            pl.BlockSpec(slab, prev),
            pl.BlockSpec(slab, cur),
            pl.BlockSpec(bias_rows.shape, lambda g: (0, 0, 0)),
        ],
        out_specs=pl.BlockSpec(slab, lambda g: (0, g, 0)),
        out_shape=jax.ShapeDtypeStruct((n_pairs, seq, LANES), BF16),
        scratch_shapes=[
            pltpu.VMEM((2, 2 * blk, LANES), BF16),
            pltpu.VMEM((2, 2 * blk, LANES), BF16),
            pltpu.VMEM((bias_rows.shape[0], A_QBLK, A_BAND), F32),
        ],
        compiler_params=_cparams(1),
        name="attn_a",
    )(qa, ka, ka, va, va, bias_rows)


def _rel_bias_rows(rel_table):
    t = rel_table.astype(F32) * LOG2_E
    far = t[:, 2 * MAX_REL:]
    n_far = LEFT_CHUNKS * CHUNK - MAX_REL
    row = jnp.concatenate([
        jnp.broadcast_to(far, (t.shape[0], n_far)),
        t[:, 2 * MAX_REL:0:-1],
        jnp.broadcast_to(far, (t.shape[0], A_ROLL - A_BAND)),
    ], axis=1)
    return row.reshape(t.shape[0], 1, A_ROLL)


def _attn_b_kernel(lam_ref, q_ref, k_ref, v_ref, g_ref, o_ref, q_sc, qn_sc, s_sc, m_sc, acc_sc, *,
                   out_scale):
    n_tiles = q_ref.shape[0] // B_TQ
    n_sub = B_TQ // B_SUB
    first_slot = 2
    lane = lax.broadcasted_iota(jnp.int32, (B_TQ, LANES), 1)
    ones = jnp.ones((B_TK, LANES), BF16)
    lam = lam_ref[0]

    def split_components(t, dst):
        q = q_ref[pl.ds(pl.multiple_of(t * B_TQ, B_TQ), B_TQ), :]
        dst[0] = jnp.where(lane < B_QK_DIM, q, jnp.zeros_like(q))
        dst[1] = jnp.where(lane >= B_QK_DIM, q, jnp.zeros_like(q))

    def score_rows(src, k, slot, sub, c):
        rows = pl.ds(sub * B_SUB, B_SUB)
        s_sc[slot, c, rows, :] = lax.dot_general(src[c, rows, :], k, (((1,), (1,)), ((), ())),
                                                 preferred_element_type=F32)

    def update_rows(s, v_ext, sub, c):
        rows = pl.ds(sub * B_SUB, B_SUB)
        m_prev = m_sc[c, rows, :]
        m_new = jnp.maximum(m_prev, jnp.max(s, axis=1, keepdims=True))
        alpha = jnp.exp2(m_prev - m_new)
        p = jnp.exp2(s - jnp.concatenate([m_new] * (s.shape[1] // LANES), axis=1))
        pv = jnp.dot(p.astype(BF16), v_ext, preferred_element_type=F32)
        acc_sc[c, rows, :] = jnp.concatenate([alpha, alpha], axis=1) * acc_sc[c, rows, :] + pv
        m_sc[c, rows, :] = m_new

    def key_block(blk):
        return k_ref[pl.ds(pl.multiple_of(blk * B_TK, B_TK), B_TK), :]

    def value_block(blk):
        return jnp.concatenate([v_ref[pl.ds(pl.multiple_of(blk * B_TK, B_TK), B_TK), :], ones], axis=1)

    def step(blk, slot, next_blk, next_slot):
        v_ext = value_block(blk)
        k_next = key_block(next_blk)
        for sub in range(n_sub):
            for c in range(2):
                update_rows(s_sc[slot, c, pl.ds(sub * B_SUB, B_SUB), :], v_ext, sub, c)
                score_rows(q_sc, k_next, next_slot, sub, c)

    def last_step(t, slot):
        v_ext = value_block(t)
        k0 = key_block(0)
        col_c = lax.broadcasted_iota(jnp.int32, (B_SUB, B_TK), 1) // CHUNK
        row_c = lax.broadcasted_iota(jnp.int32, (B_SUB, B_TK), 0) // CHUNK
        for sub in range(n_sub):
            keep = col_c <= row_c + sub * (B_SUB // CHUNK)
            for c in range(2):
                s = s_sc[slot, c, pl.ds(sub * B_SUB, B_SUB), :]
                update_rows(jnp.where(keep, s, NEG_INF), v_ext, sub, c)
                score_rows(qn_sc, k0, first_slot, sub, c)

    def start_tile(t):
        split_components(t, q_sc)
        split_components(jnp.minimum(t + 1, n_tiles - 1), qn_sc)
        m_sc[...] = jnp.full(m_sc.shape, NEG_INF, F32)
        acc_sc[...] = jnp.zeros(acc_sc.shape, F32)

    def finish_tile(t):
        a0 = acc_sc[0]
        a1 = acc_sc[1]
        o = a0[:, :B_V_DIM] / a0[:, B_V_DIM:] - lam * (a1[:, :B_V_DIM] / a1[:, B_V_DIM:])
        y = o * lax.rsqrt(jnp.mean(o * o, axis=-1, keepdims=True) + EPS) * g_ref[...]
        o_ref[pl.ds(pl.multiple_of(t * B_TQ, B_TQ), B_TQ), :] = (y * out_scale).astype(BF16)

    start_tile(0)
    k0 = key_block(0)
    for sub in range(n_sub):
        for c in range(2):
            score_rows(q_sc, k0, first_slot, sub, c)
    s_sc[0] = s_sc[first_slot]
    last_step(0, 0)
    finish_tile(0)

    def tile(t, carry):
        start_tile(t)
        step(0, first_slot, 1, 1)

        def pair(p, carry):
            odd = 2 * p + 1
            step(odd, 1, odd + 1, 0)
            step(odd + 1, 0, odd + 2, 1)
            return carry

        lax.fori_loop(0, (t - 1) // 2, pair, 0)

        @pl.when(t % 2 == 1)
        def _():
            last_step(t, 1)

        @pl.when(t % 2 == 0)
        def _():
            step(t - 1, 1, t, 0)
            last_step(t, 0)

        finish_tile(t)
        return carry

    lax.fori_loop(1, n_tiles, tile, 0)


def _attn_b(b, lam, qb, kb, vb, g_subln, seq, out_scale):
    assert B_TQ == B_TK and B_V_DIM == LANES
    by_head = lambda h: (h, b, 0)
    return pl.pallas_call(
        functools.partial(_attn_b_kernel, out_scale=out_scale),
        grid=(B_HEADS,),
        in_specs=[
            pl.BlockSpec(memory_space=pltpu.SMEM),
            pl.BlockSpec((None, seq, LANES), by_head),
            pl.BlockSpec((None, seq, LANES), by_head),
            pl.BlockSpec((None, seq, LANES), by_head),
            pl.BlockSpec((1, B_V_DIM), lambda h: (0, 0)),
        ],
        out_specs=pl.BlockSpec((None, seq, LANES), lambda h: (h, 0, 0)),
        out_shape=jax.ShapeDtypeStruct((B_HEADS, seq, LANES), BF16),
        scratch_shapes=[
            pltpu.VMEM((2, B_TQ, LANES), BF16),
            pltpu.VMEM((2, B_TQ, LANES), BF16),
            pltpu.VMEM((3, 2, B_TQ, B_TK), F32),
            pltpu.VMEM((2, B_TQ, LANES), F32),
            pltpu.VMEM((2, B_TQ, 2 * LANES), F32),
        ],
        compiler_params=_cparams(1),
        name="attn_b",
    )(lam, qb, kb, vb, g_subln.reshape(1, B_V_DIM))


def _pack_bf16_pairs(v):
    half = v.shape[1] // 2
    vb = v.astype(BF16)
    hi = lax.bitcast_convert_type(vb[:, :half].astype(F32), jnp.int32)
    lo = lax.bitcast_convert_type(vb[:, half:].astype(F32), jnp.int32)
    return hi | lax.shift_right_logical(lo, jnp.full(lo.shape, 16, jnp.int32))


def _unpack_bf16_pairs(w):
    first = lax.bitcast_convert_type(w & jnp.int32(-65536), F32)
    second = lax.bitcast_convert_type(lax.shift_left(w, jnp.full(w.shape, 16, jnp.int32)), F32)
    return first, second


def _out_route_kernel(oa_ref, ob_ref, x_ref, mod_ref, wo_ref, g_ref, wr_ref, br_ref, cin_ref,
                      x1_in_ref, x1_ref, hp_ref, ti_ref, tw_ref, rk_ref, cnt_ref, tri_sc, carry_sc):
    del x1_in_ref
    i = pl.program_id(0)
    tm = x_ref.shape[0]

    @pl.when(i == 0)
    def _():
        r = lax.broadcasted_iota(jnp.int32, (tm, tm), 0)
        c = lax.broadcasted_iota(jnp.int32, (tm, tm), 1)
        tri_sc[...] = jnp.where(r < c, 1.0, 0.0).astype(BF16)
        carry_sc[...] = cin_ref[...]

    mod = mod_ref[...]
    o = jnp.concatenate([oa_ref[s] for s in range(oa_ref.shape[0])]
                        + [ob_ref[s] for s in range(ob_ref.shape[0])], axis=1)
    mix = jnp.dot(o, wo_ref[...], preferred_element_type=F32)
    x1 = x_ref[...] + mod[2:3, :] * mix
    x1_ref[...] = x1
    y = x1 * lax.rsqrt(jnp.mean(x1 * x1, axis=-1, keepdims=True) + EPS) * g_ref[...]
    h = y * (1.0 + mod[4:5, :]) + mod[3:4, :]
    hb = h.astype(BF16)
    hp_ref[...] = _pack_bf16_pairs(h)

    logits = lax.dot_general(wr_ref[...], hb, (((1,), (1,)), ((), ())),
                             preferred_element_type=F32) + br_ref[...]
    eid = lax.broadcasted_iota(jnp.int32, logits.shape, 0).astype(F32)
    work = logits
    vals, ids = [], []
    chosen = jnp.zeros(logits.shape, F32)
    for _ in range(TOP_K):
        v = jnp.max(work, axis=0, keepdims=True)
        e = jnp.min(jnp.where(work == v, eid, float(N_EXPERTS)), axis=0, keepdims=True)
        hit = eid == e
        vals.append(v)
        ids.append(e)
        chosen = jnp.where(hit, 1.0, chosen)
        work = jnp.where(hit, -jnp.inf, work)
    ex = [jnp.exp(v - vals[0]) for v in vals]
    den = ex[0] + ex[1] + ex[2] + ex[3]

    before = jnp.dot(chosen.astype(BF16), tri_sc[...], preferred_element_type=F32) + carry_sc[...]
    slot = lax.broadcasted_iota(jnp.int32, (SUBLANES, tm), 0)
    ti = jnp.zeros((SUBLANES, tm), F32)
    tw = jnp.zeros((SUBLANES, tm), F32)
    rk = jnp.zeros((SUBLANES, tm), F32)
    for kk in range(TOP_K):
        r_k = jnp.sum(jnp.where(eid == ids[kk], before, 0.0), axis=0, keepdims=True)
        ti = jnp.where(slot == kk, ids[kk], ti)
        tw = jnp.where(slot == kk, ex[kk] / den, tw)
        rk = jnp.where(slot == kk, r_k, rk)
    ti_ref[...] = ti.astype(jnp.int32)
    tw_ref[...] = tw
    rk_ref[...] = rk.astype(jnp.int32)
    carry = carry_sc[...] + jnp.sum(chosen, axis=1, keepdims=True)
    carry_sc[...] = carry
    cnt_ref[...] = carry


def _out_route(b, oa, ob, x2, mod, w_out_bf, g_ffn, w_router_bf, b_router, counts_in, x1_buf, seq):
    n = x2.shape[0]
    tiles = seq // TOK_TILE
    row = lambda i: (b * tiles + i, 0)
    local = lambda i: (i, 0)
    by_lane = lambda i: (0, i)
    fixed = lambda i: (0, 0)
    return pl.pallas_call(
        _out_route_kernel,
        grid=(tiles,),
        in_specs=[
            pl.BlockSpec((oa.shape[0], TOK_TILE, LANES), lambda i: (0, i, 0)),
            pl.BlockSpec((ob.shape[0], TOK_TILE, LANES), lambda i: (0, i, 0)),
            pl.BlockSpec((TOK_TILE, D_MODEL), row),
            pl.BlockSpec((None, SUBLANES, D_MODEL), lambda i: (b, 0, 0)),
            pl.BlockSpec((D_MODEL, D_MODEL), fixed),
            pl.BlockSpec((1, D_MODEL), fixed),
            pl.BlockSpec((N_EXPERTS, D_MODEL), fixed),
            pl.BlockSpec((N_EXPERTS, 1), fixed),
            pl.BlockSpec((N_EXPERTS, 1), fixed),
            pl.BlockSpec(memory_space=pl.ANY),
        ],
        out_specs=[
            pl.BlockSpec((TOK_TILE, D_MODEL), row),
            pl.BlockSpec((TOK_TILE, D_MODEL // 2), local),
            pl.BlockSpec((SUBLANES, TOK_TILE), by_lane),
            pl.BlockSpec((SUBLANES, TOK_TILE), by_lane),
            pl.BlockSpec((SUBLANES, TOK_TILE), by_lane),
            pl.BlockSpec((N_EXPERTS, 1), fixed),
        ],
        out_shape=[
            jax.ShapeDtypeStruct((n, D_MODEL), F32),
            jax.ShapeDtypeStruct((seq, D_MODEL // 2), jnp.int32),
            jax.ShapeDtypeStruct((SUBLANES, seq), jnp.int32),
            jax.ShapeDtypeStruct((SUBLANES, seq), F32),
            jax.ShapeDtypeStruct((SUBLANES, seq), jnp.int32),
            jax.ShapeDtypeStruct((N_EXPERTS, 1), F32),
        ],
        input_output_aliases={9: 0},
        scratch_shapes=[pltpu.VMEM((TOK_TILE, TOK_TILE), BF16), pltpu.VMEM((N_EXPERTS, 1), F32)],
        compiler_params=_cparams(1, VMEM_LIMIT),
        name="out_route",
    )(oa, ob, x2, mod, w_out_bf, g_ffn.reshape(1, D_MODEL), w_router_bf.T,
      b_router.reshape(N_EXPERTS, 1), counts_in, x1_buf)


def _experts_kernel(tiles_ref, rows_ref, base_ref, next_ref, xs_ref, wgu_ref, bgu_ref, wd_ref,
                    bd_ref, y_ref, wgu_sc, wd_sc, x_buf, y_buf, x_sem, y_sem):
    e = pl.program_id(0)
    n_tiles = tiles_ref[e]
    n_rows = rows_ref[e]
    n_live = base_ref[N_EXPERTS]
    region = e * EXPERT_ROWS

    def x_copy(row, slot):
        return pltpu.make_async_copy(xs_ref.at[pl.ds(row, ROW_TILE), :], x_buf.at[slot],
                                     x_sem.at[slot])

    def y_copy(row, slot):
        return pltpu.make_async_copy(y_buf.at[slot], y_ref.at[pl.ds(row, ROW_TILE), :],
                                     y_sem.at[slot])

    @pl.when(e == 0)
    def _():
        y_buf[...] = jnp.zeros(y_buf.shape, jnp.int32)

        @pl.when(n_live > 0)
        def _():
            x_copy(pl.multiple_of(next_ref[N_EXPERTS], ROW_TILE), 0).start()

    @pl.when(n_tiles > 0)
    def _():
        wgu_sc[...] = wgu_ref[...].astype(BF16)
        wd_sc[...] = wd_ref[...].astype(BF16)

    def tile(j, carry):
        g = base_ref[e] + j
        slot = g % 2
        row = pl.multiple_of(region + j * ROW_TILE, ROW_TILE)
        x_copy(row, slot).wait()
        next_row = jnp.where(j + 1 < n_tiles, row + ROW_TILE, next_ref[e])

        @pl.when(g + 1 < n_live)
        def _():
            x_copy(pl.multiple_of(next_row, ROW_TILE), 1 - slot).start()

        @pl.when(g >= 2)
        def _():
            y_copy(0, slot).wait()

        needed = jnp.minimum(n_rows - j * ROW_TILE, ROW_TILE)
        for rows in range(ROW_STEP, ROW_TILE + 1, ROW_STEP):
            @pl.when(jnp.logical_and(needed > rows - ROW_STEP, needed <= rows))
            def _(rows=rows):
                used = lax.broadcasted_iota(jnp.int32, (rows, 1), 0) < needed
                first, second = _unpack_bf16_pairs(jnp.where(used, x_buf[slot, :rows, :], 0))
                x = jnp.concatenate([first.astype(BF16), second.astype(BF16)], axis=1)
                acc = jnp.zeros((rows, D_MODEL), F32)
                for c in range(D_EXPERT // FF_CHUNK):
                    lo_c, hi_c = c * FF_CHUNK, (c + 1) * FF_CHUNK
                    gate = (jnp.dot(x, wgu_sc[:, lo_c:hi_c], preferred_element_type=F32)
                            + bgu_ref[:, lo_c:hi_c])
                    up = (jnp.dot(x, wgu_sc[:, D_EXPERT + lo_c:D_EXPERT + hi_c],
                                  preferred_element_type=F32)
                          + bgu_ref[:, D_EXPERT + lo_c:D_EXPERT + hi_c])
                    gate = jnp.minimum(gate, SWIGLU_LIMIT)
                    up = jnp.clip(up, -SWIGLU_LIMIT, SWIGLU_LIMIT)
                    act = (up + 1.0) * (gate * jax.nn.sigmoid(SWIGLU_ALPHA * gate))
                    acc = acc + jnp.dot(act.astype(BF16), wd_sc[lo_c:hi_c, :],
                                        preferred_element_type=F32)
                y_buf[slot, :rows, :] = _pack_bf16_pairs(acc + bd_ref[...])

        y_copy(row, slot).start()
        return carry

    lax.fori_loop(0, n_tiles, tile, 0)

    @pl.when(e == pl.num_programs(0) - 1)
    def _():
        for back in (2, 1):
            @pl.when(n_live >= back)
            def _():
                y_copy(0, (n_live - back) % 2).wait()


def _experts(counts, xs, w_gate_up, b_gate_up, w_down, b_down):
    n_rows, width = xs.shape
    tiles = (counts + ROW_TILE - 1) // ROW_TILE
    base = jnp.concatenate([jnp.zeros((1,), jnp.int32), jnp.cumsum(tiles)]).astype(jnp.int32)
    region = jnp.arange(N_EXPERTS, dtype=jnp.int32) * EXPERT_ROWS
    later = jnp.arange(N_EXPERTS)[None, :] > jnp.arange(-1, N_EXPERTS)[:, None]
    cand = jnp.where(jnp.logical_and(later, (tiles > 0)[None, :]), region[None, :], n_rows)
    nxt = jnp.min(cand, axis=1)
    next_row = jnp.concatenate([nxt[1:], nxt[:1]]).astype(jnp.int32)
    by_expert = lambda e, *_: (e, 0, 0)
    return pl.pallas_call(
        _experts_kernel,
        grid_spec=pltpu.PrefetchScalarGridSpec(
            num_scalar_prefetch=4,
            grid=(N_EXPERTS,),
            in_specs=[
                pl.BlockSpec(memory_space=pl.ANY),
                pl.BlockSpec((None, D_MODEL, 2 * D_EXPERT), by_expert),
                pl.BlockSpec((None, 1, 2 * D_EXPERT), by_expert),
                pl.BlockSpec((None, D_EXPERT, D_MODEL), by_expert),
                pl.BlockSpec((None, 1, D_MODEL), by_expert),
            ],
            out_specs=pl.BlockSpec(memory_space=pl.ANY),
            scratch_shapes=[
                pltpu.VMEM((D_MODEL, 2 * D_EXPERT), BF16),
                pltpu.VMEM((D_EXPERT, D_MODEL), BF16),
                pltpu.VMEM((2, ROW_TILE, width), jnp.int32),
                pltpu.VMEM((2, ROW_TILE, D_MODEL // 2), jnp.int32),
                pltpu.SemaphoreType.DMA((2,)),
                pltpu.SemaphoreType.DMA((2,)),
            ],
        ),
        out_shape=jax.ShapeDtypeStruct((n_rows, D_MODEL // 2), jnp.int32),
        compiler_params=_cparams(1, VMEM_LIMIT),
        name="experts",
    )(tiles.astype(jnp.int32), counts.astype(jnp.int32), base, next_row, xs, w_gate_up,
      b_gate_up.reshape(N_EXPERTS, 1, 2 * D_EXPERT), w_down, b_down.reshape(N_EXPERTS, 1, D_MODEL))


def _sc_gather_rows(table, idx):
    m = idx.shape[0]
    width = table.shape[1]
    mesh = plsc.VectorSubcoreMesh(core_axis_name="core", subcore_axis_name="subcore")
    n_workers = mesh.num_cores * mesh.num_subcores
    per_worker = m // n_workers
    n_windows = per_worker // SC_HALF_WINDOW
    assert per_worker * n_workers == m and n_windows * SC_HALF_WINDOW == per_worker

    @pl.kernel(
        out_type=jax.ShapeDtypeStruct((m, width), table.dtype),
        mesh=mesh,
        scratch_types=[
            pltpu.VMEM((SC_HALF_WINDOW,), jnp.int32),
            pltpu.VMEM((SC_HALF_WINDOW,), jnp.int32),
            pltpu.VMEM((SC_HALF_WINDOW, width), table.dtype),
            pltpu.VMEM((SC_HALF_WINDOW, width), table.dtype),
            pltpu.SemaphoreType.DMA,
            pltpu.SemaphoreType.DMA,
            pltpu.SemaphoreType.DMA,
            pltpu.SemaphoreType.DMA,
        ],
    )
    def gather_kernel(table_hbm, idx_hbm, out_hbm, idx0, idx1, rows0, rows1, g0, g1, w0, w1):
        worker = lax.axis_index("subcore") * mesh.num_cores + lax.axis_index("core")
        idx_v, rows_v, g_sem, w_sem = (idx0, idx1), (rows0, rows1), (g0, g1), (w0, w1)

        def window(j):
            return pl.ds(pl.multiple_of(worker * per_worker + j * SC_HALF_WINDOW, SC_HALF_WINDOW),
                         SC_HALF_WINDOW)

        def fetch(j):
            s = j % 2
            pltpu.sync_copy(idx_hbm.at[window(j)], idx_v[s])
            return pltpu.async_copy(table_hbm.at[idx_v[s]], rows_v[s], g_sem[s])

        fetches = {0: fetch(0)}
        writes = {}
        for j in range(n_windows):
            s = j % 2
            if j + 1 < n_windows:
                if j >= 1:
                    writes[j - 1].wait()
                fetches[j + 1] = fetch(j + 1)
            fetches[j].wait()
            writes[j] = pltpu.async_copy(rows_v[s], out_hbm.at[window(j)], w_sem[s])
        for j in range(max(n_windows - 2, 0), n_windows):
            writes[j].wait()

    return gather_kernel(table, idx)


def _sc_scatter_rows_into(table, dst, out_ref, reps):
    m, width = table.shape
    mesh = plsc.VectorSubcoreMesh(core_axis_name="core", subcore_axis_name="subcore")
    n_workers = mesh.num_cores * mesh.num_subcores
    per_worker = m // n_workers
    assert per_worker * n_workers == m and per_worker % SC_WINDOW == 0 and dst.shape == (reps * m,)

    @pl.kernel(
        out_type=(),
        mesh=mesh,
        scratch_types=[
            pltpu.VMEM((SC_WINDOW,), jnp.int32),
            pltpu.VMEM((SC_WINDOW, width), table.dtype),
            pltpu.SemaphoreType.DMA,
        ],
    )
    def scatter_kernel(table_hbm, dst_hbm, out_hbm, dst_v, rows_v, sem):
        worker = lax.axis_index("subcore") * mesh.num_cores + lax.axis_index("core")

        @pl.loop(0, per_worker // SC_WINDOW)
        def _(j):
            base = pl.multiple_of(worker * per_worker + j * SC_WINDOW, SC_WINDOW)
            pltpu.sync_copy(table_hbm.at[pl.ds(base, SC_WINDOW)], rows_v)
            for k in range(reps):
                pltpu.sync_copy(dst_hbm.at[pl.ds(k * m + base, SC_WINDOW)], dst_v)
                pltpu.async_copy(rows_v, out_hbm.at[dst_v], sem).wait()

    scatter_kernel(table, dst, out_ref)


def _combine_dense_kernel(tw_ref, x1_ref, mod_ref, g_ref, y0_ref, y1_ref, y2_ref, y3_ref, o_in_ref,
                          o_ref):
    del o_in_ref
    tm = x1_ref.shape[0]
    tw = jnp.concatenate([tw_ref[...], jnp.zeros((LANES - SUBLANES, tm), F32)], axis=0).T
    first = second = None
    for kk, y_ref in enumerate((y0_ref, y1_ref, y2_ref, y3_ref)):
        f_k, s_k = _unpack_bf16_pairs(y_ref[...])
        w_k = tw[:, kk:kk + 1]
        first = w_k * f_k if kk == 0 else first + w_k * f_k
        second = w_k * s_k if kk == 0 else second + w_k * s_k
    ffn = jnp.concatenate([first, second], axis=1)
    x2 = x1_ref[...] + mod_ref[5:6, :] * ffn
    o_ref[...] = x2 * lax.rsqrt(jnp.mean(x2 * x2, axis=-1, keepdims=True) + EPS) * g_ref[...]


def _combine_dense(b, tw, x1, mod, g_final, yg, out_buf, seq):
    n = x1.shape[0]
    width = yg.shape[1]
    tiles = seq // TOK_TILE
    row = lambda i: (b * tiles + i, 0)
    slot = lambda kk: pl.BlockSpec((TOK_TILE, width), lambda i: (kk * tiles + i, 0))
    return pl.pallas_call(
        _combine_dense_kernel,
        grid=(tiles,),
        in_specs=[
            pl.BlockSpec((SUBLANES, TOK_TILE), lambda i: (0, i)),
            pl.BlockSpec((TOK_TILE, D_MODEL), row),
            pl.BlockSpec((None, SUBLANES, D_MODEL), lambda i: (b, 0, 0)),
            pl.BlockSpec((1, D_MODEL), lambda i: (0, 0)),
        ] + [slot(kk) for kk in range(TOP_K)] + [pl.BlockSpec(memory_space=pl.ANY)],
        out_specs=pl.BlockSpec((TOK_TILE, D_MODEL), row),
        out_shape=jax.ShapeDtypeStruct((n, D_MODEL), F32),
        input_output_aliases={4 + TOP_K: 0},
        compiler_params=_cparams(1, VMEM_LIMIT),
        name="combine",
    )(tw, x1, mod, g_final.reshape(1, D_MODEL), yg, yg, yg, yg, out_buf)


def _rope_base(positions):
    half = ROPE_DIM // 2
    inv_freq = ROPE_THETA ** (-jnp.arange(0, ROPE_DIM, 2, dtype=F32) / ROPE_DIM)
    n = positions.size
    pos = jnp.broadcast_to(positions.reshape(n, 1).astype(F32), (n, half)).reshape(-1, LANES)
    ang = pos * jnp.tile(inv_freq, LANES // half)
    cos, sin = lax.optimization_barrier((jnp.cos(ang), jnp.sin(ang)))
    pad = jnp.zeros((n, LANES - ROPE_DIM), F32)
    return jnp.concatenate([cos.reshape(n, half), sin.reshape(n, half), pad], axis=1)


def kernel(x, c, positions, w_ada, b_ada, g_mix_norm, w_in, rel_bias, lambda_q1, lambda_k1,
           lambda_q2, lambda_k2, g_subln, w_out, g_ffn_norm, w_router, b_router, w_gate_up,
           b_gate_up, w_down, b_down, g_final):
    batch, seq, _ = x.shape
    depth = w_ada.shape[0]
    assert depth == 1, "the combine kernel applies the final norm, so it must follow the only layer"
    n = batch * seq
    assert n <= EXPERT_ROWS and EXPERT_ROWS % ROW_TILE == 0
    rope_base = _rope_base(positions)
    x2 = x.reshape(n, D_MODEL)
    for l in range(depth):
        lambda_init = 0.8 - 0.6 * math.exp(-0.3 * l)
        mod = _ada(c, w_ada[l], b_ada[l])
        qa, ka, va, qb, kb, vb = _in_proj(x2, mod, g_mix_norm[l], w_in[l].astype(BF16),
                                          rope_base, seq)
        bias_rows = _rel_bias_rows(rel_bias[l])
        lam = (jnp.exp(jnp.sum(lambda_q1[l].astype(F32) * lambda_k1[l].astype(F32)))
               - jnp.exp(jnp.sum(lambda_q2[l].astype(F32) * lambda_k2[l].astype(F32)))
               + lambda_init).reshape(1)
        w_out_bf = w_out[l].astype(BF16)
        w_router_bf = w_router[l].astype(BF16)
        xs_ref = jax.new_ref(lax.empty((N_EXPERTS * EXPERT_ROWS, D_MODEL // 2), jnp.int32))
        x1 = lax.empty((n, D_MODEL), F32)
        counts = jnp.zeros((N_EXPERTS, 1), F32)
        pos_rows, weight_rows = [], []
        for b in range(batch):
            oa = _attn_a(b, qa, ka, va, bias_rows, seq)
            ob = _attn_b(b, lam, qb, kb, vb, g_subln[l], seq, 1.0 - lambda_init)
            x1, hp, top_i, top_w, rank, counts = _out_route(
                b, oa, ob, x2, mod, w_out_bf, g_ffn_norm[l], w_router_bf, b_router[l], counts, x1,
                seq)
            pos = top_i * EXPERT_ROWS + rank
            pos_rows.append(pos)
            weight_rows.append(top_w)
            dst = jnp.concatenate([pos[kk] for kk in range(TOP_K)])
            _sc_scatter_rows_into(hp, dst, xs_ref, TOP_K)
        xs = jax.freeze(xs_ref)
        y = _experts(counts[:, 0].astype(jnp.int32), xs, w_gate_up[l], b_gate_up[l], w_down[l],
                     b_down[l])
        x2 = lax.empty((n, D_MODEL), F32)
        for b in range(batch):
            yg = _sc_gather_rows(y, jnp.concatenate([pos_rows[b][kk] for kk in range(TOP_K)]))
            x2 = _combine_dense(b, weight_rows[b], x1, mod, g_final, yg, x2, seq)
    return x2.reshape(batch, seq, D_MODEL)
```

```python
import functools
import math

import jax
import jax.numpy as jnp
from jax import lax
from jax.experimental import pallas as pl
from jax.experimental.pallas import tpu as pltpu
from jax.experimental.pallas import tpu_sc as plsc

D_MODEL = 1024
CHUNK = 64
HEAD_DIM = 64
A_HEADS = 8
A_WIDTH = A_HEADS * HEAD_DIM
LEFT_CHUNKS = 8
MAX_REL = 128
B_HEADS = 4
B_QK_DIM = HEAD_DIM
B_V_DIM = 2 * HEAD_DIM
B_WIDTH = B_HEADS * B_V_DIM
ROPE_THETA = 500000.0
ROPE_DIM = B_QK_DIM // 4
N_EXPERTS = 32
TOP_K = 4
D_EXPERT = D_MODEL
SWIGLU_LIMIT = 7.0
SWIGLU_ALPHA = 1.702
EPS = 1e-6
NEG_INF = -1e30
LOG2_E = math.log2(math.e)
N_MOD = 6

LANES = 128
SUBLANES = 8
VMEM_BYTES_V7X = 64 * 1024 * 1024
VMEM_LIMIT = VMEM_BYTES_V7X * 7 // 8

TOK_TILE = 512
PROJ_TILE = 1024
COMB_TILE = 1024
A_QBLK = 2 * CHUNK
A_BAND = (LEFT_CHUNKS + 2) * CHUNK
A_ROLL = A_BAND + A_QBLK
B_TQ = 512
B_TK = 512
B_SUB = 128
ROW_TILE = 512
ROW_STEP = 128
FF_CHUNK = 512
EXPERT_ROWS = 16384
SC_WINDOW = 128
SC_HALF_WINDOW = 64

F32 = jnp.float32
BF16 = jnp.bfloat16


def _cparams(n_axes, vmem=None):
    return pltpu.CompilerParams(
        dimension_semantics=("arbitrary",) * n_axes,
        vmem_limit_bytes=vmem,
    )


def _ada_kernel(c_ref, w_ref, b_ref, o_ref):
    c = c_ref[...]
    act = c * jax.nn.sigmoid(c)
    o_ref[...] = jnp.dot(act, w_ref[...], preferred_element_type=F32,
                         precision=lax.Precision.HIGHEST) + b_ref[...]


def _ada(c, w_ada, b_ada):
    b = c.shape[0]
    rows = -(-b // SUBLANES) * SUBLANES
    c_pad = jnp.pad(c, ((0, rows - b), (0, 0)))
    n_out = w_ada.shape[1]
    out = pl.pallas_call(
        _ada_kernel,
        grid=(n_out // D_MODEL,),
        in_specs=[
            pl.BlockSpec((rows, D_MODEL), lambda j: (0, 0)),
            pl.BlockSpec((D_MODEL, D_MODEL), lambda j: (0, j)),
            pl.BlockSpec((1, D_MODEL), lambda j: (0, j)),
        ],
        out_specs=pl.BlockSpec((rows, D_MODEL), lambda j: (0, j)),
        out_shape=jax.ShapeDtypeStruct((rows, n_out), F32),
        compiler_params=_cparams(1),
        name="ada",
    )(c_pad, w_ada, b_ada.reshape(1, n_out))
    mod = out[:b].reshape(b, N_MOD, D_MODEL)
    return jnp.pad(mod, ((0, 0), (0, SUBLANES - N_MOD), (0, 0)))


def _in_proj_kernel(x_ref, mod_ref, g_ref, w_ref, rb_ref,
                    qa_ref, ka_ref, va_ref, qb_ref, kb_ref, vb_ref):
    x = x_ref[...]
    mod = mod_ref[...]
    y = x * lax.rsqrt(jnp.mean(x * x, axis=-1, keepdims=True) + EPS) * g_ref[...]
    h = (y * (1.0 + mod[1:2, :]) + mod[0:1, :]).astype(BF16)
    q_scale = HEAD_DIM ** -0.5

    half = ROPE_DIM // 2
    rb = rb_ref[...]
    lane = lax.broadcasted_iota(jnp.int32, rb.shape, 1)
    cos_lo = jnp.where(lane < half, rb, 0.0)
    sin_hi = jnp.where(jnp.logical_and(lane >= half, lane < ROPE_DIM), rb, 0.0)
    cos_pair = cos_lo + pltpu.roll(cos_lo, half, 1)
    rc = (cos_pair + pltpu.roll(cos_pair, B_QK_DIM, 1)
          + jnp.where(lane % B_QK_DIM >= ROPE_DIM, 1.0, 0.0))
    sin_lo = pltpu.roll(sin_hi, LANES - half, 1)
    rm = -(sin_lo + pltpu.roll(sin_lo, B_QK_DIM, 1))
    rp = sin_hi + pltpu.roll(sin_hi, B_QK_DIM, 1)

    def rope(p):
        cols = []
        for s in range(p.shape[1] // LANES):
            v = p[:, s * LANES:(s + 1) * LANES]
            cols.append(v * rc + pltpu.roll(v, LANES - ROPE_DIM // 2, 1) * rm
                        + pltpu.roll(v, ROPE_DIM // 2, 1) * rp)
        return jnp.concatenate(cols, axis=1)

    outs = (qa_ref, ka_ref, va_ref, qb_ref, kb_ref, vb_ref)
    for j, o_ref in enumerate(outs):
        p = jnp.dot(h, w_ref[:, j * A_WIDTH:(j + 1) * A_WIDTH], preferred_element_type=F32)
        if j in (3, 4):
            p = rope(p)
        if j in (0, 3):
            p = p * (q_scale * LOG2_E)
        pb = p.astype(BF16)
        for s in range(A_WIDTH // LANES):
            o_ref[s] = pb[:, s * LANES:(s + 1) * LANES]


def _in_proj(x2, mod, g_mix, w_in_bf, rope_base, seq):
    n = x2.shape[0]
    tiles_per_seq = seq // PROJ_TILE
    row = lambda i: (i, 0)
    fixed = lambda i: (0, 0)
    n_slabs = A_WIDTH // LANES
    out_sd = jax.ShapeDtypeStruct((n_slabs, n, LANES), BF16)
    return pl.pallas_call(
        _in_proj_kernel,
        grid=(n // PROJ_TILE,),
        in_specs=[
            pl.BlockSpec((PROJ_TILE, D_MODEL), row),
            pl.BlockSpec((None, SUBLANES, D_MODEL), lambda i: (i // tiles_per_seq, 0, 0)),
            pl.BlockSpec((1, D_MODEL), fixed),
            pl.BlockSpec(w_in_bf.shape, fixed),
            pl.BlockSpec((PROJ_TILE, LANES), row),
        ],
        out_specs=[pl.BlockSpec((n_slabs, PROJ_TILE, LANES), lambda i: (0, i, 0))] * 6,
        out_shape=[out_sd] * 6,
        compiler_params=_cparams(1, VMEM_LIMIT),
        name="in_proj",
    )(x2, mod, g_mix.reshape(1, D_MODEL), w_in_bf, rope_base)


def _attn_a_kernel(q_ref, kp_ref, kc_ref, vp_ref, vc_ref, bias_ref, o_ref, k_sc, v_sc, bias_sc, *,
                   nblk):
    step = pl.program_id(0)
    g = step % nblk
    n_pairs, blk, _ = q_ref.shape
    lane = lax.broadcasted_iota(jnp.int32, (A_QBLK, LANES), 1)
    col = lax.broadcasted_iota(jnp.int32, (A_QBLK, A_BAND), 1)
    ones = jnp.ones((A_BAND, LANES), BF16)

    @pl.when(step == 0)
    def _():
        q_chunk = lax.broadcasted_iota(jnp.int32, (A_QBLK, A_BAND), 0) // CHUNK
        k_chunk = col // CHUNK
        in_band = jnp.logical_and(k_chunk >= q_chunk, k_chunk <= q_chunk + LEFT_CHUNKS)
        for h in range(bias_ref.shape[0]):
            rolled = pltpu.roll(jnp.broadcast_to(bias_ref[h], (A_QBLK, A_ROLL)), 0, 1,
                                stride=1, stride_axis=0)
            bias_sc[h] = jnp.where(in_band, rolled[:, :A_BAND], NEG_INF)

    def pair(p):
        k_buf = k_sc.at[p % 2]
        v_buf = v_sc.at[p % 2]
        k_buf[0:blk, :] = kp_ref[p]
        k_buf[blk:2 * blk, :] = kc_ref[p]
        v_buf[0:blk, :] = vp_ref[p]
        v_buf[blk:2 * blk, :] = vc_ref[p]

        def scores(m, hh):
            r0 = m * A_QBLK
            q = q_ref[p, r0:r0 + A_QBLK, :]
            in_head = (lane < HEAD_DIM) if hh == 0 else (lane >= HEAD_DIM)
            qh = jnp.where(in_head, q, jnp.zeros_like(q))
            return lax.dot_general(qh, k_buf[r0:r0 + A_BAND, :], (((1,), (1,)), ((), ())),
                                   preferred_element_type=F32)

        chains = [(m, hh) for m in range(blk // A_QBLK) for hh in range(2)]
        s_next = scores(*chains[0])
        halves = []
        for i, (m, hh) in enumerate(chains):
            r0 = m * A_QBLK
            s = s_next
            if i + 1 < len(chains):
                s_next = scores(*chains[i + 1])
            valid = jnp.logical_or(g > 0, col + r0 >= blk)
            s = jnp.where(valid, s + bias_sc[2 * p + hh], NEG_INF)
            pr = jnp.exp2(s - jnp.max(s, axis=1, keepdims=True))
            v_ext = jnp.concatenate([v_buf[r0:r0 + A_BAND, :], ones], axis=1)
            pv = jnp.dot(pr.astype(BF16), v_ext, preferred_element_type=F32)
            halves.append(pv[:, :LANES] / pv[:, LANES:])
            if hh == 1:
                o_ref[p, r0:r0 + A_QBLK, :] = jnp.where(lane < HEAD_DIM, halves[0],
                                                        halves[1]).astype(BF16)
                halves = []

    for p in range(n_pairs):
        pair(p)


def _attn_a(qa, ka, va, bias_rows, batch, seq):
    n_pairs, n, _ = qa.shape
    blk = LEFT_CHUNKS * CHUNK
    nblk = seq // blk
    cur = lambda g: (0, g, 0)
    prev = lambda g: (0, jnp.where(g % nblk == 0, g, g - 1), 0)
    slab = (n_pairs, blk, LANES)
    return pl.pallas_call(
        functools.partial(_attn_a_kernel, nblk=nblk),
        grid=(batch * nblk,),
        in_specs=[
            pl.BlockSpec(slab, cur),
            pl.BlockSpec(slab, prev),
            pl.BlockSpec(slab, cur),
            pl.BlockSpec(slab, prev),
            pl.BlockSpec(slab, cur),
            pl.BlockSpec(bias_rows.shape, lambda g: (0, 0, 0)),
        ],
        out_specs=pl.BlockSpec(slab, cur),
        out_shape=jax.ShapeDtypeStruct((n_pairs, n, LANES), BF16),
        scratch_shapes=[
            pltpu.VMEM((2, 2 * blk, LANES), BF16),
            pltpu.VMEM((2, 2 * blk, LANES), BF16),
            pltpu.VMEM((bias_rows.shape[0], A_QBLK, A_BAND), F32),
        ],
        compiler_params=_cparams(1),
        name="attn_a",
    )(qa, ka, ka, va, va, bias_rows)


def _rel_bias_rows(rel_table):
    t = rel_table.astype(F32) * LOG2_E
    far = t[:, 2 * MAX_REL:]
    n_far = LEFT_CHUNKS * CHUNK - MAX_REL
    row = jnp.concatenate([
        jnp.broadcast_to(far, (t.shape[0], n_far)),
        t[:, 2 * MAX_REL:0:-1],
        jnp.broadcast_to(far, (t.shape[0], A_ROLL - A_BAND)),
    ], axis=1)
    return row.reshape(t.shape[0], 1, A_ROLL)


def _attn_b_kernel(lam_ref, q_ref, k_ref, v_ref, g_ref, o_ref, q_sc, qn_sc, s_sc, m_sc, acc_sc, *,
                   out_scale):
    n_tiles = q_ref.shape[0] // B_TQ
    n_sub = B_TQ // B_SUB
    first_slot = 2
    lane = lax.broadcasted_iota(jnp.int32, (B_TQ, LANES), 1)
    ones = jnp.ones((B_TK, LANES), BF16)
    lam = lam_ref[0]

    def split_components(t, dst):
        q = q_ref[pl.ds(pl.multiple_of(t * B_TQ, B_TQ), B_TQ), :]
        dst[0] = jnp.where(lane < B_QK_DIM, q, jnp.zeros_like(q))
        dst[1] = jnp.where(lane >= B_QK_DIM, q, jnp.zeros_like(q))

    def score_rows(src, k, slot, sub, c):
        rows = pl.ds(sub * B_SUB, B_SUB)
        s_sc[slot, c, rows, :] = lax.dot_general(src[c, rows, :], k, (((1,), (1,)), ((), ())),
                                                 preferred_element_type=F32)

    def update_rows(s, v_ext, sub, c):
        rows = pl.ds(sub * B_SUB, B_SUB)
        m_prev = m_sc[c, rows, :]
        m_new = jnp.maximum(m_prev, jnp.max(s, axis=1, keepdims=True))
        alpha = jnp.exp2(m_prev - m_new)
        p = jnp.exp2(s - jnp.concatenate([m_new] * (s.shape[1] // LANES), axis=1))
        pv = jnp.dot(p.astype(BF16), v_ext, preferred_element_type=F32)
        acc_sc[c, rows, :] = jnp.concatenate([alpha, alpha], axis=1) * acc_sc[c, rows, :] + pv
        m_sc[c, rows, :] = m_new

    def key_block(blk):
        return k_ref[pl.ds(pl.multiple_of(blk * B_TK, B_TK), B_TK), :]

    def value_block(blk):
        return jnp.concatenate([v_ref[pl.ds(pl.multiple_of(blk * B_TK, B_TK), B_TK), :], ones], axis=1)

    def step(blk, slot, next_blk, next_slot):
        v_ext = value_block(blk)
        k_next = key_block(next_blk)
        for sub in range(n_sub):
            for c in range(2):
                update_rows(s_sc[slot, c, pl.ds(sub * B_SUB, B_SUB), :], v_ext, sub, c)
                score_rows(q_sc, k_next, next_slot, sub, c)

    def last_step(t, slot):
        v_ext = value_block(t)
        k0 = key_block(0)
        col_c = lax.broadcasted_iota(jnp.int32, (B_SUB, B_TK), 1) // CHUNK
        row_c = lax.broadcasted_iota(jnp.int32, (B_SUB, B_TK), 0) // CHUNK
        for sub in range(n_sub):
            keep = col_c <= row_c + sub * (B_SUB // CHUNK)
            for c in range(2):
                s = s_sc[slot, c, pl.ds(sub * B_SUB, B_SUB), :]
                update_rows(jnp.where(keep, s, NEG_INF), v_ext, sub, c)
                score_rows(qn_sc, k0, first_slot, sub, c)

    def start_tile(t):
        split_components(t, q_sc)
        split_components(jnp.minimum(t + 1, n_tiles - 1), qn_sc)
        m_sc[...] = jnp.full(m_sc.shape, NEG_INF, F32)
        acc_sc[...] = jnp.zeros(acc_sc.shape, F32)

    def finish_tile(t):
        a0 = acc_sc[0]
        a1 = acc_sc[1]
        o = a0[:, :B_V_DIM] / a0[:, B_V_DIM:] - lam * (a1[:, :B_V_DIM] / a1[:, B_V_DIM:])
        y = o * lax.rsqrt(jnp.mean(o * o, axis=-1, keepdims=True) + EPS) * g_ref[...]
        o_ref[pl.ds(pl.multiple_of(t * B_TQ, B_TQ), B_TQ), :] = (y * out_scale).astype(BF16)

    start_tile(0)
    k0 = key_block(0)
    for sub in range(n_sub):
        for c in range(2):
            score_rows(q_sc, k0, first_slot, sub, c)
    s_sc[0] = s_sc[first_slot]
    last_step(0, 0)
    finish_tile(0)

    def tile(t, carry):
        start_tile(t)
        step(0, first_slot, 1, 1)

        def pair(p, carry):
            odd = 2 * p + 1
            step(odd, 1, odd + 1, 0)
            step(odd + 1, 0, odd + 2, 1)
            return carry

        lax.fori_loop(0, (t - 1) // 2, pair, 0)

        @pl.when(t % 2 == 1)
        def _():
            last_step(t, 1)

        @pl.when(t % 2 == 0)
        def _():
            step(t - 1, 1, t, 0)
            last_step(t, 0)

        finish_tile(t)
        return carry

    lax.fori_loop(1, n_tiles, tile, 0)


def _attn_b(b, lam, qb, kb, vb, g_subln, seq, out_scale):
    assert B_TQ == B_TK and B_V_DIM == LANES
    by_head = lambda h: (h, b, 0)
    return pl.pallas_call(
        functools.partial(_attn_b_kernel, out_scale=out_scale),
        grid=(B_HEADS,),
        in_specs=[
            pl.BlockSpec(memory_space=pltpu.SMEM),
            pl.BlockSpec((None, seq, LANES), by_head),
            pl.BlockSpec((None, seq, LANES), by_head),
            pl.BlockSpec((None, seq, LANES), by_head),
            pl.BlockSpec((1, B_V_DIM), lambda h: (0, 0)),
        ],
        out_specs=pl.BlockSpec((None, seq, LANES), lambda h: (h, 0, 0)),
        out_shape=jax.ShapeDtypeStruct((B_HEADS, seq, LANES), BF16),
        scratch_shapes=[
            pltpu.VMEM((2, B_TQ, LANES), BF16),
            pltpu.VMEM((2, B_TQ, LANES), BF16),
            pltpu.VMEM((3, 2, B_TQ, B_TK), F32),
            pltpu.VMEM((2, B_TQ, LANES), F32),
            pltpu.VMEM((2, B_TQ, 2 * LANES), F32),
        ],
        compiler_params=_cparams(1),
        name="attn_b",
    )(lam, qb, kb, vb, g_subln.reshape(1, B_V_DIM))


def _pack_bf16_pairs(v):
    half = v.shape[1] // 2
    vb = v.astype(BF16)
    hi = lax.bitcast_convert_type(vb[:, :half].astype(F32), jnp.int32)
    lo = lax.bitcast_convert_type(vb[:, half:].astype(F32), jnp.int32)
    return hi | lax.shift_right_logical(lo, jnp.full(lo.shape, 16, jnp.int32))


def _unpack_bf16_pairs(w):
    first = lax.bitcast_convert_type(w & jnp.int32(-65536), F32)
    second = lax.bitcast_convert_type(lax.shift_left(w, jnp.full(w.shape, 16, jnp.int32)), F32)
    return first, second


def _out_route_kernel(oa_ref, ob_ref, x_ref, mod_ref, wo_ref, g_ref, wr_ref, br_ref, cin_ref,
                      x1_in_ref, x1_ref, hp_ref, ti_ref, tw_ref, rk_ref, cnt_ref, tri_sc, carry_sc):
    del x1_in_ref
    i = pl.program_id(0)
    tm = x_ref.shape[0]

    @pl.when(i == 0)
    def _():
        r = lax.broadcasted_iota(jnp.int32, (tm, tm), 0)
        c = lax.broadcasted_iota(jnp.int32, (tm, tm), 1)
        tri_sc[...] = jnp.where(r < c, 1.0, 0.0).astype(BF16)
        carry_sc[...] = cin_ref[...]

    mod = mod_ref[...]
    o = jnp.concatenate([oa_ref[s] for s in range(oa_ref.shape[0])]
                        + [ob_ref[s] for s in range(ob_ref.shape[0])], axis=1)
    mix = jnp.dot(o, wo_ref[...], preferred_element_type=F32)
    x1 = x_ref[...] + mod[2:3, :] * mix
    x1_ref[...] = x1
    y = x1 * lax.rsqrt(jnp.mean(x1 * x1, axis=-1, keepdims=True) + EPS) * g_ref[...]
    h = y * (1.0 + mod[4:5, :]) + mod[3:4, :]
    hb = h.astype(BF16)
    hp_ref[...] = _pack_bf16_pairs(h)

    logits = lax.dot_general(wr_ref[...], hb, (((1,), (1,)), ((), ())),
                             preferred_element_type=F32) + br_ref[...]
    eid = lax.broadcasted_iota(jnp.int32, logits.shape, 0).astype(F32)
    work = logits
    vals, ids = [], []
    chosen = jnp.zeros(logits.shape, F32)
    for _ in range(TOP_K):
        v = jnp.max(work, axis=0, keepdims=True)
        e = jnp.min(jnp.where(work == v, eid, float(N_EXPERTS)), axis=0, keepdims=True)
        hit = eid == e
        vals.append(v)
        ids.append(e)
        chosen = jnp.where(hit, 1.0, chosen)
        work = jnp.where(hit, -jnp.inf, work)
    ex = [jnp.exp(v - vals[0]) for v in vals]
    den = ex[0] + ex[1] + ex[2] + ex[3]

    before = jnp.dot(chosen.astype(BF16), tri_sc[...], preferred_element_type=F32) + carry_sc[...]
    slot = lax.broadcasted_iota(jnp.int32, (SUBLANES, tm), 0)
    ti = jnp.zeros((SUBLANES, tm), F32)
    tw = jnp.zeros((SUBLANES, tm), F32)
    rk = jnp.zeros((SUBLANES, tm), F32)
    for kk in range(TOP_K):
        r_k = jnp.sum(jnp.where(eid == ids[kk], before, 0.0), axis=0, keepdims=True)
        ti = jnp.where(slot == kk, ids[kk], ti)
        tw = jnp.where(slot == kk, ex[kk] / den, tw)
        rk = jnp.where(slot == kk, r_k, rk)
    ti_ref[...] = ti.astype(jnp.int32)
    tw_ref[...] = tw
    rk_ref[...] = rk.astype(jnp.int32)
    carry = carry_sc[...] + jnp.sum(chosen, axis=1, keepdims=True)
    carry_sc[...] = carry
    cnt_ref[...] = carry


def _out_route(b, oa, ob, x2, mod, w_out_bf, g_ffn, w_router_bf, b_router, counts_in, x1_buf, seq):
    n = x2.shape[0]
    tiles = seq // TOK_TILE
    row = lambda i: (b * tiles + i, 0)
    local = lambda i: (i, 0)
    by_lane = lambda i: (0, i)
    fixed = lambda i: (0, 0)
    return pl.pallas_call(
        _out_route_kernel,
        grid=(tiles,),
        in_specs=[
            pl.BlockSpec((oa.shape[0], TOK_TILE, LANES), lambda i: (0, b * tiles + i, 0)),
            pl.BlockSpec((ob.shape[0], TOK_TILE, LANES), lambda i: (0, i, 0)),
            pl.BlockSpec((TOK_TILE, D_MODEL), row),
            pl.BlockSpec((None, SUBLANES, D_MODEL), lambda i: (b, 0, 0)),
            pl.BlockSpec((D_MODEL, D_MODEL), fixed),
            pl.BlockSpec((1, D_MODEL), fixed),
            pl.BlockSpec((N_EXPERTS, D_MODEL), fixed),
            pl.BlockSpec((N_EXPERTS, 1), fixed),
            pl.BlockSpec((N_EXPERTS, 1), fixed),
            pl.BlockSpec(memory_space=pl.ANY),
        ],
        out_specs=[
            pl.BlockSpec((TOK_TILE, D_MODEL), row),
            pl.BlockSpec((TOK_TILE, D_MODEL // 2), local),
            pl.BlockSpec((SUBLANES, TOK_TILE), by_lane),
            pl.BlockSpec((SUBLANES, TOK_TILE), by_lane),
            pl.BlockSpec((SUBLANES, TOK_TILE), by_lane),
            pl.BlockSpec((N_EXPERTS, 1), fixed),
        ],
        out_shape=[
            jax.ShapeDtypeStruct((n, D_MODEL), F32),
            jax.ShapeDtypeStruct((seq, D_MODEL // 2), jnp.int32),
            jax.ShapeDtypeStruct((SUBLANES, seq), jnp.int32),
            jax.ShapeDtypeStruct((SUBLANES, seq), F32),
            jax.ShapeDtypeStruct((SUBLANES, seq), jnp.int32),
            jax.ShapeDtypeStruct((N_EXPERTS, 1), F32),
        ],
        input_output_aliases={9: 0},
        scratch_shapes=[pltpu.VMEM((TOK_TILE, TOK_TILE), BF16), pltpu.VMEM((N_EXPERTS, 1), F32)],
        compiler_params=_cparams(1, VMEM_LIMIT),
        name="out_route",
    )(oa, ob, x2, mod, w_out_bf, g_ffn.reshape(1, D_MODEL), w_router_bf.T,
      b_router.reshape(N_EXPERTS, 1), counts_in, x1_buf)


def _experts_kernel(tiles_ref, rows_ref, base_ref, next_ref, xs_ref, wgu_ref, bgu_ref, wd_ref,
                    bd_ref, y_ref, wgu_sc, wd_sc, x_buf, y_buf, x_sem, y_sem):
    e = pl.program_id(0)
    n_tiles = tiles_ref[e]
    n_rows = rows_ref[e]
    n_live = base_ref[N_EXPERTS]
    region = e * EXPERT_ROWS

    def x_copy(row, slot):
        return pltpu.make_async_copy(xs_ref.at[pl.ds(row, ROW_TILE), :], x_buf.at[slot],
                                     x_sem.at[slot])

    def y_copy(row, slot):
        return pltpu.make_async_copy(y_buf.at[slot], y_ref.at[pl.ds(row, ROW_TILE), :],
                                     y_sem.at[slot])

    @pl.when(e == 0)
    def _():
        y_buf[...] = jnp.zeros(y_buf.shape, jnp.int32)

        @pl.when(n_live > 0)
        def _():
            x_copy(pl.multiple_of(next_ref[N_EXPERTS], ROW_TILE), 0).start()

    @pl.when(n_tiles > 0)
    def _():
        wgu_sc[...] = wgu_ref[...].astype(BF16)
        wd_sc[...] = wd_ref[...].astype(BF16)

    def tile(j, carry):
        g = base_ref[e] + j
        slot = g % 2
        row = pl.multiple_of(region + j * ROW_TILE, ROW_TILE)
        x_copy(row, slot).wait()
        next_row = jnp.where(j + 1 < n_tiles, row + ROW_TILE, next_ref[e])

        @pl.when(g + 1 < n_live)
        def _():
            x_copy(pl.multiple_of(next_row, ROW_TILE), 1 - slot).start()

        @pl.when(g >= 2)
        def _():
            y_copy(0, slot).wait()

        needed = jnp.minimum(n_rows - j * ROW_TILE, ROW_TILE)
        for rows in range(ROW_STEP, ROW_TILE + 1, ROW_STEP):
            @pl.when(jnp.logical_and(needed > rows - ROW_STEP, needed <= rows))
            def _(rows=rows):
                used = lax.broadcasted_iota(jnp.int32, (rows, 1), 0) < needed
                first, second = _unpack_bf16_pairs(jnp.where(used, x_buf[slot, :rows, :], 0))
                x = jnp.concatenate([first.astype(BF16), second.astype(BF16)], axis=1)
                acc = jnp.zeros((rows, D_MODEL), F32)
                for c in range(D_EXPERT // FF_CHUNK):
                    lo_c, hi_c = c * FF_CHUNK, (c + 1) * FF_CHUNK
                    gate = (jnp.dot(x, wgu_sc[:, lo_c:hi_c], preferred_element_type=F32)
                            + bgu_ref[:, lo_c:hi_c])
                    up = (jnp.dot(x, wgu_sc[:, D_EXPERT + lo_c:D_EXPERT + hi_c],
                                  preferred_element_type=F32)
                          + bgu_ref[:, D_EXPERT + lo_c:D_EXPERT + hi_c])
                    gate = jnp.minimum(gate, SWIGLU_LIMIT)
                    up = jnp.clip(up, -SWIGLU_LIMIT, SWIGLU_LIMIT)
                    act = (up + 1.0) * (gate * jax.nn.sigmoid(SWIGLU_ALPHA * gate))
                    acc = acc + jnp.dot(act.astype(BF16), wd_sc[lo_c:hi_c, :],
                                        preferred_element_type=F32)
                y_buf[slot, :rows, :] = _pack_bf16_pairs(acc + bd_ref[...])

        y_copy(row, slot).start()
        return carry

    lax.fori_loop(0, n_tiles, tile, 0)

    @pl.when(e == pl.num_programs(0) - 1)
    def _():
        for back in (2, 1):
            @pl.when(n_live >= back)
            def _():
                y_copy(0, (n_live - back) % 2).wait()


def _experts(counts, xs, w_gate_up, b_gate_up, w_down, b_down):
    n_rows, width = xs.shape
    tiles = (counts + ROW_TILE - 1) // ROW_TILE
    base = jnp.concatenate([jnp.zeros((1,), jnp.int32), jnp.cumsum(tiles)]).astype(jnp.int32)
    region = jnp.arange(N_EXPERTS, dtype=jnp.int32) * EXPERT_ROWS
    later = jnp.arange(N_EXPERTS)[None, :] > jnp.arange(-1, N_EXPERTS)[:, None]
    cand = jnp.where(jnp.logical_and(later, (tiles > 0)[None, :]), region[None, :], n_rows)
    nxt = jnp.min(cand, axis=1)
    next_row = jnp.concatenate([nxt[1:], nxt[:1]]).astype(jnp.int32)
    by_expert = lambda e, *_: (e, 0, 0)
    return pl.pallas_call(
        _experts_kernel,
        grid_spec=pltpu.PrefetchScalarGridSpec(
            num_scalar_prefetch=4,
            grid=(N_EXPERTS,),
            in_specs=[
                pl.BlockSpec(memory_space=pl.ANY),
                pl.BlockSpec((None, D_MODEL, 2 * D_EXPERT), by_expert),
                pl.BlockSpec((None, 1, 2 * D_EXPERT), by_expert),
                pl.BlockSpec((None, D_EXPERT, D_MODEL), by_expert),
                pl.BlockSpec((None, 1, D_MODEL), by_expert),
            ],
            out_specs=pl.BlockSpec(memory_space=pl.ANY),
            scratch_shapes=[
                pltpu.VMEM((D_MODEL, 2 * D_EXPERT), BF16),
                pltpu.VMEM((D_EXPERT, D_MODEL), BF16),
                pltpu.VMEM((2, ROW_TILE, width), jnp.int32),
                pltpu.VMEM((2, ROW_TILE, D_MODEL // 2), jnp.int32),
                pltpu.SemaphoreType.DMA((2,)),
                pltpu.SemaphoreType.DMA((2,)),
            ],
        ),
        out_shape=jax.ShapeDtypeStruct((n_rows, D_MODEL // 2), jnp.int32),
        compiler_params=_cparams(1, VMEM_LIMIT),
        name="experts",
    )(tiles.astype(jnp.int32), counts.astype(jnp.int32), base, next_row, xs, w_gate_up,
      b_gate_up.reshape(N_EXPERTS, 1, 2 * D_EXPERT), w_down, b_down.reshape(N_EXPERTS, 1, D_MODEL))


def _sc_gather_rows(table, idx):
    m = idx.shape[0]
    width = table.shape[1]
    mesh = plsc.VectorSubcoreMesh(core_axis_name="core", subcore_axis_name="subcore")
    n_workers = mesh.num_cores * mesh.num_subcores
    per_worker = m // n_workers
    n_windows = per_worker // SC_HALF_WINDOW
    assert per_worker * n_workers == m and n_windows * SC_HALF_WINDOW == per_worker

    @pl.kernel(
        out_type=jax.ShapeDtypeStruct((m, width), table.dtype),
        mesh=mesh,
        scratch_types=[
            pltpu.VMEM((SC_HALF_WINDOW,), jnp.int32),
            pltpu.VMEM((SC_HALF_WINDOW,), jnp.int32),
            pltpu.VMEM((SC_HALF_WINDOW, width), table.dtype),
            pltpu.VMEM((SC_HALF_WINDOW, width), table.dtype),
            pltpu.SemaphoreType.DMA,
            pltpu.SemaphoreType.DMA,
            pltpu.SemaphoreType.DMA,
            pltpu.SemaphoreType.DMA,
        ],
    )
    def gather_kernel(table_hbm, idx_hbm, out_hbm, idx0, idx1, rows0, rows1, g0, g1, w0, w1):
        worker = lax.axis_index("subcore") * mesh.num_cores + lax.axis_index("core")
        idx_v, rows_v, g_sem, w_sem = (idx0, idx1), (rows0, rows1), (g0, g1), (w0, w1)

        def window(j):
            return pl.ds(pl.multiple_of(worker * per_worker + j * SC_HALF_WINDOW, SC_HALF_WINDOW),
                         SC_HALF_WINDOW)

        def fetch(j):
            s = j % 2
            pltpu.sync_copy(idx_hbm.at[window(j)], idx_v[s])
            return pltpu.async_copy(table_hbm.at[idx_v[s]], rows_v[s], g_sem[s])

        fetches = {0: fetch(0)}
        writes = {}
        for j in range(n_windows):
            s = j % 2
            if j + 1 < n_windows:
                if j >= 1:
                    writes[j - 1].wait()
                fetches[j + 1] = fetch(j + 1)
            fetches[j].wait()
            writes[j] = pltpu.async_copy(rows_v[s], out_hbm.at[window(j)], w_sem[s])
        for j in range(max(n_windows - 2, 0), n_windows):
            writes[j].wait()

    return gather_kernel(table, idx)


def _sc_scatter_rows_into(table, dst, out_ref, reps):
    m, width = table.shape
    mesh = plsc.VectorSubcoreMesh(core_axis_name="core", subcore_axis_name="subcore")
    n_workers = mesh.num_cores * mesh.num_subcores
    per_worker = m // n_workers
    assert per_worker * n_workers == m and per_worker % SC_WINDOW == 0 and dst.shape == (reps * m,)

    @pl.kernel(
        out_type=(),
        mesh=mesh,
        scratch_types=[
            pltpu.VMEM((SC_WINDOW,), jnp.int32),
            pltpu.VMEM((SC_WINDOW, width), table.dtype),
            pltpu.SemaphoreType.DMA,
        ],
    )
    def scatter_kernel(table_hbm, dst_hbm, out_hbm, dst_v, rows_v, sem):
        worker = lax.axis_index("subcore") * mesh.num_cores + lax.axis_index("core")

        @pl.loop(0, per_worker // SC_WINDOW)
        def _(j):
            base = pl.multiple_of(worker * per_worker + j * SC_WINDOW, SC_WINDOW)
            pltpu.sync_copy(table_hbm.at[pl.ds(base, SC_WINDOW)], rows_v)
            for k in range(reps):
                pltpu.sync_copy(dst_hbm.at[pl.ds(k * m + base, SC_WINDOW)], dst_v)
                pltpu.async_copy(rows_v, out_hbm.at[dst_v], sem).wait()

    scatter_kernel(table, dst, out_ref)


def _combine_dense_kernel(tw_ref, x1_ref, mod_ref, g_ref, y0_ref, y1_ref, y2_ref, y3_ref, o_in_ref,
                          o_ref):
    del o_in_ref
    tm = x1_ref.shape[0]
    tw = jnp.concatenate([tw_ref[...], jnp.zeros((LANES - SUBLANES, tm), F32)], axis=0).T
    first = second = None
    for kk, y_ref in enumerate((y0_ref, y1_ref, y2_ref, y3_ref)):
        f_k, s_k = _unpack_bf16_pairs(y_ref[...])
        w_k = tw[:, kk:kk + 1]
        first = w_k * f_k if kk == 0 else first + w_k * f_k
        second = w_k * s_k if kk == 0 else second + w_k * s_k
    ffn = jnp.concatenate([first, second], axis=1)
    x2 = x1_ref[...] + mod_ref[5:6, :] * ffn
    o_ref[...] = x2 * lax.rsqrt(jnp.mean(x2 * x2, axis=-1, keepdims=True) + EPS) * g_ref[...]


def _combine_dense(b, tw, x1, mod, g_final, yg, out_buf, seq):
    n = x1.shape[0]
    width = yg.shape[1]
    tiles = seq // COMB_TILE
    row = lambda i: (b * tiles + i, 0)
    slot = lambda kk: pl.BlockSpec((COMB_TILE, width), lambda i: (kk * tiles + i, 0))
    return pl.pallas_call(
        _combine_dense_kernel,
        grid=(tiles,),
        in_specs=[
            pl.BlockSpec((SUBLANES, COMB_TILE), lambda i: (0, i)),
            pl.BlockSpec((COMB_TILE, D_MODEL), row),
            pl.BlockSpec((None, SUBLANES, D_MODEL), lambda i: (b, 0, 0)),
            pl.BlockSpec((1, D_MODEL), lambda i: (0, 0)),
        ] + [slot(kk) for kk in range(TOP_K)] + [pl.BlockSpec(memory_space=pl.ANY)],
        out_specs=pl.BlockSpec((COMB_TILE, D_MODEL), row),
        out_shape=jax.ShapeDtypeStruct((n, D_MODEL), F32),
        input_output_aliases={4 + TOP_K: 0},
        compiler_params=_cparams(1, VMEM_LIMIT),
        name="combine",
    )(tw, x1, mod, g_final.reshape(1, D_MODEL), yg, yg, yg, yg, out_buf)


def _rope_base(positions):
    half = ROPE_DIM // 2
    inv_freq = ROPE_THETA ** (-jnp.arange(0, ROPE_DIM, 2, dtype=F32) / ROPE_DIM)
    n = positions.size
    pos = jnp.broadcast_to(positions.reshape(n, 1).astype(F32), (n, half)).reshape(-1, LANES)
    ang = pos * jnp.tile(inv_freq, LANES // half)
    cos, sin = lax.optimization_barrier((jnp.cos(ang), jnp.sin(ang)))
    pad = jnp.zeros((n, LANES - ROPE_DIM), F32)
    return jnp.concatenate([cos.reshape(n, half), sin.reshape(n, half), pad], axis=1)


def kernel(x, c, positions, w_ada, b_ada, g_mix_norm, w_in, rel_bias, lambda_q1, lambda_k1,
           lambda_q2, lambda_k2, g_subln, w_out, g_ffn_norm, w_router, b_router, w_gate_up,
           b_gate_up, w_down, b_down, g_final):
    batch, seq, _ = x.shape
    depth = w_ada.shape[0]
    assert depth == 1, "the combine kernel applies the final norm, so it must follow the only layer"
    n = batch * seq
    assert n <= EXPERT_ROWS and EXPERT_ROWS % ROW_TILE == 0
    rope_base = _rope_base(positions)
    x2 = x.reshape(n, D_MODEL)
    for l in range(depth):
        lambda_init = 0.8 - 0.6 * math.exp(-0.3 * l)
        mod = _ada(c, w_ada[l], b_ada[l])
        qa, ka, va, qb, kb, vb = _in_proj(x2, mod, g_mix_norm[l], w_in[l].astype(BF16),
                                          rope_base, seq)
        oa = _attn_a(qa, ka, va, _rel_bias_rows(rel_bias[l]), batch, seq)
        lam = (jnp.exp(jnp.sum(lambda_q1[l].astype(F32) * lambda_k1[l].astype(F32)))
               - jnp.exp(jnp.sum(lambda_q2[l].astype(F32) * lambda_k2[l].astype(F32)))
               + lambda_init).reshape(1)
        w_out_bf = w_out[l].astype(BF16)
        w_router_bf = w_router[l].astype(BF16)
        xs_ref = jax.new_ref(lax.empty((N_EXPERTS * EXPERT_ROWS, D_MODEL // 2), jnp.int32))
        x1 = lax.empty((n, D_MODEL), F32)
        counts = jnp.zeros((N_EXPERTS, 1), F32)
        pos_rows, weight_rows = [], []
        for b in range(batch):
            ob = _attn_b(b, lam, qb, kb, vb, g_subln[l], seq, 1.0 - lambda_init)
            x1, hp, top_i, top_w, rank, counts = _out_route(
                b, oa, ob, x2, mod, w_out_bf, g_ffn_norm[l], w_router_bf, b_router[l], counts, x1,
                seq)
            pos = top_i * EXPERT_ROWS + rank
            pos_rows.append(pos)
            weight_rows.append(top_w)
            dst = jnp.concatenate([pos[kk] for kk in range(TOP_K)])
            _sc_scatter_rows_into(hp, dst, xs_ref, TOP_K)
        xs = jax.freeze(xs_ref)
        y = _experts(counts[:, 0].astype(jnp.int32), xs, w_gate_up[l], b_gate_up[l], w_down[l],
                     b_down[l])
        x2 = lax.empty((n, D_MODEL), F32)
        for b in range(batch):
            yg = _sc_gather_rows(y, jnp.concatenate([pos_rows[b][kk] for kk in range(TOP_K)]))
            x2 = _combine_dense(b, weight_rows[b], x1, mod, g_final, yg, x2, seq)
    return x2.reshape(batch, seq, D_MODEL)
```

```python
import functools
import math

import jax
import jax.numpy as jnp
from jax import lax
from jax.experimental import pallas as pl
from jax.experimental.pallas import tpu as pltpu
from jax.experimental.pallas import tpu_sc as plsc

D_MODEL = 1024
CHUNK = 64
HEAD_DIM = 64
A_HEADS = 8
A_WIDTH = A_HEADS * HEAD_DIM
LEFT_CHUNKS = 8
MAX_REL = 128
B_HEADS = 4
B_QK_DIM = HEAD_DIM
B_V_DIM = 2 * HEAD_DIM
B_WIDTH = B_HEADS * B_V_DIM
ROPE_THETA = 500000.0
ROPE_DIM = B_QK_DIM // 4
N_EXPERTS = 32
TOP_K = 4
D_EXPERT = D_MODEL
SWIGLU_LIMIT = 7.0
SWIGLU_ALPHA = 1.702
EPS = 1e-6
NEG_INF = -1e30
LOG2_E = math.log2(math.e)
N_MOD = 6

LANES = 128
SUBLANES = 8
VMEM_BYTES_V7X = 64 * 1024 * 1024
VMEM_LIMIT = VMEM_BYTES_V7X * 7 // 8

TOK_TILE = 512
A_QBLK = 2 * CHUNK
A_BAND = (LEFT_CHUNKS + 2) * CHUNK
A_ROLL = A_BAND + A_QBLK
B_TQ = 512
B_TK = 512
B_SUB = 128
ROW_TILE = 512
ROW_STEP = 128
FF_CHUNK = 512
EXPERT_ROWS = 16384
SC_WINDOW = 128
SC_HALF_WINDOW = 64

F32 = jnp.float32
BF16 = jnp.bfloat16


def _cparams(n_axes, vmem=None):
    return pltpu.CompilerParams(
        dimension_semantics=("arbitrary",) * n_axes,
        vmem_limit_bytes=vmem,
    )


def _ada_kernel(c_ref, w_ref, b_ref, o_ref):
    c = c_ref[...]
    act = c * jax.nn.sigmoid(c)
    o_ref[...] = jnp.dot(act, w_ref[...], preferred_element_type=F32,
                         precision=lax.Precision.HIGHEST) + b_ref[...]


def _ada(c, w_ada, b_ada):
    b = c.shape[0]
    rows = -(-b // SUBLANES) * SUBLANES
    c_pad = jnp.pad(c, ((0, rows - b), (0, 0)))
    n_out = w_ada.shape[1]
    out = pl.pallas_call(
        _ada_kernel,
        grid=(n_out // D_MODEL,),
        in_specs=[
            pl.BlockSpec((rows, D_MODEL), lambda j: (0, 0)),
            pl.BlockSpec((D_MODEL, D_MODEL), lambda j: (0, j)),
            pl.BlockSpec((1, D_MODEL), lambda j: (0, j)),
        ],
        out_specs=pl.BlockSpec((rows, D_MODEL), lambda j: (0, j)),
        out_shape=jax.ShapeDtypeStruct((rows, n_out), F32),
        compiler_params=_cparams(1),
        name="ada",
    )(c_pad, w_ada, b_ada.reshape(1, n_out))
    mod = out[:b].reshape(b, N_MOD, D_MODEL)
    return jnp.pad(mod, ((0, 0), (0, SUBLANES - N_MOD), (0, 0)))


def _in_proj_kernel(x_ref, mod_ref, g_ref, w_ref, rb_ref,
                    qa_ref, ka_ref, va_ref, qb_ref, kb_ref, vb_ref):
    x = x_ref[...]
    mod = mod_ref[...]
    y = x * lax.rsqrt(jnp.mean(x * x, axis=-1, keepdims=True) + EPS) * g_ref[...]
    h = (y * (1.0 + mod[1:2, :]) + mod[0:1, :]).astype(BF16)
    q_scale = HEAD_DIM ** -0.5

    half = ROPE_DIM // 2
    rb = rb_ref[...]
    lane = lax.broadcasted_iota(jnp.int32, rb.shape, 1)
    cos_lo = jnp.where(lane < half, rb, 0.0)
    sin_hi = jnp.where(jnp.logical_and(lane >= half, lane < ROPE_DIM), rb, 0.0)
    cos_pair = cos_lo + pltpu.roll(cos_lo, half, 1)
    rc = (cos_pair + pltpu.roll(cos_pair, B_QK_DIM, 1)
          + jnp.where(lane % B_QK_DIM >= ROPE_DIM, 1.0, 0.0))
    sin_lo = pltpu.roll(sin_hi, LANES - half, 1)
    rm = -(sin_lo + pltpu.roll(sin_lo, B_QK_DIM, 1))
    rp = sin_hi + pltpu.roll(sin_hi, B_QK_DIM, 1)

    def rope(p):
        cols = []
        for s in range(p.shape[1] // LANES):
            v = p[:, s * LANES:(s + 1) * LANES]
            cols.append(v * rc + pltpu.roll(v, LANES - ROPE_DIM // 2, 1) * rm
                        + pltpu.roll(v, ROPE_DIM // 2, 1) * rp)
        return jnp.concatenate(cols, axis=1)

    outs = (qa_ref, ka_ref, va_ref, qb_ref, kb_ref, vb_ref)
    for j, o_ref in enumerate(outs):
        p = jnp.dot(h, w_ref[:, j * A_WIDTH:(j + 1) * A_WIDTH], preferred_element_type=F32)
        if j in (3, 4):
            p = rope(p)
        if j in (0, 3):
            p = p * (q_scale * LOG2_E)
        pb = p.astype(BF16)
        for s in range(A_WIDTH // LANES):
            o_ref[s] = pb[:, s * LANES:(s + 1) * LANES]


def _in_proj(x2, mod, g_mix, w_in_bf, rope_base, seq):
    n = x2.shape[0]
    tiles_per_seq = seq // TOK_TILE
    row = lambda i: (i, 0)
    fixed = lambda i: (0, 0)
    n_slabs = A_WIDTH // LANES
    out_sd = jax.ShapeDtypeStruct((n_slabs, n, LANES), BF16)
    return pl.pallas_call(
        _in_proj_kernel,
        grid=(n // TOK_TILE,),
        in_specs=[
            pl.BlockSpec((TOK_TILE, D_MODEL), row),
            pl.BlockSpec((None, SUBLANES, D_MODEL), lambda i: (i // tiles_per_seq, 0, 0)),
            pl.BlockSpec((1, D_MODEL), fixed),
            pl.BlockSpec(w_in_bf.shape, fixed),
            pl.BlockSpec((TOK_TILE, LANES), row),
        ],
        out_specs=[pl.BlockSpec((n_slabs, TOK_TILE, LANES), lambda i: (0, i, 0))] * 6,
        out_shape=[out_sd] * 6,
        compiler_params=_cparams(1, VMEM_LIMIT),
        name="in_proj",
    )(x2, mod, g_mix.reshape(1, D_MODEL), w_in_bf, rope_base)


def _attn_a_kernel(q_ref, kp_ref, kc_ref, vp_ref, vc_ref, bias_ref, o_ref, k_sc, v_sc, bias_sc, *,
                   nblk):
    step = pl.program_id(0)
    g = step % nblk
    n_pairs, blk, _ = q_ref.shape
    lane = lax.broadcasted_iota(jnp.int32, (A_QBLK, LANES), 1)
    col = lax.broadcasted_iota(jnp.int32, (A_QBLK, A_BAND), 1)
    ones = jnp.ones((A_BAND, LANES), BF16)

    @pl.when(step == 0)
    def _():
        q_chunk = lax.broadcasted_iota(jnp.int32, (A_QBLK, A_BAND), 0) // CHUNK
        k_chunk = col // CHUNK
        in_band = jnp.logical_and(k_chunk >= q_chunk, k_chunk <= q_chunk + LEFT_CHUNKS)
        for h in range(bias_ref.shape[0]):
            rolled = pltpu.roll(jnp.broadcast_to(bias_ref[h], (A_QBLK, A_ROLL)), 0, 1,
                                stride=1, stride_axis=0)
            bias_sc[h] = jnp.where(in_band, rolled[:, :A_BAND], NEG_INF)

    def pair(p):
        k_buf = k_sc.at[p % 2]
        v_buf = v_sc.at[p % 2]
        k_buf[0:blk, :] = kp_ref[p]
        k_buf[blk:2 * blk, :] = kc_ref[p]
        v_buf[0:blk, :] = vp_ref[p]
        v_buf[blk:2 * blk, :] = vc_ref[p]

        def scores(m, hh):
            r0 = m * A_QBLK
            q = q_ref[p, r0:r0 + A_QBLK, :]
            in_head = (lane < HEAD_DIM) if hh == 0 else (lane >= HEAD_DIM)
            qh = jnp.where(in_head, q, jnp.zeros_like(q))
            return lax.dot_general(qh, k_buf[r0:r0 + A_BAND, :], (((1,), (1,)), ((), ())),
                                   preferred_element_type=F32)

        chains = [(m, hh) for m in range(blk // A_QBLK) for hh in range(2)]
        s_next = scores(*chains[0])
        halves = []
        for i, (m, hh) in enumerate(chains):
            r0 = m * A_QBLK
            s = s_next
            if i + 1 < len(chains):
                s_next = scores(*chains[i + 1])
            valid = jnp.logical_or(g > 0, col + r0 >= blk)
            s = jnp.where(valid, s + bias_sc[2 * p + hh], NEG_INF)
            pr = jnp.exp2(s - jnp.max(s, axis=1, keepdims=True))
            v_ext = jnp.concatenate([v_buf[r0:r0 + A_BAND, :], ones], axis=1)
            pv = jnp.dot(pr.astype(BF16), v_ext, preferred_element_type=F32)
            halves.append(pv[:, :LANES] / pv[:, LANES:])
            if hh == 1:
                o_ref[p, r0:r0 + A_QBLK, :] = jnp.where(lane < HEAD_DIM, halves[0],
                                                        halves[1]).astype(BF16)
                halves = []

    for p in range(n_pairs):
        pair(p)


def _attn_a(qa, ka, va, bias_rows, batch, seq):
    n_pairs, n, _ = qa.shape
    blk = LEFT_CHUNKS * CHUNK
    nblk = seq // blk
    cur = lambda g: (0, g, 0)
    prev = lambda g: (0, jnp.where(g % nblk == 0, g, g - 1), 0)
    slab = (n_pairs, blk, LANES)
    return pl.pallas_call(
        functools.partial(_attn_a_kernel, nblk=nblk),
        grid=(batch * nblk,),
        in_specs=[
            pl.BlockSpec(slab, cur),
            pl.BlockSpec(slab, prev),
            pl.BlockSpec(slab, cur),
            pl.BlockSpec(slab, prev),
            pl.BlockSpec(slab, cur),
            pl.BlockSpec(bias_rows.shape, lambda g: (0, 0, 0)),
        ],
        out_specs=pl.BlockSpec(slab, cur),
        out_shape=jax.ShapeDtypeStruct((n_pairs, n, LANES), BF16),
        scratch_shapes=[
            pltpu.VMEM((2, 2 * blk, LANES), BF16),
            pltpu.VMEM((2, 2 * blk, LANES), BF16),
            pltpu.VMEM((bias_rows.shape[0], A_QBLK, A_BAND), F32),
        ],
        compiler_params=_cparams(1),
        name="attn_a",
    )(qa, ka, ka, va, va, bias_rows)


def _rel_bias_rows(rel_table):
    t = rel_table.astype(F32) * LOG2_E
    far = t[:, 2 * MAX_REL:]
    n_far = LEFT_CHUNKS * CHUNK - MAX_REL
    row = jnp.concatenate([
        jnp.broadcast_to(far, (t.shape[0], n_far)),
        t[:, 2 * MAX_REL:0:-1],
        jnp.broadcast_to(far, (t.shape[0], A_ROLL - A_BAND)),
    ], axis=1)
    return row.reshape(t.shape[0], 1, A_ROLL)


def _attn_b_kernel(lam_ref, q_ref, k_ref, v_ref, g_ref, o_ref, q_sc, qn_sc, s_sc, m_sc, acc_sc, *,
                   out_scale):
    n_tiles = q_ref.shape[0] // B_TQ
    n_sub = B_TQ // B_SUB
    first_slot = 2
    lane = lax.broadcasted_iota(jnp.int32, (B_TQ, LANES), 1)
    ones = jnp.ones((B_TK, LANES), BF16)
    lam = lam_ref[0]

    def split_components(t, dst):
        q = q_ref[pl.ds(pl.multiple_of(t * B_TQ, B_TQ), B_TQ), :]
        dst[0] = jnp.where(lane < B_QK_DIM, q, jnp.zeros_like(q))
        dst[1] = jnp.where(lane >= B_QK_DIM, q, jnp.zeros_like(q))

    def score_rows(src, k, slot, sub, c):
        rows = pl.ds(sub * B_SUB, B_SUB)
        s_sc[slot, c, rows, :] = lax.dot_general(src[c, rows, :], k, (((1,), (1,)), ((), ())),
                                                 preferred_element_type=F32)

    def update_rows(s, v_ext, sub, c):
        rows = pl.ds(sub * B_SUB, B_SUB)
        m_prev = m_sc[c, rows, :]
        m_new = jnp.maximum(m_prev, jnp.max(s, axis=1, keepdims=True))
        alpha = jnp.exp2(m_prev - m_new)
        p = jnp.exp2(s - jnp.concatenate([m_new] * (s.shape[1] // LANES), axis=1))
        pv = jnp.dot(p.astype(BF16), v_ext, preferred_element_type=F32)
        acc_sc[c, rows, :] = jnp.concatenate([alpha, alpha], axis=1) * acc_sc[c, rows, :] + pv
        m_sc[c, rows, :] = m_new

    def key_block(blk):
        return k_ref[pl.ds(pl.multiple_of(blk * B_TK, B_TK), B_TK), :]

    def value_block(blk):
        return jnp.concatenate([v_ref[pl.ds(pl.multiple_of(blk * B_TK, B_TK), B_TK), :], ones], axis=1)

    def step(blk, slot, next_blk, next_slot):
        v_ext = value_block(blk)
        k_next = key_block(next_blk)
        for sub in range(n_sub):
            for c in range(2):
                update_rows(s_sc[slot, c, pl.ds(sub * B_SUB, B_SUB), :], v_ext, sub, c)
                score_rows(q_sc, k_next, next_slot, sub, c)

    def last_step(t, slot):
        v_ext = value_block(t)
        k0 = key_block(0)
        col_c = lax.broadcasted_iota(jnp.int32, (B_SUB, B_TK), 1) // CHUNK
        row_c = lax.broadcasted_iota(jnp.int32, (B_SUB, B_TK), 0) // CHUNK
        for sub in range(n_sub):
            keep = col_c <= row_c + sub * (B_SUB // CHUNK)
            for c in range(2):
                s = s_sc[slot, c, pl.ds(sub * B_SUB, B_SUB), :]
                update_rows(jnp.where(keep, s, NEG_INF), v_ext, sub, c)
                score_rows(qn_sc, k0, first_slot, sub, c)

    def start_tile(t):
        split_components(t, q_sc)
        split_components(jnp.minimum(t + 1, n_tiles - 1), qn_sc)
        m_sc[...] = jnp.full(m_sc.shape, NEG_INF, F32)
        acc_sc[...] = jnp.zeros(acc_sc.shape, F32)

    def finish_tile(t):
        a0 = acc_sc[0]
        a1 = acc_sc[1]
        o = a0[:, :B_V_DIM] / a0[:, B_V_DIM:] - lam * (a1[:, :B_V_DIM] / a1[:, B_V_DIM:])
        y = o * lax.rsqrt(jnp.mean(o * o, axis=-1, keepdims=True) + EPS) * g_ref[...]
        o_ref[pl.ds(pl.multiple_of(t * B_TQ, B_TQ), B_TQ), :] = (y * out_scale).astype(BF16)

    start_tile(0)
    k0 = key_block(0)
    for sub in range(n_sub):
        for c in range(2):
            score_rows(q_sc, k0, first_slot, sub, c)
    s_sc[0] = s_sc[first_slot]
    last_step(0, 0)
    finish_tile(0)

    def tile(t, carry):
        start_tile(t)
        step(0, first_slot, 1, 1)

        def pair(p, carry):
            odd = 2 * p + 1
            step(odd, 1, odd + 1, 0)
            step(odd + 1, 0, odd + 2, 1)
            return carry

        lax.fori_loop(0, (t - 1) // 2, pair, 0)

        @pl.when(t % 2 == 1)
        def _():
            last_step(t, 1)

        @pl.when(t % 2 == 0)
        def _():
            step(t - 1, 1, t, 0)
            last_step(t, 0)

        finish_tile(t)
        return carry

    lax.fori_loop(1, n_tiles, tile, 0)


def _attn_b(b, lam, qb, kb, vb, g_subln, seq, out_scale):
    assert B_TQ == B_TK and B_V_DIM == LANES
    by_head = lambda h: (h, b, 0)
    return pl.pallas_call(
        functools.partial(_attn_b_kernel, out_scale=out_scale),
        grid=(B_HEADS,),
        in_specs=[
            pl.BlockSpec(memory_space=pltpu.SMEM),
            pl.BlockSpec((None, seq, LANES), by_head),
            pl.BlockSpec((None, seq, LANES), by_head),
            pl.BlockSpec((None, seq, LANES), by_head),
            pl.BlockSpec((1, B_V_DIM), lambda h: (0, 0)),
        ],
        out_specs=pl.BlockSpec((None, seq, LANES), lambda h: (h, 0, 0)),
        out_shape=jax.ShapeDtypeStruct((B_HEADS, seq, LANES), BF16),
        scratch_shapes=[
            pltpu.VMEM((2, B_TQ, LANES), BF16),
            pltpu.VMEM((2, B_TQ, LANES), BF16),
            pltpu.VMEM((3, 2, B_TQ, B_TK), F32),
            pltpu.VMEM((2, B_TQ, LANES), F32),
            pltpu.VMEM((2, B_TQ, 2 * LANES), F32),
        ],
        compiler_params=_cparams(1),
        name="attn_b",
    )(lam, qb, kb, vb, g_subln.reshape(1, B_V_DIM))


def _pack_bf16_pairs(v):
    half = v.shape[1] // 2
    vb = v.astype(BF16)
    hi = lax.bitcast_convert_type(vb[:, :half].astype(F32), jnp.int32)
    lo = lax.bitcast_convert_type(vb[:, half:].astype(F32), jnp.int32)
    return hi | lax.shift_right_logical(lo, jnp.full(lo.shape, 16, jnp.int32))


def _unpack_bf16_pairs(w):
    first = lax.bitcast_convert_type(w & jnp.int32(-65536), F32)
    second = lax.bitcast_convert_type(lax.shift_left(w, jnp.full(w.shape, 16, jnp.int32)), F32)
    return first, second


def _out_route_kernel(oa_ref, ob_ref, x_ref, mod_ref, wo_ref, g_ref, wr_ref, br_ref, cin_ref,
                      x1_in_ref, x1_ref, hp_ref, ti_ref, tw_ref, rk_ref, cnt_ref, tri_sc, carry_sc):
    del x1_in_ref
    i = pl.program_id(0)
    tm = x_ref.shape[0]

    @pl.when(i == 0)
    def _():
        r = lax.broadcasted_iota(jnp.int32, (tm, tm), 0)
        c = lax.broadcasted_iota(jnp.int32, (tm, tm), 1)
        tri_sc[...] = jnp.where(r < c, 1.0, 0.0).astype(BF16)
        carry_sc[...] = cin_ref[...]

    mod = mod_ref[...]
    o = jnp.concatenate([oa_ref[s] for s in range(oa_ref.shape[0])]
                        + [ob_ref[s] for s in range(ob_ref.shape[0])], axis=1)
    mix = jnp.dot(o, wo_ref[...], preferred_element_type=F32)
    x1 = x_ref[...] + mod[2:3, :] * mix
    x1_ref[...] = x1
    y = x1 * lax.rsqrt(jnp.mean(x1 * x1, axis=-1, keepdims=True) + EPS) * g_ref[...]
    h = y * (1.0 + mod[4:5, :]) + mod[3:4, :]
    hb = h.astype(BF16)
    hp_ref[...] = _pack_bf16_pairs(h)

    logits = lax.dot_general(wr_ref[...], hb, (((1,), (1,)), ((), ())),
                             preferred_element_type=F32) + br_ref[...]
    eid = lax.broadcasted_iota(jnp.int32, logits.shape, 0).astype(F32)
    work = logits
    vals, ids = [], []
    chosen = jnp.zeros(logits.shape, F32)
    for _ in range(TOP_K):
        v = jnp.max(work, axis=0, keepdims=True)
        e = jnp.min(jnp.where(work == v, eid, float(N_EXPERTS)), axis=0, keepdims=True)
        hit = eid == e
        vals.append(v)
        ids.append(e)
        chosen = jnp.where(hit, 1.0, chosen)
        work = jnp.where(hit, -jnp.inf, work)
    ex = [jnp.exp(v - vals[0]) for v in vals]
    den = ex[0] + ex[1] + ex[2] + ex[3]

    before = jnp.dot(chosen.astype(BF16), tri_sc[...], preferred_element_type=F32) + carry_sc[...]
    slot = lax.broadcasted_iota(jnp.int32, (SUBLANES, tm), 0)
    ti = jnp.zeros((SUBLANES, tm), F32)
    tw = jnp.zeros((SUBLANES, tm), F32)
    rk = jnp.zeros((SUBLANES, tm), F32)
    for kk in range(TOP_K):
        r_k = jnp.sum(jnp.where(eid == ids[kk], before, 0.0), axis=0, keepdims=True)
        ti = jnp.where(slot == kk, ids[kk], ti)
        tw = jnp.where(slot == kk, ex[kk] / den, tw)
        rk = jnp.where(slot == kk, r_k, rk)
    ti_ref[...] = ti.astype(jnp.int32)
    tw_ref[...] = tw
    rk_ref[...] = rk.astype(jnp.int32)
    carry = carry_sc[...] + jnp.sum(chosen, axis=1, keepdims=True)
    carry_sc[...] = carry
    cnt_ref[...] = carry


def _out_route(b, oa, ob, x2, mod, w_out_bf, g_ffn, w_router_bf, b_router, counts_in, x1_buf, seq):
    n = x2.shape[0]
    tiles = seq // TOK_TILE
    row = lambda i: (b * tiles + i, 0)
    local = lambda i: (i, 0)
    by_lane = lambda i: (0, i)
    fixed = lambda i: (0, 0)
    return pl.pallas_call(
        _out_route_kernel,
        grid=(tiles,),
        in_specs=[
            pl.BlockSpec((oa.shape[0], TOK_TILE, LANES), lambda i: (0, b * tiles + i, 0)),
            pl.BlockSpec((ob.shape[0], TOK_TILE, LANES), lambda i: (0, i, 0)),
            pl.BlockSpec((TOK_TILE, D_MODEL), row),
            pl.BlockSpec((None, SUBLANES, D_MODEL), lambda i: (b, 0, 0)),
            pl.BlockSpec((D_MODEL, D_MODEL), fixed),
            pl.BlockSpec((1, D_MODEL), fixed),
            pl.BlockSpec((N_EXPERTS, D_MODEL), fixed),
            pl.BlockSpec((N_EXPERTS, 1), fixed),
            pl.BlockSpec((N_EXPERTS, 1), fixed),
            pl.BlockSpec(memory_space=pl.ANY),
        ],
        out_specs=[
            pl.BlockSpec((TOK_TILE, D_MODEL), row),
            pl.BlockSpec((TOK_TILE, D_MODEL // 2), local),
            pl.BlockSpec((SUBLANES, TOK_TILE), by_lane),
            pl.BlockSpec((SUBLANES, TOK_TILE), by_lane),
            pl.BlockSpec((SUBLANES, TOK_TILE), by_lane),
            pl.BlockSpec((N_EXPERTS, 1), fixed),
        ],
        out_shape=[
            jax.ShapeDtypeStruct((n, D_MODEL), F32),
            jax.ShapeDtypeStruct((seq, D_MODEL // 2), jnp.int32),
            jax.ShapeDtypeStruct((SUBLANES, seq), jnp.int32),
            jax.ShapeDtypeStruct((SUBLANES, seq), F32),
            jax.ShapeDtypeStruct((SUBLANES, seq), jnp.int32),
            jax.ShapeDtypeStruct((N_EXPERTS, 1), F32),
        ],
        input_output_aliases={9: 0},
        scratch_shapes=[pltpu.VMEM((TOK_TILE, TOK_TILE), BF16), pltpu.VMEM((N_EXPERTS, 1), F32)],
        compiler_params=_cparams(1, VMEM_LIMIT),
        name="out_route",
    )(oa, ob, x2, mod, w_out_bf, g_ffn.reshape(1, D_MODEL), w_router_bf.T,
      b_router.reshape(N_EXPERTS, 1), counts_in, x1_buf)


def _experts_kernel(tiles_ref, rows_ref, base_ref, next_ref, xs_ref, wgu_ref, bgu_ref, wd_ref,
                    bd_ref, y_ref, x_buf, y_buf, x_sem, y_sem):
    e = pl.program_id(0)
    n_tiles = tiles_ref[e]
    n_rows = rows_ref[e]
    n_live = base_ref[N_EXPERTS]
    region = e * EXPERT_ROWS

    def x_copy(row, slot):
        return pltpu.make_async_copy(xs_ref.at[pl.ds(row, ROW_TILE), :], x_buf.at[slot],
                                     x_sem.at[slot])

    def y_copy(row, slot):
        return pltpu.make_async_copy(y_buf.at[slot], y_ref.at[pl.ds(row, ROW_TILE), :],
                                     y_sem.at[slot])

    @pl.when(e == 0)
    def _():
        y_buf[...] = jnp.zeros(y_buf.shape, jnp.int32)

        @pl.when(n_live > 0)
        def _():
            x_copy(pl.multiple_of(next_ref[N_EXPERTS], ROW_TILE), 0).start()

    def tile(j, carry):
        g = base_ref[e] + j
        slot = g % 2
        row = pl.multiple_of(region + j * ROW_TILE, ROW_TILE)
        x_copy(row, slot).wait()
        next_row = jnp.where(j + 1 < n_tiles, row + ROW_TILE, next_ref[e])

        @pl.when(g + 1 < n_live)
        def _():
            x_copy(pl.multiple_of(next_row, ROW_TILE), 1 - slot).start()

        @pl.when(g >= 2)
        def _():
            y_copy(0, slot).wait()

        needed = jnp.minimum(n_rows - j * ROW_TILE, ROW_TILE)
        for rows in range(ROW_STEP, ROW_TILE + 1, ROW_STEP):
            @pl.when(jnp.logical_and(needed > rows - ROW_STEP, needed <= rows))
            def _(rows=rows):
                used = lax.broadcasted_iota(jnp.int32, (rows, 1), 0) < needed
                first, second = _unpack_bf16_pairs(jnp.where(used, x_buf[slot, :rows, :], 0))
                x = jnp.concatenate([first, second], axis=1)
                acc = jnp.zeros((rows, D_MODEL), F32)
                for c in range(D_EXPERT // FF_CHUNK):
                    lo_c, hi_c = c * FF_CHUNK, (c + 1) * FF_CHUNK
                    gate = (jnp.dot(x, wgu_ref[:, lo_c:hi_c], preferred_element_type=F32)
                            + bgu_ref[:, lo_c:hi_c])
                    up = (jnp.dot(x, wgu_ref[:, D_EXPERT + lo_c:D_EXPERT + hi_c],
                                  preferred_element_type=F32)
                          + bgu_ref[:, D_EXPERT + lo_c:D_EXPERT + hi_c])
                    gate = jnp.minimum(gate, SWIGLU_LIMIT)
                    up = jnp.clip(up, -SWIGLU_LIMIT, SWIGLU_LIMIT)
                    act = (up + 1.0) * (gate * jax.nn.sigmoid(SWIGLU_ALPHA * gate))
                    acc = acc + jnp.dot(act, wd_ref[lo_c:hi_c, :],
                                        preferred_element_type=F32)
                y_buf[slot, :rows, :] = _pack_bf16_pairs(acc + bd_ref[...])

        y_copy(row, slot).start()
        return carry

    lax.fori_loop(0, n_tiles, tile, 0)

    @pl.when(e == pl.num_programs(0) - 1)
    def _():
        for back in (2, 1):
            @pl.when(n_live >= back)
            def _():
                y_copy(0, (n_live - back) % 2).wait()


def _experts(counts, xs, w_gate_up, b_gate_up, w_down, b_down):
    n_rows, width = xs.shape
    tiles = (counts + ROW_TILE - 1) // ROW_TILE
    base = jnp.concatenate([jnp.zeros((1,), jnp.int32), jnp.cumsum(tiles)]).astype(jnp.int32)
    region = jnp.arange(N_EXPERTS, dtype=jnp.int32) * EXPERT_ROWS
    later = jnp.arange(N_EXPERTS)[None, :] > jnp.arange(-1, N_EXPERTS)[:, None]
    cand = jnp.where(jnp.logical_and(later, (tiles > 0)[None, :]), region[None, :], n_rows)
    nxt = jnp.min(cand, axis=1)
    next_row = jnp.concatenate([nxt[1:], nxt[:1]]).astype(jnp.int32)
    by_expert = lambda e, *_: (e, 0, 0)
    return pl.pallas_call(
        _experts_kernel,
        grid_spec=pltpu.PrefetchScalarGridSpec(
            num_scalar_prefetch=4,
            grid=(N_EXPERTS,),
            in_specs=[
                pl.BlockSpec(memory_space=pl.ANY),
                pl.BlockSpec((None, D_MODEL, 2 * D_EXPERT), by_expert),
                pl.BlockSpec((None, 1, 2 * D_EXPERT), by_expert),
                pl.BlockSpec((None, D_EXPERT, D_MODEL), by_expert),
                pl.BlockSpec((None, 1, D_MODEL), by_expert),
            ],
            out_specs=pl.BlockSpec(memory_space=pl.ANY),
            scratch_shapes=[
                pltpu.VMEM((2, ROW_TILE, width), jnp.int32),
                pltpu.VMEM((2, ROW_TILE, D_MODEL // 2), jnp.int32),
                pltpu.SemaphoreType.DMA((2,)),
                pltpu.SemaphoreType.DMA((2,)),
            ],
        ),
        out_shape=jax.ShapeDtypeStruct((n_rows, D_MODEL // 2), jnp.int32),
        compiler_params=_cparams(1, VMEM_LIMIT),
        name="experts",
    )(tiles.astype(jnp.int32), counts.astype(jnp.int32), base, next_row, xs, w_gate_up,
      b_gate_up.reshape(N_EXPERTS, 1, 2 * D_EXPERT), w_down, b_down.reshape(N_EXPERTS, 1, D_MODEL))


def _sc_gather_rows(table, idx):
    m = idx.shape[0]
    width = table.shape[1]
    mesh = plsc.VectorSubcoreMesh(core_axis_name="core", subcore_axis_name="subcore")
    n_workers = mesh.num_cores * mesh.num_subcores
    per_worker = m // n_workers
    n_windows = per_worker // SC_HALF_WINDOW
    assert per_worker * n_workers == m and n_windows * SC_HALF_WINDOW == per_worker

    @pl.kernel(
        out_type=jax.ShapeDtypeStruct((m, width), table.dtype),
        mesh=mesh,
        scratch_types=[
            pltpu.VMEM((SC_HALF_WINDOW,), jnp.int32),
            pltpu.VMEM((SC_HALF_WINDOW,), jnp.int32),
            pltpu.VMEM((SC_HALF_WINDOW, width), table.dtype),
            pltpu.VMEM((SC_HALF_WINDOW, width), table.dtype),
            pltpu.SemaphoreType.DMA,
            pltpu.SemaphoreType.DMA,
            pltpu.SemaphoreType.DMA,
            pltpu.SemaphoreType.DMA,
        ],
    )
    def gather_kernel(table_hbm, idx_hbm, out_hbm, idx0, idx1, rows0, rows1, g0, g1, w0, w1):
        worker = lax.axis_index("subcore") * mesh.num_cores + lax.axis_index("core")
        idx_v, rows_v, g_sem, w_sem = (idx0, idx1), (rows0, rows1), (g0, g1), (w0, w1)

        def window(j):
            return pl.ds(pl.multiple_of(worker * per_worker + j * SC_HALF_WINDOW, SC_HALF_WINDOW),
                         SC_HALF_WINDOW)

        def fetch(j):
            s = j % 2
            pltpu.sync_copy(idx_hbm.at[window(j)], idx_v[s])
            return pltpu.async_copy(table_hbm.at[idx_v[s]], rows_v[s], g_sem[s])

        fetches = {0: fetch(0)}
        writes = {}
        for j in range(n_windows):
            s = j % 2
            if j + 1 < n_windows:
                if j >= 1:
                    writes[j - 1].wait()
                fetches[j + 1] = fetch(j + 1)
            fetches[j].wait()
            writes[j] = pltpu.async_copy(rows_v[s], out_hbm.at[window(j)], w_sem[s])
        for j in range(max(n_windows - 2, 0), n_windows):
            writes[j].wait()

    return gather_kernel(table, idx)


def _sc_scatter_rows_into(table, dst, out_ref, reps):
    m, width = table.shape
    mesh = plsc.VectorSubcoreMesh(core_axis_name="core", subcore_axis_name="subcore")
    n_workers = mesh.num_cores * mesh.num_subcores
    per_worker = m // n_workers
    assert per_worker * n_workers == m and per_worker % SC_WINDOW == 0 and dst.shape == (reps * m,)

    @pl.kernel(
        out_type=(),
        mesh=mesh,
        scratch_types=[
            pltpu.VMEM((SC_WINDOW,), jnp.int32),
            pltpu.VMEM((SC_WINDOW, width), table.dtype),
            pltpu.SemaphoreType.DMA,
        ],
    )
    def scatter_kernel(table_hbm, dst_hbm, out_hbm, dst_v, rows_v, sem):
        worker = lax.axis_index("subcore") * mesh.num_cores + lax.axis_index("core")

        @pl.loop(0, per_worker // SC_WINDOW)
        def _(j):
            base = pl.multiple_of(worker * per_worker + j * SC_WINDOW, SC_WINDOW)
            pltpu.sync_copy(table_hbm.at[pl.ds(base, SC_WINDOW)], rows_v)
            for k in range(reps):
                pltpu.sync_copy(dst_hbm.at[pl.ds(k * m + base, SC_WINDOW)], dst_v)
                pltpu.async_copy(rows_v, out_hbm.at[dst_v], sem).wait()

    scatter_kernel(table, dst, out_ref)


def _combine_dense_kernel(tw_ref, x1_ref, mod_ref, g_ref, y0_ref, y1_ref, y2_ref, y3_ref, o_in_ref,
                          o_ref):
    del o_in_ref
    tm = x1_ref.shape[0]
    tw = jnp.concatenate([tw_ref[...], jnp.zeros((LANES - SUBLANES, tm), F32)], axis=0).T
    first = second = None
    for kk, y_ref in enumerate((y0_ref, y1_ref, y2_ref, y3_ref)):
        f_k, s_k = _unpack_bf16_pairs(y_ref[...])
        w_k = tw[:, kk:kk + 1]
        first = w_k * f_k if kk == 0 else first + w_k * f_k
        second = w_k * s_k if kk == 0 else second + w_k * s_k
    ffn = jnp.concatenate([first, second], axis=1)
    x2 = x1_ref[...] + mod_ref[5:6, :] * ffn
    o_ref[...] = x2 * lax.rsqrt(jnp.mean(x2 * x2, axis=-1, keepdims=True) + EPS) * g_ref[...]


def _combine_dense(b, tw, x1, mod, g_final, yg, out_buf, seq):
    n = x1.shape[0]
    width = yg.shape[1]
    tiles = seq // TOK_TILE
    row = lambda i: (b * tiles + i, 0)
    slot = lambda kk: pl.BlockSpec((TOK_TILE, width), lambda i: (kk * tiles + i, 0))
    return pl.pallas_call(
        _combine_dense_kernel,
        grid=(tiles,),
        in_specs=[
            pl.BlockSpec((SUBLANES, TOK_TILE), lambda i: (0, i)),
            pl.BlockSpec((TOK_TILE, D_MODEL), row),
            pl.BlockSpec((None, SUBLANES, D_MODEL), lambda i: (b, 0, 0)),
            pl.BlockSpec((1, D_MODEL), lambda i: (0, 0)),
        ] + [slot(kk) for kk in range(TOP_K)] + [pl.BlockSpec(memory_space=pl.ANY)],
        out_specs=pl.BlockSpec((TOK_TILE, D_MODEL), row),
        out_shape=jax.ShapeDtypeStruct((n, D_MODEL), F32),
        input_output_aliases={4 + TOP_K: 0},
        compiler_params=_cparams(1, VMEM_LIMIT),
        name="combine",
    )(tw, x1, mod, g_final.reshape(1, D_MODEL), yg, yg, yg, yg, out_buf)


def _rope_base(positions):
    half = ROPE_DIM // 2
    inv_freq = ROPE_THETA ** (-jnp.arange(0, ROPE_DIM, 2, dtype=F32) / ROPE_DIM)
    n = positions.size
    pos = jnp.broadcast_to(positions.reshape(n, 1).astype(F32), (n, half)).reshape(-1, LANES)
    ang = pos * jnp.tile(inv_freq, LANES // half)
    cos, sin = lax.optimization_barrier((jnp.cos(ang), jnp.sin(ang)))
    pad = jnp.zeros((n, LANES - ROPE_DIM), F32)
    return jnp.concatenate([cos.reshape(n, half), sin.reshape(n, half), pad], axis=1)


def kernel(x, c, positions, w_ada, b_ada, g_mix_norm, w_in, rel_bias, lambda_q1, lambda_k1,
           lambda_q2, lambda_k2, g_subln, w_out, g_ffn_norm, w_router, b_router, w_gate_up,
           b_gate_up, w_down, b_down, g_final):
    batch, seq, _ = x.shape
    depth = w_ada.shape[0]
    assert depth == 1, "the combine kernel applies the final norm, so it must follow the only layer"
    n = batch * seq
    assert n <= EXPERT_ROWS and EXPERT_ROWS % ROW_TILE == 0
    rope_base = _rope_base(positions)
    x2 = x.reshape(n, D_MODEL)
    for l in range(depth):
        lambda_init = 0.8 - 0.6 * math.exp(-0.3 * l)
        mod = _ada(c, w_ada[l], b_ada[l])
        qa, ka, va, qb, kb, vb = _in_proj(x2, mod, g_mix_norm[l], w_in[l].astype(BF16),
                                          rope_base, seq)
        oa = _attn_a(qa, ka, va, _rel_bias_rows(rel_bias[l]), batch, seq)
        lam = (jnp.exp(jnp.sum(lambda_q1[l].astype(F32) * lambda_k1[l].astype(F32)))
               - jnp.exp(jnp.sum(lambda_q2[l].astype(F32) * lambda_k2[l].astype(F32)))
               + lambda_init).reshape(1)
        w_out_bf = w_out[l].astype(BF16)
        w_router_bf = w_router[l].astype(BF16)
        xs_ref = jax.new_ref(lax.empty((N_EXPERTS * EXPERT_ROWS, D_MODEL // 2), jnp.int32))
        x1 = lax.empty((n, D_MODEL), F32)
        counts = jnp.zeros((N_EXPERTS, 1), F32)
        pos_rows, weight_rows = [], []
        for b in range(batch):
            ob = _attn_b(b, lam, qb, kb, vb, g_subln[l], seq, 1.0 - lambda_init)
            x1, hp, top_i, top_w, rank, counts = _out_route(
                b, oa, ob, x2, mod, w_out_bf, g_ffn_norm[l], w_router_bf, b_router[l], counts, x1,
                seq)
            pos = top_i * EXPERT_ROWS + rank
            pos_rows.append(pos)
            weight_rows.append(top_w)
            dst = jnp.concatenate([pos[kk] for kk in range(TOP_K)])
            _sc_scatter_rows_into(hp, dst, xs_ref, TOP_K)
        xs = jax.freeze(xs_ref)
        y = _experts(counts[:, 0].astype(jnp.int32), xs, w_gate_up[l], b_gate_up[l], w_down[l],
                     b_down[l])
        x2 = lax.empty((n, D_MODEL), F32)
        for b in range(batch):
            yg = _sc_gather_rows(y, jnp.concatenate([pos_rows[b][kk] for kk in range(TOP_K)]))
            x2 = _combine_dense(b, weight_rows[b], x1, mod, g_final, yg, x2, seq)
    return x2.reshape(batch, seq, D_MODEL)
```

```python
import functools
import math

import jax
import jax.numpy as jnp
from jax import lax
from jax.experimental import pallas as pl
from jax.experimental.pallas import tpu as pltpu
from jax.experimental.pallas import tpu_sc as plsc

D_MODEL = 1024
CHUNK = 64
HEAD_DIM = 64
A_HEADS = 8
A_WIDTH = A_HEADS * HEAD_DIM
LEFT_CHUNKS = 8
MAX_REL = 128
B_HEADS = 4
B_QK_DIM = HEAD_DIM
B_V_DIM = 2 * HEAD_DIM
B_WIDTH = B_HEADS * B_V_DIM
ROPE_THETA = 500000.0
ROPE_DIM = B_QK_DIM // 4
N_EXPERTS = 32
TOP_K = 4
D_EXPERT = D_MODEL
SWIGLU_LIMIT = 7.0
SWIGLU_ALPHA = 1.702
EPS = 1e-6
NEG_INF = -1e30
LOG2_E = math.log2(math.e)
N_MOD = 6

LANES = 128
SUBLANES = 8
VMEM_BYTES_V7X = 64 * 1024 * 1024
VMEM_LIMIT = VMEM_BYTES_V7X * 7 // 8

TOK_TILE = 512
A_QBLK = 2 * CHUNK
A_BAND = (LEFT_CHUNKS + 2) * CHUNK
A_ROLL = A_BAND + A_QBLK
B_TQ = 512
B_TK = 512
B_SUB = 128
ROW_TILE = 512
ROW_STEP = 128
FF_CHUNK = 512
EXPERT_ROWS = 16384
SC_WINDOW = 128
SC_HALF_WINDOW = 64

F32 = jnp.float32
BF16 = jnp.bfloat16


def _cparams(n_axes, vmem=None):
    return pltpu.CompilerParams(
        dimension_semantics=("arbitrary",) * n_axes,
        vmem_limit_bytes=vmem,
    )


def _ada_kernel(c_ref, w_ref, b_ref, o_ref):
    c = c_ref[...]
    act = c * jax.nn.sigmoid(c)
    o_ref[...] = jnp.dot(act, w_ref[...], preferred_element_type=F32,
                         precision=lax.Precision.HIGHEST) + b_ref[...]


def _ada(c, w_ada, b_ada):
    b = c.shape[0]
    rows = -(-b // SUBLANES) * SUBLANES
    c_pad = jnp.pad(c, ((0, rows - b), (0, 0)))
    n_out = w_ada.shape[1]
    out = pl.pallas_call(
        _ada_kernel,
        grid=(n_out // D_MODEL,),
        in_specs=[
            pl.BlockSpec((rows, D_MODEL), lambda j: (0, 0)),
            pl.BlockSpec((D_MODEL, D_MODEL), lambda j: (0, j)),
            pl.BlockSpec((1, D_MODEL), lambda j: (0, j)),
        ],
        out_specs=pl.BlockSpec((rows, D_MODEL), lambda j: (0, j)),
        out_shape=jax.ShapeDtypeStruct((rows, n_out), F32),
        compiler_params=_cparams(1),
        name="ada",
    )(c_pad, w_ada, b_ada.reshape(1, n_out))
    mod = out[:b].reshape(b, N_MOD, D_MODEL)
    return jnp.pad(mod, ((0, 0), (0, SUBLANES - N_MOD), (0, 0)))


def _in_proj_kernel(x_ref, mod_ref, g_ref, w_ref, rb_ref,
                    qa_ref, ka_ref, va_ref, qb_ref, kb_ref, vb_ref):
    x = x_ref[...]
    mod = mod_ref[...]
    y = x * lax.rsqrt(jnp.mean(x * x, axis=-1, keepdims=True) + EPS) * g_ref[...]
    h = y * (1.0 + mod[1:2, :]) + mod[0:1, :]
    q_scale = HEAD_DIM ** -0.5

    half = ROPE_DIM // 2
    rb = rb_ref[...]
    lane = lax.broadcasted_iota(jnp.int32, rb.shape, 1)
    cos_lo = jnp.where(lane < half, rb, 0.0)
    sin_hi = jnp.where(jnp.logical_and(lane >= half, lane < ROPE_DIM), rb, 0.0)
    cos_pair = cos_lo + pltpu.roll(cos_lo, half, 1)
    rc = (cos_pair + pltpu.roll(cos_pair, B_QK_DIM, 1)
          + jnp.where(lane % B_QK_DIM >= ROPE_DIM, 1.0, 0.0))
    sin_lo = pltpu.roll(sin_hi, LANES - half, 1)
    rm = -(sin_lo + pltpu.roll(sin_lo, B_QK_DIM, 1))
    rp = sin_hi + pltpu.roll(sin_hi, B_QK_DIM, 1)

    def rope(p):
        cols = []
        for s in range(p.shape[1] // LANES):
            v = p[:, s * LANES:(s + 1) * LANES]
            cols.append(v * rc + pltpu.roll(v, LANES - ROPE_DIM // 2, 1) * rm
                        + pltpu.roll(v, ROPE_DIM // 2, 1) * rp)
        return jnp.concatenate(cols, axis=1)

    outs = (qa_ref, ka_ref, va_ref, qb_ref, kb_ref, vb_ref)
    for j, o_ref in enumerate(outs):
        p = jnp.dot(h, w_ref[:, j * A_WIDTH:(j + 1) * A_WIDTH], preferred_element_type=F32)
        if j in (3, 4):
            p = rope(p)
        if j in (0, 3):
            p = p * (q_scale * LOG2_E)
        pb = p.astype(BF16)
        for s in range(A_WIDTH // LANES):
            o_ref[s] = pb[:, s * LANES:(s + 1) * LANES]


def _in_proj(x2, mod, g_mix, w_in, rope_base, seq):
    n = x2.shape[0]
    tiles_per_seq = seq // TOK_TILE
    row = lambda i: (i, 0)
    fixed = lambda i: (0, 0)
    n_slabs = A_WIDTH // LANES
    out_sd = jax.ShapeDtypeStruct((n_slabs, n, LANES), BF16)
    return pl.pallas_call(
        _in_proj_kernel,
        grid=(n // TOK_TILE,),
        in_specs=[
            pl.BlockSpec((TOK_TILE, D_MODEL), row),
            pl.BlockSpec((None, SUBLANES, D_MODEL), lambda i: (i // tiles_per_seq, 0, 0)),
            pl.BlockSpec((1, D_MODEL), fixed),
            pl.BlockSpec(w_in.shape, fixed),
            pl.BlockSpec((TOK_TILE, LANES), row),
        ],
        out_specs=[pl.BlockSpec((n_slabs, TOK_TILE, LANES), lambda i: (0, i, 0))] * 6,
        out_shape=[out_sd] * 6,
        compiler_params=_cparams(1, VMEM_LIMIT),
        name="in_proj",
    )(x2, mod, g_mix.reshape(1, D_MODEL), w_in, rope_base)


def _attn_a_kernel(q_ref, kp_ref, kc_ref, vp_ref, vc_ref, bias_ref, o_ref, k_sc, v_sc, bias_sc, *,
                   nblk):
    step = pl.program_id(0)
    g = step % nblk
    n_pairs, blk, _ = q_ref.shape
    lane = lax.broadcasted_iota(jnp.int32, (A_QBLK, LANES), 1)
    col = lax.broadcasted_iota(jnp.int32, (A_QBLK, A_BAND), 1)
    ones = jnp.ones((A_BAND, LANES), BF16)

    @pl.when(step == 0)
    def _():
        q_chunk = lax.broadcasted_iota(jnp.int32, (A_QBLK, A_BAND), 0) // CHUNK
        k_chunk = col // CHUNK
        in_band = jnp.logical_and(k_chunk >= q_chunk, k_chunk <= q_chunk + LEFT_CHUNKS)
        for h in range(bias_ref.shape[0]):
            rolled = pltpu.roll(jnp.broadcast_to(bias_ref[h], (A_QBLK, A_ROLL)), 0, 1,
                                stride=1, stride_axis=0)
            bias_sc[h] = jnp.where(in_band, rolled[:, :A_BAND], NEG_INF)

    def pair(p):
        k_buf = k_sc.at[p % 2]
        v_buf = v_sc.at[p % 2]
        k_buf[0:blk, :] = kp_ref[p]
        k_buf[blk:2 * blk, :] = kc_ref[p]
        v_buf[0:blk, :] = vp_ref[p]
        v_buf[blk:2 * blk, :] = vc_ref[p]

        def scores(m, hh):
            r0 = m * A_QBLK
            q = q_ref[p, r0:r0 + A_QBLK, :]
            in_head = (lane < HEAD_DIM) if hh == 0 else (lane >= HEAD_DIM)
            qh = jnp.where(in_head, q, jnp.zeros_like(q))
            return lax.dot_general(qh, k_buf[r0:r0 + A_BAND, :], (((1,), (1,)), ((), ())),
                                   preferred_element_type=F32)

        chains = [(m, hh) for m in range(blk // A_QBLK) for hh in range(2)]
        s_next = scores(*chains[0])
        halves = []
        for i, (m, hh) in enumerate(chains):
            r0 = m * A_QBLK
            s = s_next
            if i + 1 < len(chains):
                s_next = scores(*chains[i + 1])
            valid = jnp.logical_or(g > 0, col + r0 >= blk)
            s = jnp.where(valid, s + bias_sc[2 * p + hh], NEG_INF)
            pr = jnp.exp2(s - jnp.max(s, axis=1, keepdims=True))
            v_ext = jnp.concatenate([v_buf[r0:r0 + A_BAND, :], ones], axis=1)
            pv = jnp.dot(pr.astype(BF16), v_ext, preferred_element_type=F32)
            halves.append(pv[:, :LANES] / pv[:, LANES:])
            if hh == 1:
                o_ref[p, r0:r0 + A_QBLK, :] = jnp.where(lane < HEAD_DIM, halves[0],
                                                        halves[1]).astype(BF16)
                halves = []

    for p in range(n_pairs):
        pair(p)


def _attn_a(qa, ka, va, bias_rows, batch, seq):
    n_pairs, n, _ = qa.shape
    blk = LEFT_CHUNKS * CHUNK
    nblk = seq // blk
    cur = lambda g: (0, g, 0)
    prev = lambda g: (0, jnp.where(g % nblk == 0, g, g - 1), 0)
    slab = (n_pairs, blk, LANES)
    return pl.pallas_call(
        functools.partial(_attn_a_kernel, nblk=nblk),
        grid=(batch * nblk,),
        in_specs=[
            pl.BlockSpec(slab, cur),
            pl.BlockSpec(slab, prev),
            pl.BlockSpec(slab, cur),
            pl.BlockSpec(slab, prev),
            pl.BlockSpec(slab, cur),
            pl.BlockSpec(bias_rows.shape, lambda g: (0, 0, 0)),
        ],
        out_specs=pl.BlockSpec(slab, cur),
        out_shape=jax.ShapeDtypeStruct((n_pairs, n, LANES), BF16),
        scratch_shapes=[
            pltpu.VMEM((2, 2 * blk, LANES), BF16),
            pltpu.VMEM((2, 2 * blk, LANES), BF16),
            pltpu.VMEM((bias_rows.shape[0], A_QBLK, A_BAND), F32),
        ],
        compiler_params=_cparams(1),
        name="attn_a",
    )(qa, ka, ka, va, va, bias_rows)


def _rel_bias_rows(rel_table):
    t = rel_table.astype(F32) * LOG2_E
    far = t[:, 2 * MAX_REL:]
    n_far = LEFT_CHUNKS * CHUNK - MAX_REL
    row = jnp.concatenate([
        jnp.broadcast_to(far, (t.shape[0], n_far)),
        t[:, 2 * MAX_REL:0:-1],
        jnp.broadcast_to(far, (t.shape[0], A_ROLL - A_BAND)),
    ], axis=1)
    return row.reshape(t.shape[0], 1, A_ROLL)


def _attn_b_kernel(lam_ref, q_ref, k_ref, v_ref, g_ref, o_ref, q_sc, qn_sc, s_sc, m_sc, acc_sc, *,
                   out_scale):
    n_tiles = q_ref.shape[0] // B_TQ
    n_sub = B_TQ // B_SUB
    first_slot = 2
    lane = lax.broadcasted_iota(jnp.int32, (B_TQ, LANES), 1)
    ones = jnp.ones((B_TK, LANES), BF16)
    lam = lam_ref[0]

    def split_components(t, dst):
        q = q_ref[pl.ds(pl.multiple_of(t * B_TQ, B_TQ), B_TQ), :]
        dst[0] = jnp.where(lane < B_QK_DIM, q, jnp.zeros_like(q))
        dst[1] = jnp.where(lane >= B_QK_DIM, q, jnp.zeros_like(q))

    def score_rows(src, k, slot, sub, c):
        rows = pl.ds(sub * B_SUB, B_SUB)
        s_sc[slot, c, rows, :] = lax.dot_general(src[c, rows, :], k, (((1,), (1,)), ((), ())),
                                                 preferred_element_type=F32)

    def update_rows(s, v_ext, sub, c):
        rows = pl.ds(sub * B_SUB, B_SUB)
        m_prev = m_sc[c, rows, :]
        m_new = jnp.maximum(m_prev, jnp.max(s, axis=1, keepdims=True))
        alpha = jnp.exp2(m_prev - m_new)
        p = jnp.exp2(s - jnp.concatenate([m_new] * (s.shape[1] // LANES), axis=1))
        pv = jnp.dot(p.astype(BF16), v_ext, preferred_element_type=F32)
        acc_sc[c, rows, :] = jnp.concatenate([alpha, alpha], axis=1) * acc_sc[c, rows, :] + pv
        m_sc[c, rows, :] = m_new

    def key_block(blk):
        return k_ref[pl.ds(pl.multiple_of(blk * B_TK, B_TK), B_TK), :]

    def value_block(blk):
        return jnp.concatenate([v_ref[pl.ds(pl.multiple_of(blk * B_TK, B_TK), B_TK), :], ones], axis=1)

    def step(blk, slot, next_blk, next_slot):
        v_ext = value_block(blk)
        k_next = key_block(next_blk)
        for sub in range(n_sub):
            for c in range(2):
                update_rows(s_sc[slot, c, pl.ds(sub * B_SUB, B_SUB), :], v_ext, sub, c)
                score_rows(q_sc, k_next, next_slot, sub, c)

    def last_step(t, slot):
        v_ext = value_block(t)
        k0 = key_block(0)
        col_c = lax.broadcasted_iota(jnp.int32, (B_SUB, B_TK), 1) // CHUNK
        row_c = lax.broadcasted_iota(jnp.int32, (B_SUB, B_TK), 0) // CHUNK
        for sub in range(n_sub):
            keep = col_c <= row_c + sub * (B_SUB // CHUNK)
            for c in range(2):
                s = s_sc[slot, c, pl.ds(sub * B_SUB, B_SUB), :]
                update_rows(jnp.where(keep, s, NEG_INF), v_ext, sub, c)
                score_rows(qn_sc, k0, first_slot, sub, c)

    def start_tile(t):
        split_components(t, q_sc)
        split_components(jnp.minimum(t + 1, n_tiles - 1), qn_sc)
        m_sc[...] = jnp.full(m_sc.shape, NEG_INF, F32)
        acc_sc[...] = jnp.zeros(acc_sc.shape, F32)

    def finish_tile(t):
        a0 = acc_sc[0]
        a1 = acc_sc[1]
        o = a0[:, :B_V_DIM] / a0[:, B_V_DIM:] - lam * (a1[:, :B_V_DIM] / a1[:, B_V_DIM:])
        y = o * lax.rsqrt(jnp.mean(o * o, axis=-1, keepdims=True) + EPS) * g_ref[...]
        o_ref[pl.ds(pl.multiple_of(t * B_TQ, B_TQ), B_TQ), :] = (y * out_scale).astype(BF16)

    start_tile(0)
    k0 = key_block(0)
    for sub in range(n_sub):
        for c in range(2):
            score_rows(q_sc, k0, first_slot, sub, c)
    s_sc[0] = s_sc[first_slot]
    last_step(0, 0)
    finish_tile(0)

    def tile(t, carry):
        start_tile(t)
        step(0, first_slot, 1, 1)

        def pair(p, carry):
            odd = 2 * p + 1
            step(odd, 1, odd + 1, 0)
            step(odd + 1, 0, odd + 2, 1)
            return carry

        lax.fori_loop(0, (t - 1) // 2, pair, 0)

        @pl.when(t % 2 == 1)
        def _():
            last_step(t, 1)

        @pl.when(t % 2 == 0)
        def _():
            step(t - 1, 1, t, 0)
            last_step(t, 0)

        finish_tile(t)
        return carry

    lax.fori_loop(1, n_tiles, tile, 0)


def _attn_b(b, lam, qb, kb, vb, g_subln, seq, out_scale):
    assert B_TQ == B_TK and B_V_DIM == LANES
    by_head = lambda h: (h, b, 0)
    return pl.pallas_call(
        functools.partial(_attn_b_kernel, out_scale=out_scale),
        grid=(B_HEADS,),
        in_specs=[
            pl.BlockSpec(memory_space=pltpu.SMEM),
            pl.BlockSpec((None, seq, LANES), by_head),
            pl.BlockSpec((None, seq, LANES), by_head),
            pl.BlockSpec((None, seq, LANES), by_head),
            pl.BlockSpec((1, B_V_DIM), lambda h: (0, 0)),
        ],
        out_specs=pl.BlockSpec((None, seq, LANES), lambda h: (h, 0, 0)),
        out_shape=jax.ShapeDtypeStruct((B_HEADS, seq, LANES), BF16),
        scratch_shapes=[
            pltpu.VMEM((2, B_TQ, LANES), BF16),
            pltpu.VMEM((2, B_TQ, LANES), BF16),
            pltpu.VMEM((3, 2, B_TQ, B_TK), F32),
            pltpu.VMEM((2, B_TQ, LANES), F32),
            pltpu.VMEM((2, B_TQ, 2 * LANES), F32),
        ],
        compiler_params=_cparams(1),
        name="attn_b",
    )(lam, qb, kb, vb, g_subln.reshape(1, B_V_DIM))


def _pack_bf16_pairs(v):
    half = v.shape[1] // 2
    vb = v.astype(BF16)
    hi = lax.bitcast_convert_type(vb[:, :half].astype(F32), jnp.int32)
    lo = lax.bitcast_convert_type(vb[:, half:].astype(F32), jnp.int32)
    return hi | lax.shift_right_logical(lo, jnp.full(lo.shape, 16, jnp.int32))


def _unpack_bf16_pairs(w):
    first = lax.bitcast_convert_type(w & jnp.int32(-65536), F32)
    second = lax.bitcast_convert_type(lax.shift_left(w, jnp.full(w.shape, 16, jnp.int32)), F32)
    return first, second


def _out_route_kernel(oa_ref, ob_ref, x_ref, mod_ref, wo_ref, g_ref, wr_ref, br_ref, cin_ref,
                      x1_in_ref, x1_ref, hp_ref, ti_ref, tw_ref, rk_ref, cnt_ref, tri_sc, carry_sc):
    del x1_in_ref
    i = pl.program_id(0)
    tm = x_ref.shape[0]

    @pl.when(i == 0)
    def _():
        r = lax.broadcasted_iota(jnp.int32, (tm, tm), 0)
        c = lax.broadcasted_iota(jnp.int32, (tm, tm), 1)
        tri_sc[...] = jnp.where(r < c, 1.0, 0.0).astype(BF16)
        carry_sc[...] = cin_ref[...]

    mod = mod_ref[...]
    o = jnp.concatenate([oa_ref[s] for s in range(oa_ref.shape[0])]
                        + [ob_ref[s] for s in range(ob_ref.shape[0])], axis=1)
    mix = jnp.dot(o, wo_ref[...], preferred_element_type=F32)
    x1 = x_ref[...] + mod[2:3, :] * mix
    x1_ref[...] = x1
    y = x1 * lax.rsqrt(jnp.mean(x1 * x1, axis=-1, keepdims=True) + EPS) * g_ref[...]
    h = y * (1.0 + mod[4:5, :]) + mod[3:4, :]
    hb = h.astype(BF16)
    hp_ref[...] = _pack_bf16_pairs(h)

    logits = lax.dot_general(wr_ref[...], hb, (((1,), (1,)), ((), ())),
                             preferred_element_type=F32) + br_ref[...]
    eid = lax.broadcasted_iota(jnp.int32, logits.shape, 0).astype(F32)
    work = logits
    vals, ids = [], []
    chosen = jnp.zeros(logits.shape, F32)
    for _ in range(TOP_K):
        v = jnp.max(work, axis=0, keepdims=True)
        e = jnp.min(jnp.where(work == v, eid, float(N_EXPERTS)), axis=0, keepdims=True)
        hit = eid == e
        vals.append(v)
        ids.append(e)
        chosen = jnp.where(hit, 1.0, chosen)
        work = jnp.where(hit, -jnp.inf, work)
    ex = [jnp.exp(v - vals[0]) for v in vals]
    den = ex[0] + ex[1] + ex[2] + ex[3]

    before = jnp.dot(chosen.astype(BF16), tri_sc[...], preferred_element_type=F32) + carry_sc[...]
    slot = lax.broadcasted_iota(jnp.int32, (SUBLANES, tm), 0)
    ti = jnp.zeros((SUBLANES, tm), F32)
    tw = jnp.zeros((SUBLANES, tm), F32)
    rk = jnp.zeros((SUBLANES, tm), F32)
    for kk in range(TOP_K):
        r_k = jnp.sum(jnp.where(eid == ids[kk], before, 0.0), axis=0, keepdims=True)
        ti = jnp.where(slot == kk, ids[kk], ti)
        tw = jnp.where(slot == kk, ex[kk] / den, tw)
        rk = jnp.where(slot == kk, r_k, rk)
    ti_ref[...] = ti.astype(jnp.int32)
    tw_ref[...] = tw
    rk_ref[...] = rk.astype(jnp.int32)
    carry = carry_sc[...] + jnp.sum(chosen, axis=1, keepdims=True)
    carry_sc[...] = carry
    cnt_ref[...] = carry


def _out_route(b, oa, ob, x2, mod, w_out_bf, g_ffn, w_router_bf, b_router, counts_in, x1_buf, seq):
    n = x2.shape[0]
    tiles = seq // TOK_TILE
    row = lambda i: (b * tiles + i, 0)
    local = lambda i: (i, 0)
    by_lane = lambda i: (0, i)
    fixed = lambda i: (0, 0)
    return pl.pallas_call(
        _out_route_kernel,
        grid=(tiles,),
        in_specs=[
            pl.BlockSpec((oa.shape[0], TOK_TILE, LANES), lambda i: (0, b * tiles + i, 0)),
            pl.BlockSpec((ob.shape[0], TOK_TILE, LANES), lambda i: (0, i, 0)),
            pl.BlockSpec((TOK_TILE, D_MODEL), row),
            pl.BlockSpec((None, SUBLANES, D_MODEL), lambda i: (b, 0, 0)),
            pl.BlockSpec((D_MODEL, D_MODEL), fixed),
            pl.BlockSpec((1, D_MODEL), fixed),
            pl.BlockSpec((N_EXPERTS, D_MODEL), fixed),
            pl.BlockSpec((N_EXPERTS, 1), fixed),
            pl.BlockSpec((N_EXPERTS, 1), fixed),
            pl.BlockSpec(memory_space=pl.ANY),
        ],
        out_specs=[
            pl.BlockSpec((TOK_TILE, D_MODEL), row),
            pl.BlockSpec((TOK_TILE, D_MODEL // 2), local),
            pl.BlockSpec((SUBLANES, TOK_TILE), by_lane),
            pl.BlockSpec((SUBLANES, TOK_TILE), by_lane),
            pl.BlockSpec((SUBLANES, TOK_TILE), by_lane),
            pl.BlockSpec((N_EXPERTS, 1), fixed),
        ],
        out_shape=[
            jax.ShapeDtypeStruct((n, D_MODEL), F32),
            jax.ShapeDtypeStruct((seq, D_MODEL // 2), jnp.int32),
            jax.ShapeDtypeStruct((SUBLANES, seq), jnp.int32),
            jax.ShapeDtypeStruct((SUBLANES, seq), F32),
            jax.ShapeDtypeStruct((SUBLANES, seq), jnp.int32),
            jax.ShapeDtypeStruct((N_EXPERTS, 1), F32),
        ],
        input_output_aliases={9: 0},
        scratch_shapes=[pltpu.VMEM((TOK_TILE, TOK_TILE), BF16), pltpu.VMEM((N_EXPERTS, 1), F32)],
        compiler_params=_cparams(1, VMEM_LIMIT),
        name="out_route",
    )(oa, ob, x2, mod, w_out_bf, g_ffn.reshape(1, D_MODEL), w_router_bf.T,
      b_router.reshape(N_EXPERTS, 1), counts_in, x1_buf)


def _experts_kernel(tiles_ref, rows_ref, base_ref, next_ref, xs_ref, wgu_ref, bgu_ref, wd_ref,
                    bd_ref, y_ref, x_buf, y_buf, x_sem, y_sem):
    e = pl.program_id(0)
    n_tiles = tiles_ref[e]
    n_rows = rows_ref[e]
    n_live = base_ref[N_EXPERTS]
    region = e * EXPERT_ROWS

    def x_copy(row, slot):
        return pltpu.make_async_copy(xs_ref.at[pl.ds(row, ROW_TILE), :], x_buf.at[slot],
                                     x_sem.at[slot])

    def y_copy(row, slot):
        return pltpu.make_async_copy(y_buf.at[slot], y_ref.at[pl.ds(row, ROW_TILE), :],
                                     y_sem.at[slot])

    @pl.when(e == 0)
    def _():
        y_buf[...] = jnp.zeros(y_buf.shape, jnp.int32)

        @pl.when(n_live > 0)
        def _():
            x_copy(pl.multiple_of(next_ref[N_EXPERTS], ROW_TILE), 0).start()

    def tile(j, carry):
        g = base_ref[e] + j
        slot = g % 2
        row = pl.multiple_of(region + j * ROW_TILE, ROW_TILE)
        x_copy(row, slot).wait()
        next_row = jnp.where(j + 1 < n_tiles, row + ROW_TILE, next_ref[e])

        @pl.when(g + 1 < n_live)
        def _():
            x_copy(pl.multiple_of(next_row, ROW_TILE), 1 - slot).start()

        @pl.when(g >= 2)
        def _():
            y_copy(0, slot).wait()

        needed = jnp.minimum(n_rows - j * ROW_TILE, ROW_TILE)
        for rows in range(ROW_STEP, ROW_TILE + 1, ROW_STEP):
            @pl.when(jnp.logical_and(needed > rows - ROW_STEP, needed <= rows))
            def _(rows=rows):
                used = lax.broadcasted_iota(jnp.int32, (rows, 1), 0) < needed
                first, second = _unpack_bf16_pairs(jnp.where(used, x_buf[slot, :rows, :], 0))
                x = jnp.concatenate([first, second], axis=1)
                acc = jnp.zeros((rows, D_MODEL), F32)
                for c in range(D_EXPERT // FF_CHUNK):
                    lo_c, hi_c = c * FF_CHUNK, (c + 1) * FF_CHUNK
                    gate = (jnp.dot(x, wgu_ref[:, lo_c:hi_c], preferred_element_type=F32)
                            + bgu_ref[:, lo_c:hi_c])
                    up = (jnp.dot(x, wgu_ref[:, D_EXPERT + lo_c:D_EXPERT + hi_c],
                                  preferred_element_type=F32)
                          + bgu_ref[:, D_EXPERT + lo_c:D_EXPERT + hi_c])
                    gate = jnp.minimum(gate, SWIGLU_LIMIT)
                    up = jnp.clip(up, -SWIGLU_LIMIT, SWIGLU_LIMIT)
                    act = (up + 1.0) * (gate * jax.nn.sigmoid(SWIGLU_ALPHA * gate))
                    acc = acc + jnp.dot(act, wd_ref[lo_c:hi_c, :],
                                        preferred_element_type=F32)
                y_buf[slot, :rows, :] = _pack_bf16_pairs(acc + bd_ref[...])

        y_copy(row, slot).start()
        return carry

    lax.fori_loop(0, n_tiles, tile, 0)

    @pl.when(e == pl.num_programs(0) - 1)
    def _():
        for back in (2, 1):
            @pl.when(n_live >= back)
            def _():
                y_copy(0, (n_live - back) % 2).wait()


def _experts(counts, xs, w_gate_up, b_gate_up, w_down, b_down):
    n_rows, width = xs.shape
    tiles = (counts + ROW_TILE - 1) // ROW_TILE
    base = jnp.concatenate([jnp.zeros((1,), jnp.int32), jnp.cumsum(tiles)]).astype(jnp.int32)
    region = jnp.arange(N_EXPERTS, dtype=jnp.int32) * EXPERT_ROWS
    later = jnp.arange(N_EXPERTS)[None, :] > jnp.arange(-1, N_EXPERTS)[:, None]
    cand = jnp.where(jnp.logical_and(later, (tiles > 0)[None, :]), region[None, :], n_rows)
    nxt = jnp.min(cand, axis=1)
    next_row = jnp.concatenate([nxt[1:], nxt[:1]]).astype(jnp.int32)
    by_expert = lambda e, *_: (e, 0, 0)
    return pl.pallas_call(
        _experts_kernel,
        grid_spec=pltpu.PrefetchScalarGridSpec(
            num_scalar_prefetch=4,
            grid=(N_EXPERTS,),
            in_specs=[
                pl.BlockSpec(memory_space=pl.ANY),
                pl.BlockSpec((None, D_MODEL, 2 * D_EXPERT), by_expert),
                pl.BlockSpec((None, 1, 2 * D_EXPERT), by_expert),
                pl.BlockSpec((None, D_EXPERT, D_MODEL), by_expert),
                pl.BlockSpec((None, 1, D_MODEL), by_expert),
            ],
            out_specs=pl.BlockSpec(memory_space=pl.ANY),
            scratch_shapes=[
                pltpu.VMEM((2, ROW_TILE, width), jnp.int32),
                pltpu.VMEM((2, ROW_TILE, D_MODEL // 2), jnp.int32),
                pltpu.SemaphoreType.DMA((2,)),
                pltpu.SemaphoreType.DMA((2,)),
            ],
        ),
        out_shape=jax.ShapeDtypeStruct((n_rows, D_MODEL // 2), jnp.int32),
        compiler_params=_cparams(1, VMEM_LIMIT),
        name="experts",
    )(tiles.astype(jnp.int32), counts.astype(jnp.int32), base, next_row, xs, w_gate_up,
      b_gate_up.reshape(N_EXPERTS, 1, 2 * D_EXPERT), w_down, b_down.reshape(N_EXPERTS, 1, D_MODEL))


def _sc_gather_rows(table, idx):
    m = idx.shape[0]
    width = table.shape[1]
    mesh = plsc.VectorSubcoreMesh(core_axis_name="core", subcore_axis_name="subcore")
    n_workers = mesh.num_cores * mesh.num_subcores
    per_worker = m // n_workers
    n_windows = per_worker // SC_HALF_WINDOW
    assert per_worker * n_workers == m and n_windows * SC_HALF_WINDOW == per_worker

    @pl.kernel(
        out_type=jax.ShapeDtypeStruct((m, width), table.dtype),
        mesh=mesh,
        scratch_types=[
            pltpu.VMEM((SC_HALF_WINDOW,), jnp.int32),
            pltpu.VMEM((SC_HALF_WINDOW,), jnp.int32),
            pltpu.VMEM((SC_HALF_WINDOW, width), table.dtype),
            pltpu.VMEM((SC_HALF_WINDOW, width), table.dtype),
            pltpu.SemaphoreType.DMA,
            pltpu.SemaphoreType.DMA,
            pltpu.SemaphoreType.DMA,
            pltpu.SemaphoreType.DMA,
        ],
    )
    def gather_kernel(table_hbm, idx_hbm, out_hbm, idx0, idx1, rows0, rows1, g0, g1, w0, w1):
        worker = lax.axis_index("subcore") * mesh.num_cores + lax.axis_index("core")
        idx_v, rows_v, g_sem, w_sem = (idx0, idx1), (rows0, rows1), (g0, g1), (w0, w1)

        def window(j):
            return pl.ds(pl.multiple_of(worker * per_worker + j * SC_HALF_WINDOW, SC_HALF_WINDOW),
                         SC_HALF_WINDOW)

        def fetch(j):
            s = j % 2
            pltpu.sync_copy(idx_hbm.at[window(j)], idx_v[s])
            return pltpu.async_copy(table_hbm.at[idx_v[s]], rows_v[s], g_sem[s])

        fetches = {0: fetch(0)}
        writes = {}
        for j in range(n_windows):
            s = j % 2
            if j + 1 < n_windows:
                if j >= 1:
                    writes[j - 1].wait()
                fetches[j + 1] = fetch(j + 1)
            fetches[j].wait()
            writes[j] = pltpu.async_copy(rows_v[s], out_hbm.at[window(j)], w_sem[s])
        for j in range(max(n_windows - 2, 0), n_windows):
            writes[j].wait()

    return gather_kernel(table, idx)


def _sc_scatter_rows_into(table, dst, out_ref, reps):
    m, width = table.shape
    mesh = plsc.VectorSubcoreMesh(core_axis_name="core", subcore_axis_name="subcore")
    n_workers = mesh.num_cores * mesh.num_subcores
    per_worker = m // n_workers
    assert per_worker * n_workers == m and per_worker % SC_WINDOW == 0 and dst.shape == (reps * m,)

    @pl.kernel(
        out_type=(),
        mesh=mesh,
        scratch_types=[
            pltpu.VMEM((SC_WINDOW,), jnp.int32),
            pltpu.VMEM((SC_WINDOW, width), table.dtype),
            pltpu.SemaphoreType.DMA,
        ],
    )
    def scatter_kernel(table_hbm, dst_hbm, out_hbm, dst_v, rows_v, sem):
        worker = lax.axis_index("subcore") * mesh.num_cores + lax.axis_index("core")

        @pl.loop(0, per_worker // SC_WINDOW)
        def _(j):
            base = pl.multiple_of(worker * per_worker + j * SC_WINDOW, SC_WINDOW)
            pltpu.sync_copy(table_hbm.at[pl.ds(base, SC_WINDOW)], rows_v)
            for k in range(reps):
                pltpu.sync_copy(dst_hbm.at[pl.ds(k * m + base, SC_WINDOW)], dst_v)
                pltpu.async_copy(rows_v, out_hbm.at[dst_v], sem).wait()

    scatter_kernel(table, dst, out_ref)


def _combine_dense_kernel(tw_ref, x1_ref, mod_ref, g_ref, y0_ref, y1_ref, y2_ref, y3_ref, o_in_ref,
                          o_ref):
    del o_in_ref
    tm = x1_ref.shape[0]
    tw = jnp.concatenate([tw_ref[...], jnp.zeros((LANES - SUBLANES, tm), F32)], axis=0).T
    first = second = None
    for kk, y_ref in enumerate((y0_ref, y1_ref, y2_ref, y3_ref)):
        f_k, s_k = _unpack_bf16_pairs(y_ref[...])
        w_k = tw[:, kk:kk + 1]
        first = w_k * f_k if kk == 0 else first + w_k * f_k
        second = w_k * s_k if kk == 0 else second + w_k * s_k
    ffn = jnp.concatenate([first, second], axis=1)
    x2 = x1_ref[...] + mod_ref[5:6, :] * ffn
    o_ref[...] = x2 * lax.rsqrt(jnp.mean(x2 * x2, axis=-1, keepdims=True) + EPS) * g_ref[...]


def _combine_dense(b, tw, x1, mod, g_final, yg, out_buf, seq):
    n = x1.shape[0]
    width = yg.shape[1]
    tiles = seq // TOK_TILE
    row = lambda i: (b * tiles + i, 0)
    slot = lambda kk: pl.BlockSpec((TOK_TILE, width), lambda i: (kk * tiles + i, 0))
    return pl.pallas_call(
        _combine_dense_kernel,
        grid=(tiles,),
        in_specs=[
            pl.BlockSpec((SUBLANES, TOK_TILE), lambda i: (0, i)),
            pl.BlockSpec((TOK_TILE, D_MODEL), row),
            pl.BlockSpec((None, SUBLANES, D_MODEL), lambda i: (b, 0, 0)),
            pl.BlockSpec((1, D_MODEL), lambda i: (0, 0)),
        ] + [slot(kk) for kk in range(TOP_K)] + [pl.BlockSpec(memory_space=pl.ANY)],
        out_specs=pl.BlockSpec((TOK_TILE, D_MODEL), row),
        out_shape=jax.ShapeDtypeStruct((n, D_MODEL), F32),
        input_output_aliases={4 + TOP_K: 0},
        compiler_params=_cparams(1, VMEM_LIMIT),
        name="combine",
    )(tw, x1, mod, g_final.reshape(1, D_MODEL), yg, yg, yg, yg, out_buf)


def _rope_base(positions):
    half = ROPE_DIM // 2
    inv_freq = ROPE_THETA ** (-jnp.arange(0, ROPE_DIM, 2, dtype=F32) / ROPE_DIM)
    n = positions.size
    pos = jnp.broadcast_to(positions.reshape(n, 1).astype(F32), (n, half)).reshape(-1, LANES)
    ang = pos * jnp.tile(inv_freq, LANES // half)
    cos, sin = lax.optimization_barrier((jnp.cos(ang), jnp.sin(ang)))
    pad = jnp.zeros((n, LANES - ROPE_DIM), F32)
    return jnp.concatenate([cos.reshape(n, half), sin.reshape(n, half), pad], axis=1)


def kernel(x, c, positions, w_ada, b_ada, g_mix_norm, w_in, rel_bias, lambda_q1, lambda_k1,
           lambda_q2, lambda_k2, g_subln, w_out, g_ffn_norm, w_router, b_router, w_gate_up,
           b_gate_up, w_down, b_down, g_final):
    batch, seq, _ = x.shape
    depth = w_ada.shape[0]
    assert depth == 1, "the combine kernel applies the final norm, so it must follow the only layer"
    n = batch * seq
    assert n <= EXPERT_ROWS and EXPERT_ROWS % ROW_TILE == 0
    rope_base = _rope_base(positions)
    x2 = x.reshape(n, D_MODEL)
    for l in range(depth):
        lambda_init = 0.8 - 0.6 * math.exp(-0.3 * l)
        mod = _ada(c, w_ada[l], b_ada[l])
        qa, ka, va, qb, kb, vb = _in_proj(x2, mod, g_mix_norm[l], w_in[l],
                                          rope_base, seq)
        oa = _attn_a(qa, ka, va, _rel_bias_rows(rel_bias[l]), batch, seq)
        lam = (jnp.exp(jnp.sum(lambda_q1[l].astype(F32) * lambda_k1[l].astype(F32)))
               - jnp.exp(jnp.sum(lambda_q2[l].astype(F32) * lambda_k2[l].astype(F32)))
               + lambda_init).reshape(1)
        w_out_bf = w_out[l].astype(BF16)
        w_router_bf = w_router[l].astype(BF16)
        xs_ref = jax.new_ref(lax.empty((N_EXPERTS * EXPERT_ROWS, D_MODEL // 2), jnp.int32))
        x1 = lax.empty((n, D_MODEL), F32)
        counts = jnp.zeros((N_EXPERTS, 1), F32)
        pos_rows, weight_rows = [], []
        for b in range(batch):
            ob = _attn_b(b, lam, qb, kb, vb, g_subln[l], seq, 1.0 - lambda_init)
            x1, hp, top_i, top_w, rank, counts = _out_route(
                b, oa, ob, x2, mod, w_out_bf, g_ffn_norm[l], w_router_bf, b_router[l], counts, x1,
                seq)
            pos = top_i * EXPERT_ROWS + rank
            pos_rows.append(pos)
            weight_rows.append(top_w)
            dst = jnp.concatenate([pos[kk] for kk in range(TOP_K)])
            _sc_scatter_rows_into(hp, dst, xs_ref, TOP_K)
        xs = jax.freeze(xs_ref)
        y = _experts(counts[:, 0].astype(jnp.int32), xs, w_gate_up[l], b_gate_up[l], w_down[l],
                     b_down[l])
        x2 = lax.empty((n, D_MODEL), F32)
        for b in range(batch):
            yg = _sc_gather_rows(y, jnp.concatenate([pos_rows[b][kk] for kk in range(TOP_K)]))
            x2 = _combine_dense(b, weight_rows[b], x1, mod, g_final, yg, x2, seq)
    return x2.reshape(batch, seq, D_MODEL)
```

```python
import functools
import math

import jax
import jax.numpy as jnp
from jax import lax
from jax.experimental import pallas as pl
from jax.experimental.pallas import tpu as pltpu
from jax.experimental.pallas import tpu_sc as plsc

D_MODEL = 1024
CHUNK = 64
HEAD_DIM = 64
A_HEADS = 8
A_WIDTH = A_HEADS * HEAD_DIM
LEFT_CHUNKS = 8
MAX_REL = 128
B_HEADS = 4
B_QK_DIM = HEAD_DIM
B_V_DIM = 2 * HEAD_DIM
B_WIDTH = B_HEADS * B_V_DIM
ROPE_THETA = 500000.0
ROPE_DIM = B_QK_DIM // 4
N_EXPERTS = 32
TOP_K = 4
D_EXPERT = D_MODEL
SWIGLU_LIMIT = 7.0
SWIGLU_ALPHA = 1.702
EPS = 1e-6
NEG_INF = -1e30
LOG2_E = math.log2(math.e)
N_MOD = 6

LANES = 128
SUBLANES = 8
VMEM_BYTES_V7X = 64 * 1024 * 1024
VMEM_LIMIT = VMEM_BYTES_V7X * 7 // 8

TOK_TILE = 512
A_QBLK = 2 * CHUNK
A_BAND = (LEFT_CHUNKS + 2) * CHUNK
A_ROLL = A_BAND + A_QBLK
B_TQ = 512
B_TK = 512
B_SUB = 128
ROW_TILE = 512
ROW_STEP = 128
FF_CHUNK = 512
EXPERT_ROWS = 16384
COMBINE_BUFFERS = 3
SC_WINDOW = 128
SC_HALF_WINDOW = 64

F32 = jnp.float32
BF16 = jnp.bfloat16


def _cparams(n_axes, vmem=None):
    return pltpu.CompilerParams(
        dimension_semantics=("arbitrary",) * n_axes,
        vmem_limit_bytes=vmem,
    )


def _ada_kernel(c_ref, w_ref, b_ref, o_ref):
    c = c_ref[...]
    act = c * jax.nn.sigmoid(c)
    o_ref[...] = jnp.dot(act, w_ref[...], preferred_element_type=F32,
                         precision=lax.Precision.HIGHEST) + b_ref[...]


def _ada(c, w_ada, b_ada):
    b = c.shape[0]
    rows = -(-b // SUBLANES) * SUBLANES
    c_pad = jnp.pad(c, ((0, rows - b), (0, 0)))
    n_out = w_ada.shape[1]
    out = pl.pallas_call(
        _ada_kernel,
        grid=(n_out // D_MODEL,),
        in_specs=[
            pl.BlockSpec((rows, D_MODEL), lambda j: (0, 0)),
            pl.BlockSpec((D_MODEL, D_MODEL), lambda j: (0, j)),
            pl.BlockSpec((1, D_MODEL), lambda j: (0, j)),
        ],
        out_specs=pl.BlockSpec((rows, D_MODEL), lambda j: (0, j)),
        out_shape=jax.ShapeDtypeStruct((rows, n_out), F32),
        compiler_params=_cparams(1),
        name="ada",
    )(c_pad, w_ada, b_ada.reshape(1, n_out))
    mod = out[:b].reshape(b, N_MOD, D_MODEL)
    return jnp.pad(mod, ((0, 0), (0, SUBLANES - N_MOD), (0, 0)))


def _in_proj_kernel(x_ref, mod_ref, g_ref, w_ref, rb_ref,
                    qa_ref, ka_ref, va_ref, qb_ref, kb_ref, vb_ref):
    x = x_ref[...]
    mod = mod_ref[...]
    y = x * lax.rsqrt(jnp.mean(x * x, axis=-1, keepdims=True) + EPS) * g_ref[...]
    h = y * (1.0 + mod[1:2, :]) + mod[0:1, :]
    q_scale = HEAD_DIM ** -0.5

    half = ROPE_DIM // 2
    rb = rb_ref[...]
    lane = lax.broadcasted_iota(jnp.int32, rb.shape, 1)
    cos_lo = jnp.where(lane < half, rb, 0.0)
    sin_hi = jnp.where(jnp.logical_and(lane >= half, lane < ROPE_DIM), rb, 0.0)
    cos_pair = cos_lo + pltpu.roll(cos_lo, half, 1)
    rc = (cos_pair + pltpu.roll(cos_pair, B_QK_DIM, 1)
          + jnp.where(lane % B_QK_DIM >= ROPE_DIM, 1.0, 0.0))
    sin_lo = pltpu.roll(sin_hi, LANES - half, 1)
    rm = -(sin_lo + pltpu.roll(sin_lo, B_QK_DIM, 1))
    rp = sin_hi + pltpu.roll(sin_hi, B_QK_DIM, 1)

    def rope(p):
        cols = []
        for s in range(p.shape[1] // LANES):
            v = p[:, s * LANES:(s + 1) * LANES]
            cols.append(v * rc + pltpu.roll(v, LANES - ROPE_DIM // 2, 1) * rm
                        + pltpu.roll(v, ROPE_DIM // 2, 1) * rp)
        return jnp.concatenate(cols, axis=1)

    outs = (qa_ref, ka_ref, va_ref, qb_ref, kb_ref, vb_ref)
    for j, o_ref in enumerate(outs):
        p = jnp.dot(h, w_ref[:, j * A_WIDTH:(j + 1) * A_WIDTH], preferred_element_type=F32)
        if j in (3, 4):
            p = rope(p)
        if j in (0, 3):
            p = p * (q_scale * LOG2_E)
        pb = p.astype(BF16)
        for s in range(A_WIDTH // LANES):
            o_ref[s] = pb[:, s * LANES:(s + 1) * LANES]


def _in_proj(x2, mod, g_mix, w_in, rope_base, seq):
    n = x2.shape[0]
    tiles_per_seq = seq // TOK_TILE
    row = lambda i: (i, 0)
    fixed = lambda i: (0, 0)
    n_slabs = A_WIDTH // LANES
    out_sd = jax.ShapeDtypeStruct((n_slabs, n, LANES), BF16)
    return pl.pallas_call(
        _in_proj_kernel,
        grid=(n // TOK_TILE,),
        in_specs=[
            pl.BlockSpec((TOK_TILE, D_MODEL), row),
            pl.BlockSpec((None, SUBLANES, D_MODEL), lambda i: (i // tiles_per_seq, 0, 0)),
            pl.BlockSpec((1, D_MODEL), fixed),
            pl.BlockSpec(w_in.shape, fixed),
            pl.BlockSpec((TOK_TILE, LANES), row),
        ],
        out_specs=[pl.BlockSpec((n_slabs, TOK_TILE, LANES), lambda i: (0, i, 0))] * 6,
        out_shape=[out_sd] * 6,
        compiler_params=_cparams(1, VMEM_LIMIT),
        name="in_proj",
    )(x2, mod, g_mix.reshape(1, D_MODEL), w_in, rope_base)


def _attn_a_kernel(q_ref, kp_ref, kc_ref, vp_ref, vc_ref, bias_ref, o_ref, k_sc, v_sc, bias_sc, *,
                   nblk):
    step = pl.program_id(0)
    g = step % nblk
    n_pairs, blk, _ = q_ref.shape
    lane = lax.broadcasted_iota(jnp.int32, (A_QBLK, LANES), 1)
    col = lax.broadcasted_iota(jnp.int32, (A_QBLK, A_BAND), 1)
    ones = jnp.ones((A_BAND, LANES), BF16)

    @pl.when(step == 0)
    def _():
        q_chunk = lax.broadcasted_iota(jnp.int32, (A_QBLK, A_BAND), 0) // CHUNK
        k_chunk = col // CHUNK
        in_band = jnp.logical_and(k_chunk >= q_chunk, k_chunk <= q_chunk + LEFT_CHUNKS)
        for h in range(bias_ref.shape[0]):
            rolled = pltpu.roll(jnp.broadcast_to(bias_ref[h], (A_QBLK, A_ROLL)), 0, 1,
                                stride=1, stride_axis=0)
            bias_sc[h] = jnp.where(in_band, rolled[:, :A_BAND], NEG_INF)

    def pair(p):
        k_buf = k_sc.at[p % 2]
        v_buf = v_sc.at[p % 2]
        k_buf[0:blk, :] = kp_ref[p]
        k_buf[blk:2 * blk, :] = kc_ref[p]
        v_buf[0:blk, :] = vp_ref[p]
        v_buf[blk:2 * blk, :] = vc_ref[p]

        def scores(m, hh):
            r0 = m * A_QBLK
            q = q_ref[p, r0:r0 + A_QBLK, :]
            in_head = (lane < HEAD_DIM) if hh == 0 else (lane >= HEAD_DIM)
            qh = jnp.where(in_head, q, jnp.zeros_like(q))
            return lax.dot_general(qh, k_buf[r0:r0 + A_BAND, :], (((1,), (1,)), ((), ())),
                                   preferred_element_type=F32)

        chains = [(m, hh) for m in range(blk // A_QBLK) for hh in range(2)]
        s_next = scores(*chains[0])
        halves = []
        for i, (m, hh) in enumerate(chains):
            r0 = m * A_QBLK
            s = s_next
            if i + 1 < len(chains):
                s_next = scores(*chains[i + 1])
            valid = jnp.logical_or(g > 0, col + r0 >= blk)
            s = jnp.where(valid, s + bias_sc[2 * p + hh], NEG_INF)
            pr = jnp.exp2(s - jnp.max(s, axis=1, keepdims=True))
            v_ext = jnp.concatenate([v_buf[r0:r0 + A_BAND, :], ones], axis=1)
            pv = jnp.dot(pr.astype(BF16), v_ext, preferred_element_type=F32)
            halves.append(pv[:, :LANES] / pv[:, LANES:])
            if hh == 1:
                o_ref[p, r0:r0 + A_QBLK, :] = jnp.where(lane < HEAD_DIM, halves[0],
                                                        halves[1]).astype(BF16)
                halves = []

    for p in range(n_pairs):
        pair(p)


def _attn_a(qa, ka, va, bias_rows, batch, seq):
    n_pairs, n, _ = qa.shape
    blk = LEFT_CHUNKS * CHUNK
    nblk = seq // blk
    cur = lambda g: (0, g, 0)
    prev = lambda g: (0, jnp.where(g % nblk == 0, g, g - 1), 0)
    slab = (n_pairs, blk, LANES)
    return pl.pallas_call(
        functools.partial(_attn_a_kernel, nblk=nblk),
        grid=(batch * nblk,),
        in_specs=[
            pl.BlockSpec(slab, cur),
            pl.BlockSpec(slab, prev),
            pl.BlockSpec(slab, cur),
            pl.BlockSpec(slab, prev),
            pl.BlockSpec(slab, cur),
            pl.BlockSpec(bias_rows.shape, lambda g: (0, 0, 0)),
        ],
        out_specs=pl.BlockSpec(slab, cur),
        out_shape=jax.ShapeDtypeStruct((n_pairs, n, LANES), BF16),
        scratch_shapes=[
            pltpu.VMEM((2, 2 * blk, LANES), BF16),
            pltpu.VMEM((2, 2 * blk, LANES), BF16),
            pltpu.VMEM((bias_rows.shape[0], A_QBLK, A_BAND), F32),
        ],
        compiler_params=_cparams(1),
        name="attn_a",
    )(qa, ka, ka, va, va, bias_rows)


def _rel_bias_rows(rel_table):
    t = rel_table.astype(F32) * LOG2_E
    far = t[:, 2 * MAX_REL:]
    n_far = LEFT_CHUNKS * CHUNK - MAX_REL
    row = jnp.concatenate([
        jnp.broadcast_to(far, (t.shape[0], n_far)),
        t[:, 2 * MAX_REL:0:-1],
        jnp.broadcast_to(far, (t.shape[0], A_ROLL - A_BAND)),
    ], axis=1)
    return row.reshape(t.shape[0], 1, A_ROLL)


def _attn_b_kernel(lam_ref, q_ref, k_ref, v_ref, g_ref, o_ref, q_sc, qn_sc, s_sc, m_sc, acc_sc, *,
                   out_scale):
    n_tiles = q_ref.shape[0] // B_TQ
    n_sub = B_TQ // B_SUB
    first_slot = 2
    lane = lax.broadcasted_iota(jnp.int32, (B_TQ, LANES), 1)
    ones = jnp.ones((B_TK, LANES), BF16)
    lam = lam_ref[0]

    def split_components(t, dst):
        q = q_ref[pl.ds(pl.multiple_of(t * B_TQ, B_TQ), B_TQ), :]
        dst[0] = jnp.where(lane < B_QK_DIM, q, jnp.zeros_like(q))
        dst[1] = jnp.where(lane >= B_QK_DIM, q, jnp.zeros_like(q))

    def score_rows(src, k, slot, sub, c):
        rows = pl.ds(sub * B_SUB, B_SUB)
        s_sc[slot, c, rows, :] = lax.dot_general(src[c, rows, :], k, (((1,), (1,)), ((), ())),
                                                 preferred_element_type=F32)

    def update_rows(s, v_ext, sub, c):
        rows = pl.ds(sub * B_SUB, B_SUB)
        m_prev = m_sc[c, rows, :]
        m_new = jnp.maximum(m_prev, jnp.max(s, axis=1, keepdims=True))
        alpha = jnp.exp2(m_prev - m_new)
        p = jnp.exp2(s - jnp.concatenate([m_new] * (s.shape[1] // LANES), axis=1))
        pv = jnp.dot(p.astype(BF16), v_ext, preferred_element_type=F32)
        acc_sc[c, rows, :] = jnp.concatenate([alpha, alpha], axis=1) * acc_sc[c, rows, :] + pv
        m_sc[c, rows, :] = m_new

    def key_block(blk):
        return k_ref[pl.ds(pl.multiple_of(blk * B_TK, B_TK), B_TK), :]

    def value_block(blk):
        return jnp.concatenate([v_ref[pl.ds(pl.multiple_of(blk * B_TK, B_TK), B_TK), :], ones], axis=1)

    def step(blk, slot, next_blk, next_slot):
        v_ext = value_block(blk)
        k_next = key_block(next_blk)
        for sub in range(n_sub):
            for c in range(2):
                update_rows(s_sc[slot, c, pl.ds(sub * B_SUB, B_SUB), :], v_ext, sub, c)
                score_rows(q_sc, k_next, next_slot, sub, c)

    def last_step(t, slot):
        v_ext = value_block(t)
        k0 = key_block(0)
        col_c = lax.broadcasted_iota(jnp.int32, (B_SUB, B_TK), 1) // CHUNK
        row_c = lax.broadcasted_iota(jnp.int32, (B_SUB, B_TK), 0) // CHUNK
        for sub in range(n_sub):
            keep = col_c <= row_c + sub * (B_SUB // CHUNK)
            for c in range(2):
                s = s_sc[slot, c, pl.ds(sub * B_SUB, B_SUB), :]
                update_rows(jnp.where(keep, s, NEG_INF), v_ext, sub, c)
                score_rows(qn_sc, k0, first_slot, sub, c)

    def start_tile(t):
        split_components(t, q_sc)
        split_components(jnp.minimum(t + 1, n_tiles - 1), qn_sc)
        m_sc[...] = jnp.full(m_sc.shape, NEG_INF, F32)
        acc_sc[...] = jnp.zeros(acc_sc.shape, F32)

    def finish_tile(t):
        a0 = acc_sc[0]
        a1 = acc_sc[1]
        o = a0[:, :B_V_DIM] / a0[:, B_V_DIM:] - lam * (a1[:, :B_V_DIM] / a1[:, B_V_DIM:])
        y = o * lax.rsqrt(jnp.mean(o * o, axis=-1, keepdims=True) + EPS) * g_ref[...]
        o_ref[pl.ds(pl.multiple_of(t * B_TQ, B_TQ), B_TQ), :] = (y * out_scale).astype(BF16)

    start_tile(0)
    k0 = key_block(0)
    for sub in range(n_sub):
        for c in range(2):
            score_rows(q_sc, k0, first_slot, sub, c)
    s_sc[0] = s_sc[first_slot]
    last_step(0, 0)
    finish_tile(0)

    def tile(t, carry):
        start_tile(t)
        step(0, first_slot, 1, 1)

        def pair(p, carry):
            odd = 2 * p + 1
            step(odd, 1, odd + 1, 0)
            step(odd + 1, 0, odd + 2, 1)
            return carry

        lax.fori_loop(0, (t - 1) // 2, pair, 0)

        @pl.when(t % 2 == 1)
        def _():
            last_step(t, 1)

        @pl.when(t % 2 == 0)
        def _():
            step(t - 1, 1, t, 0)
            last_step(t, 0)

        finish_tile(t)
        return carry

    lax.fori_loop(1, n_tiles, tile, 0)


def _attn_b(b, lam, qb, kb, vb, g_subln, seq, out_scale):
    assert B_TQ == B_TK and B_V_DIM == LANES
    by_head = lambda h: (h, b, 0)
    return pl.pallas_call(
        functools.partial(_attn_b_kernel, out_scale=out_scale),
        grid=(B_HEADS,),
        in_specs=[
            pl.BlockSpec(memory_space=pltpu.SMEM),
            pl.BlockSpec((None, seq, LANES), by_head),
            pl.BlockSpec((None, seq, LANES), by_head),
            pl.BlockSpec((None, seq, LANES), by_head),
            pl.BlockSpec((1, B_V_DIM), lambda h: (0, 0)),
        ],
        out_specs=pl.BlockSpec((None, seq, LANES), lambda h: (h, 0, 0)),
        out_shape=jax.ShapeDtypeStruct((B_HEADS, seq, LANES), BF16),
        scratch_shapes=[
            pltpu.VMEM((2, B_TQ, LANES), BF16),
            pltpu.VMEM((2, B_TQ, LANES), BF16),
            pltpu.VMEM((3, 2, B_TQ, B_TK), F32),
            pltpu.VMEM((2, B_TQ, LANES), F32),
            pltpu.VMEM((2, B_TQ, 2 * LANES), F32),
        ],
        compiler_params=_cparams(1),
        name="attn_b",
    )(lam, qb, kb, vb, g_subln.reshape(1, B_V_DIM))


def _pack_bf16_pairs(v):
    half = v.shape[1] // 2
    vb = v.astype(BF16)
    hi = lax.bitcast_convert_type(vb[:, :half].astype(F32), jnp.int32)
    lo = lax.bitcast_convert_type(vb[:, half:].astype(F32), jnp.int32)
    return hi | lax.shift_right_logical(lo, jnp.full(lo.shape, 16, jnp.int32))


def _unpack_bf16_pairs(w):
    first = lax.bitcast_convert_type(w & jnp.int32(-65536), F32)
    second = lax.bitcast_convert_type(lax.shift_left(w, jnp.full(w.shape, 16, jnp.int32)), F32)
    return first, second


def _out_route_kernel(oa_ref, ob_ref, x_ref, mod_ref, wo_ref, g_ref, wr_ref, br_ref, cin_ref,
                      x1_in_ref, x1_ref, hp_ref, ti_ref, tw_ref, rk_ref, cnt_ref, tri_sc, carry_sc):
    del x1_in_ref
    i = pl.program_id(0)
    tm = x_ref.shape[0]

    @pl.when(i == 0)
    def _():
        r = lax.broadcasted_iota(jnp.int32, (tm, tm), 0)
        c = lax.broadcasted_iota(jnp.int32, (tm, tm), 1)
        tri_sc[...] = jnp.where(r < c, 1.0, 0.0).astype(BF16)
        carry_sc[...] = cin_ref[...]

    mod = mod_ref[...]
    o = jnp.concatenate([oa_ref[s] for s in range(oa_ref.shape[0])]
                        + [ob_ref[s] for s in range(ob_ref.shape[0])], axis=1)
    mix = jnp.dot(o.astype(F32), wo_ref[...], preferred_element_type=F32)
    x1 = x_ref[...] + mod[2:3, :] * mix
    x1_ref[...] = x1
    y = x1 * lax.rsqrt(jnp.mean(x1 * x1, axis=-1, keepdims=True) + EPS) * g_ref[...]
    h = y * (1.0 + mod[4:5, :]) + mod[3:4, :]
    hp_ref[...] = _pack_bf16_pairs(h)

    logits = lax.dot_general(wr_ref[...], h, (((1,), (1,)), ((), ())),
                             preferred_element_type=F32) + br_ref[...]
    eid = lax.broadcasted_iota(jnp.int32, logits.shape, 0).astype(F32)
    work = logits
    vals, ids = [], []
    chosen = jnp.zeros(logits.shape, F32)
    for _ in range(TOP_K):
        v = jnp.max(work, axis=0, keepdims=True)
        e = jnp.min(jnp.where(work == v, eid, float(N_EXPERTS)), axis=0, keepdims=True)
        hit = eid == e
        vals.append(v)
        ids.append(e)
        chosen = jnp.where(hit, 1.0, chosen)
        work = jnp.where(hit, -jnp.inf, work)
    ex = [jnp.exp(v - vals[0]) for v in vals]
    den = ex[0] + ex[1] + ex[2] + ex[3]

    before = jnp.dot(chosen.astype(BF16), tri_sc[...], preferred_element_type=F32) + carry_sc[...]
    slot = lax.broadcasted_iota(jnp.int32, (SUBLANES, tm), 0)
    ti = jnp.zeros((SUBLANES, tm), F32)
    tw = jnp.zeros((SUBLANES, tm), F32)
    rk = jnp.zeros((SUBLANES, tm), F32)
    for kk in range(TOP_K):
        r_k = jnp.sum(jnp.where(eid == ids[kk], before, 0.0), axis=0, keepdims=True)
        ti = jnp.where(slot == kk, ids[kk], ti)
        tw = jnp.where(slot == kk, ex[kk] / den, tw)
        rk = jnp.where(slot == kk, r_k, rk)
    ti_ref[...] = ti.astype(jnp.int32)
    tw_ref[...] = tw
    rk_ref[...] = rk.astype(jnp.int32)
    carry = carry_sc[...] + jnp.sum(chosen, axis=1, keepdims=True)
    carry_sc[...] = carry
    cnt_ref[...] = carry


def _out_route(b, oa, ob, x2, mod, w_out, g_ffn, w_router, b_router, counts_in, x1_buf, seq):
    n = x2.shape[0]
    tiles = seq // TOK_TILE
    row = lambda i: (b * tiles + i, 0)
    local = lambda i: (i, 0)
    by_lane = lambda i: (0, i)
    fixed = lambda i: (0, 0)
    return pl.pallas_call(
        _out_route_kernel,
        grid=(tiles,),
        in_specs=[
            pl.BlockSpec((oa.shape[0], TOK_TILE, LANES), lambda i: (0, b * tiles + i, 0)),
            pl.BlockSpec((ob.shape[0], TOK_TILE, LANES), lambda i: (0, i, 0)),
            pl.BlockSpec((TOK_TILE, D_MODEL), row),
            pl.BlockSpec((None, SUBLANES, D_MODEL), lambda i: (b, 0, 0)),
            pl.BlockSpec((D_MODEL, D_MODEL), fixed),
            pl.BlockSpec((1, D_MODEL), fixed),
            pl.BlockSpec((N_EXPERTS, D_MODEL), fixed),
            pl.BlockSpec((N_EXPERTS, 1), fixed),
            pl.BlockSpec((N_EXPERTS, 1), fixed),
            pl.BlockSpec(memory_space=pl.ANY),
        ],
        out_specs=[
            pl.BlockSpec((TOK_TILE, D_MODEL), row),
            pl.BlockSpec((TOK_TILE, D_MODEL // 2), local),
            pl.BlockSpec((SUBLANES, TOK_TILE), by_lane),
            pl.BlockSpec((SUBLANES, TOK_TILE), by_lane),
            pl.BlockSpec((SUBLANES, TOK_TILE), by_lane),
            pl.BlockSpec((N_EXPERTS, 1), fixed),
        ],
        out_shape=[
            jax.ShapeDtypeStruct((n, D_MODEL), F32),
            jax.ShapeDtypeStruct((seq, D_MODEL // 2), jnp.int32),
            jax.ShapeDtypeStruct((SUBLANES, seq), jnp.int32),
            jax.ShapeDtypeStruct((SUBLANES, seq), F32),
            jax.ShapeDtypeStruct((SUBLANES, seq), jnp.int32),
            jax.ShapeDtypeStruct((N_EXPERTS, 1), F32),
        ],
        input_output_aliases={9: 0},
        scratch_shapes=[pltpu.VMEM((TOK_TILE, TOK_TILE), BF16), pltpu.VMEM((N_EXPERTS, 1), F32)],
        compiler_params=_cparams(1, VMEM_LIMIT),
        name="out_route",
    )(oa, ob, x2, mod, w_out, g_ffn.reshape(1, D_MODEL), w_router.T,
      b_router.reshape(N_EXPERTS, 1), counts_in, x1_buf)


def _experts_kernel(tiles_ref, rows_ref, base_ref, next_ref, xs_ref, wgu_ref, bgu_ref, wd_ref,
                    bd_ref, y_ref, x_buf, y_buf, x_sem, y_sem):
    e = pl.program_id(0)
    n_tiles = tiles_ref[e]
    n_rows = rows_ref[e]
    n_live = base_ref[N_EXPERTS]
    region = e * EXPERT_ROWS

    def x_copy(row, slot):
        return pltpu.make_async_copy(xs_ref.at[pl.ds(row, ROW_TILE), :], x_buf.at[slot],
                                     x_sem.at[slot])

    def y_copy(row, slot):
        return pltpu.make_async_copy(y_buf.at[slot], y_ref.at[pl.ds(row, ROW_TILE), :],
                                     y_sem.at[slot])

    @pl.when(e == 0)
    def _():
        y_buf[...] = jnp.zeros(y_buf.shape, jnp.int32)

        @pl.when(n_live > 0)
        def _():
            x_copy(pl.multiple_of(next_ref[N_EXPERTS], ROW_TILE), 0).start()

    def tile(j, carry):
        g = base_ref[e] + j
        slot = g % 2
        row = pl.multiple_of(region + j * ROW_TILE, ROW_TILE)
        x_copy(row, slot).wait()
        next_row = jnp.where(j + 1 < n_tiles, row + ROW_TILE, next_ref[e])

        @pl.when(g + 1 < n_live)
        def _():
            x_copy(pl.multiple_of(next_row, ROW_TILE), 1 - slot).start()

        @pl.when(g >= 2)
        def _():
            y_copy(0, slot).wait()

        needed = jnp.minimum(n_rows - j * ROW_TILE, ROW_TILE)
        for rows in range(ROW_STEP, ROW_TILE + 1, ROW_STEP):
            @pl.when(jnp.logical_and(needed > rows - ROW_STEP, needed <= rows))
            def _(rows=rows):
                used = lax.broadcasted_iota(jnp.int32, (rows, 1), 0) < needed
                first, second = _unpack_bf16_pairs(jnp.where(used, x_buf[slot, :rows, :], 0))
                x = jnp.concatenate([first, second], axis=1)
                acc = jnp.zeros((rows, D_MODEL), F32)
                for c in range(D_EXPERT // FF_CHUNK):
                    lo_c, hi_c = c * FF_CHUNK, (c + 1) * FF_CHUNK
                    gate = (jnp.dot(x, wgu_ref[:, lo_c:hi_c], preferred_element_type=F32)
                            + bgu_ref[:, lo_c:hi_c])
                    up = (jnp.dot(x, wgu_ref[:, D_EXPERT + lo_c:D_EXPERT + hi_c],
                                  preferred_element_type=F32)
                          + bgu_ref[:, D_EXPERT + lo_c:D_EXPERT + hi_c])
                    gate = jnp.minimum(gate, SWIGLU_LIMIT)
                    up = jnp.clip(up, -SWIGLU_LIMIT, SWIGLU_LIMIT)
                    act = (up + 1.0) * (gate * jax.nn.sigmoid(SWIGLU_ALPHA * gate))
                    acc = acc + jnp.dot(act, wd_ref[lo_c:hi_c, :],
                                        preferred_element_type=F32)
                y_buf[slot, :rows, :] = _pack_bf16_pairs(acc + bd_ref[...])

        y_copy(row, slot).start()
        return carry

    lax.fori_loop(0, n_tiles, tile, 0)

    @pl.when(e == pl.num_programs(0) - 1)
    def _():
        for back in (2, 1):
            @pl.when(n_live >= back)
            def _():
                y_copy(0, (n_live - back) % 2).wait()


def _experts(counts, xs, w_gate_up, b_gate_up, w_down, b_down):
    n_rows, width = xs.shape
    tiles = (counts + ROW_TILE - 1) // ROW_TILE
    base = jnp.concatenate([jnp.zeros((1,), jnp.int32), jnp.cumsum(tiles)]).astype(jnp.int32)
    region = jnp.arange(N_EXPERTS, dtype=jnp.int32) * EXPERT_ROWS
    later = jnp.arange(N_EXPERTS)[None, :] > jnp.arange(-1, N_EXPERTS)[:, None]
    cand = jnp.where(jnp.logical_and(later, (tiles > 0)[None, :]), region[None, :], n_rows)
    nxt = jnp.min(cand, axis=1)
    next_row = jnp.concatenate([nxt[1:], nxt[:1]]).astype(jnp.int32)
    by_expert = lambda e, *_: (e, 0, 0)
    return pl.pallas_call(
        _experts_kernel,
        grid_spec=pltpu.PrefetchScalarGridSpec(
            num_scalar_prefetch=4,
            grid=(N_EXPERTS,),
            in_specs=[
                pl.BlockSpec(memory_space=pl.ANY),
                pl.BlockSpec((None, D_MODEL, 2 * D_EXPERT), by_expert),
                pl.BlockSpec((None, 1, 2 * D_EXPERT), by_expert),
                pl.BlockSpec((None, D_EXPERT, D_MODEL), by_expert),
                pl.BlockSpec((None, 1, D_MODEL), by_expert),
            ],
            out_specs=pl.BlockSpec(memory_space=pl.ANY),
            scratch_shapes=[
                pltpu.VMEM((2, ROW_TILE, width), jnp.int32),
                pltpu.VMEM((2, ROW_TILE, D_MODEL // 2), jnp.int32),
                pltpu.SemaphoreType.DMA((2,)),
                pltpu.SemaphoreType.DMA((2,)),
            ],
        ),
        out_shape=jax.ShapeDtypeStruct((n_rows, D_MODEL // 2), jnp.int32),
        compiler_params=_cparams(1, VMEM_LIMIT),
        name="experts",
    )(tiles.astype(jnp.int32), counts.astype(jnp.int32), base, next_row, xs, w_gate_up,
      b_gate_up.reshape(N_EXPERTS, 1, 2 * D_EXPERT), w_down, b_down.reshape(N_EXPERTS, 1, D_MODEL))


def _sc_gather_rows(table, idx):
    m = idx.shape[0]
    width = table.shape[1]
    mesh = plsc.VectorSubcoreMesh(core_axis_name="core", subcore_axis_name="subcore")
    n_workers = mesh.num_cores * mesh.num_subcores
    per_worker = m // n_workers
    n_windows = per_worker // SC_HALF_WINDOW
    assert per_worker * n_workers == m and n_windows * SC_HALF_WINDOW == per_worker

    @pl.kernel(
        out_type=jax.ShapeDtypeStruct((m, width), table.dtype),
        mesh=mesh,
        scratch_types=[
            pltpu.VMEM((SC_HALF_WINDOW,), jnp.int32),
            pltpu.VMEM((SC_HALF_WINDOW,), jnp.int32),
            pltpu.VMEM((SC_HALF_WINDOW, width), table.dtype),
            pltpu.VMEM((SC_HALF_WINDOW, width), table.dtype),
            pltpu.SemaphoreType.DMA,
            pltpu.SemaphoreType.DMA,
            pltpu.SemaphoreType.DMA,
            pltpu.SemaphoreType.DMA,
        ],
    )
    def gather_kernel(table_hbm, idx_hbm, out_hbm, idx0, idx1, rows0, rows1, g0, g1, w0, w1):
        worker = lax.axis_index("subcore") * mesh.num_cores + lax.axis_index("core")
        idx_v, rows_v, g_sem, w_sem = (idx0, idx1), (rows0, rows1), (g0, g1), (w0, w1)

        def window(j):
            return pl.ds(pl.multiple_of(worker * per_worker + j * SC_HALF_WINDOW, SC_HALF_WINDOW),
                         SC_HALF_WINDOW)

        def fetch(j):
            s = j % 2
            pltpu.sync_copy(idx_hbm.at[window(j)], idx_v[s])
            return pltpu.async_copy(table_hbm.at[idx_v[s]], rows_v[s], g_sem[s])

        fetches = {0: fetch(0)}
        writes = {}
        for j in range(n_windows):
            s = j % 2
            if j + 1 < n_windows:
                if j >= 1:
                    writes[j - 1].wait()
                fetches[j + 1] = fetch(j + 1)
            fetches[j].wait()
            writes[j] = pltpu.async_copy(rows_v[s], out_hbm.at[window(j)], w_sem[s])
        for j in range(max(n_windows - 2, 0), n_windows):
            writes[j].wait()

    return gather_kernel(table, idx)


def _sc_scatter_rows_into(table, dst, out_ref, reps):
    m, width = table.shape
    mesh = plsc.VectorSubcoreMesh(core_axis_name="core", subcore_axis_name="subcore")
    n_workers = mesh.num_cores * mesh.num_subcores
    per_worker = m // n_workers
    assert per_worker * n_workers == m and per_worker % SC_WINDOW == 0 and dst.shape == (reps * m,)

    @pl.kernel(
        out_type=(),
        mesh=mesh,
        scratch_types=[
            pltpu.VMEM((SC_WINDOW,), jnp.int32),
            pltpu.VMEM((SC_WINDOW, width), table.dtype),
            pltpu.SemaphoreType.DMA,
        ],
    )
    def scatter_kernel(table_hbm, dst_hbm, out_hbm, dst_v, rows_v, sem):
        worker = lax.axis_index("subcore") * mesh.num_cores + lax.axis_index("core")

        @pl.loop(0, per_worker // SC_WINDOW)
        def _(j):
            base = pl.multiple_of(worker * per_worker + j * SC_WINDOW, SC_WINDOW)
            pltpu.sync_copy(table_hbm.at[pl.ds(base, SC_WINDOW)], rows_v)
            for k in range(reps):
                pltpu.sync_copy(dst_hbm.at[pl.ds(k * m + base, SC_WINDOW)], dst_v)
                pltpu.async_copy(rows_v, out_hbm.at[dst_v], sem).wait()

    scatter_kernel(table, dst, out_ref)


def _combine_dense_kernel(tw_ref, x1_ref, mod_ref, g_ref, y0_ref, y1_ref, y2_ref, y3_ref, o_ref):
    tm = x1_ref.shape[0]
    tw = jnp.concatenate([tw_ref[...], jnp.zeros((LANES - SUBLANES, tm), F32)], axis=0).T
    first = second = None
    for kk, y_ref in enumerate((y0_ref, y1_ref, y2_ref, y3_ref)):
        f_k, s_k = _unpack_bf16_pairs(y_ref[...])
        w_k = tw[:, kk:kk + 1]
        first = w_k * f_k if kk == 0 else first + w_k * f_k
        second = w_k * s_k if kk == 0 else second + w_k * s_k
    ffn = jnp.concatenate([first, second], axis=1)
    x2 = x1_ref[...] + mod_ref[5:6, :] * ffn
    o_ref[...] = x2 * lax.rsqrt(jnp.mean(x2 * x2, axis=-1, keepdims=True) + EPS) * g_ref[...]


def _combine_dense(b, tw, x1, mod, g_final, yg, out_buf, seq):
    n = x1.shape[0]
    width = yg.shape[1]
    tiles = seq // TOK_TILE
    row = lambda i: (b * tiles + i, 0)
    deep = pl.Buffered(COMBINE_BUFFERS)
    slot = lambda kk: pl.BlockSpec((TOK_TILE, width), lambda i: (kk * tiles + i, 0),
                                   pipeline_mode=deep)

    def stream(tw_ref, x1_ref, mod_ref, g_ref, yg_ref, o_in_ref, o_ref):
        del o_in_ref
        pltpu.emit_pipeline(
            _combine_dense_kernel,
            grid=(tiles,),
            in_specs=[
                pl.BlockSpec((SUBLANES, TOK_TILE), lambda i: (0, i)),
                pl.BlockSpec((TOK_TILE, D_MODEL), row, pipeline_mode=deep),
                pl.BlockSpec((SUBLANES, D_MODEL), lambda i: (0, 0)),
                pl.BlockSpec((1, D_MODEL), lambda i: (0, 0)),
            ] + [slot(kk) for kk in range(TOP_K)],
            out_specs=[pl.BlockSpec((TOK_TILE, D_MODEL), row)],
        )(tw_ref, x1_ref, mod_ref, g_ref, yg_ref, yg_ref, yg_ref, yg_ref, o_ref)

    anywhere = pl.BlockSpec(memory_space=pl.ANY)
    return pl.pallas_call(
        stream,
        in_specs=[anywhere] * 6,
        out_specs=anywhere,
        out_shape=jax.ShapeDtypeStruct((n, D_MODEL), F32),
        input_output_aliases={5: 0},
        compiler_params=pltpu.CompilerParams(vmem_limit_bytes=VMEM_LIMIT),
        name="combine",
    )(tw, x1, mod[b], g_final.reshape(1, D_MODEL), yg, out_buf)


def _rope_base(positions):
    half = ROPE_DIM // 2
    inv_freq = ROPE_THETA ** (-jnp.arange(0, ROPE_DIM, 2, dtype=F32) / ROPE_DIM)
    n = positions.size
    pos = jnp.broadcast_to(positions.reshape(n, 1).astype(F32), (n, half)).reshape(-1, LANES)
    ang = pos * jnp.tile(inv_freq, LANES // half)
    cos, sin = lax.optimization_barrier((jnp.cos(ang), jnp.sin(ang)))
    pad = jnp.zeros((n, LANES - ROPE_DIM), F32)
    return jnp.concatenate([cos.reshape(n, half), sin.reshape(n, half), pad], axis=1)


def kernel(x, c, positions, w_ada, b_ada, g_mix_norm, w_in, rel_bias, lambda_q1, lambda_k1,
           lambda_q2, lambda_k2, g_subln, w_out, g_ffn_norm, w_router, b_router, w_gate_up,
           b_gate_up, w_down, b_down, g_final):
    batch, seq, _ = x.shape
    depth = w_ada.shape[0]
    assert depth == 1, "the combine kernel applies the final norm, so it must follow the only layer"
    n = batch * seq
    assert n <= EXPERT_ROWS and EXPERT_ROWS % ROW_TILE == 0
    rope_base = _rope_base(positions)
    x2 = x.reshape(n, D_MODEL)
    for l in range(depth):
        lambda_init = 0.8 - 0.6 * math.exp(-0.3 * l)
        mod = _ada(c, w_ada[l], b_ada[l])
        qa, ka, va, qb, kb, vb = _in_proj(x2, mod, g_mix_norm[l], w_in[l],
                                          rope_base, seq)
        oa = _attn_a(qa, ka, va, _rel_bias_rows(rel_bias[l]), batch, seq)
        lam = (jnp.exp(jnp.sum(lambda_q1[l].astype(F32) * lambda_k1[l].astype(F32)))
               - jnp.exp(jnp.sum(lambda_q2[l].astype(F32) * lambda_k2[l].astype(F32)))
               + lambda_init).reshape(1)
        xs_ref = jax.new_ref(lax.empty((N_EXPERTS * EXPERT_ROWS, D_MODEL // 2), jnp.int32))
        x1 = lax.empty((n, D_MODEL), F32)
        counts = jnp.zeros((N_EXPERTS, 1), F32)
        pos_rows, weight_rows = [], []
        for b in range(batch):
            ob = _attn_b(b, lam, qb, kb, vb, g_subln[l], seq, 1.0 - lambda_init)
            x1, hp, top_i, top_w, rank, counts = _out_route(
                b, oa, ob, x2, mod, w_out[l], g_ffn_norm[l], w_router[l], b_router[l], counts, x1,
                seq)
            pos = top_i * EXPERT_ROWS + rank
            pos_rows.append(pos)
            weight_rows.append(top_w)
            dst = jnp.concatenate([pos[kk] for kk in range(TOP_K)])
            _sc_scatter_rows_into(hp, dst, xs_ref, TOP_K)
        xs = jax.freeze(xs_ref)
        y = _experts(counts[:, 0].astype(jnp.int32), xs, w_gate_up[l], b_gate_up[l], w_down[l],
                     b_down[l])
        x2 = lax.empty((n, D_MODEL), F32)
        for b in range(batch):
            yg = _sc_gather_rows(y, jnp.concatenate([pos_rows[b][kk] for kk in range(TOP_K)]))
            x2 = _combine_dense(b, weight_rows[b], x1, mod, g_final, yg, x2, seq)
    return x2.reshape(batch, seq, D_MODEL)
```

```python
import functools
import math

import jax
import jax.numpy as jnp
from jax import lax
from jax.experimental import pallas as pl
from jax.experimental.pallas import tpu as pltpu
from jax.experimental.pallas import tpu_sc as plsc

D_MODEL = 1024
CHUNK = 64
HEAD_DIM = 64
A_HEADS = 8
A_WIDTH = A_HEADS * HEAD_DIM
LEFT_CHUNKS = 8
MAX_REL = 128
B_HEADS = 4
B_QK_DIM = HEAD_DIM
B_V_DIM = 2 * HEAD_DIM
B_WIDTH = B_HEADS * B_V_DIM
ROPE_THETA = 500000.0
ROPE_DIM = B_QK_DIM // 4
N_EXPERTS = 32
TOP_K = 4
D_EXPERT = D_MODEL
SWIGLU_LIMIT = 7.0
SWIGLU_ALPHA = 1.702
EPS = 1e-6
NEG_INF = -1e30
LOG2_E = math.log2(math.e)
N_MOD = 6

LANES = 128
SUBLANES = 8
VMEM_BYTES_V7X = 64 * 1024 * 1024
VMEM_LIMIT = VMEM_BYTES_V7X * 7 // 8

TOK_TILE = 512
A_QBLK = 2 * CHUNK
A_BAND = (LEFT_CHUNKS + 2) * CHUNK
A_ROLL = A_BAND + A_QBLK
B_TQ = 512
B_TK = 512
B_SUB = 128
ROW_TILE = 512
ROW_STEP = 128
FF_CHUNK = 512
EXPERT_ROWS = 16384
COMBINE_BUFFERS = 3
SC_WINDOW = 128
SC_HALF_WINDOW = 64

F32 = jnp.float32
BF16 = jnp.bfloat16


def _cparams(n_axes, vmem=None):
    return pltpu.CompilerParams(
        dimension_semantics=("arbitrary",) * n_axes,
        vmem_limit_bytes=vmem,
    )


def _ada_kernel(c_ref, w_ref, b_ref, o_ref):
    c = c_ref[...]
    act = c * jax.nn.sigmoid(c)
    o_ref[...] = jnp.dot(act, w_ref[...], preferred_element_type=F32,
                         precision=lax.Precision.HIGHEST) + b_ref[...]


def _ada(c, w_ada, b_ada):
    b = c.shape[0]
    rows = -(-b // SUBLANES) * SUBLANES
    c_pad = jnp.pad(c, ((0, rows - b), (0, 0)))
    n_out = w_ada.shape[1]
    out = pl.pallas_call(
        _ada_kernel,
        grid=(n_out // D_MODEL,),
        in_specs=[
            pl.BlockSpec((rows, D_MODEL), lambda j: (0, 0)),
            pl.BlockSpec((D_MODEL, D_MODEL), lambda j: (0, j)),
            pl.BlockSpec((1, D_MODEL), lambda j: (0, j)),
        ],
        out_specs=pl.BlockSpec((rows, D_MODEL), lambda j: (0, j)),
        out_shape=jax.ShapeDtypeStruct((rows, n_out), F32),
        compiler_params=_cparams(1),
        name="ada",
    )(c_pad, w_ada, b_ada.reshape(1, n_out))
    mod = out[:b].reshape(b, N_MOD, D_MODEL)
    return jnp.pad(mod, ((0, 0), (0, SUBLANES - N_MOD), (0, 0)))


def _in_proj_kernel(x_ref, mod_ref, g_ref, w_ref, rb_ref,
                    qa_ref, ka_ref, va_ref, qb_ref, kb_ref, vb_ref):
    x = x_ref[...]
    mod = mod_ref[...]
    y = x * lax.rsqrt(jnp.mean(x * x, axis=-1, keepdims=True) + EPS) * g_ref[...]
    h = y * (1.0 + mod[1:2, :]) + mod[0:1, :]
    q_scale = HEAD_DIM ** -0.5

    half = ROPE_DIM // 2
    rb = rb_ref[...]
    lane = lax.broadcasted_iota(jnp.int32, rb.shape, 1)
    cos_lo = jnp.where(lane < half, rb, 0.0)
    sin_hi = jnp.where(jnp.logical_and(lane >= half, lane < ROPE_DIM), rb, 0.0)
    cos_pair = cos_lo + pltpu.roll(cos_lo, half, 1)
    rc = (cos_pair + pltpu.roll(cos_pair, B_QK_DIM, 1)
          + jnp.where(lane % B_QK_DIM >= ROPE_DIM, 1.0, 0.0))
    sin_lo = pltpu.roll(sin_hi, LANES - half, 1)
    rm = -(sin_lo + pltpu.roll(sin_lo, B_QK_DIM, 1))
    rp = sin_hi + pltpu.roll(sin_hi, B_QK_DIM, 1)

    def rope(p):
        cols = []
        for s in range(p.shape[1] // LANES):
            v = p[:, s * LANES:(s + 1) * LANES]
            cols.append(v * rc + pltpu.roll(v, LANES - ROPE_DIM // 2, 1) * rm
                        + pltpu.roll(v, ROPE_DIM // 2, 1) * rp)
        return jnp.concatenate(cols, axis=1)

    outs = (qa_ref, ka_ref, va_ref, qb_ref, kb_ref, vb_ref)
    for j, o_ref in enumerate(outs):
        p = jnp.dot(h, w_ref[:, j * A_WIDTH:(j + 1) * A_WIDTH], preferred_element_type=F32)
        if j in (3, 4):
            p = rope(p)
        if j in (0, 3):
            p = p * (q_scale * LOG2_E)
        pb = p.astype(BF16)
        for s in range(A_WIDTH // LANES):
            o_ref[s] = pb[:, s * LANES:(s + 1) * LANES]


def _in_proj(x2, mod, g_mix, w_in, rope_base, seq):
    n = x2.shape[0]
    tiles_per_seq = seq // TOK_TILE
    row = lambda i: (i, 0)
    fixed = lambda i: (0, 0)
    n_slabs = A_WIDTH // LANES
    out_sd = jax.ShapeDtypeStruct((n_slabs, n, LANES), BF16)
    return pl.pallas_call(
        _in_proj_kernel,
        grid=(n // TOK_TILE,),
        in_specs=[
            pl.BlockSpec((TOK_TILE, D_MODEL), row),
            pl.BlockSpec((None, SUBLANES, D_MODEL), lambda i: (i // tiles_per_seq, 0, 0)),
            pl.BlockSpec((1, D_MODEL), fixed),
            pl.BlockSpec(w_in.shape, fixed),
            pl.BlockSpec((TOK_TILE, LANES), row),
        ],
        out_specs=[pl.BlockSpec((n_slabs, TOK_TILE, LANES), lambda i: (0, i, 0))] * 6,
        out_shape=[out_sd] * 6,
        compiler_params=_cparams(1, VMEM_LIMIT),
        name="in_proj",
    )(x2, mod, g_mix.reshape(1, D_MODEL), w_in, rope_base)


def _attn_a_kernel(q_ref, kp_ref, kc_ref, vp_ref, vc_ref, bias_ref, o_ref, k_sc, v_sc, bias_sc, *,
                   nblk):
    step = pl.program_id(0)
    g = step % nblk
    n_pairs, blk, _ = q_ref.shape
    lane = lax.broadcasted_iota(jnp.int32, (A_QBLK, LANES), 1)
    col = lax.broadcasted_iota(jnp.int32, (A_QBLK, A_BAND), 1)
    ones = jnp.ones((A_BAND, LANES), BF16)

    @pl.when(step == 0)
    def _():
        q_chunk = lax.broadcasted_iota(jnp.int32, (A_QBLK, A_BAND), 0) // CHUNK
        k_chunk = col // CHUNK
        in_band = jnp.logical_and(k_chunk >= q_chunk, k_chunk <= q_chunk + LEFT_CHUNKS)
        for h in range(bias_ref.shape[0]):
            rolled = pltpu.roll(jnp.broadcast_to(bias_ref[h], (A_QBLK, A_ROLL)), 0, 1,
                                stride=1, stride_axis=0)
            bias_sc[h] = jnp.where(in_band, rolled[:, :A_BAND], NEG_INF)

    def pair(p):
        k_buf = k_sc.at[p % 2]
        v_buf = v_sc.at[p % 2]
        k_buf[0:blk, :] = kp_ref[p]
        k_buf[blk:2 * blk, :] = kc_ref[p]
        v_buf[0:blk, :] = vp_ref[p]
        v_buf[blk:2 * blk, :] = vc_ref[p]

        def scores(m, hh):
            r0 = m * A_QBLK
            q = q_ref[p, r0:r0 + A_QBLK, :]
            in_head = (lane < HEAD_DIM) if hh == 0 else (lane >= HEAD_DIM)
            qh = jnp.where(in_head, q, jnp.zeros_like(q))
            return lax.dot_general(qh, k_buf[r0:r0 + A_BAND, :], (((1,), (1,)), ((), ())),
                                   preferred_element_type=F32)

        chains = [(m, hh) for m in range(blk // A_QBLK) for hh in range(2)]
        s_next = scores(*chains[0])
        halves = []
        for i, (m, hh) in enumerate(chains):
            r0 = m * A_QBLK
            s = s_next
            if i + 1 < len(chains):
                s_next = scores(*chains[i + 1])
            valid = jnp.logical_or(g > 0, col + r0 >= blk)
            s = jnp.where(valid, s + bias_sc[2 * p + hh], NEG_INF)
            pr = jnp.exp2(s - jnp.max(s, axis=1, keepdims=True))
            v_ext = jnp.concatenate([v_buf[r0:r0 + A_BAND, :], ones], axis=1)
            pv = jnp.dot(pr.astype(BF16), v_ext, preferred_element_type=F32)
            halves.append(pv[:, :LANES] / pv[:, LANES:])
            if hh == 1:
                o_ref[p, r0:r0 + A_QBLK, :] = jnp.where(lane < HEAD_DIM, halves[0],
                                                        halves[1]).astype(BF16)
                halves = []

    for p in range(n_pairs):
        pair(p)


def _attn_a(qa, ka, va, bias_rows, batch, seq):
    n_pairs, n, _ = qa.shape
    blk = LEFT_CHUNKS * CHUNK
    nblk = seq // blk
    cur = lambda g: (0, g, 0)
    prev = lambda g: (0, jnp.where(g % nblk == 0, g, g - 1), 0)
    slab = (n_pairs, blk, LANES)
    return pl.pallas_call(
        functools.partial(_attn_a_kernel, nblk=nblk),
        grid=(batch * nblk,),
        in_specs=[
            pl.BlockSpec(slab, cur),
            pl.BlockSpec(slab, prev),
            pl.BlockSpec(slab, cur),
            pl.BlockSpec(slab, prev),
            pl.BlockSpec(slab, cur),
            pl.BlockSpec(bias_rows.shape, lambda g: (0, 0, 0)),
        ],
        out_specs=pl.BlockSpec(slab, cur),
        out_shape=jax.ShapeDtypeStruct((n_pairs, n, LANES), BF16),
        scratch_shapes=[
            pltpu.VMEM((2, 2 * blk, LANES), BF16),
            pltpu.VMEM((2, 2 * blk, LANES), BF16),
            pltpu.VMEM((bias_rows.shape[0], A_QBLK, A_BAND), F32),
        ],
        compiler_params=_cparams(1),
        name="attn_a",
    )(qa, ka, ka, va, va, bias_rows)


def _rel_bias_rows(rel_table):
    t = rel_table.astype(F32) * LOG2_E
    far = t[:, 2 * MAX_REL:]
    n_far = LEFT_CHUNKS * CHUNK - MAX_REL
    row = jnp.concatenate([
        jnp.broadcast_to(far, (t.shape[0], n_far)),
        t[:, 2 * MAX_REL:0:-1],
        jnp.broadcast_to(far, (t.shape[0], A_ROLL - A_BAND)),
    ], axis=1)
    return row.reshape(t.shape[0], 1, A_ROLL)


def _attn_b_kernel(lam_ref, q_ref, k_ref, v_ref, g_ref, o_ref, q_sc, qn_sc, s_sc, m_sc, acc_sc, *,
                   out_scale):
    n_tiles = q_ref.shape[0] // B_TQ
    n_sub = B_TQ // B_SUB
    first_slot = 2
    lane = lax.broadcasted_iota(jnp.int32, (B_TQ, LANES), 1)
    ones = jnp.ones((B_TK, LANES), BF16)
    lam = lam_ref[0]

    def split_components(t, dst):
        q = q_ref[pl.ds(pl.multiple_of(t * B_TQ, B_TQ), B_TQ), :]
        dst[0] = jnp.where(lane < B_QK_DIM, q, jnp.zeros_like(q))
        dst[1] = jnp.where(lane >= B_QK_DIM, q, jnp.zeros_like(q))

    def score_rows(src, k, slot, sub, c):
        rows = pl.ds(sub * B_SUB, B_SUB)
        s_sc[slot, c, rows, :] = lax.dot_general(src[c, rows, :], k, (((1,), (1,)), ((), ())),
                                                 preferred_element_type=F32)

    def update_rows(s, v_ext, sub, c):
        rows = pl.ds(sub * B_SUB, B_SUB)
        m_prev = m_sc[c, rows, :]
        m_new = jnp.maximum(m_prev, jnp.max(s, axis=1, keepdims=True))
        alpha = jnp.exp2(m_prev - m_new)
        p = jnp.exp2(s - jnp.concatenate([m_new] * (s.shape[1] // LANES), axis=1))
        pv = jnp.dot(p.astype(BF16), v_ext, preferred_element_type=F32)
        acc_sc[c, rows, :] = jnp.concatenate([alpha, alpha], axis=1) * acc_sc[c, rows, :] + pv
        m_sc[c, rows, :] = m_new

    def key_block(blk):
        return k_ref[pl.ds(pl.multiple_of(blk * B_TK, B_TK), B_TK), :]

    def value_block(blk):
        return jnp.concatenate([v_ref[pl.ds(pl.multiple_of(blk * B_TK, B_TK), B_TK), :], ones], axis=1)

    def step(blk, slot, next_blk, next_slot):
        v_ext = value_block(blk)
        k_next = key_block(next_blk)
        for sub in range(n_sub):
            for c in range(2):
                update_rows(s_sc[slot, c, pl.ds(sub * B_SUB, B_SUB), :], v_ext, sub, c)
                score_rows(q_sc, k_next, next_slot, sub, c)

    def last_step(t, slot):
        v_ext = value_block(t)
        k0 = key_block(0)
        col_c = lax.broadcasted_iota(jnp.int32, (B_SUB, B_TK), 1) // CHUNK
        row_c = lax.broadcasted_iota(jnp.int32, (B_SUB, B_TK), 0) // CHUNK
        for sub in range(n_sub):
            keep = col_c <= row_c + sub * (B_SUB // CHUNK)
            for c in range(2):
                s = s_sc[slot, c, pl.ds(sub * B_SUB, B_SUB), :]
                update_rows(jnp.where(keep, s, NEG_INF), v_ext, sub, c)
                score_rows(qn_sc, k0, first_slot, sub, c)

    def start_tile(t):
        split_components(t, q_sc)
        split_components(jnp.minimum(t + 1, n_tiles - 1), qn_sc)
        m_sc[...] = jnp.full(m_sc.shape, NEG_INF, F32)
        acc_sc[...] = jnp.zeros(acc_sc.shape, F32)

    def finish_tile(t):
        a0 = acc_sc[0]
        a1 = acc_sc[1]
        o = a0[:, :B_V_DIM] / a0[:, B_V_DIM:] - lam * (a1[:, :B_V_DIM] / a1[:, B_V_DIM:])
        y = o * lax.rsqrt(jnp.mean(o * o, axis=-1, keepdims=True) + EPS) * g_ref[...]
        o_ref[pl.ds(pl.multiple_of(t * B_TQ, B_TQ), B_TQ), :] = (y * out_scale).astype(BF16)

    start_tile(0)
    k0 = key_block(0)
    for sub in range(n_sub):
        for c in range(2):
            score_rows(q_sc, k0, first_slot, sub, c)
    s_sc[0] = s_sc[first_slot]
    last_step(0, 0)
    finish_tile(0)

    def tile(t, carry):
        start_tile(t)
        step(0, first_slot, 1, 1)

        def pair(p, carry):
            odd = 2 * p + 1
            step(odd, 1, odd + 1, 0)
            step(odd + 1, 0, odd + 2, 1)
            return carry

        lax.fori_loop(0, (t - 1) // 2, pair, 0)

        @pl.when(t % 2 == 1)
        def _():
            last_step(t, 1)

        @pl.when(t % 2 == 0)
        def _():
            step(t - 1, 1, t, 0)
            last_step(t, 0)

        finish_tile(t)
        return carry

    lax.fori_loop(1, n_tiles, tile, 0)


def _attn_b(b, lam, qb, kb, vb, g_subln, seq, out_scale):
    assert B_TQ == B_TK and B_V_DIM == LANES
    by_head = lambda h: (h, b, 0)
    return pl.pallas_call(
        functools.partial(_attn_b_kernel, out_scale=out_scale),
        grid=(B_HEADS,),
        in_specs=[
            pl.BlockSpec(memory_space=pltpu.SMEM),
            pl.BlockSpec((None, seq, LANES), by_head),
            pl.BlockSpec((None, seq, LANES), by_head),
            pl.BlockSpec((None, seq, LANES), by_head),
            pl.BlockSpec((1, B_V_DIM), lambda h: (0, 0)),
        ],
        out_specs=pl.BlockSpec((None, seq, LANES), lambda h: (h, 0, 0)),
        out_shape=jax.ShapeDtypeStruct((B_HEADS, seq, LANES), BF16),
        scratch_shapes=[
            pltpu.VMEM((2, B_TQ, LANES), BF16),
            pltpu.VMEM((2, B_TQ, LANES), BF16),
            pltpu.VMEM((3, 2, B_TQ, B_TK), F32),
            pltpu.VMEM((2, B_TQ, LANES), F32),
            pltpu.VMEM((2, B_TQ, 2 * LANES), F32),
        ],
        compiler_params=_cparams(1),
        name="attn_b",
    )(lam, qb, kb, vb, g_subln.reshape(1, B_V_DIM))


def _pack_bf16_pairs(v):
    half = v.shape[1] // 2
    vb = v.astype(BF16)
    hi = lax.bitcast_convert_type(vb[:, :half].astype(F32), jnp.int32)
    lo = lax.bitcast_convert_type(vb[:, half:].astype(F32), jnp.int32)
    return hi | lax.shift_right_logical(lo, jnp.full(lo.shape, 16, jnp.int32))


def _unpack_bf16_pairs(w):
    first = lax.bitcast_convert_type(w & jnp.int32(-65536), F32)
    second = lax.bitcast_convert_type(lax.shift_left(w, jnp.full(w.shape, 16, jnp.int32)), F32)
    return first, second


def _out_route_kernel(oa_ref, ob_ref, x_ref, mod_ref, wo_ref, g_ref, wr_ref, br_ref, cin_ref,
                      x1_in_ref, x1_ref, hp_ref, ti_ref, tw_ref, rk_ref, cnt_ref, tri_sc, carry_sc):
    del x1_in_ref
    i = pl.program_id(0)
    tm = x_ref.shape[0]

    @pl.when(i == 0)
    def _():
        r = lax.broadcasted_iota(jnp.int32, (tm, tm), 0)
        c = lax.broadcasted_iota(jnp.int32, (tm, tm), 1)
        tri_sc[...] = jnp.where(r < c, 1.0, 0.0).astype(BF16)
        carry_sc[...] = cin_ref[...]

    mod = mod_ref[...]
    o = jnp.concatenate([oa_ref[s] for s in range(oa_ref.shape[0])]
                        + [ob_ref[s] for s in range(ob_ref.shape[0])], axis=1)
    mix = jnp.dot(o.astype(F32), wo_ref[...], preferred_element_type=F32)
    x1 = x_ref[...] + mod[2:3, :] * mix
    x1_ref[...] = x1.astype(BF16)
    y = x1 * lax.rsqrt(jnp.mean(x1 * x1, axis=-1, keepdims=True) + EPS) * g_ref[...]
    h = y * (1.0 + mod[4:5, :]) + mod[3:4, :]
    hp_ref[...] = _pack_bf16_pairs(h)

    logits = lax.dot_general(wr_ref[...], h, (((1,), (1,)), ((), ())),
                             preferred_element_type=F32) + br_ref[...]
    eid = lax.broadcasted_iota(jnp.int32, logits.shape, 0).astype(F32)
    work = logits
    vals, ids = [], []
    chosen = jnp.zeros(logits.shape, F32)
    for _ in range(TOP_K):
        v = jnp.max(work, axis=0, keepdims=True)
        e = jnp.min(jnp.where(work == v, eid, float(N_EXPERTS)), axis=0, keepdims=True)
        hit = eid == e
        vals.append(v)
        ids.append(e)
        chosen = jnp.where(hit, 1.0, chosen)
        work = jnp.where(hit, -jnp.inf, work)
    ex = [jnp.exp(v - vals[0]) for v in vals]
    den = ex[0] + ex[1] + ex[2] + ex[3]

    before = jnp.dot(chosen.astype(BF16), tri_sc[...], preferred_element_type=F32) + carry_sc[...]
    slot = lax.broadcasted_iota(jnp.int32, (SUBLANES, tm), 0)
    ti = jnp.zeros((SUBLANES, tm), F32)
    tw = jnp.zeros((SUBLANES, tm), F32)
    rk = jnp.zeros((SUBLANES, tm), F32)
    for kk in range(TOP_K):
        r_k = jnp.sum(jnp.where(eid == ids[kk], before, 0.0), axis=0, keepdims=True)
        ti = jnp.where(slot == kk, ids[kk], ti)
        tw = jnp.where(slot == kk, ex[kk] / den, tw)
        rk = jnp.where(slot == kk, r_k, rk)
    ti_ref[...] = ti.astype(jnp.int32)
    tw_ref[...] = tw
    rk_ref[...] = rk.astype(jnp.int32)
    carry = carry_sc[...] + jnp.sum(chosen, axis=1, keepdims=True)
    carry_sc[...] = carry
    cnt_ref[...] = carry


def _out_route(b, oa, ob, x2, mod, w_out, g_ffn, w_router, b_router, counts_in, x1_buf, seq):
    n = x2.shape[0]
    tiles = seq // TOK_TILE
    row = lambda i: (b * tiles + i, 0)
    local = lambda i: (i, 0)
    by_lane = lambda i: (0, i)
    fixed = lambda i: (0, 0)
    return pl.pallas_call(
        _out_route_kernel,
        grid=(tiles,),
        in_specs=[
            pl.BlockSpec((oa.shape[0], TOK_TILE, LANES), lambda i: (0, b * tiles + i, 0)),
            pl.BlockSpec((ob.shape[0], TOK_TILE, LANES), lambda i: (0, i, 0)),
            pl.BlockSpec((TOK_TILE, D_MODEL), row),
            pl.BlockSpec((None, SUBLANES, D_MODEL), lambda i: (b, 0, 0)),
            pl.BlockSpec((D_MODEL, D_MODEL), fixed),
            pl.BlockSpec((1, D_MODEL), fixed),
            pl.BlockSpec((N_EXPERTS, D_MODEL), fixed),
            pl.BlockSpec((N_EXPERTS, 1), fixed),
            pl.BlockSpec((N_EXPERTS, 1), fixed),
            pl.BlockSpec(memory_space=pl.ANY),
        ],
        out_specs=[
            pl.BlockSpec((TOK_TILE, D_MODEL), row),
            pl.BlockSpec((TOK_TILE, D_MODEL // 2), local),
            pl.BlockSpec((SUBLANES, TOK_TILE), by_lane),
            pl.BlockSpec((SUBLANES, TOK_TILE), by_lane),
            pl.BlockSpec((SUBLANES, TOK_TILE), by_lane),
            pl.BlockSpec((N_EXPERTS, 1), fixed),
        ],
        out_shape=[
            jax.ShapeDtypeStruct((n, D_MODEL), BF16),
            jax.ShapeDtypeStruct((seq, D_MODEL // 2), jnp.int32),
            jax.ShapeDtypeStruct((SUBLANES, seq), jnp.int32),
            jax.ShapeDtypeStruct((SUBLANES, seq), F32),
            jax.ShapeDtypeStruct((SUBLANES, seq), jnp.int32),
            jax.ShapeDtypeStruct((N_EXPERTS, 1), F32),
        ],
        input_output_aliases={9: 0},
        scratch_shapes=[pltpu.VMEM((TOK_TILE, TOK_TILE), BF16), pltpu.VMEM((N_EXPERTS, 1), F32)],
        compiler_params=_cparams(1, VMEM_LIMIT),
        name="out_route",
    )(oa, ob, x2, mod, w_out, g_ffn.reshape(1, D_MODEL), w_router.T,
      b_router.reshape(N_EXPERTS, 1), counts_in, x1_buf)


def _experts_kernel(tiles_ref, rows_ref, base_ref, next_ref, xs_ref, wgu_ref, bgu_ref, wd_ref,
                    bd_ref, y_ref, x_buf, y_buf, x_sem, y_sem):
    e = pl.program_id(0)
    n_tiles = tiles_ref[e]
    n_rows = rows_ref[e]
    n_live = base_ref[N_EXPERTS]
    region = e * EXPERT_ROWS

    def x_copy(row, slot):
        return pltpu.make_async_copy(xs_ref.at[pl.ds(row, ROW_TILE), :], x_buf.at[slot],
                                     x_sem.at[slot])

    def y_copy(row, slot):
        return pltpu.make_async_copy(y_buf.at[slot], y_ref.at[pl.ds(row, ROW_TILE), :],
                                     y_sem.at[slot])

    @pl.when(e == 0)
    def _():
        y_buf[...] = jnp.zeros(y_buf.shape, jnp.int32)

        @pl.when(n_live > 0)
        def _():
            x_copy(pl.multiple_of(next_ref[N_EXPERTS], ROW_TILE), 0).start()

    def tile(j, carry):
        g = base_ref[e] + j
        slot = g % 2
        row = pl.multiple_of(region + j * ROW_TILE, ROW_TILE)
        x_copy(row, slot).wait()
        next_row = jnp.where(j + 1 < n_tiles, row + ROW_TILE, next_ref[e])

        @pl.when(g + 1 < n_live)
        def _():
            x_copy(pl.multiple_of(next_row, ROW_TILE), 1 - slot).start()

        @pl.when(g >= 2)
        def _():
            y_copy(0, slot).wait()

        needed = jnp.minimum(n_rows - j * ROW_TILE, ROW_TILE)
        for rows in range(ROW_STEP, ROW_TILE + 1, ROW_STEP):
            @pl.when(jnp.logical_and(needed > rows - ROW_STEP, needed <= rows))
            def _(rows=rows):
                used = lax.broadcasted_iota(jnp.int32, (rows, 1), 0) < needed
                first, second = _unpack_bf16_pairs(jnp.where(used, x_buf[slot, :rows, :], 0))
                x = jnp.concatenate([first, second], axis=1)
                acc = jnp.zeros((rows, D_MODEL), F32)
                for c in range(D_EXPERT // FF_CHUNK):
                    lo_c, hi_c = c * FF_CHUNK, (c + 1) * FF_CHUNK
                    gate = (jnp.dot(x, wgu_ref[:, lo_c:hi_c], preferred_element_type=F32)
                            + bgu_ref[:, lo_c:hi_c])
                    up = (jnp.dot(x, wgu_ref[:, D_EXPERT + lo_c:D_EXPERT + hi_c],
                                  preferred_element_type=F32)
                          + bgu_ref[:, D_EXPERT + lo_c:D_EXPERT + hi_c])
                    gate = jnp.minimum(gate, SWIGLU_LIMIT)
                    up = jnp.clip(up, -SWIGLU_LIMIT, SWIGLU_LIMIT)
                    act = (up + 1.0) * (gate * jax.nn.sigmoid(SWIGLU_ALPHA * gate))
                    acc = acc + jnp.dot(act, wd_ref[lo_c:hi_c, :],
                                        preferred_element_type=F32)
                y_buf[slot, :rows, :] = _pack_bf16_pairs(acc + bd_ref[...])

        y_copy(row, slot).start()
        return carry

    lax.fori_loop(0, n_tiles, tile, 0)

    @pl.when(e == pl.num_programs(0) - 1)
    def _():
        for back in (2, 1):
            @pl.when(n_live >= back)
            def _():
                y_copy(0, (n_live - back) % 2).wait()


def _experts(counts, xs, w_gate_up, b_gate_up, w_down, b_down):
    n_rows, width = xs.shape
    tiles = (counts + ROW_TILE - 1) // ROW_TILE
    base = jnp.concatenate([jnp.zeros((1,), jnp.int32), jnp.cumsum(tiles)]).astype(jnp.int32)
    region = jnp.arange(N_EXPERTS, dtype=jnp.int32) * EXPERT_ROWS
    later = jnp.arange(N_EXPERTS)[None, :] > jnp.arange(-1, N_EXPERTS)[:, None]
    cand = jnp.where(jnp.logical_and(later, (tiles > 0)[None, :]), region[None, :], n_rows)
    nxt = jnp.min(cand, axis=1)
    next_row = jnp.concatenate([nxt[1:], nxt[:1]]).astype(jnp.int32)
    by_expert = lambda e, *_: (e, 0, 0)
    return pl.pallas_call(
        _experts_kernel,
        grid_spec=pltpu.PrefetchScalarGridSpec(
            num_scalar_prefetch=4,
            grid=(N_EXPERTS,),
            in_specs=[
                pl.BlockSpec(memory_space=pl.ANY),
                pl.BlockSpec((None, D_MODEL, 2 * D_EXPERT), by_expert),
                pl.BlockSpec((None, 1, 2 * D_EXPERT), by_expert),
                pl.BlockSpec((None, D_EXPERT, D_MODEL), by_expert),
                pl.BlockSpec((None, 1, D_MODEL), by_expert),
            ],
            out_specs=pl.BlockSpec(memory_space=pl.ANY),
            scratch_shapes=[
                pltpu.VMEM((2, ROW_TILE, width), jnp.int32),
                pltpu.VMEM((2, ROW_TILE, D_MODEL // 2), jnp.int32),
                pltpu.SemaphoreType.DMA((2,)),
                pltpu.SemaphoreType.DMA((2,)),
            ],
        ),
        out_shape=jax.ShapeDtypeStruct((n_rows, D_MODEL // 2), jnp.int32),
        compiler_params=_cparams(1, VMEM_LIMIT),
        name="experts",
    )(tiles.astype(jnp.int32), counts.astype(jnp.int32), base, next_row, xs, w_gate_up,
      b_gate_up.reshape(N_EXPERTS, 1, 2 * D_EXPERT), w_down, b_down.reshape(N_EXPERTS, 1, D_MODEL))


def _sc_gather_rows(table, idx):
    m = idx.shape[0]
    width = table.shape[1]
    mesh = plsc.VectorSubcoreMesh(core_axis_name="core", subcore_axis_name="subcore")
    n_workers = mesh.num_cores * mesh.num_subcores
    per_worker = m // n_workers
    n_windows = per_worker // SC_HALF_WINDOW
    assert per_worker * n_workers == m and n_windows * SC_HALF_WINDOW == per_worker

    @pl.kernel(
        out_type=jax.ShapeDtypeStruct((m, width), table.dtype),
        mesh=mesh,
        scratch_types=[
            pltpu.VMEM((SC_HALF_WINDOW,), jnp.int32),
            pltpu.VMEM((SC_HALF_WINDOW,), jnp.int32),
            pltpu.VMEM((SC_HALF_WINDOW, width), table.dtype),
            pltpu.VMEM((SC_HALF_WINDOW, width), table.dtype),
            pltpu.SemaphoreType.DMA,
            pltpu.SemaphoreType.DMA,
            pltpu.SemaphoreType.DMA,
            pltpu.SemaphoreType.DMA,
        ],
    )
    def gather_kernel(table_hbm, idx_hbm, out_hbm, idx0, idx1, rows0, rows1, g0, g1, w0, w1):
        worker = lax.axis_index("subcore") * mesh.num_cores + lax.axis_index("core")
        idx_v, rows_v, g_sem, w_sem = (idx0, idx1), (rows0, rows1), (g0, g1), (w0, w1)

        def window(j):
            return pl.ds(pl.multiple_of(worker * per_worker + j * SC_HALF_WINDOW, SC_HALF_WINDOW),
                         SC_HALF_WINDOW)

        def fetch(j):
            s = j % 2
            pltpu.sync_copy(idx_hbm.at[window(j)], idx_v[s])
            return pltpu.async_copy(table_hbm.at[idx_v[s]], rows_v[s], g_sem[s])

        fetches = {0: fetch(0)}
        writes = {}
        for j in range(n_windows):
            s = j % 2
            if j + 1 < n_windows:
                if j >= 1:
                    writes[j - 1].wait()
                fetches[j + 1] = fetch(j + 1)
            fetches[j].wait()
            writes[j] = pltpu.async_copy(rows_v[s], out_hbm.at[window(j)], w_sem[s])
        for j in range(max(n_windows - 2, 0), n_windows):
            writes[j].wait()

    return gather_kernel(table, idx)


def _sc_scatter_rows_into(table, dst, out_ref, reps):
    m, width = table.shape
    mesh = plsc.VectorSubcoreMesh(core_axis_name="core", subcore_axis_name="subcore")
    n_workers = mesh.num_cores * mesh.num_subcores
    per_worker = m // n_workers
    assert per_worker * n_workers == m and per_worker % SC_WINDOW == 0 and dst.shape == (reps * m,)

    @pl.kernel(
        out_type=(),
        mesh=mesh,
        scratch_types=[
            pltpu.VMEM((SC_WINDOW,), jnp.int32),
            pltpu.VMEM((SC_WINDOW, width), table.dtype),
            pltpu.SemaphoreType.DMA,
        ],
    )
    def scatter_kernel(table_hbm, dst_hbm, out_hbm, dst_v, rows_v, sem):
        worker = lax.axis_index("subcore") * mesh.num_cores + lax.axis_index("core")

        @pl.loop(0, per_worker // SC_WINDOW)
        def _(j):
            base = pl.multiple_of(worker * per_worker + j * SC_WINDOW, SC_WINDOW)
            pltpu.sync_copy(table_hbm.at[pl.ds(base, SC_WINDOW)], rows_v)
            for k in range(reps):
                pltpu.sync_copy(dst_hbm.at[pl.ds(k * m + base, SC_WINDOW)], dst_v)
                pltpu.async_copy(rows_v, out_hbm.at[dst_v], sem).wait()

    scatter_kernel(table, dst, out_ref)


def _combine_dense_kernel(tw_ref, x1_ref, mod_ref, g_ref, y0_ref, y1_ref, y2_ref, y3_ref, o_ref):
    tm = x1_ref.shape[0]
    tw = jnp.concatenate([tw_ref[...], jnp.zeros((LANES - SUBLANES, tm), F32)], axis=0).T
    first = second = None
    for kk, y_ref in enumerate((y0_ref, y1_ref, y2_ref, y3_ref)):
        f_k, s_k = _unpack_bf16_pairs(y_ref[...])
        w_k = tw[:, kk:kk + 1]
        first = w_k * f_k if kk == 0 else first + w_k * f_k
        second = w_k * s_k if kk == 0 else second + w_k * s_k
    ffn = jnp.concatenate([first, second], axis=1)
    x2 = x1_ref[...].astype(F32) + mod_ref[5:6, :] * ffn
    o_ref[...] = x2 * lax.rsqrt(jnp.mean(x2 * x2, axis=-1, keepdims=True) + EPS) * g_ref[...]


def _combine_dense(b, tw, x1, mod, g_final, yg, out_buf, seq):
    n = x1.shape[0]
    width = yg.shape[1]
    tiles = seq // TOK_TILE
    row = lambda i: (b * tiles + i, 0)
    deep = pl.Buffered(COMBINE_BUFFERS)
    slot = lambda kk: pl.BlockSpec((TOK_TILE, width), lambda i: (kk * tiles + i, 0),
                                   pipeline_mode=deep)

    def stream(tw_ref, x1_ref, mod_ref, g_ref, yg_ref, o_in_ref, o_ref):
        del o_in_ref
        pltpu.emit_pipeline(
            _combine_dense_kernel,
            grid=(tiles,),
            in_specs=[
                pl.BlockSpec((SUBLANES, TOK_TILE), lambda i: (0, i)),
                pl.BlockSpec((TOK_TILE, D_MODEL), row, pipeline_mode=deep),
                pl.BlockSpec((SUBLANES, D_MODEL), lambda i: (0, 0)),
                pl.BlockSpec((1, D_MODEL), lambda i: (0, 0)),
            ] + [slot(kk) for kk in range(TOP_K)],
            out_specs=[pl.BlockSpec((TOK_TILE, D_MODEL), row)],
        )(tw_ref, x1_ref, mod_ref, g_ref, yg_ref, yg_ref, yg_ref, yg_ref, o_ref)

    anywhere = pl.BlockSpec(memory_space=pl.ANY)
    return pl.pallas_call(
        stream,
        in_specs=[anywhere] * 6,
        out_specs=anywhere,
        out_shape=jax.ShapeDtypeStruct((n, D_MODEL), F32),
        input_output_aliases={5: 0},
        compiler_params=pltpu.CompilerParams(vmem_limit_bytes=VMEM_LIMIT),
        name="combine",
    )(tw, x1, mod[b], g_final.reshape(1, D_MODEL), yg, out_buf)


def _rope_base(positions):
    half = ROPE_DIM // 2
    inv_freq = ROPE_THETA ** (-jnp.arange(0, ROPE_DIM, 2, dtype=F32) / ROPE_DIM)
    n = positions.size
    pos = jnp.broadcast_to(positions.reshape(n, 1).astype(F32), (n, half)).reshape(-1, LANES)
    ang = pos * jnp.tile(inv_freq, LANES // half)
    cos, sin = lax.optimization_barrier((jnp.cos(ang), jnp.sin(ang)))
    pad = jnp.zeros((n, LANES - ROPE_DIM), F32)
    return jnp.concatenate([cos.reshape(n, half), sin.reshape(n, half), pad], axis=1)


def kernel(x, c, positions, w_ada, b_ada, g_mix_norm, w_in, rel_bias, lambda_q1, lambda_k1,
           lambda_q2, lambda_k2, g_subln, w_out, g_ffn_norm, w_router, b_router, w_gate_up,
           b_gate_up, w_down, b_down, g_final):
    batch, seq, _ = x.shape
    depth = w_ada.shape[0]
    assert depth == 1, "the combine kernel applies the final norm, so it must follow the only layer"
    n = batch * seq
    assert n <= EXPERT_ROWS and EXPERT_ROWS % ROW_TILE == 0
    rope_base = _rope_base(positions)
    x2 = x.reshape(n, D_MODEL)
    for l in range(depth):
        lambda_init = 0.8 - 0.6 * math.exp(-0.3 * l)
        mod = _ada(c, w_ada[l], b_ada[l])
        qa, ka, va, qb, kb, vb = _in_proj(x2, mod, g_mix_norm[l], w_in[l],
                                          rope_base, seq)
        oa = _attn_a(qa, ka, va, _rel_bias_rows(rel_bias[l]), batch, seq)
        lam = (jnp.exp(jnp.sum(lambda_q1[l].astype(F32) * lambda_k1[l].astype(F32)))
               - jnp.exp(jnp.sum(lambda_q2[l].astype(F32) * lambda_k2[l].astype(F32)))
               + lambda_init).reshape(1)
        xs_ref = jax.new_ref(lax.empty((N_EXPERTS * EXPERT_ROWS, D_MODEL // 2), jnp.int32))
        x1 = lax.empty((n, D_MODEL), BF16)
        counts = jnp.zeros((N_EXPERTS, 1), F32)
        pos_rows, weight_rows = [], []
        for b in range(batch):
            ob = _attn_b(b, lam, qb, kb, vb, g_subln[l], seq, 1.0 - lambda_init)
            x1, hp, top_i, top_w, rank, counts = _out_route(
                b, oa, ob, x2, mod, w_out[l], g_ffn_norm[l], w_router[l], b_router[l], counts, x1,
                seq)
            pos = top_i * EXPERT_ROWS + rank
            pos_rows.append(pos)
            weight_rows.append(top_w)
            dst = jnp.concatenate([pos[kk] for kk in range(TOP_K)])
            _sc_scatter_rows_into(hp, dst, xs_ref, TOP_K)
        xs = jax.freeze(xs_ref)
        y = _experts(counts[:, 0].astype(jnp.int32), xs, w_gate_up[l], b_gate_up[l], w_down[l],
                     b_down[l])
        x2 = lax.empty((n, D_MODEL), F32)
        for b in range(batch):
            yg = _sc_gather_rows(y, jnp.concatenate([pos_rows[b][kk] for kk in range(TOP_K)]))
            x2 = _combine_dense(b, weight_rows[b], x1, mod, g_final, yg, x2, seq)
    return x2.reshape(batch, seq, D_MODEL)
```

```python
import functools
import math

import jax
import jax.numpy as jnp
from jax import lax
from jax.experimental import pallas as pl
from jax.experimental.pallas import tpu as pltpu
from jax.experimental.pallas import tpu_sc as plsc

D_MODEL = 1024
CHUNK = 64
HEAD_DIM = 64
A_HEADS = 8
A_WIDTH = A_HEADS * HEAD_DIM
LEFT_CHUNKS = 8
MAX_REL = 128
B_HEADS = 4
B_QK_DIM = HEAD_DIM
B_V_DIM = 2 * HEAD_DIM
B_WIDTH = B_HEADS * B_V_DIM
ROPE_THETA = 500000.0
ROPE_DIM = B_QK_DIM // 4
N_EXPERTS = 32
TOP_K = 4
D_EXPERT = D_MODEL
SWIGLU_LIMIT = 7.0
SWIGLU_ALPHA = 1.702
EPS = 1e-6
NEG_INF = -1e30
LOG2_E = math.log2(math.e)
N_MOD = 6

LANES = 128
SUBLANES = 8
VMEM_BYTES_V7X = 64 * 1024 * 1024
VMEM_LIMIT = VMEM_BYTES_V7X * 7 // 8

TOK_TILE = 512
A_QBLK = 2 * CHUNK
A_BAND = (LEFT_CHUNKS + 2) * CHUNK
A_ROLL = A_BAND + A_QBLK
B_TQ = 512
B_TK = 512
B_SUB = 128
ROW_TILE = 512
ROW_STEP = 128
FF_CHUNK = 512
EXPERT_ROWS = 16384
COMBINE_BUFFERS = 3
SC_WINDOW = 128
SC_HALF_WINDOW = 64

F32 = jnp.float32
BF16 = jnp.bfloat16


def _cparams(n_axes, vmem=None):
    return pltpu.CompilerParams(
        dimension_semantics=("arbitrary",) * n_axes,
        vmem_limit_bytes=vmem,
    )


def _ada_kernel(c_ref, w_ref, b_ref, o_ref):
    c = c_ref[...]
    act = c * jax.nn.sigmoid(c)
    o_ref[...] = jnp.dot(act, w_ref[...], preferred_element_type=F32) + b_ref[...]


def _ada(c, w_ada, b_ada):
    b = c.shape[0]
    rows = -(-b // SUBLANES) * SUBLANES
    c_pad = jnp.pad(c, ((0, rows - b), (0, 0)))
    n_out = w_ada.shape[1]
    out = pl.pallas_call(
        _ada_kernel,
        grid=(n_out // D_MODEL,),
        in_specs=[
            pl.BlockSpec((rows, D_MODEL), lambda j: (0, 0)),
            pl.BlockSpec((D_MODEL, D_MODEL), lambda j: (0, j)),
            pl.BlockSpec((1, D_MODEL), lambda j: (0, j)),
        ],
        out_specs=pl.BlockSpec((rows, D_MODEL), lambda j: (0, j)),
        out_shape=jax.ShapeDtypeStruct((rows, n_out), F32),
        compiler_params=_cparams(1),
        name="ada",
    )(c_pad, w_ada, b_ada.reshape(1, n_out))
    mod = out[:b].reshape(b, N_MOD, D_MODEL)
    return jnp.pad(mod, ((0, 0), (0, SUBLANES - N_MOD), (0, 0)))


def _in_proj_kernel(x_ref, mod_ref, g_ref, w_ref, rb_ref,
                    qa_ref, ka_ref, va_ref, qb_ref, kb_ref, vb_ref):
    x = x_ref[...]
    mod = mod_ref[...]
    y = x * lax.rsqrt(jnp.mean(x * x, axis=-1, keepdims=True) + EPS) * g_ref[...]
    h = y * (1.0 + mod[1:2, :]) + mod[0:1, :]
    q_scale = HEAD_DIM ** -0.5

    half = ROPE_DIM // 2
    rb = rb_ref[...]
    lane = lax.broadcasted_iota(jnp.int32, rb.shape, 1)
    cos_lo = jnp.where(lane < half, rb, 0.0)
    sin_hi = jnp.where(jnp.logical_and(lane >= half, lane < ROPE_DIM), rb, 0.0)
    cos_pair = cos_lo + pltpu.roll(cos_lo, half, 1)
    rc = (cos_pair + pltpu.roll(cos_pair, B_QK_DIM, 1)
          + jnp.where(lane % B_QK_DIM >= ROPE_DIM, 1.0, 0.0))
    sin_lo = pltpu.roll(sin_hi, LANES - half, 1)
    rm = -(sin_lo + pltpu.roll(sin_lo, B_QK_DIM, 1))
    rp = sin_hi + pltpu.roll(sin_hi, B_QK_DIM, 1)

    def rope(p):
        cols = []
        for s in range(p.shape[1] // LANES):
            v = p[:, s * LANES:(s + 1) * LANES]
            cols.append(v * rc + pltpu.roll(v, LANES - ROPE_DIM // 2, 1) * rm
                        + pltpu.roll(v, ROPE_DIM // 2, 1) * rp)
        return jnp.concatenate(cols, axis=1)

    outs = (qa_ref, ka_ref, va_ref, qb_ref, kb_ref, vb_ref)
    for j, o_ref in enumerate(outs):
        p = jnp.dot(h, w_ref[:, j * A_WIDTH:(j + 1) * A_WIDTH], preferred_element_type=F32)
        if j in (3, 4):
            p = rope(p)
        if j in (0, 3):
            p = p * (q_scale * LOG2_E)
        pb = p.astype(BF16)
        for s in range(A_WIDTH // LANES):
            o_ref[s] = pb[:, s * LANES:(s + 1) * LANES]


def _in_proj(x2, mod, g_mix, w_in, rope_base, seq):
    n = x2.shape[0]
    tiles_per_seq = seq // TOK_TILE
    row = lambda i: (i, 0)
    fixed = lambda i: (0, 0)
    n_slabs = A_WIDTH // LANES
    out_sd = jax.ShapeDtypeStruct((n_slabs, n, LANES), BF16)
    return pl.pallas_call(
        _in_proj_kernel,
        grid=(n // TOK_TILE,),
        in_specs=[
            pl.BlockSpec((TOK_TILE, D_MODEL), row),
            pl.BlockSpec((None, SUBLANES, D_MODEL), lambda i: (i // tiles_per_seq, 0, 0)),
            pl.BlockSpec((1, D_MODEL), fixed),
            pl.BlockSpec(w_in.shape, fixed),
            pl.BlockSpec((TOK_TILE, LANES), row),
        ],
        out_specs=[pl.BlockSpec((n_slabs, TOK_TILE, LANES), lambda i: (0, i, 0))] * 6,
        out_shape=[out_sd] * 6,
        compiler_params=_cparams(1, VMEM_LIMIT),
        name="in_proj",
    )(x2, mod, g_mix.reshape(1, D_MODEL), w_in, rope_base)


def _attn_a_kernel(q_ref, kp_ref, kc_ref, vp_ref, vc_ref, bias_ref, o_ref, k_sc, v_sc, bias_sc, *,
                   nblk):
    step = pl.program_id(0)
    g = step % nblk
    n_pairs, blk, _ = q_ref.shape
    lane = lax.broadcasted_iota(jnp.int32, (A_QBLK, LANES), 1)
    col = lax.broadcasted_iota(jnp.int32, (A_QBLK, A_BAND), 1)
    ones = jnp.ones((A_BAND, LANES), BF16)

    @pl.when(step == 0)
    def _():
        q_chunk = lax.broadcasted_iota(jnp.int32, (A_QBLK, A_BAND), 0) // CHUNK
        k_chunk = col // CHUNK
        in_band = jnp.logical_and(k_chunk >= q_chunk, k_chunk <= q_chunk + LEFT_CHUNKS)
        for h in range(bias_ref.shape[0]):
            rolled = pltpu.roll(jnp.broadcast_to(bias_ref[h], (A_QBLK, A_ROLL)), 0, 1,
                                stride=1, stride_axis=0)
            bias_sc[h] = jnp.where(in_band, rolled[:, :A_BAND], NEG_INF)

    def pair(p):
        k_buf = k_sc.at[p % 2]
        v_buf = v_sc.at[p % 2]
        k_buf[0:blk, :] = kp_ref[p]
        k_buf[blk:2 * blk, :] = kc_ref[p]
        v_buf[0:blk, :] = vp_ref[p]
        v_buf[blk:2 * blk, :] = vc_ref[p]

        def scores(m, hh):
            r0 = m * A_QBLK
            q = q_ref[p, r0:r0 + A_QBLK, :]
            in_head = (lane < HEAD_DIM) if hh == 0 else (lane >= HEAD_DIM)
            qh = jnp.where(in_head, q, jnp.zeros_like(q))
            return lax.dot_general(qh, k_buf[r0:r0 + A_BAND, :], (((1,), (1,)), ((), ())),
                                   preferred_element_type=F32)

        chains = [(m, hh) for m in range(blk // A_QBLK) for hh in range(2)]
        s_next = scores(*chains[0])
        halves = []
        for i, (m, hh) in enumerate(chains):
            r0 = m * A_QBLK
            s = s_next
            if i + 1 < len(chains):
                s_next = scores(*chains[i + 1])
            valid = jnp.logical_or(g > 0, col + r0 >= blk)
            s = jnp.where(valid, s + bias_sc[2 * p + hh], NEG_INF)
            pr = jnp.exp2(s - jnp.max(s, axis=1, keepdims=True))
            v_ext = jnp.concatenate([v_buf[r0:r0 + A_BAND, :], ones], axis=1)
            pv = jnp.dot(pr.astype(BF16), v_ext, preferred_element_type=F32)
            halves.append(pv[:, :LANES] / pv[:, LANES:])
            if hh == 1:
                o_ref[p, r0:r0 + A_QBLK, :] = jnp.where(lane < HEAD_DIM, halves[0],
                                                        halves[1]).astype(BF16)
                halves = []

    for p in range(n_pairs):
        pair(p)


def _attn_a(qa, ka, va, bias_rows, batch, seq):
    n_pairs, n, _ = qa.shape
    blk = LEFT_CHUNKS * CHUNK
    nblk = seq // blk
    cur = lambda g: (0, g, 0)
    prev = lambda g: (0, jnp.where(g % nblk == 0, g, g - 1), 0)
    slab = (n_pairs, blk, LANES)
    return pl.pallas_call(
        functools.partial(_attn_a_kernel, nblk=nblk),
        grid=(batch * nblk,),
        in_specs=[
            pl.BlockSpec(slab, cur),
            pl.BlockSpec(slab, prev),
            pl.BlockSpec(slab, cur),
            pl.BlockSpec(slab, prev),
            pl.BlockSpec(slab, cur),
            pl.BlockSpec(bias_rows.shape, lambda g: (0, 0, 0)),
        ],
        out_specs=pl.BlockSpec(slab, cur),
        out_shape=jax.ShapeDtypeStruct((n_pairs, n, LANES), BF16),
        scratch_shapes=[
            pltpu.VMEM((2, 2 * blk, LANES), BF16),
            pltpu.VMEM((2, 2 * blk, LANES), BF16),
            pltpu.VMEM((bias_rows.shape[0], A_QBLK, A_BAND), F32),
        ],
        compiler_params=_cparams(1),
        name="attn_a",
    )(qa, ka, ka, va, va, bias_rows)


def _rel_bias_rows(rel_table):
    t = rel_table.astype(F32) * LOG2_E
    far = t[:, 2 * MAX_REL:]
    n_far = LEFT_CHUNKS * CHUNK - MAX_REL
    row = jnp.concatenate([
        jnp.broadcast_to(far, (t.shape[0], n_far)),
        t[:, 2 * MAX_REL:0:-1],
        jnp.broadcast_to(far, (t.shape[0], A_ROLL - A_BAND)),
    ], axis=1)
    return row.reshape(t.shape[0], 1, A_ROLL)


def _attn_b_kernel(lam_ref, q_ref, k_ref, v_ref, g_ref, o_ref, q_sc, qn_sc, s_sc, m_sc, acc_sc, *,
                   out_scale):
    n_tiles = q_ref.shape[0] // B_TQ
    n_sub = B_TQ // B_SUB
    first_slot = 2
    lane = lax.broadcasted_iota(jnp.int32, (B_TQ, LANES), 1)
    ones = jnp.ones((B_TK, LANES), BF16)
    lam = lam_ref[0]

    def split_components(t, dst):
        q = q_ref[pl.ds(pl.multiple_of(t * B_TQ, B_TQ), B_TQ), :]
        dst[0] = jnp.where(lane < B_QK_DIM, q, jnp.zeros_like(q))
        dst[1] = jnp.where(lane >= B_QK_DIM, q, jnp.zeros_like(q))

    def score_rows(src, k, slot, sub, c):
        rows = pl.ds(sub * B_SUB, B_SUB)
        s_sc[slot, c, rows, :] = lax.dot_general(src[c, rows, :], k, (((1,), (1,)), ((), ())),
                                                 preferred_element_type=F32)

    def update_rows(s, v_ext, sub, c):
        rows = pl.ds(sub * B_SUB, B_SUB)
        m_prev = m_sc[c, rows, :]
        m_new = jnp.maximum(m_prev, jnp.max(s, axis=1, keepdims=True))
        alpha = jnp.exp2(m_prev - m_new)
        p = jnp.exp2(s - jnp.concatenate([m_new] * (s.shape[1] // LANES), axis=1))
        pv = jnp.dot(p.astype(BF16), v_ext, preferred_element_type=F32)
        acc_sc[c, rows, :] = jnp.concatenate([alpha, alpha], axis=1) * acc_sc[c, rows, :] + pv
        m_sc[c, rows, :] = m_new

    def key_block(blk):
        return k_ref[pl.ds(pl.multiple_of(blk * B_TK, B_TK), B_TK), :]

    def value_block(blk):
        return jnp.concatenate([v_ref[pl.ds(pl.multiple_of(blk * B_TK, B_TK), B_TK), :], ones], axis=1)

    def step(blk, slot, next_blk, next_slot):
        v_ext = value_block(blk)
        k_next = key_block(next_blk)
        for sub in range(n_sub):
            for c in range(2):
                update_rows(s_sc[slot, c, pl.ds(sub * B_SUB, B_SUB), :], v_ext, sub, c)
                score_rows(q_sc, k_next, next_slot, sub, c)

    def last_step(t, slot):
        v_ext = value_block(t)
        k0 = key_block(0)
        col_c = lax.broadcasted_iota(jnp.int32, (B_SUB, B_TK), 1) // CHUNK
        row_c = lax.broadcasted_iota(jnp.int32, (B_SUB, B_TK), 0) // CHUNK
        for sub in range(n_sub):
            keep = col_c <= row_c + sub * (B_SUB // CHUNK)
            for c in range(2):
                s = s_sc[slot, c, pl.ds(sub * B_SUB, B_SUB), :]
                update_rows(jnp.where(keep, s, NEG_INF), v_ext, sub, c)
                score_rows(qn_sc, k0, first_slot, sub, c)

    def start_tile(t):
        split_components(t, q_sc)
        split_components(jnp.minimum(t + 1, n_tiles - 1), qn_sc)
        m_sc[...] = jnp.full(m_sc.shape, NEG_INF, F32)
        acc_sc[...] = jnp.zeros(acc_sc.shape, F32)

    def finish_tile(t):
        a0 = acc_sc[0]
        a1 = acc_sc[1]
        o = a0[:, :B_V_DIM] / a0[:, B_V_DIM:] - lam * (a1[:, :B_V_DIM] / a1[:, B_V_DIM:])
        y = o * lax.rsqrt(jnp.mean(o * o, axis=-1, keepdims=True) + EPS) * g_ref[...]
        o_ref[pl.ds(pl.multiple_of(t * B_TQ, B_TQ), B_TQ), :] = (y * out_scale).astype(BF16)

    start_tile(0)
    k0 = key_block(0)
    for sub in range(n_sub):
        for c in range(2):
            score_rows(q_sc, k0, first_slot, sub, c)
    s_sc[0] = s_sc[first_slot]
    last_step(0, 0)
    finish_tile(0)

    def tile(t, carry):
        start_tile(t)
        step(0, first_slot, 1, 1)

        def pair(p, carry):
            odd = 2 * p + 1
            step(odd, 1, odd + 1, 0)
            step(odd + 1, 0, odd + 2, 1)
            return carry

        lax.fori_loop(0, (t - 1) // 2, pair, 0)

        @pl.when(t % 2 == 1)
        def _():
            last_step(t, 1)

        @pl.when(t % 2 == 0)
        def _():
            step(t - 1, 1, t, 0)
            last_step(t, 0)

        finish_tile(t)
        return carry

    lax.fori_loop(1, n_tiles, tile, 0)


def _attn_b(b, lam, qb, kb, vb, g_subln, seq, out_scale):
    assert B_TQ == B_TK and B_V_DIM == LANES
    by_head = lambda h: (h, b, 0)
    return pl.pallas_call(
        functools.partial(_attn_b_kernel, out_scale=out_scale),
        grid=(B_HEADS,),
        in_specs=[
            pl.BlockSpec(memory_space=pltpu.SMEM),
            pl.BlockSpec((None, seq, LANES), by_head),
            pl.BlockSpec((None, seq, LANES), by_head),
            pl.BlockSpec((None, seq, LANES), by_head),
            pl.BlockSpec((1, B_V_DIM), lambda h: (0, 0)),
        ],
        out_specs=pl.BlockSpec((None, seq, LANES), lambda h: (h, 0, 0)),
        out_shape=jax.ShapeDtypeStruct((B_HEADS, seq, LANES), BF16),
        scratch_shapes=[
            pltpu.VMEM((2, B_TQ, LANES), BF16),
            pltpu.VMEM((2, B_TQ, LANES), BF16),
            pltpu.VMEM((3, 2, B_TQ, B_TK), F32),
            pltpu.VMEM((2, B_TQ, LANES), F32),
            pltpu.VMEM((2, B_TQ, 2 * LANES), F32),
        ],
        compiler_params=_cparams(1),
        name="attn_b",
    )(lam, qb, kb, vb, g_subln.reshape(1, B_V_DIM))


def _pack_bf16_pairs(v):
    half = v.shape[1] // 2
    vb = v.astype(BF16)
    hi = lax.bitcast_convert_type(vb[:, :half].astype(F32), jnp.int32)
    lo = lax.bitcast_convert_type(vb[:, half:].astype(F32), jnp.int32)
    return hi | lax.shift_right_logical(lo, jnp.full(lo.shape, 16, jnp.int32))


def _unpack_bf16_pairs(w):
    first = lax.bitcast_convert_type(w & jnp.int32(-65536), F32)
    second = lax.bitcast_convert_type(lax.shift_left(w, jnp.full(w.shape, 16, jnp.int32)), F32)
    return first, second


def _out_route_kernel(oa_ref, ob_ref, x_ref, mod_ref, wo_ref, g_ref, wr_ref, br_ref, cin_ref,
                      x1_in_ref, x1_ref, hp_ref, ti_ref, tw_ref, rk_ref, cnt_ref, tri_sc, carry_sc):
    del x1_in_ref
    i = pl.program_id(0)
    tm = x_ref.shape[0]

    @pl.when(i == 0)
    def _():
        r = lax.broadcasted_iota(jnp.int32, (tm, tm), 0)
        c = lax.broadcasted_iota(jnp.int32, (tm, tm), 1)
        tri_sc[...] = jnp.where(r < c, 1.0, 0.0).astype(BF16)
        carry_sc[...] = cin_ref[...]

    mod = mod_ref[...]
    o = jnp.concatenate([oa_ref[s] for s in range(oa_ref.shape[0])]
                        + [ob_ref[s] for s in range(ob_ref.shape[0])], axis=1)
    mix = jnp.dot(o.astype(F32), wo_ref[...], preferred_element_type=F32)
    x1 = x_ref[...] + mod[2:3, :] * mix
    x1_ref[...] = x1.astype(BF16)
    y = x1 * lax.rsqrt(jnp.mean(x1 * x1, axis=-1, keepdims=True) + EPS) * g_ref[...]
    h = y * (1.0 + mod[4:5, :]) + mod[3:4, :]
    hp_ref[...] = _pack_bf16_pairs(h)

    logits = lax.dot_general(wr_ref[...], h, (((1,), (1,)), ((), ())),
                             preferred_element_type=F32) + br_ref[...]
    eid = lax.broadcasted_iota(jnp.int32, logits.shape, 0).astype(F32)
    work = logits
    vals, ids = [], []
    chosen = jnp.zeros(logits.shape, F32)
    for _ in range(TOP_K):
        v = jnp.max(work, axis=0, keepdims=True)
        e = jnp.min(jnp.where(work == v, eid, float(N_EXPERTS)), axis=0, keepdims=True)
        hit = eid == e
        vals.append(v)
        ids.append(e)
        chosen = jnp.where(hit, 1.0, chosen)
        work = jnp.where(hit, -jnp.inf, work)
    ex = [jnp.exp(v - vals[0]) for v in vals]
    den = ex[0] + ex[1] + ex[2] + ex[3]

    before = jnp.dot(chosen.astype(BF16), tri_sc[...], preferred_element_type=F32) + carry_sc[...]
    slot = lax.broadcasted_iota(jnp.int32, (SUBLANES, tm), 0)
    ti = jnp.zeros((SUBLANES, tm), F32)
    tw = jnp.zeros((SUBLANES, tm), F32)
    rk = jnp.zeros((SUBLANES, tm), F32)
    for kk in range(TOP_K):
        r_k = jnp.sum(jnp.where(eid == ids[kk], before, 0.0), axis=0, keepdims=True)
        ti = jnp.where(slot == kk, ids[kk], ti)
        tw = jnp.where(slot == kk, ex[kk] / den, tw)
        rk = jnp.where(slot == kk, r_k, rk)
    ti_ref[...] = ti.astype(jnp.int32)
    tw_ref[...] = tw
    rk_ref[...] = rk.astype(jnp.int32)
    carry = carry_sc[...] + jnp.sum(chosen, axis=1, keepdims=True)
    carry_sc[...] = carry
    cnt_ref[...] = carry


def _out_route(b, oa, ob, x2, mod, w_out, g_ffn, w_router, b_router, counts_in, x1_buf, seq):
    n = x2.shape[0]
    tiles = seq // TOK_TILE
    row = lambda i: (b * tiles + i, 0)
    local = lambda i: (i, 0)
    by_lane = lambda i: (0, i)
    fixed = lambda i: (0, 0)
    return pl.pallas_call(
        _out_route_kernel,
        grid=(tiles,),
        in_specs=[
            pl.BlockSpec((oa.shape[0], TOK_TILE, LANES), lambda i: (0, b * tiles + i, 0)),
            pl.BlockSpec((ob.shape[0], TOK_TILE, LANES), lambda i: (0, i, 0)),
            pl.BlockSpec((TOK_TILE, D_MODEL), row),
            pl.BlockSpec((None, SUBLANES, D_MODEL), lambda i: (b, 0, 0)),
            pl.BlockSpec((D_MODEL, D_MODEL), fixed),
            pl.BlockSpec((1, D_MODEL), fixed),
            pl.BlockSpec((N_EXPERTS, D_MODEL), fixed),
            pl.BlockSpec((N_EXPERTS, 1), fixed),
            pl.BlockSpec((N_EXPERTS, 1), fixed),
            pl.BlockSpec(memory_space=pl.ANY),
        ],
        out_specs=[
            pl.BlockSpec((TOK_TILE, D_MODEL), row),
            pl.BlockSpec((TOK_TILE, D_MODEL // 2), local),
            pl.BlockSpec((SUBLANES, TOK_TILE), by_lane),
            pl.BlockSpec((SUBLANES, TOK_TILE), by_lane),
            pl.BlockSpec((SUBLANES, TOK_TILE), by_lane),
            pl.BlockSpec((N_EXPERTS, 1), fixed),
        ],
        out_shape=[
            jax.ShapeDtypeStruct((n, D_MODEL), BF16),
            jax.ShapeDtypeStruct((seq, D_MODEL // 2), jnp.int32),
            jax.ShapeDtypeStruct((SUBLANES, seq), jnp.int32),
            jax.ShapeDtypeStruct((SUBLANES, seq), F32),
            jax.ShapeDtypeStruct((SUBLANES, seq), jnp.int32),
            jax.ShapeDtypeStruct((N_EXPERTS, 1), F32),
        ],
        input_output_aliases={9: 0},
        scratch_shapes=[pltpu.VMEM((TOK_TILE, TOK_TILE), BF16), pltpu.VMEM((N_EXPERTS, 1), F32)],
        compiler_params=_cparams(1, VMEM_LIMIT),
        name="out_route",
    )(oa, ob, x2, mod, w_out, g_ffn.reshape(1, D_MODEL), w_router.T,
      b_router.reshape(N_EXPERTS, 1), counts_in, x1_buf)


def _experts_kernel(tiles_ref, rows_ref, base_ref, next_ref, xs_ref, wgu_ref, bgu_ref, wd_ref,
                    bd_ref, y_ref, x_buf, y_buf, x_sem, y_sem):
    e = pl.program_id(0)
    n_tiles = tiles_ref[e]
    n_rows = rows_ref[e]
    n_live = base_ref[N_EXPERTS]
    region = e * EXPERT_ROWS

    def x_copy(row, slot):
        return pltpu.make_async_copy(xs_ref.at[pl.ds(row, ROW_TILE), :], x_buf.at[slot],
                                     x_sem.at[slot])

    def y_copy(row, slot):
        return pltpu.make_async_copy(y_buf.at[slot], y_ref.at[pl.ds(row, ROW_TILE), :],
                                     y_sem.at[slot])

    @pl.when(e == 0)
    def _():
        y_buf[...] = jnp.zeros(y_buf.shape, jnp.int32)

        @pl.when(n_live > 0)
        def _():
            x_copy(pl.multiple_of(next_ref[N_EXPERTS], ROW_TILE), 0).start()

    def tile(j, carry):
        g = base_ref[e] + j
        slot = g % 2
        row = pl.multiple_of(region + j * ROW_TILE, ROW_TILE)
        x_copy(row, slot).wait()
        next_row = jnp.where(j + 1 < n_tiles, row + ROW_TILE, next_ref[e])

        @pl.when(g + 1 < n_live)
        def _():
            x_copy(pl.multiple_of(next_row, ROW_TILE), 1 - slot).start()

        @pl.when(g >= 2)
        def _():
            y_copy(0, slot).wait()

        needed = jnp.minimum(n_rows - j * ROW_TILE, ROW_TILE)
        for rows in range(ROW_STEP, ROW_TILE + 1, ROW_STEP):
            @pl.when(jnp.logical_and(needed > rows - ROW_STEP, needed <= rows))
            def _(rows=rows):
                used = lax.broadcasted_iota(jnp.int32, (rows, 1), 0) < needed
                first, second = _unpack_bf16_pairs(jnp.where(used, x_buf[slot, :rows, :], 0))
                x = jnp.concatenate([first, second], axis=1)
                acc = jnp.zeros((rows, D_MODEL), F32)
                for c in range(D_EXPERT // FF_CHUNK):
                    lo_c, hi_c = c * FF_CHUNK, (c + 1) * FF_CHUNK
                    gate = (jnp.dot(x, wgu_ref[:, lo_c:hi_c], preferred_element_type=F32)
                            + bgu_ref[:, lo_c:hi_c])
                    up = (jnp.dot(x, wgu_ref[:, D_EXPERT + lo_c:D_EXPERT + hi_c],
                                  preferred_element_type=F32)
                          + bgu_ref[:, D_EXPERT + lo_c:D_EXPERT + hi_c])
                    gate = jnp.minimum(gate, SWIGLU_LIMIT)
                    up = jnp.clip(up, -SWIGLU_LIMIT, SWIGLU_LIMIT)
                    act = (up + 1.0) * (gate * jax.nn.sigmoid(SWIGLU_ALPHA * gate))
                    acc = acc + jnp.dot(act, wd_ref[lo_c:hi_c, :],
                                        preferred_element_type=F32)
                y_buf[slot, :rows, :] = _pack_bf16_pairs(acc + bd_ref[...])

        y_copy(row, slot).start()
        return carry

    lax.fori_loop(0, n_tiles, tile, 0)

    @pl.when(e == pl.num_programs(0) - 1)
    def _():
        for back in (2, 1):
            @pl.when(n_live >= back)
            def _():
                y_copy(0, (n_live - back) % 2).wait()


def _experts(counts, xs, w_gate_up, b_gate_up, w_down, b_down):
    n_rows, width = xs.shape
    tiles = (counts + ROW_TILE - 1) // ROW_TILE
    base = jnp.concatenate([jnp.zeros((1,), jnp.int32), jnp.cumsum(tiles)]).astype(jnp.int32)
    region = jnp.arange(N_EXPERTS, dtype=jnp.int32) * EXPERT_ROWS
    later = jnp.arange(N_EXPERTS)[None, :] > jnp.arange(-1, N_EXPERTS)[:, None]
    cand = jnp.where(jnp.logical_and(later, (tiles > 0)[None, :]), region[None, :], n_rows)
    nxt = jnp.min(cand, axis=1)
    next_row = jnp.concatenate([nxt[1:], nxt[:1]]).astype(jnp.int32)
    by_expert = lambda e, *_: (e, 0, 0)
    return pl.pallas_call(
        _experts_kernel,
        grid_spec=pltpu.PrefetchScalarGridSpec(
            num_scalar_prefetch=4,
            grid=(N_EXPERTS,),
            in_specs=[
                pl.BlockSpec(memory_space=pl.ANY),
                pl.BlockSpec((None, D_MODEL, 2 * D_EXPERT), by_expert),
                pl.BlockSpec((None, 1, 2 * D_EXPERT), by_expert),
                pl.BlockSpec((None, D_EXPERT, D_MODEL), by_expert),
                pl.BlockSpec((None, 1, D_MODEL), by_expert),
            ],
            out_specs=pl.BlockSpec(memory_space=pl.ANY),
            scratch_shapes=[
                pltpu.VMEM((2, ROW_TILE, width), jnp.int32),
                pltpu.VMEM((2, ROW_TILE, D_MODEL // 2), jnp.int32),
                pltpu.SemaphoreType.DMA((2,)),
                pltpu.SemaphoreType.DMA((2,)),
            ],
        ),
        out_shape=jax.ShapeDtypeStruct((n_rows, D_MODEL // 2), jnp.int32),
        compiler_params=_cparams(1, VMEM_LIMIT),
        name="experts",
    )(tiles.astype(jnp.int32), counts.astype(jnp.int32), base, next_row, xs, w_gate_up,
      b_gate_up.reshape(N_EXPERTS, 1, 2 * D_EXPERT), w_down, b_down.reshape(N_EXPERTS, 1, D_MODEL))


def _sc_gather_rows(table, idx):
    m = idx.shape[0]
    width = table.shape[1]
    mesh = plsc.VectorSubcoreMesh(core_axis_name="core", subcore_axis_name="subcore")
    n_workers = mesh.num_cores * mesh.num_subcores
    per_worker = m // n_workers
    n_windows = per_worker // SC_HALF_WINDOW
    assert per_worker * n_workers == m and n_windows * SC_HALF_WINDOW == per_worker

    @pl.kernel(
        out_type=jax.ShapeDtypeStruct((m, width), table.dtype),
        mesh=mesh,
        scratch_types=[
            pltpu.VMEM((SC_HALF_WINDOW,), jnp.int32),
            pltpu.VMEM((SC_HALF_WINDOW,), jnp.int32),
            pltpu.VMEM((SC_HALF_WINDOW, width), table.dtype),
            pltpu.VMEM((SC_HALF_WINDOW, width), table.dtype),
            pltpu.SemaphoreType.DMA,
            pltpu.SemaphoreType.DMA,
            pltpu.SemaphoreType.DMA,
            pltpu.SemaphoreType.DMA,
        ],
    )
    def gather_kernel(table_hbm, idx_hbm, out_hbm, idx0, idx1, rows0, rows1, g0, g1, w0, w1):
        worker = lax.axis_index("subcore") * mesh.num_cores + lax.axis_index("core")
        idx_v, rows_v, g_sem, w_sem = (idx0, idx1), (rows0, rows1), (g0, g1), (w0, w1)

        def window(j):
            return pl.ds(pl.multiple_of(worker * per_worker + j * SC_HALF_WINDOW, SC_HALF_WINDOW),
                         SC_HALF_WINDOW)

        def fetch(j):
            s = j % 2
            pltpu.sync_copy(idx_hbm.at[window(j)], idx_v[s])
            return pltpu.async_copy(table_hbm.at[idx_v[s]], rows_v[s], g_sem[s])

        fetches = {0: fetch(0)}
        writes = {}
        for j in range(n_windows):
            s = j % 2
            if j + 1 < n_windows:
                if j >= 1:
                    writes[j - 1].wait()
                fetches[j + 1] = fetch(j + 1)
            fetches[j].wait()
            writes[j] = pltpu.async_copy(rows_v[s], out_hbm.at[window(j)], w_sem[s])
        for j in range(max(n_windows - 2, 0), n_windows):
            writes[j].wait()

    return gather_kernel(table, idx)


def _sc_scatter_rows_into(table, dst, out_ref, reps):
    m, width = table.shape
    mesh = plsc.VectorSubcoreMesh(core_axis_name="core", subcore_axis_name="subcore")
    n_workers = mesh.num_cores * mesh.num_subcores
    per_worker = m // n_workers
    assert per_worker * n_workers == m and per_worker % SC_WINDOW == 0 and dst.shape == (reps * m,)

    @pl.kernel(
        out_type=(),
        mesh=mesh,
        scratch_types=[
            pltpu.VMEM((SC_WINDOW,), jnp.int32),
            pltpu.VMEM((SC_WINDOW, width), table.dtype),
            pltpu.SemaphoreType.DMA,
        ],
    )
    def scatter_kernel(table_hbm, dst_hbm, out_hbm, dst_v, rows_v, sem):
        worker = lax.axis_index("subcore") * mesh.num_cores + lax.axis_index("core")

        @pl.loop(0, per_worker // SC_WINDOW)
        def _(j):
            base = pl.multiple_of(worker * per_worker + j * SC_WINDOW, SC_WINDOW)
            pltpu.sync_copy(table_hbm.at[pl.ds(base, SC_WINDOW)], rows_v)
            for k in range(reps):
                pltpu.sync_copy(dst_hbm.at[pl.ds(k * m + base, SC_WINDOW)], dst_v)
                pltpu.async_copy(rows_v, out_hbm.at[dst_v], sem).wait()

    scatter_kernel(table, dst, out_ref)


def _combine_dense_kernel(tw_ref, x1_ref, mod_ref, g_ref, y0_ref, y1_ref, y2_ref, y3_ref, o_ref):
    tm = x1_ref.shape[0]
    tw = jnp.concatenate([tw_ref[...], jnp.zeros((LANES - SUBLANES, tm), F32)], axis=0).T
    first = second = None
    for kk, y_ref in enumerate((y0_ref, y1_ref, y2_ref, y3_ref)):
        f_k, s_k = _unpack_bf16_pairs(y_ref[...])
        w_k = tw[:, kk:kk + 1]
        first = w_k * f_k if kk == 0 else first + w_k * f_k
        second = w_k * s_k if kk == 0 else second + w_k * s_k
    ffn = jnp.concatenate([first, second], axis=1)
    x2 = x1_ref[...].astype(F32) + mod_ref[5:6, :] * ffn
    o_ref[...] = x2 * lax.rsqrt(jnp.mean(x2 * x2, axis=-1, keepdims=True) + EPS) * g_ref[...]


def _combine_dense(b, tw, x1, mod, g_final, yg, out_buf, seq):
    n = x1.shape[0]
    width = yg.shape[1]
    tiles = seq // TOK_TILE
    row = lambda i: (b * tiles + i, 0)
    deep = pl.Buffered(COMBINE_BUFFERS)
    slot = lambda kk: pl.BlockSpec((TOK_TILE, width), lambda i: (kk * tiles + i, 0),
                                   pipeline_mode=deep)

    def stream(tw_ref, x1_ref, mod_ref, g_ref, yg_ref, o_in_ref, o_ref):
        del o_in_ref
        pltpu.emit_pipeline(
            _combine_dense_kernel,
            grid=(tiles,),
            in_specs=[
                pl.BlockSpec((SUBLANES, TOK_TILE), lambda i: (0, i)),
                pl.BlockSpec((TOK_TILE, D_MODEL), row, pipeline_mode=deep),
                pl.BlockSpec((SUBLANES, D_MODEL), lambda i: (0, 0)),
                pl.BlockSpec((1, D_MODEL), lambda i: (0, 0)),
            ] + [slot(kk) for kk in range(TOP_K)],
            out_specs=[pl.BlockSpec((TOK_TILE, D_MODEL), row)],
        )(tw_ref, x1_ref, mod_ref, g_ref, yg_ref, yg_ref, yg_ref, yg_ref, o_ref)

    anywhere = pl.BlockSpec(memory_space=pl.ANY)
    return pl.pallas_call(
        stream,
        in_specs=[anywhere] * 6,
        out_specs=anywhere,
        out_shape=jax.ShapeDtypeStruct((n, D_MODEL), F32),
        input_output_aliases={5: 0},
        compiler_params=pltpu.CompilerParams(vmem_limit_bytes=VMEM_LIMIT),
        name="combine",
    )(tw, x1, mod[b], g_final.reshape(1, D_MODEL), yg, out_buf)


def _rope_base(positions):
    half = ROPE_DIM // 2
    inv_freq = ROPE_THETA ** (-jnp.arange(0, ROPE_DIM, 2, dtype=F32) / ROPE_DIM)
    n = positions.size
    pos = jnp.broadcast_to(positions.reshape(n, 1).astype(F32), (n, half)).reshape(-1, LANES)
    ang = pos * jnp.tile(inv_freq, LANES // half)
    cos, sin = lax.optimization_barrier((jnp.cos(ang), jnp.sin(ang)))
    pad = jnp.zeros((n, LANES - ROPE_DIM), F32)
    return jnp.concatenate([cos.reshape(n, half), sin.reshape(n, half), pad], axis=1)


def kernel(x, c, positions, w_ada, b_ada, g_mix_norm, w_in, rel_bias, lambda_q1, lambda_k1,
           lambda_q2, lambda_k2, g_subln, w_out, g_ffn_norm, w_router, b_router, w_gate_up,
           b_gate_up, w_down, b_down, g_final):
    batch, seq, _ = x.shape
    depth = w_ada.shape[0]
    assert depth == 1, "the combine kernel applies the final norm, so it must follow the only layer"
    n = batch * seq
    assert n <= EXPERT_ROWS and EXPERT_ROWS % ROW_TILE == 0
    rope_base = _rope_base(positions)
    x2 = x.reshape(n, D_MODEL)
    for l in range(depth):
        lambda_init = 0.8 - 0.6 * math.exp(-0.3 * l)
        mod = _ada(c, w_ada[l], b_ada[l])
        qa, ka, va, qb, kb, vb = _in_proj(x2, mod, g_mix_norm[l], w_in[l],
                                          rope_base, seq)
        oa = _attn_a(qa, ka, va, _rel_bias_rows(rel_bias[l]), batch, seq)
        lam = (jnp.exp(jnp.sum(lambda_q1[l].astype(F32) * lambda_k1[l].astype(F32)))
               - jnp.exp(jnp.sum(lambda_q2[l].astype(F32) * lambda_k2[l].astype(F32)))
               + lambda_init).reshape(1)
        xs_ref = jax.new_ref(lax.empty((N_EXPERTS * EXPERT_ROWS, D_MODEL // 2), jnp.int32))
        x1 = lax.empty((n, D_MODEL), BF16)
        counts = jnp.zeros((N_EXPERTS, 1), F32)
        pos_rows, weight_rows = [], []
        for b in range(batch):
            ob = _attn_b(b, lam, qb, kb, vb, g_subln[l], seq, 1.0 - lambda_init)
            x1, hp, top_i, top_w, rank, counts = _out_route(
                b, oa, ob, x2, mod, w_out[l], g_ffn_norm[l], w_router[l], b_router[l], counts, x1,
                seq)
            pos = top_i * EXPERT_ROWS + rank
            pos_rows.append(pos)
            weight_rows.append(top_w)
            dst = jnp.concatenate([pos[kk] for kk in range(TOP_K)])
            _sc_scatter_rows_into(hp, dst, xs_ref, TOP_K)
        xs = jax.freeze(xs_ref)
        y = _experts(counts[:, 0].astype(jnp.int32), xs, w_gate_up[l], b_gate_up[l], w_down[l],
                     b_down[l])
        x2 = lax.empty((n, D_MODEL), F32)
        for b in range(batch):
            yg = _sc_gather_rows(y, jnp.concatenate([pos_rows[b][kk] for kk in range(TOP_K)]))
            x2 = _combine_dense(b, weight_rows[b], x1, mod, g_final, yg, x2, seq)
    return x2.reshape(batch, seq, D_MODEL)
```

```python
import functools
import math

import jax
import jax.numpy as jnp
from jax import lax
from jax.experimental import pallas as pl
from jax.experimental.pallas import tpu as pltpu
from jax.experimental.pallas import tpu_sc as plsc

D_MODEL = 1024
CHUNK = 64
HEAD_DIM = 64
A_HEADS = 8
A_WIDTH = A_HEADS * HEAD_DIM
LEFT_CHUNKS = 8
MAX_REL = 128
B_HEADS = 4
B_QK_DIM = HEAD_DIM
B_V_DIM = 2 * HEAD_DIM
B_WIDTH = B_HEADS * B_V_DIM
ROPE_THETA = 500000.0
ROPE_DIM = B_QK_DIM // 4
N_EXPERTS = 32
TOP_K = 4
D_EXPERT = D_MODEL
SWIGLU_LIMIT = 7.0
SWIGLU_ALPHA = 1.702
EPS = 1e-6
NEG_INF = -1e30
LOG2_E = math.log2(math.e)
N_MOD = 6

LANES = 128
SUBLANES = 8
VMEM_BYTES_V7X = 64 * 1024 * 1024
VMEM_LIMIT = VMEM_BYTES_V7X * 7 // 8

TOK_TILE = 512
A_QBLK = 2 * CHUNK
A_BAND = (LEFT_CHUNKS + 2) * CHUNK
A_ROLL = A_BAND + A_QBLK
B_TQ = 512
B_TK = 512
B_SUB = 128
ROW_TILE = 512
ROW_STEP = 128
FF_CHUNK = 512
EXPERT_ROWS = 16384
COMBINE_BUFFERS = 3
SC_WINDOW = 128
SC_HALF_WINDOW = 64

F32 = jnp.float32
BF16 = jnp.bfloat16


def _cparams(n_axes, vmem=None):
    return pltpu.CompilerParams(
        dimension_semantics=("arbitrary",) * n_axes,
        vmem_limit_bytes=vmem,
    )


def _ada_kernel(c_ref, w_lo_ref, w_hi_ref, b_ref, o_ref):
    c = c_ref[...]
    act = c * jax.nn.sigmoid(c)
    half = c.shape[1] // 2
    o_ref[...] = (jnp.dot(act[:, :half], w_lo_ref[...], preferred_element_type=F32)
                  + jnp.dot(act[:, half:], w_hi_ref[...], preferred_element_type=F32) + b_ref[...])


def _ada(c, w_ada, b_ada):
    b = c.shape[0]
    rows = -(-b // SUBLANES) * SUBLANES
    c_pad = jnp.pad(c, ((0, rows - b), (0, 0)))
    n_out = w_ada.shape[1]
    out = pl.pallas_call(
        _ada_kernel,
        grid=(n_out // D_MODEL,),
        in_specs=[
            pl.BlockSpec((rows, D_MODEL), lambda j: (0, 0)),
            pl.BlockSpec((D_MODEL // 2, D_MODEL), lambda j: (0, j)),
            pl.BlockSpec((D_MODEL // 2, D_MODEL), lambda j: (1, j)),
            pl.BlockSpec((1, D_MODEL), lambda j: (0, j)),
        ],
        out_specs=pl.BlockSpec((rows, D_MODEL), lambda j: (0, j)),
        out_shape=jax.ShapeDtypeStruct((rows, n_out), F32),
        compiler_params=_cparams(1),
        name="ada",
    )(c_pad, w_ada, w_ada, b_ada.reshape(1, n_out))
    mod = out[:b].reshape(b, N_MOD, D_MODEL)
    return jnp.pad(mod, ((0, 0), (0, SUBLANES - N_MOD), (0, 0)))


def _in_proj_kernel(x_ref, mod_ref, g_ref, w_ref, rb_ref,
                    qa_ref, ka_ref, va_ref, qb_ref, kb_ref, vb_ref):
    x = x_ref[...]
    mod = mod_ref[...]
    y = x * lax.rsqrt(jnp.mean(x * x, axis=-1, keepdims=True) + EPS) * g_ref[...]
    h = y * (1.0 + mod[1:2, :]) + mod[0:1, :]
    q_scale = HEAD_DIM ** -0.5

    half = ROPE_DIM // 2
    rb = rb_ref[...]
    lane = lax.broadcasted_iota(jnp.int32, rb.shape, 1)
    cos_lo = jnp.where(lane < half, rb, 0.0)
    sin_hi = jnp.where(jnp.logical_and(lane >= half, lane < ROPE_DIM), rb, 0.0)
    cos_pair = cos_lo + pltpu.roll(cos_lo, half, 1)
    rc = (cos_pair + pltpu.roll(cos_pair, B_QK_DIM, 1)
          + jnp.where(lane % B_QK_DIM >= ROPE_DIM, 1.0, 0.0))
    sin_lo = pltpu.roll(sin_hi, LANES - half, 1)
    rm = -(sin_lo + pltpu.roll(sin_lo, B_QK_DIM, 1))
    rp = sin_hi + pltpu.roll(sin_hi, B_QK_DIM, 1)

    def rope(p):
        cols = []
        for s in range(p.shape[1] // LANES):
            v = p[:, s * LANES:(s + 1) * LANES]
            cols.append(v * rc + pltpu.roll(v, LANES - ROPE_DIM // 2, 1) * rm
                        + pltpu.roll(v, ROPE_DIM // 2, 1) * rp)
        return jnp.concatenate(cols, axis=1)

    outs = (qa_ref, ka_ref, va_ref, qb_ref, kb_ref, vb_ref)
    for j, o_ref in enumerate(outs):
        p = jnp.dot(h, w_ref[:, j * A_WIDTH:(j + 1) * A_WIDTH], preferred_element_type=F32)
        if j in (3, 4):
            p = rope(p)
        if j in (0, 3):
            p = p * (q_scale * LOG2_E)
        pb = p.astype(BF16)
        for s in range(A_WIDTH // LANES):
            o_ref[s] = pb[:, s * LANES:(s + 1) * LANES]


def _in_proj(x2, mod, g_mix, w_in, rope_base, seq):
    n = x2.shape[0]
    tiles_per_seq = seq // TOK_TILE
    row = lambda i: (i, 0)
    fixed = lambda i: (0, 0)
    n_slabs = A_WIDTH // LANES
    out_sd = jax.ShapeDtypeStruct((n_slabs, n, LANES), BF16)
    return pl.pallas_call(
        _in_proj_kernel,
        grid=(n // TOK_TILE,),
        in_specs=[
            pl.BlockSpec((TOK_TILE, D_MODEL), row),
            pl.BlockSpec((None, SUBLANES, D_MODEL), lambda i: (i // tiles_per_seq, 0, 0)),
            pl.BlockSpec((1, D_MODEL), fixed),
            pl.BlockSpec(w_in.shape, fixed),
            pl.BlockSpec((TOK_TILE, LANES), row),
        ],
        out_specs=[pl.BlockSpec((n_slabs, TOK_TILE, LANES), lambda i: (0, i, 0))] * 6,
        out_shape=[out_sd] * 6,
        compiler_params=_cparams(1, VMEM_LIMIT),
        name="in_proj",
    )(x2, mod, g_mix.reshape(1, D_MODEL), w_in, rope_base)


def _attn_a_kernel(q_ref, kp_ref, kc_ref, vp_ref, vc_ref, bias_ref, o_ref, k_sc, v_sc, bias_sc, *,
                   nblk):
    step = pl.program_id(0)
    g = step % nblk
    n_pairs, blk, _ = q_ref.shape
    lane = lax.broadcasted_iota(jnp.int32, (A_QBLK, LANES), 1)
    col = lax.broadcasted_iota(jnp.int32, (A_QBLK, A_BAND), 1)
    ones = jnp.ones((A_BAND, LANES), BF16)

    @pl.when(step == 0)
    def _():
        q_chunk = lax.broadcasted_iota(jnp.int32, (A_QBLK, A_BAND), 0) // CHUNK
        k_chunk = col // CHUNK
        in_band = jnp.logical_and(k_chunk >= q_chunk, k_chunk <= q_chunk + LEFT_CHUNKS)
        for h in range(bias_ref.shape[0]):
            rolled = pltpu.roll(jnp.broadcast_to(bias_ref[h], (A_QBLK, A_ROLL)), 0, 1,
                                stride=1, stride_axis=0)
            bias_sc[h] = jnp.where(in_band, rolled[:, :A_BAND], NEG_INF)

    def pair(p):
        k_buf = k_sc.at[p % 2]
        v_buf = v_sc.at[p % 2]
        k_buf[0:blk, :] = kp_ref[p]
        k_buf[blk:2 * blk, :] = kc_ref[p]
        v_buf[0:blk, :] = vp_ref[p]
        v_buf[blk:2 * blk, :] = vc_ref[p]

        def scores(m, hh):
            r0 = m * A_QBLK
            q = q_ref[p, r0:r0 + A_QBLK, :]
            in_head = (lane < HEAD_DIM) if hh == 0 else (lane >= HEAD_DIM)
            qh = jnp.where(in_head, q, jnp.zeros_like(q))
            return lax.dot_general(qh, k_buf[r0:r0 + A_BAND, :], (((1,), (1,)), ((), ())),
                                   preferred_element_type=F32)

        chains = [(m, hh) for m in range(blk // A_QBLK) for hh in range(2)]
        s_next = scores(*chains[0])
        halves = []
        for i, (m, hh) in enumerate(chains):
            r0 = m * A_QBLK
            s = s_next
            if i + 1 < len(chains):
                s_next = scores(*chains[i + 1])
            valid = jnp.logical_or(g > 0, col + r0 >= blk)
            s = jnp.where(valid, s + bias_sc[2 * p + hh], NEG_INF)
            pr = jnp.exp2(s - jnp.max(s, axis=1, keepdims=True))
            v_ext = jnp.concatenate([v_buf[r0:r0 + A_BAND, :], ones], axis=1)
            pv = jnp.dot(pr.astype(BF16), v_ext, preferred_element_type=F32)
            halves.append(pv[:, :LANES] / pv[:, LANES:])
            if hh == 1:
                o_ref[p, r0:r0 + A_QBLK, :] = jnp.where(lane < HEAD_DIM, halves[0],
                                                        halves[1]).astype(BF16)
                halves = []

    for p in range(n_pairs):
        pair(p)


def _attn_a(qa, ka, va, bias_rows, batch, seq):
    n_pairs, n, _ = qa.shape
    blk = LEFT_CHUNKS * CHUNK
    nblk = seq // blk
    cur = lambda g: (0, g, 0)
    prev = lambda g: (0, jnp.where(g % nblk == 0, g, g - 1), 0)
    slab = (n_pairs, blk, LANES)
    return pl.pallas_call(
        functools.partial(_attn_a_kernel, nblk=nblk),
        grid=(batch * nblk,),
        in_specs=[
            pl.BlockSpec(slab, cur),
            pl.BlockSpec(slab, prev),
            pl.BlockSpec(slab, cur),
            pl.BlockSpec(slab, prev),
            pl.BlockSpec(slab, cur),
            pl.BlockSpec(bias_rows.shape, lambda g: (0, 0, 0)),
        ],
        out_specs=pl.BlockSpec(slab, cur),
        out_shape=jax.ShapeDtypeStruct((n_pairs, n, LANES), BF16),
        scratch_shapes=[
            pltpu.VMEM((2, 2 * blk, LANES), BF16),
            pltpu.VMEM((2, 2 * blk, LANES), BF16),
            pltpu.VMEM((bias_rows.shape[0], A_QBLK, A_BAND), F32),
        ],
        compiler_params=_cparams(1),
        name="attn_a",
    )(qa, ka, ka, va, va, bias_rows)


def _rel_bias_rows(rel_table):
    t = rel_table.astype(F32) * LOG2_E
    far = t[:, 2 * MAX_REL:]
    n_far = LEFT_CHUNKS * CHUNK - MAX_REL
    row = jnp.concatenate([
        jnp.broadcast_to(far, (t.shape[0], n_far)),
        t[:, 2 * MAX_REL:0:-1],
        jnp.broadcast_to(far, (t.shape[0], A_ROLL - A_BAND)),
    ], axis=1)
    return row.reshape(t.shape[0], 1, A_ROLL)


def _attn_b_kernel(lam_ref, q_ref, k_ref, v_ref, g_ref, o_ref, q_sc, qn_sc, s_sc, m_sc, acc_sc, *,
                   out_scale):
    n_tiles = q_ref.shape[0] // B_TQ
    n_sub = B_TQ // B_SUB
    first_slot = 2
    lane = lax.broadcasted_iota(jnp.int32, (B_TQ, LANES), 1)
    ones = jnp.ones((B_TK, LANES), BF16)
    lam = lam_ref[0]

    def split_components(t, dst):
        q = q_ref[pl.ds(pl.multiple_of(t * B_TQ, B_TQ), B_TQ), :]
        dst[0] = jnp.where(lane < B_QK_DIM, q, jnp.zeros_like(q))
        dst[1] = jnp.where(lane >= B_QK_DIM, q, jnp.zeros_like(q))

    def score_rows(src, k, slot, sub, c):
        rows = pl.ds(sub * B_SUB, B_SUB)
        s_sc[slot, c, rows, :] = lax.dot_general(src[c, rows, :], k, (((1,), (1,)), ((), ())),
                                                 preferred_element_type=F32)

    def update_rows(s, v_ext, sub, c):
        rows = pl.ds(sub * B_SUB, B_SUB)
        m_prev = m_sc[c, rows, :]
        m_new = jnp.maximum(m_prev, jnp.max(s, axis=1, keepdims=True))
        alpha = jnp.exp2(m_prev - m_new)
        p = jnp.exp2(s - jnp.concatenate([m_new] * (s.shape[1] // LANES), axis=1))
        pv = jnp.dot(p.astype(BF16), v_ext, preferred_element_type=F32)
        acc_sc[c, rows, :] = jnp.concatenate([alpha, alpha], axis=1) * acc_sc[c, rows, :] + pv
        m_sc[c, rows, :] = m_new

    def key_block(blk):
        return k_ref[pl.ds(pl.multiple_of(blk * B_TK, B_TK), B_TK), :]

    def value_block(blk):
        return jnp.concatenate([v_ref[pl.ds(pl.multiple_of(blk * B_TK, B_TK), B_TK), :], ones], axis=1)

    def step(blk, slot, next_blk, next_slot):
        v_ext = value_block(blk)
        k_next = key_block(next_blk)
        for sub in range(n_sub):
            for c in range(2):
                update_rows(s_sc[slot, c, pl.ds(sub * B_SUB, B_SUB), :], v_ext, sub, c)
                score_rows(q_sc, k_next, next_slot, sub, c)

    def last_step(t, slot):
        v_ext = value_block(t)
        k0 = key_block(0)
        col_c = lax.broadcasted_iota(jnp.int32, (B_SUB, B_TK), 1) // CHUNK
        row_c = lax.broadcasted_iota(jnp.int32, (B_SUB, B_TK), 0) // CHUNK
        for sub in range(n_sub):
            keep = col_c <= row_c + sub * (B_SUB // CHUNK)
            for c in range(2):
                s = s_sc[slot, c, pl.ds(sub * B_SUB, B_SUB), :]
                update_rows(jnp.where(keep, s, NEG_INF), v_ext, sub, c)
                score_rows(qn_sc, k0, first_slot, sub, c)

    def start_tile(t):
        split_components(t, q_sc)
        split_components(jnp.minimum(t + 1, n_tiles - 1), qn_sc)
        m_sc[...] = jnp.full(m_sc.shape, NEG_INF, F32)
        acc_sc[...] = jnp.zeros(acc_sc.shape, F32)

    def finish_tile(t):
        a0 = acc_sc[0]
        a1 = acc_sc[1]
        o = a0[:, :B_V_DIM] / a0[:, B_V_DIM:] - lam * (a1[:, :B_V_DIM] / a1[:, B_V_DIM:])
        y = o * lax.rsqrt(jnp.mean(o * o, axis=-1, keepdims=True) + EPS) * g_ref[...]
        o_ref[pl.ds(pl.multiple_of(t * B_TQ, B_TQ), B_TQ), :] = (y * out_scale).astype(BF16)

    start_tile(0)
    k0 = key_block(0)
    for sub in range(n_sub):
        for c in range(2):
            score_rows(q_sc, k0, first_slot, sub, c)
    s_sc[0] = s_sc[first_slot]
    last_step(0, 0)
    finish_tile(0)

    def tile(t, carry):
        start_tile(t)
        step(0, first_slot, 1, 1)

        def pair(p, carry):
            odd = 2 * p + 1
            step(odd, 1, odd + 1, 0)
            step(odd + 1, 0, odd + 2, 1)
            return carry

        lax.fori_loop(0, (t - 1) // 2, pair, 0)

        @pl.when(t % 2 == 1)
        def _():
            last_step(t, 1)

        @pl.when(t % 2 == 0)
        def _():
            step(t - 1, 1, t, 0)
            last_step(t, 0)

        finish_tile(t)
        return carry

    lax.fori_loop(1, n_tiles, tile, 0)


def _attn_b(b, lam, qb, kb, vb, g_subln, seq, out_scale):
    assert B_TQ == B_TK and B_V_DIM == LANES
    by_head = lambda h: (h, b, 0)
    return pl.pallas_call(
        functools.partial(_attn_b_kernel, out_scale=out_scale),
        grid=(B_HEADS,),
        in_specs=[
            pl.BlockSpec(memory_space=pltpu.SMEM),
            pl.BlockSpec((None, seq, LANES), by_head),
            pl.BlockSpec((None, seq, LANES), by_head),
            pl.BlockSpec((None, seq, LANES), by_head),
            pl.BlockSpec((1, B_V_DIM), lambda h: (0, 0)),
        ],
        out_specs=pl.BlockSpec((None, seq, LANES), lambda h: (h, 0, 0)),
        out_shape=jax.ShapeDtypeStruct((B_HEADS, seq, LANES), BF16),
        scratch_shapes=[
            pltpu.VMEM((2, B_TQ, LANES), BF16),
            pltpu.VMEM((2, B_TQ, LANES), BF16),
            pltpu.VMEM((3, 2, B_TQ, B_TK), F32),
            pltpu.VMEM((2, B_TQ, LANES), F32),
            pltpu.VMEM((2, B_TQ, 2 * LANES), F32),
        ],
        compiler_params=_cparams(1),
        name="attn_b",
    )(lam, qb, kb, vb, g_subln.reshape(1, B_V_DIM))


def _pack_bf16_pairs(v):
    half = v.shape[1] // 2
    vb = v.astype(BF16)
    hi = lax.bitcast_convert_type(vb[:, :half].astype(F32), jnp.int32)
    lo = lax.bitcast_convert_type(vb[:, half:].astype(F32), jnp.int32)
    return hi | lax.shift_right_logical(lo, jnp.full(lo.shape, 16, jnp.int32))


def _unpack_bf16_pairs(w):
    first = lax.bitcast_convert_type(w & jnp.int32(-65536), F32)
    second = lax.bitcast_convert_type(lax.shift_left(w, jnp.full(w.shape, 16, jnp.int32)), F32)
    return first, second


def _out_route_kernel(oa_ref, ob_ref, x_ref, mod_ref, wo_ref, g_ref, wr_ref, br_ref, cin_ref,
                      x1_in_ref, x1_ref, hp_ref, tw_ref, pos_ref, cnt_ref, tri_sc, carry_sc):
    del x1_in_ref
    i = pl.program_id(0)
    tm = x_ref.shape[0]

    @pl.when(i == 0)
    def _():
        r = lax.broadcasted_iota(jnp.int32, (tm, tm), 0)
        c = lax.broadcasted_iota(jnp.int32, (tm, tm), 1)
        tri_sc[...] = jnp.where(r < c, 1.0, 0.0).astype(BF16)
        carry_sc[...] = cin_ref[...]

    mod = mod_ref[...]
    o = jnp.concatenate([oa_ref[s] for s in range(oa_ref.shape[0])]
                        + [ob_ref[s] for s in range(ob_ref.shape[0])], axis=1)
    mix = jnp.dot(o.astype(F32), wo_ref[...], preferred_element_type=F32)
    x1 = x_ref[...] + mod[2:3, :] * mix
    x1_ref[...] = x1.astype(BF16)
    y = x1 * lax.rsqrt(jnp.mean(x1 * x1, axis=-1, keepdims=True) + EPS) * g_ref[...]
    h = y * (1.0 + mod[4:5, :]) + mod[3:4, :]
    hp_ref[...] = _pack_bf16_pairs(h)

    logits = lax.dot_general(wr_ref[...], h, (((1,), (1,)), ((), ())),
                             preferred_element_type=F32) + br_ref[...]
    eid = lax.broadcasted_iota(jnp.int32, logits.shape, 0).astype(F32)
    work = logits
    vals, ids = [], []
    chosen = jnp.zeros(logits.shape, F32)
    for _ in range(TOP_K):
        v = jnp.max(work, axis=0, keepdims=True)
        e = jnp.min(jnp.where(work == v, eid, float(N_EXPERTS)), axis=0, keepdims=True)
        hit = eid == e
        vals.append(v)
        ids.append(e)
        chosen = jnp.where(hit, 1.0, chosen)
        work = jnp.where(hit, -jnp.inf, work)
    ex = [jnp.exp(v - vals[0]) for v in vals]
    den = ex[0] + ex[1] + ex[2] + ex[3]

    before = jnp.dot(chosen.astype(BF16), tri_sc[...], preferred_element_type=F32) + carry_sc[...]
    slot = lax.broadcasted_iota(jnp.int32, (SUBLANES, tm), 0)
    ti = jnp.zeros((SUBLANES, tm), F32)
    tw = jnp.zeros((SUBLANES, tm), F32)
    rk = jnp.zeros((SUBLANES, tm), F32)
    for kk in range(TOP_K):
        r_k = jnp.sum(jnp.where(eid == ids[kk], before, 0.0), axis=0, keepdims=True)
        ti = jnp.where(slot == kk, ids[kk], ti)
        tw = jnp.where(slot == kk, ex[kk] / den, tw)
        rk = jnp.where(slot == kk, r_k, rk)
    tw_ref[...] = tw
    pos_ref[...] = ti.astype(jnp.int32) * EXPERT_ROWS + rk.astype(jnp.int32)
    carry = carry_sc[...] + jnp.sum(chosen, axis=1, keepdims=True)
    carry_sc[...] = carry
    cnt_ref[...] = carry


def _out_route(b, oa, ob, x2, mod, w_out, g_ffn, w_router, b_router, counts_in, x1_buf, seq):
    n = x2.shape[0]
    tiles = seq // TOK_TILE
    row = lambda i: (b * tiles + i, 0)
    local = lambda i: (i, 0)
    by_lane = lambda i: (0, i)
    fixed = lambda i: (0, 0)
    return pl.pallas_call(
        _out_route_kernel,
        grid=(tiles,),
        in_specs=[
            pl.BlockSpec((oa.shape[0], TOK_TILE, LANES), lambda i: (0, b * tiles + i, 0)),
            pl.BlockSpec((ob.shape[0], TOK_TILE, LANES), lambda i: (0, i, 0)),
            pl.BlockSpec((TOK_TILE, D_MODEL), row),
            pl.BlockSpec((None, SUBLANES, D_MODEL), lambda i: (b, 0, 0)),
            pl.BlockSpec((D_MODEL, D_MODEL), fixed),
            pl.BlockSpec((1, D_MODEL), fixed),
            pl.BlockSpec((N_EXPERTS, D_MODEL), fixed),
            pl.BlockSpec((N_EXPERTS, 1), fixed),
            pl.BlockSpec((N_EXPERTS, 1), fixed),
            pl.BlockSpec(memory_space=pl.ANY),
        ],
        out_specs=[
            pl.BlockSpec((TOK_TILE, D_MODEL), row),
            pl.BlockSpec((TOK_TILE, D_MODEL // 2), local),
            pl.BlockSpec((SUBLANES, TOK_TILE), by_lane),
            pl.BlockSpec((SUBLANES, TOK_TILE), by_lane),
            pl.BlockSpec((N_EXPERTS, 1), fixed),
        ],
        out_shape=[
            jax.ShapeDtypeStruct((n, D_MODEL), BF16),
            jax.ShapeDtypeStruct((seq, D_MODEL // 2), jnp.int32),
            jax.ShapeDtypeStruct((SUBLANES, seq), F32),
            jax.ShapeDtypeStruct((SUBLANES, seq), jnp.int32),
            jax.ShapeDtypeStruct((N_EXPERTS, 1), F32),
        ],
        input_output_aliases={9: 0},
        scratch_shapes=[pltpu.VMEM((TOK_TILE, TOK_TILE), BF16), pltpu.VMEM((N_EXPERTS, 1), F32)],
        compiler_params=_cparams(1, VMEM_LIMIT),
        name="out_route",
    )(oa, ob, x2, mod, w_out, g_ffn.reshape(1, D_MODEL), w_router.T,
      b_router.reshape(N_EXPERTS, 1), counts_in, x1_buf)


def _experts_kernel(tiles_ref, rows_ref, base_ref, next_ref, xs_ref, wgu_ref, bgu_ref, wd_ref,
                    bd_ref, y_ref, x_buf, y_buf, x_sem, y_sem):
    e = pl.program_id(0)
    n_tiles = tiles_ref[e]
    n_rows = rows_ref[e]
    n_live = base_ref[N_EXPERTS]
    region = e * EXPERT_ROWS

    def x_copy(row, slot):
        return pltpu.make_async_copy(xs_ref.at[pl.ds(row, ROW_TILE), :], x_buf.at[slot],
                                     x_sem.at[slot])

    def y_copy(row, slot):
        return pltpu.make_async_copy(y_buf.at[slot], y_ref.at[pl.ds(row, ROW_TILE), :],
                                     y_sem.at[slot])

    @pl.when(e == 0)
    def _():
        y_buf[...] = jnp.zeros(y_buf.shape, jnp.int32)

        @pl.when(n_live > 0)
        def _():
            x_copy(pl.multiple_of(next_ref[N_EXPERTS], ROW_TILE), 0).start()

    def tile(j, carry):
        g = base_ref[e] + j
        slot = g % 2
        row = pl.multiple_of(region + j * ROW_TILE, ROW_TILE)
        x_copy(row, slot).wait()
        next_row = jnp.where(j + 1 < n_tiles, row + ROW_TILE, next_ref[e])

        @pl.when(g + 1 < n_live)
        def _():
            x_copy(pl.multiple_of(next_row, ROW_TILE), 1 - slot).start()

        @pl.when(g >= 2)
        def _():
            y_copy(0, slot).wait()

        needed = jnp.minimum(n_rows - j * ROW_TILE, ROW_TILE)
        for rows in range(ROW_STEP, ROW_TILE + 1, ROW_STEP):
            @pl.when(jnp.logical_and(needed > rows - ROW_STEP, needed <= rows))
            def _(rows=rows):
                used = lax.broadcasted_iota(jnp.int32, (rows, 1), 0) < needed
                first, second = _unpack_bf16_pairs(jnp.where(used, x_buf[slot, :rows, :], 0))
                x = jnp.concatenate([first, second], axis=1)
                acc = jnp.zeros((rows, D_MODEL), F32)
                for c in range(D_EXPERT // FF_CHUNK):
                    lo_c, hi_c = c * FF_CHUNK, (c + 1) * FF_CHUNK
                    gate = (jnp.dot(x, wgu_ref[:, lo_c:hi_c], preferred_element_type=F32)
                            + bgu_ref[:, lo_c:hi_c])
                    up = (jnp.dot(x, wgu_ref[:, D_EXPERT + lo_c:D_EXPERT + hi_c],
                                  preferred_element_type=F32)
                          + bgu_ref[:, D_EXPERT + lo_c:D_EXPERT + hi_c])
                    gate = jnp.minimum(gate, SWIGLU_LIMIT)
                    up = jnp.clip(up, -SWIGLU_LIMIT, SWIGLU_LIMIT)
                    act = (up + 1.0) * (gate * jax.nn.sigmoid(SWIGLU_ALPHA * gate))
                    acc = acc + jnp.dot(act, wd_ref[lo_c:hi_c, :],
                                        preferred_element_type=F32)
                y_buf[slot, :rows, :] = _pack_bf16_pairs(acc + bd_ref[...])

        y_copy(row, slot).start()
        return carry

    lax.fori_loop(0, n_tiles, tile, 0)

    @pl.when(e == pl.num_programs(0) - 1)
    def _():
        for back in (2, 1):
            @pl.when(n_live >= back)
            def _():
                y_copy(0, (n_live - back) % 2).wait()


def _experts(counts, xs, w_gate_up, b_gate_up, w_down, b_down):
    n_rows, width = xs.shape
    tiles = (counts + ROW_TILE - 1) // ROW_TILE
    base = jnp.concatenate([jnp.zeros((1,), jnp.int32), jnp.cumsum(tiles)]).astype(jnp.int32)
    region = jnp.arange(N_EXPERTS, dtype=jnp.int32) * EXPERT_ROWS
    later = jnp.arange(N_EXPERTS)[None, :] > jnp.arange(-1, N_EXPERTS)[:, None]
    cand = jnp.where(jnp.logical_and(later, (tiles > 0)[None, :]), region[None, :], n_rows)
    nxt = jnp.min(cand, axis=1)
    next_row = jnp.concatenate([nxt[1:], nxt[:1]]).astype(jnp.int32)
    by_expert = lambda e, *_: (e, 0, 0)
    return pl.pallas_call(
        _experts_kernel,
        grid_spec=pltpu.PrefetchScalarGridSpec(
            num_scalar_prefetch=4,
            grid=(N_EXPERTS,),
            in_specs=[
                pl.BlockSpec(memory_space=pl.ANY),
                pl.BlockSpec((None, D_MODEL, 2 * D_EXPERT), by_expert),
                pl.BlockSpec((None, 1, 2 * D_EXPERT), by_expert),
                pl.BlockSpec((None, D_EXPERT, D_MODEL), by_expert),
                pl.BlockSpec((None, 1, D_MODEL), by_expert),
            ],
            out_specs=pl.BlockSpec(memory_space=pl.ANY),
            scratch_shapes=[
                pltpu.VMEM((2, ROW_TILE, width), jnp.int32),
                pltpu.VMEM((2, ROW_TILE, D_MODEL // 2), jnp.int32),
                pltpu.SemaphoreType.DMA((2,)),
                pltpu.SemaphoreType.DMA((2,)),
            ],
        ),
        out_shape=jax.ShapeDtypeStruct((n_rows, D_MODEL // 2), jnp.int32),
        compiler_params=_cparams(1, VMEM_LIMIT),
        name="experts",
    )(tiles.astype(jnp.int32), counts.astype(jnp.int32), base, next_row, xs, w_gate_up,
      b_gate_up.reshape(N_EXPERTS, 1, 2 * D_EXPERT), w_down, b_down.reshape(N_EXPERTS, 1, D_MODEL))


def _sc_gather_rows(table, idx):
    m = idx.shape[0]
    width = table.shape[1]
    mesh = plsc.VectorSubcoreMesh(core_axis_name="core", subcore_axis_name="subcore")
    n_workers = mesh.num_cores * mesh.num_subcores
    per_worker = m // n_workers
    n_windows = per_worker // SC_HALF_WINDOW
    assert per_worker * n_workers == m and n_windows * SC_HALF_WINDOW == per_worker

    @pl.kernel(
        out_type=jax.ShapeDtypeStruct((m, width), table.dtype),
        mesh=mesh,
        scratch_types=[
            pltpu.VMEM((SC_HALF_WINDOW,), jnp.int32),
            pltpu.VMEM((SC_HALF_WINDOW,), jnp.int32),
            pltpu.VMEM((SC_HALF_WINDOW, width), table.dtype),
            pltpu.VMEM((SC_HALF_WINDOW, width), table.dtype),
            pltpu.SemaphoreType.DMA,
            pltpu.SemaphoreType.DMA,
            pltpu.SemaphoreType.DMA,
            pltpu.SemaphoreType.DMA,
        ],
    )
    def gather_kernel(table_hbm, idx_hbm, out_hbm, idx0, idx1, rows0, rows1, g0, g1, w0, w1):
        worker = lax.axis_index("subcore") * mesh.num_cores + lax.axis_index("core")
        idx_v, rows_v, g_sem, w_sem = (idx0, idx1), (rows0, rows1), (g0, g1), (w0, w1)

        def window(j):
            return pl.ds(pl.multiple_of(worker * per_worker + j * SC_HALF_WINDOW, SC_HALF_WINDOW),
                         SC_HALF_WINDOW)

        def fetch(j):
            s = j % 2
            pltpu.sync_copy(idx_hbm.at[window(j)], idx_v[s])
            return pltpu.async_copy(table_hbm.at[idx_v[s]], rows_v[s], g_sem[s])

        fetches = {0: fetch(0)}
        writes = {}
        for j in range(n_windows):
            s = j % 2
            if j + 1 < n_windows:
                if j >= 1:
                    writes[j - 1].wait()
                fetches[j + 1] = fetch(j + 1)
            fetches[j].wait()
            writes[j] = pltpu.async_copy(rows_v[s], out_hbm.at[window(j)], w_sem[s])
        for j in range(max(n_windows - 2, 0), n_windows):
            writes[j].wait()

    return gather_kernel(table, idx)


def _sc_scatter_rows_into(table, dst, out_ref, reps):
    m, width = table.shape
    mesh = plsc.VectorSubcoreMesh(core_axis_name="core", subcore_axis_name="subcore")
    n_workers = mesh.num_cores * mesh.num_subcores
    per_worker = m // n_workers
    assert per_worker * n_workers == m and per_worker % SC_WINDOW == 0 and dst.shape == (reps * m,)

    @pl.kernel(
        out_type=(),
        mesh=mesh,
        scratch_types=[
            pltpu.VMEM((SC_WINDOW,), jnp.int32),
            pltpu.VMEM((SC_WINDOW, width), table.dtype),
            pltpu.SemaphoreType.DMA,
        ],
    )
    def scatter_kernel(table_hbm, dst_hbm, out_hbm, dst_v, rows_v, sem):
        worker = lax.axis_index("subcore") * mesh.num_cores + lax.axis_index("core")

        @pl.loop(0, per_worker // SC_WINDOW)
        def _(j):
            base = pl.multiple_of(worker * per_worker + j * SC_WINDOW, SC_WINDOW)
            pltpu.sync_copy(table_hbm.at[pl.ds(base, SC_WINDOW)], rows_v)
            for k in range(reps):
                pltpu.sync_copy(dst_hbm.at[pl.ds(k * m + base, SC_WINDOW)], dst_v)
                pltpu.async_copy(rows_v, out_hbm.at[dst_v], sem).wait()

    scatter_kernel(table, dst, out_ref)


def _combine_dense_kernel(tw_ref, x1_ref, mod_ref, g_ref, y0_ref, y1_ref, y2_ref, y3_ref, o_ref):
    tm = x1_ref.shape[0]
    tw = jnp.concatenate([tw_ref[...], jnp.zeros((LANES - SUBLANES, tm), F32)], axis=0).T
    first = second = None
    for kk, y_ref in enumerate((y0_ref, y1_ref, y2_ref, y3_ref)):
        f_k, s_k = _unpack_bf16_pairs(y_ref[...])
        w_k = tw[:, kk:kk + 1]
        first = w_k * f_k if kk == 0 else first + w_k * f_k
        second = w_k * s_k if kk == 0 else second + w_k * s_k
    ffn = jnp.concatenate([first, second], axis=1)
    x2 = x1_ref[...].astype(F32) + mod_ref[5:6, :] * ffn
    o_ref[...] = x2 * lax.rsqrt(jnp.mean(x2 * x2, axis=-1, keepdims=True) + EPS) * g_ref[...]


def _combine_dense(b, tw, x1, mod, g_final, yg, out_buf, seq):
    n = x1.shape[0]
    width = yg.shape[1]
    tiles = seq // TOK_TILE
    row = lambda i: (b * tiles + i, 0)
    deep = pl.Buffered(COMBINE_BUFFERS)
    slot = lambda kk: pl.BlockSpec((TOK_TILE, width), lambda i: (kk * tiles + i, 0),
                                   pipeline_mode=deep)

    def stream(tw_ref, x1_ref, mod_ref, g_ref, yg_ref, o_in_ref, o_ref):
        del o_in_ref
        pltpu.emit_pipeline(
            _combine_dense_kernel,
            grid=(tiles,),
            in_specs=[
                pl.BlockSpec((SUBLANES, TOK_TILE), lambda i: (0, i)),
                pl.BlockSpec((TOK_TILE, D_MODEL), row, pipeline_mode=deep),
                pl.BlockSpec((SUBLANES, D_MODEL), lambda i: (0, 0)),
                pl.BlockSpec((1, D_MODEL), lambda i: (0, 0)),
            ] + [slot(kk) for kk in range(TOP_K)],
            out_specs=[pl.BlockSpec((TOK_TILE, D_MODEL), row)],
        )(tw_ref, x1_ref, mod_ref, g_ref, yg_ref, yg_ref, yg_ref, yg_ref, o_ref)

    anywhere = pl.BlockSpec(memory_space=pl.ANY)
    return pl.pallas_call(
        stream,
        in_specs=[anywhere] * 6,
        out_specs=anywhere,
        out_shape=jax.ShapeDtypeStruct((n, D_MODEL), F32),
        input_output_aliases={5: 0},
        compiler_params=pltpu.CompilerParams(vmem_limit_bytes=VMEM_LIMIT),
        name="combine",
    )(tw, x1, mod[b], g_final.reshape(1, D_MODEL), yg, out_buf)


def _rope_base(positions):
    half = ROPE_DIM // 2
    inv_freq = ROPE_THETA ** (-jnp.arange(0, ROPE_DIM, 2, dtype=F32) / ROPE_DIM)
    n = positions.size
    pos = jnp.broadcast_to(positions.reshape(n, 1).astype(F32), (n, half)).reshape(-1, LANES)
    ang = pos * jnp.tile(inv_freq, LANES // half)
    cos, sin = lax.optimization_barrier((jnp.cos(ang), jnp.sin(ang)))
    pad = jnp.zeros((n, LANES - ROPE_DIM), F32)
    return jnp.concatenate([cos.reshape(n, half), sin.reshape(n, half), pad], axis=1)


def kernel(x, c, positions, w_ada, b_ada, g_mix_norm, w_in, rel_bias, lambda_q1, lambda_k1,
           lambda_q2, lambda_k2, g_subln, w_out, g_ffn_norm, w_router, b_router, w_gate_up,
           b_gate_up, w_down, b_down, g_final):
    batch, seq, _ = x.shape
    depth = w_ada.shape[0]
    assert depth == 1, "the combine kernel applies the final norm, so it must follow the only layer"
    n = batch * seq
    assert n <= EXPERT_ROWS and EXPERT_ROWS % ROW_TILE == 0
    rope_base = _rope_base(positions)
    x2 = x.reshape(n, D_MODEL)
    for l in range(depth):
        lambda_init = 0.8 - 0.6 * math.exp(-0.3 * l)
        mod = _ada(c, w_ada[l], b_ada[l])
        qa, ka, va, qb, kb, vb = _in_proj(x2, mod, g_mix_norm[l], w_in[l],
                                          rope_base, seq)
        oa = _attn_a(qa, ka, va, _rel_bias_rows(rel_bias[l]), batch, seq)
        lam = (jnp.exp(jnp.sum(lambda_q1[l].astype(F32) * lambda_k1[l].astype(F32)))
               - jnp.exp(jnp.sum(lambda_q2[l].astype(F32) * lambda_k2[l].astype(F32)))
               + lambda_init).reshape(1)
        xs_ref = jax.new_ref(lax.empty((N_EXPERTS * EXPERT_ROWS, D_MODEL // 2), jnp.int32))
        x1 = lax.empty((n, D_MODEL), BF16)
        counts = jnp.zeros((N_EXPERTS, 1), F32)
        pos_rows, weight_rows = [], []
        for b in range(batch):
            ob = _attn_b(b, lam, qb, kb, vb, g_subln[l], seq, 1.0 - lambda_init)
            x1, hp, top_w, pos, counts = _out_route(
                b, oa, ob, x2, mod, w_out[l], g_ffn_norm[l], w_router[l], b_router[l], counts, x1,
                seq)
            pos_rows.append(pos)
            weight_rows.append(top_w)
            dst = jnp.concatenate([pos[kk] for kk in range(TOP_K)])
            _sc_scatter_rows_into(hp, dst, xs_ref, TOP_K)
        xs = jax.freeze(xs_ref)
        y = _experts(counts[:, 0].astype(jnp.int32), xs, w_gate_up[l], b_gate_up[l], w_down[l],
                     b_down[l])
        x2 = lax.empty((n, D_MODEL), F32)
        for b in range(batch):
            yg = _sc_gather_rows(y, jnp.concatenate([pos_rows[b][kk] for kk in range(TOP_K)]))
            x2 = _combine_dense(b, weight_rows[b], x1, mod, g_final, yg, x2, seq)
    return x2.reshape(batch, seq, D_MODEL)
```
